```python
import math
import jax, jax.numpy as jnp
from jax import lax
import numpy as np

D_MODEL = 1024
BATCH = 8
SEQ = 4096
DEPTH = 2

ATT_HEADS = 8
ATT_KV_HEADS = 2
ATT_HEAD_DIM = 64
ATT_WIDTH = ATT_HEADS * ATT_HEAD_DIM
ATT_KV_WIDTH = ATT_KV_HEADS * ATT_HEAD_DIM
WINDOW = 128
CONV_WIDTH = 512
CONV_K = 31
DN_HEADS = 4
DN_HEAD_DIM = 128
DN_WIDTH = DN_HEADS * DN_HEAD_DIM
DN_CONV_K = 4
DN_CHUNK = 64
N_BRANCH = 3
EPS = 1e-6
NEG_INF = -1e30
IN_SIZES = (ATT_WIDTH, ATT_KV_WIDTH, ATT_KV_WIDTH, ATT_WIDTH,
            2 * CONV_WIDTH, CONV_WIDTH,
            DN_WIDTH, DN_WIDTH, DN_WIDTH, DN_HEADS, DN_HEADS, DN_WIDTH,
            N_BRANCH * D_MODEL)
D_IN = sum(IN_SIZES)

kernel_name = "hybrid_swa_conformer_gdn_gated_merge"


def rms_norm(x, g):
    xf = x.astype(jnp.float32)
    y = xf * lax.rsqrt(jnp.mean(xf * xf, axis=-1, keepdims=True) + EPS)
    return (y * g.astype(jnp.float32)).astype(x.dtype)


def layer_norm(x, g, b):
    xf = x.astype(jnp.float32)
    mu = jnp.mean(xf, axis=-1, keepdims=True)
    xc = xf - mu
    var = jnp.mean(xc * xc, axis=-1, keepdims=True)
    y = xc * lax.rsqrt(var + EPS) * g.astype(jnp.float32) + b.astype(jnp.float32)
    return y.astype(x.dtype)


def l2_norm(x):
    return x * lax.rsqrt(jnp.sum(x * x, axis=-1, keepdims=True) + EPS)


def causal_dwconv(x, w):
    k_width, ch = w.shape
    return lax.conv_general_dilated(
        x, w[:, None, :].astype(x.dtype), window_strides=(1,), padding=[(k_width - 1, 0)],
        dimension_numbers=("NWC", "WIO", "NWC"), feature_group_count=ch)


def alibi_slopes(n_heads):
    return jnp.exp2(-8.0 * jnp.arange(1, n_heads + 1, dtype=jnp.float32) / n_heads)


def sliding_window_attention(q, k, v, sinks):
    bsz, seq, n_h, d = q.shape
    n_kv = k.shape[2]
    grp = n_h // n_kv
    nb = seq // WINDOW
    qb = q.reshape(bsz, nb, WINDOW, n_kv, grp, d)

    def with_prev(t):
        tb = t.reshape(bsz, nb, WINDOW, n_kv, d)
        prev = jnp.concatenate([jnp.zeros_like(tb[:, :1]), tb[:, :-1]], axis=1)
        return jnp.concatenate([prev, tb], axis=2)

    kc, vc = with_prev(k), with_prev(v)
    s = jnp.einsum("bnqhgd,bnkhd->bhgnqk", qb, kc,
                   preferred_element_type=jnp.float32)
    qi = jnp.arange(WINDOW)[:, None]
    kj = jnp.arange(2 * WINDOW)[None, :]
    dist = qi + WINDOW - kj
    blk = jnp.arange(nb)[:, None, None]
    valid = (dist >= 0) & (dist < WINDOW) & ((blk > 0) | (kj >= WINDOW))
    slopes = alibi_slopes(n_h).reshape(n_kv, grp)[:, :, None, None, None]
    s = s - slopes * dist.astype(jnp.float32)
    s = jnp.where(valid, s, NEG_INF)
    sink = sinks.astype(jnp.float32).reshape(n_kv, grp)[:, :, None, None, None]
    m = jnp.maximum(jnp.max(s, axis=-1, keepdims=True), sink)
    p = jnp.exp(s - m)
    denom = jnp.sum(p, axis=-1, keepdims=True) + jnp.exp(sink - m)
    p = (p / denom).astype(v.dtype)
    o = jnp.einsum("bhgnqk,bnkhd->bnqhgd", p, vc)
    return o.reshape(bsz, seq, n_h * d)


def gated_delta_rule(q, k, v, g, beta):
    bsz, seq, n_h, dk = q.shape
    dv = v.shape[-1]
    cs = DN_CHUNK
    nc = seq // cs

    def chunks(t):
        return jnp.moveaxis(t.reshape(bsz, nc, cs, n_h, *t.shape[3:]), 3, 1)

    q, k, v, g, beta = chunks(q), chunks(k), chunks(v), chunks(g), chunks(beta)
    gc = jnp.cumsum(g, axis=-1)
    kb = k * beta[..., None]
    vb = v * beta[..., None]
    idx = jnp.arange(cs)
    lower = idx[:, None] >= idx[None, :]
    strict = idx[:, None] > idx[None, :]
    diff = gc[..., :, None] - gc[..., None, :]
    decay = jnp.where(lower, jnp.exp(jnp.where(lower, diff, 0.0)), 0.0)
    a = jnp.where(strict, jnp.einsum("bhncd,bhnsd->bhncs", kb, k) * decay, 0.0)
    eye = jnp.eye(cs, dtype=a.dtype)
    tmat = lax.linalg.triangular_solve(eye + a, jnp.broadcast_to(eye, a.shape),
                                       left_side=True, lower=True)
    u = tmat @ vb
    w = tmat @ (kb * jnp.exp(gc)[..., None])
    intra = jnp.where(lower, jnp.einsum("bhncd,bhnsd->bhncs", q, k) * decay, 0.0)
    qe = q * jnp.exp(gc)[..., None]
    g_last = gc[..., -1]
    ke = k * jnp.exp(g_last[..., None] - gc)[..., None]

    def step(state, inp):
        u_c, w_c, qe_c, ke_c, intra_c, gl = inp
        v_new = u_c - w_c @ state
        o_c = qe_c @ state + intra_c @ v_new
        state = state * jnp.exp(gl)[..., None, None] + jnp.swapaxes(ke_c, -1, -2) @ v_new
        return state, o_c

    xs = (jnp.moveaxis(u, 2, 0), jnp.moveaxis(w, 2, 0), jnp.moveaxis(qe, 2, 0),
          jnp.moveaxis(ke, 2, 0), jnp.moveaxis(intra, 2, 0), jnp.moveaxis(g_last, 2, 0))
    s0 = jnp.zeros((bsz, n_h, dk, dv), jnp.float32)
    _, o = lax.scan(step, s0, xs)
    return jnp.transpose(o, (1, 0, 3, 2, 4)).reshape(bsz, seq, n_h, dv)


def hybrid_layer(x, c, w_ada, b_ada, norm_g, w_in, q_norm_g, k_norm_g, sinks,
                 dw_w, dw_b, ln_g, ln_b, pw2_w, pw2_b, sconv_w, a_log, dt_bias, dn_norm_g,
                 w_proj_a, w_proj_b, w_proj_c, w_out):
    bsz, seq, _ = x.shape
    mod = jax.nn.silu(c) @ w_ada + b_ada
    shift, scale, gate = jnp.split(mod, 3, axis=-1)
    h = rms_norm(x, norm_g) * (1.0 + scale[:, None, :]) + shift[:, None, :]

    proj = h @ w_in
    split_points = np.cumsum(IN_SIZES)[:-1].tolist()
    (qa, ka, va, za, glu_in, zb, qc, kc, vc, ac, bc, zc, mg) = jnp.split(proj, split_points, axis=-1)

    qa = rms_norm(qa.reshape(bsz, seq, ATT_HEADS, ATT_HEAD_DIM), q_norm_g) * (ATT_HEAD_DIM ** -0.5)
    ka = rms_norm(ka.reshape(bsz, seq, ATT_KV_HEADS, ATT_HEAD_DIM), k_norm_g)
    va = va.reshape(bsz, seq, ATT_KV_HEADS, ATT_HEAD_DIM)
    ya = sliding_window_attention(qa, ka, va, sinks) * jax.nn.silu(za)

    val, gt = jnp.split(glu_in, 2, axis=-1)
    ub = val * jax.nn.sigmoid(gt)
    ub = causal_dwconv(ub, dw_w) + dw_b
    ub = jax.nn.silu(layer_norm(ub, ln_g, ln_b))
    yb = (ub @ pw2_w + pw2_b) * jax.nn.silu(zb)

    qkv = jax.nn.silu(causal_dwconv(jnp.concatenate([qc, kc, vc], axis=-1), sconv_w))
    qc, kc, vc = jnp.split(qkv, 3, axis=-1)
    qd = l2_norm(qc.reshape(bsz, seq, DN_HEADS, DN_HEAD_DIM).astype(jnp.float32)) * (DN_HEAD_DIM ** -0.5)
    kd = l2_norm(kc.reshape(bsz, seq, DN_HEADS, DN_HEAD_DIM).astype(jnp.float32))
    vd = vc.reshape(bsz, seq, DN_HEADS, DN_HEAD_DIM).astype(jnp.float32)
    beta = jax.nn.sigmoid(bc.astype(jnp.float32))
    g = -jnp.exp(a_log.astype(jnp.float32)) * jax.nn.softplus(
        ac.astype(jnp.float32) + dt_bias.astype(jnp.float32))
    od = gated_delta_rule(qd, kd, vd, g, beta)
    od = rms_norm(od, dn_norm_g).astype(x.dtype).reshape(bsz, seq, DN_WIDTH)
    yc = od * jax.nn.silu(zc)

    ga, gb, gcb = jnp.split(jax.nn.sigmoid(mg), N_BRANCH, axis=-1)
    merged = ga * (ya @ w_proj_a) + gb * (yb @ w_proj_b) + gcb * (yc @ w_proj_c)
    return x + gate[:, None, :] * (merged @ w_out)


def _fwd_setup_inputs(seed: int = 0) -> dict:
    key = jax.random.key(seed)
    ks = jax.random.split(key, 24)
    L, D = DEPTH, D_MODEL
    f32 = jnp.float32

    def nrm(k, shape, scale):
        return jax.random.normal(k, shape, f32) * scale

    dt = jnp.exp(jax.random.uniform(ks[17], (L, DN_HEADS), f32,
                                    minval=math.log(1e-3), maxval=math.log(1e-1)))
    return {
        "x": nrm(ks[0], (BATCH, SEQ, D), 1.0),
        "c": nrm(ks[1], (BATCH, D), 1.0),
        "w_ada": nrm(ks[2], (L, D, 3 * D), 0.5 * D ** -0.5),
        "b_ada": nrm(ks[3], (L, 3 * D), 0.02),
        "norm_g": 1.0 + nrm(ks[4], (L, D), 0.02),
        "w_in": nrm(ks[5], (L, D, D_IN), D ** -0.5),
        "q_norm_g": 1.0 + nrm(ks[6], (L, ATT_HEAD_DIM), 0.02),
        "k_norm_g": 1.0 + nrm(ks[7], (L, ATT_HEAD_DIM), 0.02),
        "sinks": nrm(ks[8], (L, ATT_HEADS), 0.5),
        "dw_w": nrm(ks[9], (L, CONV_K, CONV_WIDTH), CONV_K ** -0.5),
        "dw_b": nrm(ks[10], (L, CONV_WIDTH), 0.02),
        "ln_g": 1.0 + nrm(ks[11], (L, CONV_WIDTH), 0.02),
        "ln_b": nrm(ks[12], (L, CONV_WIDTH), 0.02),
        "pw2_w": nrm(ks[13], (L, CONV_WIDTH, CONV_WIDTH), CONV_WIDTH ** -0.5),
        "pw2_b": nrm(ks[14], (L, CONV_WIDTH), 0.02),
        "sconv_w": nrm(ks[15], (L, DN_CONV_K, 3 * DN_WIDTH), DN_CONV_K ** -0.5),
        "a_log": jnp.log(jax.random.uniform(ks[16], (L, DN_HEADS), f32, minval=1.0, maxval=16.0)),
        "dt_bias": dt + jnp.log(-jnp.expm1(-dt)),
        "dn_norm_g": 1.0 + nrm(ks[18], (L, DN_HEAD_DIM), 0.02),
        "w_proj_a": nrm(ks[19], (L, ATT_WIDTH, D), ATT_WIDTH ** -0.5),
        "w_proj_b": nrm(ks[20], (L, CONV_WIDTH, D), CONV_WIDTH ** -0.5),
        "w_proj_c": nrm(ks[21], (L, DN_WIDTH, D), DN_WIDTH ** -0.5),
        "w_out": nrm(ks[22], (L, D, D), D ** -0.5),
    }


def _fwd_reference(x, c, w_ada, b_ada, norm_g, w_in, q_norm_g, k_norm_g, sinks,
              dw_w, dw_b, ln_g, ln_b, pw2_w, pw2_b, sconv_w, a_log, dt_bias, dn_norm_g,
              w_proj_a, w_proj_b, w_proj_c, w_out):
    for l in range(DEPTH):
        x = hybrid_layer(x, c, w_ada[l], b_ada[l], norm_g[l], w_in[l], q_norm_g[l], k_norm_g[l],
                         sinks[l], dw_w[l], dw_b[l], ln_g[l], ln_b[l], pw2_w[l], pw2_b[l],
                         sconv_w[l], a_log[l], dt_bias[l], dn_norm_g[l],
                         w_proj_a[l], w_proj_b[l], w_proj_c[l], w_out[l])
    return x


import jax as _jax
import jax.numpy as _jnp

TWIN_FORMAT = 'train_step'
FWD_PARAMS = ['x', 'c', 'w_ada', 'b_ada', 'norm_g', 'w_in', 'q_norm_g', 'k_norm_g', 'sinks', 'dw_w', 'dw_b', 'ln_g', 'ln_b', 'pw2_w', 'pw2_b', 'sconv_w', 'a_log', 'dt_bias', 'dn_norm_g', 'w_proj_a', 'w_proj_b', 'w_proj_c', 'w_out']
TWIN_WEIGHTS = ['w_ada', 'b_ada', 'norm_g', 'w_in', 'q_norm_g', 'k_norm_g', 'sinks', 'dw_w', 'dw_b', 'ln_g', 'ln_b', 'pw2_w', 'pw2_b', 'sconv_w', 'a_log', 'dt_bias', 'dn_norm_g', 'w_proj_a', 'w_proj_b', 'w_proj_c', 'w_out']
TWIN_DIFF_INPUT = 'x'
TWIN_INPUTS = ['x', 'c', 'w_ada', 'b_ada', 'norm_g', 'w_in', 'q_norm_g', 'k_norm_g', 'sinks', 'dw_w', 'dw_b', 'ln_g', 'ln_b', 'pw2_w', 'pw2_b', 'sconv_w', 'a_log', 'dt_bias', 'dn_norm_g', 'w_proj_a', 'w_proj_b', 'w_proj_c', 'w_out', 'loss_target', 'm_w_ada', 'm_b_ada', 'm_norm_g', 'm_w_in', 'm_q_norm_g', 'm_k_norm_g', 'm_sinks', 'm_dw_w', 'm_dw_b', 'm_ln_g', 'm_ln_b', 'm_pw2_w', 'm_pw2_b', 'm_sconv_w', 'm_a_log', 'm_dt_bias', 'm_dn_norm_g', 'm_w_proj_a', 'm_w_proj_b', 'm_w_proj_c', 'm_w_out', 'v_w_ada', 'v_b_ada', 'v_norm_g', 'v_w_in', 'v_q_norm_g', 'v_k_norm_g', 'v_sinks', 'v_dw_w', 'v_dw_b', 'v_ln_g', 'v_ln_b', 'v_pw2_w', 'v_pw2_b', 'v_sconv_w', 'v_a_log', 'v_dt_bias', 'v_dn_norm_g', 'v_w_proj_a', 'v_w_proj_b', 'v_w_proj_c', 'v_w_out']
TWIN_OUTPUTS = ['loss', 'grad_x', 'grad_w_ada', 'grad_b_ada', 'grad_norm_g', 'grad_w_in', 'grad_q_norm_g', 'grad_k_norm_g', 'grad_sinks', 'grad_dw_w', 'grad_dw_b', 'grad_ln_g', 'grad_ln_b', 'grad_pw2_w', 'grad_pw2_b', 'grad_sconv_w', 'grad_a_log', 'grad_dt_bias', 'grad_dn_norm_g', 'grad_w_proj_a', 'grad_w_proj_b', 'grad_w_proj_c', 'grad_w_out', 'delta_w_ada', 'delta_b_ada', 'delta_norm_g', 'delta_w_in', 'delta_q_norm_g', 'delta_k_norm_g', 'delta_sinks', 'delta_dw_w', 'delta_dw_b', 'delta_ln_g', 'delta_ln_b', 'delta_pw2_w', 'delta_pw2_b', 'delta_sconv_w', 'delta_a_log', 'delta_dt_bias', 'delta_dn_norm_g', 'delta_w_proj_a', 'delta_w_proj_b', 'delta_w_proj_c', 'delta_w_out', 'new_m_w_ada', 'new_m_b_ada', 'new_m_norm_g', 'new_m_w_in', 'new_m_q_norm_g', 'new_m_k_norm_g', 'new_m_sinks', 'new_m_dw_w', 'new_m_dw_b', 'new_m_ln_g', 'new_m_ln_b', 'new_m_pw2_w', 'new_m_pw2_b', 'new_m_sconv_w', 'new_m_a_log', 'new_m_dt_bias', 'new_m_dn_norm_g', 'new_m_w_proj_a', 'new_m_w_proj_b', 'new_m_w_proj_c', 'new_m_w_out', 'new_v_w_ada', 'new_v_b_ada', 'new_v_norm_g', 'new_v_w_in', 'new_v_q_norm_g', 'new_v_k_norm_g', 'new_v_sinks', 'new_v_dw_w', 'new_v_dw_b', 'new_v_ln_g', 'new_v_ln_b', 'new_v_pw2_w', 'new_v_pw2_b', 'new_v_sconv_w', 'new_v_a_log', 'new_v_dt_bias', 'new_v_dn_norm_g', 'new_v_w_proj_a', 'new_v_w_proj_b', 'new_v_w_proj_c', 'new_v_w_out']
TWIN_LEAF_KINDS = {'loss': 'loss', 'grad_x': 'grad_x', 'grad_w_ada': 'grad_w', 'grad_b_ada': 'grad_w', 'grad_norm_g': 'grad_w', 'grad_w_in': 'grad_w', 'grad_q_norm_g': 'grad_w', 'grad_k_norm_g': 'grad_w', 'grad_sinks': 'grad_w', 'grad_dw_w': 'grad_w', 'grad_dw_b': 'grad_w', 'grad_ln_g': 'grad_w', 'grad_ln_b': 'grad_w', 'grad_pw2_w': 'grad_w', 'grad_pw2_b': 'grad_w', 'grad_sconv_w': 'grad_w', 'grad_a_log': 'grad_w', 'grad_dt_bias': 'grad_w', 'grad_dn_norm_g': 'grad_w', 'grad_w_proj_a': 'grad_w', 'grad_w_proj_b': 'grad_w', 'grad_w_proj_c': 'grad_w', 'grad_w_out': 'grad_w', 'delta_w_ada': 'delta_w', 'delta_b_ada': 'delta_w', 'delta_norm_g': 'delta_w', 'delta_w_in': 'delta_w', 'delta_q_norm_g': 'delta_w', 'delta_k_norm_g': 'delta_w', 'delta_sinks': 'delta_w', 'delta_dw_w': 'delta_w', 'delta_dw_b': 'delta_w', 'delta_ln_g': 'delta_w', 'delta_ln_b': 'delta_w', 'delta_pw2_w': 'delta_w', 'delta_pw2_b': 'delta_w', 'delta_sconv_w': 'delta_w', 'delta_a_log': 'delta_w', 'delta_dt_bias': 'delta_w', 'delta_dn_norm_g': 'delta_w', 'delta_w_proj_a': 'delta_w', 'delta_w_proj_b': 'delta_w', 'delta_w_proj_c': 'delta_w', 'delta_w_out': 'delta_w', 'new_m_w_ada': 'new_m', 'new_m_b_ada': 'new_m', 'new_m_norm_g': 'new_m', 'new_m_w_in': 'new_m', 'new_m_q_norm_g': 'new_m', 'new_m_k_norm_g': 'new_m', 'new_m_sinks': 'new_m', 'new_m_dw_w': 'new_m', 'new_m_dw_b': 'new_m', 'new_m_ln_g': 'new_m', 'new_m_ln_b': 'new_m', 'new_m_pw2_w': 'new_m', 'new_m_pw2_b': 'new_m', 'new_m_sconv_w': 'new_m', 'new_m_a_log': 'new_m', 'new_m_dt_bias': 'new_m', 'new_m_dn_norm_g': 'new_m', 'new_m_w_proj_a': 'new_m', 'new_m_w_proj_b': 'new_m', 'new_m_w_proj_c': 'new_m', 'new_m_w_out': 'new_m', 'new_v_w_ada': 'new_v', 'new_v_b_ada': 'new_v', 'new_v_norm_g': 'new_v', 'new_v_w_in': 'new_v', 'new_v_q_norm_g': 'new_v', 'new_v_k_norm_g': 'new_v', 'new_v_sinks': 'new_v', 'new_v_dw_w': 'new_v', 'new_v_dw_b': 'new_v', 'new_v_ln_g': 'new_v', 'new_v_ln_b': 'new_v', 'new_v_pw2_w': 'new_v', 'new_v_pw2_b': 'new_v', 'new_v_sconv_w': 'new_v', 'new_v_a_log': 'new_v', 'new_v_dt_bias': 'new_v', 'new_v_dn_norm_g': 'new_v', 'new_v_w_proj_a': 'new_v', 'new_v_w_proj_b': 'new_v', 'new_v_w_proj_c': 'new_v', 'new_v_w_out': 'new_v'}


def _forward(args):
    return _fwd_reference(*[args[k] for k in FWD_PARAMS])


def _output_shape():
    def fwd():
        inp = _fwd_setup_inputs(0)
        return _fwd_reference(*[inp[k] for k in FWD_PARAMS])
    out = _jax.eval_shape(fwd)
    return out.shape, out.dtype

N_MICROBATCH = 1
ADAM_LR = 0.001
ADAM_B1 = 0.9
ADAM_B2 = 0.999
ADAM_EPS = 1e-08
ADAM_WD = 0.01
ADAM_STEP = 10
PER_EXAMPLE_BATCH_AXIS = {'x': 0, 'c': 0, 'loss_target': 0}
SHARED_INPUTS = []
_WEIGHT_DTYPES = {'w_ada': _jnp.float32, 'b_ada': _jnp.float32, 'norm_g': _jnp.float32, 'w_in': _jnp.float32, 'q_norm_g': _jnp.float32, 'k_norm_g': _jnp.float32, 'sinks': _jnp.float32, 'dw_w': _jnp.float32, 'dw_b': _jnp.float32, 'ln_g': _jnp.float32, 'ln_b': _jnp.float32, 'pw2_w': _jnp.float32, 'pw2_b': _jnp.float32, 'sconv_w': _jnp.float32, 'a_log': _jnp.float32, 'dt_bias': _jnp.float32, 'dn_norm_g': _jnp.float32, 'w_proj_a': _jnp.float32, 'w_proj_b': _jnp.float32, 'w_proj_c': _jnp.float32, 'w_out': _jnp.float32}
MOMENT_SCALE = {'w_ada': 2.813049e-01, 'b_ada': 7.213295e-01, 'norm_g': 7.259175e-01, 'w_in': 4.012176e-02, 'q_norm_g': 2.435956e-01, 'k_norm_g': 2.438771e-01, 'sinks': 9.043211e-01, 'dw_w': 3.078827e-02, 'dw_b': 1.593267e-01, 'ln_g': 3.476302e-01, 'ln_b': 2.234044e-01, 'pw2_w': 4.083885e-02, 'pw2_b': 1.678278e-01, 'sconv_w': 6.795044e-02, 'a_log': 2.058318e+00, 'dt_bias': 1.963932e+00, 'dn_norm_g': 3.147867e+00, 'w_proj_a': 1.598808e-02, 'w_proj_b': 2.094029e-02, 'w_proj_c': 4.782177e-02, 'w_out': 5.033040e-02}


def _to_microbatches(a, axis):
    t = _jnp.moveaxis(a, axis, 0)
    t = t.reshape((N_MICROBATCH, t.shape[0] // N_MICROBATCH) + t.shape[1:])
    return _jnp.moveaxis(t, 1, axis + 1)


def setup_inputs(seed: int = 0) -> dict:
    inp = _fwd_setup_inputs(seed)
    key = _jax.random.fold_in(_jax.random.key(seed), 7919)
    shape, _ = _output_shape()
    out = dict(inp)
    out["loss_target"] = _jax.random.normal(_jax.random.fold_in(key, 0), shape, _jnp.float32)
    for i, name in enumerate(TWIN_WEIGHTS):
        w = inp[name].astype(_jnp.float32)
        if MOMENT_SCALE is None:
            s = _jnp.sqrt(_jnp.mean(_jnp.square(w)) + 1e-30)
        else:
            s = MOMENT_SCALE[name]
        km, kv = _jax.random.split(_jax.random.fold_in(key, i + 1))
        out[name] = w
        out["m_" + name] = s * _jax.random.normal(km, w.shape, _jnp.float32)
        out["v_" + name] = (s * s) * _jax.random.uniform(kv, w.shape, _jnp.float32, 0.5, 1.5)
    if N_MICROBATCH > 1:
        for name, axis in PER_EXAMPLE_BATCH_AXIS.items():
            out[name] = _to_microbatches(out[name], axis)
    return {'x': out['x'], 'c': out['c'], 'w_ada': out['w_ada'], 'b_ada': out['b_ada'], 'norm_g': out['norm_g'], 'w_in': out['w_in'], 'q_norm_g': out['q_norm_g'], 'k_norm_g': out['k_norm_g'], 'sinks': out['sinks'], 'dw_w': out['dw_w'], 'dw_b': out['dw_b'], 'ln_g': out['ln_g'], 'ln_b': out['ln_b'], 'pw2_w': out['pw2_w'], 'pw2_b': out['pw2_b'], 'sconv_w': out['sconv_w'], 'a_log': out['a_log'], 'dt_bias': out['dt_bias'], 'dn_norm_g': out['dn_norm_g'], 'w_proj_a': out['w_proj_a'], 'w_proj_b': out['w_proj_b'], 'w_proj_c': out['w_proj_c'], 'w_out': out['w_out'], 'loss_target': out['loss_target'], 'm_w_ada': out['m_w_ada'], 'm_b_ada': out['m_b_ada'], 'm_norm_g': out['m_norm_g'], 'm_w_in': out['m_w_in'], 'm_q_norm_g': out['m_q_norm_g'], 'm_k_norm_g': out['m_k_norm_g'], 'm_sinks': out['m_sinks'], 'm_dw_w': out['m_dw_w'], 'm_dw_b': out['m_dw_b'], 'm_ln_g': out['m_ln_g'], 'm_ln_b': out['m_ln_b'], 'm_pw2_w': out['m_pw2_w'], 'm_pw2_b': out['m_pw2_b'], 'm_sconv_w': out['m_sconv_w'], 'm_a_log': out['m_a_log'], 'm_dt_bias': out['m_dt_bias'], 'm_dn_norm_g': out['m_dn_norm_g'], 'm_w_proj_a': out['m_w_proj_a'], 'm_w_proj_b': out['m_w_proj_b'], 'm_w_proj_c': out['m_w_proj_c'], 'm_w_out': out['m_w_out'], 'v_w_ada': out['v_w_ada'], 'v_b_ada': out['v_b_ada'], 'v_norm_g': out['v_norm_g'], 'v_w_in': out['v_w_in'], 'v_q_norm_g': out['v_q_norm_g'], 'v_k_norm_g': out['v_k_norm_g'], 'v_sinks': out['v_sinks'], 'v_dw_w': out['v_dw_w'], 'v_dw_b': out['v_dw_b'], 'v_ln_g': out['v_ln_g'], 'v_ln_b': out['v_ln_b'], 'v_pw2_w': out['v_pw2_w'], 'v_pw2_b': out['v_pw2_b'], 'v_sconv_w': out['v_sconv_w'], 'v_a_log': out['v_a_log'], 'v_dt_bias': out['v_dt_bias'], 'v_dn_norm_g': out['v_dn_norm_g'], 'v_w_proj_a': out['v_w_proj_a'], 'v_w_proj_b': out['v_w_proj_b'], 'v_w_proj_c': out['v_w_proj_c'], 'v_w_out': out['v_w_out']}


def _loss(weights, diff, rest, loss_target):
    with _jax.named_scope("forward"):
        args = {**rest, TWIN_DIFF_INPUT: diff, **{k: w.astype(_WEIGHT_DTYPES[k]) for k, w in weights.items()}}
        y = _forward(args)
    with _jax.named_scope("loss_head"):
        err = _jnp.square(y.astype(_jnp.float32) - loss_target)
        return 0.5 * _jnp.sum(_jnp.mean(err, axis=-1)) if err.ndim else 0.5 * err


def _adamw(w, g, m, v):
    m = ADAM_B1 * m + (1.0 - ADAM_B1) * g
    v = ADAM_B2 * v + (1.0 - ADAM_B2) * _jnp.square(g)
    m_hat = m / (1.0 - ADAM_B1 ** ADAM_STEP)
    v_hat = v / (1.0 - ADAM_B2 ** ADAM_STEP)
    delta = -ADAM_LR * (m_hat / (_jnp.sqrt(v_hat) + ADAM_EPS) + ADAM_WD * w)
    return delta, m, v


def reference(x, c, w_ada, b_ada, norm_g, w_in, q_norm_g, k_norm_g, sinks, dw_w, dw_b, ln_g, ln_b, pw2_w, pw2_b, sconv_w, a_log, dt_bias, dn_norm_g, w_proj_a, w_proj_b, w_proj_c, w_out, loss_target, m_w_ada, m_b_ada, m_norm_g, m_w_in, m_q_norm_g, m_k_norm_g, m_sinks, m_dw_w, m_dw_b, m_ln_g, m_ln_b, m_pw2_w, m_pw2_b, m_sconv_w, m_a_log, m_dt_bias, m_dn_norm_g, m_w_proj_a, m_w_proj_b, m_w_proj_c, m_w_out, v_w_ada, v_b_ada, v_norm_g, v_w_in, v_q_norm_g, v_k_norm_g, v_sinks, v_dw_w, v_dw_b, v_ln_g, v_ln_b, v_pw2_w, v_pw2_b, v_sconv_w, v_a_log, v_dt_bias, v_dn_norm_g, v_w_proj_a, v_w_proj_b, v_w_proj_c, v_w_out):
    given = dict(x=x, c=c, w_ada=w_ada, b_ada=b_ada, norm_g=norm_g, w_in=w_in, q_norm_g=q_norm_g, k_norm_g=k_norm_g, sinks=sinks, dw_w=dw_w, dw_b=dw_b, ln_g=ln_g, ln_b=ln_b, pw2_w=pw2_w, pw2_b=pw2_b, sconv_w=sconv_w, a_log=a_log, dt_bias=dt_bias, dn_norm_g=dn_norm_g, w_proj_a=w_proj_a, w_proj_b=w_proj_b, w_proj_c=w_proj_c, w_out=w_out, loss_target=loss_target, m_w_ada=m_w_ada, m_b_ada=m_b_ada, m_norm_g=m_norm_g, m_w_in=m_w_in, m_q_norm_g=m_q_norm_g, m_k_norm_g=m_k_norm_g, m_sinks=m_sinks, m_dw_w=m_dw_w, m_dw_b=m_dw_b, m_ln_g=m_ln_g, m_ln_b=m_ln_b, m_pw2_w=m_pw2_w, m_pw2_b=m_pw2_b, m_sconv_w=m_sconv_w, m_a_log=m_a_log, m_dt_bias=m_dt_bias, m_dn_norm_g=m_dn_norm_g, m_w_proj_a=m_w_proj_a, m_w_proj_b=m_w_proj_b, m_w_proj_c=m_w_proj_c, m_w_out=m_w_out, v_w_ada=v_w_ada, v_b_ada=v_b_ada, v_norm_g=v_norm_g, v_w_in=v_w_in, v_q_norm_g=v_q_norm_g, v_k_norm_g=v_k_norm_g, v_sinks=v_sinks, v_dw_w=v_dw_w, v_dw_b=v_dw_b, v_ln_g=v_ln_g, v_ln_b=v_ln_b, v_pw2_w=v_pw2_w, v_pw2_b=v_pw2_b, v_sconv_w=v_sconv_w, v_a_log=v_a_log, v_dt_bias=v_dt_bias, v_dn_norm_g=v_dn_norm_g, v_w_proj_a=v_w_proj_a, v_w_proj_b=v_w_proj_b, v_w_proj_c=v_w_proj_c, v_w_out=v_w_out)
    weights = {n: given[n] for n in TWIN_WEIGHTS}
    shared = {n: given[n] for n in SHARED_INPUTS}
    per_example = {n: given[n] for n in ['x', 'c']}
    grad_fn = _jax.value_and_grad(_loss, argnums=(0, 1))

    def one_microbatch(ex, loss_target):
        ex = dict(ex)
        diff = ex.pop(TWIN_DIFF_INPUT)
        return grad_fn(weights, diff, {**shared, **ex}, loss_target)

    if N_MICROBATCH == 1:
        loss, (grad_w, grad_x) = one_microbatch(per_example, given["loss_target"])
    else:
        def body(carry, xs):
            loss_sum, grad_sum = carry
            l_k, (gw_k, gx_k) = one_microbatch(xs[0], xs[1])
            with _jax.named_scope("update"):
                return (loss_sum + l_k, _jax.tree.map(_jnp.add, grad_sum, gw_k)), gx_k

        init = (_jnp.zeros((), _jnp.float32), _jax.tree.map(_jnp.zeros_like, weights))
        (loss, grad_w), grad_x = _jax.lax.scan(body, init, (per_example, given["loss_target"]))
    with _jax.named_scope("update"):
        delta_w, new_m, new_v = {}, {}, {}
        for n in TWIN_WEIGHTS:
            delta_w[n], new_m[n], new_v[n] = _adamw(weights[n], grad_w[n], given["m_" + n], given["v_" + n])
    return (loss, grad_x, *[grad_w[n] for n in TWIN_WEIGHTS], *[delta_w[n] for n in TWIN_WEIGHTS],
            *[new_m[n] for n in TWIN_WEIGHTS], *[new_v[n] for n in TWIN_WEIGHTS])
```

```python
import functools
import math

import jax
import jax.numpy as jnp
import numpy as np
from jax import lax
from jax.experimental import pallas as pl
from jax.experimental.pallas import tpu as pltpu

F32 = jnp.float32
BF16 = jnp.bfloat16
HI = lax.Precision.HIGHEST

N_DEV = 8
D = 1024
DEPTH = 2
EPS = 1e-6
NEG_INF = -1e30
WINDOW = 128
ATT_HEADS = 8
ATT_HD = 64
CONV_K = 31
DN_HEADS = 4
DN_K = 4
CHUNK = 64
D_IN = 7944
VMEM_LIMIT = 56 * 1024 * 1024

C_MG, C_QA, C_ZA, C_ZB, C_QC, C_KC, C_VC, C_GV, C_GG, C_ZC, C_KA, C_VA, C_AB, NP = (
    0, 3072, 3584, 4096, 4608, 5120, 5632, 6144, 6656, 7168, 7680, 7808, 7936, 8064)
_PAD_FROM = ((4872, 7944), (0, 512), (768, 1280), (2304, 2816), (2816, 4352), (1280, 2304), (4360, 4872),
             (512, 768), (4352, 4360))
_UNPAD_FROM = ((3072, 3584), (7680, 7936), (3584, 4096), (6144, 7168), (4096, 4608), (4608, 6144), (7936, 7944),
               (7168, 7680), (0, 3072))

ALIBI = tuple(float(2.0 ** (-8.0 * (h + 1) / ATT_HEADS)) for h in range(ATT_HEADS))

ADAM_LR, ADAM_B1, ADAM_B2, ADAM_EPS, ADAM_WD, ADAM_STEP = 0.001, 0.9, 0.999, 1e-08, 0.01, 10


def _cparams(sem=None):
    return pltpu.CompilerParams(dimension_semantics=sem, vmem_limit_bytes=VMEM_LIMIT)


def _sig(x):
    return jax.nn.sigmoid(x)


def _silu(x):
    return x * _sig(x)


def _dsilu(x):
    s = _sig(x)
    return s * (1.0 + x * (1.0 - s))


def _dot(a, b, dims=((1,), (0,)), precision=None):
    return lax.dot_general(a, b, (dims, ((), ())), preferred_element_type=F32, precision=precision)


def _bdot(a, b, dims=((1,), (0,))):
    return _dot(a.astype(BF16), b.astype(BF16), dims)


NN, NT, TN = ((1,), (0,)), ((1,), (1,)), ((0,), (0,))


def _row(v):
    return v.reshape(1, -1)


def _mm(a, b, *, ta=False, tb=False, tm, tn, tk, name):
    M, K = (a.shape[1], a.shape[0]) if ta else a.shape
    N = b.shape[0] if tb else b.shape[1]
    assert M % tm == 0 and N % tn == 0 and K % tk == 0, (M, N, K, tm, tn, tk)
    nk = K // tk
    dims = ((0 if ta else 1,), (1 if tb else 0,))

    def body(a_ref, b_ref, o_ref, acc_ref):
        k = pl.program_id(2)

        @pl.when(k == 0)
        def _():
            acc_ref[...] = jnp.zeros_like(acc_ref)

        acc_ref[...] += _bdot(a_ref[...], b_ref[...], dims)

        @pl.when(k == nk - 1)
        def _():
            o_ref[...] = acc_ref[...]

    a_spec = pl.BlockSpec((tk, tm), lambda i, j, k: (k, i)) if ta else pl.BlockSpec((tm, tk), lambda i, j, k: (i, k))
    b_spec = pl.BlockSpec((tn, tk), lambda i, j, k: (j, k)) if tb else pl.BlockSpec((tk, tn), lambda i, j, k: (k, j))
    return pl.pallas_call(
        body, name=name, grid=(M // tm, N // tn, nk),
        in_specs=[a_spec, b_spec], out_specs=pl.BlockSpec((tm, tn), lambda i, j, k: (i, j)),
        out_shape=jax.ShapeDtypeStruct((M, N), F32),
        scratch_shapes=[pltpu.VMEM((tm, tn), F32)],
        compiler_params=_cparams(("parallel", "parallel", "arbitrary")),
    )(a, b)


def _norm_fwd(x, norm_g, scale, shift, name):
    T = x.shape[0]
    tm = min(512, T)

    def body(x_ref, g_ref, sc_ref, sh_ref, h_ref):
        xv = x_ref[...]
        r = lax.rsqrt(jnp.mean(xv * xv, axis=-1, keepdims=True) + EPS)
        h_ref[...] = ((xv * r) * g_ref[...] * (1.0 + sc_ref[...]) + sh_ref[...]).astype(BF16)

    vec = pl.BlockSpec((1, D), lambda i: (0, 0))
    return pl.pallas_call(
        body, name=name, grid=(T // tm,),
        in_specs=[pl.BlockSpec((tm, D), lambda i: (i, 0)), vec, vec, vec],
        out_specs=pl.BlockSpec((tm, D), lambda i: (i, 0)),
        out_shape=jax.ShapeDtypeStruct((T, D), BF16),
        compiler_params=_cparams(("parallel",)),
    )(x, _row(norm_g), _row(scale), _row(shift))


def _norm_bwd(dh, x, dres, norm_g, scale, name):
    T = x.shape[0]
    tm = min(512, T)

    def body(dh_ref, x_ref, dr_ref, g_ref, sc_ref, dx_ref, st_ref):
        i = pl.program_id(0)
        xv, dhv = x_ref[...], dh_ref[...]
        r = lax.rsqrt(jnp.mean(xv * xv, axis=-1, keepdims=True) + EPS)
        xh = xv * r
        g, s1 = g_ref[...], 1.0 + sc_ref[...]
        dxh = dhv * (g * s1)
        dx_ref[...] = dr_ref[...] + r * (dxh - xh * jnp.mean(dxh * xh, axis=-1, keepdims=True))
        dhx = dhv * xh
        upd = jnp.concatenate([jnp.sum(dhv, axis=0, keepdims=True), jnp.sum(dhx * g, axis=0, keepdims=True),
                               jnp.sum(dhx * s1, axis=0, keepdims=True), jnp.zeros((5, D), F32)], axis=0)

        @pl.when(i == 0)
        def _():
            st_ref[...] = upd

        @pl.when(i > 0)
        def _():
            st_ref[...] += upd

    vec = pl.BlockSpec((1, D), lambda i: (0, 0))
    blk = pl.BlockSpec((tm, D), lambda i: (i, 0))
    return pl.pallas_call(
        body, name=name, grid=(T // tm,),
        in_specs=[blk, blk, blk, vec, vec],
        out_specs=[blk, pl.BlockSpec((8, D), lambda i: (0, 0))],
        out_shape=[jax.ShapeDtypeStruct((T, D), F32), jax.ShapeDtypeStruct((8, D), F32)],
        compiler_params=_cparams(("arbitrary",)),
    )(dh, x, dres, _row(norm_g), _row(scale))


def _rms(x, g):
    r = lax.rsqrt(jnp.mean(x * x, axis=-1, keepdims=True) + EPS)
    return x * r, r


def _attn_mask(n):
    qi = lax.broadcasted_iota(jnp.int32, (WINDOW, 2 * WINDOW), 0)
    kj = lax.broadcasted_iota(jnp.int32, (WINDOW, 2 * WINDOW), 1)
    dist = qi + WINDOW - kj
    valid = (dist >= 0) & (dist < WINDOW) & ((n > 0) | (kj >= WINDOW))
    return valid, dist.astype(F32)


def _attn_probs(qn, kn, h, sink, valid, distf):
    s = _bdot(qn, kn, NT) - ALIBI[h] * distf
    s = jnp.where(valid, s, NEG_INF)
    m = jnp.maximum(jnp.max(s, axis=-1, keepdims=True), sink)
    p = jnp.exp(s - m)
    es = jnp.exp(sink - m)
    den = jnp.sum(p, axis=-1, keepdims=True) + es
    return p / den, es / den


def _attn_fwd(proj, q_norm_g, k_norm_g, sinks, name):
    T = proj.shape[0]
    nb = T // WINDOW

    def body(sink_ref, q_ref, z_ref, kc_ref, kp_ref, vc_ref, vp_ref, qg_ref, kg_ref, o_ref):
        n = pl.program_id(0)
        valid, distf = _attn_mask(n)
        k2 = jnp.concatenate([kp_ref[...], kc_ref[...]], axis=0)
        v2 = jnp.concatenate([vp_ref[...], vc_ref[...]], axis=0)
        for g in range(2):
            kn = _rms(k2[:, 64 * g:64 * g + 64], None)[0] * kg_ref[...]
            vg = v2[:, 64 * g:64 * g + 64]
            for j in range(4):
                h = 4 * g + j
                sl = slice(64 * h, 64 * h + 64)
                qn = (_rms(q_ref[:, sl], None)[0] * qg_ref[...]) * (ATT_HD ** -0.5)
                p, _ = _attn_probs(qn, kn, h, sink_ref[h], valid, distf)
                o_ref[:, sl] = _bdot(p, vg) * _silu(z_ref[:, sl])

    prev = lambda n: jnp.maximum(n - 1, 0)
    return pl.pallas_call(
        body, name=name, grid=(nb,),
        in_specs=[pl.BlockSpec(memory_space=pltpu.SMEM),
                  pl.BlockSpec((WINDOW, 512), lambda n: (n, C_QA // 512)),
                  pl.BlockSpec((WINDOW, 512), lambda n: (n, C_ZA // 512)),
                  pl.BlockSpec((WINDOW, 128), lambda n: (n, C_KA // 128)),
                  pl.BlockSpec((WINDOW, 128), lambda n: (prev(n), C_KA // 128)),
                  pl.BlockSpec((WINDOW, 128), lambda n: (n, C_VA // 128)),
                  pl.BlockSpec((WINDOW, 128), lambda n: (prev(n), C_VA // 128)),
                  pl.BlockSpec((1, 64), lambda n: (0, 0)), pl.BlockSpec((1, 64), lambda n: (0, 0))],
        out_specs=pl.BlockSpec((WINDOW, 512), lambda n: (n, 0)),
        out_shape=jax.ShapeDtypeStruct((T, 512), F32),
        compiler_params=_cparams(("parallel",)),
    )(sinks, proj, proj, proj, proj, proj, proj, _row(q_norm_g), _row(k_norm_g))


def _rms_bwd(dy, xh, r, g):
    dxh = dy * g
    return r * (dxh - xh * jnp.mean(dxh * xh, axis=-1, keepdims=True)), dy * xh


def _attn_bwd(dproj, proj, dya, q_norm_g, k_norm_g, sinks, name):
    T = proj.shape[0]
    nb = T // WINDOW

    def body(sink_ref, dp_any, q_ref, z_ref, kc_ref, kp_ref, vc_ref, vp_ref, dy_ref, qg_ref, kg_ref,
             dqz_ref, dkv_ref, gq_ref, gk_ref, gs_ref, ck_ref, cv_ref):
        n = pl.program_id(0)

        @pl.when(n == 0)
        def _():
            gq_ref[...] = jnp.zeros_like(gq_ref)
            gk_ref[...] = jnp.zeros_like(gk_ref)
            gs_ref[...] = jnp.zeros_like(gs_ref)
            ck_ref[...] = jnp.zeros_like(ck_ref)
            cv_ref[...] = jnp.zeros_like(cv_ref)

        lane8 = lax.broadcasted_iota(jnp.int32, (1, 8), 1)

        @pl.when(n < nb)
        def _():
            valid, distf = _attn_mask(n)
            k2 = jnp.concatenate([kp_ref[...], kc_ref[...]], axis=0)
            v2 = jnp.concatenate([vp_ref[...], vc_ref[...]], axis=0)
            gq_acc = jnp.zeros((1, 64), F32)
            gs_acc = jnp.zeros((1, 8), F32)
            for g in range(2):
                kn = _rms(k2[:, 64 * g:64 * g + 64], None)[0] * kg_ref[...]
                vg = v2[:, 64 * g:64 * g + 64]
                dkn = jnp.zeros((2 * WINDOW, 64), F32)
                dvg = jnp.zeros((2 * WINDOW, 64), F32)
                for j in range(4):
                    h = 4 * g + j
                    sl = slice(64 * h, 64 * h + 64)
                    qh, qr = _rms(q_ref[:, sl], None)
                    qn = (qh * qg_ref[...]) * (ATT_HD ** -0.5)
                    p, ps = _attn_probs(qn, kn, h, sink_ref[h], valid, distf)
                    o = _bdot(p, vg)
                    zh = z_ref[:, sl]
                    dy = dy_ref[:, sl]
                    dqz_ref[:, 512 + 64 * h:512 + 64 * h + 64] = dy * o * _dsilu(zh)
                    do = dy * _silu(zh)
                    delta = jnp.sum(do * o, axis=-1, keepdims=True)
                    dpm = _bdot(do, vg, NT)
                    ds = p * (dpm - delta)
                    gs_acc = gs_acc + jnp.where(lane8 == h, -jnp.sum(ps * delta, axis=0, keepdims=True), 0.0)
                    dvg = dvg + _bdot(p, do, TN)
                    dkn = dkn + _bdot(ds, qn, TN)
                    dqn = _bdot(ds, kn) * (ATT_HD ** -0.5)
                    dq, gq = _rms_bwd(dqn, qh, qr, qg_ref[...])
                    dqz_ref[:, sl] = dq
                    gq_acc = gq_acc + jnp.sum(gq, axis=0, keepdims=True)
                ksl = slice(64 * g, 64 * g + 64)
                vsl = slice(128 + 64 * g, 128 + 64 * g + 64)
                dkv_ref[:, ksl] = ck_ref[:, ksl] + dkn[:WINDOW]
                dkv_ref[:, vsl] = cv_ref[:, ksl] + dvg[:WINDOW]
                ck_ref[:, ksl] = dkn[WINDOW:]
                cv_ref[:, ksl] = dvg[WINDOW:]
            gq_ref[...] += gq_acc
            gs_ref[...] += gs_acc

        @pl.when(n == nb)
        def _():
            dkv_ref[:, :128] = ck_ref[...]
            dkv_ref[:, 128:] = cv_ref[...]

        @pl.when(n > 0)
        def _():
            gk_acc = jnp.zeros((1, 64), F32)
            for g in range(2):
                ksl = slice(64 * g, 64 * g + 64)
                kh, kr = _rms(kp_ref[:, ksl], None)
                dk, gk = _rms_bwd(dkv_ref[:, ksl], kh, kr, kg_ref[...])
                dkv_ref[:, ksl] = dk
                gk_acc = gk_acc + jnp.sum(gk, axis=0, keepdims=True)
            gk_ref[...] += gk_acc

    cur = lambda n: jnp.minimum(n, nb - 1)
    prev = lambda n: jnp.maximum(n - 1, 0)
    small = lambda w: pl.BlockSpec((1, w), lambda n: (0, 0))
    return pl.pallas_call(
        body, name=name, grid=(nb + 1,),
        in_specs=[pl.BlockSpec(memory_space=pltpu.SMEM), pl.BlockSpec(memory_space=pl.ANY),
                  pl.BlockSpec((WINDOW, 512), lambda n: (cur(n), C_QA // 512)),
                  pl.BlockSpec((WINDOW, 512), lambda n: (cur(n), C_ZA // 512)),
                  pl.BlockSpec((WINDOW, 128), lambda n: (cur(n), C_KA // 128)),
                  pl.BlockSpec((WINDOW, 128), lambda n: (prev(n), C_KA // 128)),
                  pl.BlockSpec((WINDOW, 128), lambda n: (cur(n), C_VA // 128)),
                  pl.BlockSpec((WINDOW, 128), lambda n: (prev(n), C_VA // 128)),
                  pl.BlockSpec((WINDOW, 512), lambda n: (cur(n), 0)),
                  small(64), small(64)],
        out_specs=[pl.BlockSpec((WINDOW, 1024), lambda n: (cur(n), C_QA // 1024)),
                   pl.BlockSpec((WINDOW, 256), lambda n: (prev(n), 0)),
                   small(64), small(64), small(8)],
        out_shape=[jax.ShapeDtypeStruct(dproj.shape, F32), jax.ShapeDtypeStruct((T, 256), F32),
                   jax.ShapeDtypeStruct((1, 64), F32), jax.ShapeDtypeStruct((1, 64), F32),
                   jax.ShapeDtypeStruct((1, 8), F32)],
        scratch_shapes=[pltpu.VMEM((WINDOW, 128), F32), pltpu.VMEM((WINDOW, 128), F32)],
        input_output_aliases={1: 0},
        compiler_params=_cparams(("arbitrary",)),
    )(sinks, dproj, proj, proj, proj, proj, proj, proj, dya, _row(q_norm_g), _row(k_norm_g))


HALO_B = 32


def _conf_specs(T, tm):
    r = tm // HALO_B
    cur = lambda c: pl.BlockSpec((tm, 512), lambda i: (i, c // 512))
    prev = lambda c: pl.BlockSpec((HALO_B, 512), lambda i: (jnp.maximum(i * r - 1, 0), c // 512))
    return cur, prev


def _conf_core(i, tm, gv_ref, gg_ref, gvp_ref, ggp_ref, w_ref, b_ref, lg_ref, lb_ref, pw_ref, pb_ref, ext_ref):
    up = gvp_ref[...] * _sig(ggp_ref[...])
    ext_ref[:HALO_B] = jnp.where(i > 0, up, 0.0)
    ext_ref[HALO_B:] = gv_ref[...] * _sig(gg_ref[...])
    acc = jnp.zeros((tm, 512), F32) + b_ref[...]
    for k in range(CONV_K):
        acc = acc + w_ref[k:k + 1, :] * ext_ref[pl.ds(HALO_B - CONV_K + 1 + k, tm), :]
    mu = jnp.mean(acc, axis=-1, keepdims=True)
    xc = acc - mu
    rstd = lax.rsqrt(jnp.mean(xc * xc, axis=-1, keepdims=True) + EPS)
    xh = xc * rstd
    u2 = xh * lg_ref[...] + lb_ref[...]
    u3 = _silu(u2)
    ypre = _bdot(u3, pw_ref[...]) + pb_ref[...]
    return xh, rstd, u2, u3, ypre


def _conf_fwd(proj, dw_w, dw_b, ln_g, ln_b, pw2, pw2_b, name):
    T = proj.shape[0]
    tm = min(512, T)
    cur, prev = _conf_specs(T, tm)

    def body(gv_ref, gg_ref, gvp_ref, ggp_ref, zb_ref, w_ref, b_ref, lg_ref, lb_ref, pw_ref, pb_ref, o_ref, ext_ref):
        i = pl.program_id(0)
        ypre = _conf_core(i, tm, gv_ref, gg_ref, gvp_ref, ggp_ref, w_ref, b_ref, lg_ref, lb_ref, pw_ref, pb_ref,
                          ext_ref)[4]
        o_ref[...] = ypre * _silu(zb_ref[...])

    full = lambda s: pl.BlockSpec(s, lambda i: (0, 0))
    return pl.pallas_call(
        body, name=name, grid=(T // tm,),
        in_specs=[cur(C_GV), cur(C_GG), prev(C_GV), prev(C_GG), cur(C_ZB), full((CONV_K, 512)), full((1, 512)),
                  full((1, 512)), full((1, 512)), full((512, 512)), full((1, 512))],
        out_specs=pl.BlockSpec((tm, 512), lambda i: (i, 0)),
        out_shape=jax.ShapeDtypeStruct((T, 512), F32),
        scratch_shapes=[pltpu.VMEM((tm + HALO_B, 512), F32)],
        compiler_params=_cparams(("parallel",)),
    )(proj, proj, proj, proj, proj, dw_w, _row(dw_b), _row(ln_g), _row(ln_b), pw2, _row(pw2_b))


def _conf_bwd1(dproj, proj, dyb, dw_w, dw_b, ln_g, ln_b, pw2, pw2_b, name):
    T = proj.shape[0]
    tm = min(512, T)
    cur, prev = _conf_specs(T, tm)

    def body(dp_any, gv_ref, gg_ref, gvp_ref, ggp_ref, zb_ref, dy_ref, w_ref, b_ref, lg_ref, lb_ref, pw_ref, pb_ref,
             dzb_ref, du1_ref, gpw_ref, st_ref, ext_ref):
        i = pl.program_id(0)
        xh, rstd, u2, u3, ypre = _conf_core(i, tm, gv_ref, gg_ref, gvp_ref, ggp_ref, w_ref, b_ref, lg_ref, lb_ref,
                                            pw_ref, pb_ref, ext_ref)
        zb, dy = zb_ref[...], dy_ref[...]
        dzb_ref[...] = dy * ypre * _dsilu(zb)
        dyp = dy * _silu(zb)
        du2 = _bdot(dyp, pw_ref[...], NT) * _dsilu(u2)
        dxh = du2 * lg_ref[...]
        du1 = rstd * (dxh - jnp.mean(dxh, axis=-1, keepdims=True) - xh * jnp.mean(dxh * xh, axis=-1, keepdims=True))
        du1_ref[...] = du1
        gpw = _bdot(u3, dyp, TN)
        rs = lambda a: jnp.sum(a, axis=0, keepdims=True)
        upd = jnp.concatenate([rs(dyp), rs(du2 * xh), rs(du2), rs(du1), jnp.zeros((4, 512), F32)], axis=0)

        @pl.when(i == 0)
        def _():
            gpw_ref[...] = gpw
            st_ref[...] = upd

        @pl.when(i > 0)
        def _():
            gpw_ref[...] += gpw
            st_ref[...] += upd

    full = lambda s: pl.BlockSpec(s, lambda i: (0, 0))
    blk = pl.BlockSpec((tm, 512), lambda i: (i, 0))
    return pl.pallas_call(
        body, name=name, grid=(T // tm,),
        in_specs=[pl.BlockSpec(memory_space=pl.ANY), cur(C_GV), cur(C_GG), prev(C_GV), prev(C_GG), cur(C_ZB), blk,
                  full((CONV_K, 512)), full((1, 512)), full((1, 512)), full((1, 512)), full((512, 512)), full((1, 512))],
        out_specs=[cur(C_ZB), blk, full((512, 512)), full((8, 512))],
        out_shape=[jax.ShapeDtypeStruct(dproj.shape, F32), jax.ShapeDtypeStruct((T, 512), F32),
                   jax.ShapeDtypeStruct((512, 512), F32), jax.ShapeDtypeStruct((8, 512), F32)],
        scratch_shapes=[pltpu.VMEM((tm + HALO_B, 512), F32)],
        input_output_aliases={0: 0},
        compiler_params=_cparams(("arbitrary",)),
    )(dproj, proj, proj, proj, proj, proj, dyb, dw_w, _row(dw_b), _row(ln_g), _row(ln_b), pw2, _row(pw2_b))


def _conf_bwd2(dproj, proj, du1, dw_w, name):
    T = proj.shape[0]
    tm = min(512, T)
    nt = T // tm
    r = tm // HALO_B
    cur, prev = _conf_specs(T, tm)

    def body(dp_any, gv_ref, gg_ref, gvp_ref, ggp_ref, du_ref, dun_ref, w_ref, dglu_ref, gw_ref, extu_ref, extd_ref):
        i = pl.program_id(0)
        gv, sg = gv_ref[...], _sig(gg_ref[...])
        extu_ref[:HALO_B] = jnp.where(i > 0, gvp_ref[...] * _sig(ggp_ref[...]), 0.0)
        extu_ref[HALO_B:] = gv * sg
        du1 = du_ref[...]
        extd_ref[:tm] = du1
        extd_ref[tm:] = jnp.where(i < nt - 1, dun_ref[...], 0.0)
        du0 = jnp.zeros((tm, 512), F32)
        rows = []
        for k in range(CONV_K):
            du0 = du0 + w_ref[k:k + 1, :] * extd_ref[pl.ds(CONV_K - 1 - k, tm), :]
            rows.append(jnp.sum(du1 * extu_ref[pl.ds(HALO_B - CONV_K + 1 + k, tm), :], axis=0, keepdims=True))
        rows.append(jnp.zeros((1, 512), F32))
        gw = jnp.concatenate(rows, axis=0)
        dglu_ref[:, :512] = du0 * sg
        dglu_ref[:, 512:] = du0 * gv * sg * (1.0 - sg)

        @pl.when(i == 0)
        def _():
            gw_ref[...] = gw

        @pl.when(i > 0)
        def _():
            gw_ref[...] += gw

    full = lambda s: pl.BlockSpec(s, lambda i: (0, 0))
    return pl.pallas_call(
        body, name=name, grid=(nt,),
        in_specs=[pl.BlockSpec(memory_space=pl.ANY), cur(C_GV), cur(C_GG), prev(C_GV), prev(C_GG),
                  pl.BlockSpec((tm, 512), lambda i: (i, 0)),
                  pl.BlockSpec((HALO_B, 512), lambda i: (jnp.minimum((i + 1) * r, T // HALO_B - 1), 0)),
                  full((CONV_K, 512))],
        out_specs=[pl.BlockSpec((tm, 1024), lambda i: (i, C_GV // 1024)), full((32, 512))],
        out_shape=[jax.ShapeDtypeStruct(dproj.shape, F32), jax.ShapeDtypeStruct((32, 512), F32)],
        scratch_shapes=[pltpu.VMEM((tm + HALO_B, 512), F32), pltpu.VMEM((tm + HALO_B, 512), F32)],
        input_output_aliases={0: 0},
        compiler_params=_cparams(("arbitrary",)),
    )(dproj, proj, proj, proj, proj, du1, du1, dw_w)


HALO_C = 8
QKV_C = 1536


def _softplus(x):
    return jnp.maximum(x, 0.0) + jnp.log1p(jnp.exp(-jnp.abs(x)))


def _gdn_conv(i, tm, x_ref, xp_ref, w_ref, ext_ref):
    ext_ref[:HALO_C] = jnp.where(i > 0, xp_ref[...], 0.0)
    ext_ref[HALO_C:] = x_ref[...]
    pre = jnp.zeros((tm, QKV_C), F32)
    for k in range(DN_K):
        pre = pre + w_ref[k:k + 1, :] * ext_ref[pl.ds(HALO_C - DN_K + 1 + k, tm), :]
    return pre


def _gdn_specs(T, tm):
    r = tm // HALO_C
    cur = pl.BlockSpec((tm, QKV_C), lambda i: (i, C_QC // QKV_C))
    prev = pl.BlockSpec((HALO_C, QKV_C), lambda i: (jnp.maximum(i * r - 1, 0), C_QC // QKV_C))
    ab = pl.BlockSpec((tm, 128), lambda i: (i, C_AB // 128))
    return cur, prev, ab


def _gdn_prep_fwd(proj, sconv_w, alog_v, dtb_v, name):
    T = proj.shape[0]
    tm = min(512, T)
    cur, prev, ab = _gdn_specs(T, tm)

    def body(x_ref, xp_ref, ab_ref, w_ref, al_ref, dt_ref, q_ref, k_ref, v_ref, gb_ref, ext_ref):
        i = pl.program_id(0)
        y = _silu(_gdn_conv(i, tm, x_ref, xp_ref, w_ref, ext_ref))
        for h in range(DN_HEADS):
            sl = slice(128 * h, 128 * h + 128)
            qh, kh = y[:, sl], y[:, 512 + 128 * h:512 + 128 * h + 128]
            q_ref[:, sl] = qh * lax.rsqrt(jnp.sum(qh * qh, axis=-1, keepdims=True) + EPS) * (128 ** -0.5)
            k_ref[:, sl] = kh * lax.rsqrt(jnp.sum(kh * kh, axis=-1, keepdims=True) + EPS)
        v_ref[...] = y[:, 1024:]
        abv = ab_ref[...]
        lane = lax.broadcasted_iota(jnp.int32, (tm, 128), 1)
        g = -jnp.exp(al_ref[...]) * _softplus(abv + dt_ref[...])
        gb_ref[...] = jnp.where(lane < DN_HEADS, g, _sig(abv))

    full = lambda s: pl.BlockSpec(s, lambda i: (0, 0))
    blk = pl.BlockSpec((tm, 512), lambda i: (i, 0))
    return pl.pallas_call(
        body, name=name, grid=(T // tm,),
        in_specs=[cur, prev, ab, full((DN_K, QKV_C)), full((1, 128)), full((1, 128))],
        out_specs=[blk, blk, blk, pl.BlockSpec((tm, 128), lambda i: (i, 0))],
        out_shape=[jax.ShapeDtypeStruct((T, 512), F32)] * 3 + [jax.ShapeDtypeStruct((T, 128), F32)],
        scratch_shapes=[pltpu.VMEM((tm + HALO_C, QKV_C), F32)],
        compiler_params=_cparams(("parallel",)),
    )(proj, proj, proj, sconv_w, alog_v, dtb_v)


def _hdot(a, b, dims=NN):
    return _dot(a, b, dims, precision=HI)


def _tri_inv(a, eye):
    n = -a
    t = eye + n
    p = n
    for _ in range(5):
        p = _hdot(p, p)
        t = t + _hdot(t, p)
    return t


def _tri_consts():
    ii = lax.broadcasted_iota(jnp.int32, (CHUNK, CHUNK), 0)
    jj = lax.broadcasted_iota(jnp.int32, (CHUNK, CHUNK), 1)
    return ii >= jj, ii > jj, (ii == jj).astype(F32)


def _gdn_chunk(q, k, v, gcol, grow, bcol, s, lower, strict, eye):
    dm = jnp.where(lower, jnp.exp(jnp.where(lower, gcol - grow, 0.0)), 0.0)
    kb = k * bcol
    a = jnp.where(strict, _hdot(kb, k, NT) * dm, 0.0)
    tm = _tri_inv(a, eye)
    gc = jnp.exp(gcol)
    vb = v * bcol
    kbg = kb * gc
    u = _hdot(tm, vb)
    w = _hdot(tm, kbg)
    p = _hdot(q, k, NT) * dm
    qe = q * gc
    glast = grow[:, CHUNK - 1:CHUNK]
    ke = k * jnp.exp(glast - gcol)
    vn = u - _hdot(w, s)
    o = _hdot(qe, s) + _hdot(p, vn)
    s_new = s * jnp.exp(glast) + _hdot(ke, vn, TN)
    return dict(q=q, k=k, v=v, bcol=bcol, gcol=gcol, glast=glast, dm=dm, kb=kb, a=a, tm=tm, gc=gc, vb=vb, kbg=kbg,
                w=w, p=p, qe=qe, ke=ke, vn=vn, o=o, s=s, s_new=s_new)


def _gdn_chunk_bwd(c, do, ds_new, lower, strict, ones):
    rs = lambda m: jnp.sum(m, axis=-1, keepdims=True)
    colsum = lambda m: _hdot(m, ones, TN)[:, :1]
    q, k, v, bcol, dm, tm, gc, s = c["q"], c["k"], c["v"], c["bcol"], c["dm"], c["tm"], c["gc"], c["s"]
    eg = jnp.exp(c["glast"])
    dvn = _hdot(c["p"], do, TN) + _hdot(c["ke"], ds_new)
    dqe = _hdot(do, s, NT)
    dp = jnp.where(lower, _hdot(do, c["vn"], NT), 0.0)
    ds = _hdot(c["qe"], do, TN) + eg * ds_new - _hdot(c["w"], dvn, TN)
    dw = -_hdot(dvn, s, NT)
    dke = _hdot(c["vn"], ds_new, NT)
    dglast = jnp.sum(rs(ds_new * s), axis=0, keepdims=True) * eg
    dk = dke * jnp.exp(c["glast"] - c["gcol"])
    r_ke = rs(dke * c["ke"])
    dglast = dglast + jnp.sum(r_ke, axis=0, keepdims=True)
    dgam = rs(dqe * c["qe"]) - r_ke
    dq = dqe * gc
    dpm = dp * dm
    dq = dq + _hdot(dpm, k)
    dk = dk + _hdot(dpm, q, TN)
    mp = dp * c["p"]
    dgam = dgam + rs(mp) - colsum(mp)
    dt = _hdot(dvn, c["vb"], NT) + _hdot(dw, c["kbg"], NT)
    dvb = _hdot(tm, dvn, TN)
    dkbg = _hdot(tm, dw, TN)
    dkb = dkbg * gc
    dgam = dgam + rs(dkbg * c["kbg"])
    da = jnp.where(strict, -_hdot(_hdot(tm, dt, TN), tm, NT), 0.0)
    dam = da * dm
    dkb = dkb + _hdot(dam, k)
    dk = dk + _hdot(dam, c["kb"], TN)
    ma = da * c["a"]
    dgam = dgam + rs(ma) - colsum(ma)
    dk = dk + dkb * bcol
    dbeta = rs(dkb * k) + rs(dvb * v)
    dv = dvb * bcol
    row = lax.broadcasted_iota(jnp.int32, (CHUNK, 1), 0)
    dgam = dgam + jnp.where(row == CHUNK - 1, dglast, 0.0)
    dg = _hdot(lower.astype(F32), dgam, TN)
    return dq, dk, dv, dg, dbeta, ds


def _gdn_scan_fwd(qd, kd, vd, gb, gbt, proj, dn_g, name):
    T = qd.shape[0]
    nc = T // CHUNK

    def body(q_ref, k_ref, v_ref, gb_ref, gt_ref, z_ref, ng_ref, y_ref, ss_ref, s_ref):
        n = pl.program_id(0)

        @pl.when(n == 0)
        def _():
            s_ref[...] = jnp.zeros_like(s_ref)

        lower, strict, eye = _tri_consts()
        lmat = lower.astype(F32)
        gcs = _hdot(lmat, gb_ref[...])
        grs = _hdot(gt_ref[0], lmat, NT)
        ss_ref[0] = s_ref[...]
        for h in range(DN_HEADS):
            sl = slice(128 * h, 128 * h + 128)
            c = _gdn_chunk(q_ref[:, sl], k_ref[:, sl], v_ref[:, sl], gcs[:, h:h + 1], grs[h:h + 1, :],
                           gb_ref[:, DN_HEADS + h:DN_HEADS + h + 1], s_ref[h], lower, strict, eye)
            s_ref[h] = c["s_new"]
            y_ref[:, sl] = _rms(c["o"], None)[0] * ng_ref[...] * _silu(z_ref[:, sl])

    blk = pl.BlockSpec((CHUNK, 512), lambda n: (n, 0))
    return pl.pallas_call(
        body, name=name, grid=(nc,),
        in_specs=[blk, blk, blk, pl.BlockSpec((CHUNK, 128), lambda n: (n, 0)),
                  pl.BlockSpec((1, 8, CHUNK), lambda n: (n, 0, 0)),
                  pl.BlockSpec((CHUNK, 512), lambda n: (n, C_ZC // 512)), pl.BlockSpec((1, 128), lambda n: (0, 0))],
        out_specs=[blk, pl.BlockSpec((1, DN_HEADS, 128, 128), lambda n: (n, 0, 0, 0))],
        out_shape=[jax.ShapeDtypeStruct((T, 512), F32), jax.ShapeDtypeStruct((nc, DN_HEADS, 128, 128), F32)],
        scratch_shapes=[pltpu.VMEM((DN_HEADS, 128, 128), F32)],
        compiler_params=_cparams(("arbitrary",)),
    )(qd, kd, vd, gb, gbt, proj, dn_g)


def _gdn_scan_bwd(dproj, qd, kd, vd, gb, gbt, ssave, proj, dyc, dn_g, name):
    T = qd.shape[0]
    nc = T // CHUNK
    rev = lambda n: nc - 1 - n

    def body(dp_any, q_ref, k_ref, v_ref, gb_ref, gt_ref, ss_ref, z_ref, dy_ref, ng_ref,
             dz_ref, dq_ref, dk_ref, dv_ref, dgb_ref, gng_ref, ds_ref):
        n = pl.program_id(0)

        @pl.when(n == 0)
        def _():
            ds_ref[...] = jnp.zeros_like(ds_ref)
            gng_ref[...] = jnp.zeros_like(gng_ref)

        lower, strict, eye = _tri_consts()
        lmat = lower.astype(F32)
        ones = jnp.ones((CHUNK, 128), F32)
        gcs = _hdot(lmat, gb_ref[...])
        grs = _hdot(gt_ref[0], lmat, NT)
        lane = lax.broadcasted_iota(jnp.int32, (CHUNK, 128), 1)
        dgb = jnp.zeros((CHUNK, 128), F32)
        gng = jnp.zeros((1, 128), F32)
        for h in range(DN_HEADS):
            sl = slice(128 * h, 128 * h + 128)
            c = _gdn_chunk(q_ref[:, sl], k_ref[:, sl], v_ref[:, sl], gcs[:, h:h + 1], grs[h:h + 1, :],
                           gb_ref[:, DN_HEADS + h:DN_HEADS + h + 1], ss_ref[0, h], lower, strict, eye)
            oh, r = _rms(c["o"], None)
            z, dy = z_ref[:, sl], dy_ref[:, sl]
            dz_ref[:, sl] = dy * (oh * ng_ref[...]) * _dsilu(z)
            do, gg = _rms_bwd(dy * _silu(z), oh, r, ng_ref[...])
            gng = gng + jnp.sum(gg, axis=0, keepdims=True)
            dq, dk, dv, dg, dbeta, ds = _gdn_chunk_bwd(c, do, ds_ref[h], lower, strict, ones)
            ds_ref[h] = ds
            dq_ref[:, sl], dk_ref[:, sl], dv_ref[:, sl] = dq, dk, dv
            dgb = dgb + jnp.where(lane == h, dg, 0.0) + jnp.where(lane == DN_HEADS + h, dbeta, 0.0)
        dgb_ref[...] = dgb
        gng_ref[...] += gng

    blk = pl.BlockSpec((CHUNK, 512), lambda n: (rev(n), 0))
    nar = pl.BlockSpec((CHUNK, 128), lambda n: (rev(n), 0))
    return pl.pallas_call(
        body, name=name, grid=(nc,),
        in_specs=[pl.BlockSpec(memory_space=pl.ANY), blk, blk, blk, nar,
                  pl.BlockSpec((1, 8, CHUNK), lambda n: (rev(n), 0, 0)),
                  pl.BlockSpec((1, DN_HEADS, 128, 128), lambda n: (rev(n), 0, 0, 0)),
                  pl.BlockSpec((CHUNK, 512), lambda n: (rev(n), C_ZC // 512)), blk,
                  pl.BlockSpec((1, 128), lambda n: (0, 0))],
        out_specs=[pl.BlockSpec((CHUNK, 512), lambda n: (rev(n), C_ZC // 512)), blk, blk, blk, nar,
                   pl.BlockSpec((1, 128), lambda n: (0, 0))],
        out_shape=[jax.ShapeDtypeStruct(dproj.shape, F32)] + [jax.ShapeDtypeStruct((T, 512), F32)] * 3
        + [jax.ShapeDtypeStruct((T, 128), F32), jax.ShapeDtypeStruct((1, 128), F32)],
        scratch_shapes=[pltpu.VMEM((DN_HEADS, 128, 128), F32)],
        input_output_aliases={0: 0},
        compiler_params=_cparams(("arbitrary",)),
    )(dproj, qd, kd, vd, gb, gbt, ssave, proj, dyc, dn_g)


def _gdn_prep_bwd1(dproj, proj, dqd, dkd, dvd, dgb, dkv_a, sconv_w, alog_v, dtb_v, name):
    T = proj.shape[0]
    tm = min(512, T)
    cur, prev, ab = _gdn_specs(T, tm)

    def body(dp_any, x_ref, xp_ref, ab_ref, dq_ref, dk_ref, dv_ref, dgb_ref, dkv_ref, w_ref, al_ref, dt_ref,
             o_ref, dpre_ref, st_ref, ext_ref):
        i = pl.program_id(0)
        pre = _gdn_conv(i, tm, x_ref, xp_ref, w_ref, ext_ref)
        y, dsl = _silu(pre), _dsilu(pre)
        for h in range(DN_HEADS):
            for base, g_ref, scale in ((0, dq_ref, 128 ** -0.5), (512, dk_ref, 1.0)):
                sl = slice(base + 128 * h, base + 128 * h + 128)
                xh = y[:, sl]
                r = lax.rsqrt(jnp.sum(xh * xh, axis=-1, keepdims=True) + EPS)
                xn = xh * r
                gy = g_ref[:, 128 * h:128 * h + 128]
                dpre_ref[:, sl] = (scale * r) * (gy - xn * jnp.sum(gy * xn, axis=-1, keepdims=True)) * dsl[:, sl]
        dpre_ref[:, 1024:] = dv_ref[...] * dsl[:, 1024:]
        abv, dgb = ab_ref[...], dgb_ref[...]
        lane = lax.broadcasted_iota(jnp.int32, (tm, 128), 1)
        na = -jnp.exp(al_ref[...])
        xs = abv + dt_ref[...]
        da = dgb * na * _sig(xs)
        b = _sig(abv)
        o_ref[:, :256] = dkv_ref[...]
        o_ref[:, 256:] = jnp.where(lane < DN_HEADS, da, jnp.where(lane < 2 * DN_HEADS, dgb * b * (1.0 - b), 0.0))
        head = lane < DN_HEADS
        upd = jnp.concatenate([jnp.sum(jnp.where(head, dgb * na * _softplus(xs), 0.0), axis=0, keepdims=True),
                               jnp.sum(jnp.where(head, da, 0.0), axis=0, keepdims=True), jnp.zeros((6, 128), F32)],
                              axis=0)

        @pl.when(i == 0)
        def _():
            st_ref[...] = upd

        @pl.when(i > 0)
        def _():
            st_ref[...] += upd

    full = lambda s: pl.BlockSpec(s, lambda i: (0, 0))
    blk = pl.BlockSpec((tm, 512), lambda i: (i, 0))
    return pl.pallas_call(
        body, name=name, grid=(T // tm,),
        in_specs=[pl.BlockSpec(memory_space=pl.ANY), cur, prev, ab, blk, blk, blk,
                  pl.BlockSpec((tm, 128), lambda i: (i, 0)), pl.BlockSpec((tm, 256), lambda i: (i, 0)),
                  full((DN_K, QKV_C)), full((1, 128)), full((1, 128))],
        out_specs=[pl.BlockSpec((tm, 384), lambda i: (i, C_KA // 384)),
                   pl.BlockSpec((tm, QKV_C), lambda i: (i, 0)), full((8, 128))],
        out_shape=[jax.ShapeDtypeStruct(dproj.shape, F32), jax.ShapeDtypeStruct((T, QKV_C), F32),
                   jax.ShapeDtypeStruct((8, 128), F32)],
        scratch_shapes=[pltpu.VMEM((tm + HALO_C, QKV_C), F32)],
        input_output_aliases={0: 0},
        compiler_params=_cparams(("arbitrary",)),
    )(dproj, proj, proj, proj, dqd, dkd, dvd, dgb, dkv_a, sconv_w, alog_v, dtb_v)


def _gdn_prep_bwd2(dproj, proj, dpre, sconv_w, name):
    T = proj.shape[0]
    tm = min(512, T)
    nt = T // tm
    r = tm // HALO_C
    cur, prev, _ = _gdn_specs(T, tm)

    def body(dp_any, x_ref, xp_ref, d_ref, dn_ref, w_ref, dx_ref, gw_ref, extx_ref, extd_ref):
        i = pl.program_id(0)
        extx_ref[:HALO_C] = jnp.where(i > 0, xp_ref[...], 0.0)
        extx_ref[HALO_C:] = x_ref[...]
        d = d_ref[...]
        extd_ref[:tm] = d
        extd_ref[tm:] = jnp.where(i < nt - 1, dn_ref[...], 0.0)
        dx = jnp.zeros((tm, QKV_C), F32)
        rows = []
        for k in range(DN_K):
            dx = dx + w_ref[k:k + 1, :] * extd_ref[pl.ds(DN_K - 1 - k, tm), :]
            rows.append(jnp.sum(d * extx_ref[pl.ds(HALO_C - DN_K + 1 + k, tm), :], axis=0, keepdims=True))
        rows.append(jnp.zeros((8 - DN_K, QKV_C), F32))
        gw = jnp.concatenate(rows, axis=0)
        dx_ref[...] = dx

        @pl.when(i == 0)
        def _():
            gw_ref[...] = gw

        @pl.when(i > 0)
        def _():
            gw_ref[...] += gw

    full = lambda s: pl.BlockSpec(s, lambda i: (0, 0))
    return pl.pallas_call(
        body, name=name, grid=(nt,),
        in_specs=[pl.BlockSpec(memory_space=pl.ANY), cur, prev, pl.BlockSpec((tm, QKV_C), lambda i: (i, 0)),
                  pl.BlockSpec((HALO_C, QKV_C), lambda i: (jnp.minimum((i + 1) * r, T // HALO_C - 1), 0)),
                  full((DN_K, QKV_C))],
        out_specs=[cur, full((8, QKV_C))],
        out_shape=[jax.ShapeDtypeStruct(dproj.shape, F32), jax.ShapeDtypeStruct((8, QKV_C), F32)],
        scratch_shapes=[pltpu.VMEM((tm + HALO_C, QKV_C), F32), pltpu.VMEM((tm + HALO_C, QKV_C), F32)],
        input_output_aliases={0: 0},
        compiler_params=_cparams(("arbitrary",)),
    )(dproj, proj, proj, dpre, dpre, sconv_w)


def _merge_fwd(x, proj, ya, yb, yc, wa, wb, wc, wo, gate, name):
    T = x.shape[0]
    tm = min(256, T)

    def body(x_ref, mg_ref, ya_ref, yb_ref, yc_ref, wa_ref, wb_ref, wc_ref, wo_ref, gate_ref, o_ref):
        merged = (_sig(mg_ref[:, :D]) * _bdot(ya_ref[...], wa_ref[...])
                  + _sig(mg_ref[:, D:2 * D]) * _bdot(yb_ref[...], wb_ref[...])
                  + _sig(mg_ref[:, 2 * D:]) * _bdot(yc_ref[...], wc_ref[...]))
        o_ref[...] = x_ref[...] + gate_ref[...] * _bdot(merged, wo_ref[...])

    full = lambda s: pl.BlockSpec(s, lambda i: (0, 0))
    yb_ = pl.BlockSpec((tm, 512), lambda i: (i, 0))
    return pl.pallas_call(
        body, name=name, grid=(T // tm,),
        in_specs=[pl.BlockSpec((tm, D), lambda i: (i, 0)), pl.BlockSpec((tm, 3 * D), lambda i: (i, 0)), yb_, yb_, yb_,
                  full((512, D)), full((512, D)), full((512, D)), full((D, D)), full((1, D))],
        out_specs=pl.BlockSpec((tm, D), lambda i: (i, 0)),
        out_shape=jax.ShapeDtypeStruct((T, D), F32),
        compiler_params=_cparams(("parallel",)),
    )(x, proj, ya, yb, yc, wa, wb, wc, wo, _row(gate))


def _merge_bwd(dout, proj, ya, yb, yc, wa, wb, wc, wo, gate, name):
    T = dout.shape[0]
    tm = min(128, T)
    nt = T // tm

    def body(do_ref, mg_ref, ya_ref, yb_ref, yc_ref, wa_ref, wb_ref, wc_ref, wo_ref, gate_ref,
             dmg_ref, dya_ref, dyb_ref, dyc_ref, gwa_hbm, gwb_hbm, gwc_hbm, gwo_hbm, gg_ref,
             gwa_ref, gwb_ref, gwc_ref, gwo_ref):
        i = pl.program_id(0)

        @pl.when(i == 0)
        def _():
            for r in (gwa_ref, gwb_ref, gwc_ref, gwo_ref, gg_ref):
                r[...] = jnp.zeros_like(r)

        ys = (ya_ref[...], yb_ref[...], yc_ref[...])
        ws = (wa_ref, wb_ref, wc_ref)
        gs = tuple(_sig(mg_ref[:, j * D:(j + 1) * D]) for j in range(3))
        ps = tuple(_bdot(ys[j], ws[j][...]) for j in range(3))
        merged = gs[0] * ps[0] + gs[1] * ps[1] + gs[2] * ps[2]
        mo = _bdot(merged, wo_ref[...])
        do = do_ref[...]
        gg_ref[...] += jnp.sum(do * mo, axis=0, keepdims=True)
        dmo = do * gate_ref[...]
        dmerged = _bdot(dmo, wo_ref[...], NT)
        gwo_ref[...] += _bdot(merged, dmo, TN)
        for j, (dy_ref, gw_ref) in enumerate(((dya_ref, gwa_ref), (dyb_ref, gwb_ref), (dyc_ref, gwc_ref))):
            dp = dmerged * gs[j]
            dmg_ref[:, j * D:(j + 1) * D] = dmerged * ps[j] * gs[j] * (1.0 - gs[j])
            dy_ref[...] = _bdot(dp, ws[j][...], NT)
            gw_ref[...] += _bdot(ys[j], dp, TN)

        @pl.when(i == nt - 1)
        def _():
            for src, dst in ((gwa_ref, gwa_hbm), (gwb_ref, gwb_hbm), (gwc_ref, gwc_hbm), (gwo_ref, gwo_hbm)):
                pltpu.sync_copy(src, dst)

    full = lambda s: pl.BlockSpec(s, lambda i: (0, 0))
    yb_ = pl.BlockSpec((tm, 512), lambda i: (i, 0))
    anyspec = pl.BlockSpec(memory_space=pl.ANY)
    return pl.pallas_call(
        body, name=name, grid=(nt,),
        in_specs=[pl.BlockSpec((tm, D), lambda i: (i, 0)), pl.BlockSpec((tm, 3 * D), lambda i: (i, 0)), yb_, yb_, yb_,
                  full((512, D)), full((512, D)), full((512, D)), full((D, D)), full((1, D))],
        out_specs=[pl.BlockSpec((tm, 3 * D), lambda i: (i, 0)), yb_, yb_, yb_, anyspec, anyspec, anyspec, anyspec,
                   full((1, D))],
        out_shape=[jax.ShapeDtypeStruct((T, NP), F32)] + [jax.ShapeDtypeStruct((T, 512), F32)] * 3
        + [jax.ShapeDtypeStruct((512, D), F32)] * 3 + [jax.ShapeDtypeStruct((D, D), F32), jax.ShapeDtypeStruct((1, D), F32)],
        scratch_shapes=[pltpu.VMEM((512, D), F32)] * 3 + [pltpu.VMEM((D, D), F32)],
        compiler_params=_cparams(("arbitrary",)),
    )(dout, proj, ya, yb, yc, wa, wb, wc, wo, _row(gate))


def _loss_head(y, tgt, name):
    T = y.shape[0]
    tm = min(512, T)

    def body(y_ref, t_ref, dy_ref, l_ref):
        i = pl.program_id(0)
        diff = y_ref[...] - t_ref[...]
        dy_ref[...] = diff * (1.0 / D)
        part = jnp.sum(diff * diff, axis=0, keepdims=True)

        @pl.when(i == 0)
        def _():
            l_ref[...] = part

        @pl.when(i > 0)
        def _():
            l_ref[...] += part

    blk = pl.BlockSpec((tm, D), lambda i: (i, 0))
    return pl.pallas_call(
        body, name=name, grid=(T // tm,), in_specs=[blk, blk],
        out_specs=[blk, pl.BlockSpec((1, D), lambda i: (0, 0))],
        out_shape=[jax.ShapeDtypeStruct((T, D), F32), jax.ShapeDtypeStruct((1, D), F32)],
        compiler_params=_cparams(("arbitrary",)),
    )(y, tgt)


def _ada_fwd(c_all, w_ada, b_my, name):
    def body(c_ref, w_ref, b_ref, o_ref):
        sc = _silu(c_ref[...])
        for l in range(DEPTH):
            o_ref[l] = _bdot(sc, w_ref[l]) + b_ref[l:l + 1, :]

    return pl.pallas_call(body, name=name, out_shape=jax.ShapeDtypeStruct((DEPTH, N_DEV, w_ada.shape[2]), F32),
                          compiler_params=_cparams())(c_all, w_ada, b_my)


def _ada_bwd(c_all, dmod_my, name):
    def body(c_ref, d_ref, o_ref):
        sc = _silu(c_ref[...])
        for l in range(DEPTH):
            o_ref[l] = _bdot(sc, d_ref[l], TN)

    return pl.pallas_call(body, name=name, out_shape=jax.ShapeDtypeStruct((DEPTH, D, dmod_my.shape[2]), F32),
                          compiler_params=_cparams())(c_all, dmod_my)


def _adam_math(w, g, m, v):
    m = ADAM_B1 * m + (1.0 - ADAM_B1) * g
    v = ADAM_B2 * v + (1.0 - ADAM_B2) * (g * g)
    m_hat = m / (1.0 - ADAM_B1 ** ADAM_STEP)
    v_hat = v / (1.0 - ADAM_B2 ** ADAM_STEP)
    return -ADAM_LR * (m_hat / (jnp.sqrt(v_hat) + ADAM_EPS) + ADAM_WD * w), m, v


def _row_tile(rows, cap):
    best = rows
    for t in range(8, min(rows, cap) + 1, 8):
        if rows % t == 0:
            best = t
    return best if best <= cap else rows


def _adamw(w, g, m, v, name):
    R, C = w.shape
    tr = _row_tile(R, 256)

    def body(w_ref, g_ref, m_ref, v_ref, d_ref, mo_ref, vo_ref):
        d_ref[...], mo_ref[...], vo_ref[...] = _adam_math(w_ref[...], g_ref[...], m_ref[...], v_ref[...])

    blk = pl.BlockSpec((tr, C), lambda i: (i, 0))
    return pl.pallas_call(body, name=name, grid=(R // tr,), in_specs=[blk] * 4, out_specs=[blk] * 3,
                          out_shape=[jax.ShapeDtypeStruct((R, C), F32)] * 3,
                          compiler_params=_cparams(("parallel",)))(w, g, m, v)


def _sum_parts(parts, name):
    _, R, C = parts.shape
    tr = _row_tile(R, 256)

    def body(p_ref, o_ref):
        acc = p_ref[0]
        for j in range(1, N_DEV):
            acc = acc + p_ref[j]
        o_ref[...] = acc

    return pl.pallas_call(body, name=name, grid=(R // tr,),
                          in_specs=[pl.BlockSpec((N_DEV, tr, C), lambda i: (0, i, 0))],
                          out_specs=pl.BlockSpec((tr, C), lambda i: (i, 0)),
                          out_shape=jax.ShapeDtypeStruct((R, C), F32), compiler_params=_cparams(("parallel",)))(parts)


def _sum_adamw(parts, w, m, v, name):
    _, R, C = parts.shape
    tr = _row_tile(R, 64)

    def body(p_ref, w_ref, m_ref, v_ref, g_ref, d_ref, mo_ref, vo_ref):
        g = p_ref[0]
        for j in range(1, N_DEV):
            g = g + p_ref[j]
        g_ref[...] = g
        d_ref[...], mo_ref[...], vo_ref[...] = _adam_math(w_ref[...], g, m_ref[...], v_ref[...])

    blk = pl.BlockSpec((tr, C), lambda i: (i, 0))
    return pl.pallas_call(body, name=name, grid=(R // tr,),
                          in_specs=[pl.BlockSpec((N_DEV, tr, C), lambda i: (0, i, 0)), blk, blk, blk],
                          out_specs=[blk] * 4, out_shape=[jax.ShapeDtypeStruct((R, C), F32)] * 4,
                          compiler_params=_cparams(("parallel",)))(parts, w, m, v)


def _mesh_pos():
    return lax.axis_index("x"), lax.axis_index("y"), lax.axis_index("c")


def _all_gather(blocks, name):
    n = len(blocks)

    def body(*refs):
        ins, outs = refs[:n], refs[n:2 * n]
        send_sems, recv_sems, local_sems = refs[2 * n:]
        x, y, c = _mesh_pos()
        me, sibling = (x, y, c), (x, y, 1 - c)
        chips = [(1 - x, y), (x, 1 - y), (1 - x, 1 - y)]
        idx = lambda p: 4 * p[0] + 2 * p[1] + p[2]

        def copy(a, k, block, to, src=None):
            dst = outs[a].at[idx(block)]
            return pltpu.make_async_remote_copy(
                src_ref=dst if src is None else src, dst_ref=dst, send_sem=send_sems.at[a, k],
                recv_sem=recv_sems.at[a, k], device_id=to, device_id_type=pl.DeviceIdType.MESH)

        mine = [pltpu.make_async_copy(ins[a], outs[a].at[idx(me)], local_sems.at[a]) for a in range(n)]
        for cp in mine:
            cp.start()
        first = []
        for a in range(n):
            first.append(copy(a, 0, me, sibling, src=ins[a]))
            first += [copy(a, 1 + j, me, (*chip, c), src=ins[a]) for j, chip in enumerate(chips)]
        for cp in first:
            cp.start()
        passed = []
        for j, chip in enumerate(chips):
            for a in range(n):
                copy(a, 1 + j, (*chip, c), me).wait_recv()
                cp = copy(a, 4 + j, (*chip, c), sibling)
                cp.start()
                passed.append(cp)
        for a in range(n):
            copy(a, 0, sibling, me).wait_recv()
            for j, chip in enumerate(chips):
                copy(a, 4 + j, (*chip, 1 - c), me).wait_recv()
        for cp in first + passed:
            cp.wait_send()
        for cp in mine:
            cp.wait()

    anyspec = pl.BlockSpec(memory_space=pl.ANY)
    return pl.pallas_call(
        body, name=name, in_specs=[anyspec] * n, out_specs=[anyspec] * n,
        out_shape=[jax.ShapeDtypeStruct((N_DEV,) + b.shape, b.dtype) for b in blocks],
        scratch_shapes=[pltpu.SemaphoreType.DMA((n, 7)), pltpu.SemaphoreType.DMA((n, 7)),
                        pltpu.SemaphoreType.DMA((n,))],
    )(*blocks)


def _all_to_all(buf, name):
    def body(in_ref, out_ref, send_sems, recv_sems, local_sem):
        x, y, c = _mesh_pos()
        me = 4 * x + 2 * y + c
        local = pltpu.make_async_copy(in_ref.at[me], out_ref.at[me], local_sem)
        local.start()
        copies = []
        for k in range(1, N_DEV):
            px = 1 - x if k & 4 else x
            py = 1 - y if k & 2 else y
            pc = 1 - c if k & 1 else c
            cp = pltpu.make_async_remote_copy(
                src_ref=in_ref.at[4 * px + 2 * py + pc], dst_ref=out_ref.at[me], send_sem=send_sems.at[k - 1],
                recv_sem=recv_sems.at[k - 1], device_id=(px, py, pc), device_id_type=pl.DeviceIdType.MESH)
            cp.start()
            copies.append(cp)
        for cp in copies:
            cp.wait()
        local.wait()

    anyspec = pl.BlockSpec(memory_space=pl.ANY)
    return pl.pallas_call(
        body, name=name, in_specs=[anyspec], out_specs=anyspec,
        out_shape=jax.ShapeDtypeStruct(buf.shape, buf.dtype),
        scratch_shapes=[pltpu.SemaphoreType.DMA((N_DEV - 1,)), pltpu.SemaphoreType.DMA((N_DEV - 1,)),
                        pltpu.SemaphoreType.DMA(())],
    )(buf)


_BIG_SHAPES = ((DEPTH, D, D_IN // N_DEV), (DEPTH, 512 // N_DEV, 512), (DEPTH, 512, D // N_DEV),
               (DEPTH, 512, D // N_DEV), (DEPTH, 512, D // N_DEV), (DEPTH, D // N_DEV, D))
_BIG_ELEMS = tuple(int(np.prod(s)) for s in _BIG_SHAPES)
PACK_COLS = 1024
PACK_ROWS = -(-sum(_BIG_ELEMS) // (PACK_COLS * 32)) * 32


def _pack_flat(parts, rows, cols, lead=()):
    flat = jnp.concatenate([p.reshape(lead + (-1,)) for p in parts], axis=-1)
    pad = rows * cols - flat.shape[-1]
    flat = jnp.pad(flat, [(0, 0)] * len(lead) + [(0, pad)])
    return flat.reshape(lead + (rows, cols))


def _unpack_flat(buf, shapes, lead=()):
    flat = buf.reshape(lead + (-1,))
    out, o = [], 0
    for s in shapes:
        n = int(np.prod(s))
        out.append(flat[..., o:o + n].reshape(lead + tuple(s)))
        o += n
    return out


def _pad_cols(w_full):
    runs = [w_full[..., a:b] for a, b in _PAD_FROM]
    runs.append(jnp.zeros(w_full.shape[:-1] + (NP - D_IN,), w_full.dtype))
    return jnp.concatenate(runs, axis=-1)


def _unpad_cols(w_pad):
    return jnp.concatenate([w_pad[..., a:b] for a, b in _UNPAD_FROM], axis=-1)


_SMALL = (("b_ada", (3 * D,)), ("norm_g", (D,)), ("q_norm_g", (64,)), ("k_norm_g", (64,)), ("sinks", (8,)),
          ("dw_b", (512,)), ("ln_g", (512,)), ("ln_b", (512,)), ("pw2_b", (512,)), ("a_log", (4,)),
          ("dt_bias", (4,)), ("dn_norm_g", (128,)), ("dw_w", (CONV_K, 512)), ("sconv_w", (DN_K, QKV_C)))
_N_REPL = 12
_SMALL_SHAPES = tuple((DEPTH,) + s for _, s in _SMALL)
_SMALL_ROWS = -(-sum(int(np.prod(s)) for s in _SMALL_SHAPES) // (128 * 8)) * 8
_UPD_SHAPES = _SMALL_SHAPES[:_N_REPL] + ((DEPTH, CONV_K, 512 // N_DEV), (DEPTH, DN_K, QKV_C // N_DEV))
_UPD_ROWS = -(-sum(int(np.prod(s)) for s in _UPD_SHAPES) // (128 * 8)) * 8


def _lane4(v):
    return jnp.pad(v, (0, 124)).reshape(1, 128)


def kernel(x, c, w_ada, b_ada, norm_g, w_in, q_norm_g, k_norm_g, sinks, dw_w, dw_b, ln_g, ln_b, pw2_w, pw2_b, sconv_w, a_log, dt_bias, dn_norm_g, w_proj_a, w_proj_b, w_proj_c, w_out, loss_target, m_w_ada, m_b_ada, m_norm_g, m_w_in, m_q_norm_g, m_k_norm_g, m_sinks, m_dw_w, m_dw_b, m_ln_g, m_ln_b, m_pw2_w, m_pw2_b, m_sconv_w, m_a_log, m_dt_bias, m_dn_norm_g, m_w_proj_a, m_w_proj_b, m_w_proj_c, m_w_out, v_w_ada, v_b_ada, v_norm_g, v_w_in, v_q_norm_g, v_k_norm_g, v_sinks, v_dw_w, v_dw_b, v_ln_g, v_ln_b, v_pw2_w, v_pw2_b, v_sconv_w, v_a_log, v_dt_bias, v_dn_norm_g, v_w_proj_a, v_w_proj_b, v_w_proj_c, v_w_out):
    T = x.shape[1]
    nc = T // CHUNK
    xi, yi, ci = _mesh_pos()
    me = 4 * xi + 2 * yi + ci
    big_w = (w_in, pw2_w, w_proj_a, w_proj_b, w_proj_c, w_out)
    big_m = (m_w_in, m_pw2_w, m_w_proj_a, m_w_proj_b, m_w_proj_c, m_w_out)
    big_v = (v_w_in, v_pw2_w, v_w_proj_a, v_w_proj_b, v_w_proj_c, v_w_out)

    ada_cols = w_ada.shape[2]
    dw_cols, sc_cols = dw_w.shape[2], sconv_w.shape[2]
    big16 = _pack_flat([a.astype(BF16) for a in big_w], PACK_ROWS, PACK_COLS)
    small_shapes = ((D,), dw_w.shape, sconv_w.shape)
    small_rows = -(-sum(int(np.prod(s)) for s in small_shapes) // (128 * 8)) * 8
    small32 = _pack_flat([c, dw_w, sconv_w], small_rows, 128)
    g16, g32 = _all_gather([big16, small32], "gather_weights")
    gw_in, gpw2, gpa, gpb, gpc, gwo = _unpack_flat(g16, _BIG_SHAPES, lead=(N_DEV,))
    wp = _pad_cols(gw_in.transpose(1, 2, 0, 3).reshape(DEPTH, D, D_IN))
    pw2_f = gpw2.transpose(1, 0, 2, 3).reshape(DEPTH, 512, 512)
    wa_f, wb_f, wc_f = (g.transpose(1, 2, 0, 3).reshape(DEPTH, 512, D) for g in (gpa, gpb, gpc))
    wo_f = gwo.transpose(1, 0, 2, 3).reshape(DEPTH, D, D)
    c_all, gdw, gsc = _unpack_flat(g32, small_shapes, lead=(N_DEV,))
    dw_f = gdw.transpose(1, 2, 0, 3).reshape(DEPTH, CONV_K, 512)
    sc_f = gsc.transpose(1, 2, 0, 3).reshape(DEPTH, DN_K, QKV_C)

    b_my = lax.dynamic_slice(b_ada, (0, me * ada_cols), (DEPTH, ada_cols))
    mod_part = _ada_fwd(c_all, w_ada, b_my, "ada_fwd")
    (gmod,) = _all_gather([mod_part.reshape(-1, 128)], "gather_mod")
    mod_all = gmod.reshape(N_DEV, DEPTH, N_DEV, ada_cols).transpose(1, 2, 0, 3).reshape(DEPTH, N_DEV, 3 * D)
    mod = lax.dynamic_index_in_dim(mod_all, me, axis=1, keepdims=False)
    shift, scale, gate = mod[:, :D], mod[:, D:2 * D], mod[:, 2 * D:]

    xs, saved = [x[0]], []
    for l in range(DEPTH):
        xl = xs[-1]
        h = _norm_fwd(xl, norm_g[l], scale[l], shift[l], f"norm_fwd{l}")
        proj = _mm(h, wp[l], tm=min(512, T), tn=1152, tk=D, name=f"in_proj{l}")
        ya = _attn_fwd(proj, q_norm_g[l], k_norm_g[l], sinks[l], f"attn_fwd{l}")
        yb = _conf_fwd(proj, dw_f[l], dw_b[l], ln_g[l], ln_b[l], pw2_f[l], pw2_b[l], f"conf_fwd{l}")
        alv, dtv, dng = _lane4(a_log[l]), _lane4(dt_bias[l]), _row(dn_norm_g[l])
        qd, kd, vd, gb = _gdn_prep_fwd(proj, sc_f[l], alv, dtv, f"gdn_prep_fwd{l}")
        gbt = gb[:, :8].reshape(nc, CHUNK, 8).transpose(0, 2, 1)
        yc, ss = _gdn_scan_fwd(qd, kd, vd, gb, gbt, proj, dng, f"gdn_scan_fwd{l}")
        xs.append(_merge_fwd(xl, proj, ya, yb, yc, wa_f[l], wb_f[l], wc_f[l], wo_f[l], gate[l], f"merge_fwd{l}"))
        saved.append((h, proj, ya, yb, yc, qd, kd, vd, gb, gbt, ss, alv, dtv, dng))

    dout, lsum = _loss_head(xs[-1], loss_target[0], "loss_head")
    loss = lax.psum(0.5 * jnp.sum(lsum) / D, ("x", "y", "c"))

    small = {name: [None] * DEPTH for name, _ in _SMALL}
    big_g = [[None] * DEPTH for _ in big_w]
    for l in reversed(range(DEPTH)):
        h, proj, ya, yb, yc, qd, kd, vd, gb, gbt, ss, alv, dtv, dng = saved[l]
        dproj, dya, dyb, dyc, g_wa, g_wb, g_wc, g_wo, g_gate = _merge_bwd(
            dout, proj, ya, yb, yc, wa_f[l], wb_f[l], wc_f[l], wo_f[l], gate[l], f"merge_bwd{l}")
        dproj, dkv_a, g_q, g_k, g_s = _attn_bwd(dproj, proj, dya, q_norm_g[l], k_norm_g[l], sinks[l], f"attn_bwd{l}")
        dproj, du1, g_pw2, st_b = _conf_bwd1(dproj, proj, dyb, dw_f[l], dw_b[l], ln_g[l], ln_b[l], pw2_f[l], pw2_b[l],
                                             f"conf_bwd_a{l}")
        dproj, g_dw = _conf_bwd2(dproj, proj, du1, dw_f[l], f"conf_bwd_b{l}")
        dproj, dqd, dkd, dvd, dgb, g_dn = _gdn_scan_bwd(dproj, qd, kd, vd, gb, gbt, ss, proj, dyc, dng,
                                                        f"gdn_scan_bwd{l}")
        dproj, dpre, st_c = _gdn_prep_bwd1(dproj, proj, dqd, dkd, dvd, dgb, dkv_a, sc_f[l], alv, dtv,
                                           f"gdn_prep_bwd_a{l}")
        dproj, g_sc = _gdn_prep_bwd2(dproj, proj, dpre, sc_f[l], f"gdn_prep_bwd_b{l}")
        dh = _mm(dproj, wp[l], tb=True, tm=min(512, T), tn=D, tk=1152, name=f"d_h{l}")
        g_wp = _mm(h, dproj, ta=True, tm=D, tn=1152, tk=min(512, T), name=f"d_w_in{l}")
        dout, st_n = _norm_bwd(dh, xs[l], dout, norm_g[l], scale[l], f"norm_bwd{l}")
        for i, g in enumerate((_unpad_cols(g_wp), g_pw2, g_wa, g_wb, g_wc, g_wo)):
            big_g[i][l] = g
        for name, g in (("b_ada", jnp.concatenate([st_n[0], st_n[1], g_gate[0]])), ("norm_g", st_n[2]),
                        ("q_norm_g", g_q[0]), ("k_norm_g", g_k[0]), ("sinks", g_s[0]), ("dw_b", st_b[3]),
                        ("ln_g", st_b[1]), ("ln_b", st_b[2]), ("pw2_b", st_b[0]), ("a_log", st_c[0, :4]),
                        ("dt_bias", st_c[1, :4]), ("dn_norm_g", g_dn[0]), ("dw_w", g_dw[:CONV_K]),
                        ("sconv_w", g_sc[:DN_K])):
            small[name][l] = g
    grad_x = dout[None]

    part = _pack_flat([jnp.stack(small[name]) for name, _ in _SMALL], _SMALL_ROWS, 128)
    (gpart,) = _all_gather([part], "gather_small_grads")
    dmod_all = gpart.reshape(N_DEV, -1)[:, :DEPTH * 3 * D].reshape(N_DEV, DEPTH, 3 * D)
    dmod_my = lax.dynamic_slice(dmod_all, (0, 0, me * ada_cols), (N_DEV, DEPTH, ada_cols)).transpose(1, 0, 2)
    g_w_ada = _ada_bwd(c_all, dmod_my, "ada_bwd")
    tot = _unpack_flat(_sum_parts(gpart, "sum_small_grads"), _SMALL_SHAPES)
    g_small = dict(zip([n for n, _ in _SMALL], tot))
    g_small["dw_w"] = lax.dynamic_slice(g_small["dw_w"], (0, 0, me * dw_cols), (DEPTH, CONV_K, dw_cols))
    g_small["sconv_w"] = lax.dynamic_slice(g_small["sconv_w"], (0, 0, me * sc_cols), (DEPTH, DN_K, sc_cols))
    env = dict(b_ada=(b_ada, m_b_ada, v_b_ada), norm_g=(norm_g, m_norm_g, v_norm_g),
               q_norm_g=(q_norm_g, m_q_norm_g, v_q_norm_g), k_norm_g=(k_norm_g, m_k_norm_g, v_k_norm_g),
               sinks=(sinks, m_sinks, v_sinks), dw_b=(dw_b, m_dw_b, v_dw_b), ln_g=(ln_g, m_ln_g, v_ln_g),
               ln_b=(ln_b, m_ln_b, v_ln_b), pw2_b=(pw2_b, m_pw2_b, v_pw2_b), a_log=(a_log, m_a_log, v_a_log),
               dt_bias=(dt_bias, m_dt_bias, v_dt_bias), dn_norm_g=(dn_norm_g, m_dn_norm_g, v_dn_norm_g),
               dw_w=(dw_w, m_dw_w, v_dw_w), sconv_w=(sconv_w, m_sconv_w, v_sconv_w))
    names = [n for n, _ in _SMALL]
    pk = lambda k: _pack_flat([env[n][k] for n in names], _UPD_ROWS, 128)
    upd = _adamw(pk(0), _pack_flat([g_small[n] for n in names], _UPD_ROWS, 128), pk(1), pk(2), "adamw_small")
    d_small, m_small, v_small = (dict(zip(names, _unpack_flat(u, _UPD_SHAPES))) for u in upd)

    flat2 = lambda a: a.reshape(-1, a.shape[-1])
    d_ada, nm_ada, nv_ada = (u.reshape(w_ada.shape) for u in
                             _adamw(flat2(w_ada), flat2(g_w_ada), flat2(m_w_ada), flat2(v_w_ada), "adamw_w_ada"))

    g_in, g_pw, g_a, g_b, g_c, g_o = (jnp.stack(g) for g in big_g)
    by_dest = [g_in.reshape(DEPTH, D, N_DEV, -1).transpose(2, 0, 1, 3),
               g_pw.reshape(DEPTH, N_DEV, -1, 512).transpose(1, 0, 2, 3)]
    by_dest += [g.reshape(DEPTH, 512, N_DEV, -1).transpose(2, 0, 1, 3) for g in (g_a, g_b, g_c)]
    by_dest.append(g_o.reshape(DEPTH, N_DEV, -1, D).transpose(1, 0, 2, 3))
    recv = _all_to_all(_pack_flat(by_dest, PACK_ROWS, PACK_COLS, lead=(N_DEV,)), "exchange_grads")
    res = _sum_adamw(recv, _pack_flat(big_w, PACK_ROWS, PACK_COLS), _pack_flat(big_m, PACK_ROWS, PACK_COLS),
                     _pack_flat(big_v, PACK_ROWS, PACK_COLS), "sum_adamw_big")
    g_big, d_big, m_big, v_big = (_unpack_flat(r, _BIG_SHAPES) for r in res)

    order = ("w_ada", "b_ada", "norm_g", "w_in", "q_norm_g", "k_norm_g", "sinks", "dw_w", "dw_b", "ln_g", "ln_b",
             "pw2_w", "pw2_b", "sconv_w", "a_log", "dt_bias", "dn_norm_g", "w_proj_a", "w_proj_b", "w_proj_c", "w_out")
    big_names = ("w_in", "pw2_w", "w_proj_a", "w_proj_b", "w_proj_c", "w_out")

    def pick(kind):
        src_small = (g_small, d_small, m_small, v_small)[kind]
        src_big = (g_big, d_big, m_big, v_big)[kind]
        src_ada = (g_w_ada, d_ada, nm_ada, nv_ada)[kind]
        return [src_ada if n == "w_ada" else src_big[big_names.index(n)] if n in big_names else src_small[n]
                for n in order]

    return (loss, grad_x, *pick(0), *pick(1), *pick(2), *pick(3))
```

```python
import functools
import math

import jax
import jax.numpy as jnp
import numpy as np
from jax import lax
from jax.experimental import pallas as pl
from jax.experimental.pallas import tpu as pltpu

F32 = jnp.float32
BF16 = jnp.bfloat16
HI = lax.Precision.HIGHEST

N_DEV = 8
D = 1024
DEPTH = 2
EPS = 1e-6
NEG_INF = -1e30
WINDOW = 128
ATT_HEADS = 8
ATT_HD = 64
CONV_K = 31
DN_HEADS = 4
DN_K = 4
CHUNK = 64
D_IN = 7944
VMEM_LIMIT = 56 * 1024 * 1024

C_MG, C_QA, C_ZA, C_ZB, C_QC, C_KC, C_VC, C_GV, C_GG, C_ZC, C_KA, C_VA, C_AB, NP = (
    0, 3072, 3584, 4096, 4608, 5120, 5632, 6144, 6656, 7168, 7680, 7808, 7936, 8064)
_PAD_FROM = ((4872, 7944), (0, 512), (768, 1280), (2304, 2816), (2816, 4352), (1280, 2304), (4360, 4872),
             (512, 768), (4352, 4360))

ALIBI = tuple(float(2.0 ** (-8.0 * (h + 1) / ATT_HEADS)) for h in range(ATT_HEADS))

ADAM_LR, ADAM_B1, ADAM_B2, ADAM_EPS, ADAM_WD, ADAM_STEP = 0.001, 0.9, 0.999, 1e-08, 0.01, 10


def _cparams(sem=None):
    return pltpu.CompilerParams(dimension_semantics=sem, vmem_limit_bytes=VMEM_LIMIT)


def _sig(x):
    return jax.nn.sigmoid(x)


def _silu(x):
    return x * _sig(x)


def _dsilu(x):
    s = _sig(x)
    return s * (1.0 + x * (1.0 - s))


def _dot(a, b, dims=((1,), (0,)), precision=None):
    return lax.dot_general(a, b, (dims, ((), ())), preferred_element_type=F32, precision=precision)


def _bdot(a, b, dims=((1,), (0,))):
    return _dot(a.astype(BF16), b.astype(BF16), dims)


NN, NT, TN = ((1,), (0,)), ((1,), (1,)), ((0,), (0,))


def _row(v):
    return v.reshape(1, -1)


def _mm(a, b, *, ta=False, tb=False, tm, tn, tk, name):
    M, K = (a.shape[1], a.shape[0]) if ta else a.shape
    N = b.shape[0] if tb else b.shape[1]
    assert M % tm == 0 and N % tn == 0 and K % tk == 0, (M, N, K, tm, tn, tk)
    nk = K // tk
    dims = ((0 if ta else 1,), (1 if tb else 0,))

    def body(a_ref, b_ref, o_ref, acc_ref):
        k = pl.program_id(2)

        @pl.when(k == 0)
        def _():
            acc_ref[...] = jnp.zeros_like(acc_ref)

        acc_ref[...] += _bdot(a_ref[...], b_ref[...], dims)

        @pl.when(k == nk - 1)
        def _():
            o_ref[...] = acc_ref[...]

    a_spec = pl.BlockSpec((tk, tm), lambda i, j, k: (k, i)) if ta else pl.BlockSpec((tm, tk), lambda i, j, k: (i, k))
    b_spec = pl.BlockSpec((tn, tk), lambda i, j, k: (j, k)) if tb else pl.BlockSpec((tk, tn), lambda i, j, k: (k, j))
    return pl.pallas_call(
        body, name=name, grid=(M // tm, N // tn, nk),
        in_specs=[a_spec, b_spec], out_specs=pl.BlockSpec((tm, tn), lambda i, j, k: (i, j)),
        out_shape=jax.ShapeDtypeStruct((M, N), F32),
        scratch_shapes=[pltpu.VMEM((tm, tn), F32)],
        compiler_params=_cparams(("parallel", "parallel", "arbitrary")),
    )(a, b)


def _norm_fwd(x, norm_g, scale, shift, name):
    T = x.shape[0]
    tm = min(512, T)

    def body(x_ref, g_ref, sc_ref, sh_ref, h_ref):
        xv = x_ref[...]
        r = lax.rsqrt(jnp.mean(xv * xv, axis=-1, keepdims=True) + EPS)
        h_ref[...] = ((xv * r) * g_ref[...] * (1.0 + sc_ref[...]) + sh_ref[...]).astype(BF16)

    vec = pl.BlockSpec((1, D), lambda i: (0, 0))
    return pl.pallas_call(
        body, name=name, grid=(T // tm,),
        in_specs=[pl.BlockSpec((tm, D), lambda i: (i, 0)), vec, vec, vec],
        out_specs=pl.BlockSpec((tm, D), lambda i: (i, 0)),
        out_shape=jax.ShapeDtypeStruct((T, D), BF16),
        compiler_params=_cparams(("parallel",)),
    )(x, _row(norm_g), _row(scale), _row(shift))


def _norm_bwd(dh, x, dres, norm_g, scale, name):
    T = x.shape[0]
    tm = min(512, T)

    def body(dh_ref, x_ref, dr_ref, g_ref, sc_ref, dx_ref, st_ref):
        i = pl.program_id(0)
        xv, dhv = x_ref[...], dh_ref[...]
        r = lax.rsqrt(jnp.mean(xv * xv, axis=-1, keepdims=True) + EPS)
        xh = xv * r
        g, s1 = g_ref[...], 1.0 + sc_ref[...]
        dxh = dhv * (g * s1)
        dx_ref[...] = dr_ref[...] + r * (dxh - xh * jnp.mean(dxh * xh, axis=-1, keepdims=True))
        dhx = dhv * xh
        upd = jnp.concatenate([jnp.sum(dhv, axis=0, keepdims=True), jnp.sum(dhx * g, axis=0, keepdims=True),
                               jnp.sum(dhx * s1, axis=0, keepdims=True), jnp.zeros((5, D), F32)], axis=0)

        @pl.when(i == 0)
        def _():
            st_ref[...] = upd

        @pl.when(i > 0)
        def _():
            st_ref[...] += upd

    vec = pl.BlockSpec((1, D), lambda i: (0, 0))
    blk = pl.BlockSpec((tm, D), lambda i: (i, 0))
    return pl.pallas_call(
        body, name=name, grid=(T // tm,),
        in_specs=[blk, blk, blk, vec, vec],
        out_specs=[blk, pl.BlockSpec((8, D), lambda i: (0, 0))],
        out_shape=[jax.ShapeDtypeStruct((T, D), F32), jax.ShapeDtypeStruct((8, D), F32)],
        compiler_params=_cparams(("arbitrary",)),
    )(dh, x, dres, _row(norm_g), _row(scale))


def _rms(x, g):
    r = lax.rsqrt(jnp.mean(x * x, axis=-1, keepdims=True) + EPS)
    return x * r, r


def _attn_mask(n):
    qi = lax.broadcasted_iota(jnp.int32, (WINDOW, 2 * WINDOW), 0)
    kj = lax.broadcasted_iota(jnp.int32, (WINDOW, 2 * WINDOW), 1)
    dist = qi + WINDOW - kj
    valid = (dist >= 0) & (dist < WINDOW) & ((n > 0) | (kj >= WINDOW))
    return valid, dist.astype(F32)


def _attn_probs(qn, kn, h, sink, valid, distf):
    s = _bdot(qn, kn, NT) - ALIBI[h] * distf
    s = jnp.where(valid, s, NEG_INF)
    m = jnp.maximum(jnp.max(s, axis=-1, keepdims=True), sink)
    p = jnp.exp(s - m)
    es = jnp.exp(sink - m)
    den = jnp.sum(p, axis=-1, keepdims=True) + es
    return p / den, es / den


def _attn_fwd(proj, q_norm_g, k_norm_g, sinks, name):
    T = proj.shape[0]
    nb = T // WINDOW

    def body(sink_ref, q_ref, z_ref, kc_ref, kp_ref, vc_ref, vp_ref, qg_ref, kg_ref, o_ref):
        n = pl.program_id(0)
        valid, distf = _attn_mask(n)
        k2 = jnp.concatenate([kp_ref[...], kc_ref[...]], axis=0)
        v2 = jnp.concatenate([vp_ref[...], vc_ref[...]], axis=0)
        for g in range(2):
            kn = _rms(k2[:, 64 * g:64 * g + 64], None)[0] * kg_ref[...]
            vg = v2[:, 64 * g:64 * g + 64]
            for j in range(4):
                h = 4 * g + j
                sl = slice(64 * h, 64 * h + 64)
                qn = (_rms(q_ref[:, sl], None)[0] * qg_ref[...]) * (ATT_HD ** -0.5)
                p, _ = _attn_probs(qn, kn, h, sink_ref[h], valid, distf)
                o_ref[:, sl] = _bdot(p, vg) * _silu(z_ref[:, sl])

    prev = lambda n: jnp.maximum(n - 1, 0)
    return pl.pallas_call(
        body, name=name, grid=(nb,),
        in_specs=[pl.BlockSpec(memory_space=pltpu.SMEM),
                  pl.BlockSpec((WINDOW, 512), lambda n: (n, C_QA // 512)),
                  pl.BlockSpec((WINDOW, 512), lambda n: (n, C_ZA // 512)),
                  pl.BlockSpec((WINDOW, 128), lambda n: (n, C_KA // 128)),
                  pl.BlockSpec((WINDOW, 128), lambda n: (prev(n), C_KA // 128)),
                  pl.BlockSpec((WINDOW, 128), lambda n: (n, C_VA // 128)),
                  pl.BlockSpec((WINDOW, 128), lambda n: (prev(n), C_VA // 128)),
                  pl.BlockSpec((1, 64), lambda n: (0, 0)), pl.BlockSpec((1, 64), lambda n: (0, 0))],
        out_specs=pl.BlockSpec((WINDOW, 512), lambda n: (n, 0)),
        out_shape=jax.ShapeDtypeStruct((T, 512), F32),
        compiler_params=_cparams(("parallel",)),
    )(sinks, proj, proj, proj, proj, proj, proj, _row(q_norm_g), _row(k_norm_g))


def _rms_bwd(dy, xh, r, g):
    dxh = dy * g
    return r * (dxh - xh * jnp.mean(dxh * xh, axis=-1, keepdims=True)), dy * xh


def _attn_bwd(dproj, proj, dya, q_norm_g, k_norm_g, sinks, name):
    T = proj.shape[0]
    nb = T // WINDOW

    def body(sink_ref, dp_any, q_ref, z_ref, kc_ref, kp_ref, vc_ref, vp_ref, dy_ref, qg_ref, kg_ref,
             dqz_ref, dkv_ref, gq_ref, gk_ref, gs_ref, ck_ref, cv_ref):
        n = pl.program_id(0)

        @pl.when(n == 0)
        def _():
            gq_ref[...] = jnp.zeros_like(gq_ref)
            gk_ref[...] = jnp.zeros_like(gk_ref)
            gs_ref[...] = jnp.zeros_like(gs_ref)
            ck_ref[...] = jnp.zeros_like(ck_ref)
            cv_ref[...] = jnp.zeros_like(cv_ref)

        lane8 = lax.broadcasted_iota(jnp.int32, (1, 8), 1)

        @pl.when(n < nb)
        def _():
            valid, distf = _attn_mask(n)
            k2 = jnp.concatenate([kp_ref[...], kc_ref[...]], axis=0)
            v2 = jnp.concatenate([vp_ref[...], vc_ref[...]], axis=0)
            gq_acc = jnp.zeros((1, 64), F32)
            gs_acc = jnp.zeros((1, 8), F32)
            for g in range(2):
                kn = _rms(k2[:, 64 * g:64 * g + 64], None)[0] * kg_ref[...]
                vg = v2[:, 64 * g:64 * g + 64]
                dkn = jnp.zeros((2 * WINDOW, 64), F32)
                dvg = jnp.zeros((2 * WINDOW, 64), F32)
                for j in range(4):
                    h = 4 * g + j
                    sl = slice(64 * h, 64 * h + 64)
                    qh, qr = _rms(q_ref[:, sl], None)
                    qn = (qh * qg_ref[...]) * (ATT_HD ** -0.5)
                    p, ps = _attn_probs(qn, kn, h, sink_ref[h], valid, distf)
                    o = _bdot(p, vg)
                    zh = z_ref[:, sl]
                    dy = dy_ref[:, sl]
                    dqz_ref[:, 512 + 64 * h:512 + 64 * h + 64] = dy * o * _dsilu(zh)
                    do = dy * _silu(zh)
                    delta = jnp.sum(do * o, axis=-1, keepdims=True)
                    dpm = _bdot(do, vg, NT)
                    ds = p * (dpm - delta)
                    gs_acc = gs_acc + jnp.where(lane8 == h, -jnp.sum(ps * delta, axis=0, keepdims=True), 0.0)
                    dvg = dvg + _bdot(p, do, TN)
                    dkn = dkn + _bdot(ds, qn, TN)
                    dqn = _bdot(ds, kn) * (ATT_HD ** -0.5)
                    dq, gq = _rms_bwd(dqn, qh, qr, qg_ref[...])
                    dqz_ref[:, sl] = dq
                    gq_acc = gq_acc + jnp.sum(gq, axis=0, keepdims=True)
                ksl = slice(64 * g, 64 * g + 64)
                vsl = slice(128 + 64 * g, 128 + 64 * g + 64)
                dkv_ref[:, ksl] = ck_ref[:, ksl] + dkn[:WINDOW]
                dkv_ref[:, vsl] = cv_ref[:, ksl] + dvg[:WINDOW]
                ck_ref[:, ksl] = dkn[WINDOW:]
                cv_ref[:, ksl] = dvg[WINDOW:]
            gq_ref[...] += gq_acc
            gs_ref[...] += gs_acc

        @pl.when(n == nb)
        def _():
            dkv_ref[:, :128] = ck_ref[...]
            dkv_ref[:, 128:] = cv_ref[...]

        @pl.when(n > 0)
        def _():
            gk_acc = jnp.zeros((1, 64), F32)
            for g in range(2):
                ksl = slice(64 * g, 64 * g + 64)
                kh, kr = _rms(kp_ref[:, ksl], None)
                dk, gk = _rms_bwd(dkv_ref[:, ksl], kh, kr, kg_ref[...])
                dkv_ref[:, ksl] = dk
                gk_acc = gk_acc + jnp.sum(gk, axis=0, keepdims=True)
            gk_ref[...] += gk_acc

    cur = lambda n: jnp.minimum(n, nb - 1)
    prev = lambda n: jnp.maximum(n - 1, 0)
    small = lambda w: pl.BlockSpec((1, w), lambda n: (0, 0))
    return pl.pallas_call(
        body, name=name, grid=(nb + 1,),
        in_specs=[pl.BlockSpec(memory_space=pltpu.SMEM), pl.BlockSpec(memory_space=pl.ANY),
                  pl.BlockSpec((WINDOW, 512), lambda n: (cur(n), C_QA // 512)),
                  pl.BlockSpec((WINDOW, 512), lambda n: (cur(n), C_ZA // 512)),
                  pl.BlockSpec((WINDOW, 128), lambda n: (cur(n), C_KA // 128)),
                  pl.BlockSpec((WINDOW, 128), lambda n: (prev(n), C_KA // 128)),
                  pl.BlockSpec((WINDOW, 128), lambda n: (cur(n), C_VA // 128)),
                  pl.BlockSpec((WINDOW, 128), lambda n: (prev(n), C_VA // 128)),
                  pl.BlockSpec((WINDOW, 512), lambda n: (cur(n), 0)),
                  small(64), small(64)],
        out_specs=[pl.BlockSpec((WINDOW, 1024), lambda n: (cur(n), C_QA // 1024)),
                   pl.BlockSpec((WINDOW, 256), lambda n: (prev(n), 0)),
                   small(64), small(64), small(8)],
        out_shape=[jax.ShapeDtypeStruct(dproj.shape, F32), jax.ShapeDtypeStruct((T, 256), F32),
                   jax.ShapeDtypeStruct((1, 64), F32), jax.ShapeDtypeStruct((1, 64), F32),
                   jax.ShapeDtypeStruct((1, 8), F32)],
        scratch_shapes=[pltpu.VMEM((WINDOW, 128), F32), pltpu.VMEM((WINDOW, 128), F32)],
        input_output_aliases={1: 0},
        compiler_params=_cparams(("arbitrary",)),
    )(sinks, dproj, proj, proj, proj, proj, proj, proj, dya, _row(q_norm_g), _row(k_norm_g))


HALO_B = 32


def _conf_specs(T, tm):
    r = tm // HALO_B
    cur = lambda c: pl.BlockSpec((tm, 512), lambda i: (i, c // 512))
    prev = lambda c: pl.BlockSpec((HALO_B, 512), lambda i: (jnp.maximum(i * r - 1, 0), c // 512))
    return cur, prev


def _conf_core(i, tm, gv_ref, gg_ref, gvp_ref, ggp_ref, w_ref, b_ref, lg_ref, lb_ref, pw_ref, pb_ref, ext_ref):
    up = gvp_ref[...] * _sig(ggp_ref[...])
    ext_ref[:HALO_B] = jnp.where(i > 0, up, 0.0)
    ext_ref[HALO_B:] = gv_ref[...] * _sig(gg_ref[...])
    acc = jnp.zeros((tm, 512), F32) + b_ref[...]
    for k in range(CONV_K):
        acc = acc + w_ref[k:k + 1, :] * ext_ref[pl.ds(HALO_B - CONV_K + 1 + k, tm), :]
    mu = jnp.mean(acc, axis=-1, keepdims=True)
    xc = acc - mu
    rstd = lax.rsqrt(jnp.mean(xc * xc, axis=-1, keepdims=True) + EPS)
    xh = xc * rstd
    u2 = xh * lg_ref[...] + lb_ref[...]
    u3 = _silu(u2)
    ypre = _bdot(u3, pw_ref[...]) + pb_ref[...]
    return xh, rstd, u2, u3, ypre


def _conf_fwd(proj, dw_w, dw_b, ln_g, ln_b, pw2, pw2_b, name):
    T = proj.shape[0]
    tm = min(512, T)
    cur, prev = _conf_specs(T, tm)

    def body(gv_ref, gg_ref, gvp_ref, ggp_ref, zb_ref, w_ref, b_ref, lg_ref, lb_ref, pw_ref, pb_ref, o_ref, ext_ref):
        i = pl.program_id(0)
        ypre = _conf_core(i, tm, gv_ref, gg_ref, gvp_ref, ggp_ref, w_ref, b_ref, lg_ref, lb_ref, pw_ref, pb_ref,
                          ext_ref)[4]
        o_ref[...] = ypre * _silu(zb_ref[...])

    full = lambda s: pl.BlockSpec(s, lambda i: (0, 0))
    return pl.pallas_call(
        body, name=name, grid=(T // tm,),
        in_specs=[cur(C_GV), cur(C_GG), prev(C_GV), prev(C_GG), cur(C_ZB), full((CONV_K, 512)), full((1, 512)),
                  full((1, 512)), full((1, 512)), full((512, 512)), full((1, 512))],
        out_specs=pl.BlockSpec((tm, 512), lambda i: (i, 0)),
        out_shape=jax.ShapeDtypeStruct((T, 512), F32),
        scratch_shapes=[pltpu.VMEM((tm + HALO_B, 512), F32)],
        compiler_params=_cparams(("parallel",)),
    )(proj, proj, proj, proj, proj, dw_w, _row(dw_b), _row(ln_g), _row(ln_b), pw2, _row(pw2_b))


def _conf_bwd1(dproj, proj, dyb, dw_w, dw_b, ln_g, ln_b, pw2, pw2_b, name):
    T = proj.shape[0]
    tm = min(512, T)
    cur, prev = _conf_specs(T, tm)

    def body(dp_any, gv_ref, gg_ref, gvp_ref, ggp_ref, zb_ref, dy_ref, w_ref, b_ref, lg_ref, lb_ref, pw_ref, pb_ref,
             dzb_ref, du1_ref, gpw_ref, st_ref, ext_ref):
        i = pl.program_id(0)
        xh, rstd, u2, u3, ypre = _conf_core(i, tm, gv_ref, gg_ref, gvp_ref, ggp_ref, w_ref, b_ref, lg_ref, lb_ref,
                                            pw_ref, pb_ref, ext_ref)
        zb, dy = zb_ref[...], dy_ref[...]
        dzb_ref[...] = dy * ypre * _dsilu(zb)
        dyp = dy * _silu(zb)
        du2 = _bdot(dyp, pw_ref[...], NT) * _dsilu(u2)
        dxh = du2 * lg_ref[...]
        du1 = rstd * (dxh - jnp.mean(dxh, axis=-1, keepdims=True) - xh * jnp.mean(dxh * xh, axis=-1, keepdims=True))
        du1_ref[...] = du1
        gpw = _bdot(u3, dyp, TN)
        rs = lambda a: jnp.sum(a, axis=0, keepdims=True)
        upd = jnp.concatenate([rs(dyp), rs(du2 * xh), rs(du2), rs(du1), jnp.zeros((4, 512), F32)], axis=0)

        @pl.when(i == 0)
        def _():
            gpw_ref[...] = gpw
            st_ref[...] = upd

        @pl.when(i > 0)
        def _():
            gpw_ref[...] += gpw
            st_ref[...] += upd

    full = lambda s: pl.BlockSpec(s, lambda i: (0, 0))
    blk = pl.BlockSpec((tm, 512), lambda i: (i, 0))
    return pl.pallas_call(
        body, name=name, grid=(T // tm,),
        in_specs=[pl.BlockSpec(memory_space=pl.ANY), cur(C_GV), cur(C_GG), prev(C_GV), prev(C_GG), cur(C_ZB), blk,
                  full((CONV_K, 512)), full((1, 512)), full((1, 512)), full((1, 512)), full((512, 512)), full((1, 512))],
        out_specs=[cur(C_ZB), blk, full((512, 512)), full((8, 512))],
        out_shape=[jax.ShapeDtypeStruct(dproj.shape, F32), jax.ShapeDtypeStruct((T, 512), F32),
                   jax.ShapeDtypeStruct((512, 512), F32), jax.ShapeDtypeStruct((8, 512), F32)],
        scratch_shapes=[pltpu.VMEM((tm + HALO_B, 512), F32)],
        input_output_aliases={0: 0},
        compiler_params=_cparams(("arbitrary",)),
    )(dproj, proj, proj, proj, proj, proj, dyb, dw_w, _row(dw_b), _row(ln_g), _row(ln_b), pw2, _row(pw2_b))


def _conf_bwd2(dproj, proj, du1, dw_w, name):
    T = proj.shape[0]
    tm = min(512, T)
    nt = T // tm
    r = tm // HALO_B
    cur, prev = _conf_specs(T, tm)

    def body(dp_any, gv_ref, gg_ref, gvp_ref, ggp_ref, du_ref, dun_ref, w_ref, dglu_ref, gw_ref, extu_ref, extd_ref):
        i = pl.program_id(0)
        gv, sg = gv_ref[...], _sig(gg_ref[...])
        extu_ref[:HALO_B] = jnp.where(i > 0, gvp_ref[...] * _sig(ggp_ref[...]), 0.0)
        extu_ref[HALO_B:] = gv * sg
        du1 = du_ref[...]
        extd_ref[:tm] = du1
        extd_ref[tm:] = jnp.where(i < nt - 1, dun_ref[...], 0.0)
        du0 = jnp.zeros((tm, 512), F32)
        rows = []
        for k in range(CONV_K):
            du0 = du0 + w_ref[k:k + 1, :] * extd_ref[pl.ds(CONV_K - 1 - k, tm), :]
            rows.append(jnp.sum(du1 * extu_ref[pl.ds(HALO_B - CONV_K + 1 + k, tm), :], axis=0, keepdims=True))
        rows.append(jnp.zeros((1, 512), F32))
        gw = jnp.concatenate(rows, axis=0)
        dglu_ref[:, :512] = du0 * sg
        dglu_ref[:, 512:] = du0 * gv * sg * (1.0 - sg)

        @pl.when(i == 0)
        def _():
            gw_ref[...] = gw

        @pl.when(i > 0)
        def _():
            gw_ref[...] += gw

    full = lambda s: pl.BlockSpec(s, lambda i: (0, 0))
    return pl.pallas_call(
        body, name=name, grid=(nt,),
        in_specs=[pl.BlockSpec(memory_space=pl.ANY), cur(C_GV), cur(C_GG), prev(C_GV), prev(C_GG),
                  pl.BlockSpec((tm, 512), lambda i: (i, 0)),
                  pl.BlockSpec((HALO_B, 512), lambda i: (jnp.minimum((i + 1) * r, T // HALO_B - 1), 0)),
                  full((CONV_K, 512))],
        out_specs=[pl.BlockSpec((tm, 1024), lambda i: (i, C_GV // 1024)), full((32, 512))],
        out_shape=[jax.ShapeDtypeStruct(dproj.shape, F32), jax.ShapeDtypeStruct((32, 512), F32)],
        scratch_shapes=[pltpu.VMEM((tm + HALO_B, 512), F32), pltpu.VMEM((tm + HALO_B, 512), F32)],
        input_output_aliases={0: 0},
        compiler_params=_cparams(("arbitrary",)),
    )(dproj, proj, proj, proj, proj, du1, du1, dw_w)


HALO_C = 8
QKV_C = 1536


def _softplus(x):
    return jnp.maximum(x, 0.0) + jnp.log1p(jnp.exp(-jnp.abs(x)))


def _gdn_conv(i, tm, x_ref, xp_ref, w_ref, ext_ref):
    ext_ref[:HALO_C] = jnp.where(i > 0, xp_ref[...], 0.0)
    ext_ref[HALO_C:] = x_ref[...]
    pre = jnp.zeros((tm, QKV_C), F32)
    for k in range(DN_K):
        pre = pre + w_ref[k:k + 1, :] * ext_ref[pl.ds(HALO_C - DN_K + 1 + k, tm), :]
    return pre


def _gdn_specs(T, tm):
    r = tm // HALO_C
    cur = pl.BlockSpec((tm, QKV_C), lambda i: (i, C_QC // QKV_C))
    prev = pl.BlockSpec((HALO_C, QKV_C), lambda i: (jnp.maximum(i * r - 1, 0), C_QC // QKV_C))
    ab = pl.BlockSpec((tm, 128), lambda i: (i, C_AB // 128))
    return cur, prev, ab


def _gdn_prep_fwd(proj, sconv_w, alog_v, dtb_v, name):
    T = proj.shape[0]
    tm = min(512, T)
    cur, prev, ab = _gdn_specs(T, tm)

    def body(x_ref, xp_ref, ab_ref, w_ref, al_ref, dt_ref, q_ref, k_ref, v_ref, gb_ref, ext_ref):
        i = pl.program_id(0)
        y = _silu(_gdn_conv(i, tm, x_ref, xp_ref, w_ref, ext_ref))
        for h in range(DN_HEADS):
            sl = slice(128 * h, 128 * h + 128)
            qh, kh = y[:, sl], y[:, 512 + 128 * h:512 + 128 * h + 128]
            q_ref[:, sl] = qh * lax.rsqrt(jnp.sum(qh * qh, axis=-1, keepdims=True) + EPS) * (128 ** -0.5)
            k_ref[:, sl] = kh * lax.rsqrt(jnp.sum(kh * kh, axis=-1, keepdims=True) + EPS)
        v_ref[...] = y[:, 1024:]
        abv = ab_ref[...]
        lane = lax.broadcasted_iota(jnp.int32, (tm, 128), 1)
        g = -jnp.exp(al_ref[...]) * _softplus(abv + dt_ref[...])
        gb_ref[...] = jnp.where(lane < DN_HEADS, g, _sig(abv))

    full = lambda s: pl.BlockSpec(s, lambda i: (0, 0))
    blk = pl.BlockSpec((tm, 512), lambda i: (i, 0))
    return pl.pallas_call(
        body, name=name, grid=(T // tm,),
        in_specs=[cur, prev, ab, full((DN_K, QKV_C)), full((1, 128)), full((1, 128))],
        out_specs=[blk, blk, blk, pl.BlockSpec((tm, 128), lambda i: (i, 0))],
        out_shape=[jax.ShapeDtypeStruct((T, 512), F32)] * 3 + [jax.ShapeDtypeStruct((T, 128), F32)],
        scratch_shapes=[pltpu.VMEM((tm + HALO_C, QKV_C), F32)],
        compiler_params=_cparams(("parallel",)),
    )(proj, proj, proj, sconv_w, alog_v, dtb_v)


def _hdot(a, b, dims=NN):
    return _dot(a, b, dims, precision=HI)


def _tri_inv(a, eye):
    n = -a
    t = eye + n
    p = n
    for _ in range(5):
        p = _hdot(p, p)
        t = t + _hdot(t, p)
    return t


def _tri_consts():
    ii = lax.broadcasted_iota(jnp.int32, (CHUNK, CHUNK), 0)
    jj = lax.broadcasted_iota(jnp.int32, (CHUNK, CHUNK), 1)
    return ii >= jj, ii > jj, (ii == jj).astype(F32)


def _gdn_chunk(q, k, v, gcol, grow, bcol, s, lower, strict, eye):
    dm = jnp.where(lower, jnp.exp(jnp.where(lower, gcol - grow, 0.0)), 0.0)
    kb = k * bcol
    a = jnp.where(strict, _bdot(kb, k, NT) * dm, 0.0)
    tm = _tri_inv(a, eye)
    gc = jnp.exp(gcol)
    vb = v * bcol
    kbg = kb * gc
    u = _bdot(tm, vb)
    w = _bdot(tm, kbg)
    p = _bdot(q, k, NT) * dm
    qe = q * gc
    glast = grow[:, CHUNK - 1:CHUNK]
    ke = k * jnp.exp(glast - gcol)
    vn = u - _bdot(w, s)
    o = _bdot(qe, s) + _bdot(p, vn)
    s_new = s * jnp.exp(glast) + _bdot(ke, vn, TN)
    return dict(q=q, k=k, v=v, bcol=bcol, gcol=gcol, glast=glast, dm=dm, kb=kb, a=a, tm=tm, gc=gc, vb=vb, kbg=kbg,
                w=w, p=p, qe=qe, ke=ke, vn=vn, o=o, s=s, s_new=s_new)


def _gdn_chunk_bwd(c, do, ds_new, lower, strict, ones):
    rs = lambda m: jnp.sum(m, axis=-1, keepdims=True)
    colsum = lambda m: _hdot(m, ones, TN)[:, :1]
    q, k, v, bcol, dm, tm, gc, s = c["q"], c["k"], c["v"], c["bcol"], c["dm"], c["tm"], c["gc"], c["s"]
    eg = jnp.exp(c["glast"])
    dvn = _bdot(c["p"], do, TN) + _bdot(c["ke"], ds_new)
    dqe = _bdot(do, s, NT)
    dp = jnp.where(lower, _bdot(do, c["vn"], NT), 0.0)
    ds = _bdot(c["qe"], do, TN) + eg * ds_new - _bdot(c["w"], dvn, TN)
    dw = -_bdot(dvn, s, NT)
    dke = _bdot(c["vn"], ds_new, NT)
    dglast = jnp.sum(rs(ds_new * s), axis=0, keepdims=True) * eg
    dk = dke * jnp.exp(c["glast"] - c["gcol"])
    r_ke = rs(dke * c["ke"])
    dglast = dglast + jnp.sum(r_ke, axis=0, keepdims=True)
    dgam = rs(dqe * c["qe"]) - r_ke
    dq = dqe * gc
    dpm = dp * dm
    dq = dq + _bdot(dpm, k)
    dk = dk + _bdot(dpm, q, TN)
    mp = dp * c["p"]
    dgam = dgam + rs(mp) - colsum(mp)
    dt = _bdot(dvn, c["vb"], NT) + _bdot(dw, c["kbg"], NT)
    dvb = _bdot(tm, dvn, TN)
    dkbg = _bdot(tm, dw, TN)
    dkb = dkbg * gc
    dgam = dgam + rs(dkbg * c["kbg"])
    da = jnp.where(strict, -_bdot(_bdot(tm, dt, TN), tm, NT), 0.0)
    dam = da * dm
    dkb = dkb + _bdot(dam, k)
    dk = dk + _bdot(dam, c["kb"], TN)
    ma = da * c["a"]
    dgam = dgam + rs(ma) - colsum(ma)
    dk = dk + dkb * bcol
    dbeta = rs(dkb * k) + rs(dvb * v)
    dv = dvb * bcol
    row = lax.broadcasted_iota(jnp.int32, (CHUNK, 1), 0)
    dgam = dgam + jnp.where(row == CHUNK - 1, dglast, 0.0)
    dg = _hdot(lower.astype(F32), dgam, TN)
    return dq, dk, dv, dg, dbeta, ds


def _gdn_scan_fwd(qd, kd, vd, gb, gbt, proj, dn_g, name):
    T = qd.shape[0]
    nc = T // CHUNK

    def body(q_ref, k_ref, v_ref, gb_ref, gt_ref, z_ref, ng_ref, y_ref, ss_ref, s_ref):
        n = pl.program_id(0)

        @pl.when(n == 0)
        def _():
            s_ref[...] = jnp.zeros_like(s_ref)

        lower, strict, eye = _tri_consts()
        lmat = lower.astype(F32)
        gcs = _hdot(lmat, gb_ref[...])
        grs = _hdot(gt_ref[0], lmat, NT)
        ss_ref[0] = s_ref[...]
        for h in range(DN_HEADS):
            sl = slice(128 * h, 128 * h + 128)
            c = _gdn_chunk(q_ref[:, sl], k_ref[:, sl], v_ref[:, sl], gcs[:, h:h + 1], grs[h:h + 1, :],
                           gb_ref[:, DN_HEADS + h:DN_HEADS + h + 1], s_ref[h], lower, strict, eye)
            s_ref[h] = c["s_new"]
            y_ref[:, sl] = _rms(c["o"], None)[0] * ng_ref[...] * _silu(z_ref[:, sl])

    blk = pl.BlockSpec((CHUNK, 512), lambda n: (n, 0))
    return pl.pallas_call(
        body, name=name, grid=(nc,),
        in_specs=[blk, blk, blk, pl.BlockSpec((CHUNK, 128), lambda n: (n, 0)),
                  pl.BlockSpec((1, 8, CHUNK), lambda n: (n, 0, 0)),
                  pl.BlockSpec((CHUNK, 512), lambda n: (n, C_ZC // 512)), pl.BlockSpec((1, 128), lambda n: (0, 0))],
        out_specs=[blk, pl.BlockSpec((1, DN_HEADS, 128, 128), lambda n: (n, 0, 0, 0))],
        out_shape=[jax.ShapeDtypeStruct((T, 512), F32), jax.ShapeDtypeStruct((nc, DN_HEADS, 128, 128), F32)],
        scratch_shapes=[pltpu.VMEM((DN_HEADS, 128, 128), F32)],
        compiler_params=_cparams(("arbitrary",)),
    )(qd, kd, vd, gb, gbt, proj, dn_g)


def _gdn_scan_bwd(dproj, qd, kd, vd, gb, gbt, ssave, proj, dyc, dn_g, name):
    T = qd.shape[0]
    nc = T // CHUNK
    rev = lambda n: nc - 1 - n

    def body(dp_any, q_ref, k_ref, v_ref, gb_ref, gt_ref, ss_ref, z_ref, dy_ref, ng_ref,
             dz_ref, dq_ref, dk_ref, dv_ref, dgb_ref, gng_ref, ds_ref):
        n = pl.program_id(0)

        @pl.when(n == 0)
        def _():
            ds_ref[...] = jnp.zeros_like(ds_ref)
            gng_ref[...] = jnp.zeros_like(gng_ref)

        lower, strict, eye = _tri_consts()
        lmat = lower.astype(F32)
        ones = jnp.ones((CHUNK, 128), F32)
        gcs = _hdot(lmat, gb_ref[...])
        grs = _hdot(gt_ref[0], lmat, NT)
        lane = lax.broadcasted_iota(jnp.int32, (CHUNK, 128), 1)
        dgb = jnp.zeros((CHUNK, 128), F32)
        gng = jnp.zeros((1, 128), F32)
        for h in range(DN_HEADS):
            sl = slice(128 * h, 128 * h + 128)
            c = _gdn_chunk(q_ref[:, sl], k_ref[:, sl], v_ref[:, sl], gcs[:, h:h + 1], grs[h:h + 1, :],
                           gb_ref[:, DN_HEADS + h:DN_HEADS + h + 1], ss_ref[0, h], lower, strict, eye)
            oh, r = _rms(c["o"], None)
            z, dy = z_ref[:, sl], dy_ref[:, sl]
            dz_ref[:, sl] = dy * (oh * ng_ref[...]) * _dsilu(z)
            do, gg = _rms_bwd(dy * _silu(z), oh, r, ng_ref[...])
            gng = gng + jnp.sum(gg, axis=0, keepdims=True)
            dq, dk, dv, dg, dbeta, ds = _gdn_chunk_bwd(c, do, ds_ref[h], lower, strict, ones)
            ds_ref[h] = ds
            dq_ref[:, sl], dk_ref[:, sl], dv_ref[:, sl] = dq, dk, dv
            dgb = dgb + jnp.where(lane == h, dg, 0.0) + jnp.where(lane == DN_HEADS + h, dbeta, 0.0)
        dgb_ref[...] = dgb
        gng_ref[...] += gng

    blk = pl.BlockSpec((CHUNK, 512), lambda n: (rev(n), 0))
    nar = pl.BlockSpec((CHUNK, 128), lambda n: (rev(n), 0))
    return pl.pallas_call(
        body, name=name, grid=(nc,),
        in_specs=[pl.BlockSpec(memory_space=pl.ANY), blk, blk, blk, nar,
                  pl.BlockSpec((1, 8, CHUNK), lambda n: (rev(n), 0, 0)),
                  pl.BlockSpec((1, DN_HEADS, 128, 128), lambda n: (rev(n), 0, 0, 0)),
                  pl.BlockSpec((CHUNK, 512), lambda n: (rev(n), C_ZC // 512)), blk,
                  pl.BlockSpec((1, 128), lambda n: (0, 0))],
        out_specs=[pl.BlockSpec((CHUNK, 512), lambda n: (rev(n), C_ZC // 512)), blk, blk, blk, nar,
                   pl.BlockSpec((1, 128), lambda n: (0, 0))],
        out_shape=[jax.ShapeDtypeStruct(dproj.shape, F32)] + [jax.ShapeDtypeStruct((T, 512), F32)] * 3
        + [jax.ShapeDtypeStruct((T, 128), F32), jax.ShapeDtypeStruct((1, 128), F32)],
        scratch_shapes=[pltpu.VMEM((DN_HEADS, 128, 128), F32)],
        input_output_aliases={0: 0},
        compiler_params=_cparams(("arbitrary",)),
    )(dproj, qd, kd, vd, gb, gbt, ssave, proj, dyc, dn_g)


def _gdn_prep_bwd1(dproj, proj, dqd, dkd, dvd, dgb, dkv_a, sconv_w, alog_v, dtb_v, name):
    T = proj.shape[0]
    tm = min(512, T)
    cur, prev, ab = _gdn_specs(T, tm)

    def body(dp_any, x_ref, xp_ref, ab_ref, dq_ref, dk_ref, dv_ref, dgb_ref, dkv_ref, w_ref, al_ref, dt_ref,
             o_ref, dpre_ref, st_ref, ext_ref):
        i = pl.program_id(0)
        pre = _gdn_conv(i, tm, x_ref, xp_ref, w_ref, ext_ref)
        y, dsl = _silu(pre), _dsilu(pre)
        for h in range(DN_HEADS):
            for base, g_ref, scale in ((0, dq_ref, 128 ** -0.5), (512, dk_ref, 1.0)):
                sl = slice(base + 128 * h, base + 128 * h + 128)
                xh = y[:, sl]
                r = lax.rsqrt(jnp.sum(xh * xh, axis=-1, keepdims=True) + EPS)
                xn = xh * r
                gy = g_ref[:, 128 * h:128 * h + 128]
                dpre_ref[:, sl] = (scale * r) * (gy - xn * jnp.sum(gy * xn, axis=-1, keepdims=True)) * dsl[:, sl]
        dpre_ref[:, 1024:] = dv_ref[...] * dsl[:, 1024:]
        abv, dgb = ab_ref[...], dgb_ref[...]
        lane = lax.broadcasted_iota(jnp.int32, (tm, 128), 1)
        na = -jnp.exp(al_ref[...])
        xs = abv + dt_ref[...]
        da = dgb * na * _sig(xs)
        b = _sig(abv)
        o_ref[:, :256] = dkv_ref[...]
        o_ref[:, 256:] = jnp.where(lane < DN_HEADS, da, jnp.where(lane < 2 * DN_HEADS, dgb * b * (1.0 - b), 0.0))
        head = lane < DN_HEADS
        upd = jnp.concatenate([jnp.sum(jnp.where(head, dgb * na * _softplus(xs), 0.0), axis=0, keepdims=True),
                               jnp.sum(jnp.where(head, da, 0.0), axis=0, keepdims=True), jnp.zeros((6, 128), F32)],
                              axis=0)

        @pl.when(i == 0)
        def _():
            st_ref[...] = upd

        @pl.when(i > 0)
        def _():
            st_ref[...] += upd

    full = lambda s: pl.BlockSpec(s, lambda i: (0, 0))
    blk = pl.BlockSpec((tm, 512), lambda i: (i, 0))
    return pl.pallas_call(
        body, name=name, grid=(T // tm,),
        in_specs=[pl.BlockSpec(memory_space=pl.ANY), cur, prev, ab, blk, blk, blk,
                  pl.BlockSpec((tm, 128), lambda i: (i, 0)), pl.BlockSpec((tm, 256), lambda i: (i, 0)),
                  full((DN_K, QKV_C)), full((1, 128)), full((1, 128))],
        out_specs=[pl.BlockSpec((tm, 384), lambda i: (i, C_KA // 384)),
                   pl.BlockSpec((tm, QKV_C), lambda i: (i, 0)), full((8, 128))],
        out_shape=[jax.ShapeDtypeStruct(dproj.shape, F32), jax.ShapeDtypeStruct((T, QKV_C), F32),
                   jax.ShapeDtypeStruct((8, 128), F32)],
        scratch_shapes=[pltpu.VMEM((tm + HALO_C, QKV_C), F32)],
        input_output_aliases={0: 0},
        compiler_params=_cparams(("arbitrary",)),
    )(dproj, proj, proj, proj, dqd, dkd, dvd, dgb, dkv_a, sconv_w, alog_v, dtb_v)


def _gdn_prep_bwd2(dproj, proj, dpre, sconv_w, name):
    T = proj.shape[0]
    tm = min(512, T)
    nt = T // tm
    r = tm // HALO_C
    cur, prev, _ = _gdn_specs(T, tm)

    def body(dp_any, x_ref, xp_ref, d_ref, dn_ref, w_ref, dx_ref, gw_ref, extx_ref, extd_ref):
        i = pl.program_id(0)
        extx_ref[:HALO_C] = jnp.where(i > 0, xp_ref[...], 0.0)
        extx_ref[HALO_C:] = x_ref[...]
        d = d_ref[...]
        extd_ref[:tm] = d
        extd_ref[tm:] = jnp.where(i < nt - 1, dn_ref[...], 0.0)
        dx = jnp.zeros((tm, QKV_C), F32)
        rows = []
        for k in range(DN_K):
            dx = dx + w_ref[k:k + 1, :] * extd_ref[pl.ds(DN_K - 1 - k, tm), :]
            rows.append(jnp.sum(d * extx_ref[pl.ds(HALO_C - DN_K + 1 + k, tm), :], axis=0, keepdims=True))
        rows.append(jnp.zeros((8 - DN_K, QKV_C), F32))
        gw = jnp.concatenate(rows, axis=0)
        dx_ref[...] = dx

        @pl.when(i == 0)
        def _():
            gw_ref[...] = gw

        @pl.when(i > 0)
        def _():
            gw_ref[...] += gw

    full = lambda s: pl.BlockSpec(s, lambda i: (0, 0))
    return pl.pallas_call(
        body, name=name, grid=(nt,),
        in_specs=[pl.BlockSpec(memory_space=pl.ANY), cur, prev, pl.BlockSpec((tm, QKV_C), lambda i: (i, 0)),
                  pl.BlockSpec((HALO_C, QKV_C), lambda i: (jnp.minimum((i + 1) * r, T // HALO_C - 1), 0)),
                  full((DN_K, QKV_C))],
        out_specs=[cur, full((8, QKV_C))],
        out_shape=[jax.ShapeDtypeStruct(dproj.shape, F32), jax.ShapeDtypeStruct((8, QKV_C), F32)],
        scratch_shapes=[pltpu.VMEM((tm + HALO_C, QKV_C), F32), pltpu.VMEM((tm + HALO_C, QKV_C), F32)],
        input_output_aliases={0: 0},
        compiler_params=_cparams(("arbitrary",)),
    )(dproj, proj, proj, dpre, dpre, sconv_w)


def _merge_fwd(x, proj, ya, yb, yc, wa, wb, wc, wo, gate, name):
    T = x.shape[0]
    tm = min(256, T)

    def body(x_ref, mg_ref, ya_ref, yb_ref, yc_ref, wa_ref, wb_ref, wc_ref, wo_ref, gate_ref, o_ref):
        merged = (_sig(mg_ref[:, :D]) * _bdot(ya_ref[...], wa_ref[...])
                  + _sig(mg_ref[:, D:2 * D]) * _bdot(yb_ref[...], wb_ref[...])
                  + _sig(mg_ref[:, 2 * D:]) * _bdot(yc_ref[...], wc_ref[...]))
        o_ref[...] = x_ref[...] + gate_ref[...] * _bdot(merged, wo_ref[...])

    full = lambda s: pl.BlockSpec(s, lambda i: (0, 0))
    yb_ = pl.BlockSpec((tm, 512), lambda i: (i, 0))
    return pl.pallas_call(
        body, name=name, grid=(T // tm,),
        in_specs=[pl.BlockSpec((tm, D), lambda i: (i, 0)), pl.BlockSpec((tm, 3 * D), lambda i: (i, 0)), yb_, yb_, yb_,
                  full((512, D)), full((512, D)), full((512, D)), full((D, D)), full((1, D))],
        out_specs=pl.BlockSpec((tm, D), lambda i: (i, 0)),
        out_shape=jax.ShapeDtypeStruct((T, D), F32),
        compiler_params=_cparams(("parallel",)),
    )(x, proj, ya, yb, yc, wa, wb, wc, wo, _row(gate))


def _merge_bwd(dout, proj, ya, yb, yc, wa, wb, wc, wo, gate, name):
    T = dout.shape[0]
    tm = min(128, T)
    nt = T // tm

    def body(do_ref, mg_ref, ya_ref, yb_ref, yc_ref, wa_ref, wb_ref, wc_ref, wo_ref, gate_ref,
             dmg_ref, dya_ref, dyb_ref, dyc_ref, gwa_hbm, gwb_hbm, gwc_hbm, gwo_hbm, gg_ref,
             gwa_ref, gwb_ref, gwc_ref, gwo_ref):
        i = pl.program_id(0)

        @pl.when(i == 0)
        def _():
            for r in (gwa_ref, gwb_ref, gwc_ref, gwo_ref, gg_ref):
                r[...] = jnp.zeros_like(r)

        ys = (ya_ref[...], yb_ref[...], yc_ref[...])
        ws = (wa_ref, wb_ref, wc_ref)
        gs = tuple(_sig(mg_ref[:, j * D:(j + 1) * D]) for j in range(3))
        ps = tuple(_bdot(ys[j], ws[j][...]) for j in range(3))
        merged = gs[0] * ps[0] + gs[1] * ps[1] + gs[2] * ps[2]
        mo = _bdot(merged, wo_ref[...])
        do = do_ref[...]
        gg_ref[...] += jnp.sum(do * mo, axis=0, keepdims=True)
        dmo = do * gate_ref[...]
        dmerged = _bdot(dmo, wo_ref[...], NT)
        gwo_ref[...] += _bdot(merged, dmo, TN)
        for j, (dy_ref, gw_ref) in enumerate(((dya_ref, gwa_ref), (dyb_ref, gwb_ref), (dyc_ref, gwc_ref))):
            dp = dmerged * gs[j]
            dmg_ref[:, j * D:(j + 1) * D] = dmerged * ps[j] * gs[j] * (1.0 - gs[j])
            dy_ref[...] = _bdot(dp, ws[j][...], NT)
            gw_ref[...] += _bdot(ys[j], dp, TN)

        @pl.when(i == nt - 1)
        def _():
            for src, dst in ((gwa_ref, gwa_hbm), (gwb_ref, gwb_hbm), (gwc_ref, gwc_hbm), (gwo_ref, gwo_hbm)):
                pltpu.sync_copy(src, dst)

    full = lambda s: pl.BlockSpec(s, lambda i: (0, 0))
    yb_ = pl.BlockSpec((tm, 512), lambda i: (i, 0))
    anyspec = pl.BlockSpec(memory_space=pl.ANY)
    return pl.pallas_call(
        body, name=name, grid=(nt,),
        in_specs=[pl.BlockSpec((tm, D), lambda i: (i, 0)), pl.BlockSpec((tm, 3 * D), lambda i: (i, 0)), yb_, yb_, yb_,
                  full((512, D)), full((512, D)), full((512, D)), full((D, D)), full((1, D))],
        out_specs=[pl.BlockSpec((tm, 3 * D), lambda i: (i, 0)), yb_, yb_, yb_, anyspec, anyspec, anyspec, anyspec,
                   full((1, D))],
        out_shape=[jax.ShapeDtypeStruct((T, NP), F32)] + [jax.ShapeDtypeStruct((T, 512), F32)] * 3
        + [jax.ShapeDtypeStruct((512, D), F32)] * 3 + [jax.ShapeDtypeStruct((D, D), F32), jax.ShapeDtypeStruct((1, D), F32)],
        scratch_shapes=[pltpu.VMEM((512, D), F32)] * 3 + [pltpu.VMEM((D, D), F32)],
        compiler_params=_cparams(("arbitrary",)),
    )(dout, proj, ya, yb, yc, wa, wb, wc, wo, _row(gate))


def _loss_head(y, tgt, name):
    T = y.shape[0]
    tm = min(512, T)

    def body(y_ref, t_ref, dy_ref, l_ref):
        i = pl.program_id(0)
        diff = y_ref[...] - t_ref[...]
        dy_ref[...] = diff * (1.0 / D)
        part = jnp.sum(diff * diff, axis=0, keepdims=True)

        @pl.when(i == 0)
        def _():
            l_ref[...] = part

        @pl.when(i > 0)
        def _():
            l_ref[...] += part

    blk = pl.BlockSpec((tm, D), lambda i: (i, 0))
    return pl.pallas_call(
        body, name=name, grid=(T // tm,), in_specs=[blk, blk],
        out_specs=[blk, pl.BlockSpec((1, D), lambda i: (0, 0))],
        out_shape=[jax.ShapeDtypeStruct((T, D), F32), jax.ShapeDtypeStruct((1, D), F32)],
        compiler_params=_cparams(("arbitrary",)),
    )(y, tgt)


def _ada_fwd(c_all, w_ada, b_my, name):
    def body(c_ref, w_ref, b_ref, o_ref):
        sc = _silu(c_ref[...])
        for l in range(DEPTH):
            o_ref[l] = _bdot(sc, w_ref[l]) + b_ref[l:l + 1, :]

    return pl.pallas_call(body, name=name, out_shape=jax.ShapeDtypeStruct((DEPTH, N_DEV, w_ada.shape[2]), F32),
                          compiler_params=_cparams())(c_all, w_ada, b_my)


def _ada_bwd(c_all, dmod_my, name):
    def body(c_ref, d_ref, o_ref):
        sc = _silu(c_ref[...])
        for l in range(DEPTH):
            o_ref[l] = _bdot(sc, d_ref[l], TN)

    return pl.pallas_call(body, name=name, out_shape=jax.ShapeDtypeStruct((DEPTH, D, dmod_my.shape[2]), F32),
                          compiler_params=_cparams())(c_all, dmod_my)


def _adam_math(w, g, m, v):
    m = ADAM_B1 * m + (1.0 - ADAM_B1) * g
    v = ADAM_B2 * v + (1.0 - ADAM_B2) * (g * g)
    m_hat = m / (1.0 - ADAM_B1 ** ADAM_STEP)
    v_hat = v / (1.0 - ADAM_B2 ** ADAM_STEP)
    return -ADAM_LR * (m_hat / (jnp.sqrt(v_hat) + ADAM_EPS) + ADAM_WD * w), m, v


def _row_tile(rows, cap):
    best = rows
    for t in range(8, min(rows, cap) + 1, 8):
        if rows % t == 0:
            best = t
    return best if best <= cap else rows


def _adamw(w, g, m, v, name):
    R, C = w.shape
    tr = _row_tile(R, 256)

    def body(w_ref, g_ref, m_ref, v_ref, d_ref, mo_ref, vo_ref):
        d_ref[...], mo_ref[...], vo_ref[...] = _adam_math(w_ref[...], g_ref[...], m_ref[...], v_ref[...])

    blk = pl.BlockSpec((tr, C), lambda i: (i, 0))
    return pl.pallas_call(body, name=name, grid=(R // tr,), in_specs=[blk] * 4, out_specs=[blk] * 3,
                          out_shape=[jax.ShapeDtypeStruct((R, C), F32)] * 3,
                          compiler_params=_cparams(("parallel",)))(w, g, m, v)


def _sum_parts(parts, name):
    _, R, C = parts.shape
    tr = _row_tile(R, 256)

    def body(p_ref, o_ref):
        acc = p_ref[0]
        for j in range(1, N_DEV):
            acc = acc + p_ref[j]
        o_ref[...] = acc

    return pl.pallas_call(body, name=name, grid=(R // tr,),
                          in_specs=[pl.BlockSpec((N_DEV, tr, C), lambda i: (0, i, 0))],
                          out_specs=pl.BlockSpec((tr, C), lambda i: (i, 0)),
                          out_shape=jax.ShapeDtypeStruct((R, C), F32), compiler_params=_cparams(("parallel",)))(parts)


def _sum_adamw(parts, w, m, v, name):
    P, R, C = parts.shape
    tr = _row_tile(R, 128)

    def body(p_ref, w_ref, m_ref, v_ref, g_ref, d_ref, mo_ref, vo_ref):
        g = p_ref[0].astype(F32)
        for j in range(1, P):
            g = g + p_ref[j].astype(F32)
        g_ref[...] = g
        d_ref[...], mo_ref[...], vo_ref[...] = _adam_math(w_ref[...], g, m_ref[...], v_ref[...])

    blk = pl.BlockSpec((tr, C), lambda i: (i, 0))
    return pl.pallas_call(body, name=name, grid=(R // tr,),
                          in_specs=[pl.BlockSpec((P, tr, C), lambda i: (0, i, 0)), blk, blk, blk],
                          out_specs=[blk] * 4, out_shape=[jax.ShapeDtypeStruct((R, C), F32)] * 4,
                          compiler_params=_cparams(("parallel",)))(parts, w, m, v)


def _pair_sum(core, buf, recv, name):
    _, _, R, C = buf.shape
    tr = _row_tile(R, 128)

    def body(c_ref, a_ref, b_ref, o_ref):
        o_ref[...] = (a_ref[:, 0].astype(F32) + b_ref[...].astype(F32)).astype(BF16)

    return pl.pallas_call(
        body, name=name,
        grid_spec=pltpu.PrefetchScalarGridSpec(
            num_scalar_prefetch=1, grid=(R // tr,),
            in_specs=[pl.BlockSpec((4, 1, tr, C), lambda i, c: (0, c[0], i, 0)),
                      pl.BlockSpec((4, tr, C), lambda i, c: (0, i, 0))],
            out_specs=pl.BlockSpec((4, tr, C), lambda i, c: (0, i, 0))),
        out_shape=jax.ShapeDtypeStruct((4, R, C), BF16),
        compiler_params=_cparams(("parallel",)))(core, buf, recv)


SHARD_IN = D_IN // N_DEV


def _w_in_pieces():
    out, p = [], 0
    for a, b in _PAD_FROM:
        for j in range(N_DEV):
            lo, hi = max(a, SHARD_IN * j), min(b, SHARD_IN * (j + 1))
            if lo < hi:
                out.append((j, lo - SHARD_IN * j, hi - SHARD_IN * j, p + lo - a))
        p += b - a
    return out


def _assemble_w_in(gw, name):
    tr = 256
    nt = D // tr

    def body(x_ref, o_ref):
        for j, s0, s1, d0 in _w_in_pieces():
            o_ref[0, :, d0:d0 + s1 - s0] = x_ref[j, :, s0:s1]
        o_ref[0, :, D_IN:] = jnp.zeros((tr, NP - D_IN), gw.dtype)

    return pl.pallas_call(
        body, name=name, grid=(DEPTH, nt),
        in_specs=[pl.BlockSpec((N_DEV, tr, SHARD_IN), lambda l, i: (0, l * nt + i, 0))],
        out_specs=pl.BlockSpec((1, tr, NP), lambda l, i: (l, i, 0)),
        out_shape=jax.ShapeDtypeStruct((DEPTH, D, NP), gw.dtype),
        compiler_params=_cparams(("parallel", "parallel")))(gw)


def _split_w_in_grad(g0, g1, name):
    tr = 256
    nt = D // tr

    def body(g0_ref, g1_ref, o_ref):
        l = pl.program_id(0)

        def emit(g_ref):
            for j, s0, s1, d0 in _w_in_pieces():
                o_ref[j, :, s0:s1] = g_ref[:, d0:d0 + s1 - s0].astype(BF16)

        @pl.when(l == 0)
        def _():
            emit(g0_ref)

        @pl.when(l == 1)
        def _():
            emit(g1_ref)

    return pl.pallas_call(
        body, name=name, grid=(DEPTH, nt),
        in_specs=[pl.BlockSpec((tr, NP), lambda l, i: (i * (1 - l) + (nt - 1) * l, 0)),
                  pl.BlockSpec((tr, NP), lambda l, i: (i * l, 0))],
        out_specs=pl.BlockSpec((N_DEV, tr, SHARD_IN), lambda l, i: (0, l * nt + i, 0)),
        out_shape=jax.ShapeDtypeStruct((N_DEV, DEPTH * D, SHARD_IN), BF16),
        compiler_params=_cparams(("arbitrary", "arbitrary")))(g0, g1)


def _mesh_pos():
    return lax.axis_index("x"), lax.axis_index("y"), lax.axis_index("c")


def _all_gather(blocks, name):
    n = len(blocks)

    def body(*refs):
        ins, outs = refs[:n], refs[n:2 * n]
        send_sems, recv_sems, local_sems = refs[2 * n:]
        x, y, c = _mesh_pos()
        me, sibling = (x, y, c), (x, y, 1 - c)
        chips = [(1 - x, y), (x, 1 - y), (1 - x, 1 - y)]
        idx = lambda p: 4 * p[0] + 2 * p[1] + p[2]

        def copy(a, k, block, to, src=None):
            dst = outs[a].at[idx(block)]
            return pltpu.make_async_remote_copy(
                src_ref=dst if src is None else src, dst_ref=dst, send_sem=send_sems.at[a, k],
                recv_sem=recv_sems.at[a, k], device_id=to, device_id_type=pl.DeviceIdType.MESH)

        mine = [pltpu.make_async_copy(ins[a], outs[a].at[idx(me)], local_sems.at[a]) for a in range(n)]
        for cp in mine:
            cp.start()
        first = []
        for a in range(n):
            first.append(copy(a, 0, me, sibling, src=ins[a]))
            first += [copy(a, 1 + j, me, (*chip, c), src=ins[a]) for j, chip in enumerate(chips)]
        for cp in first:
            cp.start()
        passed = []
        for j, chip in enumerate(chips):
            for a in range(n):
                copy(a, 1 + j, (*chip, c), me).wait_recv()
                cp = copy(a, 4 + j, (*chip, c), sibling)
                cp.start()
                passed.append(cp)
        for a in range(n):
            copy(a, 0, sibling, me).wait_recv()
            for j, chip in enumerate(chips):
                copy(a, 4 + j, (*chip, 1 - c), me).wait_recv()
        for cp in first + passed:
            cp.wait_send()
        for cp in mine:
            cp.wait()

    anyspec = pl.BlockSpec(memory_space=pl.ANY)
    return pl.pallas_call(
        body, name=name, in_specs=[anyspec] * n, out_specs=[anyspec] * n,
        out_shape=[jax.ShapeDtypeStruct((N_DEV,) + b.shape, b.dtype) for b in blocks],
        scratch_shapes=[pltpu.SemaphoreType.DMA((n, 7)), pltpu.SemaphoreType.DMA((n, 7)),
                        pltpu.SemaphoreType.DMA((n,))],
    )(*blocks)


def _exchange_core(bufs, name):
    n = len(bufs)

    def body(*refs):
        ins, outs = refs[:n], refs[n:2 * n]
        send_sems, recv_sems = refs[2 * n:]
        x, y, c = _mesh_pos()
        copies = []
        for a in range(n):
            for q in range(4):
                cp = pltpu.make_async_remote_copy(
                    src_ref=ins[a].at[q, 1 - c], dst_ref=outs[a].at[q], send_sem=send_sems.at[a, q],
                    recv_sem=recv_sems.at[a, q], device_id=(x, y, 1 - c), device_id_type=pl.DeviceIdType.MESH)
                cp.start()
                copies.append(cp)
        for cp in copies:
            cp.wait()

    anyspec = pl.BlockSpec(memory_space=pl.ANY)
    return pl.pallas_call(
        body, name=name, in_specs=[anyspec] * n, out_specs=[anyspec] * n,
        out_shape=[jax.ShapeDtypeStruct((4,) + b.shape[2:], b.dtype) for b in bufs],
        scratch_shapes=[pltpu.SemaphoreType.DMA((n, 4)), pltpu.SemaphoreType.DMA((n, 4))],
    )(*bufs)


def _exchange_chips(bufs, name):
    n = len(bufs)

    def body(*refs):
        ins, outs = refs[:n], refs[n:2 * n]
        send_sems, recv_sems, local_sems = refs[2 * n:]
        x, y, c = _mesh_pos()
        chip = 2 * x + y
        local = [pltpu.make_async_copy(ins[a].at[chip], outs[a].at[chip], local_sems.at[a]) for a in range(n)]
        for cp in local:
            cp.start()
        copies = []
        for k in range(1, 4):
            px = 1 - x if k & 2 else x
            py = 1 - y if k & 1 else y
            for a in range(n):
                cp = pltpu.make_async_remote_copy(
                    src_ref=ins[a].at[2 * px + py], dst_ref=outs[a].at[chip], send_sem=send_sems.at[a, k - 1],
                    recv_sem=recv_sems.at[a, k - 1], device_id=(px, py, c), device_id_type=pl.DeviceIdType.MESH)
                cp.start()
                copies.append(cp)
        for cp in copies:
            cp.wait()
        for cp in local:
            cp.wait()

    anyspec = pl.BlockSpec(memory_space=pl.ANY)
    return pl.pallas_call(
        body, name=name, in_specs=[anyspec] * n, out_specs=[anyspec] * n,
        out_shape=[jax.ShapeDtypeStruct(b.shape, b.dtype) for b in bufs],
        scratch_shapes=[pltpu.SemaphoreType.DMA((n, 3)), pltpu.SemaphoreType.DMA((n, 3)),
                        pltpu.SemaphoreType.DMA((n,))],
    )(*bufs)


def _pack_flat(parts, rows, cols, lead=()):
    flat = jnp.concatenate([p.reshape(lead + (-1,)) for p in parts], axis=-1)
    pad = rows * cols - flat.shape[-1]
    flat = jnp.pad(flat, [(0, 0)] * len(lead) + [(0, pad)])
    return flat.reshape(lead + (rows, cols))


def _unpack_flat(buf, shapes, lead=()):
    flat = buf.reshape(lead + (-1,))
    out, o = [], 0
    for s in shapes:
        n = int(np.prod(s))
        out.append(flat[..., o:o + n].reshape(lead + tuple(s)))
        o += n
    return out


_SMALL = (("b_ada", (3 * D,)), ("norm_g", (D,)), ("q_norm_g", (64,)), ("k_norm_g", (64,)), ("sinks", (8,)),
          ("dw_b", (512,)), ("ln_g", (512,)), ("ln_b", (512,)), ("pw2_b", (512,)), ("a_log", (4,)),
          ("dt_bias", (4,)), ("dn_norm_g", (128,)), ("dw_w", (CONV_K, 512)), ("sconv_w", (DN_K, QKV_C)))
_N_REPL = 12
_SMALL_SHAPES = tuple((DEPTH,) + s for _, s in _SMALL)
_SMALL_ROWS = -(-sum(int(np.prod(s)) for s in _SMALL_SHAPES) // (128 * 8)) * 8
_UPD_SHAPES = _SMALL_SHAPES[:_N_REPL] + ((DEPTH, CONV_K, 512 // N_DEV), (DEPTH, DN_K, QKV_C // N_DEV))
_UPD_ROWS = -(-sum(int(np.prod(s)) for s in _UPD_SHAPES) // (128 * 8)) * 8


def _lane4(v):
    return jnp.pad(v, (0, 124)).reshape(1, 128)


def kernel(x, c, w_ada, b_ada, norm_g, w_in, q_norm_g, k_norm_g, sinks, dw_w, dw_b, ln_g, ln_b, pw2_w, pw2_b, sconv_w, a_log, dt_bias, dn_norm_g, w_proj_a, w_proj_b, w_proj_c, w_out, loss_target, m_w_ada, m_b_ada, m_norm_g, m_w_in, m_q_norm_g, m_k_norm_g, m_sinks, m_dw_w, m_dw_b, m_ln_g, m_ln_b, m_pw2_w, m_pw2_b, m_sconv_w, m_a_log, m_dt_bias, m_dn_norm_g, m_w_proj_a, m_w_proj_b, m_w_proj_c, m_w_out, v_w_ada, v_b_ada, v_norm_g, v_w_in, v_q_norm_g, v_k_norm_g, v_sinks, v_dw_w, v_dw_b, v_ln_g, v_ln_b, v_pw2_w, v_pw2_b, v_sconv_w, v_a_log, v_dt_bias, v_dn_norm_g, v_w_proj_a, v_w_proj_b, v_w_proj_c, v_w_out):
    T = x.shape[1]
    nc = T // CHUNK
    xi, yi, ci = _mesh_pos()
    me = 4 * xi + 2 * yi + ci
    big_w = (w_in, pw2_w, w_proj_a, w_proj_b, w_proj_c, w_out)
    big_m = (m_w_in, m_pw2_w, m_w_proj_a, m_w_proj_b, m_w_proj_c, m_w_out)
    big_v = (v_w_in, v_pw2_w, v_w_proj_a, v_w_proj_b, v_w_proj_c, v_w_out)

    ada_cols = w_ada.shape[2]
    dw_cols, sc_cols = dw_w.shape[2], sconv_w.shape[2]
    flat2 = lambda a: a.reshape(-1, a.shape[-1])
    small_shapes = ((D,), dw_w.shape, sconv_w.shape)
    small_rows = -(-sum(int(np.prod(s)) for s in small_shapes) // (128 * 8)) * 8
    small32 = _pack_flat([c, dw_w, sconv_w], small_rows, 128)
    gw_in, gpw2, gpa, gpb, gpc, gwo, g32 = _all_gather([flat2(a.astype(BF16)) for a in big_w] + [small32],
                                                        "gather_weights")
    wp = _assemble_w_in(gw_in, "assemble_w_in")
    pw2_f = gpw2.reshape(N_DEV, DEPTH, -1, 512).transpose(1, 0, 2, 3).reshape(DEPTH, 512, 512)
    wa_f, wb_f, wc_f = (g.reshape(N_DEV, DEPTH, 512, -1).transpose(1, 2, 0, 3).reshape(DEPTH, 512, D)
                        for g in (gpa, gpb, gpc))
    wo_f = gwo.reshape(N_DEV, DEPTH, -1, D).transpose(1, 0, 2, 3).reshape(DEPTH, D, D)
    c_all, gdw, gsc = _unpack_flat(g32, small_shapes, lead=(N_DEV,))
    dw_f = gdw.transpose(1, 2, 0, 3).reshape(DEPTH, CONV_K, 512)
    sc_f = gsc.transpose(1, 2, 0, 3).reshape(DEPTH, DN_K, QKV_C)

    b_my = lax.dynamic_slice(b_ada, (0, me * ada_cols), (DEPTH, ada_cols))
    mod_part = _ada_fwd(c_all, w_ada, b_my, "ada_fwd")
    (gmod,) = _all_gather([mod_part.reshape(-1, 128)], "gather_mod")
    mod_all = gmod.reshape(N_DEV, DEPTH, N_DEV, ada_cols).transpose(1, 2, 0, 3).reshape(DEPTH, N_DEV, 3 * D)
    mod = lax.dynamic_index_in_dim(mod_all, me, axis=1, keepdims=False)
    shift, scale, gate = mod[:, :D], mod[:, D:2 * D], mod[:, 2 * D:]

    xs, saved = [x[0]], []
    for l in range(DEPTH):
        xl = xs[-1]
        h = _norm_fwd(xl, norm_g[l], scale[l], shift[l], f"norm_fwd{l}")
        proj = _mm(h, wp[l], tm=min(512, T), tn=1152, tk=D, name=f"in_proj{l}")
        ya = _attn_fwd(proj, q_norm_g[l], k_norm_g[l], sinks[l], f"attn_fwd{l}")
        yb = _conf_fwd(proj, dw_f[l], dw_b[l], ln_g[l], ln_b[l], pw2_f[l], pw2_b[l], f"conf_fwd{l}")
        alv, dtv, dng = _lane4(a_log[l]), _lane4(dt_bias[l]), _row(dn_norm_g[l])
        qd, kd, vd, gb = _gdn_prep_fwd(proj, sc_f[l], alv, dtv, f"gdn_prep_fwd{l}")
        gbt = gb[:, :8].reshape(nc, CHUNK, 8).transpose(0, 2, 1)
        yc, ss = _gdn_scan_fwd(qd, kd, vd, gb, gbt, proj, dng, f"gdn_scan_fwd{l}")
        xs.append(_merge_fwd(xl, proj, ya, yb, yc, wa_f[l], wb_f[l], wc_f[l], wo_f[l], gate[l], f"merge_fwd{l}"))
        saved.append((h, proj, ya, yb, yc, qd, kd, vd, gb, gbt, ss, alv, dtv, dng))

    dout, lsum = _loss_head(xs[-1], loss_target[0], "loss_head")
    loss = lax.psum(0.5 * jnp.sum(lsum) / D, ("x", "y", "c"))

    small = {name: [None] * DEPTH for name, _ in _SMALL}
    big_g = [[None] * DEPTH for _ in big_w]
    for l in reversed(range(DEPTH)):
        h, proj, ya, yb, yc, qd, kd, vd, gb, gbt, ss, alv, dtv, dng = saved[l]
        dproj, dya, dyb, dyc, g_wa, g_wb, g_wc, g_wo, g_gate = _merge_bwd(
            dout, proj, ya, yb, yc, wa_f[l], wb_f[l], wc_f[l], wo_f[l], gate[l], f"merge_bwd{l}")
        dproj, dkv_a, g_q, g_k, g_s = _attn_bwd(dproj, proj, dya, q_norm_g[l], k_norm_g[l], sinks[l], f"attn_bwd{l}")
        dproj, du1, g_pw2, st_b = _conf_bwd1(dproj, proj, dyb, dw_f[l], dw_b[l], ln_g[l], ln_b[l], pw2_f[l], pw2_b[l],
                                             f"conf_bwd_a{l}")
        dproj, g_dw = _conf_bwd2(dproj, proj, du1, dw_f[l], f"conf_bwd_b{l}")
        dproj, dqd, dkd, dvd, dgb, g_dn = _gdn_scan_bwd(dproj, qd, kd, vd, gb, gbt, ss, proj, dyc, dng,
                                                        f"gdn_scan_bwd{l}")
        dproj, dpre, st_c = _gdn_prep_bwd1(dproj, proj, dqd, dkd, dvd, dgb, dkv_a, sc_f[l], alv, dtv,
                                           f"gdn_prep_bwd_a{l}")
        dproj, g_sc = _gdn_prep_bwd2(dproj, proj, dpre, sc_f[l], f"gdn_prep_bwd_b{l}")
        dh = _mm(dproj, wp[l], tb=True, tm=min(512, T), tn=D, tk=1152, name=f"d_h{l}")
        g_wp = _mm(h, dproj, ta=True, tm=D, tn=1152, tk=min(512, T), name=f"d_w_in{l}")
        dout, st_n = _norm_bwd(dh, xs[l], dout, norm_g[l], scale[l], f"norm_bwd{l}")
        for i, g in enumerate((g_wp, g_pw2, g_wa, g_wb, g_wc, g_wo)):
            big_g[i][l] = g
        for name, g in (("b_ada", jnp.concatenate([st_n[0], st_n[1], g_gate[0]])), ("norm_g", st_n[2]),
                        ("q_norm_g", g_q[0]), ("k_norm_g", g_k[0]), ("sinks", g_s[0]), ("dw_b", st_b[3]),
                        ("ln_g", st_b[1]), ("ln_b", st_b[2]), ("pw2_b", st_b[0]), ("a_log", st_c[0, :4]),
                        ("dt_bias", st_c[1, :4]), ("dn_norm_g", g_dn[0]), ("dw_w", g_dw[:CONV_K]),
                        ("sconv_w", g_sc[:DN_K])):
            small[name][l] = g
    grad_x = dout[None]

    part = _pack_flat([jnp.stack(small[name]) for name, _ in _SMALL], _SMALL_ROWS, 128)
    (gpart,) = _all_gather([part], "gather_small_grads")
    dmod_all = gpart.reshape(N_DEV, -1)[:, :DEPTH * 3 * D].reshape(N_DEV, DEPTH, 3 * D)
    dmod_my = lax.dynamic_slice(dmod_all, (0, 0, me * ada_cols), (N_DEV, DEPTH, ada_cols)).transpose(1, 0, 2)
    g_w_ada = _ada_bwd(c_all, dmod_my, "ada_bwd")
    tot = _unpack_flat(_sum_parts(gpart, "sum_small_grads"), _SMALL_SHAPES)
    g_small = dict(zip([n for n, _ in _SMALL], tot))
    g_small["dw_w"] = lax.dynamic_slice(g_small["dw_w"], (0, 0, me * dw_cols), (DEPTH, CONV_K, dw_cols))
    g_small["sconv_w"] = lax.dynamic_slice(g_small["sconv_w"], (0, 0, me * sc_cols), (DEPTH, DN_K, sc_cols))
    env = dict(b_ada=(b_ada, m_b_ada, v_b_ada), norm_g=(norm_g, m_norm_g, v_norm_g),
               q_norm_g=(q_norm_g, m_q_norm_g, v_q_norm_g), k_norm_g=(k_norm_g, m_k_norm_g, v_k_norm_g),
               sinks=(sinks, m_sinks, v_sinks), dw_b=(dw_b, m_dw_b, v_dw_b), ln_g=(ln_g, m_ln_g, v_ln_g),
               ln_b=(ln_b, m_ln_b, v_ln_b), pw2_b=(pw2_b, m_pw2_b, v_pw2_b), a_log=(a_log, m_a_log, v_a_log),
               dt_bias=(dt_bias, m_dt_bias, v_dt_bias), dn_norm_g=(dn_norm_g, m_dn_norm_g, v_dn_norm_g),
               dw_w=(dw_w, m_dw_w, v_dw_w), sconv_w=(sconv_w, m_sconv_w, v_sconv_w))
    names = [n for n, _ in _SMALL]
    pk = lambda k: _pack_flat([env[n][k] for n in names], _UPD_ROWS, 128)
    upd = _adamw(pk(0), _pack_flat([g_small[n] for n in names], _UPD_ROWS, 128), pk(1), pk(2), "adamw_small")
    d_small, m_small, v_small = (dict(zip(names, _unpack_flat(u, _UPD_SHAPES))) for u in upd)

    d_ada, nm_ada, nv_ada = (u.reshape(w_ada.shape) for u in
                             _adamw(flat2(w_ada), flat2(g_w_ada), flat2(m_w_ada), flat2(v_w_ada), "adamw_w_ada"))

    g_pw, g_a, g_b, g_c, g_o = (jnp.stack(g) for g in big_g[1:])
    by_dest = [_split_w_in_grad(big_g[0][0], big_g[0][1], "split_w_in_grad"),
               g_pw.reshape(DEPTH, N_DEV, -1, 512).transpose(1, 0, 2, 3).astype(BF16)]
    by_dest += [g.reshape(DEPTH, 512, N_DEV, -1).transpose(2, 0, 1, 3).astype(BF16) for g in (g_a, g_b, g_c)]
    by_dest.append(g_o.reshape(DEPTH, N_DEV, -1, D).transpose(1, 0, 2, 3).astype(BF16))
    by_dest = [b.reshape(4, 2, -1, b.shape[-1]) for b in by_dest]
    from_sibling = _exchange_core(by_dest, "exchange_grads_core")
    core = jnp.reshape(ci, (1,)).astype(jnp.int32)
    chip_sums = [_pair_sum(core, b, r, f"pair_sum{i}") for i, (b, r) in enumerate(zip(by_dest, from_sibling))]
    parts = _exchange_chips(chip_sums, "exchange_grads_chips")
    res = [_sum_adamw(p, flat2(w), flat2(m), flat2(v), f"sum_adamw{i}")
           for i, (p, w, m, v) in enumerate(zip(parts, big_w, big_m, big_v))]
    g_big, d_big, m_big, v_big = ([r[k].reshape(w.shape) for r, w in zip(res, big_w)] for k in range(4))

    order = ("w_ada", "b_ada", "norm_g", "w_in", "q_norm_g", "k_norm_g", "sinks", "dw_w", "dw_b", "ln_g", "ln_b",
             "pw2_w", "pw2_b", "sconv_w", "a_log", "dt_bias", "dn_norm_g", "w_proj_a", "w_proj_b", "w_proj_c", "w_out")
    big_names = ("w_in", "pw2_w", "w_proj_a", "w_proj_b", "w_proj_c", "w_out")

    def pick(kind):
        src_small = (g_small, d_small, m_small, v_small)[kind]
        src_big = (g_big, d_big, m_big, v_big)[kind]
        src_ada = (g_w_ada, d_ada, nm_ada, nv_ada)[kind]
        return [src_ada if n == "w_ada" else src_big[big_names.index(n)] if n in big_names else src_small[n]
                for n in order]

    return (loss, grad_x, *pick(0), *pick(1), *pick(2), *pick(3))
```

```python
import functools
import math

import jax
import jax.numpy as jnp
import numpy as np
from jax import lax
from jax.experimental import pallas as pl
from jax.experimental.pallas import tpu as pltpu

F32 = jnp.float32
BF16 = jnp.bfloat16
HI = lax.Precision.HIGHEST

N_DEV = 8
D = 1024
DEPTH = 2
EPS = 1e-6
NEG_INF = -1e30
WINDOW = 128
ATT_HEADS = 8
ATT_HD = 64
CONV_K = 31
DN_HEADS = 4
DN_K = 4
CHUNK = 64
D_IN = 7944
VMEM_LIMIT = 56 * 1024 * 1024

C_MG, C_QA, C_ZA, C_ZB, C_QC, C_KC, C_VC, C_GV, C_GG, C_ZC, C_KA, C_VA, C_AB, NP = (
    0, 3072, 3584, 4096, 4608, 5120, 5632, 6144, 6656, 7168, 7680, 7808, 7936, 8064)
_PAD_FROM = ((4872, 7944), (0, 512), (768, 1280), (2304, 2816), (2816, 4352), (1280, 2304), (4360, 4872),
             (512, 768), (4352, 4360))

ALIBI = tuple(float(2.0 ** (-8.0 * (h + 1) / ATT_HEADS)) for h in range(ATT_HEADS))

ADAM_LR, ADAM_B1, ADAM_B2, ADAM_EPS, ADAM_WD, ADAM_STEP = 0.001, 0.9, 0.999, 1e-08, 0.01, 10


def _cparams(sem=None):
    return pltpu.CompilerParams(dimension_semantics=sem, vmem_limit_bytes=VMEM_LIMIT)


def _sig(x):
    return jax.nn.sigmoid(x)


def _silu(x):
    return x * _sig(x)


def _dsilu(x):
    s = _sig(x)
    return s * (1.0 + x * (1.0 - s))


def _dot(a, b, dims=((1,), (0,)), precision=None):
    return lax.dot_general(a, b, (dims, ((), ())), preferred_element_type=F32, precision=precision)


def _bdot(a, b, dims=((1,), (0,))):
    return _dot(a.astype(BF16), b.astype(BF16), dims)


NN, NT, TN = ((1,), (0,)), ((1,), (1,)), ((0,), (0,))


def _row(v):
    return v.reshape(1, -1)


def _mm(a, b, *, ta=False, tb=False, tm, tn, tk, name):
    M, K = (a.shape[1], a.shape[0]) if ta else a.shape
    N = b.shape[0] if tb else b.shape[1]
    assert M % tm == 0 and N % tn == 0 and K % tk == 0, (M, N, K, tm, tn, tk)
    nk = K // tk
    dims = ((0 if ta else 1,), (1 if tb else 0,))

    def body(a_ref, b_ref, o_ref, acc_ref):
        k = pl.program_id(2)

        @pl.when(k == 0)
        def _():
            acc_ref[...] = jnp.zeros_like(acc_ref)

        acc_ref[...] += _bdot(a_ref[...], b_ref[...], dims)

        @pl.when(k == nk - 1)
        def _():
            o_ref[...] = acc_ref[...]

    a_spec = pl.BlockSpec((tk, tm), lambda i, j, k: (k, i)) if ta else pl.BlockSpec((tm, tk), lambda i, j, k: (i, k))
    b_spec = pl.BlockSpec((tn, tk), lambda i, j, k: (j, k)) if tb else pl.BlockSpec((tk, tn), lambda i, j, k: (k, j))
    return pl.pallas_call(
        body, name=name, grid=(M // tm, N // tn, nk),
        in_specs=[a_spec, b_spec], out_specs=pl.BlockSpec((tm, tn), lambda i, j, k: (i, j)),
        out_shape=jax.ShapeDtypeStruct((M, N), F32),
        scratch_shapes=[pltpu.VMEM((tm, tn), F32)],
        compiler_params=_cparams(("parallel", "parallel", "arbitrary")),
    )(a, b)


def _norm_fwd(x, norm_g, scale, shift, name):
    T = x.shape[0]
    tm = min(512, T)

    def body(x_ref, g_ref, sc_ref, sh_ref, h_ref):
        xv = x_ref[...]
        r = lax.rsqrt(jnp.mean(xv * xv, axis=-1, keepdims=True) + EPS)
        h_ref[...] = ((xv * r) * g_ref[...] * (1.0 + sc_ref[...]) + sh_ref[...]).astype(BF16)

    vec = pl.BlockSpec((1, D), lambda i: (0, 0))
    return pl.pallas_call(
        body, name=name, grid=(T // tm,),
        in_specs=[pl.BlockSpec((tm, D), lambda i: (i, 0)), vec, vec, vec],
        out_specs=pl.BlockSpec((tm, D), lambda i: (i, 0)),
        out_shape=jax.ShapeDtypeStruct((T, D), BF16),
        compiler_params=_cparams(("parallel",)),
    )(x, _row(norm_g), _row(scale), _row(shift))


def _norm_bwd(dh, x, dres, norm_g, scale, name):
    T = x.shape[0]
    tm = min(512, T)

    def body(dh_ref, x_ref, dr_ref, g_ref, sc_ref, dx_ref, st_ref):
        i = pl.program_id(0)
        xv, dhv = x_ref[...], dh_ref[...]
        r = lax.rsqrt(jnp.mean(xv * xv, axis=-1, keepdims=True) + EPS)
        xh = xv * r
        g, s1 = g_ref[...], 1.0 + sc_ref[...]
        dxh = dhv * (g * s1)
        dx_ref[...] = dr_ref[...] + r * (dxh - xh * jnp.mean(dxh * xh, axis=-1, keepdims=True))
        dhx = dhv * xh
        upd = jnp.concatenate([jnp.sum(dhv, axis=0, keepdims=True), jnp.sum(dhx * g, axis=0, keepdims=True),
                               jnp.sum(dhx * s1, axis=0, keepdims=True), jnp.zeros((5, D), F32)], axis=0)

        @pl.when(i == 0)
        def _():
            st_ref[...] = upd

        @pl.when(i > 0)
        def _():
            st_ref[...] += upd

    vec = pl.BlockSpec((1, D), lambda i: (0, 0))
    blk = pl.BlockSpec((tm, D), lambda i: (i, 0))
    return pl.pallas_call(
        body, name=name, grid=(T // tm,),
        in_specs=[blk, blk, blk, vec, vec],
        out_specs=[blk, pl.BlockSpec((8, D), lambda i: (0, 0))],
        out_shape=[jax.ShapeDtypeStruct((T, D), F32), jax.ShapeDtypeStruct((8, D), F32)],
        compiler_params=_cparams(("arbitrary",)),
    )(dh, x, dres, _row(norm_g), _row(scale))


def _rms(x, g):
    r = lax.rsqrt(jnp.mean(x * x, axis=-1, keepdims=True) + EPS)
    return x * r, r


def _attn_mask(n):
    qi = lax.broadcasted_iota(jnp.int32, (WINDOW, 2 * WINDOW), 0)
    kj = lax.broadcasted_iota(jnp.int32, (WINDOW, 2 * WINDOW), 1)
    dist = qi + WINDOW - kj
    valid = (dist >= 0) & (dist < WINDOW) & ((n > 0) | (kj >= WINDOW))
    return valid, dist.astype(F32)


def _attn_probs(s, h, sink, valid, distf):
    s = s - ALIBI[h] * distf
    s = jnp.where(valid, s, NEG_INF)
    m = jnp.maximum(jnp.max(s, axis=-1, keepdims=True), sink)
    p = jnp.exp(s - m)
    es = jnp.exp(sink - m)
    den = jnp.sum(p, axis=-1, keepdims=True) + es
    return p / den, es / den


def _attn_fwd(proj, q_norm_g, k_norm_g, sinks, name):
    T = proj.shape[0]
    nb = T // WINDOW

    def body(sink_ref, q_ref, z_ref, kc_ref, kp_ref, vc_ref, vp_ref, qg_ref, kg_ref, o_ref):
        n = pl.program_id(0)
        valid, distf = _attn_mask(n)
        k2 = jnp.concatenate([kp_ref[...], kc_ref[...]], axis=0)
        v2 = jnp.concatenate([vp_ref[...], vc_ref[...]], axis=0)
        kns = [(_rms(k2[:, 64 * g:64 * g + 64], None)[0] * kg_ref[...]).astype(BF16) for g in range(2)]
        vgs = [v2[:, 64 * g:64 * g + 64].astype(BF16) for g in range(2)]

        def head(h):
            sl = slice(64 * h, 64 * h + 64)
            qn = (_rms(q_ref[:, sl], None)[0] * qg_ref[...]) * (ATT_HD ** -0.5)
            s = _bdot(qn, kns[h // 4], NT)
            yield
            p, _ = _attn_probs(s, h, sink_ref[h], valid, distf)
            o = _bdot(p, vgs[h // 4])
            yield
            o_ref[:, sl] = o * _silu(z_ref[:, sl])

        _lockstep([head(h) for h in range(ATT_HEADS)])

    prev = lambda n: jnp.maximum(n - 1, 0)
    return pl.pallas_call(
        body, name=name, grid=(nb,),
        in_specs=[pl.BlockSpec(memory_space=pltpu.SMEM),
                  pl.BlockSpec((WINDOW, 512), lambda n: (n, C_QA // 512)),
                  pl.BlockSpec((WINDOW, 512), lambda n: (n, C_ZA // 512)),
                  pl.BlockSpec((WINDOW, 128), lambda n: (n, C_KA // 128)),
                  pl.BlockSpec((WINDOW, 128), lambda n: (prev(n), C_KA // 128)),
                  pl.BlockSpec((WINDOW, 128), lambda n: (n, C_VA // 128)),
                  pl.BlockSpec((WINDOW, 128), lambda n: (prev(n), C_VA // 128)),
                  pl.BlockSpec((1, 64), lambda n: (0, 0)), pl.BlockSpec((1, 64), lambda n: (0, 0))],
        out_specs=pl.BlockSpec((WINDOW, 512), lambda n: (n, 0)),
        out_shape=jax.ShapeDtypeStruct((T, 512), F32),
        compiler_params=_cparams(("parallel",)),
    )(sinks, proj, proj, proj, proj, proj, proj, _row(q_norm_g), _row(k_norm_g))


def _rms_bwd(dy, xh, r, g):
    dxh = dy * g
    return r * (dxh - xh * jnp.mean(dxh * xh, axis=-1, keepdims=True)), dy * xh


def _attn_bwd(dproj, proj, dya, q_norm_g, k_norm_g, sinks, name):
    T = proj.shape[0]
    nb = T // WINDOW

    def body(sink_ref, dp_any, q_ref, z_ref, kc_ref, kp_ref, vc_ref, vp_ref, dy_ref, qg_ref, kg_ref,
             dqz_ref, dkv_ref, gq_ref, gk_ref, gs_ref, ck_ref, cv_ref):
        n = pl.program_id(0)

        @pl.when(n == 0)
        def _():
            gq_ref[...] = jnp.zeros_like(gq_ref)
            gk_ref[...] = jnp.zeros_like(gk_ref)
            gs_ref[...] = jnp.zeros_like(gs_ref)
            ck_ref[...] = jnp.zeros_like(ck_ref)
            cv_ref[...] = jnp.zeros_like(cv_ref)

        lane8 = lax.broadcasted_iota(jnp.int32, (1, 8), 1)

        @pl.when(n < nb)
        def _():
            valid, distf = _attn_mask(n)
            k2 = jnp.concatenate([kp_ref[...], kc_ref[...]], axis=0)
            v2 = jnp.concatenate([vp_ref[...], vc_ref[...]], axis=0)
            kns = [(_rms(k2[:, 64 * g:64 * g + 64], None)[0] * kg_ref[...]).astype(BF16) for g in range(2)]
            vgs = [v2[:, 64 * g:64 * g + 64].astype(BF16) for g in range(2)]

            def head(h):
                kn, vg = kns[h // 4], vgs[h // 4]
                sl = slice(64 * h, 64 * h + 64)
                qh, qr = _rms(q_ref[:, sl], None)
                qn = ((qh * qg_ref[...]) * (ATT_HD ** -0.5)).astype(BF16)
                zh = z_ref[:, sl]
                dy = dy_ref[:, sl]
                do = dy * _silu(zh)
                dob = do.astype(BF16)
                s = _dot(qn, kn, NT)
                dpm = _dot(dob, vg, NT)
                yield
                p, ps = _attn_probs(s, h, sink_ref[h], valid, distf)
                pb = p.astype(BF16)
                o = _dot(pb, vg)
                dvg = _dot(pb, dob, TN)
                yield
                dqz_ref[:, 512 + 64 * h:512 + 64 * h + 64] = dy * o * _dsilu(zh)
                delta = jnp.sum(do * o, axis=-1, keepdims=True)
                ds = (p * (dpm - delta)).astype(BF16)
                gs = jnp.where(lane8 == h, -jnp.sum(ps * delta, axis=0, keepdims=True), 0.0)
                dkn = _dot(ds, qn, TN)
                dqn = _dot(ds, kn) * (ATT_HD ** -0.5)
                yield
                dq, gq = _rms_bwd(dqn, qh, qr, qg_ref[...])
                dqz_ref[:, sl] = dq
                return dkn, dvg, jnp.sum(gq, axis=0, keepdims=True), gs

            res = _lockstep([head(h) for h in range(ATT_HEADS)])
            gq_acc = sum(r[2] for r in res[1:]) + res[0][2]
            gs_acc = sum(r[3] for r in res[1:]) + res[0][3]
            for g in range(2):
                dkn = (res[4 * g][0] + res[4 * g + 1][0]) + (res[4 * g + 2][0] + res[4 * g + 3][0])
                dvg = (res[4 * g][1] + res[4 * g + 1][1]) + (res[4 * g + 2][1] + res[4 * g + 3][1])
                ksl = slice(64 * g, 64 * g + 64)
                vsl = slice(128 + 64 * g, 128 + 64 * g + 64)
                dkv_ref[:, ksl] = ck_ref[:, ksl] + dkn[:WINDOW]
                dkv_ref[:, vsl] = cv_ref[:, ksl] + dvg[:WINDOW]
                ck_ref[:, ksl] = dkn[WINDOW:]
                cv_ref[:, ksl] = dvg[WINDOW:]
            gq_ref[...] += gq_acc
            gs_ref[...] += gs_acc

        @pl.when(n == nb)
        def _():
            dkv_ref[:, :128] = ck_ref[...]
            dkv_ref[:, 128:] = cv_ref[...]

        @pl.when(n > 0)
        def _():
            gk_acc = jnp.zeros((1, 64), F32)
            for g in range(2):
                ksl = slice(64 * g, 64 * g + 64)
                kh, kr = _rms(kp_ref[:, ksl], None)
                dk, gk = _rms_bwd(dkv_ref[:, ksl], kh, kr, kg_ref[...])
                dkv_ref[:, ksl] = dk
                gk_acc = gk_acc + jnp.sum(gk, axis=0, keepdims=True)
            gk_ref[...] += gk_acc

    cur = lambda n: jnp.minimum(n, nb - 1)
    prev = lambda n: jnp.maximum(n - 1, 0)
    small = lambda w: pl.BlockSpec((1, w), lambda n: (0, 0))
    return pl.pallas_call(
        body, name=name, grid=(nb + 1,),
        in_specs=[pl.BlockSpec(memory_space=pltpu.SMEM), pl.BlockSpec(memory_space=pl.ANY),
                  pl.BlockSpec((WINDOW, 512), lambda n: (cur(n), C_QA // 512)),
                  pl.BlockSpec((WINDOW, 512), lambda n: (cur(n), C_ZA // 512)),
                  pl.BlockSpec((WINDOW, 128), lambda n: (cur(n), C_KA // 128)),
                  pl.BlockSpec((WINDOW, 128), lambda n: (prev(n), C_KA // 128)),
                  pl.BlockSpec((WINDOW, 128), lambda n: (cur(n), C_VA // 128)),
                  pl.BlockSpec((WINDOW, 128), lambda n: (prev(n), C_VA // 128)),
                  pl.BlockSpec((WINDOW, 512), lambda n: (cur(n), 0)),
                  small(64), small(64)],
        out_specs=[pl.BlockSpec((WINDOW, 1024), lambda n: (cur(n), C_QA // 1024)),
                   pl.BlockSpec((WINDOW, 256), lambda n: (prev(n), 0)),
                   small(64), small(64), small(8)],
        out_shape=[jax.ShapeDtypeStruct(dproj.shape, F32), jax.ShapeDtypeStruct((T, 256), F32),
                   jax.ShapeDtypeStruct((1, 64), F32), jax.ShapeDtypeStruct((1, 64), F32),
                   jax.ShapeDtypeStruct((1, 8), F32)],
        scratch_shapes=[pltpu.VMEM((WINDOW, 128), F32), pltpu.VMEM((WINDOW, 128), F32)],
        input_output_aliases={1: 0},
        compiler_params=_cparams(("arbitrary",)),
    )(sinks, dproj, proj, proj, proj, proj, proj, proj, dya, _row(q_norm_g), _row(k_norm_g))


HALO_B = 32


def _conf_specs(T, tm):
    r = tm // HALO_B
    cur = lambda c: pl.BlockSpec((tm, 512), lambda i: (i, c // 512))
    prev = lambda c: pl.BlockSpec((HALO_B, 512), lambda i: (jnp.maximum(i * r - 1, 0), c // 512))
    return cur, prev


def _conf_core(i, tm, gv_ref, gg_ref, gvp_ref, ggp_ref, w_ref, b_ref, lg_ref, lb_ref, pw_ref, pb_ref, ext_ref):
    up = gvp_ref[...] * _sig(ggp_ref[...])
    ext_ref[:HALO_B] = jnp.where(i > 0, up, 0.0)
    ext_ref[HALO_B:] = gv_ref[...] * _sig(gg_ref[...])
    acc = jnp.zeros((tm, 512), F32) + b_ref[...]
    for k in range(CONV_K):
        acc = acc + w_ref[k:k + 1, :] * ext_ref[pl.ds(HALO_B - CONV_K + 1 + k, tm), :]
    mu = jnp.mean(acc, axis=-1, keepdims=True)
    xc = acc - mu
    rstd = lax.rsqrt(jnp.mean(xc * xc, axis=-1, keepdims=True) + EPS)
    xh = xc * rstd
    u2 = xh * lg_ref[...] + lb_ref[...]
    u3 = _silu(u2)
    ypre = _bdot(u3, pw_ref[...]) + pb_ref[...]
    return xh, rstd, u2, u3, ypre


def _conf_fwd(proj, dw_w, dw_b, ln_g, ln_b, pw2, pw2_b, name):
    T = proj.shape[0]
    tm = min(512, T)
    cur, prev = _conf_specs(T, tm)

    def body(gv_ref, gg_ref, gvp_ref, ggp_ref, zb_ref, w_ref, b_ref, lg_ref, lb_ref, pw_ref, pb_ref, o_ref, ext_ref):
        i = pl.program_id(0)
        ypre = _conf_core(i, tm, gv_ref, gg_ref, gvp_ref, ggp_ref, w_ref, b_ref, lg_ref, lb_ref, pw_ref, pb_ref,
                          ext_ref)[4]
        o_ref[...] = ypre * _silu(zb_ref[...])

    full = lambda s: pl.BlockSpec(s, lambda i: (0, 0))
    return pl.pallas_call(
        body, name=name, grid=(T // tm,),
        in_specs=[cur(C_GV), cur(C_GG), prev(C_GV), prev(C_GG), cur(C_ZB), full((CONV_K, 512)), full((1, 512)),
                  full((1, 512)), full((1, 512)), full((512, 512)), full((1, 512))],
        out_specs=pl.BlockSpec((tm, 512), lambda i: (i, 0)),
        out_shape=jax.ShapeDtypeStruct((T, 512), F32),
        scratch_shapes=[pltpu.VMEM((tm + HALO_B, 512), F32)],
        compiler_params=_cparams(("parallel",)),
    )(proj, proj, proj, proj, proj, dw_w, _row(dw_b), _row(ln_g), _row(ln_b), pw2, _row(pw2_b))


def _conf_bwd1(dproj, proj, dyb, dw_w, dw_b, ln_g, ln_b, pw2, pw2_b, name):
    T = proj.shape[0]
    tm = min(512, T)
    cur, prev = _conf_specs(T, tm)

    def body(dp_any, gv_ref, gg_ref, gvp_ref, ggp_ref, zb_ref, dy_ref, w_ref, b_ref, lg_ref, lb_ref, pw_ref, pb_ref,
             dzb_ref, du1_ref, gpw_ref, st_ref, ext_ref):
        i = pl.program_id(0)
        xh, rstd, u2, u3, ypre = _conf_core(i, tm, gv_ref, gg_ref, gvp_ref, ggp_ref, w_ref, b_ref, lg_ref, lb_ref,
                                            pw_ref, pb_ref, ext_ref)
        zb, dy = zb_ref[...], dy_ref[...]
        dzb_ref[...] = dy * ypre * _dsilu(zb)
        dyp = dy * _silu(zb)
        du2 = _bdot(dyp, pw_ref[...], NT) * _dsilu(u2)
        dxh = du2 * lg_ref[...]
        du1 = rstd * (dxh - jnp.mean(dxh, axis=-1, keepdims=True) - xh * jnp.mean(dxh * xh, axis=-1, keepdims=True))
        du1_ref[...] = du1
        gpw = _bdot(u3, dyp, TN)
        rs = lambda a: jnp.sum(a, axis=0, keepdims=True)
        upd = jnp.concatenate([rs(dyp), rs(du2 * xh), rs(du2), rs(du1), jnp.zeros((4, 512), F32)], axis=0)

        @pl.when(i == 0)
        def _():
            gpw_ref[...] = gpw
            st_ref[...] = upd

        @pl.when(i > 0)
        def _():
            gpw_ref[...] += gpw
            st_ref[...] += upd

    full = lambda s: pl.BlockSpec(s, lambda i: (0, 0))
    blk = pl.BlockSpec((tm, 512), lambda i: (i, 0))
    return pl.pallas_call(
        body, name=name, grid=(T // tm,),
        in_specs=[pl.BlockSpec(memory_space=pl.ANY), cur(C_GV), cur(C_GG), prev(C_GV), prev(C_GG), cur(C_ZB), blk,
                  full((CONV_K, 512)), full((1, 512)), full((1, 512)), full((1, 512)), full((512, 512)), full((1, 512))],
        out_specs=[cur(C_ZB), blk, full((512, 512)), full((8, 512))],
        out_shape=[jax.ShapeDtypeStruct(dproj.shape, F32), jax.ShapeDtypeStruct((T, 512), F32),
                   jax.ShapeDtypeStruct((512, 512), F32), jax.ShapeDtypeStruct((8, 512), F32)],
        scratch_shapes=[pltpu.VMEM((tm + HALO_B, 512), F32)],
        input_output_aliases={0: 0},
        compiler_params=_cparams(("arbitrary",)),
    )(dproj, proj, proj, proj, proj, proj, dyb, dw_w, _row(dw_b), _row(ln_g), _row(ln_b), pw2, _row(pw2_b))


def _conf_bwd2(dproj, proj, du1, dw_w, name):
    T = proj.shape[0]
    tm = min(512, T)
    nt = T // tm
    r = tm // HALO_B
    cur, prev = _conf_specs(T, tm)

    def body(dp_any, gv_ref, gg_ref, gvp_ref, ggp_ref, du_ref, dun_ref, w_ref, dglu_ref, gw_ref, extu_ref, extd_ref):
        i = pl.program_id(0)
        gv, sg = gv_ref[...], _sig(gg_ref[...])
        extu_ref[:HALO_B] = jnp.where(i > 0, gvp_ref[...] * _sig(ggp_ref[...]), 0.0)
        extu_ref[HALO_B:] = gv * sg
        du1 = du_ref[...]
        extd_ref[:tm] = du1
        extd_ref[tm:] = jnp.where(i < nt - 1, dun_ref[...], 0.0)
        du0 = jnp.zeros((tm, 512), F32)
        rows = []
        for k in range(CONV_K):
            du0 = du0 + w_ref[k:k + 1, :] * extd_ref[pl.ds(CONV_K - 1 - k, tm), :]
            rows.append(jnp.sum(du1 * extu_ref[pl.ds(HALO_B - CONV_K + 1 + k, tm), :], axis=0, keepdims=True))
        rows.append(jnp.zeros((1, 512), F32))
        gw = jnp.concatenate(rows, axis=0)
        dglu_ref[:, :512] = du0 * sg
        dglu_ref[:, 512:] = du0 * gv * sg * (1.0 - sg)

        @pl.when(i == 0)
        def _():
            gw_ref[...] = gw

        @pl.when(i > 0)
        def _():
            gw_ref[...] += gw

    full = lambda s: pl.BlockSpec(s, lambda i: (0, 0))
    return pl.pallas_call(
        body, name=name, grid=(nt,),
        in_specs=[pl.BlockSpec(memory_space=pl.ANY), cur(C_GV), cur(C_GG), prev(C_GV), prev(C_GG),
                  pl.BlockSpec((tm, 512), lambda i: (i, 0)),
                  pl.BlockSpec((HALO_B, 512), lambda i: (jnp.minimum((i + 1) * r, T // HALO_B - 1), 0)),
                  full((CONV_K, 512))],
        out_specs=[pl.BlockSpec((tm, 1024), lambda i: (i, C_GV // 1024)), full((32, 512))],
        out_shape=[jax.ShapeDtypeStruct(dproj.shape, F32), jax.ShapeDtypeStruct((32, 512), F32)],
        scratch_shapes=[pltpu.VMEM((tm + HALO_B, 512), F32), pltpu.VMEM((tm + HALO_B, 512), F32)],
        input_output_aliases={0: 0},
        compiler_params=_cparams(("arbitrary",)),
    )(dproj, proj, proj, proj, proj, du1, du1, dw_w)


HALO_C = 8
QKV_C = 1536


def _softplus(x):
    return jnp.maximum(x, 0.0) + jnp.log1p(jnp.exp(-jnp.abs(x)))


def _gdn_conv(i, tm, x_ref, xp_ref, w_ref, ext_ref):
    ext_ref[:HALO_C] = jnp.where(i > 0, xp_ref[...], 0.0)
    ext_ref[HALO_C:] = x_ref[...]
    pre = jnp.zeros((tm, QKV_C), F32)
    for k in range(DN_K):
        pre = pre + w_ref[k:k + 1, :] * ext_ref[pl.ds(HALO_C - DN_K + 1 + k, tm), :]
    return pre


def _gdn_specs(T, tm):
    r = tm // HALO_C
    cur = pl.BlockSpec((tm, QKV_C), lambda i: (i, C_QC // QKV_C))
    prev = pl.BlockSpec((HALO_C, QKV_C), lambda i: (jnp.maximum(i * r - 1, 0), C_QC // QKV_C))
    ab = pl.BlockSpec((tm, 128), lambda i: (i, C_AB // 128))
    return cur, prev, ab


def _gdn_prep_fwd(proj, sconv_w, alog_v, dtb_v, name):
    T = proj.shape[0]
    tm = min(512, T)
    cur, prev, ab = _gdn_specs(T, tm)

    def body(x_ref, xp_ref, ab_ref, w_ref, al_ref, dt_ref, q_ref, k_ref, v_ref, gb_ref, ext_ref):
        i = pl.program_id(0)
        y = _silu(_gdn_conv(i, tm, x_ref, xp_ref, w_ref, ext_ref))
        for h in range(DN_HEADS):
            sl = slice(128 * h, 128 * h + 128)
            qh, kh = y[:, sl], y[:, 512 + 128 * h:512 + 128 * h + 128]
            q_ref[:, sl] = qh * lax.rsqrt(jnp.sum(qh * qh, axis=-1, keepdims=True) + EPS) * (128 ** -0.5)
            k_ref[:, sl] = kh * lax.rsqrt(jnp.sum(kh * kh, axis=-1, keepdims=True) + EPS)
        v_ref[...] = y[:, 1024:]
        abv = ab_ref[...]
        lane = lax.broadcasted_iota(jnp.int32, (tm, 128), 1)
        g = -jnp.exp(al_ref[...]) * _softplus(abv + dt_ref[...])
        gb_ref[...] = jnp.where(lane < DN_HEADS, g, _sig(abv))

    full = lambda s: pl.BlockSpec(s, lambda i: (0, 0))
    blk = pl.BlockSpec((tm, 512), lambda i: (i, 0))
    return pl.pallas_call(
        body, name=name, grid=(T // tm,),
        in_specs=[cur, prev, ab, full((DN_K, QKV_C)), full((1, 128)), full((1, 128))],
        out_specs=[blk, blk, blk, pl.BlockSpec((tm, 128), lambda i: (i, 0))],
        out_shape=[jax.ShapeDtypeStruct((T, 512), F32)] * 3 + [jax.ShapeDtypeStruct((T, 128), F32)],
        scratch_shapes=[pltpu.VMEM((tm + HALO_C, QKV_C), F32)],
        compiler_params=_cparams(("parallel",)),
    )(proj, proj, proj, sconv_w, alog_v, dtb_v)


def _hdot(a, b, dims=NN):
    return _dot(a, b, dims, precision=HI)


def _lockstep(gens):
    results, live = [None] * len(gens), list(range(len(gens)))
    while live:
        for i in list(live):
            try:
                next(gens[i])
            except StopIteration as stop:
                results[i] = stop.value
                live.remove(i)
    return results


def _split(a):
    hi = a.astype(BF16)
    return hi, (a - hi.astype(F32)).astype(BF16)


def _dot_exact(a, b, dims=NN, split_left=True):
    x = (a if split_left else b).astype(F32)
    hi = x.astype(BF16)
    r = x - hi.astype(F32)
    mid = r.astype(BF16)
    lo = (r - mid.astype(F32)).astype(BF16)
    other = (b if split_left else a).astype(BF16)
    one = (lambda p: _dot(p, other, dims)) if split_left else (lambda p: _dot(other, p, dims))
    return (one(lo) + one(mid)) + one(hi)


def _dot3(a, b):
    (ah, al), (bh, bl) = a, b
    return _dot(ah, bh) + (_dot(ah, bl) + _dot(al, bh))


def _tri_inv(mats, eye):
    ps = [-a for a in mats]
    ts = [eye + p for p in ps]
    for _ in range(5):
        sp = [_split(p) for p in ps]
        ps = [_dot3(s, s) for s in sp]
        sp = [_split(p) for p in ps]
        ts = [t + _dot3(_split(t), s) for t, s in zip(ts, sp)]
    return ts


def _tri_consts():
    ii = lax.broadcasted_iota(jnp.int32, (CHUNK, CHUNK), 0)
    jj = lax.broadcasted_iota(jnp.int32, (CHUNK, CHUNK), 1)
    return ii >= jj, ii > jj, (ii == jj).astype(F32)


def _gdn_local(q, k, v, gcol, grow, bcol, lower, strict):
    dm = jnp.where(lower, jnp.exp(jnp.where(lower, gcol - grow, 0.0)), 0.0)
    kb = k * bcol
    a = jnp.where(strict, _bdot(kb, k, NT) * dm, 0.0)
    gc = jnp.exp(gcol)
    glast = grow[:, CHUNK - 1:CHUNK]
    return dict(q=q, k=k, v=v, bcol=bcol, gcol=gcol, glast=glast, dm=dm, kb=kb, a=a, gc=gc, vb=v * bcol,
                kbg=kb * gc, p=_bdot(q, k, NT) * dm, qe=q * gc, ke=k * jnp.exp(glast - gcol))


def _gdn_chunk_bwd(c, do, dvn, ds_new, lower, strict, ones):
    rs = lambda m: jnp.sum(m, axis=-1, keepdims=True)
    colsum = lambda m: _dot_exact(m, ones, TN)[:, :1]
    q, k, v, bcol, dm, tm, gc, s = c["q"], c["k"], c["v"], c["bcol"], c["dm"], c["tm"], c["gc"], c["s"]
    eg = jnp.exp(c["glast"])
    dqe = _bdot(do, s, NT)
    dp = jnp.where(lower, _bdot(do, c["vn"], NT), 0.0)
    dw = -_bdot(dvn, s, NT)
    dke = _bdot(c["vn"], ds_new, NT)
    dvb = _bdot(tm, dvn, TN)
    yield
    dglast = jnp.sum(rs(ds_new * s), axis=0, keepdims=True) * eg
    dk = dke * jnp.exp(c["glast"] - c["gcol"])
    r_ke = rs(dke * c["ke"])
    dglast = dglast + jnp.sum(r_ke, axis=0, keepdims=True)
    dgam = rs(dqe * c["qe"]) - r_ke
    dq = dqe * gc
    dpm = dp * dm
    mp = dp * c["p"]
    dq = dq + _bdot(dpm, k)
    dk = dk + _bdot(dpm, q, TN)
    dt = _bdot(dvn, c["vb"], NT) + _bdot(dw, c["kbg"], NT)
    dkbg = _bdot(tm, dw, TN)
    dgam = dgam + rs(mp) - colsum(mp)
    yield
    dkb = dkbg * gc
    dgam = dgam + rs(dkbg * c["kbg"])
    dat = _bdot(tm, dt, TN)
    yield
    da = jnp.where(strict, -_bdot(dat, tm, NT), 0.0)
    yield
    dam = da * dm
    ma = da * c["a"]
    dkb = dkb + _bdot(dam, k)
    dk = dk + _bdot(dam, c["kb"], TN)
    dgam = dgam + rs(ma) - colsum(ma)
    yield
    dk = dk + dkb * bcol
    dbeta = rs(dkb * k) + rs(dvb * v)
    dv = dvb * bcol
    row = lax.broadcasted_iota(jnp.int32, (CHUNK, 1), 0)
    dgam = dgam + jnp.where(row == CHUNK - 1, dglast, 0.0)
    dg = _dot_exact(lower, dgam, TN, split_left=False)
    return dq, dk, dv, dg, dbeta


GROUP = 2


def _chunk_decay(gb_ref, gt_ref, lmat, g):
    rows = slice(CHUNK * g, CHUNK * g + CHUNK)
    return rows, _dot_exact(lmat, gb_ref[rows, :], split_left=False), _dot_exact(gt_ref[g], lmat, NT)


def _gdn_chunk_fwd(qd, kd, vd, gb, gbt, name):
    T = qd.shape[0]
    G = GROUP
    ng = T // (CHUNK * G)

    def body(q_ref, k_ref, v_ref, gb_ref, gt_ref, u_ref, w_ref, qe_ref, ke_ref, p_ref, t_ref, eg_ref):
        lower, strict, eye = _tri_consts()
        lmat = lower.astype(F32)
        decay = [_chunk_decay(gb_ref, gt_ref, lmat, g) for g in range(G)]
        chains = [(g, h) for g in range(G) for h in range(DN_HEADS)]
        cs = []
        for g, h in chains:
            rows, gcs, grs = decay[g]
            sl = slice(128 * h, 128 * h + 128)
            c = _gdn_local(q_ref[rows, sl], k_ref[rows, sl], v_ref[rows, sl], gcs[:, h:h + 1], grs[h:h + 1, :],
                           gb_ref[rows, DN_HEADS + h:DN_HEADS + h + 1], lower, strict)
            qe_ref[rows, sl] = c["qe"].astype(BF16)
            ke_ref[rows, sl] = c["ke"].astype(BF16)
            p_ref[rows, 64 * h:64 * h + 64] = c["p"].astype(BF16)
            eg_ref[g, h:h + 1, :] = jnp.broadcast_to(jnp.exp(c["glast"]), (1, 128))
            cs.append(c)
        tms = [t.astype(BF16) for t in _tri_inv([c["a"] for c in cs], eye)]
        us = [_dot(t, c["vb"].astype(BF16)) for t, c in zip(tms, cs)]
        ws = [_dot(t, c["kbg"].astype(BF16)) for t, c in zip(tms, cs)]
        for (g, h), tm, u, w in zip(chains, tms, us, ws):
            rows, sl = decay[g][0], slice(128 * h, 128 * h + 128)
            u_ref[rows, sl] = u
            w_ref[rows, sl] = w.astype(BF16)
            t_ref[rows, 64 * h:64 * h + 64] = tm
        for g in range(G):
            eg_ref[g, DN_HEADS:, :] = jnp.zeros((8 - DN_HEADS, 128), F32)

    blk = pl.BlockSpec((CHUNK * G, 512), lambda n: (n, 0))
    half = pl.BlockSpec((CHUNK * G, 256), lambda n: (n, 0))
    return pl.pallas_call(
        body, name=name, grid=(ng,),
        in_specs=[blk, blk, blk, pl.BlockSpec((CHUNK * G, 128), lambda n: (n, 0)),
                  pl.BlockSpec((G, 8, CHUNK), lambda n: (n, 0, 0))],
        out_specs=[blk, blk, blk, blk, half, half, pl.BlockSpec((G, 8, 128), lambda n: (n, 0, 0))],
        out_shape=[jax.ShapeDtypeStruct((T, 512), F32)] + [jax.ShapeDtypeStruct((T, 512), BF16)] * 3
        + [jax.ShapeDtypeStruct((T, 256), BF16)] * 2 + [jax.ShapeDtypeStruct((T // CHUNK, 8, 128), F32)],
        compiler_params=_cparams(("parallel",)),
    )(qd, kd, vd, gb, gbt)


def _gdn_scan_fwd(u, w, qe, ke, pm, eg, proj, dn_g, name):
    T = u.shape[0]
    nc = T // CHUNK

    def body(u_ref, w_ref, qe_ref, ke_ref, p_ref, eg_ref, z_ref, ng_ref, y_ref, o_ref, vn_ref, ss_ref, s_ref):
        n = pl.program_id(0)

        @pl.when(n == 0)
        def _():
            s_ref[...] = jnp.zeros_like(s_ref)

        ss_ref[0] = s_ref[...]

        def head(h):
            sl = slice(128 * h, 128 * h + 128)
            s = s_ref[h]
            sb = s.astype(BF16)
            vn = u_ref[:, sl] - _dot(w_ref[:, sl], sb)
            qs = _dot(qe_ref[:, sl], sb)
            yield
            vb = vn.astype(BF16)
            o = qs + _dot(p_ref[:, 64 * h:64 * h + 64], vb)
            s_ref[h] = s * eg_ref[0, h:h + 1, :] + _dot(ke_ref[:, sl], vb, TN)
            yield
            vn_ref[:, sl] = vb
            o_ref[:, sl] = o
            y_ref[:, sl] = _rms(o, None)[0] * ng_ref[...] * _silu(z_ref[:, sl])

        _lockstep([head(h) for h in range(DN_HEADS)])

    blk = pl.BlockSpec((CHUNK, 512), lambda n: (n, 0))
    return pl.pallas_call(
        body, name=name, grid=(nc,),
        in_specs=[blk, blk, blk, blk, pl.BlockSpec((CHUNK, 256), lambda n: (n, 0)),
                  pl.BlockSpec((1, 8, 128), lambda n: (n, 0, 0)),
                  pl.BlockSpec((CHUNK, 512), lambda n: (n, C_ZC // 512)), pl.BlockSpec((1, 128), lambda n: (0, 0))],
        out_specs=[blk, blk, blk, pl.BlockSpec((1, DN_HEADS, 128, 128), lambda n: (n, 0, 0, 0))],
        out_shape=[jax.ShapeDtypeStruct((T, 512), F32), jax.ShapeDtypeStruct((T, 512), F32),
                   jax.ShapeDtypeStruct((T, 512), BF16), jax.ShapeDtypeStruct((nc, DN_HEADS, 128, 128), F32)],
        scratch_shapes=[pltpu.VMEM((DN_HEADS, 128, 128), F32)],
        compiler_params=_cparams(("arbitrary",)),
    )(u, w, qe, ke, pm, eg, proj, dn_g)


def _gdn_scan_bwd(dproj, w, qe, ke, pm, eg, o, proj, dyc, dn_g, name):
    T = o.shape[0]
    nc = T // CHUNK
    rev = lambda n: nc - 1 - n

    def body(dp_any, w_ref, qe_ref, ke_ref, p_ref, eg_ref, o_ref, z_ref, dy_ref, ng_ref,
             dz_ref, do_ref, dvn_ref, dsn_ref, gng_ref, ds_ref):
        n = pl.program_id(0)

        @pl.when(n == 0)
        def _():
            ds_ref[...] = jnp.zeros_like(ds_ref)
            gng_ref[...] = jnp.zeros_like(gng_ref)

        dsn_ref[0] = ds_ref[...]

        def head(h):
            sl = slice(128 * h, 128 * h + 128)
            oh, r = _rms(o_ref[:, sl], None)
            z, dy = z_ref[:, sl], dy_ref[:, sl]
            dz_ref[:, sl] = dy * (oh * ng_ref[...]) * _dsilu(z)
            do, gg = _rms_bwd(dy * _silu(z), oh, r, ng_ref[...])
            dob = do.astype(BF16)
            ds = ds_ref[h]
            dvn = _dot(p_ref[:, 64 * h:64 * h + 64], dob, TN) + _dot(ke_ref[:, sl], ds.astype(BF16))
            qd = _dot(qe_ref[:, sl], dob, TN)
            yield
            dvb = dvn.astype(BF16)
            ds_ref[h] = qd + eg_ref[0, h:h + 1, :] * ds - _dot(w_ref[:, sl], dvb, TN)
            do_ref[:, sl] = dob
            dvn_ref[:, sl] = dvb
            return jnp.sum(gg, axis=0, keepdims=True)

        gng = _lockstep([head(h) for h in range(DN_HEADS)])
        gng_ref[...] += (gng[0] + gng[1]) + (gng[2] + gng[3])

    blk = pl.BlockSpec((CHUNK, 512), lambda n: (rev(n), 0))
    state = pl.BlockSpec((1, DN_HEADS, 128, 128), lambda n: (rev(n), 0, 0, 0))
    return pl.pallas_call(
        body, name=name, grid=(nc,),
        in_specs=[pl.BlockSpec(memory_space=pl.ANY), blk, blk, blk, pl.BlockSpec((CHUNK, 256), lambda n: (rev(n), 0)),
                  pl.BlockSpec((1, 8, 128), lambda n: (rev(n), 0, 0)), blk,
                  pl.BlockSpec((CHUNK, 512), lambda n: (rev(n), C_ZC // 512)), blk,
                  pl.BlockSpec((1, 128), lambda n: (0, 0))],
        out_specs=[pl.BlockSpec((CHUNK, 512), lambda n: (rev(n), C_ZC // 512)), blk, blk, state,
                   pl.BlockSpec((1, 128), lambda n: (0, 0))],
        out_shape=[jax.ShapeDtypeStruct(dproj.shape, F32), jax.ShapeDtypeStruct((T, 512), BF16),
                   jax.ShapeDtypeStruct((T, 512), BF16), jax.ShapeDtypeStruct((nc, DN_HEADS, 128, 128), F32),
                   jax.ShapeDtypeStruct((1, 128), F32)],
        scratch_shapes=[pltpu.VMEM((DN_HEADS, 128, 128), F32)],
        input_output_aliases={0: 0},
        compiler_params=_cparams(("arbitrary",)),
    )(dproj, w, qe, ke, pm, eg, o, proj, dyc, dn_g)


def _gdn_chunk_grad(qd, kd, vd, gb, gbt, tmi, ssave, dsn, do, dvn, vn, name):
    T = qd.shape[0]
    G = GROUP
    ng = T // (CHUNK * G)

    def body(q_ref, k_ref, v_ref, gb_ref, gt_ref, t_ref, ss_ref, dsn_ref, do_ref, dvn_ref, vn_ref,
             dq_ref, dk_ref, dv_ref, dgb_ref):
        lower, strict, _ = _tri_consts()
        lmat = lower.astype(F32)
        ones = jnp.ones((CHUNK, 128), F32)
        lane = lax.broadcasted_iota(jnp.int32, (CHUNK, 128), 1)
        decay = [_chunk_decay(gb_ref, gt_ref, lmat, g) for g in range(G)]
        chains = [(g, h) for g in range(G) for h in range(DN_HEADS)]
        gens = []
        for g, h in chains:
            rows, gcs, grs = decay[g]
            sl = slice(128 * h, 128 * h + 128)
            c = _gdn_local(q_ref[rows, sl], k_ref[rows, sl], v_ref[rows, sl], gcs[:, h:h + 1], grs[h:h + 1, :],
                           gb_ref[rows, DN_HEADS + h:DN_HEADS + h + 1], lower, strict)
            c.update(tm=t_ref[rows, 64 * h:64 * h + 64], s=ss_ref[g, h], vn=vn_ref[rows, sl])
            gens.append(_gdn_chunk_bwd(c, do_ref[rows, sl], dvn_ref[rows, sl], dsn_ref[g, h], lower, strict, ones))
        dgb = [jnp.zeros((CHUNK, 128), F32) for _ in range(G)]
        for (g, h), (dq, dk, dv, dg, dbeta) in zip(chains, _lockstep(gens)):
            rows, sl = decay[g][0], slice(128 * h, 128 * h + 128)
            dq_ref[rows, sl], dk_ref[rows, sl], dv_ref[rows, sl] = dq, dk, dv
            dgb[g] = dgb[g] + jnp.where(lane == h, dg, 0.0) + jnp.where(lane == DN_HEADS + h, dbeta, 0.0)
        for g in range(G):
            dgb_ref[decay[g][0], :] = dgb[g]

    blk = pl.BlockSpec((CHUNK * G, 512), lambda n: (n, 0))
    half = pl.BlockSpec((CHUNK * G, 256), lambda n: (n, 0))
    nar = pl.BlockSpec((CHUNK * G, 128), lambda n: (n, 0))
    state = pl.BlockSpec((G, DN_HEADS, 128, 128), lambda n: (n, 0, 0, 0))
    return pl.pallas_call(
        body, name=name, grid=(ng,),
        in_specs=[blk, blk, blk, nar, pl.BlockSpec((G, 8, CHUNK), lambda n: (n, 0, 0)), half, state, state,
                  blk, blk, blk],
        out_specs=[blk, blk, blk, nar],
        out_shape=[jax.ShapeDtypeStruct((T, 512), F32)] * 3 + [jax.ShapeDtypeStruct((T, 128), F32)],
        compiler_params=_cparams(("parallel",)),
    )(qd, kd, vd, gb, gbt, tmi, ssave, dsn, do, dvn, vn)


def _gdn_prep_bwd1(dproj, proj, dqd, dkd, dvd, dgb, dkv_a, sconv_w, alog_v, dtb_v, name):
    T = proj.shape[0]
    tm = min(512, T)
    cur, prev, ab = _gdn_specs(T, tm)

    def body(dp_any, x_ref, xp_ref, ab_ref, dq_ref, dk_ref, dv_ref, dgb_ref, dkv_ref, w_ref, al_ref, dt_ref,
             o_ref, dpre_ref, st_ref, ext_ref):
        i = pl.program_id(0)
        pre = _gdn_conv(i, tm, x_ref, xp_ref, w_ref, ext_ref)
        y, dsl = _silu(pre), _dsilu(pre)
        for h in range(DN_HEADS):
            for base, g_ref, scale in ((0, dq_ref, 128 ** -0.5), (512, dk_ref, 1.0)):
                sl = slice(base + 128 * h, base + 128 * h + 128)
                xh = y[:, sl]
                r = lax.rsqrt(jnp.sum(xh * xh, axis=-1, keepdims=True) + EPS)
                xn = xh * r
                gy = g_ref[:, 128 * h:128 * h + 128]
                dpre_ref[:, sl] = (scale * r) * (gy - xn * jnp.sum(gy * xn, axis=-1, keepdims=True)) * dsl[:, sl]
        dpre_ref[:, 1024:] = dv_ref[...] * dsl[:, 1024:]
        abv, dgb = ab_ref[...], dgb_ref[...]
        lane = lax.broadcasted_iota(jnp.int32, (tm, 128), 1)
        na = -jnp.exp(al_ref[...])
        xs = abv + dt_ref[...]
        da = dgb * na * _sig(xs)
        b = _sig(abv)
        o_ref[:, :256] = dkv_ref[...]
        o_ref[:, 256:] = jnp.where(lane < DN_HEADS, da, jnp.where(lane < 2 * DN_HEADS, dgb * b * (1.0 - b), 0.0))
        head = lane < DN_HEADS
        upd = jnp.concatenate([jnp.sum(jnp.where(head, dgb * na * _softplus(xs), 0.0), axis=0, keepdims=True),
                               jnp.sum(jnp.where(head, da, 0.0), axis=0, keepdims=True), jnp.zeros((6, 128), F32)],
                              axis=0)

        @pl.when(i == 0)
        def _():
            st_ref[...] = upd

        @pl.when(i > 0)
        def _():
            st_ref[...] += upd

    full = lambda s: pl.BlockSpec(s, lambda i: (0, 0))
    blk = pl.BlockSpec((tm, 512), lambda i: (i, 0))
    return pl.pallas_call(
        body, name=name, grid=(T // tm,),
        in_specs=[pl.BlockSpec(memory_space=pl.ANY), cur, prev, ab, blk, blk, blk,
                  pl.BlockSpec((tm, 128), lambda i: (i, 0)), pl.BlockSpec((tm, 256), lambda i: (i, 0)),
                  full((DN_K, QKV_C)), full((1, 128)), full((1, 128))],
        out_specs=[pl.BlockSpec((tm, 384), lambda i: (i, C_KA // 384)),
                   pl.BlockSpec((tm, QKV_C), lambda i: (i, 0)), full((8, 128))],
        out_shape=[jax.ShapeDtypeStruct(dproj.shape, F32), jax.ShapeDtypeStruct((T, QKV_C), F32),
                   jax.ShapeDtypeStruct((8, 128), F32)],
        scratch_shapes=[pltpu.VMEM((tm + HALO_C, QKV_C), F32)],
        input_output_aliases={0: 0},
        compiler_params=_cparams(("arbitrary",)),
    )(dproj, proj, proj, proj, dqd, dkd, dvd, dgb, dkv_a, sconv_w, alog_v, dtb_v)


def _gdn_prep_bwd2(dproj, proj, dpre, sconv_w, name):
    T = proj.shape[0]
    tm = min(512, T)
    nt = T // tm
    r = tm // HALO_C
    cur, prev, _ = _gdn_specs(T, tm)

    def body(dp_any, x_ref, xp_ref, d_ref, dn_ref, w_ref, dx_ref, gw_ref, extx_ref, extd_ref):
        i = pl.program_id(0)
        extx_ref[:HALO_C] = jnp.where(i > 0, xp_ref[...], 0.0)
        extx_ref[HALO_C:] = x_ref[...]
        d = d_ref[...]
        extd_ref[:tm] = d
        extd_ref[tm:] = jnp.where(i < nt - 1, dn_ref[...], 0.0)
        dx = jnp.zeros((tm, QKV_C), F32)
        rows = []
        for k in range(DN_K):
            dx = dx + w_ref[k:k + 1, :] * extd_ref[pl.ds(DN_K - 1 - k, tm), :]
            rows.append(jnp.sum(d * extx_ref[pl.ds(HALO_C - DN_K + 1 + k, tm), :], axis=0, keepdims=True))
        rows.append(jnp.zeros((8 - DN_K, QKV_C), F32))
        gw = jnp.concatenate(rows, axis=0)
        dx_ref[...] = dx

        @pl.when(i == 0)
        def _():
            gw_ref[...] = gw

        @pl.when(i > 0)
        def _():
            gw_ref[...] += gw

    full = lambda s: pl.BlockSpec(s, lambda i: (0, 0))
    return pl.pallas_call(
        body, name=name, grid=(nt,),
        in_specs=[pl.BlockSpec(memory_space=pl.ANY), cur, prev, pl.BlockSpec((tm, QKV_C), lambda i: (i, 0)),
                  pl.BlockSpec((HALO_C, QKV_C), lambda i: (jnp.minimum((i + 1) * r, T // HALO_C - 1), 0)),
                  full((DN_K, QKV_C))],
        out_specs=[cur, full((8, QKV_C))],
        out_shape=[jax.ShapeDtypeStruct(dproj.shape, F32), jax.ShapeDtypeStruct((8, QKV_C), F32)],
        scratch_shapes=[pltpu.VMEM((tm + HALO_C, QKV_C), F32), pltpu.VMEM((tm + HALO_C, QKV_C), F32)],
        input_output_aliases={0: 0},
        compiler_params=_cparams(("arbitrary",)),
    )(dproj, proj, proj, dpre, dpre, sconv_w)


def _merge_fwd(x, proj, ya, yb, yc, wa, wb, wc, wo, gate, name):
    T = x.shape[0]
    tm = min(256, T)

    def body(x_ref, mg_ref, ya_ref, yb_ref, yc_ref, wa_ref, wb_ref, wc_ref, wo_ref, gate_ref, o_ref):
        merged = (_sig(mg_ref[:, :D]) * _bdot(ya_ref[...], wa_ref[...])
                  + _sig(mg_ref[:, D:2 * D]) * _bdot(yb_ref[...], wb_ref[...])
                  + _sig(mg_ref[:, 2 * D:]) * _bdot(yc_ref[...], wc_ref[...]))
        o_ref[...] = x_ref[...] + gate_ref[...] * _bdot(merged, wo_ref[...])

    full = lambda s: pl.BlockSpec(s, lambda i: (0, 0))
    yb_ = pl.BlockSpec((tm, 512), lambda i: (i, 0))
    return pl.pallas_call(
        body, name=name, grid=(T // tm,),
        in_specs=[pl.BlockSpec((tm, D), lambda i: (i, 0)), pl.BlockSpec((tm, 3 * D), lambda i: (i, 0)), yb_, yb_, yb_,
                  full((512, D)), full((512, D)), full((512, D)), full((D, D)), full((1, D))],
        out_specs=pl.BlockSpec((tm, D), lambda i: (i, 0)),
        out_shape=jax.ShapeDtypeStruct((T, D), F32),
        compiler_params=_cparams(("parallel",)),
    )(x, proj, ya, yb, yc, wa, wb, wc, wo, _row(gate))


def _merge_bwd(dout, proj, ya, yb, yc, wa, wb, wc, wo, gate, name):
    T = dout.shape[0]
    tm = min(128, T)
    nt = T // tm

    def body(do_ref, mg_ref, ya_ref, yb_ref, yc_ref, wa_ref, wb_ref, wc_ref, wo_ref, gate_ref,
             dmg_ref, dya_ref, dyb_ref, dyc_ref, gwa_hbm, gwb_hbm, gwc_hbm, gwo_hbm, gg_ref,
             gwa_ref, gwb_ref, gwc_ref, gwo_ref):
        i = pl.program_id(0)

        @pl.when(i == 0)
        def _():
            for r in (gwa_ref, gwb_ref, gwc_ref, gwo_ref, gg_ref):
                r[...] = jnp.zeros_like(r)

        ys = (ya_ref[...], yb_ref[...], yc_ref[...])
        ws = (wa_ref, wb_ref, wc_ref)
        gs = tuple(_sig(mg_ref[:, j * D:(j + 1) * D]) for j in range(3))
        ps = tuple(_bdot(ys[j], ws[j][...]) for j in range(3))
        merged = gs[0] * ps[0] + gs[1] * ps[1] + gs[2] * ps[2]
        mo = _bdot(merged, wo_ref[...])
        do = do_ref[...]
        gg_ref[...] += jnp.sum(do * mo, axis=0, keepdims=True)
        dmo = do * gate_ref[...]
        dmerged = _bdot(dmo, wo_ref[...], NT)
        gwo_ref[...] += _bdot(merged, dmo, TN)
        for j, (dy_ref, gw_ref) in enumerate(((dya_ref, gwa_ref), (dyb_ref, gwb_ref), (dyc_ref, gwc_ref))):
            dp = dmerged * gs[j]
            dmg_ref[:, j * D:(j + 1) * D] = dmerged * ps[j] * gs[j] * (1.0 - gs[j])
            dy_ref[...] = _bdot(dp, ws[j][...], NT)
            gw_ref[...] += _bdot(ys[j], dp, TN)

        @pl.when(i == nt - 1)
        def _():
            for src, dst in ((gwa_ref, gwa_hbm), (gwb_ref, gwb_hbm), (gwc_ref, gwc_hbm), (gwo_ref, gwo_hbm)):
                pltpu.sync_copy(src, dst)

    full = lambda s: pl.BlockSpec(s, lambda i: (0, 0))
    yb_ = pl.BlockSpec((tm, 512), lambda i: (i, 0))
    anyspec = pl.BlockSpec(memory_space=pl.ANY)
    return pl.pallas_call(
        body, name=name, grid=(nt,),
        in_specs=[pl.BlockSpec((tm, D), lambda i: (i, 0)), pl.BlockSpec((tm, 3 * D), lambda i: (i, 0)), yb_, yb_, yb_,
                  full((512, D)), full((512, D)), full((512, D)), full((D, D)), full((1, D))],
        out_specs=[pl.BlockSpec((tm, 3 * D), lambda i: (i, 0)), yb_, yb_, yb_, anyspec, anyspec, anyspec, anyspec,
                   full((1, D))],
        out_shape=[jax.ShapeDtypeStruct((T, NP), F32)] + [jax.ShapeDtypeStruct((T, 512), F32)] * 3
        + [jax.ShapeDtypeStruct((512, D), F32)] * 3 + [jax.ShapeDtypeStruct((D, D), F32), jax.ShapeDtypeStruct((1, D), F32)],
        scratch_shapes=[pltpu.VMEM((512, D), F32)] * 3 + [pltpu.VMEM((D, D), F32)],
        compiler_params=_cparams(("arbitrary",)),
    )(dout, proj, ya, yb, yc, wa, wb, wc, wo, _row(gate))


def _loss_head(y, tgt, name):
    T = y.shape[0]
    tm = min(512, T)

    def body(y_ref, t_ref, dy_ref, l_ref):
        i = pl.program_id(0)
        diff = y_ref[...] - t_ref[...]
        dy_ref[...] = diff * (1.0 / D)
        part = jnp.sum(diff * diff, axis=0, keepdims=True)

        @pl.when(i == 0)
        def _():
            l_ref[...] = part

        @pl.when(i > 0)
        def _():
            l_ref[...] += part

    blk = pl.BlockSpec((tm, D), lambda i: (i, 0))
    return pl.pallas_call(
        body, name=name, grid=(T // tm,), in_specs=[blk, blk],
        out_specs=[blk, pl.BlockSpec((1, D), lambda i: (0, 0))],
        out_shape=[jax.ShapeDtypeStruct((T, D), F32), jax.ShapeDtypeStruct((1, D), F32)],
        compiler_params=_cparams(("arbitrary",)),
    )(y, tgt)


def _ada_fwd(c_all, w_ada, b_my, name):
    def body(c_ref, w_ref, b_ref, o_ref):
        sc = _silu(c_ref[...])
        for l in range(DEPTH):
            o_ref[l] = _bdot(sc, w_ref[l]) + b_ref[l:l + 1, :]

    return pl.pallas_call(body, name=name, out_shape=jax.ShapeDtypeStruct((DEPTH, N_DEV, w_ada.shape[2]), F32),
                          compiler_params=_cparams())(c_all, w_ada, b_my)


def _ada_bwd(c_all, dmod_my, name):
    def body(c_ref, d_ref, o_ref):
        sc = _silu(c_ref[...])
        for l in range(DEPTH):
            o_ref[l] = _bdot(sc, d_ref[l], TN)

    return pl.pallas_call(body, name=name, out_shape=jax.ShapeDtypeStruct((DEPTH, D, dmod_my.shape[2]), F32),
                          compiler_params=_cparams())(c_all, dmod_my)


def _adam_math(w, g, m, v):
    m = ADAM_B1 * m + (1.0 - ADAM_B1) * g
    v = ADAM_B2 * v + (1.0 - ADAM_B2) * (g * g)
    m_hat = m / (1.0 - ADAM_B1 ** ADAM_STEP)
    v_hat = v / (1.0 - ADAM_B2 ** ADAM_STEP)
    return -ADAM_LR * (m_hat / (jnp.sqrt(v_hat) + ADAM_EPS) + ADAM_WD * w), m, v


def _row_tile(rows, cap):
    best = rows
    for t in range(8, min(rows, cap) + 1, 8):
        if rows % t == 0:
            best = t
    return best if best <= cap else rows


def _adamw(w, g, m, v, name):
    R, C = w.shape
    tr = _row_tile(R, 256)

    def body(w_ref, g_ref, m_ref, v_ref, d_ref, mo_ref, vo_ref):
        d_ref[...], mo_ref[...], vo_ref[...] = _adam_math(w_ref[...], g_ref[...], m_ref[...], v_ref[...])

    blk = pl.BlockSpec((tr, C), lambda i: (i, 0))
    return pl.pallas_call(body, name=name, grid=(R // tr,), in_specs=[blk] * 4, out_specs=[blk] * 3,
                          out_shape=[jax.ShapeDtypeStruct((R, C), F32)] * 3,
                          compiler_params=_cparams(("parallel",)))(w, g, m, v)


def _sum_parts(parts, name):
    _, R, C = parts.shape
    tr = _row_tile(R, 256)

    def body(p_ref, o_ref):
        acc = p_ref[0]
        for j in range(1, N_DEV):
            acc = acc + p_ref[j]
        o_ref[...] = acc

    return pl.pallas_call(body, name=name, grid=(R // tr,),
                          in_specs=[pl.BlockSpec((N_DEV, tr, C), lambda i: (0, i, 0))],
                          out_specs=pl.BlockSpec((tr, C), lambda i: (i, 0)),
                          out_shape=jax.ShapeDtypeStruct((R, C), F32), compiler_params=_cparams(("parallel",)))(parts)


def _sum_adamw(parts, w, m, v, name):
    P, R, C = parts.shape
    tr = _row_tile(R, 128)

    def body(p_ref, w_ref, m_ref, v_ref, g_ref, d_ref, mo_ref, vo_ref):
        g = p_ref[0].astype(F32)
        for j in range(1, P):
            g = g + p_ref[j].astype(F32)
        g_ref[...] = g
        d_ref[...], mo_ref[...], vo_ref[...] = _adam_math(w_ref[...], g, m_ref[...], v_ref[...])

    blk = pl.BlockSpec((tr, C), lambda i: (i, 0))
    return pl.pallas_call(body, name=name, grid=(R // tr,),
                          in_specs=[pl.BlockSpec((P, tr, C), lambda i: (0, i, 0)), blk, blk, blk],
                          out_specs=[blk] * 4, out_shape=[jax.ShapeDtypeStruct((R, C), F32)] * 4,
                          compiler_params=_cparams(("parallel",)))(parts, w, m, v)


def _pair_sum(core, buf, recv, name):
    _, _, R, C = buf.shape
    tr = _row_tile(R, 128)

    def body(c_ref, a_ref, b_ref, o_ref):
        o_ref[...] = (a_ref[:, 0].astype(F32) + b_ref[...].astype(F32)).astype(BF16)

    return pl.pallas_call(
        body, name=name,
        grid_spec=pltpu.PrefetchScalarGridSpec(
            num_scalar_prefetch=1, grid=(R // tr,),
            in_specs=[pl.BlockSpec((4, 1, tr, C), lambda i, c: (0, c[0], i, 0)),
                      pl.BlockSpec((4, tr, C), lambda i, c: (0, i, 0))],
            out_specs=pl.BlockSpec((4, tr, C), lambda i, c: (0, i, 0))),
        out_shape=jax.ShapeDtypeStruct((4, R, C), BF16),
        compiler_params=_cparams(("parallel",)))(core, buf, recv)


SHARD_IN = D_IN // N_DEV


def _w_in_pieces():
    out, p = [], 0
    for a, b in _PAD_FROM:
        for j in range(N_DEV):
            lo, hi = max(a, SHARD_IN * j), min(b, SHARD_IN * (j + 1))
            if lo < hi:
                out.append((j, lo - SHARD_IN * j, hi - SHARD_IN * j, p + lo - a))
        p += b - a
    return out


def _assemble_w_in(gw, name):
    tr = 256
    nt = D // tr

    def body(x_ref, o_ref):
        for j, s0, s1, d0 in _w_in_pieces():
            o_ref[0, :, d0:d0 + s1 - s0] = x_ref[j, :, s0:s1]
        o_ref[0, :, D_IN:] = jnp.zeros((tr, NP - D_IN), gw.dtype)

    return pl.pallas_call(
        body, name=name, grid=(DEPTH, nt),
        in_specs=[pl.BlockSpec((N_DEV, tr, SHARD_IN), lambda l, i: (0, l * nt + i, 0))],
        out_specs=pl.BlockSpec((1, tr, NP), lambda l, i: (l, i, 0)),
        out_shape=jax.ShapeDtypeStruct((DEPTH, D, NP), gw.dtype),
        compiler_params=_cparams(("parallel", "parallel")))(gw)


def _split_w_in_grad(g0, g1, name):
    tr = 256
    nt = D // tr

    def body(g0_ref, g1_ref, o_ref):
        l = pl.program_id(0)

        def emit(g_ref):
            for j, s0, s1, d0 in _w_in_pieces():
                o_ref[j, :, s0:s1] = g_ref[:, d0:d0 + s1 - s0].astype(BF16)

        @pl.when(l == 0)
        def _():
            emit(g0_ref)

        @pl.when(l == 1)
        def _():
            emit(g1_ref)

    return pl.pallas_call(
        body, name=name, grid=(DEPTH, nt),
        in_specs=[pl.BlockSpec((tr, NP), lambda l, i: (i * (1 - l) + (nt - 1) * l, 0)),
                  pl.BlockSpec((tr, NP), lambda l, i: (i * l, 0))],
        out_specs=pl.BlockSpec((N_DEV, tr, SHARD_IN), lambda l, i: (0, l * nt + i, 0)),
        out_shape=jax.ShapeDtypeStruct((N_DEV, DEPTH * D, SHARD_IN), BF16),
        compiler_params=_cparams(("arbitrary", "arbitrary")))(g0, g1)


def _mesh_pos():
    return lax.axis_index("x"), lax.axis_index("y"), lax.axis_index("c")


def _all_gather(blocks, name):
    n = len(blocks)

    def body(*refs):
        ins, outs = refs[:n], refs[n:2 * n]
        send_sems, recv_sems, local_sems = refs[2 * n:]
        x, y, c = _mesh_pos()
        me, sibling = (x, y, c), (x, y, 1 - c)
        chips = [(1 - x, y), (x, 1 - y), (1 - x, 1 - y)]
        idx = lambda p: 4 * p[0] + 2 * p[1] + p[2]

        def copy(a, k, block, to, src=None):
            dst = outs[a].at[idx(block)]
            return pltpu.make_async_remote_copy(
                src_ref=dst if src is None else src, dst_ref=dst, send_sem=send_sems.at[a, k],
                recv_sem=recv_sems.at[a, k], device_id=to, device_id_type=pl.DeviceIdType.MESH)

        mine = [pltpu.make_async_copy(ins[a], outs[a].at[idx(me)], local_sems.at[a]) for a in range(n)]
        for cp in mine:
            cp.start()
        first = []
        for a in range(n):
            first.append(copy(a, 0, me, sibling, src=ins[a]))
            first += [copy(a, 1 + j, me, (*chip, c), src=ins[a]) for j, chip in enumerate(chips)]
        for cp in first:
            cp.start()
        passed = []
        for j, chip in enumerate(chips):
            for a in range(n):
                copy(a, 1 + j, (*chip, c), me).wait_recv()
                cp = copy(a, 4 + j, (*chip, c), sibling)
                cp.start()
                passed.append(cp)
        for a in range(n):
            copy(a, 0, sibling, me).wait_recv()
            for j, chip in enumerate(chips):
                copy(a, 4 + j, (*chip, 1 - c), me).wait_recv()
        for cp in first + passed:
            cp.wait_send()
        for cp in mine:
            cp.wait()

    anyspec = pl.BlockSpec(memory_space=pl.ANY)
    return pl.pallas_call(
        body, name=name, in_specs=[anyspec] * n, out_specs=[anyspec] * n,
        out_shape=[jax.ShapeDtypeStruct((N_DEV,) + b.shape, b.dtype) for b in blocks],
        scratch_shapes=[pltpu.SemaphoreType.DMA((n, 7)), pltpu.SemaphoreType.DMA((n, 7)),
                        pltpu.SemaphoreType.DMA((n,))],
    )(*blocks)


def _exchange_core(bufs, name):
    n = len(bufs)

    def body(*refs):
        ins, outs = refs[:n], refs[n:2 * n]
        send_sems, recv_sems = refs[2 * n:]
        x, y, c = _mesh_pos()
        copies = []
        for a in range(n):
            for q in range(4):
                cp = pltpu.make_async_remote_copy(
                    src_ref=ins[a].at[q, 1 - c], dst_ref=outs[a].at[q], send_sem=send_sems.at[a, q],
                    recv_sem=recv_sems.at[a, q], device_id=(x, y, 1 - c), device_id_type=pl.DeviceIdType.MESH)
                cp.start()
                copies.append(cp)
        for cp in copies:
            cp.wait()

    anyspec = pl.BlockSpec(memory_space=pl.ANY)
    return pl.pallas_call(
        body, name=name, in_specs=[anyspec] * n, out_specs=[anyspec] * n,
        out_shape=[jax.ShapeDtypeStruct((4,) + b.shape[2:], b.dtype) for b in bufs],
        scratch_shapes=[pltpu.SemaphoreType.DMA((n, 4)), pltpu.SemaphoreType.DMA((n, 4))],
    )(*bufs)


def _exchange_chips(bufs, name):
    n = len(bufs)

    def body(*refs):
        ins, outs = refs[:n], refs[n:2 * n]
        send_sems, recv_sems, local_sems = refs[2 * n:]
        x, y, c = _mesh_pos()
        chip = 2 * x + y
        local = [pltpu.make_async_copy(ins[a].at[chip], outs[a].at[chip], local_sems.at[a]) for a in range(n)]
        for cp in local:
            cp.start()
        copies = []
        for k in range(1, 4):
            px = 1 - x if k & 2 else x
            py = 1 - y if k & 1 else y
            for a in range(n):
                cp = pltpu.make_async_remote_copy(
                    src_ref=ins[a].at[2 * px + py], dst_ref=outs[a].at[chip], send_sem=send_sems.at[a, k - 1],
                    recv_sem=recv_sems.at[a, k - 1], device_id=(px, py, c), device_id_type=pl.DeviceIdType.MESH)
                cp.start()
                copies.append(cp)
        for cp in copies:
            cp.wait()
        for cp in local:
            cp.wait()

    anyspec = pl.BlockSpec(memory_space=pl.ANY)
    return pl.pallas_call(
        body, name=name, in_specs=[anyspec] * n, out_specs=[anyspec] * n,
        out_shape=[jax.ShapeDtypeStruct(b.shape, b.dtype) for b in bufs],
        scratch_shapes=[pltpu.SemaphoreType.DMA((n, 3)), pltpu.SemaphoreType.DMA((n, 3)),
                        pltpu.SemaphoreType.DMA((n,))],
    )(*bufs)


def _pack_flat(parts, rows, cols, lead=()):
    flat = jnp.concatenate([p.reshape(lead + (-1,)) for p in parts], axis=-1)
    pad = rows * cols - flat.shape[-1]
    flat = jnp.pad(flat, [(0, 0)] * len(lead) + [(0, pad)])
    return flat.reshape(lead + (rows, cols))


def _unpack_flat(buf, shapes, lead=()):
    flat = buf.reshape(lead + (-1,))
    out, o = [], 0
    for s in shapes:
        n = int(np.prod(s))
        out.append(flat[..., o:o + n].reshape(lead + tuple(s)))
        o += n
    return out


_SMALL = (("b_ada", (3 * D,)), ("norm_g", (D,)), ("q_norm_g", (64,)), ("k_norm_g", (64,)), ("sinks", (8,)),
          ("dw_b", (512,)), ("ln_g", (512,)), ("ln_b", (512,)), ("pw2_b", (512,)), ("a_log", (4,)),
          ("dt_bias", (4,)), ("dn_norm_g", (128,)), ("dw_w", (CONV_K, 512)), ("sconv_w", (DN_K, QKV_C)))
_N_REPL = 12
_SMALL_SHAPES = tuple((DEPTH,) + s for _, s in _SMALL)
_SMALL_ROWS = -(-sum(int(np.prod(s)) for s in _SMALL_SHAPES) // (128 * 8)) * 8
_UPD_SHAPES = _SMALL_SHAPES[:_N_REPL] + ((DEPTH, CONV_K, 512 // N_DEV), (DEPTH, DN_K, QKV_C // N_DEV))
_UPD_ROWS = -(-sum(int(np.prod(s)) for s in _UPD_SHAPES) // (128 * 8)) * 8


def _lane4(v):
    return jnp.pad(v, (0, 124)).reshape(1, 128)


def kernel(x, c, w_ada, b_ada, norm_g, w_in, q_norm_g, k_norm_g, sinks, dw_w, dw_b, ln_g, ln_b, pw2_w, pw2_b, sconv_w, a_log, dt_bias, dn_norm_g, w_proj_a, w_proj_b, w_proj_c, w_out, loss_target, m_w_ada, m_b_ada, m_norm_g, m_w_in, m_q_norm_g, m_k_norm_g, m_sinks, m_dw_w, m_dw_b, m_ln_g, m_ln_b, m_pw2_w, m_pw2_b, m_sconv_w, m_a_log, m_dt_bias, m_dn_norm_g, m_w_proj_a, m_w_proj_b, m_w_proj_c, m_w_out, v_w_ada, v_b_ada, v_norm_g, v_w_in, v_q_norm_g, v_k_norm_g, v_sinks, v_dw_w, v_dw_b, v_ln_g, v_ln_b, v_pw2_w, v_pw2_b, v_sconv_w, v_a_log, v_dt_bias, v_dn_norm_g, v_w_proj_a, v_w_proj_b, v_w_proj_c, v_w_out):
    T = x.shape[1]
    nc = T // CHUNK
    xi, yi, ci = _mesh_pos()
    me = 4 * xi + 2 * yi + ci
    big_w = (w_in, pw2_w, w_proj_a, w_proj_b, w_proj_c, w_out)
    big_m = (m_w_in, m_pw2_w, m_w_proj_a, m_w_proj_b, m_w_proj_c, m_w_out)
    big_v = (v_w_in, v_pw2_w, v_w_proj_a, v_w_proj_b, v_w_proj_c, v_w_out)

    ada_cols = w_ada.shape[2]
    dw_cols, sc_cols = dw_w.shape[2], sconv_w.shape[2]
    flat2 = lambda a: a.reshape(-1, a.shape[-1])
    small_shapes = ((D,), dw_w.shape, sconv_w.shape)
    small_rows = -(-sum(int(np.prod(s)) for s in small_shapes) // (128 * 8)) * 8
    small32 = _pack_flat([c, dw_w, sconv_w], small_rows, 128)
    gw_in, gpw2, gpa, gpb, gpc, gwo, g32 = _all_gather([flat2(a.astype(BF16)) for a in big_w] + [small32],
                                                        "gather_weights")
    wp = _assemble_w_in(gw_in, "assemble_w_in")
    pw2_f = gpw2.reshape(N_DEV, DEPTH, -1, 512).transpose(1, 0, 2, 3).reshape(DEPTH, 512, 512)
    wa_f, wb_f, wc_f = (g.reshape(N_DEV, DEPTH, 512, -1).transpose(1, 2, 0, 3).reshape(DEPTH, 512, D)
                        for g in (gpa, gpb, gpc))
    wo_f = gwo.reshape(N_DEV, DEPTH, -1, D).transpose(1, 0, 2, 3).reshape(DEPTH, D, D)
    c_all, gdw, gsc = _unpack_flat(g32, small_shapes, lead=(N_DEV,))
    dw_f = gdw.transpose(1, 2, 0, 3).reshape(DEPTH, CONV_K, 512)
    sc_f = gsc.transpose(1, 2, 0, 3).reshape(DEPTH, DN_K, QKV_C)

    b_my = lax.dynamic_slice(b_ada, (0, me * ada_cols), (DEPTH, ada_cols))
    mod_part = _ada_fwd(c_all, w_ada, b_my, "ada_fwd")
    (gmod,) = _all_gather([mod_part.reshape(-1, 128)], "gather_mod")
    mod_all = gmod.reshape(N_DEV, DEPTH, N_DEV, ada_cols).transpose(1, 2, 0, 3).reshape(DEPTH, N_DEV, 3 * D)
    mod = lax.dynamic_index_in_dim(mod_all, me, axis=1, keepdims=False)
    shift, scale, gate = mod[:, :D], mod[:, D:2 * D], mod[:, 2 * D:]

    xs, saved = [x[0]], []
    for l in range(DEPTH):
        xl = xs[-1]
        h = _norm_fwd(xl, norm_g[l], scale[l], shift[l], f"norm_fwd{l}")
        proj = _mm(h, wp[l], tm=min(512, T), tn=1152, tk=D, name=f"in_proj{l}")
        ya = _attn_fwd(proj, q_norm_g[l], k_norm_g[l], sinks[l], f"attn_fwd{l}")
        yb = _conf_fwd(proj, dw_f[l], dw_b[l], ln_g[l], ln_b[l], pw2_f[l], pw2_b[l], f"conf_fwd{l}")
        alv, dtv, dng = _lane4(a_log[l]), _lane4(dt_bias[l]), _row(dn_norm_g[l])
        qd, kd, vd, gb = _gdn_prep_fwd(proj, sc_f[l], alv, dtv, f"gdn_prep_fwd{l}")
        gbt = gb[:, :8].reshape(nc, CHUNK, 8).transpose(0, 2, 1)
        u, w, qe, ke, pm, tmi, eg = _gdn_chunk_fwd(qd, kd, vd, gb, gbt, f"gdn_chunk_fwd{l}")
        yc, o, vn, ss = _gdn_scan_fwd(u, w, qe, ke, pm, eg, proj, dng, f"gdn_scan_fwd{l}")
        xs.append(_merge_fwd(xl, proj, ya, yb, yc, wa_f[l], wb_f[l], wc_f[l], wo_f[l], gate[l], f"merge_fwd{l}"))
        saved.append((h, proj, ya, yb, yc, qd, kd, vd, gb, gbt, ss, alv, dtv, dng, w, qe, ke, pm, tmi, eg, o, vn))

    dout, lsum = _loss_head(xs[-1], loss_target[0], "loss_head")
    loss = lax.psum(0.5 * jnp.sum(lsum) / D, ("x", "y", "c"))

    small = {name: [None] * DEPTH for name, _ in _SMALL}
    big_g = [[None] * DEPTH for _ in big_w]
    for l in reversed(range(DEPTH)):
        h, proj, ya, yb, yc, qd, kd, vd, gb, gbt, ss, alv, dtv, dng, w, qe, ke, pm, tmi, eg, o, vn = saved[l]
        dproj, dya, dyb, dyc, g_wa, g_wb, g_wc, g_wo, g_gate = _merge_bwd(
            dout, proj, ya, yb, yc, wa_f[l], wb_f[l], wc_f[l], wo_f[l], gate[l], f"merge_bwd{l}")
        dproj, dkv_a, g_q, g_k, g_s = _attn_bwd(dproj, proj, dya, q_norm_g[l], k_norm_g[l], sinks[l], f"attn_bwd{l}")
        dproj, du1, g_pw2, st_b = _conf_bwd1(dproj, proj, dyb, dw_f[l], dw_b[l], ln_g[l], ln_b[l], pw2_f[l], pw2_b[l],
                                             f"conf_bwd_a{l}")
        dproj, g_dw = _conf_bwd2(dproj, proj, du1, dw_f[l], f"conf_bwd_b{l}")
        dproj, do, dvn, dsn, g_dn = _gdn_scan_bwd(dproj, w, qe, ke, pm, eg, o, proj, dyc, dng, f"gdn_scan_bwd{l}")
        dqd, dkd, dvd, dgb = _gdn_chunk_grad(qd, kd, vd, gb, gbt, tmi, ss, dsn, do, dvn, vn, f"gdn_chunk_bwd{l}")
        dproj, dpre, st_c = _gdn_prep_bwd1(dproj, proj, dqd, dkd, dvd, dgb, dkv_a, sc_f[l], alv, dtv,
                                           f"gdn_prep_bwd_a{l}")
        dproj, g_sc = _gdn_prep_bwd2(dproj, proj, dpre, sc_f[l], f"gdn_prep_bwd_b{l}")
        dh = _mm(dproj, wp[l], tb=True, tm=min(512, T), tn=D, tk=1152, name=f"d_h{l}")
        g_wp = _mm(h, dproj, ta=True, tm=D, tn=1152, tk=min(512, T), name=f"d_w_in{l}")
        dout, st_n = _norm_bwd(dh, xs[l], dout, norm_g[l], scale[l], f"norm_bwd{l}")
        for i, g in enumerate((g_wp, g_pw2, g_wa, g_wb, g_wc, g_wo)):
            big_g[i][l] = g
        for name, g in (("b_ada", jnp.concatenate([st_n[0], st_n[1], g_gate[0]])), ("norm_g", st_n[2]),
                        ("q_norm_g", g_q[0]), ("k_norm_g", g_k[0]), ("sinks", g_s[0]), ("dw_b", st_b[3]),
                        ("ln_g", st_b[1]), ("ln_b", st_b[2]), ("pw2_b", st_b[0]), ("a_log", st_c[0, :4]),
                        ("dt_bias", st_c[1, :4]), ("dn_norm_g", g_dn[0]), ("dw_w", g_dw[:CONV_K]),
                        ("sconv_w", g_sc[:DN_K])):
            small[name][l] = g
    grad_x = dout[None]

    part = _pack_flat([jnp.stack(small[name]) for name, _ in _SMALL], _SMALL_ROWS, 128)
    (gpart,) = _all_gather([part], "gather_small_grads")
    dmod_all = gpart.reshape(N_DEV, -1)[:, :DEPTH * 3 * D].reshape(N_DEV, DEPTH, 3 * D)
    dmod_my = lax.dynamic_slice(dmod_all, (0, 0, me * ada_cols), (N_DEV, DEPTH, ada_cols)).transpose(1, 0, 2)
    g_w_ada = _ada_bwd(c_all, dmod_my, "ada_bwd")
    tot = _unpack_flat(_sum_parts(gpart, "sum_small_grads"), _SMALL_SHAPES)
    g_small = dict(zip([n for n, _ in _SMALL], tot))
    g_small["dw_w"] = lax.dynamic_slice(g_small["dw_w"], (0, 0, me * dw_cols), (DEPTH, CONV_K, dw_cols))
    g_small["sconv_w"] = lax.dynamic_slice(g_small["sconv_w"], (0, 0, me * sc_cols), (DEPTH, DN_K, sc_cols))
    env = dict(b_ada=(b_ada, m_b_ada, v_b_ada), norm_g=(norm_g, m_norm_g, v_norm_g),
               q_norm_g=(q_norm_g, m_q_norm_g, v_q_norm_g), k_norm_g=(k_norm_g, m_k_norm_g, v_k_norm_g),
               sinks=(sinks, m_sinks, v_sinks), dw_b=(dw_b, m_dw_b, v_dw_b), ln_g=(ln_g, m_ln_g, v_ln_g),
               ln_b=(ln_b, m_ln_b, v_ln_b), pw2_b=(pw2_b, m_pw2_b, v_pw2_b), a_log=(a_log, m_a_log, v_a_log),
               dt_bias=(dt_bias, m_dt_bias, v_dt_bias), dn_norm_g=(dn_norm_g, m_dn_norm_g, v_dn_norm_g),
               dw_w=(dw_w, m_dw_w, v_dw_w), sconv_w=(sconv_w, m_sconv_w, v_sconv_w))
    names = [n for n, _ in _SMALL]
    pk = lambda k: _pack_flat([env[n][k] for n in names], _UPD_ROWS, 128)
    upd = _adamw(pk(0), _pack_flat([g_small[n] for n in names], _UPD_ROWS, 128), pk(1), pk(2), "adamw_small")
    d_small, m_small, v_small = (dict(zip(names, _unpack_flat(u, _UPD_SHAPES))) for u in upd)

    d_ada, nm_ada, nv_ada = (u.reshape(w_ada.shape) for u in
                             _adamw(flat2(w_ada), flat2(g_w_ada), flat2(m_w_ada), flat2(v_w_ada), "adamw_w_ada"))

    g_pw, g_a, g_b, g_c, g_o = (jnp.stack(g) for g in big_g[1:])
    by_dest = [_split_w_in_grad(big_g[0][0], big_g[0][1], "split_w_in_grad"),
               g_pw.reshape(DEPTH, N_DEV, -1, 512).transpose(1, 0, 2, 3).astype(BF16)]
    by_dest += [g.reshape(DEPTH, 512, N_DEV, -1).transpose(2, 0, 1, 3).astype(BF16) for g in (g_a, g_b, g_c)]
    by_dest.append(g_o.reshape(DEPTH, N_DEV, -1, D).transpose(1, 0, 2, 3).astype(BF16))
    by_dest = [b.reshape(4, 2, -1, b.shape[-1]) for b in by_dest]
    from_sibling = _exchange_core(by_dest, "exchange_grads_core")
    core = jnp.reshape(ci, (1,)).astype(jnp.int32)
    chip_sums = [_pair_sum(core, b, r, f"pair_sum{i}") for i, (b, r) in enumerate(zip(by_dest, from_sibling))]
    parts = _exchange_chips(chip_sums, "exchange_grads_chips")
    res = [_sum_adamw(p, flat2(w), flat2(m), flat2(v), f"sum_adamw{i}")
           for i, (p, w, m, v) in enumerate(zip(parts, big_w, big_m, big_v))]
    g_big, d_big, m_big, v_big = ([r[k].reshape(w.shape) for r, w in zip(res, big_w)] for k in range(4))

    order = ("w_ada", "b_ada", "norm_g", "w_in", "q_norm_g", "k_norm_g", "sinks", "dw_w", "dw_b", "ln_g", "ln_b",
             "pw2_w", "pw2_b", "sconv_w", "a_log", "dt_bias", "dn_norm_g", "w_proj_a", "w_proj_b", "w_proj_c", "w_out")
    big_names = ("w_in", "pw2_w", "w_proj_a", "w_proj_b", "w_proj_c", "w_out")

    def pick(kind):
        src_small = (g_small, d_small, m_small, v_small)[kind]
        src_big = (g_big, d_big, m_big, v_big)[kind]
        src_ada = (g_w_ada, d_ada, nm_ada, nv_ada)[kind]
        return [src_ada if n == "w_ada" else src_big[big_names.index(n)] if n in big_names else src_small[n]
                for n in order]

    return (loss, grad_x, *pick(0), *pick(1), *pick(2), *pick(3))
```

```python
import functools
import math

import jax
import jax.numpy as jnp
import numpy as np
from jax import lax
from jax.experimental import pallas as pl
from jax.experimental.pallas import tpu as pltpu

F32 = jnp.float32
BF16 = jnp.bfloat16
HI = lax.Precision.HIGHEST

N_DEV = 8
D = 1024
DEPTH = 2
EPS = 1e-6
NEG_INF = -1e30
WINDOW = 128
ATT_HEADS = 8
ATT_HD = 64
CONV_K = 31
DN_HEADS = 4
DN_K = 4
CHUNK = 64
D_IN = 7944
VMEM_LIMIT = 56 * 1024 * 1024

C_MG, C_QA, C_ZA, C_ZB, C_QC, C_KC, C_VC, C_GV, C_GG, C_ZC, C_KA, C_VA, C_AB, NP = (
    0, 3072, 3584, 4096, 4608, 5120, 5632, 6144, 6656, 7168, 7680, 7808, 7936, 8064)
_PAD_FROM = ((4872, 7944), (0, 512), (768, 1280), (2304, 2816), (2816, 4352), (1280, 2304), (4360, 4872),
             (512, 768), (4352, 4360))

ALIBI = tuple(float(2.0 ** (-8.0 * (h + 1) / ATT_HEADS)) for h in range(ATT_HEADS))

ADAM_LR, ADAM_B1, ADAM_B2, ADAM_EPS, ADAM_WD, ADAM_STEP = 0.001, 0.9, 0.999, 1e-08, 0.01, 10


def _cparams(sem=None):
    return pltpu.CompilerParams(dimension_semantics=sem, vmem_limit_bytes=VMEM_LIMIT)


def _sig(x):
    return jax.nn.sigmoid(x)


def _silu(x):
    return x * _sig(x)


def _dsilu(x):
    s = _sig(x)
    return s * (1.0 + x * (1.0 - s))


def _dot(a, b, dims=((1,), (0,)), precision=None):
    return lax.dot_general(a, b, (dims, ((), ())), preferred_element_type=F32, precision=precision)


def _bdot(a, b, dims=((1,), (0,))):
    return _dot(a.astype(BF16), b.astype(BF16), dims)


NN, NT, TN = ((1,), (0,)), ((1,), (1,)), ((0,), (0,))


def _row(v):
    return v.reshape(1, -1)


def _mm(a, b, *, ta=False, tb=False, tm, tn, tk, name):
    M, K = (a.shape[1], a.shape[0]) if ta else a.shape
    N = b.shape[0] if tb else b.shape[1]
    assert M % tm == 0 and N % tn == 0 and K % tk == 0, (M, N, K, tm, tn, tk)
    nk = K // tk
    dims = ((0 if ta else 1,), (1 if tb else 0,))

    def body(a_ref, b_ref, o_ref):
        k = pl.program_id(2)
        part = _bdot(a_ref[...], b_ref[...], dims)

        @pl.when(k == 0)
        def _():
            o_ref[...] = part

        @pl.when(k > 0)
        def _():
            o_ref[...] += part

    a_spec = pl.BlockSpec((tk, tm), lambda i, j, k: (k, i)) if ta else pl.BlockSpec((tm, tk), lambda i, j, k: (i, k))
    b_spec = pl.BlockSpec((tn, tk), lambda i, j, k: (j, k)) if tb else pl.BlockSpec((tk, tn), lambda i, j, k: (k, j))
    return pl.pallas_call(
        body, name=name, grid=(M // tm, N // tn, nk),
        in_specs=[a_spec, b_spec], out_specs=pl.BlockSpec((tm, tn), lambda i, j, k: (i, j)),
        out_shape=jax.ShapeDtypeStruct((M, N), F32),
        compiler_params=_cparams(("parallel", "parallel", "arbitrary")),
    )(a, b)


def _norm_fwd(x, norm_g, scale, shift, name):
    T = x.shape[0]
    tm = min(512, T)

    def body(x_ref, g_ref, sc_ref, sh_ref, h_ref):
        xv = x_ref[...]
        r = lax.rsqrt(jnp.mean(xv * xv, axis=-1, keepdims=True) + EPS)
        h_ref[...] = ((xv * r) * g_ref[...] * (1.0 + sc_ref[...]) + sh_ref[...]).astype(BF16)

    vec = pl.BlockSpec((1, D), lambda i: (0, 0))
    return pl.pallas_call(
        body, name=name, grid=(T // tm,),
        in_specs=[pl.BlockSpec((tm, D), lambda i: (i, 0)), vec, vec, vec],
        out_specs=pl.BlockSpec((tm, D), lambda i: (i, 0)),
        out_shape=jax.ShapeDtypeStruct((T, D), BF16),
        compiler_params=_cparams(("parallel",)),
    )(x, _row(norm_g), _row(scale), _row(shift))


def _norm_bwd(dh, x, dres, norm_g, scale, name):
    T = x.shape[0]
    tm = min(512, T)

    def body(dh_ref, x_ref, dr_ref, g_ref, sc_ref, dx_ref, st_ref):
        i = pl.program_id(0)
        xv, dhv = x_ref[...], dh_ref[...]
        r = lax.rsqrt(jnp.mean(xv * xv, axis=-1, keepdims=True) + EPS)
        xh = xv * r
        g, s1 = g_ref[...], 1.0 + sc_ref[...]
        dxh = dhv * (g * s1)
        dx_ref[...] = dr_ref[...] + r * (dxh - xh * jnp.mean(dxh * xh, axis=-1, keepdims=True))
        dhx = dhv * xh
        upd = jnp.concatenate([jnp.sum(dhv, axis=0, keepdims=True), jnp.sum(dhx * g, axis=0, keepdims=True),
                               jnp.sum(dhx * s1, axis=0, keepdims=True), jnp.zeros((5, D), F32)], axis=0)

        @pl.when(i == 0)
        def _():
            st_ref[...] = upd

        @pl.when(i > 0)
        def _():
            st_ref[...] += upd

    vec = pl.BlockSpec((1, D), lambda i: (0, 0))
    blk = pl.BlockSpec((tm, D), lambda i: (i, 0))
    return pl.pallas_call(
        body, name=name, grid=(T // tm,),
        in_specs=[blk, blk, blk, vec, vec],
        out_specs=[blk, pl.BlockSpec((8, D), lambda i: (0, 0))],
        out_shape=[jax.ShapeDtypeStruct((T, D), F32), jax.ShapeDtypeStruct((8, D), F32)],
        compiler_params=_cparams(("arbitrary",)),
    )(dh, x, dres, _row(norm_g), _row(scale))


def _rms(x, g):
    r = lax.rsqrt(jnp.mean(x * x, axis=-1, keepdims=True) + EPS)
    return x * r, r


def _head_mean_matrix():
    head = np.arange(ATT_HEADS * ATT_HD) // ATT_HD
    return jnp.asarray((head[:, None] == head[None, :]) * (1.0 / ATT_HD), BF16)


def _head_rms(x, hm):
    r = lax.rsqrt(_dot_exact(x * x, hm) + EPS)
    return x * r, r


def _head_rms_bwd(dy, xh, r, g, hm):
    dxh = dy * g
    return r * (dxh - xh * _dot_exact(dxh * xh, hm)), dy * xh


def _attn_mask(n):
    qi = lax.broadcasted_iota(jnp.int32, (WINDOW, 2 * WINDOW), 0)
    kj = lax.broadcasted_iota(jnp.int32, (WINDOW, 2 * WINDOW), 1)
    dist = qi + WINDOW - kj
    valid = (dist >= 0) & (dist < WINDOW) & ((n > 0) | (kj >= WINDOW))
    return valid, dist.astype(F32)


def _attn_probs(s, h, sink, valid, distf):
    s = s - ALIBI[h] * distf
    s = jnp.where(valid, s, NEG_INF)
    m = jnp.maximum(jnp.max(s, axis=-1, keepdims=True), sink)
    p = jnp.exp(s - m)
    es = jnp.exp(sink - m)
    den = jnp.sum(p, axis=-1, keepdims=True) + es
    return p / den, es / den


def _attn_fwd(proj, q_norm_g, k_norm_g, sinks, name):
    T = proj.shape[0]
    nb = T // WINDOW

    def body(sink_ref, q_ref, z_ref, kc_ref, kp_ref, vc_ref, vp_ref, qg_ref, kg_ref, hm_ref, o_ref):
        n = pl.program_id(0)
        valid, distf = _attn_mask(n)
        k2 = jnp.concatenate([kp_ref[...], kc_ref[...]], axis=0)
        v2 = jnp.concatenate([vp_ref[...], vc_ref[...]], axis=0).astype(BF16)
        kn = (_head_rms(k2, hm_ref[:128, :128])[0] * kg_ref[...]).astype(BF16)
        qn = ((_head_rms(q_ref[...], hm_ref[...])[0] * qg_ref[...]) * (ATT_HD ** -0.5)).astype(BF16)

        def head(h):
            sl, gsl = slice(64 * h, 64 * h + 64), slice(64 * (h // 4), 64 * (h // 4) + 64)
            s = _dot(qn[:, sl], kn[:, gsl], NT)
            yield
            p, _ = _attn_probs(s, h, sink_ref[h], valid, distf)
            o_ref[:, sl] = _dot(p.astype(BF16), v2[:, gsl])
            yield

        _lockstep([head(h) for h in range(ATT_HEADS)])
        o_ref[...] = o_ref[...] * _silu(z_ref[...])

    prev = lambda n: jnp.maximum(n - 1, 0)
    return pl.pallas_call(
        body, name=name, grid=(nb,),
        in_specs=[pl.BlockSpec(memory_space=pltpu.SMEM),
                  pl.BlockSpec((WINDOW, 512), lambda n: (n, C_QA // 512)),
                  pl.BlockSpec((WINDOW, 512), lambda n: (n, C_ZA // 512)),
                  pl.BlockSpec((WINDOW, 128), lambda n: (n, C_KA // 128)),
                  pl.BlockSpec((WINDOW, 128), lambda n: (prev(n), C_KA // 128)),
                  pl.BlockSpec((WINDOW, 128), lambda n: (n, C_VA // 128)),
                  pl.BlockSpec((WINDOW, 128), lambda n: (prev(n), C_VA // 128)),
                  pl.BlockSpec((1, 512), lambda n: (0, 0)), pl.BlockSpec((1, 128), lambda n: (0, 0)),
                  pl.BlockSpec((512, 512), lambda n: (0, 0))],
        out_specs=pl.BlockSpec((WINDOW, 512), lambda n: (n, 0)),
        out_shape=jax.ShapeDtypeStruct((T, 512), F32),
        compiler_params=_cparams(("parallel",)),
    )(sinks, proj, proj, proj, proj, proj, proj, _row(jnp.tile(q_norm_g, ATT_HEADS)), _row(jnp.tile(k_norm_g, 2)),
      _head_mean_matrix())


def _rms_bwd(dy, xh, r, g):
    dxh = dy * g
    return r * (dxh - xh * jnp.mean(dxh * xh, axis=-1, keepdims=True)), dy * xh


def _attn_bwd(dproj, proj, dya, q_norm_g, k_norm_g, sinks, name):
    T = proj.shape[0]
    nb = T // WINDOW

    def body(sink_ref, dp_any, q_ref, z_ref, kc_ref, kp_ref, vc_ref, vp_ref, dy_ref, qg_ref, kg_ref, hm_ref,
             dqz_ref, dkv_ref, gq_ref, gk_ref, gs_ref, ck_ref, cv_ref, o_sc, dq_sc):
        n = pl.program_id(0)

        @pl.when(n == 0)
        def _():
            gq_ref[...] = jnp.zeros_like(gq_ref)
            gk_ref[...] = jnp.zeros_like(gk_ref)
            gs_ref[...] = jnp.zeros_like(gs_ref)
            ck_ref[...] = jnp.zeros_like(ck_ref)
            cv_ref[...] = jnp.zeros_like(cv_ref)

        lane8 = lax.broadcasted_iota(jnp.int32, (1, 8), 1)

        @pl.when(n < nb)
        def _():
            valid, distf = _attn_mask(n)
            k2 = jnp.concatenate([kp_ref[...], kc_ref[...]], axis=0)
            v2 = jnp.concatenate([vp_ref[...], vc_ref[...]], axis=0).astype(BF16)
            kn = (_head_rms(k2, hm_ref[:128, :128])[0] * kg_ref[...]).astype(BF16)
            qh, qr = _head_rms(q_ref[...], hm_ref[...])
            qn = ((qh * qg_ref[...]) * (ATT_HD ** -0.5)).astype(BF16)
            zs = z_ref[...]
            do_all = dy_ref[...] * _silu(zs)
            dob_all = do_all.astype(BF16)

            def head(h):
                sl, gsl = slice(64 * h, 64 * h + 64), slice(64 * (h // 4), 64 * (h // 4) + 64)
                s = _dot(qn[:, sl], kn[:, gsl], NT)
                dpm = _dot(dob_all[:, sl], v2[:, gsl], NT)
                yield
                p, ps = _attn_probs(s, h, sink_ref[h], valid, distf)
                pb = p.astype(BF16)
                o_sc[:, sl] = _dot(pb, v2[:, gsl])
                dvg = _dot(pb, dob_all[:, sl], TN)
                delta = jnp.sum(p * dpm, axis=-1, keepdims=True)
                ds = (p * (dpm - delta)).astype(BF16)
                gs = jnp.where(lane8 == h, -jnp.sum(ps * delta, axis=0, keepdims=True), 0.0)
                yield
                dkn = _dot(ds, qn[:, sl], TN)
                dq_sc[:, sl] = _dot(ds, kn[:, gsl])
                yield
                return dkn, dvg, gs

            res = _lockstep([head(h) for h in range(ATT_HEADS)])
            dqz_ref[:, 512:] = (dy_ref[...] * o_sc[...] * _dsilu(zs)).astype(BF16)
            dq, gq = _head_rms_bwd(dq_sc[...] * (ATT_HD ** -0.5), qh, qr, qg_ref[...], hm_ref[...])
            dqz_ref[:, :512] = dq.astype(BF16)
            gq_acc = jnp.sum(gq, axis=0, keepdims=True)
            gs_acc = sum(r[2] for r in res[1:]) + res[0][2]
            for g in range(2):
                dkn = (res[4 * g][0] + res[4 * g + 1][0]) + (res[4 * g + 2][0] + res[4 * g + 3][0])
                dvg = (res[4 * g][1] + res[4 * g + 1][1]) + (res[4 * g + 2][1] + res[4 * g + 3][1])
                ksl = slice(64 * g, 64 * g + 64)
                vsl = slice(128 + 64 * g, 128 + 64 * g + 64)
                dkv_ref[:, ksl] = ck_ref[:, ksl] + dkn[:WINDOW]
                dkv_ref[:, vsl] = cv_ref[:, ksl] + dvg[:WINDOW]
                ck_ref[:, ksl] = dkn[WINDOW:]
                cv_ref[:, ksl] = dvg[WINDOW:]
            gq_ref[...] += gq_acc
            gs_ref[...] += gs_acc

        @pl.when(n == nb)
        def _():
            dkv_ref[:, :128] = ck_ref[...]
            dkv_ref[:, 128:] = cv_ref[...]

        @pl.when(n > 0)
        def _():
            hm = hm_ref[:128, :128]
            kh, kr = _head_rms(kp_ref[...], hm)
            dk, gk = _head_rms_bwd(dkv_ref[:, :128], kh, kr, kg_ref[...], hm)
            dkv_ref[:, :128] = dk
            gk_ref[...] += jnp.sum(gk, axis=0, keepdims=True)

    cur = lambda n: jnp.minimum(n, nb - 1)
    prev = lambda n: jnp.maximum(n - 1, 0)
    small = lambda w: pl.BlockSpec((1, w), lambda n: (0, 0))
    return pl.pallas_call(
        body, name=name, grid=(nb + 1,),
        in_specs=[pl.BlockSpec(memory_space=pltpu.SMEM), pl.BlockSpec(memory_space=pl.ANY),
                  pl.BlockSpec((WINDOW, 512), lambda n: (cur(n), C_QA // 512)),
                  pl.BlockSpec((WINDOW, 512), lambda n: (cur(n), C_ZA // 512)),
                  pl.BlockSpec((WINDOW, 128), lambda n: (cur(n), C_KA // 128)),
                  pl.BlockSpec((WINDOW, 128), lambda n: (prev(n), C_KA // 128)),
                  pl.BlockSpec((WINDOW, 128), lambda n: (cur(n), C_VA // 128)),
                  pl.BlockSpec((WINDOW, 128), lambda n: (prev(n), C_VA // 128)),
                  pl.BlockSpec((WINDOW, 512), lambda n: (cur(n), 0)),
                  small(512), small(128), pl.BlockSpec((512, 512), lambda n: (0, 0))],
        out_specs=[pl.BlockSpec((WINDOW, 1024), lambda n: (cur(n), C_QA // 1024)),
                   pl.BlockSpec((WINDOW, 256), lambda n: (prev(n), 0)),
                   small(512), small(128), small(8)],
        out_shape=[jax.ShapeDtypeStruct(dproj.shape, BF16), jax.ShapeDtypeStruct((T, 256), F32),
                   jax.ShapeDtypeStruct((1, 512), F32), jax.ShapeDtypeStruct((1, 128), F32),
                   jax.ShapeDtypeStruct((1, 8), F32)],
        scratch_shapes=[pltpu.VMEM((WINDOW, 128), F32), pltpu.VMEM((WINDOW, 128), F32),
                        pltpu.VMEM((WINDOW, 512), F32), pltpu.VMEM((WINDOW, 512), F32)],
        input_output_aliases={1: 0},
        compiler_params=_cparams(("arbitrary",)),
    )(sinks, dproj, proj, proj, proj, proj, proj, proj, dya, _row(jnp.tile(q_norm_g, ATT_HEADS)),
      _row(jnp.tile(k_norm_g, 2)), _head_mean_matrix())


HALO_B = 32


def _conf_specs(T, tm):
    r = tm // HALO_B
    cur = lambda c: pl.BlockSpec((tm, 512), lambda i: (i, c // 512))
    prev = lambda c: pl.BlockSpec((HALO_B, 512), lambda i: (jnp.maximum(i * r - 1, 0), c // 512))
    return cur, prev


def _conf_core(i, tm, gv_ref, gg_ref, gvp_ref, ggp_ref, w_ref, b_ref, lg_ref, lb_ref, pw_ref, pb_ref, ext_ref):
    up = gvp_ref[...] * _sig(ggp_ref[...])
    ext_ref[:HALO_B] = jnp.where(i > 0, up, 0.0)
    ext_ref[HALO_B:] = gv_ref[...] * _sig(gg_ref[...])
    acc = jnp.zeros((tm, 512), F32) + b_ref[...]
    for k in range(CONV_K):
        acc = acc + w_ref[k:k + 1, :] * ext_ref[pl.ds(HALO_B - CONV_K + 1 + k, tm), :]
    mu = jnp.mean(acc, axis=-1, keepdims=True)
    xc = acc - mu
    rstd = lax.rsqrt(jnp.mean(xc * xc, axis=-1, keepdims=True) + EPS)
    xh = xc * rstd
    u2 = xh * lg_ref[...] + lb_ref[...]
    u3 = _silu(u2)
    ypre = _bdot(u3, pw_ref[...]) + pb_ref[...]
    return xh, rstd, u2, u3, ypre


def _conf_fwd(proj, dw_w, dw_b, ln_g, ln_b, pw2, pw2_b, name):
    T = proj.shape[0]
    tm = min(512, T)
    cur, prev = _conf_specs(T, tm)

    def body(gv_ref, gg_ref, gvp_ref, ggp_ref, zb_ref, w_ref, b_ref, lg_ref, lb_ref, pw_ref, pb_ref, o_ref, ext_ref):
        i = pl.program_id(0)
        ypre = _conf_core(i, tm, gv_ref, gg_ref, gvp_ref, ggp_ref, w_ref, b_ref, lg_ref, lb_ref, pw_ref, pb_ref,
                          ext_ref)[4]
        o_ref[...] = ypre * _silu(zb_ref[...])

    full = lambda s: pl.BlockSpec(s, lambda i: (0, 0))
    return pl.pallas_call(
        body, name=name, grid=(T // tm,),
        in_specs=[cur(C_GV), cur(C_GG), prev(C_GV), prev(C_GG), cur(C_ZB), full((CONV_K, 512)), full((1, 512)),
                  full((1, 512)), full((1, 512)), full((512, 512)), full((1, 512))],
        out_specs=pl.BlockSpec((tm, 512), lambda i: (i, 0)),
        out_shape=jax.ShapeDtypeStruct((T, 512), F32),
        scratch_shapes=[pltpu.VMEM((tm + HALO_B, 512), F32)],
        compiler_params=_cparams(("parallel",)),
    )(proj, proj, proj, proj, proj, dw_w, _row(dw_b), _row(ln_g), _row(ln_b), pw2, _row(pw2_b))


def _conf_bwd1(dproj, proj, dyb, dw_w, dw_b, ln_g, ln_b, pw2, pw2_b, name):
    T = proj.shape[0]
    tm = min(512, T)
    cur, prev = _conf_specs(T, tm)

    def body(dp_any, gv_ref, gg_ref, gvp_ref, ggp_ref, zb_ref, dy_ref, w_ref, b_ref, lg_ref, lb_ref, pw_ref, pb_ref,
             dzb_ref, du1_ref, gpw_ref, st_ref, ext_ref):
        i = pl.program_id(0)
        xh, rstd, u2, u3, ypre = _conf_core(i, tm, gv_ref, gg_ref, gvp_ref, ggp_ref, w_ref, b_ref, lg_ref, lb_ref,
                                            pw_ref, pb_ref, ext_ref)
        zb, dy = zb_ref[...], dy_ref[...]
        dzb_ref[...] = (dy * ypre * _dsilu(zb)).astype(BF16)
        dyp = dy * _silu(zb)
        du2 = _bdot(dyp, pw_ref[...], NT) * _dsilu(u2)
        dxh = du2 * lg_ref[...]
        du1 = rstd * (dxh - jnp.mean(dxh, axis=-1, keepdims=True) - xh * jnp.mean(dxh * xh, axis=-1, keepdims=True))
        du1_ref[...] = du1
        gpw = _bdot(u3, dyp, TN)
        rs = lambda a: jnp.sum(a, axis=0, keepdims=True)
        upd = jnp.concatenate([rs(dyp), rs(du2 * xh), rs(du2), rs(du1), jnp.zeros((4, 512), F32)], axis=0)

        @pl.when(i == 0)
        def _():
            gpw_ref[...] = gpw
            st_ref[...] = upd

        @pl.when(i > 0)
        def _():
            gpw_ref[...] += gpw
            st_ref[...] += upd

    full = lambda s: pl.BlockSpec(s, lambda i: (0, 0))
    blk = pl.BlockSpec((tm, 512), lambda i: (i, 0))
    return pl.pallas_call(
        body, name=name, grid=(T // tm,),
        in_specs=[pl.BlockSpec(memory_space=pl.ANY), cur(C_GV), cur(C_GG), prev(C_GV), prev(C_GG), cur(C_ZB), blk,
                  full((CONV_K, 512)), full((1, 512)), full((1, 512)), full((1, 512)), full((512, 512)), full((1, 512))],
        out_specs=[cur(C_ZB), blk, full((512, 512)), full((8, 512))],
        out_shape=[jax.ShapeDtypeStruct(dproj.shape, BF16), jax.ShapeDtypeStruct((T, 512), F32),
                   jax.ShapeDtypeStruct((512, 512), F32), jax.ShapeDtypeStruct((8, 512), F32)],
        scratch_shapes=[pltpu.VMEM((tm + HALO_B, 512), F32)],
        input_output_aliases={0: 0},
        compiler_params=_cparams(("arbitrary",)),
    )(dproj, proj, proj, proj, proj, proj, dyb, dw_w, _row(dw_b), _row(ln_g), _row(ln_b), pw2, _row(pw2_b))


def _conf_bwd2(dproj, proj, du1, dw_w, name):
    T = proj.shape[0]
    tm = min(512, T)
    nt = T // tm
    r = tm // HALO_B
    cur, prev = _conf_specs(T, tm)

    def body(dp_any, gv_ref, gg_ref, gvp_ref, ggp_ref, du_ref, dun_ref, w_ref, dglu_ref, gw_ref, extu_ref, extd_ref):
        i = pl.program_id(0)
        gv, sg = gv_ref[...], _sig(gg_ref[...])
        extu_ref[:HALO_B] = jnp.where(i > 0, gvp_ref[...] * _sig(ggp_ref[...]), 0.0)
        extu_ref[HALO_B:] = gv * sg
        du1 = du_ref[...]
        extd_ref[:tm] = du1
        extd_ref[tm:] = jnp.where(i < nt - 1, dun_ref[...], 0.0)
        du0 = jnp.zeros((tm, 512), F32)
        rows = []
        for k in range(CONV_K):
            du0 = du0 + w_ref[k:k + 1, :] * extd_ref[pl.ds(CONV_K - 1 - k, tm), :]
            rows.append(jnp.sum(du1 * extu_ref[pl.ds(HALO_B - CONV_K + 1 + k, tm), :], axis=0, keepdims=True))
        rows.append(jnp.zeros((1, 512), F32))
        gw = jnp.concatenate(rows, axis=0)
        dglu_ref[:, :512] = (du0 * sg).astype(BF16)
        dglu_ref[:, 512:] = (du0 * gv * sg * (1.0 - sg)).astype(BF16)

        @pl.when(i == 0)
        def _():
            gw_ref[...] = gw

        @pl.when(i > 0)
        def _():
            gw_ref[...] += gw

    full = lambda s: pl.BlockSpec(s, lambda i: (0, 0))
    return pl.pallas_call(
        body, name=name, grid=(nt,),
        in_specs=[pl.BlockSpec(memory_space=pl.ANY), cur(C_GV), cur(C_GG), prev(C_GV), prev(C_GG),
                  pl.BlockSpec((tm, 512), lambda i: (i, 0)),
                  pl.BlockSpec((HALO_B, 512), lambda i: (jnp.minimum((i + 1) * r, T // HALO_B - 1), 0)),
                  full((CONV_K, 512))],
        out_specs=[pl.BlockSpec((tm, 1024), lambda i: (i, C_GV // 1024)), full((32, 512))],
        out_shape=[jax.ShapeDtypeStruct(dproj.shape, BF16), jax.ShapeDtypeStruct((32, 512), F32)],
        scratch_shapes=[pltpu.VMEM((tm + HALO_B, 512), F32), pltpu.VMEM((tm + HALO_B, 512), F32)],
        input_output_aliases={0: 0},
        compiler_params=_cparams(("arbitrary",)),
    )(dproj, proj, proj, proj, proj, du1, du1, dw_w)


HALO_C = 8
QKV_C = 1536


def _softplus(x):
    return jnp.maximum(x, 0.0) + jnp.log1p(jnp.exp(-jnp.abs(x)))


def _gdn_conv(i, tm, x_ref, xp_ref, w_ref, ext_ref):
    ext_ref[:HALO_C] = jnp.where(i > 0, xp_ref[...], 0.0)
    ext_ref[HALO_C:] = x_ref[...]
    pre = jnp.zeros((tm, QKV_C), F32)
    for k in range(DN_K):
        pre = pre + w_ref[k:k + 1, :] * ext_ref[pl.ds(HALO_C - DN_K + 1 + k, tm), :]
    return pre


def _gdn_specs(T, tm):
    r = tm // HALO_C
    cur = pl.BlockSpec((tm, QKV_C), lambda i: (i, C_QC // QKV_C))
    prev = pl.BlockSpec((HALO_C, QKV_C), lambda i: (jnp.maximum(i * r - 1, 0), C_QC // QKV_C))
    ab = pl.BlockSpec((tm, 128), lambda i: (i, C_AB // 128))
    return cur, prev, ab


def _gdn_prep_fwd(proj, sconv_w, alog_v, dtb_v, name):
    T = proj.shape[0]
    tm = min(512, T)
    cur, prev, ab = _gdn_specs(T, tm)

    def body(x_ref, xp_ref, ab_ref, w_ref, al_ref, dt_ref, q_ref, k_ref, v_ref, gb_ref, ext_ref):
        i = pl.program_id(0)
        y = _silu(_gdn_conv(i, tm, x_ref, xp_ref, w_ref, ext_ref))
        for h in range(DN_HEADS):
            sl = slice(128 * h, 128 * h + 128)
            qh, kh = y[:, sl], y[:, 512 + 128 * h:512 + 128 * h + 128]
            q_ref[:, sl] = qh * lax.rsqrt(jnp.sum(qh * qh, axis=-1, keepdims=True) + EPS) * (128 ** -0.5)
            k_ref[:, sl] = kh * lax.rsqrt(jnp.sum(kh * kh, axis=-1, keepdims=True) + EPS)
        v_ref[...] = y[:, 1024:]
        abv = ab_ref[...]
        lane = lax.broadcasted_iota(jnp.int32, (tm, 128), 1)
        g = -jnp.exp(al_ref[...]) * _softplus(abv + dt_ref[...])
        gb_ref[...] = jnp.where(lane < DN_HEADS, g, _sig(abv))

    full = lambda s: pl.BlockSpec(s, lambda i: (0, 0))
    blk = pl.BlockSpec((tm, 512), lambda i: (i, 0))
    return pl.pallas_call(
        body, name=name, grid=(T // tm,),
        in_specs=[cur, prev, ab, full((DN_K, QKV_C)), full((1, 128)), full((1, 128))],
        out_specs=[blk, blk, blk, pl.BlockSpec((tm, 128), lambda i: (i, 0))],
        out_shape=[jax.ShapeDtypeStruct((T, 512), F32)] * 3 + [jax.ShapeDtypeStruct((T, 128), F32)],
        scratch_shapes=[pltpu.VMEM((tm + HALO_C, QKV_C), F32)],
        compiler_params=_cparams(("parallel",)),
    )(proj, proj, proj, sconv_w, alog_v, dtb_v)


def _hdot(a, b, dims=NN):
    return _dot(a, b, dims, precision=HI)


def _lockstep(gens):
    results, live = [None] * len(gens), list(range(len(gens)))
    while live:
        for i in list(live):
            try:
                next(gens[i])
            except StopIteration as stop:
                results[i] = stop.value
                live.remove(i)
    return results


def _split(a):
    hi = a.astype(BF16)
    return hi, (a - hi.astype(F32)).astype(BF16)


def _dot_exact(a, b, dims=NN, split_left=True):
    x = (a if split_left else b).astype(F32)
    hi = x.astype(BF16)
    r = x - hi.astype(F32)
    mid = r.astype(BF16)
    lo = (r - mid.astype(F32)).astype(BF16)
    other = (b if split_left else a).astype(BF16)
    one = (lambda p: _dot(p, other, dims)) if split_left else (lambda p: _dot(other, p, dims))
    return (one(lo) + one(mid)) + one(hi)


def _dot3(a, b):
    (ah, al), (bh, bl) = a, b
    return _dot(ah, bh) + (_dot(ah, bl) + _dot(al, bh))


def _tri_inv(mats, eye):
    ps = [-a for a in mats]
    ts = [eye + p for p in ps]
    for _ in range(5):
        sp = [_split(p) for p in ps]
        ps = [_dot3(s, s) for s in sp]
        sp = [_split(p) for p in ps]
        ts = [t + _dot3(_split(t), s) for t, s in zip(ts, sp)]
    return ts


def _tri_consts():
    ii = lax.broadcasted_iota(jnp.int32, (CHUNK, CHUNK), 0)
    jj = lax.broadcasted_iota(jnp.int32, (CHUNK, CHUNK), 1)
    return ii >= jj, ii > jj, (ii == jj).astype(F32)


def _gdn_local(q, k, v, gcol, grow, bcol, lower, strict):
    dm = jnp.where(lower, jnp.exp(jnp.where(lower, gcol - grow, 0.0)), 0.0)
    kb = k * bcol
    a = jnp.where(strict, _bdot(kb, k, NT) * dm, 0.0)
    gc = jnp.exp(gcol)
    glast = grow[:, CHUNK - 1:CHUNK]
    return dict(q=q, k=k, v=v, bcol=bcol, gcol=gcol, glast=glast, dm=dm, kb=kb, a=a, gc=gc, vb=v * bcol,
                kbg=kb * gc, p=_bdot(q, k, NT) * dm, qe=q * gc, ke=k * jnp.exp(glast - gcol))


def _gdn_chunk_bwd(c, do, dvn, ds_new, lower, strict, ones):
    rs = lambda m: jnp.sum(m, axis=-1, keepdims=True)
    colsum = lambda m: _dot_exact(m, ones, TN)[:, :1]
    q, k, v, bcol, dm, tm, gc, s = c["q"], c["k"], c["v"], c["bcol"], c["dm"], c["tm"], c["gc"], c["s"]
    eg = jnp.exp(c["glast"])
    dqe = _bdot(do, s, NT)
    dp = jnp.where(lower, _bdot(do, c["vn"], NT), 0.0)
    dw = -_bdot(dvn, s, NT)
    dke = _bdot(c["vn"], ds_new, NT)
    dvb = _bdot(tm, dvn, TN)
    yield
    dglast = jnp.sum(rs(ds_new * s), axis=0, keepdims=True) * eg
    dk = dke * jnp.exp(c["glast"] - c["gcol"])
    r_ke = rs(dke * c["ke"])
    dglast = dglast + jnp.sum(r_ke, axis=0, keepdims=True)
    dgam = rs(dqe * c["qe"]) - r_ke
    dq = dqe * gc
    dpm = dp * dm
    mp = dp * c["p"]
    dq = dq + _bdot(dpm, k)
    dk = dk + _bdot(dpm, q, TN)
    dt = _bdot(dvn, c["vb"], NT) + _bdot(dw, c["kbg"], NT)
    dkbg = _bdot(tm, dw, TN)
    dgam = dgam + rs(mp) - colsum(mp)
    yield
    dkb = dkbg * gc
    dgam = dgam + rs(dkbg * c["kbg"])
    dat = _bdot(tm, dt, TN)
    yield
    da = jnp.where(strict, -_bdot(dat, tm, NT), 0.0)
    yield
    dam = da * dm
    ma = da * c["a"]
    dkb = dkb + _bdot(dam, k)
    dk = dk + _bdot(dam, c["kb"], TN)
    dgam = dgam + rs(ma) - colsum(ma)
    yield
    dk = dk + dkb * bcol
    dbeta = rs(dkb * k) + rs(dvb * v)
    dv = dvb * bcol
    row = lax.broadcasted_iota(jnp.int32, (CHUNK, 1), 0)
    dgam = dgam + jnp.where(row == CHUNK - 1, dglast, 0.0)
    dg = _dot_exact(lower, dgam, TN, split_left=False)
    return dq, dk, dv, dg, dbeta


GROUP = 2


def _chunk_decay(gb_ref, gt_ref, lmat, g):
    rows = slice(CHUNK * g, CHUNK * g + CHUNK)
    return rows, _dot_exact(lmat, gb_ref[rows, :], split_left=False), _dot_exact(gt_ref[g], lmat, NT)


def _gdn_chunk_fwd(qd, kd, vd, gb, gbt, name):
    T = qd.shape[0]
    G = GROUP
    ng = T // (CHUNK * G)

    def body(q_ref, k_ref, v_ref, gb_ref, gt_ref, u_ref, w_ref, qe_ref, ke_ref, p_ref, t_ref, eg_ref):
        lower, strict, eye = _tri_consts()
        lmat = lower.astype(F32)
        decay = [_chunk_decay(gb_ref, gt_ref, lmat, g) for g in range(G)]
        chains = [(g, h) for g in range(G) for h in range(DN_HEADS)]
        cs = []
        for g, h in chains:
            rows, gcs, grs = decay[g]
            sl = slice(128 * h, 128 * h + 128)
            c = _gdn_local(q_ref[rows, sl], k_ref[rows, sl], v_ref[rows, sl], gcs[:, h:h + 1], grs[h:h + 1, :],
                           gb_ref[rows, DN_HEADS + h:DN_HEADS + h + 1], lower, strict)
            qe_ref[rows, sl] = c["qe"].astype(BF16)
            ke_ref[rows, sl] = c["ke"].astype(BF16)
            p_ref[rows, 64 * h:64 * h + 64] = c["p"].astype(BF16)
            eg_ref[g, h:h + 1, :] = jnp.broadcast_to(jnp.exp(c["glast"]), (1, 128))
            cs.append(c)
        tms = [t.astype(BF16) for t in _tri_inv([c["a"] for c in cs], eye)]
        us = [_dot(t, c["vb"].astype(BF16)) for t, c in zip(tms, cs)]
        ws = [_dot(t, c["kbg"].astype(BF16)) for t, c in zip(tms, cs)]
        for (g, h), tm, u, w in zip(chains, tms, us, ws):
            rows, sl = decay[g][0], slice(128 * h, 128 * h + 128)
            u_ref[rows, sl] = u
            w_ref[rows, sl] = w.astype(BF16)
            t_ref[rows, 64 * h:64 * h + 64] = tm
        for g in range(G):
            eg_ref[g, DN_HEADS:, :] = jnp.zeros((8 - DN_HEADS, 128), F32)

    blk = pl.BlockSpec((CHUNK * G, 512), lambda n: (n, 0))
    half = pl.BlockSpec((CHUNK * G, 256), lambda n: (n, 0))
    return pl.pallas_call(
        body, name=name, grid=(ng,),
        in_specs=[blk, blk, blk, pl.BlockSpec((CHUNK * G, 128), lambda n: (n, 0)),
                  pl.BlockSpec((G, 8, CHUNK), lambda n: (n, 0, 0))],
        out_specs=[blk, blk, blk, blk, half, half, pl.BlockSpec((G, 8, 128), lambda n: (n, 0, 0))],
        out_shape=[jax.ShapeDtypeStruct((T, 512), F32)] + [jax.ShapeDtypeStruct((T, 512), BF16)] * 3
        + [jax.ShapeDtypeStruct((T, 256), BF16)] * 2 + [jax.ShapeDtypeStruct((T // CHUNK, 8, 128), F32)],
        compiler_params=_cparams(("parallel",)),
    )(qd, kd, vd, gb, gbt)


def _gdn_scan_fwd(u, w, qe, ke, pm, eg, proj, dn_g, name):
    T = u.shape[0]
    nc = T // CHUNK

    def body(u_ref, w_ref, qe_ref, ke_ref, p_ref, eg_ref, z_ref, ng_ref, y_ref, o_ref, vn_ref, ss_ref, s_ref):
        n = pl.program_id(0)

        @pl.when(n == 0)
        def _():
            s_ref[...] = jnp.zeros_like(s_ref)

        ss_ref[0] = s_ref[...]

        def head(h):
            sl = slice(128 * h, 128 * h + 128)
            s = s_ref[h]
            sb = s.astype(BF16)
            vn = u_ref[:, sl] - _dot(w_ref[:, sl], sb)
            qs = _dot(qe_ref[:, sl], sb)
            yield
            vb = vn.astype(BF16)
            o = qs + _dot(p_ref[:, 64 * h:64 * h + 64], vb)
            s_ref[h] = s * eg_ref[0, h:h + 1, :] + _dot(ke_ref[:, sl], vb, TN)
            yield
            vn_ref[:, sl] = vb
            o_ref[:, sl] = o
            y_ref[:, sl] = _rms(o, None)[0] * ng_ref[...] * _silu(z_ref[:, sl])

        _lockstep([head(h) for h in range(DN_HEADS)])

    blk = pl.BlockSpec((CHUNK, 512), lambda n: (n, 0))
    return pl.pallas_call(
        body, name=name, grid=(nc,),
        in_specs=[blk, blk, blk, blk, pl.BlockSpec((CHUNK, 256), lambda n: (n, 0)),
                  pl.BlockSpec((1, 8, 128), lambda n: (n, 0, 0)),
                  pl.BlockSpec((CHUNK, 512), lambda n: (n, C_ZC // 512)), pl.BlockSpec((1, 128), lambda n: (0, 0))],
        out_specs=[blk, blk, blk, pl.BlockSpec((1, DN_HEADS, 128, 128), lambda n: (n, 0, 0, 0))],
        out_shape=[jax.ShapeDtypeStruct((T, 512), F32), jax.ShapeDtypeStruct((T, 512), F32),
                   jax.ShapeDtypeStruct((T, 512), BF16), jax.ShapeDtypeStruct((nc, DN_HEADS, 128, 128), F32)],
        scratch_shapes=[pltpu.VMEM((DN_HEADS, 128, 128), F32)],
        compiler_params=_cparams(("arbitrary",)),
    )(u, w, qe, ke, pm, eg, proj, dn_g)


def _gdn_scan_bwd(dproj, w, qe, ke, pm, eg, o, proj, dyc, dn_g, name):
    T = o.shape[0]
    nc = T // CHUNK
    rev = lambda n: nc - 1 - n

    def body(dp_any, w_ref, qe_ref, ke_ref, p_ref, eg_ref, o_ref, z_ref, dy_ref, ng_ref,
             dz_ref, do_ref, dvn_ref, dsn_ref, gng_ref, ds_ref):
        n = pl.program_id(0)

        @pl.when(n == 0)
        def _():
            ds_ref[...] = jnp.zeros_like(ds_ref)
            gng_ref[...] = jnp.zeros_like(gng_ref)

        dsn_ref[0] = ds_ref[...]

        def head(h):
            sl = slice(128 * h, 128 * h + 128)
            oh, r = _rms(o_ref[:, sl], None)
            z, dy = z_ref[:, sl], dy_ref[:, sl]
            dz_ref[:, sl] = (dy * (oh * ng_ref[...]) * _dsilu(z)).astype(BF16)
            do, gg = _rms_bwd(dy * _silu(z), oh, r, ng_ref[...])
            dob = do.astype(BF16)
            ds = ds_ref[h]
            dvn = _dot(p_ref[:, 64 * h:64 * h + 64], dob, TN) + _dot(ke_ref[:, sl], ds.astype(BF16))
            qd = _dot(qe_ref[:, sl], dob, TN)
            yield
            dvb = dvn.astype(BF16)
            ds_ref[h] = qd + eg_ref[0, h:h + 1, :] * ds - _dot(w_ref[:, sl], dvb, TN)
            do_ref[:, sl] = dob
            dvn_ref[:, sl] = dvb
            return jnp.sum(gg, axis=0, keepdims=True)

        gng = _lockstep([head(h) for h in range(DN_HEADS)])
        gng_ref[...] += (gng[0] + gng[1]) + (gng[2] + gng[3])

    blk = pl.BlockSpec((CHUNK, 512), lambda n: (rev(n), 0))
    state = pl.BlockSpec((1, DN_HEADS, 128, 128), lambda n: (rev(n), 0, 0, 0))
    return pl.pallas_call(
        body, name=name, grid=(nc,),
        in_specs=[pl.BlockSpec(memory_space=pl.ANY), blk, blk, blk, pl.BlockSpec((CHUNK, 256), lambda n: (rev(n), 0)),
                  pl.BlockSpec((1, 8, 128), lambda n: (rev(n), 0, 0)), blk,
                  pl.BlockSpec((CHUNK, 512), lambda n: (rev(n), C_ZC // 512)), blk,
                  pl.BlockSpec((1, 128), lambda n: (0, 0))],
        out_specs=[pl.BlockSpec((CHUNK, 512), lambda n: (rev(n), C_ZC // 512)), blk, blk, state,
                   pl.BlockSpec((1, 128), lambda n: (0, 0))],
        out_shape=[jax.ShapeDtypeStruct(dproj.shape, BF16), jax.ShapeDtypeStruct((T, 512), BF16),
                   jax.ShapeDtypeStruct((T, 512), BF16), jax.ShapeDtypeStruct((nc, DN_HEADS, 128, 128), F32),
                   jax.ShapeDtypeStruct((1, 128), F32)],
        scratch_shapes=[pltpu.VMEM((DN_HEADS, 128, 128), F32)],
        input_output_aliases={0: 0},
        compiler_params=_cparams(("arbitrary",)),
    )(dproj, w, qe, ke, pm, eg, o, proj, dyc, dn_g)


def _gdn_chunk_grad(qd, kd, vd, gb, gbt, tmi, ssave, dsn, do, dvn, vn, name):
    T = qd.shape[0]
    G = GROUP
    ng = T // (CHUNK * G)

    def body(q_ref, k_ref, v_ref, gb_ref, gt_ref, t_ref, ss_ref, dsn_ref, do_ref, dvn_ref, vn_ref,
             dq_ref, dk_ref, dv_ref, dgb_ref):
        lower, strict, _ = _tri_consts()
        lmat = lower.astype(F32)
        ones = jnp.ones((CHUNK, 128), F32)
        lane = lax.broadcasted_iota(jnp.int32, (CHUNK, 128), 1)
        decay = [_chunk_decay(gb_ref, gt_ref, lmat, g) for g in range(G)]
        chains = [(g, h) for g in range(G) for h in range(DN_HEADS)]
        gens = []
        for g, h in chains:
            rows, gcs, grs = decay[g]
            sl = slice(128 * h, 128 * h + 128)
            c = _gdn_local(q_ref[rows, sl], k_ref[rows, sl], v_ref[rows, sl], gcs[:, h:h + 1], grs[h:h + 1, :],
                           gb_ref[rows, DN_HEADS + h:DN_HEADS + h + 1], lower, strict)
            c.update(tm=t_ref[rows, 64 * h:64 * h + 64], s=ss_ref[g, h], vn=vn_ref[rows, sl])
            gens.append(_gdn_chunk_bwd(c, do_ref[rows, sl], dvn_ref[rows, sl], dsn_ref[g, h], lower, strict, ones))
        dgb = [jnp.zeros((CHUNK, 128), F32) for _ in range(G)]
        for (g, h), (dq, dk, dv, dg, dbeta) in zip(chains, _lockstep(gens)):
            rows, sl = decay[g][0], slice(128 * h, 128 * h + 128)
            dq_ref[rows, sl], dk_ref[rows, sl], dv_ref[rows, sl] = dq, dk, dv
            dgb[g] = dgb[g] + jnp.where(lane == h, dg, 0.0) + jnp.where(lane == DN_HEADS + h, dbeta, 0.0)
        for g in range(G):
            dgb_ref[decay[g][0], :] = dgb[g]

    blk = pl.BlockSpec((CHUNK * G, 512), lambda n: (n, 0))
    half = pl.BlockSpec((CHUNK * G, 256), lambda n: (n, 0))
    nar = pl.BlockSpec((CHUNK * G, 128), lambda n: (n, 0))
    state = pl.BlockSpec((G, DN_HEADS, 128, 128), lambda n: (n, 0, 0, 0))
    return pl.pallas_call(
        body, name=name, grid=(ng,),
        in_specs=[blk, blk, blk, nar, pl.BlockSpec((G, 8, CHUNK), lambda n: (n, 0, 0)), half, state, state,
                  blk, blk, blk],
        out_specs=[blk, blk, blk, nar],
        out_shape=[jax.ShapeDtypeStruct((T, 512), F32)] * 3 + [jax.ShapeDtypeStruct((T, 128), F32)],
        compiler_params=_cparams(("parallel",)),
    )(qd, kd, vd, gb, gbt, tmi, ssave, dsn, do, dvn, vn)


def _gdn_prep_bwd1(dproj, proj, dqd, dkd, dvd, dgb, dkv_a, sconv_w, alog_v, dtb_v, name):
    T = proj.shape[0]
    tm = min(512, T)
    cur, prev, ab = _gdn_specs(T, tm)

    def body(dp_any, x_ref, xp_ref, ab_ref, dq_ref, dk_ref, dv_ref, dgb_ref, dkv_ref, w_ref, al_ref, dt_ref,
             o_ref, dpre_ref, st_ref, ext_ref):
        i = pl.program_id(0)
        pre = _gdn_conv(i, tm, x_ref, xp_ref, w_ref, ext_ref)
        y, dsl = _silu(pre), _dsilu(pre)
        for h in range(DN_HEADS):
            for base, g_ref, scale in ((0, dq_ref, 128 ** -0.5), (512, dk_ref, 1.0)):
                sl = slice(base + 128 * h, base + 128 * h + 128)
                xh = y[:, sl]
                r = lax.rsqrt(jnp.sum(xh * xh, axis=-1, keepdims=True) + EPS)
                xn = xh * r
                gy = g_ref[:, 128 * h:128 * h + 128]
                dpre_ref[:, sl] = (scale * r) * (gy - xn * jnp.sum(gy * xn, axis=-1, keepdims=True)) * dsl[:, sl]
        dpre_ref[:, 1024:] = dv_ref[...] * dsl[:, 1024:]
        abv, dgb = ab_ref[...], dgb_ref[...]
        lane = lax.broadcasted_iota(jnp.int32, (tm, 128), 1)
        na = -jnp.exp(al_ref[...])
        xs = abv + dt_ref[...]
        da = dgb * na * _sig(xs)
        b = _sig(abv)
        o_ref[:, :256] = dkv_ref[...].astype(BF16)
        o_ref[:, 256:] = jnp.where(lane < DN_HEADS, da,
                                   jnp.where(lane < 2 * DN_HEADS, dgb * b * (1.0 - b), 0.0)).astype(BF16)
        head = lane < DN_HEADS
        upd = jnp.concatenate([jnp.sum(jnp.where(head, dgb * na * _softplus(xs), 0.0), axis=0, keepdims=True),
                               jnp.sum(jnp.where(head, da, 0.0), axis=0, keepdims=True), jnp.zeros((6, 128), F32)],
                              axis=0)

        @pl.when(i == 0)
        def _():
            st_ref[...] = upd

        @pl.when(i > 0)
        def _():
            st_ref[...] += upd

    full = lambda s: pl.BlockSpec(s, lambda i: (0, 0))
    blk = pl.BlockSpec((tm, 512), lambda i: (i, 0))
    return pl.pallas_call(
        body, name=name, grid=(T // tm,),
        in_specs=[pl.BlockSpec(memory_space=pl.ANY), cur, prev, ab, blk, blk, blk,
                  pl.BlockSpec((tm, 128), lambda i: (i, 0)), pl.BlockSpec((tm, 256), lambda i: (i, 0)),
                  full((DN_K, QKV_C)), full((1, 128)), full((1, 128))],
        out_specs=[pl.BlockSpec((tm, 384), lambda i: (i, C_KA // 384)),
                   pl.BlockSpec((tm, QKV_C), lambda i: (i, 0)), full((8, 128))],
        out_shape=[jax.ShapeDtypeStruct(dproj.shape, BF16), jax.ShapeDtypeStruct((T, QKV_C), F32),
                   jax.ShapeDtypeStruct((8, 128), F32)],
        scratch_shapes=[pltpu.VMEM((tm + HALO_C, QKV_C), F32)],
        input_output_aliases={0: 0},
        compiler_params=_cparams(("arbitrary",)),
    )(dproj, proj, proj, proj, dqd, dkd, dvd, dgb, dkv_a, sconv_w, alog_v, dtb_v)


def _gdn_prep_bwd2(dproj, proj, dpre, sconv_w, name):
    T = proj.shape[0]
    tm = min(512, T)
    nt = T // tm
    r = tm // HALO_C
    cur, prev, _ = _gdn_specs(T, tm)

    def body(dp_any, x_ref, xp_ref, d_ref, dn_ref, w_ref, dx_ref, gw_ref, extx_ref, extd_ref):
        i = pl.program_id(0)
        extx_ref[:HALO_C] = jnp.where(i > 0, xp_ref[...], 0.0)
        extx_ref[HALO_C:] = x_ref[...]
        d = d_ref[...]
        extd_ref[:tm] = d
        extd_ref[tm:] = jnp.where(i < nt - 1, dn_ref[...], 0.0)
        dx = jnp.zeros((tm, QKV_C), F32)
        rows = []
        for k in range(DN_K):
            dx = dx + w_ref[k:k + 1, :] * extd_ref[pl.ds(DN_K - 1 - k, tm), :]
            rows.append(jnp.sum(d * extx_ref[pl.ds(HALO_C - DN_K + 1 + k, tm), :], axis=0, keepdims=True))
        rows.append(jnp.zeros((8 - DN_K, QKV_C), F32))
        gw = jnp.concatenate(rows, axis=0)
        dx_ref[...] = dx.astype(BF16)

        @pl.when(i == 0)
        def _():
            gw_ref[...] = gw

        @pl.when(i > 0)
        def _():
            gw_ref[...] += gw

    full = lambda s: pl.BlockSpec(s, lambda i: (0, 0))
    return pl.pallas_call(
        body, name=name, grid=(nt,),
        in_specs=[pl.BlockSpec(memory_space=pl.ANY), cur, prev, pl.BlockSpec((tm, QKV_C), lambda i: (i, 0)),
                  pl.BlockSpec((HALO_C, QKV_C), lambda i: (jnp.minimum((i + 1) * r, T // HALO_C - 1), 0)),
                  full((DN_K, QKV_C))],
        out_specs=[cur, full((8, QKV_C))],
        out_shape=[jax.ShapeDtypeStruct(dproj.shape, BF16), jax.ShapeDtypeStruct((8, QKV_C), F32)],
        scratch_shapes=[pltpu.VMEM((tm + HALO_C, QKV_C), F32), pltpu.VMEM((tm + HALO_C, QKV_C), F32)],
        input_output_aliases={0: 0},
        compiler_params=_cparams(("arbitrary",)),
    )(dproj, proj, proj, dpre, dpre, sconv_w)


def _merge_fwd(x, proj, ya, yb, yc, wa, wb, wc, wo, gate, name):
    T = x.shape[0]
    tm = min(256, T)

    def body(x_ref, mg_ref, ya_ref, yb_ref, yc_ref, wa_ref, wb_ref, wc_ref, wo_ref, gate_ref, o_ref):
        merged = (_sig(mg_ref[:, :D]) * _bdot(ya_ref[...], wa_ref[...])
                  + _sig(mg_ref[:, D:2 * D]) * _bdot(yb_ref[...], wb_ref[...])
                  + _sig(mg_ref[:, 2 * D:]) * _bdot(yc_ref[...], wc_ref[...]))
        o_ref[...] = x_ref[...] + gate_ref[...] * _bdot(merged, wo_ref[...])

    full = lambda s: pl.BlockSpec(s, lambda i: (0, 0))
    yb_ = pl.BlockSpec((tm, 512), lambda i: (i, 0))
    return pl.pallas_call(
        body, name=name, grid=(T // tm,),
        in_specs=[pl.BlockSpec((tm, D), lambda i: (i, 0)), pl.BlockSpec((tm, 3 * D), lambda i: (i, 0)), yb_, yb_, yb_,
                  full((512, D)), full((512, D)), full((512, D)), full((D, D)), full((1, D))],
        out_specs=pl.BlockSpec((tm, D), lambda i: (i, 0)),
        out_shape=jax.ShapeDtypeStruct((T, D), F32),
        compiler_params=_cparams(("parallel",)),
    )(x, proj, ya, yb, yc, wa, wb, wc, wo, _row(gate))


def _merge_bwd(dout, proj, ya, yb, yc, wa, wb, wc, wo, gate, name):
    T = dout.shape[0]
    tm = min(256, T)
    nt = T // tm

    def body(do_ref, mg_ref, ya_ref, yb_ref, yc_ref, wa_ref, wb_ref, wc_ref, wo_ref, gate_ref,
             dmg_ref, dya_ref, dyb_ref, dyc_ref, gwa_hbm, gwb_hbm, gwc_hbm, gwo_hbm, gg_ref,
             gwa_ref, gwb_ref, gwc_ref, gwo_ref):
        i = pl.program_id(0)

        @pl.when(i == 0)
        def _():
            for r in (gwa_ref, gwb_ref, gwc_ref, gwo_ref, gg_ref):
                r[...] = jnp.zeros_like(r)

        ys = (ya_ref[...], yb_ref[...], yc_ref[...])
        ws = (wa_ref, wb_ref, wc_ref)
        gs = tuple(_sig(mg_ref[:, j * D:(j + 1) * D]) for j in range(3))
        ps = tuple(_bdot(ys[j], ws[j][...]) for j in range(3))
        merged = gs[0] * ps[0] + gs[1] * ps[1] + gs[2] * ps[2]
        mo = _bdot(merged, wo_ref[...])
        do = do_ref[...]
        gg_ref[...] += jnp.sum(do * mo, axis=0, keepdims=True)
        dmo = do * gate_ref[...]
        dmerged = _bdot(dmo, wo_ref[...], NT)
        gwo_ref[...] += _bdot(merged, dmo, TN)
        for j, (dy_ref, gw_ref) in enumerate(((dya_ref, gwa_ref), (dyb_ref, gwb_ref), (dyc_ref, gwc_ref))):
            dp = dmerged * gs[j]
            dmg_ref[:, j * D:(j + 1) * D] = (dmerged * ps[j] * gs[j] * (1.0 - gs[j])).astype(BF16)
            dy_ref[...] = _bdot(dp, ws[j][...], NT)
            gw_ref[...] += _bdot(ys[j], dp, TN)

        @pl.when(i == nt - 1)
        def _():
            for src, dst in ((gwa_ref, gwa_hbm), (gwb_ref, gwb_hbm), (gwc_ref, gwc_hbm), (gwo_ref, gwo_hbm)):
                pltpu.sync_copy(src, dst)

    full = lambda s: pl.BlockSpec(s, lambda i: (0, 0))
    yb_ = pl.BlockSpec((tm, 512), lambda i: (i, 0))
    anyspec = pl.BlockSpec(memory_space=pl.ANY)
    return pl.pallas_call(
        body, name=name, grid=(nt,),
        in_specs=[pl.BlockSpec((tm, D), lambda i: (i, 0)), pl.BlockSpec((tm, 3 * D), lambda i: (i, 0)), yb_, yb_, yb_,
                  full((512, D)), full((512, D)), full((512, D)), full((D, D)), full((1, D))],
        out_specs=[pl.BlockSpec((tm, 3 * D), lambda i: (i, 0)), yb_, yb_, yb_, anyspec, anyspec, anyspec, anyspec,
                   full((1, D))],
        out_shape=[jax.ShapeDtypeStruct((T, NP), BF16)] + [jax.ShapeDtypeStruct((T, 512), F32)] * 3
        + [jax.ShapeDtypeStruct((512, D), F32)] * 3 + [jax.ShapeDtypeStruct((D, D), F32), jax.ShapeDtypeStruct((1, D), F32)],
        scratch_shapes=[pltpu.VMEM((512, D), F32)] * 3 + [pltpu.VMEM((D, D), F32)],
        compiler_params=_cparams(("arbitrary",)),
    )(dout, proj, ya, yb, yc, wa, wb, wc, wo, _row(gate))


def _loss_head(y, tgt, name):
    T = y.shape[0]
    tm = min(512, T)

    def body(y_ref, t_ref, dy_ref, l_ref):
        i = pl.program_id(0)
        diff = y_ref[...] - t_ref[...]
        dy_ref[...] = diff * (1.0 / D)
        part = jnp.sum(diff * diff, axis=0, keepdims=True)

        @pl.when(i == 0)
        def _():
            l_ref[...] = part

        @pl.when(i > 0)
        def _():
            l_ref[...] += part

    blk = pl.BlockSpec((tm, D), lambda i: (i, 0))
    return pl.pallas_call(
        body, name=name, grid=(T // tm,), in_specs=[blk, blk],
        out_specs=[blk, pl.BlockSpec((1, D), lambda i: (0, 0))],
        out_shape=[jax.ShapeDtypeStruct((T, D), F32), jax.ShapeDtypeStruct((1, D), F32)],
        compiler_params=_cparams(("arbitrary",)),
    )(y, tgt)


def _ada_fwd(c_all, w_ada, b_my, name):
    def body(c_ref, w_ref, b_ref, o_ref):
        sc = _silu(c_ref[...])
        for l in range(DEPTH):
            o_ref[l] = _bdot(sc, w_ref[l]) + b_ref[l:l + 1, :]

    return pl.pallas_call(body, name=name, out_shape=jax.ShapeDtypeStruct((DEPTH, N_DEV, w_ada.shape[2]), F32),
                          compiler_params=_cparams())(c_all, w_ada, b_my)


def _ada_bwd(c_all, dmod_my, name):
    def body(c_ref, d_ref, o_ref):
        sc = _silu(c_ref[...])
        for l in range(DEPTH):
            o_ref[l] = _bdot(sc, d_ref[l], TN)

    return pl.pallas_call(body, name=name, out_shape=jax.ShapeDtypeStruct((DEPTH, D, dmod_my.shape[2]), F32),
                          compiler_params=_cparams())(c_all, dmod_my)


def _adam_math(w, g, m, v):
    m = ADAM_B1 * m + (1.0 - ADAM_B1) * g
    v = ADAM_B2 * v + (1.0 - ADAM_B2) * (g * g)
    m_hat = m / (1.0 - ADAM_B1 ** ADAM_STEP)
    v_hat = v / (1.0 - ADAM_B2 ** ADAM_STEP)
    return -ADAM_LR * (m_hat / (jnp.sqrt(v_hat) + ADAM_EPS) + ADAM_WD * w), m, v


def _row_tile(rows, cap):
    best = rows
    for t in range(8, min(rows, cap) + 1, 8):
        if rows % t == 0:
            best = t
    return best if best <= cap else rows


def _adamw(w, g, m, v, name):
    R, C = w.shape
    tr = _row_tile(R, 256)

    def body(w_ref, g_ref, m_ref, v_ref, d_ref, mo_ref, vo_ref):
        d_ref[...], mo_ref[...], vo_ref[...] = _adam_math(w_ref[...], g_ref[...], m_ref[...], v_ref[...])

    blk = pl.BlockSpec((tr, C), lambda i: (i, 0))
    return pl.pallas_call(body, name=name, grid=(R // tr,), in_specs=[blk] * 4, out_specs=[blk] * 3,
                          out_shape=[jax.ShapeDtypeStruct((R, C), F32)] * 3,
                          compiler_params=_cparams(("parallel",)))(w, g, m, v)


def _sum_parts(parts, name):
    _, R, C = parts.shape
    tr = _row_tile(R, 256)

    def body(p_ref, o_ref):
        acc = p_ref[0]
        for j in range(1, N_DEV):
            acc = acc + p_ref[j]
        o_ref[...] = acc

    return pl.pallas_call(body, name=name, grid=(R // tr,),
                          in_specs=[pl.BlockSpec((N_DEV, tr, C), lambda i: (0, i, 0))],
                          out_specs=pl.BlockSpec((tr, C), lambda i: (i, 0)),
                          out_shape=jax.ShapeDtypeStruct((R, C), F32), compiler_params=_cparams(("parallel",)))(parts)


def _sum_adamw(parts, w, m, v, name):
    P, R, C = parts.shape
    tr = _row_tile(R, 128)

    def body(p_ref, w_ref, m_ref, v_ref, g_ref, d_ref, mo_ref, vo_ref):
        g = p_ref[0].astype(F32)
        for j in range(1, P):
            g = g + p_ref[j].astype(F32)
        g_ref[...] = g
        d_ref[...], mo_ref[...], vo_ref[...] = _adam_math(w_ref[...], g, m_ref[...], v_ref[...])

    blk = pl.BlockSpec((tr, C), lambda i: (i, 0))
    return pl.pallas_call(body, name=name, grid=(R // tr,),
                          in_specs=[pl.BlockSpec((P, tr, C), lambda i: (0, i, 0)), blk, blk, blk],
                          out_specs=[blk] * 4, out_shape=[jax.ShapeDtypeStruct((R, C), F32)] * 4,
                          compiler_params=_cparams(("parallel",)))(parts, w, m, v)


def _pair_sum(core, buf, recv, name):
    _, _, R, C = buf.shape
    tr = _row_tile(R, 128)

    def body(c_ref, a_ref, b_ref, o_ref):
        o_ref[...] = (a_ref[:, 0].astype(F32) + b_ref[...].astype(F32)).astype(BF16)

    return pl.pallas_call(
        body, name=name,
        grid_spec=pltpu.PrefetchScalarGridSpec(
            num_scalar_prefetch=1, grid=(R // tr,),
            in_specs=[pl.BlockSpec((4, 1, tr, C), lambda i, c: (0, c[0], i, 0)),
                      pl.BlockSpec((4, tr, C), lambda i, c: (0, i, 0))],
            out_specs=pl.BlockSpec((4, tr, C), lambda i, c: (0, i, 0))),
        out_shape=jax.ShapeDtypeStruct((4, R, C), BF16),
        compiler_params=_cparams(("parallel",)))(core, buf, recv)


SHARD_IN = D_IN // N_DEV


def _w_in_pieces():
    out, p = [], 0
    for a, b in _PAD_FROM:
        for j in range(N_DEV):
            lo, hi = max(a, SHARD_IN * j), min(b, SHARD_IN * (j + 1))
            if lo < hi:
                out.append((j, lo - SHARD_IN * j, hi - SHARD_IN * j, p + lo - a))
        p += b - a
    return out


def _assemble_w_in(gw, name):
    tr = 256
    nt = D // tr

    def body(x_ref, o_ref):
        for j, s0, s1, d0 in _w_in_pieces():
            o_ref[0, :, d0:d0 + s1 - s0] = x_ref[j, :, s0:s1]
        o_ref[0, :, D_IN:] = jnp.zeros((tr, NP - D_IN), gw.dtype)

    return pl.pallas_call(
        body, name=name, grid=(DEPTH, nt),
        in_specs=[pl.BlockSpec((N_DEV, tr, SHARD_IN), lambda l, i: (0, l * nt + i, 0))],
        out_specs=pl.BlockSpec((1, tr, NP), lambda l, i: (l, i, 0)),
        out_shape=jax.ShapeDtypeStruct((DEPTH, D, NP), gw.dtype),
        compiler_params=_cparams(("parallel", "parallel")))(gw)


def _split_w_in_grad(g0, g1, name):
    tr = 256
    nt = D // tr

    def body(g0_ref, g1_ref, o_ref):
        l = pl.program_id(0)

        def emit(g_ref):
            for j, s0, s1, d0 in _w_in_pieces():
                o_ref[j, :, s0:s1] = g_ref[:, d0:d0 + s1 - s0].astype(BF16)

        @pl.when(l == 0)
        def _():
            emit(g0_ref)

        @pl.when(l == 1)
        def _():
            emit(g1_ref)

    return pl.pallas_call(
        body, name=name, grid=(DEPTH, nt),
        in_specs=[pl.BlockSpec((tr, NP), lambda l, i: (i * (1 - l) + (nt - 1) * l, 0)),
                  pl.BlockSpec((tr, NP), lambda l, i: (i * l, 0))],
        out_specs=pl.BlockSpec((N_DEV, tr, SHARD_IN), lambda l, i: (0, l * nt + i, 0)),
        out_shape=jax.ShapeDtypeStruct((N_DEV, DEPTH * D, SHARD_IN), BF16),
        compiler_params=_cparams(("arbitrary", "arbitrary")))(g0, g1)


def _mesh_pos():
    return lax.axis_index("x"), lax.axis_index("y"), lax.axis_index("c")


def _all_gather(blocks, name):
    n = len(blocks)

    def body(*refs):
        ins, outs = refs[:n], refs[n:2 * n]
        send_sems, recv_sems, local_sems = refs[2 * n:]
        x, y, c = _mesh_pos()
        me, sibling = (x, y, c), (x, y, 1 - c)
        chips = [(1 - x, y), (x, 1 - y), (1 - x, 1 - y)]
        idx = lambda p: 4 * p[0] + 2 * p[1] + p[2]

        def copy(a, k, block, to, src=None):
            dst = outs[a].at[idx(block)]
            return pltpu.make_async_remote_copy(
                src_ref=dst if src is None else src, dst_ref=dst, send_sem=send_sems.at[a, k],
                recv_sem=recv_sems.at[a, k], device_id=to, device_id_type=pl.DeviceIdType.MESH)

        mine = [pltpu.make_async_copy(ins[a], outs[a].at[idx(me)], local_sems.at[a]) for a in range(n)]
        for cp in mine:
            cp.start()
        first = []
        for a in range(n):
            first.append(copy(a, 0, me, sibling, src=ins[a]))
            first += [copy(a, 1 + j, me, (*chip, c), src=ins[a]) for j, chip in enumerate(chips)]
        for cp in first:
            cp.start()
        passed = []
        for j, chip in enumerate(chips):
            for a in range(n):
                copy(a, 1 + j, (*chip, c), me).wait_recv()
                cp = copy(a, 4 + j, (*chip, c), sibling)
                cp.start()
                passed.append(cp)
        for a in range(n):
            copy(a, 0, sibling, me).wait_recv()
            for j, chip in enumerate(chips):
                copy(a, 4 + j, (*chip, 1 - c), me).wait_recv()
        for cp in first + passed:
            cp.wait_send()
        for cp in mine:
            cp.wait()

    anyspec = pl.BlockSpec(memory_space=pl.ANY)
    return pl.pallas_call(
        body, name=name, in_specs=[anyspec] * n, out_specs=[anyspec] * n,
        out_shape=[jax.ShapeDtypeStruct((N_DEV,) + b.shape, b.dtype) for b in blocks],
        scratch_shapes=[pltpu.SemaphoreType.DMA((n, 7)), pltpu.SemaphoreType.DMA((n, 7)),
                        pltpu.SemaphoreType.DMA((n,))],
    )(*blocks)


def _exchange_core(bufs, name):
    n = len(bufs)

    def body(*refs):
        ins, outs = refs[:n], refs[n:2 * n]
        send_sems, recv_sems = refs[2 * n:]
        x, y, c = _mesh_pos()
        copies = []
        for a in range(n):
            for q in range(4):
                cp = pltpu.make_async_remote_copy(
                    src_ref=ins[a].at[q, 1 - c], dst_ref=outs[a].at[q], send_sem=send_sems.at[a, q],
                    recv_sem=recv_sems.at[a, q], device_id=(x, y, 1 - c), device_id_type=pl.DeviceIdType.MESH)
                cp.start()
                copies.append(cp)
        for cp in copies:
            cp.wait()

    anyspec = pl.BlockSpec(memory_space=pl.ANY)
    return pl.pallas_call(
        body, name=name, in_specs=[anyspec] * n, out_specs=[anyspec] * n,
        out_shape=[jax.ShapeDtypeStruct((4,) + b.shape[2:], b.dtype) for b in bufs],
        scratch_shapes=[pltpu.SemaphoreType.DMA((n, 4)), pltpu.SemaphoreType.DMA((n, 4))],
    )(*bufs)


def _exchange_chips(bufs, name):
    n = len(bufs)

    def body(*refs):
        ins, outs = refs[:n], refs[n:2 * n]
        send_sems, recv_sems, local_sems = refs[2 * n:]
        x, y, c = _mesh_pos()
        chip = 2 * x + y
        local = [pltpu.make_async_copy(ins[a].at[chip], outs[a].at[chip], local_sems.at[a]) for a in range(n)]
        for cp in local:
            cp.start()
        copies = []
        for k in range(1, 4):
            px = 1 - x if k & 2 else x
            py = 1 - y if k & 1 else y
            for a in range(n):
                cp = pltpu.make_async_remote_copy(
                    src_ref=ins[a].at[2 * px + py], dst_ref=outs[a].at[chip], send_sem=send_sems.at[a, k - 1],
                    recv_sem=recv_sems.at[a, k - 1], device_id=(px, py, c), device_id_type=pl.DeviceIdType.MESH)
                cp.start()
                copies.append(cp)
        for cp in copies:
            cp.wait()
        for cp in local:
            cp.wait()

    anyspec = pl.BlockSpec(memory_space=pl.ANY)
    return pl.pallas_call(
        body, name=name, in_specs=[anyspec] * n, out_specs=[anyspec] * n,
        out_shape=[jax.ShapeDtypeStruct(b.shape, b.dtype) for b in bufs],
        scratch_shapes=[pltpu.SemaphoreType.DMA((n, 3)), pltpu.SemaphoreType.DMA((n, 3)),
                        pltpu.SemaphoreType.DMA((n,))],
    )(*bufs)


def _pack_flat(parts, rows, cols, lead=()):
    flat = jnp.concatenate([p.reshape(lead + (-1,)) for p in parts], axis=-1)
    pad = rows * cols - flat.shape[-1]
    flat = jnp.pad(flat, [(0, 0)] * len(lead) + [(0, pad)])
    return flat.reshape(lead + (rows, cols))


def _unpack_flat(buf, shapes, lead=()):
    flat = buf.reshape(lead + (-1,))
    out, o = [], 0
    for s in shapes:
        n = int(np.prod(s))
        out.append(flat[..., o:o + n].reshape(lead + tuple(s)))
        o += n
    return out


_SMALL = (("b_ada", (3 * D,)), ("norm_g", (D,)), ("q_norm_g", (64,)), ("k_norm_g", (64,)), ("sinks", (8,)),
          ("dw_b", (512,)), ("ln_g", (512,)), ("ln_b", (512,)), ("pw2_b", (512,)), ("a_log", (4,)),
          ("dt_bias", (4,)), ("dn_norm_g", (128,)), ("dw_w", (CONV_K, 512)), ("sconv_w", (DN_K, QKV_C)))
_N_REPL = 12
_SMALL_SHAPES = tuple((DEPTH,) + s for _, s in _SMALL)
_SMALL_ROWS = -(-sum(int(np.prod(s)) for s in _SMALL_SHAPES) // (128 * 8)) * 8
_UPD_SHAPES = _SMALL_SHAPES[:_N_REPL] + ((DEPTH, CONV_K, 512 // N_DEV), (DEPTH, DN_K, QKV_C // N_DEV))
_UPD_ROWS = -(-sum(int(np.prod(s)) for s in _UPD_SHAPES) // (128 * 8)) * 8


def _lane4(v):
    return jnp.pad(v, (0, 124)).reshape(1, 128)


def kernel(x, c, w_ada, b_ada, norm_g, w_in, q_norm_g, k_norm_g, sinks, dw_w, dw_b, ln_g, ln_b, pw2_w, pw2_b, sconv_w, a_log, dt_bias, dn_norm_g, w_proj_a, w_proj_b, w_proj_c, w_out, loss_target, m_w_ada, m_b_ada, m_norm_g, m_w_in, m_q_norm_g, m_k_norm_g, m_sinks, m_dw_w, m_dw_b, m_ln_g, m_ln_b, m_pw2_w, m_pw2_b, m_sconv_w, m_a_log, m_dt_bias, m_dn_norm_g, m_w_proj_a, m_w_proj_b, m_w_proj_c, m_w_out, v_w_ada, v_b_ada, v_norm_g, v_w_in, v_q_norm_g, v_k_norm_g, v_sinks, v_dw_w, v_dw_b, v_ln_g, v_ln_b, v_pw2_w, v_pw2_b, v_sconv_w, v_a_log, v_dt_bias, v_dn_norm_g, v_w_proj_a, v_w_proj_b, v_w_proj_c, v_w_out):
    T = x.shape[1]
    nc = T // CHUNK
    xi, yi, ci = _mesh_pos()
    me = 4 * xi + 2 * yi + ci
    big_w = (w_in, pw2_w, w_proj_a, w_proj_b, w_proj_c, w_out)
    big_m = (m_w_in, m_pw2_w, m_w_proj_a, m_w_proj_b, m_w_proj_c, m_w_out)
    big_v = (v_w_in, v_pw2_w, v_w_proj_a, v_w_proj_b, v_w_proj_c, v_w_out)

    ada_cols = w_ada.shape[2]
    dw_cols, sc_cols = dw_w.shape[2], sconv_w.shape[2]
    flat2 = lambda a: a.reshape(-1, a.shape[-1])
    small_shapes = ((D,), dw_w.shape, sconv_w.shape)
    small_rows = -(-sum(int(np.prod(s)) for s in small_shapes) // (128 * 8)) * 8
    small32 = _pack_flat([c, dw_w, sconv_w], small_rows, 128)
    gw_in, gpw2, gpa, gpb, gpc, gwo, g32 = _all_gather([flat2(a.astype(BF16)) for a in big_w] + [small32],
                                                        "gather_weights")
    wp = _assemble_w_in(gw_in, "assemble_w_in")
    pw2_f = gpw2.reshape(N_DEV, DEPTH, -1, 512).transpose(1, 0, 2, 3).reshape(DEPTH, 512, 512)
    wa_f, wb_f, wc_f = (g.reshape(N_DEV, DEPTH, 512, -1).transpose(1, 2, 0, 3).reshape(DEPTH, 512, D)
                        for g in (gpa, gpb, gpc))
    wo_f = gwo.reshape(N_DEV, DEPTH, -1, D).transpose(1, 0, 2, 3).reshape(DEPTH, D, D)
    c_all, gdw, gsc = _unpack_flat(g32, small_shapes, lead=(N_DEV,))
    dw_f = gdw.transpose(1, 2, 0, 3).reshape(DEPTH, CONV_K, 512)
    sc_f = gsc.transpose(1, 2, 0, 3).reshape(DEPTH, DN_K, QKV_C)

    b_my = lax.dynamic_slice(b_ada, (0, me * ada_cols), (DEPTH, ada_cols))
    mod_part = _ada_fwd(c_all, w_ada, b_my, "ada_fwd")
    (gmod,) = _all_gather([mod_part.reshape(-1, 128)], "gather_mod")
    mod_all = gmod.reshape(N_DEV, DEPTH, N_DEV, ada_cols).transpose(1, 2, 0, 3).reshape(DEPTH, N_DEV, 3 * D)
    mod = lax.dynamic_index_in_dim(mod_all, me, axis=1, keepdims=False)
    shift, scale, gate = mod[:, :D], mod[:, D:2 * D], mod[:, 2 * D:]

    xs, saved = [x[0]], []
    for l in range(DEPTH):
        xl = xs[-1]
        h = _norm_fwd(xl, norm_g[l], scale[l], shift[l], f"norm_fwd{l}")
        proj = _mm(h, wp[l], tm=min(1024, T), tn=1152, tk=D, name=f"in_proj{l}")
        ya = _attn_fwd(proj, q_norm_g[l], k_norm_g[l], sinks[l], f"attn_fwd{l}")
        yb = _conf_fwd(proj, dw_f[l], dw_b[l], ln_g[l], ln_b[l], pw2_f[l], pw2_b[l], f"conf_fwd{l}")
        alv, dtv, dng = _lane4(a_log[l]), _lane4(dt_bias[l]), _row(dn_norm_g[l])
        qd, kd, vd, gb = _gdn_prep_fwd(proj, sc_f[l], alv, dtv, f"gdn_prep_fwd{l}")
        gbt = gb[:, :8].reshape(nc, CHUNK, 8).transpose(0, 2, 1)
        u, w, qe, ke, pm, tmi, eg = _gdn_chunk_fwd(qd, kd, vd, gb, gbt, f"gdn_chunk_fwd{l}")
        yc, o, vn, ss = _gdn_scan_fwd(u, w, qe, ke, pm, eg, proj, dng, f"gdn_scan_fwd{l}")
        xs.append(_merge_fwd(xl, proj, ya, yb, yc, wa_f[l], wb_f[l], wc_f[l], wo_f[l], gate[l], f"merge_fwd{l}"))
        saved.append((h, proj, ya, yb, yc, qd, kd, vd, gb, gbt, ss, alv, dtv, dng, w, qe, ke, pm, tmi, eg, o, vn))

    dout, lsum = _loss_head(xs[-1], loss_target[0], "loss_head")
    loss = lax.psum(0.5 * jnp.sum(lsum) / D, ("x", "y", "c"))

    small = {name: [None] * DEPTH for name, _ in _SMALL}
    big_g = [[None] * DEPTH for _ in big_w]
    for l in reversed(range(DEPTH)):
        h, proj, ya, yb, yc, qd, kd, vd, gb, gbt, ss, alv, dtv, dng, w, qe, ke, pm, tmi, eg, o, vn = saved[l]
        dproj, dya, dyb, dyc, g_wa, g_wb, g_wc, g_wo, g_gate = _merge_bwd(
            dout, proj, ya, yb, yc, wa_f[l], wb_f[l], wc_f[l], wo_f[l], gate[l], f"merge_bwd{l}")
        dproj, dkv_a, g_q, g_k, g_s = _attn_bwd(dproj, proj, dya, q_norm_g[l], k_norm_g[l], sinks[l], f"attn_bwd{l}")
        dproj, du1, g_pw2, st_b = _conf_bwd1(dproj, proj, dyb, dw_f[l], dw_b[l], ln_g[l], ln_b[l], pw2_f[l], pw2_b[l],
                                             f"conf_bwd_a{l}")
        dproj, g_dw = _conf_bwd2(dproj, proj, du1, dw_f[l], f"conf_bwd_b{l}")
        dproj, do, dvn, dsn, g_dn = _gdn_scan_bwd(dproj, w, qe, ke, pm, eg, o, proj, dyc, dng, f"gdn_scan_bwd{l}")
        dqd, dkd, dvd, dgb = _gdn_chunk_grad(qd, kd, vd, gb, gbt, tmi, ss, dsn, do, dvn, vn, f"gdn_chunk_bwd{l}")
        dproj, dpre, st_c = _gdn_prep_bwd1(dproj, proj, dqd, dkd, dvd, dgb, dkv_a, sc_f[l], alv, dtv,
                                           f"gdn_prep_bwd_a{l}")
        dproj, g_sc = _gdn_prep_bwd2(dproj, proj, dpre, sc_f[l], f"gdn_prep_bwd_b{l}")
        dh = _mm(dproj, wp[l], tb=True, tm=min(1024, T), tn=D, tk=1152, name=f"d_h{l}")
        g_wp = _mm(h, dproj, ta=True, tm=D, tn=1152, tk=min(1024, T), name=f"d_w_in{l}")
        dout, st_n = _norm_bwd(dh, xs[l], dout, norm_g[l], scale[l], f"norm_bwd{l}")
        for i, g in enumerate((g_wp, g_pw2, g_wa, g_wb, g_wc, g_wo)):
            big_g[i][l] = g
        for name, g in (("b_ada", jnp.concatenate([st_n[0], st_n[1], g_gate[0]])), ("norm_g", st_n[2]),
                        ("q_norm_g", g_q.reshape(ATT_HEADS, ATT_HD).sum(0)), ("k_norm_g", g_k.reshape(2, ATT_HD).sum(0)),
                        ("sinks", g_s[0]), ("dw_b", st_b[3]),
                        ("ln_g", st_b[1]), ("ln_b", st_b[2]), ("pw2_b", st_b[0]), ("a_log", st_c[0, :4]),
                        ("dt_bias", st_c[1, :4]), ("dn_norm_g", g_dn[0]), ("dw_w", g_dw[:CONV_K]),
                        ("sconv_w", g_sc[:DN_K])):
            small[name][l] = g
    grad_x = dout[None]

    part = _pack_flat([jnp.stack(small[name]) for name, _ in _SMALL], _SMALL_ROWS, 128)
    (gpart,) = _all_gather([part], "gather_small_grads")
    dmod_all = gpart.reshape(N_DEV, -1)[:, :DEPTH * 3 * D].reshape(N_DEV, DEPTH, 3 * D)
    dmod_my = lax.dynamic_slice(dmod_all, (0, 0, me * ada_cols), (N_DEV, DEPTH, ada_cols)).transpose(1, 0, 2)
    g_w_ada = _ada_bwd(c_all, dmod_my, "ada_bwd")
    tot = _unpack_flat(_sum_parts(gpart, "sum_small_grads"), _SMALL_SHAPES)
    g_small = dict(zip([n for n, _ in _SMALL], tot))
    g_small["dw_w"] = lax.dynamic_slice(g_small["dw_w"], (0, 0, me * dw_cols), (DEPTH, CONV_K, dw_cols))
    g_small["sconv_w"] = lax.dynamic_slice(g_small["sconv_w"], (0, 0, me * sc_cols), (DEPTH, DN_K, sc_cols))
    env = dict(b_ada=(b_ada, m_b_ada, v_b_ada), norm_g=(norm_g, m_norm_g, v_norm_g),
               q_norm_g=(q_norm_g, m_q_norm_g, v_q_norm_g), k_norm_g=(k_norm_g, m_k_norm_g, v_k_norm_g),
               sinks=(sinks, m_sinks, v_sinks), dw_b=(dw_b, m_dw_b, v_dw_b), ln_g=(ln_g, m_ln_g, v_ln_g),
               ln_b=(ln_b, m_ln_b, v_ln_b), pw2_b=(pw2_b, m_pw2_b, v_pw2_b), a_log=(a_log, m_a_log, v_a_log),
               dt_bias=(dt_bias, m_dt_bias, v_dt_bias), dn_norm_g=(dn_norm_g, m_dn_norm_g, v_dn_norm_g),
               dw_w=(dw_w, m_dw_w, v_dw_w), sconv_w=(sconv_w, m_sconv_w, v_sconv_w))
    names = [n for n, _ in _SMALL]
    pk = lambda k: _pack_flat([env[n][k] for n in names], _UPD_ROWS, 128)
    upd = _adamw(pk(0), _pack_flat([g_small[n] for n in names], _UPD_ROWS, 128), pk(1), pk(2), "adamw_small")
    d_small, m_small, v_small = (dict(zip(names, _unpack_flat(u, _UPD_SHAPES))) for u in upd)

    d_ada, nm_ada, nv_ada = (u.reshape(w_ada.shape) for u in
                             _adamw(flat2(w_ada), flat2(g_w_ada), flat2(m_w_ada), flat2(v_w_ada), "adamw_w_ada"))

    g_pw, g_a, g_b, g_c, g_o = (jnp.stack(g) for g in big_g[1:])
    by_dest = [_split_w_in_grad(big_g[0][0], big_g[0][1], "split_w_in_grad"),
               g_pw.reshape(DEPTH, N_DEV, -1, 512).transpose(1, 0, 2, 3).astype(BF16)]
    by_dest += [g.reshape(DEPTH, 512, N_DEV, -1).transpose(2, 0, 1, 3).astype(BF16) for g in (g_a, g_b, g_c)]
    by_dest.append(g_o.reshape(DEPTH, N_DEV, -1, D).transpose(1, 0, 2, 3).astype(BF16))
    by_dest = [b.reshape(4, 2, -1, b.shape[-1]) for b in by_dest]
    from_sibling = _exchange_core(by_dest, "exchange_grads_core")
    core = jnp.reshape(ci, (1,)).astype(jnp.int32)
    chip_sums = [_pair_sum(core, b, r, f"pair_sum{i}") for i, (b, r) in enumerate(zip(by_dest, from_sibling))]
    parts = _exchange_chips(chip_sums, "exchange_grads_chips")
    res = [_sum_adamw(p, flat2(w), flat2(m), flat2(v), f"sum_adamw{i}")
           for i, (p, w, m, v) in enumerate(zip(parts, big_w, big_m, big_v))]
    g_big, d_big, m_big, v_big = ([r[k].reshape(w.shape) for r, w in zip(res, big_w)] for k in range(4))

    order = ("w_ada", "b_ada", "norm_g", "w_in", "q_norm_g", "k_norm_g", "sinks", "dw_w", "dw_b", "ln_g", "ln_b",
             "pw2_w", "pw2_b", "sconv_w", "a_log", "dt_bias", "dn_norm_g", "w_proj_a", "w_proj_b", "w_proj_c", "w_out")
    big_names = ("w_in", "pw2_w", "w_proj_a", "w_proj_b", "w_proj_c", "w_out")

    def pick(kind):
        src_small = (g_small, d_small, m_small, v_small)[kind]
        src_big = (g_big, d_big, m_big, v_big)[kind]
        src_ada = (g_w_ada, d_ada, nm_ada, nv_ada)[kind]
        return [src_ada if n == "w_ada" else src_big[big_names.index(n)] if n in big_names else src_small[n]
                for n in order]

    return (loss, grad_x, *pick(0), *pick(1), *pick(2), *pick(3))
```

```python
import functools
import math

import jax
import jax.numpy as jnp
import numpy as np
from jax import lax
from jax.experimental import pallas as pl
from jax.experimental.pallas import tpu as pltpu

F32 = jnp.float32
BF16 = jnp.bfloat16
HI = lax.Precision.HIGHEST

N_DEV = 8
D = 1024
DEPTH = 2
EPS = 1e-6
NEG_INF = -1e30
WINDOW = 128
ATT_HEADS = 8
ATT_HD = 64
CONV_K = 31
DN_HEADS = 4
DN_K = 4
CHUNK = 64
D_IN = 7944
VMEM_LIMIT = 56 * 1024 * 1024

C_MG, C_QA, C_ZA, C_ZB, C_QC, C_KC, C_VC, C_GV, C_GG, C_ZC, C_KA, C_VA, C_AB, NP = (
    0, 3072, 3584, 4096, 4608, 5120, 5632, 6144, 6656, 7168, 7680, 7808, 7936, 8064)
_PAD_FROM = ((4872, 7944), (0, 512), (768, 1280), (2304, 2816), (2816, 4352), (1280, 2304), (4360, 4872),
             (512, 768), (4352, 4360))

ALIBI = tuple(float(2.0 ** (-8.0 * (h + 1) / ATT_HEADS)) for h in range(ATT_HEADS))

ADAM_LR, ADAM_B1, ADAM_B2, ADAM_EPS, ADAM_WD, ADAM_STEP = 0.001, 0.9, 0.999, 1e-08, 0.01, 10


def _cparams(sem=None):
    return pltpu.CompilerParams(dimension_semantics=sem, vmem_limit_bytes=VMEM_LIMIT)


def _sig(x):
    return jax.nn.sigmoid(x)


def _silu(x):
    return x * _sig(x)


def _dsilu(x):
    s = _sig(x)
    return s * (1.0 + x * (1.0 - s))


def _dot(a, b, dims=((1,), (0,)), precision=None):
    return lax.dot_general(a, b, (dims, ((), ())), preferred_element_type=F32, precision=precision)


def _bdot(a, b, dims=((1,), (0,))):
    return _dot(a.astype(BF16), b.astype(BF16), dims)


NN, NT, TN = ((1,), (0,)), ((1,), (1,)), ((0,), (0,))


def _row(v):
    return v.reshape(1, -1)


def _mm(a, b, *, ta=False, tb=False, tm, tn, tk, name):
    M, K = (a.shape[1], a.shape[0]) if ta else a.shape
    N = b.shape[0] if tb else b.shape[1]
    assert M % tm == 0 and N % tn == 0 and K % tk == 0, (M, N, K, tm, tn, tk)
    nk = K // tk
    dims = ((0 if ta else 1,), (1 if tb else 0,))

    def body(a_ref, b_ref, o_ref):
        k = pl.program_id(2)
        part = _bdot(a_ref[...], b_ref[...], dims)

        @pl.when(k == 0)
        def _():
            o_ref[...] = part

        @pl.when(k > 0)
        def _():
            o_ref[...] += part

    a_spec = pl.BlockSpec((tk, tm), lambda i, j, k: (k, i)) if ta else pl.BlockSpec((tm, tk), lambda i, j, k: (i, k))
    b_spec = pl.BlockSpec((tn, tk), lambda i, j, k: (j, k)) if tb else pl.BlockSpec((tk, tn), lambda i, j, k: (k, j))
    return pl.pallas_call(
        body, name=name, grid=(M // tm, N // tn, nk),
        in_specs=[a_spec, b_spec], out_specs=pl.BlockSpec((tm, tn), lambda i, j, k: (i, j)),
        out_shape=jax.ShapeDtypeStruct((M, N), F32),
        compiler_params=_cparams(("parallel", "parallel", "arbitrary")),
    )(a, b)


def _norm_fwd(x, norm_g, scale, shift, name):
    T = x.shape[0]
    tm = min(512, T)

    def body(x_ref, g_ref, sc_ref, sh_ref, h_ref):
        xv = x_ref[...]
        r = lax.rsqrt(jnp.mean(xv * xv, axis=-1, keepdims=True) + EPS)
        h_ref[...] = ((xv * r) * g_ref[...] * (1.0 + sc_ref[...]) + sh_ref[...]).astype(BF16)

    vec = pl.BlockSpec((1, D), lambda i: (0, 0))
    return pl.pallas_call(
        body, name=name, grid=(T // tm,),
        in_specs=[pl.BlockSpec((tm, D), lambda i: (i, 0)), vec, vec, vec],
        out_specs=pl.BlockSpec((tm, D), lambda i: (i, 0)),
        out_shape=jax.ShapeDtypeStruct((T, D), BF16),
        compiler_params=_cparams(("parallel",)),
    )(x, _row(norm_g), _row(scale), _row(shift))


def _norm_bwd(dh, x, dres, norm_g, scale, name):
    T = x.shape[0]
    tm = min(512, T)

    def body(dh_ref, x_ref, dr_ref, g_ref, sc_ref, dx_ref, st_ref):
        i = pl.program_id(0)
        xv, dhv = x_ref[...], dh_ref[...]
        r = lax.rsqrt(jnp.mean(xv * xv, axis=-1, keepdims=True) + EPS)
        xh = xv * r
        g, s1 = g_ref[...], 1.0 + sc_ref[...]
        dxh = dhv * (g * s1)
        dx_ref[...] = dr_ref[...] + r * (dxh - xh * jnp.mean(dxh * xh, axis=-1, keepdims=True))
        dhx = dhv * xh
        upd = jnp.concatenate([jnp.sum(dhv, axis=0, keepdims=True), jnp.sum(dhx * g, axis=0, keepdims=True),
                               jnp.sum(dhx * s1, axis=0, keepdims=True), jnp.zeros((5, D), F32)], axis=0)

        @pl.when(i == 0)
        def _():
            st_ref[...] = upd

        @pl.when(i > 0)
        def _():
            st_ref[...] += upd

    vec = pl.BlockSpec((1, D), lambda i: (0, 0))
    blk = pl.BlockSpec((tm, D), lambda i: (i, 0))
    return pl.pallas_call(
        body, name=name, grid=(T // tm,),
        in_specs=[blk, blk, blk, vec, vec],
        out_specs=[blk, pl.BlockSpec((8, D), lambda i: (0, 0))],
        out_shape=[jax.ShapeDtypeStruct((T, D), F32), jax.ShapeDtypeStruct((8, D), F32)],
        compiler_params=_cparams(("arbitrary",)),
    )(dh, x, dres, _row(norm_g), _row(scale))


def _rms(x, g):
    r = lax.rsqrt(jnp.mean(x * x, axis=-1, keepdims=True) + EPS)
    return x * r, r


def _head_mean_matrix():
    head = np.arange(ATT_HEADS * ATT_HD) // ATT_HD
    return jnp.asarray((head[:, None] == head[None, :]) * (1.0 / ATT_HD), BF16)


def _head_rms(x, hm):
    r = lax.rsqrt(_dot_exact(x * x, hm) + EPS)
    return x * r, r


def _head_rms_bwd(dy, xh, r, g, hm):
    dxh = dy * g
    return r * (dxh - xh * _dot_exact(dxh * xh, hm)), dy * xh


def _attn_mask(n):
    qi = lax.broadcasted_iota(jnp.int32, (WINDOW, 2 * WINDOW), 0)
    kj = lax.broadcasted_iota(jnp.int32, (WINDOW, 2 * WINDOW), 1)
    dist = qi + WINDOW - kj
    valid = (dist >= 0) & (dist < WINDOW) & ((n > 0) | (kj >= WINDOW))
    return valid, dist.astype(F32)


def _attn_probs(s, h, sink, valid, distf):
    s = s - ALIBI[h] * distf
    s = jnp.where(valid, s, NEG_INF)
    m = jnp.maximum(jnp.max(s, axis=-1, keepdims=True), sink)
    p = jnp.exp(s - m)
    es = jnp.exp(sink - m)
    den = jnp.sum(p, axis=-1, keepdims=True) + es
    return p / den, es / den


def _attn_fwd(proj, q_norm_g, k_norm_g, sinks, name):
    T = proj.shape[0]
    nb = T // WINDOW

    def body(sink_ref, q_ref, z_ref, kc_ref, kp_ref, vc_ref, vp_ref, qg_ref, kg_ref, hm_ref, o_ref):
        n = pl.program_id(0)
        valid, distf = _attn_mask(n)
        k2 = jnp.concatenate([kp_ref[...], kc_ref[...]], axis=0)
        v2 = jnp.concatenate([vp_ref[...], vc_ref[...]], axis=0).astype(BF16)
        kn = (_head_rms(k2, hm_ref[:128, :128])[0] * kg_ref[...]).astype(BF16)
        qn = ((_head_rms(q_ref[...], hm_ref[...])[0] * qg_ref[...]) * (ATT_HD ** -0.5)).astype(BF16)

        def head(h):
            sl, gsl = slice(64 * h, 64 * h + 64), slice(64 * (h // 4), 64 * (h // 4) + 64)
            s = _dot(qn[:, sl], kn[:, gsl], NT)
            yield
            p, _ = _attn_probs(s, h, sink_ref[h], valid, distf)
            o_ref[:, sl] = _dot(p.astype(BF16), v2[:, gsl])
            yield

        _lockstep([head(h) for h in range(ATT_HEADS)])
        o_ref[...] = o_ref[...] * _silu(z_ref[...])

    prev = lambda n: jnp.maximum(n - 1, 0)
    return pl.pallas_call(
        body, name=name, grid=(nb,),
        in_specs=[pl.BlockSpec(memory_space=pltpu.SMEM),
                  pl.BlockSpec((WINDOW, 512), lambda n: (n, C_QA // 512)),
                  pl.BlockSpec((WINDOW, 512), lambda n: (n, C_ZA // 512)),
                  pl.BlockSpec((WINDOW, 128), lambda n: (n, C_KA // 128)),
                  pl.BlockSpec((WINDOW, 128), lambda n: (prev(n), C_KA // 128)),
                  pl.BlockSpec((WINDOW, 128), lambda n: (n, C_VA // 128)),
                  pl.BlockSpec((WINDOW, 128), lambda n: (prev(n), C_VA // 128)),
                  pl.BlockSpec((1, 512), lambda n: (0, 0)), pl.BlockSpec((1, 128), lambda n: (0, 0)),
                  pl.BlockSpec((512, 512), lambda n: (0, 0))],
        out_specs=pl.BlockSpec((WINDOW, 512), lambda n: (n, 0)),
        out_shape=jax.ShapeDtypeStruct((T, 512), F32),
        compiler_params=_cparams(("parallel",)),
    )(sinks, proj, proj, proj, proj, proj, proj, _row(jnp.tile(q_norm_g, ATT_HEADS)), _row(jnp.tile(k_norm_g, 2)),
      _head_mean_matrix())


def _rms_bwd(dy, xh, r, g):
    dxh = dy * g
    return r * (dxh - xh * jnp.mean(dxh * xh, axis=-1, keepdims=True)), dy * xh


def _attn_bwd(dproj, proj, dya, q_norm_g, k_norm_g, sinks, name):
    T = proj.shape[0]
    nb = T // WINDOW

    def body(sink_ref, dp_any, q_ref, z_ref, kc_ref, kp_ref, vc_ref, vp_ref, dy_ref, qg_ref, kg_ref, hm_ref,
             dqz_ref, dkv_ref, gq_ref, gk_ref, gs_ref, ck_ref, cv_ref, o_sc, dq_sc):
        n = pl.program_id(0)

        @pl.when(n == 0)
        def _():
            gq_ref[...] = jnp.zeros_like(gq_ref)
            gk_ref[...] = jnp.zeros_like(gk_ref)
            gs_ref[...] = jnp.zeros_like(gs_ref)
            ck_ref[...] = jnp.zeros_like(ck_ref)
            cv_ref[...] = jnp.zeros_like(cv_ref)

        lane8 = lax.broadcasted_iota(jnp.int32, (1, 8), 1)

        @pl.when(n < nb)
        def _():
            valid, distf = _attn_mask(n)
            k2 = jnp.concatenate([kp_ref[...], kc_ref[...]], axis=0)
            v2 = jnp.concatenate([vp_ref[...], vc_ref[...]], axis=0).astype(BF16)
            kn = (_head_rms(k2, hm_ref[:128, :128])[0] * kg_ref[...]).astype(BF16)
            qh, qr = _head_rms(q_ref[...], hm_ref[...])
            qn = ((qh * qg_ref[...]) * (ATT_HD ** -0.5)).astype(BF16)
            zs = z_ref[...]
            do_all = dy_ref[...] * _silu(zs)
            dob_all = do_all.astype(BF16)

            def head(h):
                sl, gsl = slice(64 * h, 64 * h + 64), slice(64 * (h // 4), 64 * (h // 4) + 64)
                s = _dot(qn[:, sl], kn[:, gsl], NT)
                dpm = _dot(dob_all[:, sl], v2[:, gsl], NT)
                yield
                p, ps = _attn_probs(s, h, sink_ref[h], valid, distf)
                pb = p.astype(BF16)
                o_sc[:, sl] = _dot(pb, v2[:, gsl])
                dvg = _dot(pb, dob_all[:, sl], TN)
                delta = jnp.sum(p * dpm, axis=-1, keepdims=True)
                ds = (p * (dpm - delta)).astype(BF16)
                gs = jnp.where(lane8 == h, -jnp.sum(ps * delta, axis=0, keepdims=True), 0.0)
                yield
                dkn = _dot(ds, qn[:, sl], TN)
                dq_sc[:, sl] = _dot(ds, kn[:, gsl])
                yield
                return dkn, dvg, gs

            res = _lockstep([head(h) for h in range(ATT_HEADS)])
            dqz_ref[:, 512:] = (dy_ref[...] * o_sc[...] * _dsilu(zs)).astype(BF16)
            dq, gq = _head_rms_bwd(dq_sc[...] * (ATT_HD ** -0.5), qh, qr, qg_ref[...], hm_ref[...])
            dqz_ref[:, :512] = dq.astype(BF16)
            gq_acc = jnp.sum(gq, axis=0, keepdims=True)
            gs_acc = sum(r[2] for r in res[1:]) + res[0][2]
            for g in range(2):
                dkn = (res[4 * g][0] + res[4 * g + 1][0]) + (res[4 * g + 2][0] + res[4 * g + 3][0])
                dvg = (res[4 * g][1] + res[4 * g + 1][1]) + (res[4 * g + 2][1] + res[4 * g + 3][1])
                ksl = slice(64 * g, 64 * g + 64)
                vsl = slice(128 + 64 * g, 128 + 64 * g + 64)
                dkv_ref[:, ksl] = ck_ref[:, ksl] + dkn[:WINDOW]
                dkv_ref[:, vsl] = cv_ref[:, ksl] + dvg[:WINDOW]
                ck_ref[:, ksl] = dkn[WINDOW:]
                cv_ref[:, ksl] = dvg[WINDOW:]
            gq_ref[...] += gq_acc
            gs_ref[...] += gs_acc

        @pl.when(n == nb)
        def _():
            dkv_ref[:, :128] = ck_ref[...]
            dkv_ref[:, 128:] = cv_ref[...]

        @pl.when(n > 0)
        def _():
            hm = hm_ref[:128, :128]
            kh, kr = _head_rms(kp_ref[...], hm)
            dk, gk = _head_rms_bwd(dkv_ref[:, :128], kh, kr, kg_ref[...], hm)
            dkv_ref[:, :128] = dk
            gk_ref[...] += jnp.sum(gk, axis=0, keepdims=True)

    cur = lambda n: jnp.minimum(n, nb - 1)
    prev = lambda n: jnp.maximum(n - 1, 0)
    small = lambda w: pl.BlockSpec((1, w), lambda n: (0, 0))
    return pl.pallas_call(
        body, name=name, grid=(nb + 1,),
        in_specs=[pl.BlockSpec(memory_space=pltpu.SMEM), pl.BlockSpec(memory_space=pl.ANY),
                  pl.BlockSpec((WINDOW, 512), lambda n: (cur(n), C_QA // 512)),
                  pl.BlockSpec((WINDOW, 512), lambda n: (cur(n), C_ZA // 512)),
                  pl.BlockSpec((WINDOW, 128), lambda n: (cur(n), C_KA // 128)),
                  pl.BlockSpec((WINDOW, 128), lambda n: (prev(n), C_KA // 128)),
                  pl.BlockSpec((WINDOW, 128), lambda n: (cur(n), C_VA // 128)),
                  pl.BlockSpec((WINDOW, 128), lambda n: (prev(n), C_VA // 128)),
                  pl.BlockSpec((WINDOW, 512), lambda n: (cur(n), 0)),
                  small(512), small(128), pl.BlockSpec((512, 512), lambda n: (0, 0))],
        out_specs=[pl.BlockSpec((WINDOW, 1024), lambda n: (cur(n), C_QA // 1024)),
                   pl.BlockSpec((WINDOW, 256), lambda n: (prev(n), 0)),
                   small(512), small(128), small(8)],
        out_shape=[jax.ShapeDtypeStruct(dproj.shape, BF16), jax.ShapeDtypeStruct((T, 256), F32),
                   jax.ShapeDtypeStruct((1, 512), F32), jax.ShapeDtypeStruct((1, 128), F32),
                   jax.ShapeDtypeStruct((1, 8), F32)],
        scratch_shapes=[pltpu.VMEM((WINDOW, 128), F32), pltpu.VMEM((WINDOW, 128), F32),
                        pltpu.VMEM((WINDOW, 512), F32), pltpu.VMEM((WINDOW, 512), F32)],
        input_output_aliases={1: 0},
        compiler_params=_cparams(("arbitrary",)),
    )(sinks, dproj, proj, proj, proj, proj, proj, proj, dya, _row(jnp.tile(q_norm_g, ATT_HEADS)),
      _row(jnp.tile(k_norm_g, 2)), _head_mean_matrix())


HALO_B = 32


def _conf_specs(T, tm):
    r = tm // HALO_B
    cur = lambda c: pl.BlockSpec((tm, 512), lambda i: (i, c // 512))
    prev = lambda c: pl.BlockSpec((HALO_B, 512), lambda i: (jnp.maximum(i * r - 1, 0), c // 512))
    return cur, prev


def _conf_core(i, tm, gv_ref, gg_ref, gvp_ref, ggp_ref, w_ref, b_ref, lg_ref, lb_ref, pw_ref, pb_ref, ext_ref):
    up = gvp_ref[...] * _sig(ggp_ref[...])
    ext_ref[:HALO_B] = jnp.where(i > 0, up, 0.0)
    ext_ref[HALO_B:] = gv_ref[...] * _sig(gg_ref[...])
    acc = jnp.zeros((tm, 512), F32) + b_ref[...]
    for k in range(CONV_K):
        acc = acc + w_ref[k:k + 1, :] * ext_ref[pl.ds(HALO_B - CONV_K + 1 + k, tm), :]
    mu = jnp.mean(acc, axis=-1, keepdims=True)
    xc = acc - mu
    rstd = lax.rsqrt(jnp.mean(xc * xc, axis=-1, keepdims=True) + EPS)
    xh = xc * rstd
    u2 = xh * lg_ref[...] + lb_ref[...]
    u3 = _silu(u2)
    ypre = _bdot(u3, pw_ref[...]) + pb_ref[...]
    return xh, rstd, u2, u3, ypre


def _conf_fwd(proj, dw_w, dw_b, ln_g, ln_b, pw2, pw2_b, name):
    T = proj.shape[0]
    tm = min(512, T)
    cur, prev = _conf_specs(T, tm)

    def body(gv_ref, gg_ref, gvp_ref, ggp_ref, zb_ref, w_ref, b_ref, lg_ref, lb_ref, pw_ref, pb_ref, o_ref, ext_ref):
        i = pl.program_id(0)
        ypre = _conf_core(i, tm, gv_ref, gg_ref, gvp_ref, ggp_ref, w_ref, b_ref, lg_ref, lb_ref, pw_ref, pb_ref,
                          ext_ref)[4]
        o_ref[...] = ypre * _silu(zb_ref[...])

    full = lambda s: pl.BlockSpec(s, lambda i: (0, 0))
    return pl.pallas_call(
        body, name=name, grid=(T // tm,),
        in_specs=[cur(C_GV), cur(C_GG), prev(C_GV), prev(C_GG), cur(C_ZB), full((CONV_K, 512)), full((1, 512)),
                  full((1, 512)), full((1, 512)), full((512, 512)), full((1, 512))],
        out_specs=pl.BlockSpec((tm, 512), lambda i: (i, 0)),
        out_shape=jax.ShapeDtypeStruct((T, 512), F32),
        scratch_shapes=[pltpu.VMEM((tm + HALO_B, 512), F32)],
        compiler_params=_cparams(("parallel",)),
    )(proj, proj, proj, proj, proj, dw_w, _row(dw_b), _row(ln_g), _row(ln_b), pw2, _row(pw2_b))


def _conf_bwd1(dproj, proj, dyb, dw_w, dw_b, ln_g, ln_b, pw2, pw2_b, name):
    T = proj.shape[0]
    tm = min(512, T)
    cur, prev = _conf_specs(T, tm)

    def body(dp_any, gv_ref, gg_ref, gvp_ref, ggp_ref, zb_ref, dy_ref, w_ref, b_ref, lg_ref, lb_ref, pw_ref, pb_ref,
             dzb_ref, du1_ref, gpw_ref, st_ref, ext_ref):
        i = pl.program_id(0)
        xh, rstd, u2, u3, ypre = _conf_core(i, tm, gv_ref, gg_ref, gvp_ref, ggp_ref, w_ref, b_ref, lg_ref, lb_ref,
                                            pw_ref, pb_ref, ext_ref)
        zb, dy = zb_ref[...], dy_ref[...]
        dzb_ref[...] = (dy * ypre * _dsilu(zb)).astype(BF16)
        dyp = dy * _silu(zb)
        du2 = _bdot(dyp, pw_ref[...], NT) * _dsilu(u2)
        dxh = du2 * lg_ref[...]
        du1 = rstd * (dxh - jnp.mean(dxh, axis=-1, keepdims=True) - xh * jnp.mean(dxh * xh, axis=-1, keepdims=True))
        du1_ref[...] = du1
        gpw = _bdot(u3, dyp, TN)
        rs = lambda a: jnp.sum(a, axis=0, keepdims=True)
        upd = jnp.concatenate([rs(dyp), rs(du2 * xh), rs(du2), rs(du1), jnp.zeros((4, 512), F32)], axis=0)

        @pl.when(i == 0)
        def _():
            gpw_ref[...] = gpw
            st_ref[...] = upd

        @pl.when(i > 0)
        def _():
            gpw_ref[...] += gpw
            st_ref[...] += upd

    full = lambda s: pl.BlockSpec(s, lambda i: (0, 0))
    blk = pl.BlockSpec((tm, 512), lambda i: (i, 0))
    return pl.pallas_call(
        body, name=name, grid=(T // tm,),
        in_specs=[pl.BlockSpec(memory_space=pl.ANY), cur(C_GV), cur(C_GG), prev(C_GV), prev(C_GG), cur(C_ZB), blk,
                  full((CONV_K, 512)), full((1, 512)), full((1, 512)), full((1, 512)), full((512, 512)), full((1, 512))],
        out_specs=[cur(C_ZB), blk, full((512, 512)), full((8, 512))],
        out_shape=[jax.ShapeDtypeStruct(dproj.shape, BF16), jax.ShapeDtypeStruct((T, 512), F32),
                   jax.ShapeDtypeStruct((512, 512), F32), jax.ShapeDtypeStruct((8, 512), F32)],
        scratch_shapes=[pltpu.VMEM((tm + HALO_B, 512), F32)],
        input_output_aliases={0: 0},
        compiler_params=_cparams(("arbitrary",)),
    )(dproj, proj, proj, proj, proj, proj, dyb, dw_w, _row(dw_b), _row(ln_g), _row(ln_b), pw2, _row(pw2_b))


def _conf_bwd2(dproj, proj, du1, dw_w, name):
    T = proj.shape[0]
    tm = min(512, T)
    nt = T // tm
    r = tm // HALO_B
    cur, prev = _conf_specs(T, tm)

    def body(dp_any, gv_ref, gg_ref, gvp_ref, ggp_ref, du_ref, dun_ref, w_ref, dglu_ref, gw_ref, extu_ref, extd_ref):
        i = pl.program_id(0)
        gv, sg = gv_ref[...], _sig(gg_ref[...])
        extu_ref[:HALO_B] = jnp.where(i > 0, gvp_ref[...] * _sig(ggp_ref[...]), 0.0)
        extu_ref[HALO_B:] = gv * sg
        du1 = du_ref[...]
        extd_ref[:tm] = du1
        extd_ref[tm:] = jnp.where(i < nt - 1, dun_ref[...], 0.0)
        du0 = jnp.zeros((tm, 512), F32)
        rows = []
        for k in range(CONV_K):
            du0 = du0 + w_ref[k:k + 1, :] * extd_ref[pl.ds(CONV_K - 1 - k, tm), :]
            rows.append(jnp.sum(du1 * extu_ref[pl.ds(HALO_B - CONV_K + 1 + k, tm), :], axis=0, keepdims=True))
        rows.append(jnp.zeros((1, 512), F32))
        gw = jnp.concatenate(rows, axis=0)
        dglu_ref[:, :512] = (du0 * sg).astype(BF16)
        dglu_ref[:, 512:] = (du0 * gv * sg * (1.0 - sg)).astype(BF16)

        @pl.when(i == 0)
        def _():
            gw_ref[...] = gw

        @pl.when(i > 0)
        def _():
            gw_ref[...] += gw

    full = lambda s: pl.BlockSpec(s, lambda i: (0, 0))
    return pl.pallas_call(
        body, name=name, grid=(nt,),
        in_specs=[pl.BlockSpec(memory_space=pl.ANY), cur(C_GV), cur(C_GG), prev(C_GV), prev(C_GG),
                  pl.BlockSpec((tm, 512), lambda i: (i, 0)),
                  pl.BlockSpec((HALO_B, 512), lambda i: (jnp.minimum((i + 1) * r, T // HALO_B - 1), 0)),
                  full((CONV_K, 512))],
        out_specs=[pl.BlockSpec((tm, 1024), lambda i: (i, C_GV // 1024)), full((32, 512))],
        out_shape=[jax.ShapeDtypeStruct(dproj.shape, BF16), jax.ShapeDtypeStruct((32, 512), F32)],
        scratch_shapes=[pltpu.VMEM((tm + HALO_B, 512), F32), pltpu.VMEM((tm + HALO_B, 512), F32)],
        input_output_aliases={0: 0},
        compiler_params=_cparams(("arbitrary",)),
    )(dproj, proj, proj, proj, proj, du1, du1, dw_w)


HALO_C = 8
QKV_C = 1536


def _softplus(x):
    return jnp.maximum(x, 0.0) + jnp.log1p(jnp.exp(-jnp.abs(x)))


def _gdn_conv(i, tm, x_ref, xp_ref, w_ref, ext_ref):
    ext_ref[:HALO_C] = jnp.where(i > 0, xp_ref[...], 0.0)
    ext_ref[HALO_C:] = x_ref[...]
    pre = jnp.zeros((tm, QKV_C), F32)
    for k in range(DN_K):
        pre = pre + w_ref[k:k + 1, :] * ext_ref[pl.ds(HALO_C - DN_K + 1 + k, tm), :]
    return pre


def _gdn_specs(T, tm):
    r = tm // HALO_C
    cur = pl.BlockSpec((tm, QKV_C), lambda i: (i, C_QC // QKV_C))
    prev = pl.BlockSpec((HALO_C, QKV_C), lambda i: (jnp.maximum(i * r - 1, 0), C_QC // QKV_C))
    ab = pl.BlockSpec((tm, 128), lambda i: (i, C_AB // 128))
    return cur, prev, ab


def _gdn_prep_fwd(proj, sconv_w, alog_v, dtb_v, name):
    T = proj.shape[0]
    tm = min(512, T)
    cur, prev, ab = _gdn_specs(T, tm)

    def body(x_ref, xp_ref, ab_ref, w_ref, al_ref, dt_ref, q_ref, k_ref, v_ref, gb_ref, ext_ref):
        i = pl.program_id(0)
        y = _silu(_gdn_conv(i, tm, x_ref, xp_ref, w_ref, ext_ref))
        for h in range(DN_HEADS):
            sl = slice(128 * h, 128 * h + 128)
            qh, kh = y[:, sl], y[:, 512 + 128 * h:512 + 128 * h + 128]
            q_ref[:, sl] = qh * lax.rsqrt(jnp.sum(qh * qh, axis=-1, keepdims=True) + EPS) * (128 ** -0.5)
            k_ref[:, sl] = kh * lax.rsqrt(jnp.sum(kh * kh, axis=-1, keepdims=True) + EPS)
        v_ref[...] = y[:, 1024:]
        abv = ab_ref[...]
        lane = lax.broadcasted_iota(jnp.int32, (tm, 128), 1)
        g = -jnp.exp(al_ref[...]) * _softplus(abv + dt_ref[...])
        gb_ref[...] = jnp.where(lane < DN_HEADS, g, _sig(abv))

    full = lambda s: pl.BlockSpec(s, lambda i: (0, 0))
    blk = pl.BlockSpec((tm, 512), lambda i: (i, 0))
    return pl.pallas_call(
        body, name=name, grid=(T // tm,),
        in_specs=[cur, prev, ab, full((DN_K, QKV_C)), full((1, 128)), full((1, 128))],
        out_specs=[blk, blk, blk, pl.BlockSpec((tm, 128), lambda i: (i, 0))],
        out_shape=[jax.ShapeDtypeStruct((T, 512), F32)] * 3 + [jax.ShapeDtypeStruct((T, 128), F32)],
        scratch_shapes=[pltpu.VMEM((tm + HALO_C, QKV_C), F32)],
        compiler_params=_cparams(("parallel",)),
    )(proj, proj, proj, sconv_w, alog_v, dtb_v)


def _hdot(a, b, dims=NN):
    return _dot(a, b, dims, precision=HI)


def _lockstep(gens):
    results, live = [None] * len(gens), list(range(len(gens)))
    while live:
        for i in list(live):
            try:
                next(gens[i])
            except StopIteration as stop:
                results[i] = stop.value
                live.remove(i)
    return results


def _split(a):
    hi = a.astype(BF16)
    return hi, (a - hi.astype(F32)).astype(BF16)


def _dot_exact(a, b, dims=NN, split_left=True):
    x = (a if split_left else b).astype(F32)
    hi = x.astype(BF16)
    r = x - hi.astype(F32)
    mid = r.astype(BF16)
    lo = (r - mid.astype(F32)).astype(BF16)
    other = (b if split_left else a).astype(BF16)
    one = (lambda p: _dot(p, other, dims)) if split_left else (lambda p: _dot(other, p, dims))
    return (one(lo) + one(mid)) + one(hi)


def _dot3(a, b):
    (ah, al), (bh, bl) = a, b
    return _dot(ah, bh) + (_dot(ah, bl) + _dot(al, bh))


def _tri_inv(mats, eye):
    ps = [-a for a in mats]
    ts = [eye + p for p in ps]
    for _ in range(5):
        sp = [_split(p) for p in ps]
        ps = [_dot3(s, s) for s in sp]
        sp = [_split(p) for p in ps]
        ts = [t + _dot3(_split(t), s) for t, s in zip(ts, sp)]
    return ts


def _tri_consts():
    ii = lax.broadcasted_iota(jnp.int32, (CHUNK, CHUNK), 0)
    jj = lax.broadcasted_iota(jnp.int32, (CHUNK, CHUNK), 1)
    return ii >= jj, ii > jj, (ii == jj).astype(F32)


def _gdn_local(q, k, v, gcol, grow, bcol, lower, strict):
    dm = jnp.where(lower, jnp.exp(jnp.where(lower, gcol - grow, 0.0)), 0.0)
    kb = k * bcol
    a = jnp.where(strict, _bdot(kb, k, NT) * dm, 0.0)
    gc = jnp.exp(gcol)
    glast = grow[:, CHUNK - 1:CHUNK]
    return dict(q=q, k=k, v=v, bcol=bcol, gcol=gcol, glast=glast, dm=dm, kb=kb, a=a, gc=gc, vb=v * bcol,
                kbg=kb * gc, p=_bdot(q, k, NT) * dm, qe=q * gc, ke=k * jnp.exp(glast - gcol))


def _gdn_chunk_bwd(c, do, dvn, ds_new, lower, strict, ones):
    rs = lambda m: jnp.sum(m, axis=-1, keepdims=True)
    colsum = lambda m: _dot_exact(m, ones, TN)[:, :1]
    q, k, v, bcol, dm, tm, gc, s = c["q"], c["k"], c["v"], c["bcol"], c["dm"], c["tm"], c["gc"], c["s"]
    eg = jnp.exp(c["glast"])
    dqe = _bdot(do, s, NT)
    dp = jnp.where(lower, _bdot(do, c["vn"], NT), 0.0)
    dw = -_bdot(dvn, s, NT)
    dke = _bdot(c["vn"], ds_new, NT)
    dvb = _bdot(tm, dvn, TN)
    yield
    dglast = jnp.sum(rs(ds_new * s), axis=0, keepdims=True) * eg
    dk = dke * jnp.exp(c["glast"] - c["gcol"])
    r_ke = rs(dke * c["ke"])
    dglast = dglast + jnp.sum(r_ke, axis=0, keepdims=True)
    dgam = rs(dqe * c["qe"]) - r_ke
    dq = dqe * gc
    dpm = dp * dm
    mp = dp * c["p"]
    dq = dq + _bdot(dpm, k)
    dk = dk + _bdot(dpm, q, TN)
    dt = _bdot(dvn, c["vb"], NT) + _bdot(dw, c["kbg"], NT)
    dkbg = _bdot(tm, dw, TN)
    dgam = dgam + rs(mp) - colsum(mp)
    yield
    dkb = dkbg * gc
    dgam = dgam + rs(dkbg * c["kbg"])
    dat = _bdot(tm, dt, TN)
    yield
    da = jnp.where(strict, -_bdot(dat, tm, NT), 0.0)
    yield
    dam = da * dm
    ma = da * c["a"]
    dkb = dkb + _bdot(dam, k)
    dk = dk + _bdot(dam, c["kb"], TN)
    dgam = dgam + rs(ma) - colsum(ma)
    yield
    dk = dk + dkb * bcol
    dbeta = rs(dkb * k) + rs(dvb * v)
    dv = dvb * bcol
    row = lax.broadcasted_iota(jnp.int32, (CHUNK, 1), 0)
    dgam = dgam + jnp.where(row == CHUNK - 1, dglast, 0.0)
    dg = _dot_exact(lower, dgam, TN, split_left=False)
    return dq, dk, dv, dg, dbeta


GROUP = 4


def _chunk_decay(gb_ref, gt_ref, lmat, g):
    rows = slice(CHUNK * g, CHUNK * g + CHUNK)
    return rows, _dot_exact(lmat, gb_ref[rows, :], split_left=False), _dot_exact(gt_ref[g], lmat, NT)


def _gdn_chunk_fwd(qd, kd, vd, gb, gbt, name):
    T = qd.shape[0]
    G = GROUP
    ng = T // (CHUNK * G)

    def body(q_ref, k_ref, v_ref, gb_ref, gt_ref, u_ref, w_ref, qe_ref, ke_ref, p_ref, t_ref, eg_ref):
        lower, strict, eye = _tri_consts()
        lmat = lower.astype(F32)
        decay = [_chunk_decay(gb_ref, gt_ref, lmat, g) for g in range(G)]
        chains = [(g, h) for g in range(G) for h in range(DN_HEADS)]
        cs = []
        for g, h in chains:
            rows, gcs, grs = decay[g]
            sl = slice(128 * h, 128 * h + 128)
            c = _gdn_local(q_ref[rows, sl], k_ref[rows, sl], v_ref[rows, sl], gcs[:, h:h + 1], grs[h:h + 1, :],
                           gb_ref[rows, DN_HEADS + h:DN_HEADS + h + 1], lower, strict)
            qe_ref[rows, sl] = c["qe"].astype(BF16)
            ke_ref[rows, sl] = c["ke"].astype(BF16)
            p_ref[rows, 64 * h:64 * h + 64] = c["p"].astype(BF16)
            eg_ref[g, h:h + 1, :] = jnp.broadcast_to(jnp.exp(c["glast"]), (1, 128))
            cs.append(c)
        tms = [t.astype(BF16) for t in _tri_inv([c["a"] for c in cs], eye)]
        us = [_dot(t, c["vb"].astype(BF16)) for t, c in zip(tms, cs)]
        ws = [_dot(t, c["kbg"].astype(BF16)) for t, c in zip(tms, cs)]
        for (g, h), tm, u, w in zip(chains, tms, us, ws):
            rows, sl = decay[g][0], slice(128 * h, 128 * h + 128)
            u_ref[rows, sl] = u
            w_ref[rows, sl] = w.astype(BF16)
            t_ref[rows, 64 * h:64 * h + 64] = tm
        for g in range(G):
            eg_ref[g, DN_HEADS:, :] = jnp.zeros((8 - DN_HEADS, 128), F32)

    blk = pl.BlockSpec((CHUNK * G, 512), lambda n: (n, 0))
    half = pl.BlockSpec((CHUNK * G, 256), lambda n: (n, 0))
    return pl.pallas_call(
        body, name=name, grid=(ng,),
        in_specs=[blk, blk, blk, pl.BlockSpec((CHUNK * G, 128), lambda n: (n, 0)),
                  pl.BlockSpec((G, 8, CHUNK), lambda n: (n, 0, 0))],
        out_specs=[blk, blk, blk, blk, half, half, pl.BlockSpec((G, 8, 128), lambda n: (n, 0, 0))],
        out_shape=[jax.ShapeDtypeStruct((T, 512), F32)] + [jax.ShapeDtypeStruct((T, 512), BF16)] * 3
        + [jax.ShapeDtypeStruct((T, 256), BF16)] * 2 + [jax.ShapeDtypeStruct((T // CHUNK, 8, 128), F32)],
        compiler_params=_cparams(("parallel",)),
    )(qd, kd, vd, gb, gbt)


def _gdn_scan_fwd(u, w, qe, ke, pm, eg, proj, dn_g, name):
    T = u.shape[0]
    nc = T // CHUNK

    def body(u_ref, w_ref, qe_ref, ke_ref, p_ref, eg_ref, z_ref, ng_ref, y_ref, o_ref, vn_ref, ss_ref, s_ref):
        n = pl.program_id(0)

        @pl.when(n == 0)
        def _():
            s_ref[...] = jnp.zeros_like(s_ref)

        ss_ref[0] = s_ref[...]

        def head(h):
            sl = slice(128 * h, 128 * h + 128)
            s = s_ref[h]
            sb = s.astype(BF16)
            vn = u_ref[:, sl] - _dot(w_ref[:, sl], sb)
            qs = _dot(qe_ref[:, sl], sb)
            yield
            vb = vn.astype(BF16)
            o = qs + _dot(p_ref[:, 64 * h:64 * h + 64], vb)
            s_ref[h] = s * eg_ref[0, h:h + 1, :] + _dot(ke_ref[:, sl], vb, TN)
            yield
            vn_ref[:, sl] = vb
            o_ref[:, sl] = o
            y_ref[:, sl] = _rms(o, None)[0] * ng_ref[...] * _silu(z_ref[:, sl])

        _lockstep([head(h) for h in range(DN_HEADS)])

    blk = pl.BlockSpec((CHUNK, 512), lambda n: (n, 0))
    return pl.pallas_call(
        body, name=name, grid=(nc,),
        in_specs=[blk, blk, blk, blk, pl.BlockSpec((CHUNK, 256), lambda n: (n, 0)),
                  pl.BlockSpec((1, 8, 128), lambda n: (n, 0, 0)),
                  pl.BlockSpec((CHUNK, 512), lambda n: (n, C_ZC // 512)), pl.BlockSpec((1, 128), lambda n: (0, 0))],
        out_specs=[blk, blk, blk, pl.BlockSpec((1, DN_HEADS, 128, 128), lambda n: (n, 0, 0, 0))],
        out_shape=[jax.ShapeDtypeStruct((T, 512), F32), jax.ShapeDtypeStruct((T, 512), F32),
                   jax.ShapeDtypeStruct((T, 512), BF16), jax.ShapeDtypeStruct((nc, DN_HEADS, 128, 128), F32)],
        scratch_shapes=[pltpu.VMEM((DN_HEADS, 128, 128), F32)],
        compiler_params=_cparams(("arbitrary",)),
    )(u, w, qe, ke, pm, eg, proj, dn_g)


def _gdn_scan_bwd(dproj, w, qe, ke, pm, eg, o, proj, dyc, dn_g, name):
    T = o.shape[0]
    nc = T // CHUNK
    rev = lambda n: nc - 1 - n

    def body(dp_any, w_ref, qe_ref, ke_ref, p_ref, eg_ref, o_ref, z_ref, dy_ref, ng_ref,
             dz_ref, do_ref, dvn_ref, dsn_ref, gng_ref, ds_ref):
        n = pl.program_id(0)

        @pl.when(n == 0)
        def _():
            ds_ref[...] = jnp.zeros_like(ds_ref)
            gng_ref[...] = jnp.zeros_like(gng_ref)

        dsn_ref[0] = ds_ref[...]

        def head(h):
            sl = slice(128 * h, 128 * h + 128)
            oh, r = _rms(o_ref[:, sl], None)
            z, dy = z_ref[:, sl], dy_ref[:, sl]
            dz_ref[:, sl] = (dy * (oh * ng_ref[...]) * _dsilu(z)).astype(BF16)
            do, gg = _rms_bwd(dy * _silu(z), oh, r, ng_ref[...])
            dob = do.astype(BF16)
            ds = ds_ref[h]
            dvn = _dot(p_ref[:, 64 * h:64 * h + 64], dob, TN) + _dot(ke_ref[:, sl], ds.astype(BF16))
            qd = _dot(qe_ref[:, sl], dob, TN)
            yield
            dvb = dvn.astype(BF16)
            ds_ref[h] = qd + eg_ref[0, h:h + 1, :] * ds - _dot(w_ref[:, sl], dvb, TN)
            do_ref[:, sl] = dob
            dvn_ref[:, sl] = dvb
            return jnp.sum(gg, axis=0, keepdims=True)

        gng = _lockstep([head(h) for h in range(DN_HEADS)])
        gng_ref[...] += (gng[0] + gng[1]) + (gng[2] + gng[3])

    blk = pl.BlockSpec((CHUNK, 512), lambda n: (rev(n), 0))
    state = pl.BlockSpec((1, DN_HEADS, 128, 128), lambda n: (rev(n), 0, 0, 0))
    return pl.pallas_call(
        body, name=name, grid=(nc,),
        in_specs=[pl.BlockSpec(memory_space=pl.ANY), blk, blk, blk, pl.BlockSpec((CHUNK, 256), lambda n: (rev(n), 0)),
                  pl.BlockSpec((1, 8, 128), lambda n: (rev(n), 0, 0)), blk,
                  pl.BlockSpec((CHUNK, 512), lambda n: (rev(n), C_ZC // 512)), blk,
                  pl.BlockSpec((1, 128), lambda n: (0, 0))],
        out_specs=[pl.BlockSpec((CHUNK, 512), lambda n: (rev(n), C_ZC // 512)), blk, blk, state,
                   pl.BlockSpec((1, 128), lambda n: (0, 0))],
        out_shape=[jax.ShapeDtypeStruct(dproj.shape, BF16), jax.ShapeDtypeStruct((T, 512), BF16),
                   jax.ShapeDtypeStruct((T, 512), BF16), jax.ShapeDtypeStruct((nc, DN_HEADS, 128, 128), F32),
                   jax.ShapeDtypeStruct((1, 128), F32)],
        scratch_shapes=[pltpu.VMEM((DN_HEADS, 128, 128), F32)],
        input_output_aliases={0: 0},
        compiler_params=_cparams(("arbitrary",)),
    )(dproj, w, qe, ke, pm, eg, o, proj, dyc, dn_g)


def _gdn_chunk_grad(qd, kd, vd, gb, gbt, tmi, ssave, dsn, do, dvn, vn, name):
    T = qd.shape[0]
    G = GROUP
    ng = T // (CHUNK * G)

    def body(q_ref, k_ref, v_ref, gb_ref, gt_ref, t_ref, ss_ref, dsn_ref, do_ref, dvn_ref, vn_ref,
             dq_ref, dk_ref, dv_ref, dgb_ref):
        lower, strict, _ = _tri_consts()
        lmat = lower.astype(F32)
        ones = jnp.ones((CHUNK, 128), F32)
        lane = lax.broadcasted_iota(jnp.int32, (CHUNK, 128), 1)
        decay = [_chunk_decay(gb_ref, gt_ref, lmat, g) for g in range(G)]
        chains = [(g, h) for g in range(G) for h in range(DN_HEADS)]
        gens = []
        for g, h in chains:
            rows, gcs, grs = decay[g]
            sl = slice(128 * h, 128 * h + 128)
            c = _gdn_local(q_ref[rows, sl], k_ref[rows, sl], v_ref[rows, sl], gcs[:, h:h + 1], grs[h:h + 1, :],
                           gb_ref[rows, DN_HEADS + h:DN_HEADS + h + 1], lower, strict)
            c.update(tm=t_ref[rows, 64 * h:64 * h + 64], s=ss_ref[g, h], vn=vn_ref[rows, sl])
            gens.append(_gdn_chunk_bwd(c, do_ref[rows, sl], dvn_ref[rows, sl], dsn_ref[g, h], lower, strict, ones))
        dgb = [jnp.zeros((CHUNK, 128), F32) for _ in range(G)]
        for (g, h), (dq, dk, dv, dg, dbeta) in zip(chains, _lockstep(gens)):
            rows, sl = decay[g][0], slice(128 * h, 128 * h + 128)
            dq_ref[rows, sl], dk_ref[rows, sl], dv_ref[rows, sl] = dq, dk, dv
            dgb[g] = dgb[g] + jnp.where(lane == h, dg, 0.0) + jnp.where(lane == DN_HEADS + h, dbeta, 0.0)
        for g in range(G):
            dgb_ref[decay[g][0], :] = dgb[g]

    blk = pl.BlockSpec((CHUNK * G, 512), lambda n: (n, 0))
    half = pl.BlockSpec((CHUNK * G, 256), lambda n: (n, 0))
    nar = pl.BlockSpec((CHUNK * G, 128), lambda n: (n, 0))
    state = pl.BlockSpec((G, DN_HEADS, 128, 128), lambda n: (n, 0, 0, 0))
    return pl.pallas_call(
        body, name=name, grid=(ng,),
        in_specs=[blk, blk, blk, nar, pl.BlockSpec((G, 8, CHUNK), lambda n: (n, 0, 0)), half, state, state,
                  blk, blk, blk],
        out_specs=[blk, blk, blk, nar],
        out_shape=[jax.ShapeDtypeStruct((T, 512), F32)] * 3 + [jax.ShapeDtypeStruct((T, 128), F32)],
        compiler_params=_cparams(("parallel",)),
    )(qd, kd, vd, gb, gbt, tmi, ssave, dsn, do, dvn, vn)


def _gdn_prep_bwd1(dproj, proj, dqd, dkd, dvd, dgb, dkv_a, sconv_w, alog_v, dtb_v, name):
    T = proj.shape[0]
    tm = min(512, T)
    cur, prev, ab = _gdn_specs(T, tm)

    def body(dp_any, x_ref, xp_ref, ab_ref, dq_ref, dk_ref, dv_ref, dgb_ref, dkv_ref, w_ref, al_ref, dt_ref,
             o_ref, dpre_ref, st_ref, ext_ref):
        i = pl.program_id(0)
        pre = _gdn_conv(i, tm, x_ref, xp_ref, w_ref, ext_ref)
        y, dsl = _silu(pre), _dsilu(pre)
        for h in range(DN_HEADS):
            for base, g_ref, scale in ((0, dq_ref, 128 ** -0.5), (512, dk_ref, 1.0)):
                sl = slice(base + 128 * h, base + 128 * h + 128)
                xh = y[:, sl]
                r = lax.rsqrt(jnp.sum(xh * xh, axis=-1, keepdims=True) + EPS)
                xn = xh * r
                gy = g_ref[:, 128 * h:128 * h + 128]
                dpre_ref[:, sl] = (scale * r) * (gy - xn * jnp.sum(gy * xn, axis=-1, keepdims=True)) * dsl[:, sl]
        dpre_ref[:, 1024:] = dv_ref[...] * dsl[:, 1024:]
        abv, dgb = ab_ref[...], dgb_ref[...]
        lane = lax.broadcasted_iota(jnp.int32, (tm, 128), 1)
        na = -jnp.exp(al_ref[...])
        xs = abv + dt_ref[...]
        da = dgb * na * _sig(xs)
        b = _sig(abv)
        o_ref[:, :256] = dkv_ref[...].astype(BF16)
        o_ref[:, 256:] = jnp.where(lane < DN_HEADS, da,
                                   jnp.where(lane < 2 * DN_HEADS, dgb * b * (1.0 - b), 0.0)).astype(BF16)
        head = lane < DN_HEADS
        upd = jnp.concatenate([jnp.sum(jnp.where(head, dgb * na * _softplus(xs), 0.0), axis=0, keepdims=True),
                               jnp.sum(jnp.where(head, da, 0.0), axis=0, keepdims=True), jnp.zeros((6, 128), F32)],
                              axis=0)

        @pl.when(i == 0)
        def _():
            st_ref[...] = upd

        @pl.when(i > 0)
        def _():
            st_ref[...] += upd

    full = lambda s: pl.BlockSpec(s, lambda i: (0, 0))
    blk = pl.BlockSpec((tm, 512), lambda i: (i, 0))
    return pl.pallas_call(
        body, name=name, grid=(T // tm,),
        in_specs=[pl.BlockSpec(memory_space=pl.ANY), cur, prev, ab, blk, blk, blk,
                  pl.BlockSpec((tm, 128), lambda i: (i, 0)), pl.BlockSpec((tm, 256), lambda i: (i, 0)),
                  full((DN_K, QKV_C)), full((1, 128)), full((1, 128))],
        out_specs=[pl.BlockSpec((tm, 384), lambda i: (i, C_KA // 384)),
                   pl.BlockSpec((tm, QKV_C), lambda i: (i, 0)), full((8, 128))],
        out_shape=[jax.ShapeDtypeStruct(dproj.shape, BF16), jax.ShapeDtypeStruct((T, QKV_C), F32),
                   jax.ShapeDtypeStruct((8, 128), F32)],
        scratch_shapes=[pltpu.VMEM((tm + HALO_C, QKV_C), F32)],
        input_output_aliases={0: 0},
        compiler_params=_cparams(("arbitrary",)),
    )(dproj, proj, proj, proj, dqd, dkd, dvd, dgb, dkv_a, sconv_w, alog_v, dtb_v)


def _gdn_prep_bwd2(dproj, proj, dpre, sconv_w, name):
    T = proj.shape[0]
    tm = min(512, T)
    nt = T // tm
    r = tm // HALO_C
    cur, prev, _ = _gdn_specs(T, tm)

    def body(dp_any, x_ref, xp_ref, d_ref, dn_ref, w_ref, dx_ref, gw_ref, extx_ref, extd_ref):
        i = pl.program_id(0)
        extx_ref[:HALO_C] = jnp.where(i > 0, xp_ref[...], 0.0)
        extx_ref[HALO_C:] = x_ref[...]
        d = d_ref[...]
        extd_ref[:tm] = d
        extd_ref[tm:] = jnp.where(i < nt - 1, dn_ref[...], 0.0)
        dx = jnp.zeros((tm, QKV_C), F32)
        rows = []
        for k in range(DN_K):
            dx = dx + w_ref[k:k + 1, :] * extd_ref[pl.ds(DN_K - 1 - k, tm), :]
            rows.append(jnp.sum(d * extx_ref[pl.ds(HALO_C - DN_K + 1 + k, tm), :], axis=0, keepdims=True))
        rows.append(jnp.zeros((8 - DN_K, QKV_C), F32))
        gw = jnp.concatenate(rows, axis=0)
        dx_ref[...] = dx.astype(BF16)

        @pl.when(i == 0)
        def _():
            gw_ref[...] = gw

        @pl.when(i > 0)
        def _():
            gw_ref[...] += gw

    full = lambda s: pl.BlockSpec(s, lambda i: (0, 0))
    return pl.pallas_call(
        body, name=name, grid=(nt,),
        in_specs=[pl.BlockSpec(memory_space=pl.ANY), cur, prev, pl.BlockSpec((tm, QKV_C), lambda i: (i, 0)),
                  pl.BlockSpec((HALO_C, QKV_C), lambda i: (jnp.minimum((i + 1) * r, T // HALO_C - 1), 0)),
                  full((DN_K, QKV_C))],
        out_specs=[cur, full((8, QKV_C))],
        out_shape=[jax.ShapeDtypeStruct(dproj.shape, BF16), jax.ShapeDtypeStruct((8, QKV_C), F32)],
        scratch_shapes=[pltpu.VMEM((tm + HALO_C, QKV_C), F32), pltpu.VMEM((tm + HALO_C, QKV_C), F32)],
        input_output_aliases={0: 0},
        compiler_params=_cparams(("arbitrary",)),
    )(dproj, proj, proj, dpre, dpre, sconv_w)


def _merge_fwd(x, proj, ya, yb, yc, wa, wb, wc, wo, gate, name):
    T = x.shape[0]
    tm = min(256, T)

    def body(x_ref, mg_ref, ya_ref, yb_ref, yc_ref, wa_ref, wb_ref, wc_ref, wo_ref, gate_ref, o_ref):
        merged = (_sig(mg_ref[:, :D]) * _bdot(ya_ref[...], wa_ref[...])
                  + _sig(mg_ref[:, D:2 * D]) * _bdot(yb_ref[...], wb_ref[...])
                  + _sig(mg_ref[:, 2 * D:]) * _bdot(yc_ref[...], wc_ref[...]))
        o_ref[...] = x_ref[...] + gate_ref[...] * _bdot(merged, wo_ref[...])

    full = lambda s: pl.BlockSpec(s, lambda i: (0, 0))
    yb_ = pl.BlockSpec((tm, 512), lambda i: (i, 0))
    return pl.pallas_call(
        body, name=name, grid=(T // tm,),
        in_specs=[pl.BlockSpec((tm, D), lambda i: (i, 0)), pl.BlockSpec((tm, 3 * D), lambda i: (i, 0)), yb_, yb_, yb_,
                  full((512, D)), full((512, D)), full((512, D)), full((D, D)), full((1, D))],
        out_specs=pl.BlockSpec((tm, D), lambda i: (i, 0)),
        out_shape=jax.ShapeDtypeStruct((T, D), F32),
        compiler_params=_cparams(("parallel",)),
    )(x, proj, ya, yb, yc, wa, wb, wc, wo, _row(gate))


def _merge_bwd(dout, proj, ya, yb, yc, wa, wb, wc, wo, gate, name):
    T = dout.shape[0]
    tm = min(256, T)
    nt = T // tm

    def body(do_ref, mg_ref, ya_ref, yb_ref, yc_ref, wa_ref, wb_ref, wc_ref, wo_ref, gate_ref,
             dmg_ref, dya_ref, dyb_ref, dyc_ref, gwa_hbm, gwb_hbm, gwc_hbm, gwo_hbm, gg_ref,
             gwa_ref, gwb_ref, gwc_ref, gwo_ref):
        i = pl.program_id(0)

        @pl.when(i == 0)
        def _():
            for r in (gwa_ref, gwb_ref, gwc_ref, gwo_ref, gg_ref):
                r[...] = jnp.zeros_like(r)

        ys = (ya_ref[...], yb_ref[...], yc_ref[...])
        ws = (wa_ref, wb_ref, wc_ref)
        gs = tuple(_sig(mg_ref[:, j * D:(j + 1) * D]) for j in range(3))
        ps = tuple(_bdot(ys[j], ws[j][...]) for j in range(3))
        merged = gs[0] * ps[0] + gs[1] * ps[1] + gs[2] * ps[2]
        mo = _bdot(merged, wo_ref[...])
        do = do_ref[...]
        gg_ref[...] += jnp.sum(do * mo, axis=0, keepdims=True)
        dmo = do * gate_ref[...]
        dmerged = _bdot(dmo, wo_ref[...], NT)
        gwo_ref[...] += _bdot(merged, dmo, TN)
        for j, (dy_ref, gw_ref) in enumerate(((dya_ref, gwa_ref), (dyb_ref, gwb_ref), (dyc_ref, gwc_ref))):
            dp = dmerged * gs[j]
            dmg_ref[:, j * D:(j + 1) * D] = (dmerged * ps[j] * gs[j] * (1.0 - gs[j])).astype(BF16)
            dy_ref[...] = _bdot(dp, ws[j][...], NT)
            gw_ref[...] += _bdot(ys[j], dp, TN)

        @pl.when(i == nt - 1)
        def _():
            for src, dst in ((gwa_ref, gwa_hbm), (gwb_ref, gwb_hbm), (gwc_ref, gwc_hbm), (gwo_ref, gwo_hbm)):
                pltpu.sync_copy(src, dst)

    full = lambda s: pl.BlockSpec(s, lambda i: (0, 0))
    yb_ = pl.BlockSpec((tm, 512), lambda i: (i, 0))
    anyspec = pl.BlockSpec(memory_space=pl.ANY)
    return pl.pallas_call(
        body, name=name, grid=(nt,),
        in_specs=[pl.BlockSpec((tm, D), lambda i: (i, 0)), pl.BlockSpec((tm, 3 * D), lambda i: (i, 0)), yb_, yb_, yb_,
                  full((512, D)), full((512, D)), full((512, D)), full((D, D)), full((1, D))],
        out_specs=[pl.BlockSpec((tm, 3 * D), lambda i: (i, 0)), yb_, yb_, yb_, anyspec, anyspec, anyspec, anyspec,
                   full((1, D))],
        out_shape=[jax.ShapeDtypeStruct((T, NP), BF16)] + [jax.ShapeDtypeStruct((T, 512), F32)] * 3
        + [jax.ShapeDtypeStruct((512, D), F32)] * 3 + [jax.ShapeDtypeStruct((D, D), F32), jax.ShapeDtypeStruct((1, D), F32)],
        scratch_shapes=[pltpu.VMEM((512, D), F32)] * 3 + [pltpu.VMEM((D, D), F32)],
        compiler_params=_cparams(("arbitrary",)),
    )(dout, proj, ya, yb, yc, wa, wb, wc, wo, _row(gate))


def _loss_head(y, tgt, name):
    T = y.shape[0]
    tm = min(512, T)

    def body(y_ref, t_ref, dy_ref, l_ref):
        i = pl.program_id(0)
        diff = y_ref[...] - t_ref[...]
        dy_ref[...] = diff * (1.0 / D)
        part = jnp.sum(diff * diff, axis=0, keepdims=True)

        @pl.when(i == 0)
        def _():
            l_ref[...] = part

        @pl.when(i > 0)
        def _():
            l_ref[...] += part

    blk = pl.BlockSpec((tm, D), lambda i: (i, 0))
    return pl.pallas_call(
        body, name=name, grid=(T // tm,), in_specs=[blk, blk],
        out_specs=[blk, pl.BlockSpec((1, D), lambda i: (0, 0))],
        out_shape=[jax.ShapeDtypeStruct((T, D), F32), jax.ShapeDtypeStruct((1, D), F32)],
        compiler_params=_cparams(("arbitrary",)),
    )(y, tgt)


def _ada_fwd(c_all, w_ada, b_my, name):
    def body(c_ref, w_ref, b_ref, o_ref):
        sc = _silu(c_ref[...])
        for l in range(DEPTH):
            o_ref[l] = _bdot(sc, w_ref[l]) + b_ref[l:l + 1, :]

    return pl.pallas_call(body, name=name, out_shape=jax.ShapeDtypeStruct((DEPTH, N_DEV, w_ada.shape[2]), F32),
                          compiler_params=_cparams())(c_all, w_ada, b_my)


def _ada_bwd(c_all, dmod_my, name):
    def body(c_ref, d_ref, o_ref):
        sc = _silu(c_ref[...])
        for l in range(DEPTH):
            o_ref[l] = _bdot(sc, d_ref[l], TN)

    return pl.pallas_call(body, name=name, out_shape=jax.ShapeDtypeStruct((DEPTH, D, dmod_my.shape[2]), F32),
                          compiler_params=_cparams())(c_all, dmod_my)


def _adam_math(w, g, m, v):
    m = ADAM_B1 * m + (1.0 - ADAM_B1) * g
    v = ADAM_B2 * v + (1.0 - ADAM_B2) * (g * g)
    m_hat = m / (1.0 - ADAM_B1 ** ADAM_STEP)
    v_hat = v / (1.0 - ADAM_B2 ** ADAM_STEP)
    return -ADAM_LR * (m_hat / (jnp.sqrt(v_hat) + ADAM_EPS) + ADAM_WD * w), m, v


def _row_tile(rows, cap):
    best = rows
    for t in range(8, min(rows, cap) + 1, 8):
        if rows % t == 0:
            best = t
    return best if best <= cap else rows


def _adamw(w, g, m, v, name):
    R, C = w.shape
    tr = _row_tile(R, 256)

    def body(w_ref, g_ref, m_ref, v_ref, d_ref, mo_ref, vo_ref):
        d_ref[...], mo_ref[...], vo_ref[...] = _adam_math(w_ref[...], g_ref[...], m_ref[...], v_ref[...])

    blk = pl.BlockSpec((tr, C), lambda i: (i, 0))
    return pl.pallas_call(body, name=name, grid=(R // tr,), in_specs=[blk] * 4, out_specs=[blk] * 3,
                          out_shape=[jax.ShapeDtypeStruct((R, C), F32)] * 3,
                          compiler_params=_cparams(("parallel",)))(w, g, m, v)


def _sum_adamw_many(parts, ws, ms, vs, name):
    n = len(ws)

    def body(*refs):
        ins, outs = refs[:4 * n], refs[4 * n:]
        for i in range(n):
            g = ins[i][0]
            for j in range(1, N_DEV):
                g = g + ins[i][j]
            d, m, v = _adam_math(ins[n + i][...], g, ins[2 * n + i][...], ins[3 * n + i][...])
            outs[i][...], outs[n + i][...], outs[2 * n + i][...], outs[3 * n + i][...] = g, d, m, v

    shapes = [jax.ShapeDtypeStruct(w.shape, F32) for w in ws]
    out = pl.pallas_call(body, name=name, out_shape=shapes * 4, compiler_params=_cparams())(*parts, *ws, *ms, *vs)
    return out[:n], out[n:2 * n], out[2 * n:3 * n], out[3 * n:]


def _sum_adamw(parts, w, m, v, name):
    P, R, C = parts.shape
    tr = _row_tile(R, 128)

    def body(p_ref, w_ref, m_ref, v_ref, g_ref, d_ref, mo_ref, vo_ref):
        g = p_ref[0].astype(F32)
        for j in range(1, P):
            g = g + p_ref[j].astype(F32)
        g_ref[...] = g
        d_ref[...], mo_ref[...], vo_ref[...] = _adam_math(w_ref[...], g, m_ref[...], v_ref[...])

    blk = pl.BlockSpec((tr, C), lambda i: (i, 0))
    return pl.pallas_call(body, name=name, grid=(R // tr,),
                          in_specs=[pl.BlockSpec((P, tr, C), lambda i: (0, i, 0)), blk, blk, blk],
                          out_specs=[blk] * 4, out_shape=[jax.ShapeDtypeStruct((R, C), F32)] * 4,
                          compiler_params=_cparams(("parallel",)))(parts, w, m, v)


def _pair_sum(core, buf, recv, name):
    _, _, R, C = buf.shape
    tr = _row_tile(R, 128)

    def body(c_ref, a_ref, b_ref, o_ref):
        o_ref[...] = (a_ref[:, 0].astype(F32) + b_ref[...].astype(F32)).astype(BF16)

    return pl.pallas_call(
        body, name=name,
        grid_spec=pltpu.PrefetchScalarGridSpec(
            num_scalar_prefetch=1, grid=(R // tr,),
            in_specs=[pl.BlockSpec((4, 1, tr, C), lambda i, c: (0, c[0], i, 0)),
                      pl.BlockSpec((4, tr, C), lambda i, c: (0, i, 0))],
            out_specs=pl.BlockSpec((4, tr, C), lambda i, c: (0, i, 0))),
        out_shape=jax.ShapeDtypeStruct((4, R, C), BF16),
        compiler_params=_cparams(("parallel",)))(core, buf, recv)


SHARD_IN = D_IN // N_DEV


def _w_in_pieces():
    out, p = [], 0
    for a, b in _PAD_FROM:
        for j in range(N_DEV):
            lo, hi = max(a, SHARD_IN * j), min(b, SHARD_IN * (j + 1))
            if lo < hi:
                out.append((j, lo - SHARD_IN * j, hi - SHARD_IN * j, p + lo - a))
        p += b - a
    return out


def _assemble_w_in(gw, layer, name):
    tr = 256
    nt = D // tr

    def body(x_ref, o_ref):
        for j, s0, s1, d0 in _w_in_pieces():
            o_ref[:, d0:d0 + s1 - s0] = x_ref[j, :, s0:s1]
        o_ref[:, D_IN:] = jnp.zeros((tr, NP - D_IN), gw.dtype)

    return pl.pallas_call(
        body, name=name, grid=(nt,),
        in_specs=[pl.BlockSpec((N_DEV, tr, SHARD_IN), lambda i: (0, layer * nt + i, 0))],
        out_specs=pl.BlockSpec((tr, NP), lambda i: (i, 0)),
        out_shape=jax.ShapeDtypeStruct((D, NP), gw.dtype),
        compiler_params=_cparams(("parallel",)))(gw)


def _split_w_in_grad(g0, g1, name):
    tr = 256
    nt = D // tr

    def body(g0_ref, g1_ref, o_ref):
        l = pl.program_id(0)

        def emit(g_ref):
            for j, s0, s1, d0 in _w_in_pieces():
                o_ref[j, :, s0:s1] = g_ref[:, d0:d0 + s1 - s0].astype(BF16)

        @pl.when(l == 0)
        def _():
            emit(g0_ref)

        @pl.when(l == 1)
        def _():
            emit(g1_ref)

    return pl.pallas_call(
        body, name=name, grid=(DEPTH, nt),
        in_specs=[pl.BlockSpec((tr, NP), lambda l, i: (i * (1 - l) + (nt - 1) * l, 0)),
                  pl.BlockSpec((tr, NP), lambda l, i: (i * l, 0))],
        out_specs=pl.BlockSpec((N_DEV, tr, SHARD_IN), lambda l, i: (0, l * nt + i, 0)),
        out_shape=jax.ShapeDtypeStruct((N_DEV, DEPTH * D, SHARD_IN), BF16),
        compiler_params=_cparams(("arbitrary", "arbitrary")))(g0, g1)


def _mesh_pos():
    return lax.axis_index("x"), lax.axis_index("y"), lax.axis_index("c")


def _all_gather(blocks, name):
    n = len(blocks)

    def body(*refs):
        ins, outs = refs[:n], refs[n:2 * n]
        send_sems, recv_sems, local_sems = refs[2 * n:]
        x, y, c = _mesh_pos()
        me, sibling = (x, y, c), (x, y, 1 - c)
        chips = [(1 - x, y), (x, 1 - y), (1 - x, 1 - y)]
        idx = lambda p: 4 * p[0] + 2 * p[1] + p[2]

        def copy(a, k, block, to, src=None):
            dst = outs[a].at[idx(block)]
            return pltpu.make_async_remote_copy(
                src_ref=dst if src is None else src, dst_ref=dst, send_sem=send_sems.at[a, k],
                recv_sem=recv_sems.at[a, k], device_id=to, device_id_type=pl.DeviceIdType.MESH)

        mine = [pltpu.make_async_copy(ins[a], outs[a].at[idx(me)], local_sems.at[a]) for a in range(n)]
        for cp in mine:
            cp.start()
        first = []
        for a in range(n):
            first.append(copy(a, 0, me, sibling, src=ins[a]))
            first += [copy(a, 1 + j, me, (*chip, c), src=ins[a]) for j, chip in enumerate(chips)]
        for cp in first:
            cp.start()
        passed = []
        for j, chip in enumerate(chips):
            for a in range(n):
                copy(a, 1 + j, (*chip, c), me).wait_recv()
                cp = copy(a, 4 + j, (*chip, c), sibling)
                cp.start()
                passed.append(cp)
        for a in range(n):
            copy(a, 0, sibling, me).wait_recv()
            for j, chip in enumerate(chips):
                copy(a, 4 + j, (*chip, 1 - c), me).wait_recv()
        for cp in first + passed:
            cp.wait_send()
        for cp in mine:
            cp.wait()

    anyspec = pl.BlockSpec(memory_space=pl.ANY)
    return pl.pallas_call(
        body, name=name, in_specs=[anyspec] * n, out_specs=[anyspec] * n,
        out_shape=[jax.ShapeDtypeStruct((N_DEV,) + b.shape, b.dtype) for b in blocks],
        scratch_shapes=[pltpu.SemaphoreType.DMA((n, 7)), pltpu.SemaphoreType.DMA((n, 7)),
                        pltpu.SemaphoreType.DMA((n,))],
    )(*blocks)


def _exchange_core(bufs, name):
    n = len(bufs)

    def body(*refs):
        ins, outs = refs[:n], refs[n:2 * n]
        send_sems, recv_sems = refs[2 * n:]
        x, y, c = _mesh_pos()
        copies = []
        for a in range(n):
            for q in range(4):
                cp = pltpu.make_async_remote_copy(
                    src_ref=ins[a].at[q, 1 - c], dst_ref=outs[a].at[q], send_sem=send_sems.at[a, q],
                    recv_sem=recv_sems.at[a, q], device_id=(x, y, 1 - c), device_id_type=pl.DeviceIdType.MESH)
                cp.start()
                copies.append(cp)
        for cp in copies:
            cp.wait()

    anyspec = pl.BlockSpec(memory_space=pl.ANY)
    return pl.pallas_call(
        body, name=name, in_specs=[anyspec] * n, out_specs=[anyspec] * n,
        out_shape=[jax.ShapeDtypeStruct((4,) + b.shape[2:], b.dtype) for b in bufs],
        scratch_shapes=[pltpu.SemaphoreType.DMA((n, 4)), pltpu.SemaphoreType.DMA((n, 4))],
    )(*bufs)


def _exchange_chips(bufs, name):
    n = len(bufs)

    def body(*refs):
        ins, outs = refs[:n], refs[n:2 * n]
        send_sems, recv_sems, local_sems = refs[2 * n:]
        x, y, c = _mesh_pos()
        chip = 2 * x + y
        local = [pltpu.make_async_copy(ins[a].at[chip], outs[a].at[chip], local_sems.at[a]) for a in range(n)]
        for cp in local:
            cp.start()
        copies = []
        for k in range(1, 4):
            px = 1 - x if k & 2 else x
            py = 1 - y if k & 1 else y
            for a in range(n):
                cp = pltpu.make_async_remote_copy(
                    src_ref=ins[a].at[2 * px + py], dst_ref=outs[a].at[chip], send_sem=send_sems.at[a, k - 1],
                    recv_sem=recv_sems.at[a, k - 1], device_id=(px, py, c), device_id_type=pl.DeviceIdType.MESH)
                cp.start()
                copies.append(cp)
        for cp in copies:
            cp.wait()
        for cp in local:
            cp.wait()

    anyspec = pl.BlockSpec(memory_space=pl.ANY)
    return pl.pallas_call(
        body, name=name, in_specs=[anyspec] * n, out_specs=[anyspec] * n,
        out_shape=[jax.ShapeDtypeStruct(b.shape, b.dtype) for b in bufs],
        scratch_shapes=[pltpu.SemaphoreType.DMA((n, 3)), pltpu.SemaphoreType.DMA((n, 3)),
                        pltpu.SemaphoreType.DMA((n,))],
    )(*bufs)


_SMALL = ("b_ada", "norm_g", "q_norm_g", "k_norm_g", "sinks", "dw_b", "ln_g", "ln_b", "pw2_b", "a_log", "dt_bias",
          "dn_norm_g", "dw_w", "sconv_w")


def _lane4(v):
    return jnp.pad(v, (0, 124)).reshape(1, 128)


def kernel(x, c, w_ada, b_ada, norm_g, w_in, q_norm_g, k_norm_g, sinks, dw_w, dw_b, ln_g, ln_b, pw2_w, pw2_b, sconv_w, a_log, dt_bias, dn_norm_g, w_proj_a, w_proj_b, w_proj_c, w_out, loss_target, m_w_ada, m_b_ada, m_norm_g, m_w_in, m_q_norm_g, m_k_norm_g, m_sinks, m_dw_w, m_dw_b, m_ln_g, m_ln_b, m_pw2_w, m_pw2_b, m_sconv_w, m_a_log, m_dt_bias, m_dn_norm_g, m_w_proj_a, m_w_proj_b, m_w_proj_c, m_w_out, v_w_ada, v_b_ada, v_norm_g, v_w_in, v_q_norm_g, v_k_norm_g, v_sinks, v_dw_w, v_dw_b, v_ln_g, v_ln_b, v_pw2_w, v_pw2_b, v_sconv_w, v_a_log, v_dt_bias, v_dn_norm_g, v_w_proj_a, v_w_proj_b, v_w_proj_c, v_w_out):
    T = x.shape[1]
    nc = T // CHUNK
    xi, yi, ci = _mesh_pos()
    me = 4 * xi + 2 * yi + ci
    big_w = (w_in, pw2_w, w_proj_a, w_proj_b, w_proj_c, w_out)
    big_m = (m_w_in, m_pw2_w, m_w_proj_a, m_w_proj_b, m_w_proj_c, m_w_out)
    big_v = (v_w_in, v_pw2_w, v_w_proj_a, v_w_proj_b, v_w_proj_c, v_w_out)

    ada_cols = w_ada.shape[2]
    dw_cols, sc_cols = dw_w.shape[2], sconv_w.shape[2]
    flat2 = lambda a: a.reshape(-1, a.shape[-1])
    gw_in, gpw2, gpa, gpb, gpc, gwo, c_all, gdw, gsc = _all_gather(
        [flat2(a.astype(BF16)) for a in big_w] + [c, dw_w, sconv_w], "gather_weights")
    c_all = c_all.reshape(N_DEV, D)
    wp = [_assemble_w_in(gw_in, l, f"assemble_w_in{l}") for l in range(DEPTH)]
    pw2_f = gpw2.reshape(N_DEV, DEPTH, -1, 512).transpose(1, 0, 2, 3).reshape(DEPTH, 512, 512)
    wa_f, wb_f, wc_f = (g.reshape(N_DEV, DEPTH, 512, -1).transpose(1, 2, 0, 3).reshape(DEPTH, 512, D)
                        for g in (gpa, gpb, gpc))
    wo_f = gwo.reshape(N_DEV, DEPTH, -1, D).transpose(1, 0, 2, 3).reshape(DEPTH, D, D)
    dw_f = gdw.transpose(1, 2, 0, 3).reshape(DEPTH, CONV_K, 512)
    sc_f = gsc.transpose(1, 2, 0, 3).reshape(DEPTH, DN_K, QKV_C)

    b_my = lax.dynamic_slice(b_ada, (0, me * ada_cols), (DEPTH, ada_cols))
    mod_part = _ada_fwd(c_all, w_ada, b_my, "ada_fwd")
    (gmod,) = _all_gather([mod_part.reshape(-1, 128)], "gather_mod")
    mod_all = gmod.reshape(N_DEV, DEPTH, N_DEV, ada_cols).transpose(1, 2, 0, 3).reshape(DEPTH, N_DEV, 3 * D)
    mod = lax.dynamic_index_in_dim(mod_all, me, axis=1, keepdims=False)
    shift, scale, gate = mod[:, :D], mod[:, D:2 * D], mod[:, 2 * D:]

    xs, saved = [x[0]], []
    for l in range(DEPTH):
        xl = xs[-1]
        h = _norm_fwd(xl, norm_g[l], scale[l], shift[l], f"norm_fwd{l}")
        proj = _mm(h, wp[l], tm=min(1024, T), tn=1152, tk=D, name=f"in_proj{l}")
        ya = _attn_fwd(proj, q_norm_g[l], k_norm_g[l], sinks[l], f"attn_fwd{l}")
        yb = _conf_fwd(proj, dw_f[l], dw_b[l], ln_g[l], ln_b[l], pw2_f[l], pw2_b[l], f"conf_fwd{l}")
        alv, dtv, dng = _lane4(a_log[l]), _lane4(dt_bias[l]), _row(dn_norm_g[l])
        qd, kd, vd, gb = _gdn_prep_fwd(proj, sc_f[l], alv, dtv, f"gdn_prep_fwd{l}")
        gbt = gb[:, :8].reshape(nc, CHUNK, 8).transpose(0, 2, 1)
        u, w, qe, ke, pm, tmi, eg = _gdn_chunk_fwd(qd, kd, vd, gb, gbt, f"gdn_chunk_fwd{l}")
        yc, o, vn, ss = _gdn_scan_fwd(u, w, qe, ke, pm, eg, proj, dng, f"gdn_scan_fwd{l}")
        xs.append(_merge_fwd(xl, proj, ya, yb, yc, wa_f[l], wb_f[l], wc_f[l], wo_f[l], gate[l], f"merge_fwd{l}"))
        saved.append((h, proj, ya, yb, yc, qd, kd, vd, gb, gbt, ss, alv, dtv, dng, w, qe, ke, pm, tmi, eg, o, vn))

    dout, lsum = _loss_head(xs[-1], loss_target[0], "loss_head")
    loss = lax.psum(0.5 * jnp.sum(lsum) / D, ("x", "y", "c"))

    small = {name: [None] * DEPTH for name in _SMALL}
    big_g = [[None] * DEPTH for _ in big_w]
    for l in reversed(range(DEPTH)):
        h, proj, ya, yb, yc, qd, kd, vd, gb, gbt, ss, alv, dtv, dng, w, qe, ke, pm, tmi, eg, o, vn = saved[l]
        dproj, dya, dyb, dyc, g_wa, g_wb, g_wc, g_wo, g_gate = _merge_bwd(
            dout, proj, ya, yb, yc, wa_f[l], wb_f[l], wc_f[l], wo_f[l], gate[l], f"merge_bwd{l}")
        dproj, dkv_a, g_q, g_k, g_s = _attn_bwd(dproj, proj, dya, q_norm_g[l], k_norm_g[l], sinks[l], f"attn_bwd{l}")
        dproj, du1, g_pw2, st_b = _conf_bwd1(dproj, proj, dyb, dw_f[l], dw_b[l], ln_g[l], ln_b[l], pw2_f[l], pw2_b[l],
                                             f"conf_bwd_a{l}")
        dproj, g_dw = _conf_bwd2(dproj, proj, du1, dw_f[l], f"conf_bwd_b{l}")
        dproj, do, dvn, dsn, g_dn = _gdn_scan_bwd(dproj, w, qe, ke, pm, eg, o, proj, dyc, dng, f"gdn_scan_bwd{l}")
        dqd, dkd, dvd, dgb = _gdn_chunk_grad(qd, kd, vd, gb, gbt, tmi, ss, dsn, do, dvn, vn, f"gdn_chunk_bwd{l}")
        dproj, dpre, st_c = _gdn_prep_bwd1(dproj, proj, dqd, dkd, dvd, dgb, dkv_a, sc_f[l], alv, dtv,
                                           f"gdn_prep_bwd_a{l}")
        dproj, g_sc = _gdn_prep_bwd2(dproj, proj, dpre, sc_f[l], f"gdn_prep_bwd_b{l}")
        dh = _mm(dproj, wp[l], tb=True, tm=min(1024, T), tn=D, tk=1152, name=f"d_h{l}")
        g_wp = _mm(h, dproj, ta=True, tm=D, tn=1152, tk=min(1024, T), name=f"d_w_in{l}")
        dout, st_n = _norm_bwd(dh, xs[l], dout, norm_g[l], scale[l], f"norm_bwd{l}")
        for i, g in enumerate((g_wp, g_pw2, g_wa, g_wb, g_wc, g_wo)):
            big_g[i][l] = g
        for name, g in (("b_ada", jnp.concatenate([st_n[0], st_n[1], g_gate[0]])), ("norm_g", st_n[2]),
                        ("q_norm_g", g_q.reshape(ATT_HEADS, ATT_HD).sum(0)), ("k_norm_g", g_k.reshape(2, ATT_HD).sum(0)),
                        ("sinks", g_s[0]), ("dw_b", st_b[3]),
                        ("ln_g", st_b[1]), ("ln_b", st_b[2]), ("pw2_b", st_b[0]), ("a_log", st_c[0, :4]),
                        ("dt_bias", st_c[1, :4]), ("dn_norm_g", g_dn[0]), ("dw_w", g_dw[:CONV_K]),
                        ("sconv_w", g_sc[:DN_K])):
            small[name][l] = g
    grad_x = dout[None]

    names = list(_SMALL)
    gparts = dict(zip(names, _all_gather([jnp.stack(small[n]) for n in names], "gather_small_grads")))
    dmod_my = lax.dynamic_slice(gparts["b_ada"], (0, 0, me * ada_cols), (N_DEV, DEPTH, ada_cols)).transpose(1, 0, 2)
    g_w_ada = _ada_bwd(c_all, dmod_my, "ada_bwd")
    gparts["dw_w"] = lax.dynamic_slice(gparts["dw_w"], (0, 0, 0, me * dw_cols), (N_DEV, DEPTH, CONV_K, dw_cols))
    gparts["sconv_w"] = lax.dynamic_slice(gparts["sconv_w"], (0, 0, 0, me * sc_cols), (N_DEV, DEPTH, DN_K, sc_cols))
    env = dict(b_ada=(b_ada, m_b_ada, v_b_ada), norm_g=(norm_g, m_norm_g, v_norm_g),
               q_norm_g=(q_norm_g, m_q_norm_g, v_q_norm_g), k_norm_g=(k_norm_g, m_k_norm_g, v_k_norm_g),
               sinks=(sinks, m_sinks, v_sinks), dw_b=(dw_b, m_dw_b, v_dw_b), ln_g=(ln_g, m_ln_g, v_ln_g),
               ln_b=(ln_b, m_ln_b, v_ln_b), pw2_b=(pw2_b, m_pw2_b, v_pw2_b), a_log=(a_log, m_a_log, v_a_log),
               dt_bias=(dt_bias, m_dt_bias, v_dt_bias), dn_norm_g=(dn_norm_g, m_dn_norm_g, v_dn_norm_g),
               dw_w=(dw_w, m_dw_w, v_dw_w), sconv_w=(sconv_w, m_sconv_w, v_sconv_w))
    upd = _sum_adamw_many([gparts[n] for n in names], [env[n][0] for n in names], [env[n][1] for n in names],
                          [env[n][2] for n in names], "sum_adamw_small")
    g_small, d_small, m_small, v_small = (dict(zip(names, u)) for u in upd)

    d_ada, nm_ada, nv_ada = (u.reshape(w_ada.shape) for u in
                             _adamw(flat2(w_ada), flat2(g_w_ada), flat2(m_w_ada), flat2(v_w_ada), "adamw_w_ada"))

    g_pw, g_a, g_b, g_c, g_o = (jnp.stack(g) for g in big_g[1:])
    by_dest = [_split_w_in_grad(big_g[0][0], big_g[0][1], "split_w_in_grad"),
               g_pw.reshape(DEPTH, N_DEV, -1, 512).transpose(1, 0, 2, 3).astype(BF16)]
    by_dest += [g.reshape(DEPTH, 512, N_DEV, -1).transpose(2, 0, 1, 3).astype(BF16) for g in (g_a, g_b, g_c)]
    by_dest.append(g_o.reshape(DEPTH, N_DEV, -1, D).transpose(1, 0, 2, 3).astype(BF16))
    by_dest = [b.reshape(4, 2, -1, b.shape[-1]) for b in by_dest]
    from_sibling = _exchange_core(by_dest, "exchange_grads_core")
    core = jnp.reshape(ci, (1,)).astype(jnp.int32)
    chip_sums = [_pair_sum(core, b, r, f"pair_sum{i}") for i, (b, r) in enumerate(zip(by_dest, from_sibling))]
    parts = _exchange_chips(chip_sums, "exchange_grads_chips")
    res = [_sum_adamw(p, flat2(w), flat2(m), flat2(v), f"sum_adamw{i}")
           for i, (p, w, m, v) in enumerate(zip(parts, big_w, big_m, big_v))]
    g_big, d_big, m_big, v_big = ([r[k].reshape(w.shape) for r, w in zip(res, big_w)] for k in range(4))

    order = ("w_ada", "b_ada", "norm_g", "w_in", "q_norm_g", "k_norm_g", "sinks", "dw_w", "dw_b", "ln_g", "ln_b",
             "pw2_w", "pw2_b", "sconv_w", "a_log", "dt_bias", "dn_norm_g", "w_proj_a", "w_proj_b", "w_proj_c", "w_out")
    big_names = ("w_in", "pw2_w", "w_proj_a", "w_proj_b", "w_proj_c", "w_out")

    def pick(kind):
        src_small = (g_small, d_small, m_small, v_small)[kind]
        src_big = (g_big, d_big, m_big, v_big)[kind]
        src_ada = (g_w_ada, d_ada, nm_ada, nv_ada)[kind]
        return [src_ada if n == "w_ada" else src_big[big_names.index(n)] if n in big_names else src_small[n]
                for n in order]

    return (loss, grad_x, *pick(0), *pick(1), *pick(2), *pick(3))
```

```python
import functools
import math

import jax
import jax.numpy as jnp
import numpy as np
from jax import lax
from jax.experimental import pallas as pl
from jax.experimental.pallas import tpu as pltpu
from jax.experimental.pallas import tpu_sc as plsc

F32 = jnp.float32
BF16 = jnp.bfloat16
HI = lax.Precision.HIGHEST

N_DEV = 8
D = 1024
DEPTH = 2
EPS = 1e-6
NEG_INF = -1e30
WINDOW = 128
ATT_HEADS = 8
ATT_HD = 64
CONV_K = 31
DN_HEADS = 4
DN_K = 4
CHUNK = 64
D_IN = 7944
VMEM_LIMIT = 56 * 1024 * 1024

C_MG, C_QA, C_ZA, C_ZB, C_QC, C_KC, C_VC, C_GV, C_GG, C_ZC, C_KA, C_VA, C_AB, NP = (
    0, 3072, 3584, 4096, 4608, 5120, 5632, 6144, 6656, 7168, 7680, 7808, 7936, 8064)
_PAD_FROM = ((4872, 7944), (0, 512), (768, 1280), (2304, 2816), (2816, 4352), (1280, 2304), (4360, 4872),
             (512, 768), (4352, 4360))

ALIBI = tuple(float(2.0 ** (-8.0 * (h + 1) / ATT_HEADS)) for h in range(ATT_HEADS))

ADAM_LR, ADAM_B1, ADAM_B2, ADAM_EPS, ADAM_WD, ADAM_STEP = 0.001, 0.9, 0.999, 1e-08, 0.01, 10


def _cparams(sem=None):
    return pltpu.CompilerParams(dimension_semantics=sem, vmem_limit_bytes=VMEM_LIMIT)


def _sig(x):
    return jax.nn.sigmoid(x)


def _silu(x):
    return x * _sig(x)


def _dsilu(x):
    s = _sig(x)
    return s * (1.0 + x * (1.0 - s))


def _dot(a, b, dims=((1,), (0,)), precision=None):
    return lax.dot_general(a, b, (dims, ((), ())), preferred_element_type=F32, precision=precision)


def _bdot(a, b, dims=((1,), (0,))):
    return _dot(a.astype(BF16), b.astype(BF16), dims)


NN, NT, TN = ((1,), (0,)), ((1,), (1,)), ((0,), (0,))


def _row(v):
    return v.reshape(1, -1)


def _mm(a, b, *, ta=False, tb=False, tm, tn, tk, name):
    M, K = (a.shape[1], a.shape[0]) if ta else a.shape
    N = b.shape[0] if tb else b.shape[1]
    assert M % tm == 0 and N % tn == 0 and K % tk == 0, (M, N, K, tm, tn, tk)
    nk = K // tk
    dims = ((0 if ta else 1,), (1 if tb else 0,))

    def body(a_ref, b_ref, o_ref):
        k = pl.program_id(2)
        part = _bdot(a_ref[...], b_ref[...], dims)

        @pl.when(k == 0)
        def _():
            o_ref[...] = part

        @pl.when(k > 0)
        def _():
            o_ref[...] += part

    a_spec = pl.BlockSpec((tk, tm), lambda i, j, k: (k, i)) if ta else pl.BlockSpec((tm, tk), lambda i, j, k: (i, k))
    b_spec = pl.BlockSpec((tn, tk), lambda i, j, k: (j, k)) if tb else pl.BlockSpec((tk, tn), lambda i, j, k: (k, j))
    return pl.pallas_call(
        body, name=name, grid=(M // tm, N // tn, nk),
        in_specs=[a_spec, b_spec], out_specs=pl.BlockSpec((tm, tn), lambda i, j, k: (i, j)),
        out_shape=jax.ShapeDtypeStruct((M, N), F32),
        compiler_params=_cparams(("parallel", "parallel", "arbitrary")),
    )(a, b)


def _norm_fwd(x, norm_g, scale, shift, name):
    T = x.shape[0]
    tm = min(512, T)

    def body(x_ref, g_ref, sc_ref, sh_ref, h_ref):
        xv = x_ref[...]
        r = lax.rsqrt(jnp.mean(xv * xv, axis=-1, keepdims=True) + EPS)
        h_ref[...] = ((xv * r) * g_ref[...] * (1.0 + sc_ref[...]) + sh_ref[...]).astype(BF16)

    vec = pl.BlockSpec((1, D), lambda i: (0, 0))
    return pl.pallas_call(
        body, name=name, grid=(T // tm,),
        in_specs=[pl.BlockSpec((tm, D), lambda i: (i, 0)), vec, vec, vec],
        out_specs=pl.BlockSpec((tm, D), lambda i: (i, 0)),
        out_shape=jax.ShapeDtypeStruct((T, D), BF16),
        compiler_params=_cparams(("parallel",)),
    )(x, _row(norm_g), _row(scale), _row(shift))


def _norm_bwd(dh, x, dres, norm_g, scale, name):
    T = x.shape[0]
    tm = min(512, T)

    def body(dh_ref, x_ref, dr_ref, g_ref, sc_ref, dx_ref, st_ref):
        i = pl.program_id(0)
        xv, dhv = x_ref[...], dh_ref[...]
        r = lax.rsqrt(jnp.mean(xv * xv, axis=-1, keepdims=True) + EPS)
        xh = xv * r
        g, s1 = g_ref[...], 1.0 + sc_ref[...]
        dxh = dhv * (g * s1)
        dx_ref[...] = dr_ref[...] + r * (dxh - xh * jnp.mean(dxh * xh, axis=-1, keepdims=True))
        dhx = dhv * xh
        upd = jnp.concatenate([jnp.sum(dhv, axis=0, keepdims=True), jnp.sum(dhx * g, axis=0, keepdims=True),
                               jnp.sum(dhx * s1, axis=0, keepdims=True), jnp.zeros((5, D), F32)], axis=0)

        @pl.when(i == 0)
        def _():
            st_ref[...] = upd

        @pl.when(i > 0)
        def _():
            st_ref[...] += upd

    vec = pl.BlockSpec((1, D), lambda i: (0, 0))
    blk = pl.BlockSpec((tm, D), lambda i: (i, 0))
    return pl.pallas_call(
        body, name=name, grid=(T // tm,),
        in_specs=[blk, blk, blk, vec, vec],
        out_specs=[blk, pl.BlockSpec((8, D), lambda i: (0, 0))],
        out_shape=[jax.ShapeDtypeStruct((T, D), F32), jax.ShapeDtypeStruct((8, D), F32)],
        compiler_params=_cparams(("arbitrary",)),
    )(dh, x, dres, _row(norm_g), _row(scale))


def _rms(x, g):
    r = lax.rsqrt(jnp.mean(x * x, axis=-1, keepdims=True) + EPS)
    return x * r, r


def _head_mean_matrix():
    head = np.arange(ATT_HEADS * ATT_HD) // ATT_HD
    return jnp.asarray((head[:, None] == head[None, :]) * (1.0 / ATT_HD), BF16)


def _head_rms(x, hm):
    r = lax.rsqrt(_dot_exact(x * x, hm) + EPS)
    return x * r, r


def _head_rms_bwd(dy, xh, r, g, hm):
    dxh = dy * g
    return r * (dxh - xh * _dot_exact(dxh * xh, hm)), dy * xh


def _attn_mask(n):
    qi = lax.broadcasted_iota(jnp.int32, (WINDOW, 2 * WINDOW), 0)
    kj = lax.broadcasted_iota(jnp.int32, (WINDOW, 2 * WINDOW), 1)
    dist = qi + WINDOW - kj
    valid = (dist >= 0) & (dist < WINDOW) & ((n > 0) | (kj >= WINDOW))
    return valid, dist.astype(F32)


def _attn_probs(s, h, sink, valid, distf):
    s = s - ALIBI[h] * distf
    s = jnp.where(valid, s, NEG_INF)
    m = jnp.maximum(jnp.max(s, axis=-1, keepdims=True), sink)
    p = jnp.exp(s - m)
    es = jnp.exp(sink - m)
    den = jnp.sum(p, axis=-1, keepdims=True) + es
    return p / den, es / den


def _attn_fwd(proj, q_norm_g, k_norm_g, sinks, name):
    T = proj.shape[0]
    nb = T // WINDOW

    def body(sink_ref, q_ref, z_ref, kc_ref, kp_ref, vc_ref, vp_ref, qg_ref, kg_ref, hm_ref, o_ref):
        n = pl.program_id(0)
        valid, distf = _attn_mask(n)
        k2 = jnp.concatenate([kp_ref[...], kc_ref[...]], axis=0)
        v2 = jnp.concatenate([vp_ref[...], vc_ref[...]], axis=0).astype(BF16)
        kn = (_head_rms(k2, hm_ref[:128, :128])[0] * kg_ref[...]).astype(BF16)
        qn = ((_head_rms(q_ref[...], hm_ref[...])[0] * qg_ref[...]) * (ATT_HD ** -0.5)).astype(BF16)

        def head(h):
            sl, gsl = slice(64 * h, 64 * h + 64), slice(64 * (h // 4), 64 * (h // 4) + 64)
            s = _dot(qn[:, sl], kn[:, gsl], NT)
            yield
            p, _ = _attn_probs(s, h, sink_ref[h], valid, distf)
            o_ref[:, sl] = _dot(p.astype(BF16), v2[:, gsl])
            yield

        _lockstep([head(h) for h in range(ATT_HEADS)])
        o_ref[...] = o_ref[...] * _silu(z_ref[...])

    prev = lambda n: jnp.maximum(n - 1, 0)
    return pl.pallas_call(
        body, name=name, grid=(nb,),
        in_specs=[pl.BlockSpec(memory_space=pltpu.SMEM),
                  pl.BlockSpec((WINDOW, 512), lambda n: (n, C_QA // 512)),
                  pl.BlockSpec((WINDOW, 512), lambda n: (n, C_ZA // 512)),
                  pl.BlockSpec((WINDOW, 128), lambda n: (n, C_KA // 128)),
                  pl.BlockSpec((WINDOW, 128), lambda n: (prev(n), C_KA // 128)),
                  pl.BlockSpec((WINDOW, 128), lambda n: (n, C_VA // 128)),
                  pl.BlockSpec((WINDOW, 128), lambda n: (prev(n), C_VA // 128)),
                  pl.BlockSpec((1, 512), lambda n: (0, 0)), pl.BlockSpec((1, 128), lambda n: (0, 0)),
                  pl.BlockSpec((512, 512), lambda n: (0, 0))],
        out_specs=pl.BlockSpec((WINDOW, 512), lambda n: (n, 0)),
        out_shape=jax.ShapeDtypeStruct((T, 512), F32),
        compiler_params=_cparams(("parallel",)),
    )(sinks, proj, proj, proj, proj, proj, proj, _row(jnp.tile(q_norm_g, ATT_HEADS)), _row(jnp.tile(k_norm_g, 2)),
      _head_mean_matrix())


def _rms_bwd(dy, xh, r, g):
    dxh = dy * g
    return r * (dxh - xh * jnp.mean(dxh * xh, axis=-1, keepdims=True)), dy * xh


def _attn_bwd(dproj, proj, dya, q_norm_g, k_norm_g, sinks, name):
    T = proj.shape[0]
    nb = T // WINDOW

    def body(sink_ref, dp_any, q_ref, z_ref, kc_ref, kp_ref, vc_ref, vp_ref, dy_ref, qg_ref, kg_ref, hm_ref,
             dqz_ref, dkv_ref, gq_ref, gk_ref, gs_ref, ck_ref, cv_ref, o_sc, dq_sc):
        n = pl.program_id(0)

        @pl.when(n == 0)
        def _():
            gq_ref[...] = jnp.zeros_like(gq_ref)
            gk_ref[...] = jnp.zeros_like(gk_ref)
            gs_ref[...] = jnp.zeros_like(gs_ref)
            ck_ref[...] = jnp.zeros_like(ck_ref)
            cv_ref[...] = jnp.zeros_like(cv_ref)

        lane8 = lax.broadcasted_iota(jnp.int32, (1, 8), 1)

        @pl.when(n < nb)
        def _():
            valid, distf = _attn_mask(n)
            k2 = jnp.concatenate([kp_ref[...], kc_ref[...]], axis=0)
            v2 = jnp.concatenate([vp_ref[...], vc_ref[...]], axis=0).astype(BF16)
            kn = (_head_rms(k2, hm_ref[:128, :128])[0] * kg_ref[...]).astype(BF16)
            qh, qr = _head_rms(q_ref[...], hm_ref[...])
            qn = ((qh * qg_ref[...]) * (ATT_HD ** -0.5)).astype(BF16)
            zs = z_ref[...]
            do_all = dy_ref[...] * _silu(zs)
            dob_all = do_all.astype(BF16)

            def head(h):
                sl, gsl = slice(64 * h, 64 * h + 64), slice(64 * (h // 4), 64 * (h // 4) + 64)
                s = _dot(qn[:, sl], kn[:, gsl], NT)
                dpm = _dot(dob_all[:, sl], v2[:, gsl], NT)
                yield
                p, ps = _attn_probs(s, h, sink_ref[h], valid, distf)
                pb = p.astype(BF16)
                o_sc[:, sl] = _dot(pb, v2[:, gsl])
                dvg = _dot(pb, dob_all[:, sl], TN)
                delta = jnp.sum(p * dpm, axis=-1, keepdims=True)
                ds = (p * (dpm - delta)).astype(BF16)
                gs = jnp.where(lane8 == h, -jnp.sum(ps * delta, axis=0, keepdims=True), 0.0)
                yield
                dkn = _dot(ds, qn[:, sl], TN)
                dq_sc[:, sl] = _dot(ds, kn[:, gsl])
                yield
                return dkn, dvg, gs

            res = _lockstep([head(h) for h in range(ATT_HEADS)])
            dqz_ref[:, 512:] = (dy_ref[...] * o_sc[...] * _dsilu(zs)).astype(BF16)
            dq, gq = _head_rms_bwd(dq_sc[...] * (ATT_HD ** -0.5), qh, qr, qg_ref[...], hm_ref[...])
            dqz_ref[:, :512] = dq.astype(BF16)
            gq_acc = jnp.sum(gq, axis=0, keepdims=True)
            gs_acc = sum(r[2] for r in res[1:]) + res[0][2]
            for g in range(2):
                dkn = (res[4 * g][0] + res[4 * g + 1][0]) + (res[4 * g + 2][0] + res[4 * g + 3][0])
                dvg = (res[4 * g][1] + res[4 * g + 1][1]) + (res[4 * g + 2][1] + res[4 * g + 3][1])
                ksl = slice(64 * g, 64 * g + 64)
                vsl = slice(128 + 64 * g, 128 + 64 * g + 64)
                dkv_ref[:, ksl] = ck_ref[:, ksl] + dkn[:WINDOW]
                dkv_ref[:, vsl] = cv_ref[:, ksl] + dvg[:WINDOW]
                ck_ref[:, ksl] = dkn[WINDOW:]
                cv_ref[:, ksl] = dvg[WINDOW:]
            gq_ref[...] += gq_acc
            gs_ref[...] += gs_acc

        @pl.when(n == nb)
        def _():
            dkv_ref[:, :128] = ck_ref[...]
            dkv_ref[:, 128:] = cv_ref[...]

        @pl.when(n > 0)
        def _():
            hm = hm_ref[:128, :128]
            kh, kr = _head_rms(kp_ref[...], hm)
            dk, gk = _head_rms_bwd(dkv_ref[:, :128], kh, kr, kg_ref[...], hm)
            dkv_ref[:, :128] = dk
            gk_ref[...] += jnp.sum(gk, axis=0, keepdims=True)

    cur = lambda n: jnp.minimum(n, nb - 1)
    prev = lambda n: jnp.maximum(n - 1, 0)
    small = lambda w: pl.BlockSpec((1, w), lambda n: (0, 0))
    return pl.pallas_call(
        body, name=name, grid=(nb + 1,),
        in_specs=[pl.BlockSpec(memory_space=pltpu.SMEM), pl.BlockSpec(memory_space=pl.ANY),
                  pl.BlockSpec((WINDOW, 512), lambda n: (cur(n), C_QA // 512)),
                  pl.BlockSpec((WINDOW, 512), lambda n: (cur(n), C_ZA // 512)),
                  pl.BlockSpec((WINDOW, 128), lambda n: (cur(n), C_KA // 128)),
                  pl.BlockSpec((WINDOW, 128), lambda n: (prev(n), C_KA // 128)),
                  pl.BlockSpec((WINDOW, 128), lambda n: (cur(n), C_VA // 128)),
                  pl.BlockSpec((WINDOW, 128), lambda n: (prev(n), C_VA // 128)),
                  pl.BlockSpec((WINDOW, 512), lambda n: (cur(n), 0)),
                  small(512), small(128), pl.BlockSpec((512, 512), lambda n: (0, 0))],
        out_specs=[pl.BlockSpec((WINDOW, 1024), lambda n: (cur(n), C_QA // 1024)),
                   pl.BlockSpec((WINDOW, 256), lambda n: (prev(n), 0)),
                   small(512), small(128), small(8)],
        out_shape=[jax.ShapeDtypeStruct(dproj.shape, BF16), jax.ShapeDtypeStruct((T, 256), F32),
                   jax.ShapeDtypeStruct((1, 512), F32), jax.ShapeDtypeStruct((1, 128), F32),
                   jax.ShapeDtypeStruct((1, 8), F32)],
        scratch_shapes=[pltpu.VMEM((WINDOW, 128), F32), pltpu.VMEM((WINDOW, 128), F32),
                        pltpu.VMEM((WINDOW, 512), F32), pltpu.VMEM((WINDOW, 512), F32)],
        input_output_aliases={1: 0},
        compiler_params=_cparams(("arbitrary",)),
    )(sinks, dproj, proj, proj, proj, proj, proj, proj, dya, _row(jnp.tile(q_norm_g, ATT_HEADS)),
      _row(jnp.tile(k_norm_g, 2)), _head_mean_matrix())


HALO_B = 32


def _conf_specs(T, tm):
    r = tm // HALO_B
    cur = lambda c: pl.BlockSpec((tm, 512), lambda i: (i, c // 512))
    prev = lambda c: pl.BlockSpec((HALO_B, 512), lambda i: (jnp.maximum(i * r - 1, 0), c // 512))
    return cur, prev


def _conf_core(i, tm, gv_ref, gg_ref, gvp_ref, ggp_ref, w_ref, b_ref, lg_ref, lb_ref, pw_ref, pb_ref, ext_ref):
    up = gvp_ref[...] * _sig(ggp_ref[...])
    ext_ref[:HALO_B] = jnp.where(i > 0, up, 0.0)
    ext_ref[HALO_B:] = gv_ref[...] * _sig(gg_ref[...])
    acc = jnp.zeros((tm, 512), F32) + b_ref[...]
    for k in range(CONV_K):
        acc = acc + w_ref[k:k + 1, :] * ext_ref[pl.ds(HALO_B - CONV_K + 1 + k, tm), :]
    mu = jnp.mean(acc, axis=-1, keepdims=True)
    xc = acc - mu
    rstd = lax.rsqrt(jnp.mean(xc * xc, axis=-1, keepdims=True) + EPS)
    xh = xc * rstd
    u2 = xh * lg_ref[...] + lb_ref[...]
    u3 = _silu(u2)
    ypre = _bdot(u3, pw_ref[...]) + pb_ref[...]
    return xh, rstd, u2, u3, ypre


def _conf_fwd(proj, dw_w, dw_b, ln_g, ln_b, pw2, pw2_b, name):
    T = proj.shape[0]
    tm = min(512, T)
    cur, prev = _conf_specs(T, tm)

    def body(gv_ref, gg_ref, gvp_ref, ggp_ref, zb_ref, w_ref, b_ref, lg_ref, lb_ref, pw_ref, pb_ref, o_ref, ext_ref):
        i = pl.program_id(0)
        ypre = _conf_core(i, tm, gv_ref, gg_ref, gvp_ref, ggp_ref, w_ref, b_ref, lg_ref, lb_ref, pw_ref, pb_ref,
                          ext_ref)[4]
        o_ref[...] = ypre * _silu(zb_ref[...])

    full = lambda s: pl.BlockSpec(s, lambda i: (0, 0))
    return pl.pallas_call(
        body, name=name, grid=(T // tm,),
        in_specs=[cur(C_GV), cur(C_GG), prev(C_GV), prev(C_GG), cur(C_ZB), full((CONV_K, 512)), full((1, 512)),
                  full((1, 512)), full((1, 512)), full((512, 512)), full((1, 512))],
        out_specs=pl.BlockSpec((tm, 512), lambda i: (i, 0)),
        out_shape=jax.ShapeDtypeStruct((T, 512), F32),
        scratch_shapes=[pltpu.VMEM((tm + HALO_B, 512), F32)],
        compiler_params=_cparams(("parallel",)),
    )(proj, proj, proj, proj, proj, dw_w, _row(dw_b), _row(ln_g), _row(ln_b), pw2, _row(pw2_b))


def _conf_bwd1(dproj, proj, dyb, dw_w, dw_b, ln_g, ln_b, pw2, pw2_b, name):
    T = proj.shape[0]
    tm = min(512, T)
    cur, prev = _conf_specs(T, tm)

    def body(dp_any, gv_ref, gg_ref, gvp_ref, ggp_ref, zb_ref, dy_ref, w_ref, b_ref, lg_ref, lb_ref, pw_ref, pb_ref,
             dzb_ref, du1_ref, gpw_ref, st_ref, ext_ref):
        i = pl.program_id(0)
        xh, rstd, u2, u3, ypre = _conf_core(i, tm, gv_ref, gg_ref, gvp_ref, ggp_ref, w_ref, b_ref, lg_ref, lb_ref,
                                            pw_ref, pb_ref, ext_ref)
        zb, dy = zb_ref[...], dy_ref[...]
        dzb_ref[...] = (dy * ypre * _dsilu(zb)).astype(BF16)
        dyp = dy * _silu(zb)
        du2 = _bdot(dyp, pw_ref[...], NT) * _dsilu(u2)
        dxh = du2 * lg_ref[...]
        du1 = rstd * (dxh - jnp.mean(dxh, axis=-1, keepdims=True) - xh * jnp.mean(dxh * xh, axis=-1, keepdims=True))
        du1_ref[...] = du1
        gpw = _bdot(u3, dyp, TN)
        rs = lambda a: jnp.sum(a, axis=0, keepdims=True)
        upd = jnp.concatenate([rs(dyp), rs(du2 * xh), rs(du2), rs(du1), jnp.zeros((4, 512), F32)], axis=0)

        @pl.when(i == 0)
        def _():
            gpw_ref[...] = gpw
            st_ref[...] = upd

        @pl.when(i > 0)
        def _():
            gpw_ref[...] += gpw
            st_ref[...] += upd

    full = lambda s: pl.BlockSpec(s, lambda i: (0, 0))
    blk = pl.BlockSpec((tm, 512), lambda i: (i, 0))
    return pl.pallas_call(
        body, name=name, grid=(T // tm,),
        in_specs=[pl.BlockSpec(memory_space=pl.ANY), cur(C_GV), cur(C_GG), prev(C_GV), prev(C_GG), cur(C_ZB), blk,
                  full((CONV_K, 512)), full((1, 512)), full((1, 512)), full((1, 512)), full((512, 512)), full((1, 512))],
        out_specs=[cur(C_ZB), blk, full((512, 512)), full((8, 512))],
        out_shape=[jax.ShapeDtypeStruct(dproj.shape, BF16), jax.ShapeDtypeStruct((T, 512), F32),
                   jax.ShapeDtypeStruct((512, 512), F32), jax.ShapeDtypeStruct((8, 512), F32)],
        scratch_shapes=[pltpu.VMEM((tm + HALO_B, 512), F32)],
        input_output_aliases={0: 0},
        compiler_params=_cparams(("arbitrary",)),
    )(dproj, proj, proj, proj, proj, proj, dyb, dw_w, _row(dw_b), _row(ln_g), _row(ln_b), pw2, _row(pw2_b))


def _conf_bwd2(dproj, proj, du1, dw_w, name):
    T = proj.shape[0]
    tm = min(512, T)
    nt = T // tm
    r = tm // HALO_B
    cur, prev = _conf_specs(T, tm)

    def body(dp_any, gv_ref, gg_ref, gvp_ref, ggp_ref, du_ref, dun_ref, w_ref, dglu_ref, gw_ref, extu_ref, extd_ref):
        i = pl.program_id(0)
        gv, sg = gv_ref[...], _sig(gg_ref[...])
        extu_ref[:HALO_B] = jnp.where(i > 0, gvp_ref[...] * _sig(ggp_ref[...]), 0.0)
        extu_ref[HALO_B:] = gv * sg
        du1 = du_ref[...]
        extd_ref[:tm] = du1
        extd_ref[tm:] = jnp.where(i < nt - 1, dun_ref[...], 0.0)
        du0 = jnp.zeros((tm, 512), F32)
        rows = []
        for k in range(CONV_K):
            du0 = du0 + w_ref[k:k + 1, :] * extd_ref[pl.ds(CONV_K - 1 - k, tm), :]
            rows.append(jnp.sum(du1 * extu_ref[pl.ds(HALO_B - CONV_K + 1 + k, tm), :], axis=0, keepdims=True))
        rows.append(jnp.zeros((1, 512), F32))
        gw = jnp.concatenate(rows, axis=0)
        dglu_ref[:, :512] = (du0 * sg).astype(BF16)
        dglu_ref[:, 512:] = (du0 * gv * sg * (1.0 - sg)).astype(BF16)

        @pl.when(i == 0)
        def _():
            gw_ref[...] = gw

        @pl.when(i > 0)
        def _():
            gw_ref[...] += gw

    full = lambda s: pl.BlockSpec(s, lambda i: (0, 0))
    return pl.pallas_call(
        body, name=name, grid=(nt,),
        in_specs=[pl.BlockSpec(memory_space=pl.ANY), cur(C_GV), cur(C_GG), prev(C_GV), prev(C_GG),
                  pl.BlockSpec((tm, 512), lambda i: (i, 0)),
                  pl.BlockSpec((HALO_B, 512), lambda i: (jnp.minimum((i + 1) * r, T // HALO_B - 1), 0)),
                  full((CONV_K, 512))],
        out_specs=[pl.BlockSpec((tm, 1024), lambda i: (i, C_GV // 1024)), full((32, 512))],
        out_shape=[jax.ShapeDtypeStruct(dproj.shape, BF16), jax.ShapeDtypeStruct((32, 512), F32)],
        scratch_shapes=[pltpu.VMEM((tm + HALO_B, 512), F32), pltpu.VMEM((tm + HALO_B, 512), F32)],
        input_output_aliases={0: 0},
        compiler_params=_cparams(("arbitrary",)),
    )(dproj, proj, proj, proj, proj, du1, du1, dw_w)


HALO_C = 8
QKV_C = 1536


def _softplus(x):
    return jnp.maximum(x, 0.0) + jnp.log1p(jnp.exp(-jnp.abs(x)))


def _gdn_conv(i, tm, x_ref, xp_ref, w_ref, ext_ref):
    ext_ref[:HALO_C] = jnp.where(i > 0, xp_ref[...], 0.0)
    ext_ref[HALO_C:] = x_ref[...]
    pre = jnp.zeros((tm, QKV_C), F32)
    for k in range(DN_K):
        pre = pre + w_ref[k:k + 1, :] * ext_ref[pl.ds(HALO_C - DN_K + 1 + k, tm), :]
    return pre


def _gdn_specs(T, tm):
    r = tm // HALO_C
    cur = pl.BlockSpec((tm, QKV_C), lambda i: (i, C_QC // QKV_C))
    prev = pl.BlockSpec((HALO_C, QKV_C), lambda i: (jnp.maximum(i * r - 1, 0), C_QC // QKV_C))
    ab = pl.BlockSpec((tm, 128), lambda i: (i, C_AB // 128))
    return cur, prev, ab


def _gdn_prep_fwd(proj, sconv_w, alog_v, dtb_v, name):
    T = proj.shape[0]
    tm = min(512, T)
    cur, prev, ab = _gdn_specs(T, tm)

    def body(x_ref, xp_ref, ab_ref, w_ref, al_ref, dt_ref, q_ref, k_ref, v_ref, gb_ref, ext_ref):
        i = pl.program_id(0)
        y = _silu(_gdn_conv(i, tm, x_ref, xp_ref, w_ref, ext_ref))
        for h in range(DN_HEADS):
            sl = slice(128 * h, 128 * h + 128)
            qh, kh = y[:, sl], y[:, 512 + 128 * h:512 + 128 * h + 128]
            q_ref[:, sl] = qh * lax.rsqrt(jnp.sum(qh * qh, axis=-1, keepdims=True) + EPS) * (128 ** -0.5)
            k_ref[:, sl] = kh * lax.rsqrt(jnp.sum(kh * kh, axis=-1, keepdims=True) + EPS)
        v_ref[...] = y[:, 1024:]
        abv = ab_ref[...]
        lane = lax.broadcasted_iota(jnp.int32, (tm, 128), 1)
        g = -jnp.exp(al_ref[...]) * _softplus(abv + dt_ref[...])
        gb_ref[...] = jnp.where(lane < DN_HEADS, g, _sig(abv))

    full = lambda s: pl.BlockSpec(s, lambda i: (0, 0))
    blk = pl.BlockSpec((tm, 512), lambda i: (i, 0))
    return pl.pallas_call(
        body, name=name, grid=(T // tm,),
        in_specs=[cur, prev, ab, full((DN_K, QKV_C)), full((1, 128)), full((1, 128))],
        out_specs=[blk, blk, blk, pl.BlockSpec((tm, 128), lambda i: (i, 0))],
        out_shape=[jax.ShapeDtypeStruct((T, 512), F32)] * 3 + [jax.ShapeDtypeStruct((T, 128), F32)],
        scratch_shapes=[pltpu.VMEM((tm + HALO_C, QKV_C), F32)],
        compiler_params=_cparams(("parallel",)),
    )(proj, proj, proj, sconv_w, alog_v, dtb_v)


def _hdot(a, b, dims=NN):
    return _dot(a, b, dims, precision=HI)


def _lockstep(gens):
    results, live = [None] * len(gens), list(range(len(gens)))
    while live:
        for i in list(live):
            try:
                next(gens[i])
            except StopIteration as stop:
                results[i] = stop.value
                live.remove(i)
    return results


def _split(a):
    hi = a.astype(BF16)
    return hi, (a - hi.astype(F32)).astype(BF16)


def _dot_exact(a, b, dims=NN, split_left=True):
    x = (a if split_left else b).astype(F32)
    hi = x.astype(BF16)
    r = x - hi.astype(F32)
    mid = r.astype(BF16)
    lo = (r - mid.astype(F32)).astype(BF16)
    other = (b if split_left else a).astype(BF16)
    one = (lambda p: _dot(p, other, dims)) if split_left else (lambda p: _dot(other, p, dims))
    return (one(lo) + one(mid)) + one(hi)


def _dot3(a, b):
    (ah, al), (bh, bl) = a, b
    return _dot(ah, bh) + (_dot(ah, bl) + _dot(al, bh))


def _tri_inv(mats, eye):
    ps = [-a for a in mats]
    ts = [eye + p for p in ps]
    for _ in range(5):
        sp = [_split(p) for p in ps]
        ps = [_dot3(s, s) for s in sp]
        sp = [_split(p) for p in ps]
        ts = [t + _dot3(_split(t), s) for t, s in zip(ts, sp)]
    return ts


def _tri_consts():
    ii = lax.broadcasted_iota(jnp.int32, (CHUNK, CHUNK), 0)
    jj = lax.broadcasted_iota(jnp.int32, (CHUNK, CHUNK), 1)
    return ii >= jj, ii > jj, (ii == jj).astype(F32)


def _gdn_local(q, k, v, gcol, grow, bcol, lower, strict):
    dm = jnp.where(lower, jnp.exp(jnp.where(lower, gcol - grow, 0.0)), 0.0)
    kb = k * bcol
    a = jnp.where(strict, _bdot(kb, k, NT) * dm, 0.0)
    gc = jnp.exp(gcol)
    glast = grow[:, CHUNK - 1:CHUNK]
    return dict(q=q, k=k, v=v, bcol=bcol, gcol=gcol, glast=glast, dm=dm, kb=kb, a=a, gc=gc, vb=v * bcol,
                kbg=kb * gc, p=_bdot(q, k, NT) * dm, qe=q * gc, ke=k * jnp.exp(glast - gcol))


def _gdn_chunk_bwd(c, do, dvn, ds_new, lower, strict, ones):
    rs = lambda m: jnp.sum(m, axis=-1, keepdims=True)
    colsum = lambda m: _dot_exact(m, ones, TN)[:, :1]
    q, k, v, bcol, dm, tm, gc, s = c["q"], c["k"], c["v"], c["bcol"], c["dm"], c["tm"], c["gc"], c["s"]
    eg = jnp.exp(c["glast"])
    dqe = _bdot(do, s, NT)
    dp = jnp.where(lower, _bdot(do, c["vn"], NT), 0.0)
    dw = -_bdot(dvn, s, NT)
    dke = _bdot(c["vn"], ds_new, NT)
    dvb = _bdot(tm, dvn, TN)
    yield
    dglast = jnp.sum(rs(ds_new * s), axis=0, keepdims=True) * eg
    dk = dke * jnp.exp(c["glast"] - c["gcol"])
    r_ke = rs(dke * c["ke"])
    dglast = dglast + jnp.sum(r_ke, axis=0, keepdims=True)
    dgam = rs(dqe * c["qe"]) - r_ke
    dq = dqe * gc
    dpm = dp * dm
    mp = dp * c["p"]
    dq = dq + _bdot(dpm, k)
    dk = dk + _bdot(dpm, q, TN)
    dt = _bdot(dvn, c["vb"], NT) + _bdot(dw, c["kbg"], NT)
    dkbg = _bdot(tm, dw, TN)
    dgam = dgam + rs(mp) - colsum(mp)
    yield
    dkb = dkbg * gc
    dgam = dgam + rs(dkbg * c["kbg"])
    dat = _bdot(tm, dt, TN)
    yield
    da = jnp.where(strict, -_bdot(dat, tm, NT), 0.0)
    yield
    dam = da * dm
    ma = da * c["a"]
    dkb = dkb + _bdot(dam, k)
    dk = dk + _bdot(dam, c["kb"], TN)
    dgam = dgam + rs(ma) - colsum(ma)
    yield
    dk = dk + dkb * bcol
    dbeta = rs(dkb * k) + rs(dvb * v)
    dv = dvb * bcol
    row = lax.broadcasted_iota(jnp.int32, (CHUNK, 1), 0)
    dgam = dgam + jnp.where(row == CHUNK - 1, dglast, 0.0)
    dg = _dot_exact(lower, dgam, TN, split_left=False)
    return dq, dk, dv, dg, dbeta


GROUP = 4


def _chunk_decay(gb_ref, gt_ref, lmat, g):
    rows = slice(CHUNK * g, CHUNK * g + CHUNK)
    return rows, _dot_exact(lmat, gb_ref[rows, :], split_left=False), _dot_exact(gt_ref[g], lmat, NT)


def _gdn_chunk_fwd(qd, kd, vd, gb, gbt, name):
    T = qd.shape[0]
    G = GROUP
    ng = T // (CHUNK * G)

    def body(q_ref, k_ref, v_ref, gb_ref, gt_ref, u_ref, w_ref, qe_ref, ke_ref, p_ref, t_ref, eg_ref):
        lower, strict, eye = _tri_consts()
        lmat = lower.astype(F32)
        decay = [_chunk_decay(gb_ref, gt_ref, lmat, g) for g in range(G)]
        chains = [(g, h) for g in range(G) for h in range(DN_HEADS)]
        cs = []
        for g, h in chains:
            rows, gcs, grs = decay[g]
            sl = slice(128 * h, 128 * h + 128)
            c = _gdn_local(q_ref[rows, sl], k_ref[rows, sl], v_ref[rows, sl], gcs[:, h:h + 1], grs[h:h + 1, :],
                           gb_ref[rows, DN_HEADS + h:DN_HEADS + h + 1], lower, strict)
            qe_ref[rows, sl] = c["qe"].astype(BF16)
            ke_ref[rows, sl] = c["ke"].astype(BF16)
            p_ref[rows, 64 * h:64 * h + 64] = c["p"].astype(BF16)
            eg_ref[g, h:h + 1, :] = jnp.broadcast_to(jnp.exp(c["glast"]), (1, 128))
            cs.append(c)
        tms = [t.astype(BF16) for t in _tri_inv([c["a"] for c in cs], eye)]
        us = [_dot(t, c["vb"].astype(BF16)) for t, c in zip(tms, cs)]
        ws = [_dot(t, c["kbg"].astype(BF16)) for t, c in zip(tms, cs)]
        for (g, h), tm, u, w in zip(chains, tms, us, ws):
            rows, sl = decay[g][0], slice(128 * h, 128 * h + 128)
            u_ref[rows, sl] = u
            w_ref[rows, sl] = w.astype(BF16)
            t_ref[rows, 64 * h:64 * h + 64] = tm
        for g in range(G):
            eg_ref[g, DN_HEADS:, :] = jnp.zeros((8 - DN_HEADS, 128), F32)

    blk = pl.BlockSpec((CHUNK * G, 512), lambda n: (n, 0))
    half = pl.BlockSpec((CHUNK * G, 256), lambda n: (n, 0))
    return pl.pallas_call(
        body, name=name, grid=(ng,),
        in_specs=[blk, blk, blk, pl.BlockSpec((CHUNK * G, 128), lambda n: (n, 0)),
                  pl.BlockSpec((G, 8, CHUNK), lambda n: (n, 0, 0))],
        out_specs=[blk, blk, blk, blk, half, half, pl.BlockSpec((G, 8, 128), lambda n: (n, 0, 0))],
        out_shape=[jax.ShapeDtypeStruct((T, 512), F32)] + [jax.ShapeDtypeStruct((T, 512), BF16)] * 3
        + [jax.ShapeDtypeStruct((T, 256), BF16)] * 2 + [jax.ShapeDtypeStruct((T // CHUNK, 8, 128), F32)],
        compiler_params=_cparams(("parallel",)),
    )(qd, kd, vd, gb, gbt)


def _gdn_scan_fwd(u, w, qe, ke, pm, eg, proj, dn_g, name):
    T = u.shape[0]
    nc = T // CHUNK

    def body(u_ref, w_ref, qe_ref, ke_ref, p_ref, eg_ref, z_ref, ng_ref, y_ref, o_ref, vn_ref, ss_ref, s_ref):
        n = pl.program_id(0)

        @pl.when(n == 0)
        def _():
            s_ref[...] = jnp.zeros_like(s_ref)

        ss_ref[0] = s_ref[...]

        def head(h):
            sl = slice(128 * h, 128 * h + 128)
            s = s_ref[h]
            sb = s.astype(BF16)
            vn = u_ref[:, sl] - _dot(w_ref[:, sl], sb)
            qs = _dot(qe_ref[:, sl], sb)
            yield
            vb = vn.astype(BF16)
            o = qs + _dot(p_ref[:, 64 * h:64 * h + 64], vb)
            s_ref[h] = s * eg_ref[0, h:h + 1, :] + _dot(ke_ref[:, sl], vb, TN)
            yield
            vn_ref[:, sl] = vb
            o_ref[:, sl] = o
            y_ref[:, sl] = _rms(o, None)[0] * ng_ref[...] * _silu(z_ref[:, sl])

        _lockstep([head(h) for h in range(DN_HEADS)])

    blk = pl.BlockSpec((CHUNK, 512), lambda n: (n, 0))
    return pl.pallas_call(
        body, name=name, grid=(nc,),
        in_specs=[blk, blk, blk, blk, pl.BlockSpec((CHUNK, 256), lambda n: (n, 0)),
                  pl.BlockSpec((1, 8, 128), lambda n: (n, 0, 0)),
                  pl.BlockSpec((CHUNK, 512), lambda n: (n, C_ZC // 512)), pl.BlockSpec((1, 128), lambda n: (0, 0))],
        out_specs=[blk, blk, blk, pl.BlockSpec((1, DN_HEADS, 128, 128), lambda n: (n, 0, 0, 0))],
        out_shape=[jax.ShapeDtypeStruct((T, 512), F32), jax.ShapeDtypeStruct((T, 512), F32),
                   jax.ShapeDtypeStruct((T, 512), BF16), jax.ShapeDtypeStruct((nc, DN_HEADS, 128, 128), F32)],
        scratch_shapes=[pltpu.VMEM((DN_HEADS, 128, 128), F32)],
        compiler_params=_cparams(("arbitrary",)),
    )(u, w, qe, ke, pm, eg, proj, dn_g)


def _gdn_scan_bwd(dproj, w, qe, ke, pm, eg, o, proj, dyc, dn_g, name):
    T = o.shape[0]
    nc = T // CHUNK
    rev = lambda n: nc - 1 - n

    def body(dp_any, w_ref, qe_ref, ke_ref, p_ref, eg_ref, o_ref, z_ref, dy_ref, ng_ref,
             dz_ref, do_ref, dvn_ref, dsn_ref, gng_ref, ds_ref):
        n = pl.program_id(0)

        @pl.when(n == 0)
        def _():
            ds_ref[...] = jnp.zeros_like(ds_ref)
            gng_ref[...] = jnp.zeros_like(gng_ref)

        dsn_ref[0] = ds_ref[...]

        def head(h):
            sl = slice(128 * h, 128 * h + 128)
            oh, r = _rms(o_ref[:, sl], None)
            z, dy = z_ref[:, sl], dy_ref[:, sl]
            dz_ref[:, sl] = (dy * (oh * ng_ref[...]) * _dsilu(z)).astype(BF16)
            do, gg = _rms_bwd(dy * _silu(z), oh, r, ng_ref[...])
            dob = do.astype(BF16)
            ds = ds_ref[h]
            dvn = _dot(p_ref[:, 64 * h:64 * h + 64], dob, TN) + _dot(ke_ref[:, sl], ds.astype(BF16))
            qd = _dot(qe_ref[:, sl], dob, TN)
            yield
            dvb = dvn.astype(BF16)
            ds_ref[h] = qd + eg_ref[0, h:h + 1, :] * ds - _dot(w_ref[:, sl], dvb, TN)
            do_ref[:, sl] = dob
            dvn_ref[:, sl] = dvb
            return jnp.sum(gg, axis=0, keepdims=True)

        gng = _lockstep([head(h) for h in range(DN_HEADS)])
        gng_ref[...] += (gng[0] + gng[1]) + (gng[2] + gng[3])

    blk = pl.BlockSpec((CHUNK, 512), lambda n: (rev(n), 0))
    state = pl.BlockSpec((1, DN_HEADS, 128, 128), lambda n: (rev(n), 0, 0, 0))
    return pl.pallas_call(
        body, name=name, grid=(nc,),
        in_specs=[pl.BlockSpec(memory_space=pl.ANY), blk, blk, blk, pl.BlockSpec((CHUNK, 256), lambda n: (rev(n), 0)),
                  pl.BlockSpec((1, 8, 128), lambda n: (rev(n), 0, 0)), blk,
                  pl.BlockSpec((CHUNK, 512), lambda n: (rev(n), C_ZC // 512)), blk,
                  pl.BlockSpec((1, 128), lambda n: (0, 0))],
        out_specs=[pl.BlockSpec((CHUNK, 512), lambda n: (rev(n), C_ZC // 512)), blk, blk, state,
                   pl.BlockSpec((1, 128), lambda n: (0, 0))],
        out_shape=[jax.ShapeDtypeStruct(dproj.shape, BF16), jax.ShapeDtypeStruct((T, 512), BF16),
                   jax.ShapeDtypeStruct((T, 512), BF16), jax.ShapeDtypeStruct((nc, DN_HEADS, 128, 128), F32),
                   jax.ShapeDtypeStruct((1, 128), F32)],
        scratch_shapes=[pltpu.VMEM((DN_HEADS, 128, 128), F32)],
        input_output_aliases={0: 0},
        compiler_params=_cparams(("arbitrary",)),
    )(dproj, w, qe, ke, pm, eg, o, proj, dyc, dn_g)


def _gdn_chunk_grad(qd, kd, vd, gb, gbt, tmi, ssave, dsn, do, dvn, vn, name):
    T = qd.shape[0]
    G = GROUP
    ng = T // (CHUNK * G)

    def body(q_ref, k_ref, v_ref, gb_ref, gt_ref, t_ref, ss_ref, dsn_ref, do_ref, dvn_ref, vn_ref,
             dq_ref, dk_ref, dv_ref, dgb_ref):
        lower, strict, _ = _tri_consts()
        lmat = lower.astype(F32)
        ones = jnp.ones((CHUNK, 128), F32)
        lane = lax.broadcasted_iota(jnp.int32, (CHUNK, 128), 1)
        decay = [_chunk_decay(gb_ref, gt_ref, lmat, g) for g in range(G)]
        chains = [(g, h) for g in range(G) for h in range(DN_HEADS)]
        gens = []
        for g, h in chains:
            rows, gcs, grs = decay[g]
            sl = slice(128 * h, 128 * h + 128)
            c = _gdn_local(q_ref[rows, sl], k_ref[rows, sl], v_ref[rows, sl], gcs[:, h:h + 1], grs[h:h + 1, :],
                           gb_ref[rows, DN_HEADS + h:DN_HEADS + h + 1], lower, strict)
            c.update(tm=t_ref[rows, 64 * h:64 * h + 64], s=ss_ref[g, h], vn=vn_ref[rows, sl])
            gens.append(_gdn_chunk_bwd(c, do_ref[rows, sl], dvn_ref[rows, sl], dsn_ref[g, h], lower, strict, ones))
        dgb = [jnp.zeros((CHUNK, 128), F32) for _ in range(G)]
        for (g, h), (dq, dk, dv, dg, dbeta) in zip(chains, _lockstep(gens)):
            rows, sl = decay[g][0], slice(128 * h, 128 * h + 128)
            dq_ref[rows, sl], dk_ref[rows, sl], dv_ref[rows, sl] = dq, dk, dv
            dgb[g] = dgb[g] + jnp.where(lane == h, dg, 0.0) + jnp.where(lane == DN_HEADS + h, dbeta, 0.0)
        for g in range(G):
            dgb_ref[decay[g][0], :] = dgb[g]

    blk = pl.BlockSpec((CHUNK * G, 512), lambda n: (n, 0))
    half = pl.BlockSpec((CHUNK * G, 256), lambda n: (n, 0))
    nar = pl.BlockSpec((CHUNK * G, 128), lambda n: (n, 0))
    state = pl.BlockSpec((G, DN_HEADS, 128, 128), lambda n: (n, 0, 0, 0))
    return pl.pallas_call(
        body, name=name, grid=(ng,),
        in_specs=[blk, blk, blk, nar, pl.BlockSpec((G, 8, CHUNK), lambda n: (n, 0, 0)), half, state, state,
                  blk, blk, blk],
        out_specs=[blk, blk, blk, nar],
        out_shape=[jax.ShapeDtypeStruct((T, 512), F32)] * 3 + [jax.ShapeDtypeStruct((T, 128), F32)],
        compiler_params=_cparams(("parallel",)),
    )(qd, kd, vd, gb, gbt, tmi, ssave, dsn, do, dvn, vn)


def _gdn_prep_bwd1(dproj, proj, dqd, dkd, dvd, dgb, dkv_a, sconv_w, alog_v, dtb_v, name):
    T = proj.shape[0]
    tm = min(512, T)
    cur, prev, ab = _gdn_specs(T, tm)

    def body(dp_any, x_ref, xp_ref, ab_ref, dq_ref, dk_ref, dv_ref, dgb_ref, dkv_ref, w_ref, al_ref, dt_ref,
             o_ref, dpre_ref, st_ref, ext_ref):
        i = pl.program_id(0)
        pre = _gdn_conv(i, tm, x_ref, xp_ref, w_ref, ext_ref)
        y, dsl = _silu(pre), _dsilu(pre)
        for h in range(DN_HEADS):
            for base, g_ref, scale in ((0, dq_ref, 128 ** -0.5), (512, dk_ref, 1.0)):
                sl = slice(base + 128 * h, base + 128 * h + 128)
                xh = y[:, sl]
                r = lax.rsqrt(jnp.sum(xh * xh, axis=-1, keepdims=True) + EPS)
                xn = xh * r
                gy = g_ref[:, 128 * h:128 * h + 128]
                dpre_ref[:, sl] = (scale * r) * (gy - xn * jnp.sum(gy * xn, axis=-1, keepdims=True)) * dsl[:, sl]
        dpre_ref[:, 1024:] = dv_ref[...] * dsl[:, 1024:]
        abv, dgb = ab_ref[...], dgb_ref[...]
        lane = lax.broadcasted_iota(jnp.int32, (tm, 128), 1)
        na = -jnp.exp(al_ref[...])
        xs = abv + dt_ref[...]
        da = dgb * na * _sig(xs)
        b = _sig(abv)
        o_ref[:, :256] = dkv_ref[...].astype(BF16)
        o_ref[:, 256:] = jnp.where(lane < DN_HEADS, da,
                                   jnp.where(lane < 2 * DN_HEADS, dgb * b * (1.0 - b), 0.0)).astype(BF16)
        head = lane < DN_HEADS
        upd = jnp.concatenate([jnp.sum(jnp.where(head, dgb * na * _softplus(xs), 0.0), axis=0, keepdims=True),
                               jnp.sum(jnp.where(head, da, 0.0), axis=0, keepdims=True), jnp.zeros((6, 128), F32)],
                              axis=0)

        @pl.when(i == 0)
        def _():
            st_ref[...] = upd

        @pl.when(i > 0)
        def _():
            st_ref[...] += upd

    full = lambda s: pl.BlockSpec(s, lambda i: (0, 0))
    blk = pl.BlockSpec((tm, 512), lambda i: (i, 0))
    return pl.pallas_call(
        body, name=name, grid=(T // tm,),
        in_specs=[pl.BlockSpec(memory_space=pl.ANY), cur, prev, ab, blk, blk, blk,
                  pl.BlockSpec((tm, 128), lambda i: (i, 0)), pl.BlockSpec((tm, 256), lambda i: (i, 0)),
                  full((DN_K, QKV_C)), full((1, 128)), full((1, 128))],
        out_specs=[pl.BlockSpec((tm, 384), lambda i: (i, C_KA // 384)),
                   pl.BlockSpec((tm, QKV_C), lambda i: (i, 0)), full((8, 128))],
        out_shape=[jax.ShapeDtypeStruct(dproj.shape, BF16), jax.ShapeDtypeStruct((T, QKV_C), F32),
                   jax.ShapeDtypeStruct((8, 128), F32)],
        scratch_shapes=[pltpu.VMEM((tm + HALO_C, QKV_C), F32)],
        input_output_aliases={0: 0},
        compiler_params=_cparams(("arbitrary",)),
    )(dproj, proj, proj, proj, dqd, dkd, dvd, dgb, dkv_a, sconv_w, alog_v, dtb_v)


def _gdn_prep_bwd2(dproj, proj, dpre, sconv_w, name):
    T = proj.shape[0]
    tm = min(512, T)
    nt = T // tm
    r = tm // HALO_C
    cur, prev, _ = _gdn_specs(T, tm)

    def body(dp_any, x_ref, xp_ref, d_ref, dn_ref, w_ref, dx_ref, gw_ref, extx_ref, extd_ref):
        i = pl.program_id(0)
        extx_ref[:HALO_C] = jnp.where(i > 0, xp_ref[...], 0.0)
        extx_ref[HALO_C:] = x_ref[...]
        d = d_ref[...]
        extd_ref[:tm] = d
        extd_ref[tm:] = jnp.where(i < nt - 1, dn_ref[...], 0.0)
        dx = jnp.zeros((tm, QKV_C), F32)
        rows = []
        for k in range(DN_K):
            dx = dx + w_ref[k:k + 1, :] * extd_ref[pl.ds(DN_K - 1 - k, tm), :]
            rows.append(jnp.sum(d * extx_ref[pl.ds(HALO_C - DN_K + 1 + k, tm), :], axis=0, keepdims=True))
        rows.append(jnp.zeros((8 - DN_K, QKV_C), F32))
        gw = jnp.concatenate(rows, axis=0)
        dx_ref[...] = dx.astype(BF16)

        @pl.when(i == 0)
        def _():
            gw_ref[...] = gw

        @pl.when(i > 0)
        def _():
            gw_ref[...] += gw

    full = lambda s: pl.BlockSpec(s, lambda i: (0, 0))
    return pl.pallas_call(
        body, name=name, grid=(nt,),
        in_specs=[pl.BlockSpec(memory_space=pl.ANY), cur, prev, pl.BlockSpec((tm, QKV_C), lambda i: (i, 0)),
                  pl.BlockSpec((HALO_C, QKV_C), lambda i: (jnp.minimum((i + 1) * r, T // HALO_C - 1), 0)),
                  full((DN_K, QKV_C))],
        out_specs=[cur, full((8, QKV_C))],
        out_shape=[jax.ShapeDtypeStruct(dproj.shape, BF16), jax.ShapeDtypeStruct((8, QKV_C), F32)],
        scratch_shapes=[pltpu.VMEM((tm + HALO_C, QKV_C), F32), pltpu.VMEM((tm + HALO_C, QKV_C), F32)],
        input_output_aliases={0: 0},
        compiler_params=_cparams(("arbitrary",)),
    )(dproj, proj, proj, dpre, dpre, sconv_w)


def _merge_fwd(x, proj, ya, yb, yc, wa, wb, wc, wo, gate, name):
    T = x.shape[0]
    tm = min(256, T)

    def body(x_ref, mg_ref, ya_ref, yb_ref, yc_ref, wa_ref, wb_ref, wc_ref, wo_ref, gate_ref, o_ref):
        merged = (_sig(mg_ref[:, :D]) * _bdot(ya_ref[...], wa_ref[...])
                  + _sig(mg_ref[:, D:2 * D]) * _bdot(yb_ref[...], wb_ref[...])
                  + _sig(mg_ref[:, 2 * D:]) * _bdot(yc_ref[...], wc_ref[...]))
        o_ref[...] = x_ref[...] + gate_ref[...] * _bdot(merged, wo_ref[...])

    full = lambda s: pl.BlockSpec(s, lambda i: (0, 0))
    yb_ = pl.BlockSpec((tm, 512), lambda i: (i, 0))
    return pl.pallas_call(
        body, name=name, grid=(T // tm,),
        in_specs=[pl.BlockSpec((tm, D), lambda i: (i, 0)), pl.BlockSpec((tm, 3 * D), lambda i: (i, 0)), yb_, yb_, yb_,
                  full((512, D)), full((512, D)), full((512, D)), full((D, D)), full((1, D))],
        out_specs=pl.BlockSpec((tm, D), lambda i: (i, 0)),
        out_shape=jax.ShapeDtypeStruct((T, D), F32),
        compiler_params=_cparams(("parallel",)),
    )(x, proj, ya, yb, yc, wa, wb, wc, wo, _row(gate))


def _merge_bwd(dout, proj, ya, yb, yc, wa, wb, wc, wo, gate, name):
    T = dout.shape[0]
    tm = min(256, T)
    nt = T // tm

    def body(do_ref, mg_ref, ya_ref, yb_ref, yc_ref, wa_ref, wb_ref, wc_ref, wo_ref, gate_ref,
             dmg_ref, dya_ref, dyb_ref, dyc_ref, gwa_hbm, gwb_hbm, gwc_hbm, gwo_hbm, gg_ref,
             gwa_ref, gwb_ref, gwc_ref, gwo_ref):
        i = pl.program_id(0)

        @pl.when(i == 0)
        def _():
            for r in (gwa_ref, gwb_ref, gwc_ref, gwo_ref, gg_ref):
                r[...] = jnp.zeros_like(r)

        ys = (ya_ref[...], yb_ref[...], yc_ref[...])
        ws = (wa_ref, wb_ref, wc_ref)
        gs = tuple(_sig(mg_ref[:, j * D:(j + 1) * D]) for j in range(3))
        ps = tuple(_bdot(ys[j], ws[j][...]) for j in range(3))
        merged = gs[0] * ps[0] + gs[1] * ps[1] + gs[2] * ps[2]
        mo = _bdot(merged, wo_ref[...])
        do = do_ref[...]
        gg_ref[...] += jnp.sum(do * mo, axis=0, keepdims=True)
        dmo = do * gate_ref[...]
        dmerged = _bdot(dmo, wo_ref[...], NT)
        gwo_ref[...] += _bdot(merged, dmo, TN)
        for j, (dy_ref, gw_ref) in enumerate(((dya_ref, gwa_ref), (dyb_ref, gwb_ref), (dyc_ref, gwc_ref))):
            dp = dmerged * gs[j]
            dmg_ref[:, j * D:(j + 1) * D] = (dmerged * ps[j] * gs[j] * (1.0 - gs[j])).astype(BF16)
            dy_ref[...] = _bdot(dp, ws[j][...], NT)
            gw_ref[...] += _bdot(ys[j], dp, TN)

        @pl.when(i == nt - 1)
        def _():
            for src, dst in ((gwa_ref, gwa_hbm), (gwb_ref, gwb_hbm), (gwc_ref, gwc_hbm), (gwo_ref, gwo_hbm)):
                pltpu.sync_copy(src, dst)

    full = lambda s: pl.BlockSpec(s, lambda i: (0, 0))
    yb_ = pl.BlockSpec((tm, 512), lambda i: (i, 0))
    anyspec = pl.BlockSpec(memory_space=pl.ANY)
    return pl.pallas_call(
        body, name=name, grid=(nt,),
        in_specs=[pl.BlockSpec((tm, D), lambda i: (i, 0)), pl.BlockSpec((tm, 3 * D), lambda i: (i, 0)), yb_, yb_, yb_,
                  full((512, D)), full((512, D)), full((512, D)), full((D, D)), full((1, D))],
        out_specs=[pl.BlockSpec((tm, 3 * D), lambda i: (i, 0)), yb_, yb_, yb_, anyspec, anyspec, anyspec, anyspec,
                   full((1, D))],
        out_shape=[jax.ShapeDtypeStruct((T, NP), BF16)] + [jax.ShapeDtypeStruct((T, 512), F32)] * 3
        + [jax.ShapeDtypeStruct((512, D), F32)] * 3 + [jax.ShapeDtypeStruct((D, D), F32), jax.ShapeDtypeStruct((1, D), F32)],
        scratch_shapes=[pltpu.VMEM((512, D), F32)] * 3 + [pltpu.VMEM((D, D), F32)],
        compiler_params=_cparams(("arbitrary",)),
    )(dout, proj, ya, yb, yc, wa, wb, wc, wo, _row(gate))


def _loss_head(y, tgt, name):
    T = y.shape[0]
    tm = min(512, T)

    def body(y_ref, t_ref, dy_ref, l_ref):
        i = pl.program_id(0)
        diff = y_ref[...] - t_ref[...]
        dy_ref[...] = diff * (1.0 / D)
        part = jnp.sum(diff * diff, axis=0, keepdims=True)

        @pl.when(i == 0)
        def _():
            l_ref[...] = part

        @pl.when(i > 0)
        def _():
            l_ref[...] += part

    blk = pl.BlockSpec((tm, D), lambda i: (i, 0))
    return pl.pallas_call(
        body, name=name, grid=(T // tm,), in_specs=[blk, blk],
        out_specs=[blk, pl.BlockSpec((1, D), lambda i: (0, 0))],
        out_shape=[jax.ShapeDtypeStruct((T, D), F32), jax.ShapeDtypeStruct((1, D), F32)],
        compiler_params=_cparams(("arbitrary",)),
    )(y, tgt)


def _ada_fwd(c_all, w_ada, b_my, name):
    def body(c_ref, w_ref, b_ref, o_ref):
        sc = _silu(c_ref[...])
        for l in range(DEPTH):
            o_ref[l] = _bdot(sc, w_ref[l]) + b_ref[l:l + 1, :]

    return pl.pallas_call(body, name=name, out_shape=jax.ShapeDtypeStruct((DEPTH, N_DEV, w_ada.shape[2]), F32),
                          compiler_params=_cparams())(c_all, w_ada, b_my)


def _ada_bwd(c_all, dmod_my, name):
    def body(c_ref, d_ref, o_ref):
        sc = _silu(c_ref[...])
        for l in range(DEPTH):
            o_ref[l] = _bdot(sc, d_ref[l], TN)

    return pl.pallas_call(body, name=name, out_shape=jax.ShapeDtypeStruct((DEPTH, D, dmod_my.shape[2]), F32),
                          compiler_params=_cparams())(c_all, dmod_my)


def _adam_math(w, g, m, v):
    m = ADAM_B1 * m + (1.0 - ADAM_B1) * g
    v = ADAM_B2 * v + (1.0 - ADAM_B2) * (g * g)
    m_hat = m / (1.0 - ADAM_B1 ** ADAM_STEP)
    v_hat = v / (1.0 - ADAM_B2 ** ADAM_STEP)
    return -ADAM_LR * (m_hat / (jnp.sqrt(v_hat) + ADAM_EPS) + ADAM_WD * w), m, v


def _row_tile(rows, cap):
    best = rows
    for t in range(8, min(rows, cap) + 1, 8):
        if rows % t == 0:
            best = t
    return best if best <= cap else rows


def _adamw(w, g, m, v, name):
    R, C = w.shape
    tr = _row_tile(R, 256)

    def body(w_ref, g_ref, m_ref, v_ref, d_ref, mo_ref, vo_ref):
        d_ref[...], mo_ref[...], vo_ref[...] = _adam_math(w_ref[...], g_ref[...], m_ref[...], v_ref[...])

    blk = pl.BlockSpec((tr, C), lambda i: (i, 0))
    return pl.pallas_call(body, name=name, grid=(R // tr,), in_specs=[blk] * 4, out_specs=[blk] * 3,
                          out_shape=[jax.ShapeDtypeStruct((R, C), F32)] * 3,
                          compiler_params=_cparams(("parallel",)))(w, g, m, v)


def _sum_adamw_many(parts, ws, ms, vs, name):
    n = len(ws)

    def body(*refs):
        ins, outs = refs[:4 * n], refs[4 * n:]
        for i in range(n):
            g = ins[i][0]
            for j in range(1, N_DEV):
                g = g + ins[i][j]
            d, m, v = _adam_math(ins[n + i][...], g, ins[2 * n + i][...], ins[3 * n + i][...])
            outs[i][...], outs[n + i][...], outs[2 * n + i][...], outs[3 * n + i][...] = g, d, m, v

    shapes = [jax.ShapeDtypeStruct(w.shape, F32) for w in ws]
    out = pl.pallas_call(body, name=name, out_shape=shapes * 4, compiler_params=_cparams())(*parts, *ws, *ms, *vs)
    return out[:n], out[n:2 * n], out[2 * n:3 * n], out[3 * n:]


def _sum_adamw(parts0, parts1, w, m, v, name):
    P, R, C = parts0.shape
    tr = _row_tile(R, 128)
    nt = R // tr

    def body(p0_ref, p1_ref, w_ref, m_ref, v_ref, g_ref, d_ref, mo_ref, vo_ref):
        def emit(p_ref):
            g = p_ref[0].astype(F32)
            for j in range(1, P):
                g = g + p_ref[j].astype(F32)
            g_ref[...] = g
            d_ref[...], mo_ref[...], vo_ref[...] = _adam_math(w_ref[...], g, m_ref[...], v_ref[...])

        @pl.when(pl.program_id(0) == 0)
        def _():
            emit(p0_ref)

        @pl.when(pl.program_id(0) == 1)
        def _():
            emit(p1_ref)

    blk = pl.BlockSpec((tr, C), lambda l, i: (l * nt + i, 0))
    return pl.pallas_call(
        body, name=name, grid=(DEPTH, nt),
        in_specs=[pl.BlockSpec((P, tr, C), lambda l, i: (0, i * (1 - l) + (nt - 1) * l, 0)),
                  pl.BlockSpec((P, tr, C), lambda l, i: (0, i * l, 0)), blk, blk, blk],
        out_specs=[blk] * 4, out_shape=[jax.ShapeDtypeStruct((DEPTH * R, C), F32)] * 4,
        compiler_params=_cparams(("arbitrary", "arbitrary")))(parts0, parts1, w, m, v)


def _pair_sum(core, buf, recv, name):
    _, _, R, C = buf.shape
    tr = _row_tile(R, 128)

    def body(c_ref, a_ref, b_ref, o_ref):
        o_ref[...] = (a_ref[:, 0].astype(F32) + b_ref[...].astype(F32)).astype(BF16)

    return pl.pallas_call(
        body, name=name,
        grid_spec=pltpu.PrefetchScalarGridSpec(
            num_scalar_prefetch=1, grid=(R // tr,),
            in_specs=[pl.BlockSpec((4, 1, tr, C), lambda i, c: (0, c[0], i, 0)),
                      pl.BlockSpec((4, tr, C), lambda i, c: (0, i, 0))],
            out_specs=pl.BlockSpec((4, tr, C), lambda i, c: (0, i, 0))),
        out_shape=jax.ShapeDtypeStruct((4, R, C), BF16),
        compiler_params=_cparams(("parallel",)))(core, buf, recv)


SHARD_IN = D_IN // N_DEV


def _w_in_pieces():
    out, p = [], 0
    for a, b in _PAD_FROM:
        for j in range(N_DEV):
            lo, hi = max(a, SHARD_IN * j), min(b, SHARD_IN * (j + 1))
            if lo < hi:
                out.append((j, lo - SHARD_IN * j, hi - SHARD_IN * j, p + lo - a))
        p += b - a
    return out


def _assemble_w_in(gw, name):
    tr = 256
    nt = D // tr

    def body(x_ref, o_ref):
        for j, s0, s1, d0 in _w_in_pieces():
            o_ref[:, d0:d0 + s1 - s0] = x_ref[j, :, s0:s1]
        o_ref[:, D_IN:] = jnp.zeros((tr, NP - D_IN), gw.dtype)

    return pl.pallas_call(
        body, name=name, grid=(nt,),
        in_specs=[pl.BlockSpec((N_DEV, tr, SHARD_IN), lambda i: (0, i, 0))],
        out_specs=pl.BlockSpec((tr, NP), lambda i: (i, 0)),
        out_shape=jax.ShapeDtypeStruct((D, NP), gw.dtype),
        compiler_params=_cparams(("parallel",)))(gw)


def _split_w_in_grad(g, name):
    tr = 256

    def body(g_ref, o_ref):
        for j, s0, s1, d0 in _w_in_pieces():
            o_ref[j, :, s0:s1] = g_ref[:, d0:d0 + s1 - s0].astype(BF16)

    return pl.pallas_call(
        body, name=name, grid=(D // tr,),
        in_specs=[pl.BlockSpec((tr, NP), lambda i: (i, 0))],
        out_specs=pl.BlockSpec((N_DEV, tr, SHARD_IN), lambda i: (0, i, 0)),
        out_shape=jax.ShapeDtypeStruct((N_DEV, D, SHARD_IN), BF16),
        compiler_params=_cparams(("parallel",)))(g)


def _mesh_pos():
    return lax.axis_index("x"), lax.axis_index("y"), lax.axis_index("c")


def _launch(copies, peers, bufs, out_structs, sems, name, collective_id):
    n = len(bufs)
    if collective_id is None:
        anyspec = pl.BlockSpec(memory_space=pl.ANY)
        return pl.pallas_call(
            lambda *refs: copies(refs[:n], refs[n:n + len(out_structs)], *refs[n + len(out_structs):]),
            name=name, in_specs=[anyspec] * n, out_specs=[anyspec] * len(out_structs), out_shape=list(out_structs),
            scratch_shapes=list(sems))(*bufs)
    ins = [jax.new_ref(b, memory_space=pltpu.MemorySpace.HBM) for b in bufs]
    outs = [jax.empty_ref(s, memory_space=pltpu.MemorySpace.HBM) for s in out_structs]

    @pl.kernel(mesh=plsc.ScalarSubcoreMesh(axis_name="sequencer", num_cores=1), name=name, scratch_types=tuple(sems),
               compiler_params=pltpu.CompilerParams(collective_id=collective_id))
    def on_sequencer(*sem_refs):
        barrier = pltpu.get_barrier_semaphore()
        targets = peers()
        for p in targets:
            pl.semaphore_signal(barrier, inc=1, device_id=p, device_id_type=pl.DeviceIdType.MESH)
        pl.semaphore_wait(barrier, len(targets))
        copies(ins, outs, *sem_refs)

    on_sequencer()
    return [r[...] for r in outs]


def _all_gather(blocks, name, collective_id=None):
    n = len(blocks)

    def peers():
        x, y, c = _mesh_pos()
        return [(x, y, 1 - c), (1 - x, y, c), (x, 1 - y, c), (1 - x, 1 - y, c)]

    def copies(ins, outs, send_sems, recv_sems, local_sems):
        x, y, c = _mesh_pos()
        me, sibling = (x, y, c), (x, y, 1 - c)
        chips = [(1 - x, y), (x, 1 - y), (1 - x, 1 - y)]
        idx = lambda p: 4 * p[0] + 2 * p[1] + p[2]

        def copy(a, k, block, to, src=None):
            dst = outs[a].at[idx(block)]
            return pltpu.make_async_remote_copy(
                src_ref=dst if src is None else src, dst_ref=dst, send_sem=send_sems.at[a, k],
                recv_sem=recv_sems.at[a, k], device_id=to, device_id_type=pl.DeviceIdType.MESH)

        mine = [pltpu.make_async_copy(ins[a], outs[a].at[idx(me)], local_sems.at[a]) for a in range(n)]
        for cp in mine:
            cp.start()
        first = []
        for a in range(n):
            first.append(copy(a, 0, me, sibling, src=ins[a]))
            first += [copy(a, 1 + j, me, (*chip, c), src=ins[a]) for j, chip in enumerate(chips)]
        for cp in first:
            cp.start()
        passed = []
        for j, chip in enumerate(chips):
            for a in range(n):
                copy(a, 1 + j, (*chip, c), me).wait_recv()
                cp = copy(a, 4 + j, (*chip, c), sibling)
                cp.start()
                passed.append(cp)
        for a in range(n):
            copy(a, 0, sibling, me).wait_recv()
            for j, chip in enumerate(chips):
                copy(a, 4 + j, (*chip, 1 - c), me).wait_recv()
        for cp in first + passed:
            cp.wait_send()
        for cp in mine:
            cp.wait()

    return _launch(copies, peers, blocks, [jax.ShapeDtypeStruct((N_DEV,) + b.shape, b.dtype) for b in blocks],
                   [pltpu.SemaphoreType.DMA((n, 7)), pltpu.SemaphoreType.DMA((n, 7)), pltpu.SemaphoreType.DMA((n,))],
                   name, collective_id)


def _exchange_core(bufs, name, collective_id=None):
    n = len(bufs)

    def peers():
        x, y, c = _mesh_pos()
        return [(x, y, 1 - c)]

    def copies(ins, outs, send_sems, recv_sems):
        x, y, c = _mesh_pos()
        started = []
        for a in range(n):
            for q in range(4):
                cp = pltpu.make_async_remote_copy(
                    src_ref=ins[a].at[q, 1 - c], dst_ref=outs[a].at[q], send_sem=send_sems.at[a, q],
                    recv_sem=recv_sems.at[a, q], device_id=(x, y, 1 - c), device_id_type=pl.DeviceIdType.MESH)
                cp.start()
                started.append(cp)
        for cp in started:
            cp.wait()

    return _launch(copies, peers, bufs, [jax.ShapeDtypeStruct((4,) + b.shape[2:], b.dtype) for b in bufs],
                   [pltpu.SemaphoreType.DMA((n, 4)), pltpu.SemaphoreType.DMA((n, 4))], name, collective_id)


def _exchange_chips(bufs, name, collective_id=None):
    n = len(bufs)

    def peers():
        x, y, c = _mesh_pos()
        return [(1 - x, y, c), (x, 1 - y, c), (1 - x, 1 - y, c)]

    def copies(ins, outs, send_sems, recv_sems, local_sems):
        x, y, c = _mesh_pos()
        chip = 2 * x + y
        local = [pltpu.make_async_copy(ins[a].at[chip], outs[a].at[chip], local_sems.at[a]) for a in range(n)]
        for cp in local:
            cp.start()
        started = []
        for k in range(1, 4):
            px = 1 - x if k & 2 else x
            py = 1 - y if k & 1 else y
            for a in range(n):
                cp = pltpu.make_async_remote_copy(
                    src_ref=ins[a].at[2 * px + py], dst_ref=outs[a].at[chip], send_sem=send_sems.at[a, k - 1],
                    recv_sem=recv_sems.at[a, k - 1], device_id=(px, py, c), device_id_type=pl.DeviceIdType.MESH)
                cp.start()
                started.append(cp)
        for cp in started:
            cp.wait()
        for cp in local:
            cp.wait()

    return _launch(copies, peers, bufs, [jax.ShapeDtypeStruct(b.shape, b.dtype) for b in bufs],
                   [pltpu.SemaphoreType.DMA((n, 3)), pltpu.SemaphoreType.DMA((n, 3)), pltpu.SemaphoreType.DMA((n,))],
                   name, collective_id)


_SMALL = ("b_ada", "norm_g", "q_norm_g", "k_norm_g", "sinks", "dw_b", "ln_g", "ln_b", "pw2_b", "a_log", "dt_bias",
          "dn_norm_g", "dw_w", "sconv_w")


def _lane4(v):
    return jnp.pad(v, (0, 124)).reshape(1, 128)


def kernel(x, c, w_ada, b_ada, norm_g, w_in, q_norm_g, k_norm_g, sinks, dw_w, dw_b, ln_g, ln_b, pw2_w, pw2_b, sconv_w, a_log, dt_bias, dn_norm_g, w_proj_a, w_proj_b, w_proj_c, w_out, loss_target, m_w_ada, m_b_ada, m_norm_g, m_w_in, m_q_norm_g, m_k_norm_g, m_sinks, m_dw_w, m_dw_b, m_ln_g, m_ln_b, m_pw2_w, m_pw2_b, m_sconv_w, m_a_log, m_dt_bias, m_dn_norm_g, m_w_proj_a, m_w_proj_b, m_w_proj_c, m_w_out, v_w_ada, v_b_ada, v_norm_g, v_w_in, v_q_norm_g, v_k_norm_g, v_sinks, v_dw_w, v_dw_b, v_ln_g, v_ln_b, v_pw2_w, v_pw2_b, v_sconv_w, v_a_log, v_dt_bias, v_dn_norm_g, v_w_proj_a, v_w_proj_b, v_w_proj_c, v_w_out):
    T = x.shape[1]
    nc = T // CHUNK
    xi, yi, ci = _mesh_pos()
    me = 4 * xi + 2 * yi + ci
    big_w = (w_in, pw2_w, w_proj_a, w_proj_b, w_proj_c, w_out)
    big_m = (m_w_in, m_pw2_w, m_w_proj_a, m_w_proj_b, m_w_proj_c, m_w_out)
    big_v = (v_w_in, v_pw2_w, v_w_proj_a, v_w_proj_b, v_w_proj_c, v_w_out)

    ada_cols = w_ada.shape[2]
    dw_cols, sc_cols = dw_w.shape[2], sconv_w.shape[2]
    flat2 = lambda a: a.reshape(-1, a.shape[-1])
    big16 = [a.astype(BF16) for a in big_w]
    wp, pw2_f, wa_f, wb_f, wc_f, wo_f = [], [], [], [], [], []
    for l in range(DEPTH):
        extra = [c, dw_w, sconv_w] if l == 0 else []
        got = _all_gather([a[l] for a in big16] + extra, f"gather_weights{l}", collective_id=l)
        gw_in, gpw2, gpa, gpb, gpc, gwo = got[:6]
        if l == 0:
            c_all, gdw, gsc = got[6].reshape(N_DEV, D), got[7], got[8]
        wp.append(_assemble_w_in(gw_in, f"assemble_w_in{l}"))
        pw2_f.append(gpw2.reshape(512, 512))
        for dst, g in ((wa_f, gpa), (wb_f, gpb), (wc_f, gpc)):
            dst.append(g.transpose(1, 0, 2).reshape(512, D))
        wo_f.append(gwo.reshape(D, D))
    dw_f = gdw.transpose(1, 2, 0, 3).reshape(DEPTH, CONV_K, 512)
    sc_f = gsc.transpose(1, 2, 0, 3).reshape(DEPTH, DN_K, QKV_C)

    b_my = lax.dynamic_slice(b_ada, (0, me * ada_cols), (DEPTH, ada_cols))
    mod_part = _ada_fwd(c_all, w_ada, b_my, "ada_fwd")
    (gmod,) = _all_gather([mod_part.reshape(-1, 128)], "gather_mod")
    mod_all = gmod.reshape(N_DEV, DEPTH, N_DEV, ada_cols).transpose(1, 2, 0, 3).reshape(DEPTH, N_DEV, 3 * D)
    mod = lax.dynamic_index_in_dim(mod_all, me, axis=1, keepdims=False)
    shift, scale, gate = mod[:, :D], mod[:, D:2 * D], mod[:, 2 * D:]

    xs, saved = [x[0]], []
    for l in range(DEPTH):
        xl = xs[-1]
        h = _norm_fwd(xl, norm_g[l], scale[l], shift[l], f"norm_fwd{l}")
        proj = _mm(h, wp[l], tm=min(1024, T), tn=1152, tk=D, name=f"in_proj{l}")
        ya = _attn_fwd(proj, q_norm_g[l], k_norm_g[l], sinks[l], f"attn_fwd{l}")
        yb = _conf_fwd(proj, dw_f[l], dw_b[l], ln_g[l], ln_b[l], pw2_f[l], pw2_b[l], f"conf_fwd{l}")
        alv, dtv, dng = _lane4(a_log[l]), _lane4(dt_bias[l]), _row(dn_norm_g[l])
        qd, kd, vd, gb = _gdn_prep_fwd(proj, sc_f[l], alv, dtv, f"gdn_prep_fwd{l}")
        gbt = gb[:, :8].reshape(nc, CHUNK, 8).transpose(0, 2, 1)
        u, w, qe, ke, pm, tmi, eg = _gdn_chunk_fwd(qd, kd, vd, gb, gbt, f"gdn_chunk_fwd{l}")
        yc, o, vn, ss = _gdn_scan_fwd(u, w, qe, ke, pm, eg, proj, dng, f"gdn_scan_fwd{l}")
        xs.append(_merge_fwd(xl, proj, ya, yb, yc, wa_f[l], wb_f[l], wc_f[l], wo_f[l], gate[l], f"merge_fwd{l}"))
        saved.append((h, proj, ya, yb, yc, qd, kd, vd, gb, gbt, ss, alv, dtv, dng, w, qe, ke, pm, tmi, eg, o, vn))

    dout, lsum = _loss_head(xs[-1], loss_target[0], "loss_head")
    loss = lax.psum(0.5 * jnp.sum(lsum) / D, ("x", "y", "c"))

    small = {name: [None] * DEPTH for name in _SMALL}
    big_parts = [None] * DEPTH
    core = jnp.reshape(ci, (1,)).astype(jnp.int32)
    for l in reversed(range(DEPTH)):
        h, proj, ya, yb, yc, qd, kd, vd, gb, gbt, ss, alv, dtv, dng, w, qe, ke, pm, tmi, eg, o, vn = saved[l]
        dproj, dya, dyb, dyc, g_wa, g_wb, g_wc, g_wo, g_gate = _merge_bwd(
            dout, proj, ya, yb, yc, wa_f[l], wb_f[l], wc_f[l], wo_f[l], gate[l], f"merge_bwd{l}")
        dproj, dkv_a, g_q, g_k, g_s = _attn_bwd(dproj, proj, dya, q_norm_g[l], k_norm_g[l], sinks[l], f"attn_bwd{l}")
        dproj, du1, g_pw2, st_b = _conf_bwd1(dproj, proj, dyb, dw_f[l], dw_b[l], ln_g[l], ln_b[l], pw2_f[l], pw2_b[l],
                                             f"conf_bwd_a{l}")
        dproj, g_dw = _conf_bwd2(dproj, proj, du1, dw_f[l], f"conf_bwd_b{l}")
        dproj, do, dvn, dsn, g_dn = _gdn_scan_bwd(dproj, w, qe, ke, pm, eg, o, proj, dyc, dng, f"gdn_scan_bwd{l}")
        dqd, dkd, dvd, dgb = _gdn_chunk_grad(qd, kd, vd, gb, gbt, tmi, ss, dsn, do, dvn, vn, f"gdn_chunk_bwd{l}")
        dproj, dpre, st_c = _gdn_prep_bwd1(dproj, proj, dqd, dkd, dvd, dgb, dkv_a, sc_f[l], alv, dtv,
                                           f"gdn_prep_bwd_a{l}")
        dproj, g_sc = _gdn_prep_bwd2(dproj, proj, dpre, sc_f[l], f"gdn_prep_bwd_b{l}")
        dh = _mm(dproj, wp[l], tb=True, tm=min(1024, T), tn=D, tk=1152, name=f"d_h{l}")
        g_wp = _mm(h, dproj, ta=True, tm=D, tn=1152, tk=min(1024, T), name=f"d_w_in{l}")
        dout, st_n = _norm_bwd(dh, xs[l], dout, norm_g[l], scale[l], f"norm_bwd{l}")
        by_dest = [_split_w_in_grad(g_wp, f"split_w_in_grad{l}"), g_pw2.reshape(N_DEV, -1, 512).astype(BF16)]
        by_dest += [g.reshape(512, N_DEV, -1).transpose(1, 0, 2).astype(BF16) for g in (g_wa, g_wb, g_wc)]
        by_dest.append(g_wo.reshape(N_DEV, -1, D).astype(BF16))
        by_dest = [b.reshape(4, 2, -1, b.shape[-1]) for b in by_dest]
        from_sibling = _exchange_core(by_dest, f"exchange_grads_core{l}", collective_id=2 + 2 * l)
        chip_sums = [_pair_sum(core, b, r, f"pair_sum{l}_{i}") for i, (b, r) in enumerate(zip(by_dest, from_sibling))]
        big_parts[l] = _exchange_chips(chip_sums, f"exchange_grads_chips{l}", collective_id=3 + 2 * l)
        for name, g in (("b_ada", jnp.concatenate([st_n[0], st_n[1], g_gate[0]])), ("norm_g", st_n[2]),
                        ("q_norm_g", g_q.reshape(ATT_HEADS, ATT_HD).sum(0)), ("k_norm_g", g_k.reshape(2, ATT_HD).sum(0)),
                        ("sinks", g_s[0]), ("dw_b", st_b[3]),
                        ("ln_g", st_b[1]), ("ln_b", st_b[2]), ("pw2_b", st_b[0]), ("a_log", st_c[0, :4]),
                        ("dt_bias", st_c[1, :4]), ("dn_norm_g", g_dn[0]), ("dw_w", g_dw[:CONV_K]),
                        ("sconv_w", g_sc[:DN_K])):
            small[name][l] = g
    grad_x = dout[None]

    names = list(_SMALL)
    gparts = dict(zip(names, _all_gather([jnp.stack(small[n]) for n in names], "gather_small_grads")))
    dmod_my = lax.dynamic_slice(gparts["b_ada"], (0, 0, me * ada_cols), (N_DEV, DEPTH, ada_cols)).transpose(1, 0, 2)
    g_w_ada = _ada_bwd(c_all, dmod_my, "ada_bwd")
    gparts["dw_w"] = lax.dynamic_slice(gparts["dw_w"], (0, 0, 0, me * dw_cols), (N_DEV, DEPTH, CONV_K, dw_cols))
    gparts["sconv_w"] = lax.dynamic_slice(gparts["sconv_w"], (0, 0, 0, me * sc_cols), (N_DEV, DEPTH, DN_K, sc_cols))
    env = dict(b_ada=(b_ada, m_b_ada, v_b_ada), norm_g=(norm_g, m_norm_g, v_norm_g),
               q_norm_g=(q_norm_g, m_q_norm_g, v_q_norm_g), k_norm_g=(k_norm_g, m_k_norm_g, v_k_norm_g),
               sinks=(sinks, m_sinks, v_sinks), dw_b=(dw_b, m_dw_b, v_dw_b), ln_g=(ln_g, m_ln_g, v_ln_g),
               ln_b=(ln_b, m_ln_b, v_ln_b), pw2_b=(pw2_b, m_pw2_b, v_pw2_b), a_log=(a_log, m_a_log, v_a_log),
               dt_bias=(dt_bias, m_dt_bias, v_dt_bias), dn_norm_g=(dn_norm_g, m_dn_norm_g, v_dn_norm_g),
               dw_w=(dw_w, m_dw_w, v_dw_w), sconv_w=(sconv_w, m_sconv_w, v_sconv_w))
    upd = _sum_adamw_many([gparts[n] for n in names], [env[n][0] for n in names], [env[n][1] for n in names],
                          [env[n][2] for n in names], "sum_adamw_small")
    g_small, d_small, m_small, v_small = (dict(zip(names, u)) for u in upd)

    d_ada, nm_ada, nv_ada = (u.reshape(w_ada.shape) for u in
                             _adamw(flat2(w_ada), flat2(g_w_ada), flat2(m_w_ada), flat2(v_w_ada), "adamw_w_ada"))

    res = [_sum_adamw(p0, p1, flat2(w), flat2(m), flat2(v), f"sum_adamw{i}")
           for i, (p0, p1, w, m, v) in enumerate(zip(big_parts[0], big_parts[1], big_w, big_m, big_v))]
    g_big, d_big, m_big, v_big = ([r[k].reshape(w.shape) for r, w in zip(res, big_w)] for k in range(4))

    order = ("w_ada", "b_ada", "norm_g", "w_in", "q_norm_g", "k_norm_g", "sinks", "dw_w", "dw_b", "ln_g", "ln_b",
             "pw2_w", "pw2_b", "sconv_w", "a_log", "dt_bias", "dn_norm_g", "w_proj_a", "w_proj_b", "w_proj_c", "w_out")
    big_names = ("w_in", "pw2_w", "w_proj_a", "w_proj_b", "w_proj_c", "w_out")

    def pick(kind):
        src_small = (g_small, d_small, m_small, v_small)[kind]
        src_big = (g_big, d_big, m_big, v_big)[kind]
        src_ada = (g_w_ada, d_ada, nm_ada, nv_ada)[kind]
        return [src_ada if n == "w_ada" else src_big[big_names.index(n)] if n in big_names else src_small[n]
                for n in order]

    return (loss, grad_x, *pick(0), *pick(1), *pick(2), *pick(3))
```

```python
import functools
import math

import jax
import jax.numpy as jnp
import numpy as np
from jax import lax
from jax.experimental import pallas as pl
from jax.experimental.pallas import tpu as pltpu
from jax.experimental.pallas import tpu_sc as plsc

F32 = jnp.float32
BF16 = jnp.bfloat16
HI = lax.Precision.HIGHEST

N_DEV = 8
D = 1024
DEPTH = 2
EPS = 1e-6
NEG_INF = -1e30
WINDOW = 128
ATT_HEADS = 8
ATT_HD = 64
CONV_K = 31
DN_HEADS = 4
DN_K = 4
CHUNK = 64
D_IN = 7944
VMEM_LIMIT = 56 * 1024 * 1024

C_MG, C_QA, C_ZA, C_ZB, C_QC, C_KC, C_VC, C_GV, C_GG, C_ZC, C_KA, C_VA, C_AB, NP = (
    0, 3072, 3584, 4096, 4608, 5120, 5632, 6144, 6656, 7168, 7680, 7808, 7936, 8064)
_PAD_FROM = ((4872, 7944), (0, 512), (768, 1280), (2304, 2816), (2816, 4352), (1280, 2304), (4360, 4872),
             (512, 768), (4352, 4360))

ALIBI = tuple(float(2.0 ** (-8.0 * (h + 1) / ATT_HEADS)) for h in range(ATT_HEADS))

ADAM_LR, ADAM_B1, ADAM_B2, ADAM_EPS, ADAM_WD, ADAM_STEP = 0.001, 0.9, 0.999, 1e-08, 0.01, 10


def _cparams(sem=None):
    return pltpu.CompilerParams(dimension_semantics=sem, vmem_limit_bytes=VMEM_LIMIT)


def _sig(x):
    return jax.nn.sigmoid(x)


def _silu(x):
    return x * _sig(x)


def _dsilu(x):
    s = _sig(x)
    return s * (1.0 + x * (1.0 - s))


def _dot(a, b, dims=((1,), (0,)), precision=None):
    return lax.dot_general(a, b, (dims, ((), ())), preferred_element_type=F32, precision=precision)


def _bdot(a, b, dims=((1,), (0,))):
    return _dot(a.astype(BF16), b.astype(BF16), dims)


NN, NT, TN = ((1,), (0,)), ((1,), (1,)), ((0,), (0,))


def _row(v):
    return v.reshape(1, -1)


def _mm(a, b, *, ta=False, tb=False, tm, tn, tk, name):
    M, K = (a.shape[1], a.shape[0]) if ta else a.shape
    N = b.shape[0] if tb else b.shape[1]
    assert M % tm == 0 and N % tn == 0 and K % tk == 0, (M, N, K, tm, tn, tk)
    nk = K // tk
    dims = ((0 if ta else 1,), (1 if tb else 0,))

    def body(a_ref, b_ref, o_ref):
        k = pl.program_id(2)
        part = _bdot(a_ref[...], b_ref[...], dims)

        @pl.when(k == 0)
        def _():
            o_ref[...] = part

        @pl.when(k > 0)
        def _():
            o_ref[...] += part

    a_spec = pl.BlockSpec((tk, tm), lambda i, j, k: (k, i)) if ta else pl.BlockSpec((tm, tk), lambda i, j, k: (i, k))
    b_spec = pl.BlockSpec((tn, tk), lambda i, j, k: (j, k)) if tb else pl.BlockSpec((tk, tn), lambda i, j, k: (k, j))
    return pl.pallas_call(
        body, name=name, grid=(M // tm, N // tn, nk),
        in_specs=[a_spec, b_spec], out_specs=pl.BlockSpec((tm, tn), lambda i, j, k: (i, j)),
        out_shape=jax.ShapeDtypeStruct((M, N), F32),
        compiler_params=_cparams(("parallel", "parallel", "arbitrary")),
    )(a, b)


def _norm_fwd(x, norm_g, scale, shift, name):
    T = x.shape[0]
    tm = min(512, T)

    def body(x_ref, g_ref, sc_ref, sh_ref, h_ref):
        xv = x_ref[...]
        r = lax.rsqrt(jnp.mean(xv * xv, axis=-1, keepdims=True) + EPS)
        h_ref[...] = ((xv * r) * g_ref[...] * (1.0 + sc_ref[...]) + sh_ref[...]).astype(BF16)

    vec = pl.BlockSpec((1, D), lambda i: (0, 0))
    return pl.pallas_call(
        body, name=name, grid=(T // tm,),
        in_specs=[pl.BlockSpec((tm, D), lambda i: (i, 0)), vec, vec, vec],
        out_specs=pl.BlockSpec((tm, D), lambda i: (i, 0)),
        out_shape=jax.ShapeDtypeStruct((T, D), BF16),
        compiler_params=_cparams(("parallel",)),
    )(x, _row(norm_g), _row(scale), _row(shift))


def _norm_bwd(dh, x, dres, norm_g, scale, name):
    T = x.shape[0]
    tm = min(512, T)

    def body(dh_ref, x_ref, dr_ref, g_ref, sc_ref, dx_ref, st_ref):
        i = pl.program_id(0)
        xv, dhv = x_ref[...], dh_ref[...]
        r = lax.rsqrt(jnp.mean(xv * xv, axis=-1, keepdims=True) + EPS)
        xh = xv * r
        g, s1 = g_ref[...], 1.0 + sc_ref[...]
        dxh = dhv * (g * s1)
        dx_ref[...] = dr_ref[...] + r * (dxh - xh * jnp.mean(dxh * xh, axis=-1, keepdims=True))
        dhx = dhv * xh
        upd = jnp.concatenate([jnp.sum(dhv, axis=0, keepdims=True), jnp.sum(dhx * g, axis=0, keepdims=True),
                               jnp.sum(dhx * s1, axis=0, keepdims=True), jnp.zeros((5, D), F32)], axis=0)

        @pl.when(i == 0)
        def _():
            st_ref[...] = upd

        @pl.when(i > 0)
        def _():
            st_ref[...] += upd

    vec = pl.BlockSpec((1, D), lambda i: (0, 0))
    blk = pl.BlockSpec((tm, D), lambda i: (i, 0))
    return pl.pallas_call(
        body, name=name, grid=(T // tm,),
        in_specs=[blk, blk, blk, vec, vec],
        out_specs=[blk, pl.BlockSpec((8, D), lambda i: (0, 0))],
        out_shape=[jax.ShapeDtypeStruct((T, D), F32), jax.ShapeDtypeStruct((8, D), F32)],
        compiler_params=_cparams(("arbitrary",)),
    )(dh, x, dres, _row(norm_g), _row(scale))


def _rms(x, g):
    r = lax.rsqrt(jnp.mean(x * x, axis=-1, keepdims=True) + EPS)
    return x * r, r


def _head_mean_matrix():
    head = np.arange(ATT_HEADS * ATT_HD) // ATT_HD
    return jnp.asarray((head[:, None] == head[None, :]) * (1.0 / ATT_HD), BF16)


def _head_rms(x, hm):
    r = lax.rsqrt(_dot_exact(x * x, hm) + EPS)
    return x * r, r


def _head_rms_bwd(dy, xh, r, g, hm):
    dxh = dy * g
    return r * (dxh - xh * _dot_exact(dxh * xh, hm)), dy * xh


def _attn_mask(n):
    qi = lax.broadcasted_iota(jnp.int32, (WINDOW, 2 * WINDOW), 0)
    kj = lax.broadcasted_iota(jnp.int32, (WINDOW, 2 * WINDOW), 1)
    dist = qi + WINDOW - kj
    valid = (dist >= 0) & (dist < WINDOW) & ((n > 0) | (kj >= WINDOW))
    return valid, dist.astype(F32)


def _attn_probs(s, h, sink, valid, distf):
    s = s - ALIBI[h] * distf
    s = jnp.where(valid, s, NEG_INF)
    m = jnp.maximum(jnp.max(s, axis=-1, keepdims=True), sink)
    p = jnp.exp(s - m)
    es = jnp.exp(sink - m)
    den = jnp.sum(p, axis=-1, keepdims=True) + es
    return p / den, es / den


def _attn_fwd(proj, q_norm_g, k_norm_g, sinks, name):
    T = proj.shape[0]
    nb = T // WINDOW

    def body(sink_ref, q_ref, z_ref, kc_ref, kp_ref, vc_ref, vp_ref, qg_ref, kg_ref, hm_ref, o_ref):
        n = pl.program_id(0)
        valid, distf = _attn_mask(n)
        k2 = jnp.concatenate([kp_ref[...], kc_ref[...]], axis=0)
        v2 = jnp.concatenate([vp_ref[...], vc_ref[...]], axis=0).astype(BF16)
        kn = (_head_rms(k2, hm_ref[:128, :128])[0] * kg_ref[...]).astype(BF16)
        qn = ((_head_rms(q_ref[...], hm_ref[...])[0] * qg_ref[...]) * (ATT_HD ** -0.5)).astype(BF16)

        def head(h):
            sl, gsl = slice(64 * h, 64 * h + 64), slice(64 * (h // 4), 64 * (h // 4) + 64)
            s = _dot(qn[:, sl], kn[:, gsl], NT)
            yield
            p, _ = _attn_probs(s, h, sink_ref[h], valid, distf)
            o_ref[:, sl] = _dot(p.astype(BF16), v2[:, gsl])
            yield

        _lockstep([head(h) for h in range(ATT_HEADS)])
        o_ref[...] = o_ref[...] * _silu(z_ref[...])

    prev = lambda n: jnp.maximum(n - 1, 0)
    return pl.pallas_call(
        body, name=name, grid=(nb,),
        in_specs=[pl.BlockSpec(memory_space=pltpu.SMEM),
                  pl.BlockSpec((WINDOW, 512), lambda n: (n, C_QA // 512)),
                  pl.BlockSpec((WINDOW, 512), lambda n: (n, C_ZA // 512)),
                  pl.BlockSpec((WINDOW, 128), lambda n: (n, C_KA // 128)),
                  pl.BlockSpec((WINDOW, 128), lambda n: (prev(n), C_KA // 128)),
                  pl.BlockSpec((WINDOW, 128), lambda n: (n, C_VA // 128)),
                  pl.BlockSpec((WINDOW, 128), lambda n: (prev(n), C_VA // 128)),
                  pl.BlockSpec((1, 512), lambda n: (0, 0)), pl.BlockSpec((1, 128), lambda n: (0, 0)),
                  pl.BlockSpec((512, 512), lambda n: (0, 0))],
        out_specs=pl.BlockSpec((WINDOW, 512), lambda n: (n, 0)),
        out_shape=jax.ShapeDtypeStruct((T, 512), F32),
        compiler_params=_cparams(("parallel",)),
    )(sinks, proj, proj, proj, proj, proj, proj, _row(jnp.tile(q_norm_g, ATT_HEADS)), _row(jnp.tile(k_norm_g, 2)),
      _head_mean_matrix())


def _rms_bwd(dy, xh, r, g):
    dxh = dy * g
    return r * (dxh - xh * jnp.mean(dxh * xh, axis=-1, keepdims=True)), dy * xh


def _attn_bwd(dproj, proj, dya, q_norm_g, k_norm_g, sinks, name):
    T = proj.shape[0]
    nb = T // WINDOW

    def body(sink_ref, dp_any, q_ref, z_ref, kc_ref, kp_ref, vc_ref, vp_ref, dy_ref, qg_ref, kg_ref, hm_ref,
             dqz_ref, dkv_ref, gq_ref, gk_ref, gs_ref, ck_ref, cv_ref, o_sc, dq_sc):
        n = pl.program_id(0)

        @pl.when(n == 0)
        def _():
            gq_ref[...] = jnp.zeros_like(gq_ref)
            gk_ref[...] = jnp.zeros_like(gk_ref)
            gs_ref[...] = jnp.zeros_like(gs_ref)
            ck_ref[...] = jnp.zeros_like(ck_ref)
            cv_ref[...] = jnp.zeros_like(cv_ref)

        lane8 = lax.broadcasted_iota(jnp.int32, (1, 8), 1)

        @pl.when(n < nb)
        def _():
            valid, distf = _attn_mask(n)
            k2 = jnp.concatenate([kp_ref[...], kc_ref[...]], axis=0)
            v2 = jnp.concatenate([vp_ref[...], vc_ref[...]], axis=0).astype(BF16)
            kn = (_head_rms(k2, hm_ref[:128, :128])[0] * kg_ref[...]).astype(BF16)
            qh, qr = _head_rms(q_ref[...], hm_ref[...])
            qn = ((qh * qg_ref[...]) * (ATT_HD ** -0.5)).astype(BF16)
            zs = z_ref[...]
            do_all = dy_ref[...] * _silu(zs)
            dob_all = do_all.astype(BF16)

            def head(h):
                sl, gsl = slice(64 * h, 64 * h + 64), slice(64 * (h // 4), 64 * (h // 4) + 64)
                s = _dot(qn[:, sl], kn[:, gsl], NT)
                dpm = _dot(dob_all[:, sl], v2[:, gsl], NT)
                yield
                p, ps = _attn_probs(s, h, sink_ref[h], valid, distf)
                pb = p.astype(BF16)
                o_sc[:, sl] = _dot(pb, v2[:, gsl])
                dvg = _dot(pb, dob_all[:, sl], TN)
                delta = jnp.sum(p * dpm, axis=-1, keepdims=True)
                ds = (p * (dpm - delta)).astype(BF16)
                gs = jnp.where(lane8 == h, -jnp.sum(ps * delta, axis=0, keepdims=True), 0.0)
                yield
                dkn = _dot(ds, qn[:, sl], TN)
                dq_sc[:, sl] = _dot(ds, kn[:, gsl])
                yield
                return dkn, dvg, gs

            res = _lockstep([head(h) for h in range(ATT_HEADS)])
            dqz_ref[:, 512:] = (dy_ref[...] * o_sc[...] * _dsilu(zs)).astype(BF16)
            dq, gq = _head_rms_bwd(dq_sc[...] * (ATT_HD ** -0.5), qh, qr, qg_ref[...], hm_ref[...])
            dqz_ref[:, :512] = dq.astype(BF16)
            gq_acc = jnp.sum(gq, axis=0, keepdims=True)
            gs_acc = sum(r[2] for r in res[1:]) + res[0][2]
            for g in range(2):
                dkn = (res[4 * g][0] + res[4 * g + 1][0]) + (res[4 * g + 2][0] + res[4 * g + 3][0])
                dvg = (res[4 * g][1] + res[4 * g + 1][1]) + (res[4 * g + 2][1] + res[4 * g + 3][1])
                ksl = slice(64 * g, 64 * g + 64)
                vsl = slice(128 + 64 * g, 128 + 64 * g + 64)
                dkv_ref[:, ksl] = ck_ref[:, ksl] + dkn[:WINDOW]
                dkv_ref[:, vsl] = cv_ref[:, ksl] + dvg[:WINDOW]
                ck_ref[:, ksl] = dkn[WINDOW:]
                cv_ref[:, ksl] = dvg[WINDOW:]
            gq_ref[...] += gq_acc
            gs_ref[...] += gs_acc

        @pl.when(n == nb)
        def _():
            dkv_ref[:, :128] = ck_ref[...]
            dkv_ref[:, 128:] = cv_ref[...]

        @pl.when(n > 0)
        def _():
            hm = hm_ref[:128, :128]
            kh, kr = _head_rms(kp_ref[...], hm)
            dk, gk = _head_rms_bwd(dkv_ref[:, :128], kh, kr, kg_ref[...], hm)
            dkv_ref[:, :128] = dk
            gk_ref[...] += jnp.sum(gk, axis=0, keepdims=True)

    cur = lambda n: jnp.minimum(n, nb - 1)
    prev = lambda n: jnp.maximum(n - 1, 0)
    small = lambda w: pl.BlockSpec((1, w), lambda n: (0, 0))
    return pl.pallas_call(
        body, name=name, grid=(nb + 1,),
        in_specs=[pl.BlockSpec(memory_space=pltpu.SMEM), pl.BlockSpec(memory_space=pl.ANY),
                  pl.BlockSpec((WINDOW, 512), lambda n: (cur(n), C_QA // 512)),
                  pl.BlockSpec((WINDOW, 512), lambda n: (cur(n), C_ZA // 512)),
                  pl.BlockSpec((WINDOW, 128), lambda n: (cur(n), C_KA // 128)),
                  pl.BlockSpec((WINDOW, 128), lambda n: (prev(n), C_KA // 128)),
                  pl.BlockSpec((WINDOW, 128), lambda n: (cur(n), C_VA // 128)),
                  pl.BlockSpec((WINDOW, 128), lambda n: (prev(n), C_VA // 128)),
                  pl.BlockSpec((WINDOW, 512), lambda n: (cur(n), 0)),
                  small(512), small(128), pl.BlockSpec((512, 512), lambda n: (0, 0))],
        out_specs=[pl.BlockSpec((WINDOW, 1024), lambda n: (cur(n), C_QA // 1024)),
                   pl.BlockSpec((WINDOW, 256), lambda n: (prev(n), 0)),
                   small(512), small(128), small(8)],
        out_shape=[jax.ShapeDtypeStruct(dproj.shape, BF16), jax.ShapeDtypeStruct((T, 256), F32),
                   jax.ShapeDtypeStruct((1, 512), F32), jax.ShapeDtypeStruct((1, 128), F32),
                   jax.ShapeDtypeStruct((1, 8), F32)],
        scratch_shapes=[pltpu.VMEM((WINDOW, 128), F32), pltpu.VMEM((WINDOW, 128), F32),
                        pltpu.VMEM((WINDOW, 512), F32), pltpu.VMEM((WINDOW, 512), F32)],
        input_output_aliases={1: 0},
        compiler_params=_cparams(("arbitrary",)),
    )(sinks, dproj, proj, proj, proj, proj, proj, proj, dya, _row(jnp.tile(q_norm_g, ATT_HEADS)),
      _row(jnp.tile(k_norm_g, 2)), _head_mean_matrix())


HALO_B = 32


def _conf_specs(T, tm):
    r = tm // HALO_B
    cur = lambda c: pl.BlockSpec((tm, 512), lambda i: (i, c // 512))
    prev = lambda c: pl.BlockSpec((HALO_B, 512), lambda i: (jnp.maximum(i * r - 1, 0), c // 512))
    return cur, prev


def _conf_core(i, tm, gv_ref, gg_ref, gvp_ref, ggp_ref, w_ref, b_ref, lg_ref, lb_ref, pw_ref, pb_ref, ext_ref):
    up = gvp_ref[...] * _sig(ggp_ref[...])
    ext_ref[:HALO_B] = jnp.where(i > 0, up, 0.0)
    ext_ref[HALO_B:] = gv_ref[...] * _sig(gg_ref[...])
    acc = jnp.zeros((tm, 512), F32) + b_ref[...]
    for k in range(CONV_K):
        acc = acc + w_ref[k:k + 1, :] * ext_ref[pl.ds(HALO_B - CONV_K + 1 + k, tm), :]
    mu = jnp.mean(acc, axis=-1, keepdims=True)
    xc = acc - mu
    rstd = lax.rsqrt(jnp.mean(xc * xc, axis=-1, keepdims=True) + EPS)
    xh = xc * rstd
    u2 = xh * lg_ref[...] + lb_ref[...]
    u3 = _silu(u2)
    ypre = _bdot(u3, pw_ref[...]) + pb_ref[...]
    return xh, rstd, u2, u3, ypre


def _conf_fwd(proj, dw_w, dw_b, ln_g, ln_b, pw2, pw2_b, name):
    T = proj.shape[0]
    tm = min(512, T)
    cur, prev = _conf_specs(T, tm)

    def body(gv_ref, gg_ref, gvp_ref, ggp_ref, zb_ref, w_ref, b_ref, lg_ref, lb_ref, pw_ref, pb_ref, o_ref, ext_ref):
        i = pl.program_id(0)
        ypre = _conf_core(i, tm, gv_ref, gg_ref, gvp_ref, ggp_ref, w_ref, b_ref, lg_ref, lb_ref, pw_ref, pb_ref,
                          ext_ref)[4]
        o_ref[...] = ypre * _silu(zb_ref[...])

    full = lambda s: pl.BlockSpec(s, lambda i: (0, 0))
    return pl.pallas_call(
        body, name=name, grid=(T // tm,),
        in_specs=[cur(C_GV), cur(C_GG), prev(C_GV), prev(C_GG), cur(C_ZB), full((CONV_K, 512)), full((1, 512)),
                  full((1, 512)), full((1, 512)), full((512, 512)), full((1, 512))],
        out_specs=pl.BlockSpec((tm, 512), lambda i: (i, 0)),
        out_shape=jax.ShapeDtypeStruct((T, 512), F32),
        scratch_shapes=[pltpu.VMEM((tm + HALO_B, 512), F32)],
        compiler_params=_cparams(("parallel",)),
    )(proj, proj, proj, proj, proj, dw_w, _row(dw_b), _row(ln_g), _row(ln_b), pw2, _row(pw2_b))


def _conf_bwd1(dproj, proj, dyb, dw_w, dw_b, ln_g, ln_b, pw2, pw2_b, name):
    T = proj.shape[0]
    tm = min(512, T)
    cur, prev = _conf_specs(T, tm)

    def body(dp_any, gv_ref, gg_ref, gvp_ref, ggp_ref, zb_ref, dy_ref, w_ref, b_ref, lg_ref, lb_ref, pw_ref, pb_ref,
             dzb_ref, du1_ref, gpw_ref, st_ref, ext_ref):
        i = pl.program_id(0)
        xh, rstd, u2, u3, ypre = _conf_core(i, tm, gv_ref, gg_ref, gvp_ref, ggp_ref, w_ref, b_ref, lg_ref, lb_ref,
                                            pw_ref, pb_ref, ext_ref)
        zb, dy = zb_ref[...], dy_ref[...]
        dzb_ref[...] = (dy * ypre * _dsilu(zb)).astype(BF16)
        dyp = dy * _silu(zb)
        du2 = _bdot(dyp, pw_ref[...], NT) * _dsilu(u2)
        dxh = du2 * lg_ref[...]
        du1 = rstd * (dxh - jnp.mean(dxh, axis=-1, keepdims=True) - xh * jnp.mean(dxh * xh, axis=-1, keepdims=True))
        du1_ref[...] = du1
        gpw = _bdot(u3, dyp, TN)
        rs = lambda a: jnp.sum(a, axis=0, keepdims=True)
        upd = jnp.concatenate([rs(dyp), rs(du2 * xh), rs(du2), rs(du1), jnp.zeros((4, 512), F32)], axis=0)

        @pl.when(i == 0)
        def _():
            gpw_ref[...] = gpw
            st_ref[...] = upd

        @pl.when(i > 0)
        def _():
            gpw_ref[...] += gpw
            st_ref[...] += upd

    full = lambda s: pl.BlockSpec(s, lambda i: (0, 0))
    blk = pl.BlockSpec((tm, 512), lambda i: (i, 0))
    return pl.pallas_call(
        body, name=name, grid=(T // tm,),
        in_specs=[pl.BlockSpec(memory_space=pl.ANY), cur(C_GV), cur(C_GG), prev(C_GV), prev(C_GG), cur(C_ZB), blk,
                  full((CONV_K, 512)), full((1, 512)), full((1, 512)), full((1, 512)), full((512, 512)), full((1, 512))],
        out_specs=[cur(C_ZB), blk, full((512, 512)), full((8, 512))],
        out_shape=[jax.ShapeDtypeStruct(dproj.shape, BF16), jax.ShapeDtypeStruct((T, 512), F32),
                   jax.ShapeDtypeStruct((512, 512), F32), jax.ShapeDtypeStruct((8, 512), F32)],
        scratch_shapes=[pltpu.VMEM((tm + HALO_B, 512), F32)],
        input_output_aliases={0: 0},
        compiler_params=_cparams(("arbitrary",)),
    )(dproj, proj, proj, proj, proj, proj, dyb, dw_w, _row(dw_b), _row(ln_g), _row(ln_b), pw2, _row(pw2_b))


def _conf_bwd2(dproj, proj, du1, dw_w, name):
    T = proj.shape[0]
    tm = min(512, T)
    nt = T // tm
    r = tm // HALO_B
    cur, prev = _conf_specs(T, tm)

    def body(dp_any, gv_ref, gg_ref, gvp_ref, ggp_ref, du_ref, dun_ref, w_ref, dglu_ref, gw_ref, extu_ref, extd_ref):
        i = pl.program_id(0)
        gv, sg = gv_ref[...], _sig(gg_ref[...])
        extu_ref[:HALO_B] = jnp.where(i > 0, gvp_ref[...] * _sig(ggp_ref[...]), 0.0)
        extu_ref[HALO_B:] = gv * sg
        du1 = du_ref[...]
        extd_ref[:tm] = du1
        extd_ref[tm:] = jnp.where(i < nt - 1, dun_ref[...], 0.0)
        du0 = jnp.zeros((tm, 512), F32)
        rows = []
        for k in range(CONV_K):
            du0 = du0 + w_ref[k:k + 1, :] * extd_ref[pl.ds(CONV_K - 1 - k, tm), :]
            rows.append(jnp.sum(du1 * extu_ref[pl.ds(HALO_B - CONV_K + 1 + k, tm), :], axis=0, keepdims=True))
        rows.append(jnp.zeros((1, 512), F32))
        gw = jnp.concatenate(rows, axis=0)
        dglu_ref[:, :512] = (du0 * sg).astype(BF16)
        dglu_ref[:, 512:] = (du0 * gv * sg * (1.0 - sg)).astype(BF16)

        @pl.when(i == 0)
        def _():
            gw_ref[...] = gw

        @pl.when(i > 0)
        def _():
            gw_ref[...] += gw

    full = lambda s: pl.BlockSpec(s, lambda i: (0, 0))
    return pl.pallas_call(
        body, name=name, grid=(nt,),
        in_specs=[pl.BlockSpec(memory_space=pl.ANY), cur(C_GV), cur(C_GG), prev(C_GV), prev(C_GG),
                  pl.BlockSpec((tm, 512), lambda i: (i, 0)),
                  pl.BlockSpec((HALO_B, 512), lambda i: (jnp.minimum((i + 1) * r, T // HALO_B - 1), 0)),
                  full((CONV_K, 512))],
        out_specs=[pl.BlockSpec((tm, 1024), lambda i: (i, C_GV // 1024)), full((32, 512))],
        out_shape=[jax.ShapeDtypeStruct(dproj.shape, BF16), jax.ShapeDtypeStruct((32, 512), F32)],
        scratch_shapes=[pltpu.VMEM((tm + HALO_B, 512), F32), pltpu.VMEM((tm + HALO_B, 512), F32)],
        input_output_aliases={0: 0},
        compiler_params=_cparams(("arbitrary",)),
    )(dproj, proj, proj, proj, proj, du1, du1, dw_w)


HALO_C = 8
QKV_C = 1536


def _softplus(x):
    return jnp.maximum(x, 0.0) + jnp.log1p(jnp.exp(-jnp.abs(x)))


def _gdn_conv(i, tm, x_ref, xp_ref, w_ref, ext_ref):
    ext_ref[:HALO_C] = jnp.where(i > 0, xp_ref[...], 0.0)
    ext_ref[HALO_C:] = x_ref[...]
    pre = jnp.zeros((tm, QKV_C), F32)
    for k in range(DN_K):
        pre = pre + w_ref[k:k + 1, :] * ext_ref[pl.ds(HALO_C - DN_K + 1 + k, tm), :]
    return pre


def _gdn_specs(T, tm):
    r = tm // HALO_C
    cur = pl.BlockSpec((tm, QKV_C), lambda i: (i, C_QC // QKV_C))
    prev = pl.BlockSpec((HALO_C, QKV_C), lambda i: (jnp.maximum(i * r - 1, 0), C_QC // QKV_C))
    ab = pl.BlockSpec((tm, 128), lambda i: (i, C_AB // 128))
    return cur, prev, ab


def _gdn_prep_fwd(proj, sconv_w, alog_v, dtb_v, name):
    T = proj.shape[0]
    tm = min(512, T)
    cur, prev, ab = _gdn_specs(T, tm)

    def body(x_ref, xp_ref, ab_ref, w_ref, al_ref, dt_ref, q_ref, k_ref, v_ref, gb_ref, ext_ref):
        i = pl.program_id(0)
        y = _silu(_gdn_conv(i, tm, x_ref, xp_ref, w_ref, ext_ref))
        for h in range(DN_HEADS):
            sl = slice(128 * h, 128 * h + 128)
            qh, kh = y[:, sl], y[:, 512 + 128 * h:512 + 128 * h + 128]
            q_ref[:, sl] = qh * lax.rsqrt(jnp.sum(qh * qh, axis=-1, keepdims=True) + EPS) * (128 ** -0.5)
            k_ref[:, sl] = kh * lax.rsqrt(jnp.sum(kh * kh, axis=-1, keepdims=True) + EPS)
        v_ref[...] = y[:, 1024:]
        abv = ab_ref[...]
        lane = lax.broadcasted_iota(jnp.int32, (tm, 128), 1)
        g = -jnp.exp(al_ref[...]) * _softplus(abv + dt_ref[...])
        gb_ref[...] = jnp.where(lane < DN_HEADS, g, _sig(abv))

    full = lambda s: pl.BlockSpec(s, lambda i: (0, 0))
    blk = pl.BlockSpec((tm, 512), lambda i: (i, 0))
    return pl.pallas_call(
        body, name=name, grid=(T // tm,),
        in_specs=[cur, prev, ab, full((DN_K, QKV_C)), full((1, 128)), full((1, 128))],
        out_specs=[blk, blk, blk, pl.BlockSpec((tm, 128), lambda i: (i, 0))],
        out_shape=[jax.ShapeDtypeStruct((T, 512), F32)] * 3 + [jax.ShapeDtypeStruct((T, 128), F32)],
        scratch_shapes=[pltpu.VMEM((tm + HALO_C, QKV_C), F32)],
        compiler_params=_cparams(("parallel",)),
    )(proj, proj, proj, sconv_w, alog_v, dtb_v)


def _hdot(a, b, dims=NN):
    return _dot(a, b, dims, precision=HI)


def _lockstep(gens):
    results, live = [None] * len(gens), list(range(len(gens)))
    while live:
        for i in list(live):
            try:
                next(gens[i])
            except StopIteration as stop:
                results[i] = stop.value
                live.remove(i)
    return results


def _split(a):
    hi = a.astype(BF16)
    return hi, (a - hi.astype(F32)).astype(BF16)


def _dot_exact(a, b, dims=NN, split_left=True):
    x = (a if split_left else b).astype(F32)
    hi = x.astype(BF16)
    r = x - hi.astype(F32)
    mid = r.astype(BF16)
    lo = (r - mid.astype(F32)).astype(BF16)
    other = (b if split_left else a).astype(BF16)
    one = (lambda p: _dot(p, other, dims)) if split_left else (lambda p: _dot(other, p, dims))
    return (one(lo) + one(mid)) + one(hi)


def _dot3(a, b):
    (ah, al), (bh, bl) = a, b
    return _dot(ah, bh) + (_dot(ah, bl) + _dot(al, bh))


def _tri_inv(mats, eye):
    ps = [-a for a in mats]
    ts = [eye + p for p in ps]
    for _ in range(5):
        sp = [_split(p) for p in ps]
        ps = [_dot3(s, s) for s in sp]
        sp = [_split(p) for p in ps]
        ts = [t + _dot3(_split(t), s) for t, s in zip(ts, sp)]
    return ts


def _tri_consts():
    ii = lax.broadcasted_iota(jnp.int32, (CHUNK, CHUNK), 0)
    jj = lax.broadcasted_iota(jnp.int32, (CHUNK, CHUNK), 1)
    return ii >= jj, ii > jj, (ii == jj).astype(F32)


def _gdn_local(q, k, v, gcol, grow, bcol, lower, strict):
    dm = jnp.where(lower, jnp.exp(jnp.where(lower, gcol - grow, 0.0)), 0.0)
    kb = k * bcol
    a = jnp.where(strict, _bdot(kb, k, NT) * dm, 0.0)
    gc = jnp.exp(gcol)
    glast = grow[:, CHUNK - 1:CHUNK]
    return dict(q=q, k=k, v=v, bcol=bcol, gcol=gcol, glast=glast, dm=dm, kb=kb, a=a, gc=gc, vb=v * bcol,
                kbg=kb * gc, p=_bdot(q, k, NT) * dm, qe=q * gc, ke=k * jnp.exp(glast - gcol))


def _gdn_chunk_bwd(c, do, dvn, ds_new, lower, strict, ones):
    rs = lambda m: jnp.sum(m, axis=-1, keepdims=True)
    colsum = lambda m: _dot_exact(m, ones, TN)[:, :1]
    q, k, v, bcol, dm, tm, gc, s = c["q"], c["k"], c["v"], c["bcol"], c["dm"], c["tm"], c["gc"], c["s"]
    eg = jnp.exp(c["glast"])
    dqe = _bdot(do, s, NT)
    dp = jnp.where(lower, _bdot(do, c["vn"], NT), 0.0)
    dw = -_bdot(dvn, s, NT)
    dke = _bdot(c["vn"], ds_new, NT)
    dvb = _bdot(tm, dvn, TN)
    yield
    dglast = jnp.sum(rs(ds_new * s), axis=0, keepdims=True) * eg
    dk = dke * jnp.exp(c["glast"] - c["gcol"])
    r_ke = rs(dke * c["ke"])
    dglast = dglast + jnp.sum(r_ke, axis=0, keepdims=True)
    dgam = rs(dqe * c["qe"]) - r_ke
    dq = dqe * gc
    dpm = dp * dm
    mp = dp * c["p"]
    dq = dq + _bdot(dpm, k)
    dk = dk + _bdot(dpm, q, TN)
    dt = _bdot(dvn, c["vb"], NT) + _bdot(dw, c["kbg"], NT)
    dkbg = _bdot(tm, dw, TN)
    dgam = dgam + rs(mp) - colsum(mp)
    yield
    dkb = dkbg * gc
    dgam = dgam + rs(dkbg * c["kbg"])
    dat = _bdot(tm, dt, TN)
    yield
    da = jnp.where(strict, -_bdot(dat, tm, NT), 0.0)
    yield
    dam = da * dm
    ma = da * c["a"]
    dkb = dkb + _bdot(dam, k)
    dk = dk + _bdot(dam, c["kb"], TN)
    dgam = dgam + rs(ma) - colsum(ma)
    yield
    dk = dk + dkb * bcol
    dbeta = rs(dkb * k) + rs(dvb * v)
    dv = dvb * bcol
    row = lax.broadcasted_iota(jnp.int32, (CHUNK, 1), 0)
    dgam = dgam + jnp.where(row == CHUNK - 1, dglast, 0.0)
    dg = _dot_exact(lower, dgam, TN, split_left=False)
    return dq, dk, dv, dg, dbeta


GROUP = 4


def _chunk_decay(gb_ref, gt_ref, lmat, g):
    rows = slice(CHUNK * g, CHUNK * g + CHUNK)
    return rows, _dot_exact(lmat, gb_ref[rows, :], split_left=False), _dot_exact(gt_ref[g], lmat, NT)


def _gdn_chunk_fwd(qd, kd, vd, gb, gbt, name):
    T = qd.shape[0]
    G = GROUP
    ng = T // (CHUNK * G)

    def body(q_ref, k_ref, v_ref, gb_ref, gt_ref, u_ref, w_ref, qe_ref, ke_ref, p_ref, t_ref, eg_ref):
        lower, strict, eye = _tri_consts()
        lmat = lower.astype(F32)
        decay = [_chunk_decay(gb_ref, gt_ref, lmat, g) for g in range(G)]
        chains = [(g, h) for g in range(G) for h in range(DN_HEADS)]
        cs = []
        for g, h in chains:
            rows, gcs, grs = decay[g]
            sl = slice(128 * h, 128 * h + 128)
            c = _gdn_local(q_ref[rows, sl], k_ref[rows, sl], v_ref[rows, sl], gcs[:, h:h + 1], grs[h:h + 1, :],
                           gb_ref[rows, DN_HEADS + h:DN_HEADS + h + 1], lower, strict)
            qe_ref[rows, sl] = c["qe"].astype(BF16)
            ke_ref[rows, sl] = c["ke"].astype(BF16)
            p_ref[rows, 64 * h:64 * h + 64] = c["p"].astype(BF16)
            eg_ref[g, h:h + 1, :] = jnp.broadcast_to(jnp.exp(c["glast"]), (1, 128))
            cs.append(c)
        tms = [t.astype(BF16) for t in _tri_inv([c["a"] for c in cs], eye)]
        us = [_dot(t, c["vb"].astype(BF16)) for t, c in zip(tms, cs)]
        ws = [_dot(t, c["kbg"].astype(BF16)) for t, c in zip(tms, cs)]
        for (g, h), tm, u, w in zip(chains, tms, us, ws):
            rows, sl = decay[g][0], slice(128 * h, 128 * h + 128)
            u_ref[rows, sl] = u
            w_ref[rows, sl] = w.astype(BF16)
            t_ref[rows, 64 * h:64 * h + 64] = tm
        for g in range(G):
            eg_ref[g, DN_HEADS:, :] = jnp.zeros((8 - DN_HEADS, 128), F32)

    blk = pl.BlockSpec((CHUNK * G, 512), lambda n: (n, 0))
    half = pl.BlockSpec((CHUNK * G, 256), lambda n: (n, 0))
    return pl.pallas_call(
        body, name=name, grid=(ng,),
        in_specs=[blk, blk, blk, pl.BlockSpec((CHUNK * G, 128), lambda n: (n, 0)),
                  pl.BlockSpec((G, 8, CHUNK), lambda n: (n, 0, 0))],
        out_specs=[blk, blk, blk, blk, half, half, pl.BlockSpec((G, 8, 128), lambda n: (n, 0, 0))],
        out_shape=[jax.ShapeDtypeStruct((T, 512), F32)] + [jax.ShapeDtypeStruct((T, 512), BF16)] * 3
        + [jax.ShapeDtypeStruct((T, 256), BF16)] * 2 + [jax.ShapeDtypeStruct((T // CHUNK, 8, 128), F32)],
        compiler_params=_cparams(("parallel",)),
    )(qd, kd, vd, gb, gbt)


def _gdn_scan_fwd(u, w, qe, ke, pm, eg, proj, dn_g, name):
    T = u.shape[0]
    nc = T // CHUNK

    def body(u_ref, w_ref, qe_ref, ke_ref, p_ref, eg_ref, z_ref, ng_ref, y_ref, o_ref, vn_ref, ss_ref, s_ref):
        n = pl.program_id(0)

        @pl.when(n == 0)
        def _():
            s_ref[...] = jnp.zeros_like(s_ref)

        ss_ref[0] = s_ref[...]

        def head(h):
            sl = slice(128 * h, 128 * h + 128)
            s = s_ref[h]
            sb = s.astype(BF16)
            vn = u_ref[:, sl] - _dot(w_ref[:, sl], sb)
            qs = _dot(qe_ref[:, sl], sb)
            yield
            vb = vn.astype(BF16)
            o = qs + _dot(p_ref[:, 64 * h:64 * h + 64], vb)
            s_ref[h] = s * eg_ref[0, h:h + 1, :] + _dot(ke_ref[:, sl], vb, TN)
            yield
            vn_ref[:, sl] = vb
            o_ref[:, sl] = o
            y_ref[:, sl] = _rms(o, None)[0] * ng_ref[...] * _silu(z_ref[:, sl])

        _lockstep([head(h) for h in range(DN_HEADS)])

    blk = pl.BlockSpec((CHUNK, 512), lambda n: (n, 0))
    return pl.pallas_call(
        body, name=name, grid=(nc,),
        in_specs=[blk, blk, blk, blk, pl.BlockSpec((CHUNK, 256), lambda n: (n, 0)),
                  pl.BlockSpec((1, 8, 128), lambda n: (n, 0, 0)),
                  pl.BlockSpec((CHUNK, 512), lambda n: (n, C_ZC // 512)), pl.BlockSpec((1, 128), lambda n: (0, 0))],
        out_specs=[blk, blk, blk, pl.BlockSpec((1, DN_HEADS, 128, 128), lambda n: (n, 0, 0, 0))],
        out_shape=[jax.ShapeDtypeStruct((T, 512), F32), jax.ShapeDtypeStruct((T, 512), F32),
                   jax.ShapeDtypeStruct((T, 512), BF16), jax.ShapeDtypeStruct((nc, DN_HEADS, 128, 128), F32)],
        scratch_shapes=[pltpu.VMEM((DN_HEADS, 128, 128), F32)],
        compiler_params=_cparams(("arbitrary",)),
    )(u, w, qe, ke, pm, eg, proj, dn_g)


def _gdn_scan_bwd(dproj, w, qe, ke, pm, eg, o, proj, dyc, dn_g, name):
    T = o.shape[0]
    nc = T // CHUNK
    rev = lambda n: nc - 1 - n

    def body(dp_any, w_ref, qe_ref, ke_ref, p_ref, eg_ref, o_ref, z_ref, dy_ref, ng_ref,
             dz_ref, do_ref, dvn_ref, dsn_ref, gng_ref, ds_ref):
        n = pl.program_id(0)

        @pl.when(n == 0)
        def _():
            ds_ref[...] = jnp.zeros_like(ds_ref)
            gng_ref[...] = jnp.zeros_like(gng_ref)

        dsn_ref[0] = ds_ref[...]

        def head(h):
            sl = slice(128 * h, 128 * h + 128)
            oh, r = _rms(o_ref[:, sl], None)
            z, dy = z_ref[:, sl], dy_ref[:, sl]
            dz_ref[:, sl] = (dy * (oh * ng_ref[...]) * _dsilu(z)).astype(BF16)
            do, gg = _rms_bwd(dy * _silu(z), oh, r, ng_ref[...])
            dob = do.astype(BF16)
            ds = ds_ref[h]
            dvn = _dot(p_ref[:, 64 * h:64 * h + 64], dob, TN) + _dot(ke_ref[:, sl], ds.astype(BF16))
            qd = _dot(qe_ref[:, sl], dob, TN)
            yield
            dvb = dvn.astype(BF16)
            ds_ref[h] = qd + eg_ref[0, h:h + 1, :] * ds - _dot(w_ref[:, sl], dvb, TN)
            do_ref[:, sl] = dob
            dvn_ref[:, sl] = dvb
            return jnp.sum(gg, axis=0, keepdims=True)

        gng = _lockstep([head(h) for h in range(DN_HEADS)])
        gng_ref[...] += (gng[0] + gng[1]) + (gng[2] + gng[3])

    blk = pl.BlockSpec((CHUNK, 512), lambda n: (rev(n), 0))
    state = pl.BlockSpec((1, DN_HEADS, 128, 128), lambda n: (rev(n), 0, 0, 0))
    return pl.pallas_call(
        body, name=name, grid=(nc,),
        in_specs=[pl.BlockSpec(memory_space=pl.ANY), blk, blk, blk, pl.BlockSpec((CHUNK, 256), lambda n: (rev(n), 0)),
                  pl.BlockSpec((1, 8, 128), lambda n: (rev(n), 0, 0)), blk,
                  pl.BlockSpec((CHUNK, 512), lambda n: (rev(n), C_ZC // 512)), blk,
                  pl.BlockSpec((1, 128), lambda n: (0, 0))],
        out_specs=[pl.BlockSpec((CHUNK, 512), lambda n: (rev(n), C_ZC // 512)), blk, blk, state,
                   pl.BlockSpec((1, 128), lambda n: (0, 0))],
        out_shape=[jax.ShapeDtypeStruct(dproj.shape, BF16), jax.ShapeDtypeStruct((T, 512), BF16),
                   jax.ShapeDtypeStruct((T, 512), BF16), jax.ShapeDtypeStruct((nc, DN_HEADS, 128, 128), F32),
                   jax.ShapeDtypeStruct((1, 128), F32)],
        scratch_shapes=[pltpu.VMEM((DN_HEADS, 128, 128), F32)],
        input_output_aliases={0: 0},
        compiler_params=_cparams(("arbitrary",)),
    )(dproj, w, qe, ke, pm, eg, o, proj, dyc, dn_g)


def _gdn_chunk_grad(qd, kd, vd, gb, gbt, tmi, ssave, dsn, do, dvn, vn, name):
    T = qd.shape[0]
    G = GROUP
    ng = T // (CHUNK * G)

    def body(q_ref, k_ref, v_ref, gb_ref, gt_ref, t_ref, ss_ref, dsn_ref, do_ref, dvn_ref, vn_ref,
             dq_ref, dk_ref, dv_ref, dgb_ref):
        lower, strict, _ = _tri_consts()
        lmat = lower.astype(F32)
        ones = jnp.ones((CHUNK, 128), F32)
        lane = lax.broadcasted_iota(jnp.int32, (CHUNK, 128), 1)
        decay = [_chunk_decay(gb_ref, gt_ref, lmat, g) for g in range(G)]
        chains = [(g, h) for g in range(G) for h in range(DN_HEADS)]
        gens = []
        for g, h in chains:
            rows, gcs, grs = decay[g]
            sl = slice(128 * h, 128 * h + 128)
            c = _gdn_local(q_ref[rows, sl], k_ref[rows, sl], v_ref[rows, sl], gcs[:, h:h + 1], grs[h:h + 1, :],
                           gb_ref[rows, DN_HEADS + h:DN_HEADS + h + 1], lower, strict)
            c.update(tm=t_ref[rows, 64 * h:64 * h + 64], s=ss_ref[g, h], vn=vn_ref[rows, sl])
            gens.append(_gdn_chunk_bwd(c, do_ref[rows, sl], dvn_ref[rows, sl], dsn_ref[g, h], lower, strict, ones))
        dgb = [jnp.zeros((CHUNK, 128), F32) for _ in range(G)]
        for (g, h), (dq, dk, dv, dg, dbeta) in zip(chains, _lockstep(gens)):
            rows, sl = decay[g][0], slice(128 * h, 128 * h + 128)
            dq_ref[rows, sl], dk_ref[rows, sl], dv_ref[rows, sl] = dq, dk, dv
            dgb[g] = dgb[g] + jnp.where(lane == h, dg, 0.0) + jnp.where(lane == DN_HEADS + h, dbeta, 0.0)
        for g in range(G):
            dgb_ref[decay[g][0], :] = dgb[g]

    blk = pl.BlockSpec((CHUNK * G, 512), lambda n: (n, 0))
    half = pl.BlockSpec((CHUNK * G, 256), lambda n: (n, 0))
    nar = pl.BlockSpec((CHUNK * G, 128), lambda n: (n, 0))
    state = pl.BlockSpec((G, DN_HEADS, 128, 128), lambda n: (n, 0, 0, 0))
    return pl.pallas_call(
        body, name=name, grid=(ng,),
        in_specs=[blk, blk, blk, nar, pl.BlockSpec((G, 8, CHUNK), lambda n: (n, 0, 0)), half, state, state,
                  blk, blk, blk],
        out_specs=[blk, blk, blk, nar],
        out_shape=[jax.ShapeDtypeStruct((T, 512), F32)] * 3 + [jax.ShapeDtypeStruct((T, 128), F32)],
        compiler_params=_cparams(("parallel",)),
    )(qd, kd, vd, gb, gbt, tmi, ssave, dsn, do, dvn, vn)


def _gdn_prep_bwd1(dproj, proj, dqd, dkd, dvd, dgb, dkv_a, sconv_w, alog_v, dtb_v, name):
    T = proj.shape[0]
    tm = min(512, T)
    cur, prev, ab = _gdn_specs(T, tm)

    def body(dp_any, x_ref, xp_ref, ab_ref, dq_ref, dk_ref, dv_ref, dgb_ref, dkv_ref, w_ref, al_ref, dt_ref,
             o_ref, dpre_ref, st_ref, ext_ref):
        i = pl.program_id(0)
        pre = _gdn_conv(i, tm, x_ref, xp_ref, w_ref, ext_ref)
        y, dsl = _silu(pre), _dsilu(pre)
        for h in range(DN_HEADS):
            for base, g_ref, scale in ((0, dq_ref, 128 ** -0.5), (512, dk_ref, 1.0)):
                sl = slice(base + 128 * h, base + 128 * h + 128)
                xh = y[:, sl]
                r = lax.rsqrt(jnp.sum(xh * xh, axis=-1, keepdims=True) + EPS)
                xn = xh * r
                gy = g_ref[:, 128 * h:128 * h + 128]
                dpre_ref[:, sl] = (scale * r) * (gy - xn * jnp.sum(gy * xn, axis=-1, keepdims=True)) * dsl[:, sl]
        dpre_ref[:, 1024:] = dv_ref[...] * dsl[:, 1024:]
        abv, dgb = ab_ref[...], dgb_ref[...]
        lane = lax.broadcasted_iota(jnp.int32, (tm, 128), 1)
        na = -jnp.exp(al_ref[...])
        xs = abv + dt_ref[...]
        da = dgb * na * _sig(xs)
        b = _sig(abv)
        o_ref[:, :256] = dkv_ref[...].astype(BF16)
        o_ref[:, 256:] = jnp.where(lane < DN_HEADS, da,
                                   jnp.where(lane < 2 * DN_HEADS, dgb * b * (1.0 - b), 0.0)).astype(BF16)
        head = lane < DN_HEADS
        upd = jnp.concatenate([jnp.sum(jnp.where(head, dgb * na * _softplus(xs), 0.0), axis=0, keepdims=True),
                               jnp.sum(jnp.where(head, da, 0.0), axis=0, keepdims=True), jnp.zeros((6, 128), F32)],
                              axis=0)

        @pl.when(i == 0)
        def _():
            st_ref[...] = upd

        @pl.when(i > 0)
        def _():
            st_ref[...] += upd

    full = lambda s: pl.BlockSpec(s, lambda i: (0, 0))
    blk = pl.BlockSpec((tm, 512), lambda i: (i, 0))
    return pl.pallas_call(
        body, name=name, grid=(T // tm,),
        in_specs=[pl.BlockSpec(memory_space=pl.ANY), cur, prev, ab, blk, blk, blk,
                  pl.BlockSpec((tm, 128), lambda i: (i, 0)), pl.BlockSpec((tm, 256), lambda i: (i, 0)),
                  full((DN_K, QKV_C)), full((1, 128)), full((1, 128))],
        out_specs=[pl.BlockSpec((tm, 384), lambda i: (i, C_KA // 384)),
                   pl.BlockSpec((tm, QKV_C), lambda i: (i, 0)), full((8, 128))],
        out_shape=[jax.ShapeDtypeStruct(dproj.shape, BF16), jax.ShapeDtypeStruct((T, QKV_C), F32),
                   jax.ShapeDtypeStruct((8, 128), F32)],
        scratch_shapes=[pltpu.VMEM((tm + HALO_C, QKV_C), F32)],
        input_output_aliases={0: 0},
        compiler_params=_cparams(("arbitrary",)),
    )(dproj, proj, proj, proj, dqd, dkd, dvd, dgb, dkv_a, sconv_w, alog_v, dtb_v)


def _gdn_prep_bwd2(dproj, proj, dpre, sconv_w, name):
    T = proj.shape[0]
    tm = min(512, T)
    nt = T // tm
    r = tm // HALO_C
    cur, prev, _ = _gdn_specs(T, tm)

    def body(dp_any, x_ref, xp_ref, d_ref, dn_ref, w_ref, dx_ref, gw_ref, extx_ref, extd_ref):
        i = pl.program_id(0)
        extx_ref[:HALO_C] = jnp.where(i > 0, xp_ref[...], 0.0)
        extx_ref[HALO_C:] = x_ref[...]
        d = d_ref[...]
        extd_ref[:tm] = d
        extd_ref[tm:] = jnp.where(i < nt - 1, dn_ref[...], 0.0)
        dx = jnp.zeros((tm, QKV_C), F32)
        rows = []
        for k in range(DN_K):
            dx = dx + w_ref[k:k + 1, :] * extd_ref[pl.ds(DN_K - 1 - k, tm), :]
            rows.append(jnp.sum(d * extx_ref[pl.ds(HALO_C - DN_K + 1 + k, tm), :], axis=0, keepdims=True))
        rows.append(jnp.zeros((8 - DN_K, QKV_C), F32))
        gw = jnp.concatenate(rows, axis=0)
        dx_ref[...] = dx.astype(BF16)

        @pl.when(i == 0)
        def _():
            gw_ref[...] = gw

        @pl.when(i > 0)
        def _():
            gw_ref[...] += gw

    full = lambda s: pl.BlockSpec(s, lambda i: (0, 0))
    return pl.pallas_call(
        body, name=name, grid=(nt,),
        in_specs=[pl.BlockSpec(memory_space=pl.ANY), cur, prev, pl.BlockSpec((tm, QKV_C), lambda i: (i, 0)),
                  pl.BlockSpec((HALO_C, QKV_C), lambda i: (jnp.minimum((i + 1) * r, T // HALO_C - 1), 0)),
                  full((DN_K, QKV_C))],
        out_specs=[cur, full((8, QKV_C))],
        out_shape=[jax.ShapeDtypeStruct(dproj.shape, BF16), jax.ShapeDtypeStruct((8, QKV_C), F32)],
        scratch_shapes=[pltpu.VMEM((tm + HALO_C, QKV_C), F32), pltpu.VMEM((tm + HALO_C, QKV_C), F32)],
        input_output_aliases={0: 0},
        compiler_params=_cparams(("arbitrary",)),
    )(dproj, proj, proj, dpre, dpre, sconv_w)


def _merge_fwd(x, proj, ya, yb, yc, wa, wb, wc, wo, gate, name):
    T = x.shape[0]
    tm = min(256, T)

    def body(x_ref, mg_ref, ya_ref, yb_ref, yc_ref, wa_ref, wb_ref, wc_ref, wo_ref, gate_ref, o_ref):
        merged = (_sig(mg_ref[:, :D]) * _bdot(ya_ref[...], wa_ref[...])
                  + _sig(mg_ref[:, D:2 * D]) * _bdot(yb_ref[...], wb_ref[...])
                  + _sig(mg_ref[:, 2 * D:]) * _bdot(yc_ref[...], wc_ref[...]))
        o_ref[...] = x_ref[...] + gate_ref[...] * _bdot(merged, wo_ref[...])

    full = lambda s: pl.BlockSpec(s, lambda i: (0, 0))
    yb_ = pl.BlockSpec((tm, 512), lambda i: (i, 0))
    return pl.pallas_call(
        body, name=name, grid=(T // tm,),
        in_specs=[pl.BlockSpec((tm, D), lambda i: (i, 0)), pl.BlockSpec((tm, 3 * D), lambda i: (i, 0)), yb_, yb_, yb_,
                  full((512, D)), full((512, D)), full((512, D)), full((D, D)), full((1, D))],
        out_specs=pl.BlockSpec((tm, D), lambda i: (i, 0)),
        out_shape=jax.ShapeDtypeStruct((T, D), F32),
        compiler_params=_cparams(("parallel",)),
    )(x, proj, ya, yb, yc, wa, wb, wc, wo, _row(gate))


def _merge_bwd(dout, proj, ya, yb, yc, wa, wb, wc, wo, gate, name):
    T = dout.shape[0]
    tm = min(256, T)
    nt = T // tm

    def body(do_ref, mg_ref, ya_ref, yb_ref, yc_ref, wa_ref, wb_ref, wc_ref, wo_ref, gate_ref,
             dmg_ref, dya_ref, dyb_ref, dyc_ref, gwa_hbm, gwb_hbm, gwc_hbm, gwo_hbm, gg_ref,
             gwa_ref, gwb_ref, gwc_ref, gwo_ref):
        i = pl.program_id(0)

        @pl.when(i == 0)
        def _():
            for r in (gwa_ref, gwb_ref, gwc_ref, gwo_ref, gg_ref):
                r[...] = jnp.zeros_like(r)

        ys = (ya_ref[...], yb_ref[...], yc_ref[...])
        ws = (wa_ref, wb_ref, wc_ref)
        gs = tuple(_sig(mg_ref[:, j * D:(j + 1) * D]) for j in range(3))
        ps = tuple(_bdot(ys[j], ws[j][...]) for j in range(3))
        merged = gs[0] * ps[0] + gs[1] * ps[1] + gs[2] * ps[2]
        mo = _bdot(merged, wo_ref[...])
        do = do_ref[...]
        gg_ref[...] += jnp.sum(do * mo, axis=0, keepdims=True)
        dmo = do * gate_ref[...]
        dmerged = _bdot(dmo, wo_ref[...], NT)
        gwo_ref[...] += _bdot(merged, dmo, TN)
        for j, (dy_ref, gw_ref) in enumerate(((dya_ref, gwa_ref), (dyb_ref, gwb_ref), (dyc_ref, gwc_ref))):
            dp = dmerged * gs[j]
            dmg_ref[:, j * D:(j + 1) * D] = (dmerged * ps[j] * gs[j] * (1.0 - gs[j])).astype(BF16)
            dy_ref[...] = _bdot(dp, ws[j][...], NT)
            gw_ref[...] += _bdot(ys[j], dp, TN)

        @pl.when(i == nt - 1)
        def _():
            for src, dst in ((gwa_ref, gwa_hbm), (gwb_ref, gwb_hbm), (gwc_ref, gwc_hbm), (gwo_ref, gwo_hbm)):
                pltpu.sync_copy(src, dst)

    full = lambda s: pl.BlockSpec(s, lambda i: (0, 0))
    yb_ = pl.BlockSpec((tm, 512), lambda i: (i, 0))
    anyspec = pl.BlockSpec(memory_space=pl.ANY)
    return pl.pallas_call(
        body, name=name, grid=(nt,),
        in_specs=[pl.BlockSpec((tm, D), lambda i: (i, 0)), pl.BlockSpec((tm, 3 * D), lambda i: (i, 0)), yb_, yb_, yb_,
                  full((512, D)), full((512, D)), full((512, D)), full((D, D)), full((1, D))],
        out_specs=[pl.BlockSpec((tm, 3 * D), lambda i: (i, 0)), yb_, yb_, yb_, anyspec, anyspec, anyspec, anyspec,
                   full((1, D))],
        out_shape=[jax.ShapeDtypeStruct((T, NP), BF16)] + [jax.ShapeDtypeStruct((T, 512), F32)] * 3
        + [jax.ShapeDtypeStruct((512, D), F32)] * 3 + [jax.ShapeDtypeStruct((D, D), F32), jax.ShapeDtypeStruct((1, D), F32)],
        scratch_shapes=[pltpu.VMEM((512, D), F32)] * 3 + [pltpu.VMEM((D, D), F32)],
        compiler_params=_cparams(("arbitrary",)),
    )(dout, proj, ya, yb, yc, wa, wb, wc, wo, _row(gate))


def _loss_head(y, tgt, name):
    T = y.shape[0]
    tm = min(512, T)

    def body(y_ref, t_ref, dy_ref, l_ref):
        i = pl.program_id(0)
        diff = y_ref[...] - t_ref[...]
        dy_ref[...] = diff * (1.0 / D)
        part = jnp.sum(diff * diff, axis=0, keepdims=True)

        @pl.when(i == 0)
        def _():
            l_ref[...] = part

        @pl.when(i > 0)
        def _():
            l_ref[...] += part

    blk = pl.BlockSpec((tm, D), lambda i: (i, 0))
    return pl.pallas_call(
        body, name=name, grid=(T // tm,), in_specs=[blk, blk],
        out_specs=[blk, pl.BlockSpec((1, D), lambda i: (0, 0))],
        out_shape=[jax.ShapeDtypeStruct((T, D), F32), jax.ShapeDtypeStruct((1, D), F32)],
        compiler_params=_cparams(("arbitrary",)),
    )(y, tgt)


def _ada_fwd(c_all, w_ada, b_my, name):
    def body(c_ref, w_ref, b_ref, o_ref):
        sc = _silu(c_ref[...])
        for l in range(DEPTH):
            o_ref[l] = _bdot(sc, w_ref[l]) + b_ref[l:l + 1, :]

    return pl.pallas_call(body, name=name, out_shape=jax.ShapeDtypeStruct((DEPTH, N_DEV, w_ada.shape[2]), F32),
                          compiler_params=_cparams())(c_all, w_ada, b_my)


def _ada_bwd(c_all, dmod_my, name):
    def body(c_ref, d_ref, o_ref):
        sc = _silu(c_ref[...])
        for l in range(DEPTH):
            o_ref[l] = _bdot(sc, d_ref[l], TN)

    return pl.pallas_call(body, name=name, out_shape=jax.ShapeDtypeStruct((DEPTH, D, dmod_my.shape[2]), F32),
                          compiler_params=_cparams())(c_all, dmod_my)


def _adam_math(w, g, m, v):
    m = ADAM_B1 * m + (1.0 - ADAM_B1) * g
    v = ADAM_B2 * v + (1.0 - ADAM_B2) * (g * g)
    m_hat = m / (1.0 - ADAM_B1 ** ADAM_STEP)
    v_hat = v / (1.0 - ADAM_B2 ** ADAM_STEP)
    return -ADAM_LR * (m_hat / (jnp.sqrt(v_hat) + ADAM_EPS) + ADAM_WD * w), m, v


def _row_tile(rows, cap):
    best = rows
    for t in range(8, min(rows, cap) + 1, 8):
        if rows % t == 0:
            best = t
    return best if best <= cap else rows


def _adamw(w, g, m, v, name):
    R, C = w.shape
    tr = _row_tile(R, 256)

    def body(w_ref, g_ref, m_ref, v_ref, d_ref, mo_ref, vo_ref):
        d_ref[...], mo_ref[...], vo_ref[...] = _adam_math(w_ref[...], g_ref[...], m_ref[...], v_ref[...])

    blk = pl.BlockSpec((tr, C), lambda i: (i, 0))
    return pl.pallas_call(body, name=name, grid=(R // tr,), in_specs=[blk] * 4, out_specs=[blk] * 3,
                          out_shape=[jax.ShapeDtypeStruct((R, C), F32)] * 3,
                          compiler_params=_cparams(("parallel",)))(w, g, m, v)


def _sum_adamw_many(parts, ws, ms, vs, name):
    n = len(ws)

    def body(*refs):
        ins, outs = refs[:4 * n], refs[4 * n:]
        for i in range(n):
            g = ins[i][0]
            for j in range(1, N_DEV):
                g = g + ins[i][j]
            d, m, v = _adam_math(ins[n + i][...], g, ins[2 * n + i][...], ins[3 * n + i][...])
            outs[i][...], outs[n + i][...], outs[2 * n + i][...], outs[3 * n + i][...] = g, d, m, v

    shapes = [jax.ShapeDtypeStruct(w.shape, F32) for w in ws]
    out = pl.pallas_call(body, name=name, out_shape=shapes * 4, compiler_params=_cparams())(*parts, *ws, *ms, *vs)
    return out[:n], out[n:2 * n], out[2 * n:3 * n], out[3 * n:]


def _sum_adamw(parts0, parts1, w, m, v, name):
    P, R, C = parts0.shape
    tr = _row_tile(R, 128)
    nt = R // tr

    def body(p0_ref, p1_ref, w_ref, m_ref, v_ref, g_ref, d_ref, mo_ref, vo_ref):
        def emit(p_ref):
            g = p_ref[0].astype(F32)
            for j in range(1, P):
                g = g + p_ref[j].astype(F32)
            g_ref[...] = g
            d_ref[...], mo_ref[...], vo_ref[...] = _adam_math(w_ref[...], g, m_ref[...], v_ref[...])

        @pl.when(pl.program_id(0) == 0)
        def _():
            emit(p0_ref)

        @pl.when(pl.program_id(0) == 1)
        def _():
            emit(p1_ref)

    blk = pl.BlockSpec((tr, C), lambda l, i: (l * nt + i, 0))
    return pl.pallas_call(
        body, name=name, grid=(DEPTH, nt),
        in_specs=[pl.BlockSpec((P, tr, C), lambda l, i: (0, i * (1 - l) + (nt - 1) * l, 0)),
                  pl.BlockSpec((P, tr, C), lambda l, i: (0, i * l, 0)), blk, blk, blk],
        out_specs=[blk] * 4, out_shape=[jax.ShapeDtypeStruct((DEPTH * R, C), F32)] * 4,
        compiler_params=_cparams(("arbitrary", "arbitrary")))(parts0, parts1, w, m, v)


def _pair_sum(core, buf, recv, name):
    _, _, R, C = buf.shape
    tr = _row_tile(R, 128)

    def body(c_ref, a_ref, b_ref, o_ref):
        o_ref[...] = (a_ref[:, 0].astype(F32) + b_ref[...].astype(F32)).astype(BF16)

    return pl.pallas_call(
        body, name=name,
        grid_spec=pltpu.PrefetchScalarGridSpec(
            num_scalar_prefetch=1, grid=(R // tr,),
            in_specs=[pl.BlockSpec((4, 1, tr, C), lambda i, c: (0, c[0], i, 0)),
                      pl.BlockSpec((4, tr, C), lambda i, c: (0, i, 0))],
            out_specs=pl.BlockSpec((4, tr, C), lambda i, c: (0, i, 0))),
        out_shape=jax.ShapeDtypeStruct((4, R, C), BF16),
        compiler_params=_cparams(("parallel",)))(core, buf, recv)


SHARD_IN = D_IN // N_DEV


def _w_in_pieces():
    out, p = [], 0
    for a, b in _PAD_FROM:
        for j in range(N_DEV):
            lo, hi = max(a, SHARD_IN * j), min(b, SHARD_IN * (j + 1))
            if lo < hi:
                out.append((j, lo - SHARD_IN * j, hi - SHARD_IN * j, p + lo - a))
        p += b - a
    return out


def _assemble_w_in(gw, name):
    tr = 256
    nt = D // tr

    def body(x_ref, o_ref):
        for j, s0, s1, d0 in _w_in_pieces():
            o_ref[:, d0:d0 + s1 - s0] = x_ref[j, :, s0:s1]
        o_ref[:, D_IN:] = jnp.zeros((tr, NP - D_IN), gw.dtype)

    return pl.pallas_call(
        body, name=name, grid=(nt,),
        in_specs=[pl.BlockSpec((N_DEV, tr, SHARD_IN), lambda i: (0, i, 0))],
        out_specs=pl.BlockSpec((tr, NP), lambda i: (i, 0)),
        out_shape=jax.ShapeDtypeStruct((D, NP), gw.dtype),
        compiler_params=_cparams(("parallel",)))(gw)


def _split_w_in_grad(g, name):
    tr = 256

    def body(g_ref, o_ref):
        for j, s0, s1, d0 in _w_in_pieces():
            o_ref[j, :, s0:s1] = g_ref[:, d0:d0 + s1 - s0].astype(BF16)

    return pl.pallas_call(
        body, name=name, grid=(D // tr,),
        in_specs=[pl.BlockSpec((tr, NP), lambda i: (i, 0))],
        out_specs=pl.BlockSpec((N_DEV, tr, SHARD_IN), lambda i: (0, i, 0)),
        out_shape=jax.ShapeDtypeStruct((N_DEV, D, SHARD_IN), BF16),
        compiler_params=_cparams(("parallel",)))(g)


def _mesh_pos():
    return lax.axis_index("x"), lax.axis_index("y"), lax.axis_index("c")


def _launch(copies, peers, bufs, out_structs, sems, name, collective_id):
    n = len(bufs)
    if collective_id is None:
        anyspec = pl.BlockSpec(memory_space=pl.ANY)
        return list(pl.pallas_call(
            lambda *refs: copies(refs[:n], refs[n:n + len(out_structs)], *refs[n + len(out_structs):]),
            name=name, in_specs=[anyspec] * n, out_specs=[anyspec] * len(out_structs), out_shape=list(out_structs),
            scratch_shapes=list(sems))(*bufs))
    ins = [jax.new_ref(b, memory_space=pltpu.MemorySpace.HBM) for b in bufs]
    outs = [jax.empty_ref(s, memory_space=pltpu.MemorySpace.HBM) for s in out_structs]

    @pl.kernel(mesh=plsc.ScalarSubcoreMesh(axis_name="sequencer", num_cores=1), name=name, scratch_types=tuple(sems),
               compiler_params=pltpu.CompilerParams(collective_id=collective_id))
    def on_sequencer(*sem_refs):
        barrier = pltpu.get_barrier_semaphore()
        targets = peers()
        for p in targets:
            pl.semaphore_signal(barrier, inc=1, device_id=p, device_id_type=pl.DeviceIdType.MESH)
        pl.semaphore_wait(barrier, len(targets))
        copies(ins, outs, *sem_refs)

    on_sequencer()
    return [r[...] for r in outs]


def _all_gather(blocks, name, collective_id=None):
    n = len(blocks)

    def peers():
        x, y, c = _mesh_pos()
        return [(x, y, 1 - c), (1 - x, y, c), (x, 1 - y, c), (1 - x, 1 - y, c)]

    def copies(ins, outs, send_sems, recv_sems, local_sems):
        x, y, c = _mesh_pos()
        me, sibling = (x, y, c), (x, y, 1 - c)
        chips = [(1 - x, y), (x, 1 - y), (1 - x, 1 - y)]
        idx = lambda p: 4 * p[0] + 2 * p[1] + p[2]

        def copy(a, k, block, to, src=None):
            dst = outs[a].at[idx(block)]
            return pltpu.make_async_remote_copy(
                src_ref=dst if src is None else src, dst_ref=dst, send_sem=send_sems.at[a, k],
                recv_sem=recv_sems.at[a, k], device_id=to, device_id_type=pl.DeviceIdType.MESH)

        mine = [pltpu.make_async_copy(ins[a], outs[a].at[idx(me)], local_sems.at[a]) for a in range(n)]
        for cp in mine:
            cp.start()
        first = []
        for a in range(n):
            first.append(copy(a, 0, me, sibling, src=ins[a]))
            first += [copy(a, 1 + j, me, (*chip, c), src=ins[a]) for j, chip in enumerate(chips)]
        for cp in first:
            cp.start()
        passed = []
        for j, chip in enumerate(chips):
            for a in range(n):
                copy(a, 1 + j, (*chip, c), me).wait_recv()
                cp = copy(a, 4 + j, (*chip, c), sibling)
                cp.start()
                passed.append(cp)
        for a in range(n):
            copy(a, 0, sibling, me).wait_recv()
            for j, chip in enumerate(chips):
                copy(a, 4 + j, (*chip, 1 - c), me).wait_recv()
        for cp in first + passed:
            cp.wait_send()
        for cp in mine:
            cp.wait()

    return _launch(copies, peers, blocks, [jax.ShapeDtypeStruct((N_DEV,) + b.shape, b.dtype) for b in blocks],
                   [pltpu.SemaphoreType.DMA((n, 7)), pltpu.SemaphoreType.DMA((n, 7)), pltpu.SemaphoreType.DMA((n,))],
                   name, collective_id)


def _exchange_core(bufs, name, collective_id=None):
    n = len(bufs)

    def peers():
        x, y, c = _mesh_pos()
        return [(x, y, 1 - c)]

    def copies(ins, outs, send_sems, recv_sems):
        x, y, c = _mesh_pos()
        started = []
        for a in range(n):
            for q in range(4):
                cp = pltpu.make_async_remote_copy(
                    src_ref=ins[a].at[q, 1 - c], dst_ref=outs[a].at[q], send_sem=send_sems.at[a, q],
                    recv_sem=recv_sems.at[a, q], device_id=(x, y, 1 - c), device_id_type=pl.DeviceIdType.MESH)
                cp.start()
                started.append(cp)
        for cp in started:
            cp.wait()

    return _launch(copies, peers, bufs, [jax.ShapeDtypeStruct((4,) + b.shape[2:], b.dtype) for b in bufs],
                   [pltpu.SemaphoreType.DMA((n, 4)), pltpu.SemaphoreType.DMA((n, 4))], name, collective_id)


def _exchange_chips(bufs, name, collective_id=None):
    n = len(bufs)

    def peers():
        x, y, c = _mesh_pos()
        return [(1 - x, y, c), (x, 1 - y, c), (1 - x, 1 - y, c)]

    def copies(ins, outs, send_sems, recv_sems, local_sems):
        x, y, c = _mesh_pos()
        chip = 2 * x + y
        local = [pltpu.make_async_copy(ins[a].at[chip], outs[a].at[chip], local_sems.at[a]) for a in range(n)]
        for cp in local:
            cp.start()
        started = []
        for k in range(1, 4):
            px = 1 - x if k & 2 else x
            py = 1 - y if k & 1 else y
            for a in range(n):
                cp = pltpu.make_async_remote_copy(
                    src_ref=ins[a].at[2 * px + py], dst_ref=outs[a].at[chip], send_sem=send_sems.at[a, k - 1],
                    recv_sem=recv_sems.at[a, k - 1], device_id=(px, py, c), device_id_type=pl.DeviceIdType.MESH)
                cp.start()
                started.append(cp)
        for cp in started:
            cp.wait()
        for cp in local:
            cp.wait()

    return _launch(copies, peers, bufs, [jax.ShapeDtypeStruct(b.shape, b.dtype) for b in bufs],
                   [pltpu.SemaphoreType.DMA((n, 3)), pltpu.SemaphoreType.DMA((n, 3)), pltpu.SemaphoreType.DMA((n,))],
                   name, collective_id)


_SMALL = ("b_ada", "norm_g", "q_norm_g", "k_norm_g", "sinks", "dw_b", "ln_g", "ln_b", "pw2_b", "a_log", "dt_bias",
          "dn_norm_g", "dw_w", "sconv_w")


def _lane4(v):
    return jnp.pad(v, (0, 124)).reshape(1, 128)


def kernel(x, c, w_ada, b_ada, norm_g, w_in, q_norm_g, k_norm_g, sinks, dw_w, dw_b, ln_g, ln_b, pw2_w, pw2_b, sconv_w, a_log, dt_bias, dn_norm_g, w_proj_a, w_proj_b, w_proj_c, w_out, loss_target, m_w_ada, m_b_ada, m_norm_g, m_w_in, m_q_norm_g, m_k_norm_g, m_sinks, m_dw_w, m_dw_b, m_ln_g, m_ln_b, m_pw2_w, m_pw2_b, m_sconv_w, m_a_log, m_dt_bias, m_dn_norm_g, m_w_proj_a, m_w_proj_b, m_w_proj_c, m_w_out, v_w_ada, v_b_ada, v_norm_g, v_w_in, v_q_norm_g, v_k_norm_g, v_sinks, v_dw_w, v_dw_b, v_ln_g, v_ln_b, v_pw2_w, v_pw2_b, v_sconv_w, v_a_log, v_dt_bias, v_dn_norm_g, v_w_proj_a, v_w_proj_b, v_w_proj_c, v_w_out):
    T = x.shape[1]
    nc = T // CHUNK
    xi, yi, ci = _mesh_pos()
    me = 4 * xi + 2 * yi + ci
    big_w = (w_in, pw2_w, w_proj_a, w_proj_b, w_proj_c, w_out)
    big_m = (m_w_in, m_pw2_w, m_w_proj_a, m_w_proj_b, m_w_proj_c, m_w_out)
    big_v = (v_w_in, v_pw2_w, v_w_proj_a, v_w_proj_b, v_w_proj_c, v_w_out)

    ada_cols = w_ada.shape[2]
    dw_cols, sc_cols = dw_w.shape[2], sconv_w.shape[2]
    flat2 = lambda a: a.reshape(-1, a.shape[-1])
    big16 = [a.astype(BF16) for a in big_w]
    gw_in0, c_all, gdw, gsc = _all_gather([big16[0][0], c, dw_w, sconv_w], "gather_first", collective_id=0)
    c_all = c_all.reshape(N_DEV, D)
    dw_f = gdw.transpose(1, 2, 0, 3).reshape(DEPTH, CONV_K, 512)
    sc_f = gsc.transpose(1, 2, 0, 3).reshape(DEPTH, DN_K, QKV_C)

    b_my = lax.dynamic_slice(b_ada, (0, me * ada_cols), (DEPTH, ada_cols))
    mod_part = _ada_fwd(c_all, w_ada, b_my, "ada_fwd")
    (gmod,) = _all_gather([mod_part.reshape(-1, 128)], "gather_mod")

    rest0 = [a[0] for a in big16[1:]]
    all1 = [a[1] for a in big16]
    (rest0, all1), gmod = lax.optimization_barrier(((rest0, all1), gmod))
    got0 = [gw_in0] + _all_gather(rest0, "gather_rest0", collective_id=1)
    got1 = _all_gather(all1, "gather_weights1", collective_id=6)
    wp, pw2_f, wa_f, wb_f, wc_f, wo_f = [], [], [], [], [], []
    for l, (gw_in, gpw2, gpa, gpb, gpc, gwo) in enumerate((got0, got1)):
        wp.append(_assemble_w_in(gw_in, f"assemble_w_in{l}"))
        pw2_f.append(gpw2.reshape(512, 512))
        for dst, g in ((wa_f, gpa), (wb_f, gpb), (wc_f, gpc)):
            dst.append(g.transpose(1, 0, 2).reshape(512, D))
        wo_f.append(gwo.reshape(D, D))
    mod_all = gmod.reshape(N_DEV, DEPTH, N_DEV, ada_cols).transpose(1, 2, 0, 3).reshape(DEPTH, N_DEV, 3 * D)
    mod = lax.dynamic_index_in_dim(mod_all, me, axis=1, keepdims=False)
    shift, scale, gate = mod[:, :D], mod[:, D:2 * D], mod[:, 2 * D:]

    xs, saved = [x[0]], []
    for l in range(DEPTH):
        xl = xs[-1]
        h = _norm_fwd(xl, norm_g[l], scale[l], shift[l], f"norm_fwd{l}")
        proj = _mm(h, wp[l], tm=min(1024, T), tn=1152, tk=D, name=f"in_proj{l}")
        ya = _attn_fwd(proj, q_norm_g[l], k_norm_g[l], sinks[l], f"attn_fwd{l}")
        yb = _conf_fwd(proj, dw_f[l], dw_b[l], ln_g[l], ln_b[l], pw2_f[l], pw2_b[l], f"conf_fwd{l}")
        alv, dtv, dng = _lane4(a_log[l]), _lane4(dt_bias[l]), _row(dn_norm_g[l])
        qd, kd, vd, gb = _gdn_prep_fwd(proj, sc_f[l], alv, dtv, f"gdn_prep_fwd{l}")
        gbt = gb[:, :8].reshape(nc, CHUNK, 8).transpose(0, 2, 1)
        u, w, qe, ke, pm, tmi, eg = _gdn_chunk_fwd(qd, kd, vd, gb, gbt, f"gdn_chunk_fwd{l}")
        yc, o, vn, ss = _gdn_scan_fwd(u, w, qe, ke, pm, eg, proj, dng, f"gdn_scan_fwd{l}")
        xs.append(_merge_fwd(xl, proj, ya, yb, yc, wa_f[l], wb_f[l], wc_f[l], wo_f[l], gate[l], f"merge_fwd{l}"))
        saved.append((h, proj, ya, yb, yc, qd, kd, vd, gb, gbt, ss, alv, dtv, dng, w, qe, ke, pm, tmi, eg, o, vn))

    dout, lsum = _loss_head(xs[-1], loss_target[0], "loss_head")

    small = {name: [None] * DEPTH for name in _SMALL}
    big_parts = [None] * DEPTH
    core = jnp.reshape(ci, (1,)).astype(jnp.int32)
    for l in reversed(range(DEPTH)):
        h, proj, ya, yb, yc, qd, kd, vd, gb, gbt, ss, alv, dtv, dng, w, qe, ke, pm, tmi, eg, o, vn = saved[l]
        dproj, dya, dyb, dyc, g_wa, g_wb, g_wc, g_wo, g_gate = _merge_bwd(
            dout, proj, ya, yb, yc, wa_f[l], wb_f[l], wc_f[l], wo_f[l], gate[l], f"merge_bwd{l}")
        dproj, dkv_a, g_q, g_k, g_s = _attn_bwd(dproj, proj, dya, q_norm_g[l], k_norm_g[l], sinks[l], f"attn_bwd{l}")
        dproj, du1, g_pw2, st_b = _conf_bwd1(dproj, proj, dyb, dw_f[l], dw_b[l], ln_g[l], ln_b[l], pw2_f[l], pw2_b[l],
                                             f"conf_bwd_a{l}")
        dproj, g_dw = _conf_bwd2(dproj, proj, du1, dw_f[l], f"conf_bwd_b{l}")
        dproj, do, dvn, dsn, g_dn = _gdn_scan_bwd(dproj, w, qe, ke, pm, eg, o, proj, dyc, dng, f"gdn_scan_bwd{l}")
        dqd, dkd, dvd, dgb = _gdn_chunk_grad(qd, kd, vd, gb, gbt, tmi, ss, dsn, do, dvn, vn, f"gdn_chunk_bwd{l}")
        dproj, dpre, st_c = _gdn_prep_bwd1(dproj, proj, dqd, dkd, dvd, dgb, dkv_a, sc_f[l], alv, dtv,
                                           f"gdn_prep_bwd_a{l}")
        dproj, g_sc = _gdn_prep_bwd2(dproj, proj, dpre, sc_f[l], f"gdn_prep_bwd_b{l}")
        dh = _mm(dproj, wp[l], tb=True, tm=min(1024, T), tn=D, tk=1152, name=f"d_h{l}")
        g_wp = _mm(h, dproj, ta=True, tm=D, tn=1152, tk=min(1024, T), name=f"d_w_in{l}")
        dout, st_n = _norm_bwd(dh, xs[l], dout, norm_g[l], scale[l], f"norm_bwd{l}")
        by_dest = [_split_w_in_grad(g_wp, f"split_w_in_grad{l}"), g_pw2.reshape(N_DEV, -1, 512).astype(BF16)]
        by_dest += [g.reshape(512, N_DEV, -1).transpose(1, 0, 2).astype(BF16) for g in (g_wa, g_wb, g_wc)]
        by_dest.append(g_wo.reshape(N_DEV, -1, D).astype(BF16))
        by_dest = [b.reshape(4, 2, -1, b.shape[-1]) for b in by_dest]
        from_sibling = _exchange_core(by_dest, f"exchange_grads_core{l}", collective_id=2 + 2 * l)
        chip_sums = [_pair_sum(core, b, r, f"pair_sum{l}_{i}") for i, (b, r) in enumerate(zip(by_dest, from_sibling))]
        big_parts[l] = _exchange_chips(chip_sums, f"exchange_grads_chips{l}", collective_id=3 + 2 * l)
        if l > 0:
            dout, chip_sums = lax.optimization_barrier((dout, chip_sums))
        for name, g in (("b_ada", jnp.concatenate([st_n[0], st_n[1], g_gate[0]])), ("norm_g", st_n[2]),
                        ("q_norm_g", g_q.reshape(ATT_HEADS, ATT_HD).sum(0)), ("k_norm_g", g_k.reshape(2, ATT_HD).sum(0)),
                        ("sinks", g_s[0]), ("dw_b", st_b[3]),
                        ("ln_g", st_b[1]), ("ln_b", st_b[2]), ("pw2_b", st_b[0]), ("a_log", st_c[0, :4]),
                        ("dt_bias", st_c[1, :4]), ("dn_norm_g", g_dn[0]), ("dw_w", g_dw[:CONV_K]),
                        ("sconv_w", g_sc[:DN_K])):
            small[name][l] = g
    grad_x = dout[None]

    names = list(_SMALL)
    gathered = _all_gather([jnp.stack(small[n]) for n in names] + [lsum], "gather_small_grads")
    gparts = dict(zip(names, gathered))
    loss = 0.5 * jnp.sum(jnp.sum(gathered[-1], axis=(1, 2))) / D
    dmod_my = lax.dynamic_slice(gparts["b_ada"], (0, 0, me * ada_cols), (N_DEV, DEPTH, ada_cols)).transpose(1, 0, 2)
    g_w_ada = _ada_bwd(c_all, dmod_my, "ada_bwd")
    gparts["dw_w"] = lax.dynamic_slice(gparts["dw_w"], (0, 0, 0, me * dw_cols), (N_DEV, DEPTH, CONV_K, dw_cols))
    gparts["sconv_w"] = lax.dynamic_slice(gparts["sconv_w"], (0, 0, 0, me * sc_cols), (N_DEV, DEPTH, DN_K, sc_cols))
    env = dict(b_ada=(b_ada, m_b_ada, v_b_ada), norm_g=(norm_g, m_norm_g, v_norm_g),
               q_norm_g=(q_norm_g, m_q_norm_g, v_q_norm_g), k_norm_g=(k_norm_g, m_k_norm_g, v_k_norm_g),
               sinks=(sinks, m_sinks, v_sinks), dw_b=(dw_b, m_dw_b, v_dw_b), ln_g=(ln_g, m_ln_g, v_ln_g),
               ln_b=(ln_b, m_ln_b, v_ln_b), pw2_b=(pw2_b, m_pw2_b, v_pw2_b), a_log=(a_log, m_a_log, v_a_log),
               dt_bias=(dt_bias, m_dt_bias, v_dt_bias), dn_norm_g=(dn_norm_g, m_dn_norm_g, v_dn_norm_g),
               dw_w=(dw_w, m_dw_w, v_dw_w), sconv_w=(sconv_w, m_sconv_w, v_sconv_w))
    upd = _sum_adamw_many([gparts[n] for n in names], [env[n][0] for n in names], [env[n][1] for n in names],
                          [env[n][2] for n in names], "sum_adamw_small")
    g_small, d_small, m_small, v_small = (dict(zip(names, u)) for u in upd)

    d_ada, nm_ada, nv_ada = (u.reshape(w_ada.shape) for u in
                             _adamw(flat2(w_ada), flat2(g_w_ada), flat2(m_w_ada), flat2(v_w_ada), "adamw_w_ada"))

    res = [_sum_adamw(p0, p1, flat2(w), flat2(m), flat2(v), f"sum_adamw{i}")
           for i, (p0, p1, w, m, v) in enumerate(zip(big_parts[0], big_parts[1], big_w, big_m, big_v))]
    g_big, d_big, m_big, v_big = ([r[k].reshape(w.shape) for r, w in zip(res, big_w)] for k in range(4))

    order = ("w_ada", "b_ada", "norm_g", "w_in", "q_norm_g", "k_norm_g", "sinks", "dw_w", "dw_b", "ln_g", "ln_b",
             "pw2_w", "pw2_b", "sconv_w", "a_log", "dt_bias", "dn_norm_g", "w_proj_a", "w_proj_b", "w_proj_c", "w_out")
    big_names = ("w_in", "pw2_w", "w_proj_a", "w_proj_b", "w_proj_c", "w_out")

    def pick(kind):
        src_small = (g_small, d_small, m_small, v_small)[kind]
        src_big = (g_big, d_big, m_big, v_big)[kind]
        src_ada = (g_w_ada, d_ada, nm_ada, nv_ada)[kind]
        return [src_ada if n == "w_ada" else src_big[big_names.index(n)] if n in big_names else src_small[n]
                for n in order]

    return (loss, grad_x, *pick(0), *pick(1), *pick(2), *pick(3))
```

```python
import functools
import math

import jax
import jax.numpy as jnp
import numpy as np
from jax import lax
from jax.experimental import pallas as pl
from jax.experimental.pallas import tpu as pltpu
from jax.experimental.pallas import tpu_sc as plsc

F32 = jnp.float32
BF16 = jnp.bfloat16
HI = lax.Precision.HIGHEST

N_DEV = 8
D = 1024
DEPTH = 2
EPS = 1e-6
NEG_INF = -1e30
WINDOW = 128
ATT_HEADS = 8
ATT_HD = 64
CONV_K = 31
DN_HEADS = 4
DN_K = 4
CHUNK = 64
D_IN = 7944
VMEM_LIMIT = 56 * 1024 * 1024

C_MG, C_QA, C_ZA, C_ZB, C_QC, C_KC, C_VC, C_GV, C_GG, C_ZC, C_KA, C_VA, C_AB, NP = (
    0, 3072, 3584, 4096, 4608, 5120, 5632, 6144, 6656, 7168, 7680, 7808, 7936, 8064)
_PAD_FROM = ((4872, 7944), (0, 512), (768, 1280), (2304, 2816), (2816, 4352), (1280, 2304), (4360, 4872),
             (512, 768), (4352, 4360))

ALIBI = tuple(float(2.0 ** (-8.0 * (h + 1) / ATT_HEADS)) for h in range(ATT_HEADS))

ADAM_LR, ADAM_B1, ADAM_B2, ADAM_EPS, ADAM_WD, ADAM_STEP = 0.001, 0.9, 0.999, 1e-08, 0.01, 10


def _cparams(sem=None):
    return pltpu.CompilerParams(dimension_semantics=sem, vmem_limit_bytes=VMEM_LIMIT)


def _sig(x):
    return jax.nn.sigmoid(x)


def _silu(x):
    return x * _sig(x)


def _dsilu(x):
    s = _sig(x)
    return s * (1.0 + x * (1.0 - s))


def _dot(a, b, dims=((1,), (0,)), precision=None):
    return lax.dot_general(a, b, (dims, ((), ())), preferred_element_type=F32, precision=precision)


def _bdot(a, b, dims=((1,), (0,))):
    return _dot(a.astype(BF16), b.astype(BF16), dims)


NN, NT, TN = ((1,), (0,)), ((1,), (1,)), ((0,), (0,))


def _row(v):
    return v.reshape(1, -1)


def _mm(a, b, *, ta=False, tb=False, tm, tn, tk, name):
    M, K = (a.shape[1], a.shape[0]) if ta else a.shape
    N = b.shape[0] if tb else b.shape[1]
    assert M % tm == 0 and N % tn == 0 and K % tk == 0, (M, N, K, tm, tn, tk)
    nk = K // tk
    dims = ((0 if ta else 1,), (1 if tb else 0,))

    def body(a_ref, b_ref, o_ref):
        k = pl.program_id(2)
        part = _bdot(a_ref[...], b_ref[...], dims)

        @pl.when(k == 0)
        def _():
            o_ref[...] = part

        @pl.when(k > 0)
        def _():
            o_ref[...] += part

    a_spec = pl.BlockSpec((tk, tm), lambda i, j, k: (k, i)) if ta else pl.BlockSpec((tm, tk), lambda i, j, k: (i, k))
    b_spec = pl.BlockSpec((tn, tk), lambda i, j, k: (j, k)) if tb else pl.BlockSpec((tk, tn), lambda i, j, k: (k, j))
    return pl.pallas_call(
        body, name=name, grid=(M // tm, N // tn, nk),
        in_specs=[a_spec, b_spec], out_specs=pl.BlockSpec((tm, tn), lambda i, j, k: (i, j)),
        out_shape=jax.ShapeDtypeStruct((M, N), F32),
        compiler_params=_cparams(("parallel", "parallel", "arbitrary")),
    )(a, b)


def _norm_fwd(x, norm_g, scale, shift, name):
    T = x.shape[0]
    tm = min(512, T)

    def body(x_ref, g_ref, sc_ref, sh_ref, h_ref):
        xv = x_ref[...]
        r = lax.rsqrt(jnp.mean(xv * xv, axis=-1, keepdims=True) + EPS)
        h_ref[...] = ((xv * r) * g_ref[...] * (1.0 + sc_ref[...]) + sh_ref[...]).astype(BF16)

    vec = pl.BlockSpec((1, D), lambda i: (0, 0))
    return pl.pallas_call(
        body, name=name, grid=(T // tm,),
        in_specs=[pl.BlockSpec((tm, D), lambda i: (i, 0)), vec, vec, vec],
        out_specs=pl.BlockSpec((tm, D), lambda i: (i, 0)),
        out_shape=jax.ShapeDtypeStruct((T, D), BF16),
        compiler_params=_cparams(("parallel",)),
    )(x, _row(norm_g), _row(scale), _row(shift))


def _norm_bwd(dh, x, dres, norm_g, scale, name):
    T = x.shape[0]
    tm = min(512, T)

    def body(dh_ref, x_ref, dr_ref, g_ref, sc_ref, dx_ref, st_ref):
        i = pl.program_id(0)
        xv, dhv = x_ref[...], dh_ref[...]
        r = lax.rsqrt(jnp.mean(xv * xv, axis=-1, keepdims=True) + EPS)
        xh = xv * r
        g, s1 = g_ref[...], 1.0 + sc_ref[...]
        dxh = dhv * (g * s1)
        dx_ref[...] = dr_ref[...] + r * (dxh - xh * jnp.mean(dxh * xh, axis=-1, keepdims=True))
        dhx = dhv * xh
        upd = jnp.concatenate([jnp.sum(dhv, axis=0, keepdims=True), jnp.sum(dhx * g, axis=0, keepdims=True),
                               jnp.sum(dhx * s1, axis=0, keepdims=True), jnp.zeros((5, D), F32)], axis=0)

        @pl.when(i == 0)
        def _():
            st_ref[...] = upd

        @pl.when(i > 0)
        def _():
            st_ref[...] += upd

    vec = pl.BlockSpec((1, D), lambda i: (0, 0))
    blk = pl.BlockSpec((tm, D), lambda i: (i, 0))
    return pl.pallas_call(
        body, name=name, grid=(T // tm,),
        in_specs=[blk, blk, blk, vec, vec],
        out_specs=[blk, pl.BlockSpec((8, D), lambda i: (0, 0))],
        out_shape=[jax.ShapeDtypeStruct((T, D), F32), jax.ShapeDtypeStruct((8, D), F32)],
        compiler_params=_cparams(("arbitrary",)),
    )(dh, x, dres, _row(norm_g), _row(scale))


def _rms(x, g):
    r = lax.rsqrt(jnp.mean(x * x, axis=-1, keepdims=True) + EPS)
    return x * r, r


def _head_mean_matrix():
    head = np.arange(ATT_HEADS * ATT_HD) // ATT_HD
    return jnp.asarray((head[:, None] == head[None, :]) * (1.0 / ATT_HD), BF16)


def _head_rms(x, hm):
    r = lax.rsqrt(_dot_exact(x * x, hm) + EPS)
    return x * r, r


def _head_rms_bwd(dy, xh, r, g, hm):
    dxh = dy * g
    return r * (dxh - xh * _dot_exact(dxh * xh, hm)), dy * xh


def _attn_mask(n):
    qi = lax.broadcasted_iota(jnp.int32, (WINDOW, 2 * WINDOW), 0)
    kj = lax.broadcasted_iota(jnp.int32, (WINDOW, 2 * WINDOW), 1)
    dist = qi + WINDOW - kj
    valid = (dist >= 0) & (dist < WINDOW) & ((n > 0) | (kj >= WINDOW))
    return valid, dist.astype(F32)


def _attn_probs(s, h, sink, valid, distf):
    s = s - ALIBI[h] * distf
    s = jnp.where(valid, s, NEG_INF)
    m = jnp.maximum(jnp.max(s, axis=-1, keepdims=True), sink)
    p = jnp.exp(s - m)
    es = jnp.exp(sink - m)
    den = jnp.sum(p, axis=-1, keepdims=True) + es
    return p / den, es / den


def _attn_fwd(proj, q_norm_g, k_norm_g, sinks, name):
    T = proj.shape[0]
    nb = T // WINDOW

    def body(sink_ref, q_ref, z_ref, kc_ref, kp_ref, vc_ref, vp_ref, qg_ref, kg_ref, hm_ref, o_ref):
        n = pl.program_id(0)
        valid, distf = _attn_mask(n)
        k2 = jnp.concatenate([kp_ref[...], kc_ref[...]], axis=0)
        v2 = jnp.concatenate([vp_ref[...], vc_ref[...]], axis=0).astype(BF16)
        kn = (_head_rms(k2, hm_ref[:128, :128])[0] * kg_ref[...]).astype(BF16)
        qn = ((_head_rms(q_ref[...], hm_ref[...])[0] * qg_ref[...]) * (ATT_HD ** -0.5)).astype(BF16)

        def head(h):
            sl, gsl = slice(64 * h, 64 * h + 64), slice(64 * (h // 4), 64 * (h // 4) + 64)
            s = _dot(qn[:, sl], kn[:, gsl], NT)
            yield
            p, _ = _attn_probs(s, h, sink_ref[h], valid, distf)
            o_ref[:, sl] = _dot(p.astype(BF16), v2[:, gsl])
            yield

        _lockstep([head(h) for h in range(ATT_HEADS)])
        o_ref[...] = o_ref[...] * _silu(z_ref[...])

    prev = lambda n: jnp.maximum(n - 1, 0)
    return pl.pallas_call(
        body, name=name, grid=(nb,),
        in_specs=[pl.BlockSpec(memory_space=pltpu.SMEM),
                  pl.BlockSpec((WINDOW, 512), lambda n: (n, C_QA // 512)),
                  pl.BlockSpec((WINDOW, 512), lambda n: (n, C_ZA // 512)),
                  pl.BlockSpec((WINDOW, 128), lambda n: (n, C_KA // 128)),
                  pl.BlockSpec((WINDOW, 128), lambda n: (prev(n), C_KA // 128)),
                  pl.BlockSpec((WINDOW, 128), lambda n: (n, C_VA // 128)),
                  pl.BlockSpec((WINDOW, 128), lambda n: (prev(n), C_VA // 128)),
                  pl.BlockSpec((1, 512), lambda n: (0, 0)), pl.BlockSpec((1, 128), lambda n: (0, 0)),
                  pl.BlockSpec((512, 512), lambda n: (0, 0))],
        out_specs=pl.BlockSpec((WINDOW, 512), lambda n: (n, 0)),
        out_shape=jax.ShapeDtypeStruct((T, 512), F32),
        compiler_params=_cparams(("parallel",)),
    )(sinks, proj, proj, proj, proj, proj, proj, _row(jnp.tile(q_norm_g, ATT_HEADS)), _row(jnp.tile(k_norm_g, 2)),
      _head_mean_matrix())


def _rms_bwd(dy, xh, r, g):
    dxh = dy * g
    return r * (dxh - xh * jnp.mean(dxh * xh, axis=-1, keepdims=True)), dy * xh


def _attn_bwd(dproj, proj, dya, q_norm_g, k_norm_g, sinks, name):
    T = proj.shape[0]
    nb = T // WINDOW

    def body(sink_ref, dp_any, q_ref, z_ref, kc_ref, kp_ref, vc_ref, vp_ref, dy_ref, qg_ref, kg_ref, hm_ref,
             dqz_ref, dkv_ref, gq_ref, gk_ref, gs_ref, ck_ref, cv_ref, o_sc, dq_sc):
        n = pl.program_id(0)

        @pl.when(n == 0)
        def _():
            gq_ref[...] = jnp.zeros_like(gq_ref)
            gk_ref[...] = jnp.zeros_like(gk_ref)
            gs_ref[...] = jnp.zeros_like(gs_ref)
            ck_ref[...] = jnp.zeros_like(ck_ref)
            cv_ref[...] = jnp.zeros_like(cv_ref)

        lane8 = lax.broadcasted_iota(jnp.int32, (1, 8), 1)

        @pl.when(n < nb)
        def _():
            valid, distf = _attn_mask(n)
            k2 = jnp.concatenate([kp_ref[...], kc_ref[...]], axis=0)
            v2 = jnp.concatenate([vp_ref[...], vc_ref[...]], axis=0).astype(BF16)
            kn = (_head_rms(k2, hm_ref[:128, :128])[0] * kg_ref[...]).astype(BF16)
            qh, qr = _head_rms(q_ref[...], hm_ref[...])
            qn = ((qh * qg_ref[...]) * (ATT_HD ** -0.5)).astype(BF16)
            zs = z_ref[...]
            do_all = dy_ref[...] * _silu(zs)
            dob_all = do_all.astype(BF16)

            def head(h):
                sl, gsl = slice(64 * h, 64 * h + 64), slice(64 * (h // 4), 64 * (h // 4) + 64)
                s = _dot(qn[:, sl], kn[:, gsl], NT)
                dpm = _dot(dob_all[:, sl], v2[:, gsl], NT)
                yield
                p, ps = _attn_probs(s, h, sink_ref[h], valid, distf)
                pb = p.astype(BF16)
                o_sc[:, sl] = _dot(pb, v2[:, gsl])
                dvg = _dot(pb, dob_all[:, sl], TN)
                delta = jnp.sum(p * dpm, axis=-1, keepdims=True)
                ds = (p * (dpm - delta)).astype(BF16)
                gs = jnp.where(lane8 == h, -jnp.sum(ps * delta, axis=0, keepdims=True), 0.0)
                yield
                dkn = _dot(ds, qn[:, sl], TN)
                dq_sc[:, sl] = _dot(ds, kn[:, gsl])
                yield
                return dkn, dvg, gs

            res = _lockstep([head(h) for h in range(ATT_HEADS)])
            dqz_ref[:, 512:] = (dy_ref[...] * o_sc[...] * _dsilu(zs)).astype(BF16)
            dq, gq = _head_rms_bwd(dq_sc[...] * (ATT_HD ** -0.5), qh, qr, qg_ref[...], hm_ref[...])
            dqz_ref[:, :512] = dq.astype(BF16)
            gq_acc = jnp.sum(gq, axis=0, keepdims=True)
            gs_acc = sum(r[2] for r in res[1:]) + res[0][2]
            for g in range(2):
                dkn = (res[4 * g][0] + res[4 * g + 1][0]) + (res[4 * g + 2][0] + res[4 * g + 3][0])
                dvg = (res[4 * g][1] + res[4 * g + 1][1]) + (res[4 * g + 2][1] + res[4 * g + 3][1])
                ksl = slice(64 * g, 64 * g + 64)
                vsl = slice(128 + 64 * g, 128 + 64 * g + 64)
                dkv_ref[:, ksl] = ck_ref[:, ksl] + dkn[:WINDOW]
                dkv_ref[:, vsl] = cv_ref[:, ksl] + dvg[:WINDOW]
                ck_ref[:, ksl] = dkn[WINDOW:]
                cv_ref[:, ksl] = dvg[WINDOW:]
            gq_ref[...] += gq_acc
            gs_ref[...] += gs_acc

        @pl.when(n == nb)
        def _():
            dkv_ref[:, :128] = ck_ref[...]
            dkv_ref[:, 128:] = cv_ref[...]

        @pl.when(n > 0)
        def _():
            hm = hm_ref[:128, :128]
            kh, kr = _head_rms(kp_ref[...], hm)
            dk, gk = _head_rms_bwd(dkv_ref[:, :128], kh, kr, kg_ref[...], hm)
            dkv_ref[:, :128] = dk
            gk_ref[...] += jnp.sum(gk, axis=0, keepdims=True)

    cur = lambda n: jnp.minimum(n, nb - 1)
    prev = lambda n: jnp.maximum(n - 1, 0)
    small = lambda w: pl.BlockSpec((1, w), lambda n: (0, 0))
    return pl.pallas_call(
        body, name=name, grid=(nb + 1,),
        in_specs=[pl.BlockSpec(memory_space=pltpu.SMEM), pl.BlockSpec(memory_space=pl.ANY),
                  pl.BlockSpec((WINDOW, 512), lambda n: (cur(n), C_QA // 512)),
                  pl.BlockSpec((WINDOW, 512), lambda n: (cur(n), C_ZA // 512)),
                  pl.BlockSpec((WINDOW, 128), lambda n: (cur(n), C_KA // 128)),
                  pl.BlockSpec((WINDOW, 128), lambda n: (prev(n), C_KA // 128)),
                  pl.BlockSpec((WINDOW, 128), lambda n: (cur(n), C_VA // 128)),
                  pl.BlockSpec((WINDOW, 128), lambda n: (prev(n), C_VA // 128)),
                  pl.BlockSpec((WINDOW, 512), lambda n: (cur(n), 0)),
                  small(512), small(128), pl.BlockSpec((512, 512), lambda n: (0, 0))],
        out_specs=[pl.BlockSpec((WINDOW, 1024), lambda n: (cur(n), C_QA // 1024)),
                   pl.BlockSpec((WINDOW, 256), lambda n: (prev(n), 0)),
                   small(512), small(128), small(8)],
        out_shape=[jax.ShapeDtypeStruct(dproj.shape, BF16), jax.ShapeDtypeStruct((T, 256), F32),
                   jax.ShapeDtypeStruct((1, 512), F32), jax.ShapeDtypeStruct((1, 128), F32),
                   jax.ShapeDtypeStruct((1, 8), F32)],
        scratch_shapes=[pltpu.VMEM((WINDOW, 128), F32), pltpu.VMEM((WINDOW, 128), F32),
                        pltpu.VMEM((WINDOW, 512), F32), pltpu.VMEM((WINDOW, 512), F32)],
        input_output_aliases={1: 0},
        compiler_params=_cparams(("arbitrary",)),
    )(sinks, dproj, proj, proj, proj, proj, proj, proj, dya, _row(jnp.tile(q_norm_g, ATT_HEADS)),
      _row(jnp.tile(k_norm_g, 2)), _head_mean_matrix())


HALO_B = 32


def _conf_specs(T, tm):
    r = tm // HALO_B
    cur = lambda c: pl.BlockSpec((tm, 512), lambda i: (i, c // 512))
    prev = lambda c: pl.BlockSpec((HALO_B, 512), lambda i: (jnp.maximum(i * r - 1, 0), c // 512))
    return cur, prev


SUB = 8
ROW_CHUNK = 64


def _shifted_copies(ext_ref, sh_ref):
    total = ext_ref.shape[0]
    for r in range(SUB):
        rows = total if r == 0 else total - SUB
        sh_ref[r, :rows, :] = ext_ref[pl.ds(r, rows), :]


def _taps_by_shift(offsets):
    groups = {}
    for k, o in enumerate(offsets):
        q, r = divmod(o, SUB)
        groups.setdefault(r, []).append((k, q))
    return groups


def _conv_taps(sh_ref, w_ref, offsets, out_ref, init):
    groups = _taps_by_shift(offsets)

    def chunk(ci, carry):
        r0 = pl.multiple_of(ci * ROW_CHUNK, ROW_CHUNK)
        acc = jnp.zeros((ROW_CHUNK, out_ref.shape[1]), F32) + init
        for r, taps in groups.items():
            win = sh_ref[r, pl.ds(r0, ROW_CHUNK + SUB * max(q for _, q in taps)), :]
            for k, q in taps:
                acc = acc + w_ref[k:k + 1, :] * win[SUB * q:SUB * q + ROW_CHUNK]
        out_ref[pl.ds(r0, ROW_CHUNK), :] = acc
        return carry

    lax.fori_loop(0, out_ref.shape[0] // ROW_CHUNK, chunk, 0)


def _conv_weight_grad(sh_ref, d_ref, offsets):
    tm, width = d_ref.shape
    out = [None] * len(offsets)
    for r, taps in _taps_by_shift(offsets).items():
        def chunk(ci, accs, r=r, taps=taps):
            r0 = pl.multiple_of(ci * ROW_CHUNK, ROW_CHUNK)
            d = d_ref[pl.ds(r0, ROW_CHUNK), :]
            win = sh_ref[r, pl.ds(r0, ROW_CHUNK + SUB * max(q for _, q in taps)), :]
            return tuple(a + jnp.sum((d * win[SUB * q:SUB * q + ROW_CHUNK]).reshape(ROW_CHUNK // SUB, SUB, width),
                                     axis=0) for a, (_, q) in zip(accs, taps))

        accs = lax.fori_loop(0, tm // ROW_CHUNK, chunk, tuple(jnp.zeros((SUB, width), F32) for _ in taps))
        for a, (k, _) in zip(accs, taps):
            out[k] = jnp.sum(a, axis=0, keepdims=True)
    return out


def _conf_scratch(tm):
    return [pltpu.VMEM((tm + HALO_B, 512), F32), pltpu.VMEM((SUB, tm + HALO_B, 512), F32), pltpu.VMEM((tm, 512), F32)]


def _conf_core(i, tm, gv_ref, gg_ref, gvp_ref, ggp_ref, w_ref, b_ref, lg_ref, lb_ref, pw_ref, pb_ref, ext_ref, sh_ref,
               cv_ref):
    up = gvp_ref[...] * _sig(ggp_ref[...])
    ext_ref[:HALO_B] = jnp.where(i > 0, up, 0.0)
    ext_ref[HALO_B:] = gv_ref[...] * _sig(gg_ref[...])
    _shifted_copies(ext_ref, sh_ref)
    _conv_taps(sh_ref, w_ref, [HALO_B - CONV_K + 1 + k for k in range(CONV_K)], cv_ref, b_ref[...])
    acc = cv_ref[...]
    mu = jnp.mean(acc, axis=-1, keepdims=True)
    xc = acc - mu
    rstd = lax.rsqrt(jnp.mean(xc * xc, axis=-1, keepdims=True) + EPS)
    xh = xc * rstd
    u2 = xh * lg_ref[...] + lb_ref[...]
    u3 = _silu(u2)
    ypre = _bdot(u3, pw_ref[...]) + pb_ref[...]
    return xh, rstd, u2, u3, ypre


def _conf_fwd(proj, dw_w, dw_b, ln_g, ln_b, pw2, pw2_b, name):
    T = proj.shape[0]
    tm = min(512, T)
    cur, prev = _conf_specs(T, tm)

    def body(gv_ref, gg_ref, gvp_ref, ggp_ref, zb_ref, w_ref, b_ref, lg_ref, lb_ref, pw_ref, pb_ref, o_ref, *scratch):
        i = pl.program_id(0)
        ypre = _conf_core(i, tm, gv_ref, gg_ref, gvp_ref, ggp_ref, w_ref, b_ref, lg_ref, lb_ref, pw_ref, pb_ref,
                          *scratch)[4]
        o_ref[...] = ypre * _silu(zb_ref[...])

    full = lambda s: pl.BlockSpec(s, lambda i: (0, 0))
    return pl.pallas_call(
        body, name=name, grid=(T // tm,),
        in_specs=[cur(C_GV), cur(C_GG), prev(C_GV), prev(C_GG), cur(C_ZB), full((CONV_K, 512)), full((1, 512)),
                  full((1, 512)), full((1, 512)), full((512, 512)), full((1, 512))],
        out_specs=pl.BlockSpec((tm, 512), lambda i: (i, 0)),
        out_shape=jax.ShapeDtypeStruct((T, 512), F32),
        scratch_shapes=_conf_scratch(tm),
        compiler_params=_cparams(("parallel",)),
    )(proj, proj, proj, proj, proj, dw_w, _row(dw_b), _row(ln_g), _row(ln_b), pw2, _row(pw2_b))


def _conf_bwd1(dproj, proj, dyb, dw_w, dw_b, ln_g, ln_b, pw2, pw2_b, name):
    T = proj.shape[0]
    tm = min(512, T)
    cur, prev = _conf_specs(T, tm)

    def body(dp_any, gv_ref, gg_ref, gvp_ref, ggp_ref, zb_ref, dy_ref, w_ref, b_ref, lg_ref, lb_ref, pw_ref, pb_ref,
             dzb_ref, du1_ref, gpw_ref, st_ref, *scratch):
        i = pl.program_id(0)
        xh, rstd, u2, u3, ypre = _conf_core(i, tm, gv_ref, gg_ref, gvp_ref, ggp_ref, w_ref, b_ref, lg_ref, lb_ref,
                                            pw_ref, pb_ref, *scratch)
        zb, dy = zb_ref[...], dy_ref[...]
        dzb_ref[...] = (dy * ypre * _dsilu(zb)).astype(BF16)
        dyp = dy * _silu(zb)
        du2 = _bdot(dyp, pw_ref[...], NT) * _dsilu(u2)
        dxh = du2 * lg_ref[...]
        du1 = rstd * (dxh - jnp.mean(dxh, axis=-1, keepdims=True) - xh * jnp.mean(dxh * xh, axis=-1, keepdims=True))
        du1_ref[...] = du1
        gpw = _bdot(u3, dyp, TN)
        rs = lambda a: jnp.sum(a, axis=0, keepdims=True)
        upd = jnp.concatenate([rs(dyp), rs(du2 * xh), rs(du2), rs(du1), jnp.zeros((4, 512), F32)], axis=0)

        @pl.when(i == 0)
        def _():
            gpw_ref[...] = gpw
            st_ref[...] = upd

        @pl.when(i > 0)
        def _():
            gpw_ref[...] += gpw
            st_ref[...] += upd

    full = lambda s: pl.BlockSpec(s, lambda i: (0, 0))
    blk = pl.BlockSpec((tm, 512), lambda i: (i, 0))
    return pl.pallas_call(
        body, name=name, grid=(T // tm,),
        in_specs=[pl.BlockSpec(memory_space=pl.ANY), cur(C_GV), cur(C_GG), prev(C_GV), prev(C_GG), cur(C_ZB), blk,
                  full((CONV_K, 512)), full((1, 512)), full((1, 512)), full((1, 512)), full((512, 512)), full((1, 512))],
        out_specs=[cur(C_ZB), blk, full((512, 512)), full((8, 512))],
        out_shape=[jax.ShapeDtypeStruct(dproj.shape, BF16), jax.ShapeDtypeStruct((T, 512), F32),
                   jax.ShapeDtypeStruct((512, 512), F32), jax.ShapeDtypeStruct((8, 512), F32)],
        scratch_shapes=_conf_scratch(tm),
        input_output_aliases={0: 0},
        compiler_params=_cparams(("arbitrary",)),
    )(dproj, proj, proj, proj, proj, proj, dyb, dw_w, _row(dw_b), _row(ln_g), _row(ln_b), pw2, _row(pw2_b))


def _conf_bwd2(dproj, proj, du1, dw_w, name):
    T = proj.shape[0]
    tm = min(512, T)
    nt = T // tm
    r = tm // HALO_B
    cur, prev = _conf_specs(T, tm)

    def body(dp_any, gv_ref, gg_ref, gvp_ref, ggp_ref, du_ref, dun_ref, w_ref, dglu_ref, gw_ref, ext_ref, sh_ref,
             cv_ref):
        i = pl.program_id(0)
        gv, sg = gv_ref[...], _sig(gg_ref[...])
        ext_ref[:HALO_B] = jnp.where(i > 0, gvp_ref[...] * _sig(ggp_ref[...]), 0.0)
        ext_ref[HALO_B:] = gv * sg
        _shifted_copies(ext_ref, sh_ref)
        rows = _conv_weight_grad(sh_ref, du_ref, [HALO_B - CONV_K + 1 + k for k in range(CONV_K)])
        rows.append(jnp.zeros((1, 512), F32))
        gw = jnp.concatenate(rows, axis=0)
        ext_ref[:tm] = du_ref[...]
        ext_ref[tm:] = jnp.where(i < nt - 1, dun_ref[...], 0.0)
        _shifted_copies(ext_ref, sh_ref)
        _conv_taps(sh_ref, w_ref, [CONV_K - 1 - k for k in range(CONV_K)], cv_ref, 0.0)
        du0 = cv_ref[...]
        dglu_ref[:, :512] = (du0 * sg).astype(BF16)
        dglu_ref[:, 512:] = (du0 * gv * sg * (1.0 - sg)).astype(BF16)

        @pl.when(i == 0)
        def _():
            gw_ref[...] = gw

        @pl.when(i > 0)
        def _():
            gw_ref[...] += gw

    full = lambda s: pl.BlockSpec(s, lambda i: (0, 0))
    return pl.pallas_call(
        body, name=name, grid=(nt,),
        in_specs=[pl.BlockSpec(memory_space=pl.ANY), cur(C_GV), cur(C_GG), prev(C_GV), prev(C_GG),
                  pl.BlockSpec((tm, 512), lambda i: (i, 0)),
                  pl.BlockSpec((HALO_B, 512), lambda i: (jnp.minimum((i + 1) * r, T // HALO_B - 1), 0)),
                  full((CONV_K, 512))],
        out_specs=[pl.BlockSpec((tm, 1024), lambda i: (i, C_GV // 1024)), full((32, 512))],
        out_shape=[jax.ShapeDtypeStruct(dproj.shape, BF16), jax.ShapeDtypeStruct((32, 512), F32)],
        scratch_shapes=_conf_scratch(tm),
        input_output_aliases={0: 0},
        compiler_params=_cparams(("arbitrary",)),
    )(dproj, proj, proj, proj, proj, du1, du1, dw_w)


HALO_C = 8
QKV_C = 1536


def _softplus(x):
    return jnp.maximum(x, 0.0) + jnp.log1p(jnp.exp(-jnp.abs(x)))


def _gdn_conv(i, tm, x_ref, xp_ref, w_ref, ext_ref):
    ext_ref[:HALO_C] = jnp.where(i > 0, xp_ref[...], 0.0)
    ext_ref[HALO_C:] = x_ref[...]
    pre = jnp.zeros((tm, QKV_C), F32)
    for k in range(DN_K):
        pre = pre + w_ref[k:k + 1, :] * ext_ref[pl.ds(HALO_C - DN_K + 1 + k, tm), :]
    return pre


def _gdn_specs(T, tm):
    r = tm // HALO_C
    cur = pl.BlockSpec((tm, QKV_C), lambda i: (i, C_QC // QKV_C))
    prev = pl.BlockSpec((HALO_C, QKV_C), lambda i: (jnp.maximum(i * r - 1, 0), C_QC // QKV_C))
    ab = pl.BlockSpec((tm, 128), lambda i: (i, C_AB // 128))
    return cur, prev, ab


def _gdn_prep_fwd(proj, sconv_w, alog_v, dtb_v, name):
    T = proj.shape[0]
    tm = min(512, T)
    cur, prev, ab = _gdn_specs(T, tm)

    def body(x_ref, xp_ref, ab_ref, w_ref, al_ref, dt_ref, q_ref, k_ref, v_ref, gb_ref, ext_ref):
        i = pl.program_id(0)
        y = _silu(_gdn_conv(i, tm, x_ref, xp_ref, w_ref, ext_ref))
        for h in range(DN_HEADS):
            sl = slice(128 * h, 128 * h + 128)
            qh, kh = y[:, sl], y[:, 512 + 128 * h:512 + 128 * h + 128]
            q_ref[:, sl] = qh * lax.rsqrt(jnp.sum(qh * qh, axis=-1, keepdims=True) + EPS) * (128 ** -0.5)
            k_ref[:, sl] = kh * lax.rsqrt(jnp.sum(kh * kh, axis=-1, keepdims=True) + EPS)
        v_ref[...] = y[:, 1024:]
        abv = ab_ref[...]
        lane = lax.broadcasted_iota(jnp.int32, (tm, 128), 1)
        g = -jnp.exp(al_ref[...]) * _softplus(abv + dt_ref[...])
        gb_ref[...] = jnp.where(lane < DN_HEADS, g, _sig(abv))

    full = lambda s: pl.BlockSpec(s, lambda i: (0, 0))
    blk = pl.BlockSpec((tm, 512), lambda i: (i, 0))
    return pl.pallas_call(
        body, name=name, grid=(T // tm,),
        in_specs=[cur, prev, ab, full((DN_K, QKV_C)), full((1, 128)), full((1, 128))],
        out_specs=[blk, blk, blk, pl.BlockSpec((tm, 128), lambda i: (i, 0))],
        out_shape=[jax.ShapeDtypeStruct((T, 512), F32)] * 3 + [jax.ShapeDtypeStruct((T, 128), F32)],
        scratch_shapes=[pltpu.VMEM((tm + HALO_C, QKV_C), F32)],
        compiler_params=_cparams(("parallel",)),
    )(proj, proj, proj, sconv_w, alog_v, dtb_v)


def _hdot(a, b, dims=NN):
    return _dot(a, b, dims, precision=HI)


def _lockstep(gens):
    results, live = [None] * len(gens), list(range(len(gens)))
    while live:
        for i in list(live):
            try:
                next(gens[i])
            except StopIteration as stop:
                results[i] = stop.value
                live.remove(i)
    return results


def _split(a):
    hi = a.astype(BF16)
    return hi, (a - hi.astype(F32)).astype(BF16)


def _dot_exact(a, b, dims=NN, split_left=True):
    x = (a if split_left else b).astype(F32)
    hi = x.astype(BF16)
    r = x - hi.astype(F32)
    mid = r.astype(BF16)
    lo = (r - mid.astype(F32)).astype(BF16)
    other = (b if split_left else a).astype(BF16)
    one = (lambda p: _dot(p, other, dims)) if split_left else (lambda p: _dot(other, p, dims))
    return (one(lo) + one(mid)) + one(hi)


def _dot3(a, b):
    (ah, al), (bh, bl) = a, b
    return _dot(ah, bh) + (_dot(ah, bl) + _dot(al, bh))


def _tri_inv(mats, eye):
    ps = [-a for a in mats]
    ts = [eye + p for p in ps]
    for _ in range(5):
        sp = [_split(p) for p in ps]
        ps = [_dot3(s, s) for s in sp]
        sp = [_split(p) for p in ps]
        ts = [t + _dot3(_split(t), s) for t, s in zip(ts, sp)]
    return ts


def _tri_consts():
    ii = lax.broadcasted_iota(jnp.int32, (CHUNK, CHUNK), 0)
    jj = lax.broadcasted_iota(jnp.int32, (CHUNK, CHUNK), 1)
    return ii >= jj, ii > jj, (ii == jj).astype(F32)


def _gdn_local(q, k, v, gcol, grow, bcol, lower, strict):
    dm = jnp.where(lower, jnp.exp(jnp.where(lower, gcol - grow, 0.0)), 0.0)
    kb = k * bcol
    a = jnp.where(strict, _bdot(kb, k, NT) * dm, 0.0)
    gc = jnp.exp(gcol)
    glast = grow[:, CHUNK - 1:CHUNK]
    return dict(q=q, k=k, v=v, bcol=bcol, gcol=gcol, glast=glast, dm=dm, kb=kb, a=a, gc=gc, vb=v * bcol,
                kbg=kb * gc, p=_bdot(q, k, NT) * dm, qe=q * gc, ke=k * jnp.exp(glast - gcol))


def _gdn_chunk_bwd(c, do, dvn, ds_new, lower, strict, ones):
    rs = lambda m: jnp.sum(m, axis=-1, keepdims=True)
    colsum = lambda m: _dot_exact(m, ones, TN)[:, :1]
    q, k, v, bcol, dm, tm, gc, s = c["q"], c["k"], c["v"], c["bcol"], c["dm"], c["tm"], c["gc"], c["s"]
    eg = jnp.exp(c["glast"])
    dqe = _bdot(do, s, NT)
    dp = jnp.where(lower, _bdot(do, c["vn"], NT), 0.0)
    dw = -_bdot(dvn, s, NT)
    dke = _bdot(c["vn"], ds_new, NT)
    dvb = _bdot(tm, dvn, TN)
    yield
    dglast = jnp.sum(rs(ds_new * s), axis=0, keepdims=True) * eg
    dk = dke * jnp.exp(c["glast"] - c["gcol"])
    r_ke = rs(dke * c["ke"])
    dglast = dglast + jnp.sum(r_ke, axis=0, keepdims=True)
    dgam = rs(dqe * c["qe"]) - r_ke
    dq = dqe * gc
    dpm = dp * dm
    mp = dp * c["p"]
    dq = dq + _bdot(dpm, k)
    dk = dk + _bdot(dpm, q, TN)
    dt = _bdot(dvn, c["vb"], NT) + _bdot(dw, c["kbg"], NT)
    dkbg = _bdot(tm, dw, TN)
    dgam = dgam + rs(mp) - colsum(mp)
    yield
    dkb = dkbg * gc
    dgam = dgam + rs(dkbg * c["kbg"])
    dat = _bdot(tm, dt, TN)
    yield
    da = jnp.where(strict, -_bdot(dat, tm, NT), 0.0)
    yield
    dam = da * dm
    ma = da * c["a"]
    dkb = dkb + _bdot(dam, k)
    dk = dk + _bdot(dam, c["kb"], TN)
    dgam = dgam + rs(ma) - colsum(ma)
    yield
    dk = dk + dkb * bcol
    dbeta = rs(dkb * k) + rs(dvb * v)
    dv = dvb * bcol
    row = lax.broadcasted_iota(jnp.int32, (CHUNK, 1), 0)
    dgam = dgam + jnp.where(row == CHUNK - 1, dglast, 0.0)
    dg = _dot_exact(lower, dgam, TN, split_left=False)
    return dq, dk, dv, dg, dbeta


SCAN_GROUP = 4
GROUP = 4


def _chunk_decay(gb_ref, gt_ref, lmat, g):
    rows = slice(CHUNK * g, CHUNK * g + CHUNK)
    return rows, _dot_exact(lmat, gb_ref[rows, :], split_left=False), _dot_exact(gt_ref[g], lmat, NT)


def _gdn_chunk_fwd(qd, kd, vd, gb, gbt, name):
    T = qd.shape[0]
    G = GROUP
    ng = T // (CHUNK * G)

    def body(q_ref, k_ref, v_ref, gb_ref, gt_ref, u_ref, w_ref, qe_ref, ke_ref, p_ref, t_ref, eg_ref):
        lower, strict, eye = _tri_consts()
        lmat = lower.astype(F32)
        decay = [_chunk_decay(gb_ref, gt_ref, lmat, g) for g in range(G)]
        chains = [(g, h) for g in range(G) for h in range(DN_HEADS)]
        cs = []
        for g, h in chains:
            rows, gcs, grs = decay[g]
            sl = slice(128 * h, 128 * h + 128)
            c = _gdn_local(q_ref[rows, sl], k_ref[rows, sl], v_ref[rows, sl], gcs[:, h:h + 1], grs[h:h + 1, :],
                           gb_ref[rows, DN_HEADS + h:DN_HEADS + h + 1], lower, strict)
            qe_ref[rows, sl] = c["qe"].astype(BF16)
            ke_ref[rows, sl] = c["ke"].astype(BF16)
            p_ref[rows, 64 * h:64 * h + 64] = c["p"].astype(BF16)
            eg_ref[g, h:h + 1, :] = jnp.broadcast_to(jnp.exp(c["glast"]), (1, 128))
            cs.append(c)
        tms = [t.astype(BF16) for t in _tri_inv([c["a"] for c in cs], eye)]
        us = [_dot(t, c["vb"].astype(BF16)) for t, c in zip(tms, cs)]
        ws = [_dot(t, c["kbg"].astype(BF16)) for t, c in zip(tms, cs)]
        for (g, h), tm, u, w in zip(chains, tms, us, ws):
            rows, sl = decay[g][0], slice(128 * h, 128 * h + 128)
            u_ref[rows, sl] = u
            w_ref[rows, sl] = w.astype(BF16)
            t_ref[rows, 64 * h:64 * h + 64] = tm
        for g in range(G):
            eg_ref[g, DN_HEADS:, :] = jnp.zeros((8 - DN_HEADS, 128), F32)

    blk = pl.BlockSpec((CHUNK * G, 512), lambda n: (n, 0))
    half = pl.BlockSpec((CHUNK * G, 256), lambda n: (n, 0))
    return pl.pallas_call(
        body, name=name, grid=(ng,),
        in_specs=[blk, blk, blk, pl.BlockSpec((CHUNK * G, 128), lambda n: (n, 0)),
                  pl.BlockSpec((G, 8, CHUNK), lambda n: (n, 0, 0))],
        out_specs=[blk, blk, blk, blk, half, half, pl.BlockSpec((G, 8, 128), lambda n: (n, 0, 0))],
        out_shape=[jax.ShapeDtypeStruct((T, 512), F32)] + [jax.ShapeDtypeStruct((T, 512), BF16)] * 3
        + [jax.ShapeDtypeStruct((T, 256), BF16)] * 2 + [jax.ShapeDtypeStruct((T // CHUNK, 8, 128), F32)],
        compiler_params=_cparams(("parallel",)),
    )(qd, kd, vd, gb, gbt)


def _gdn_scan_fwd(u, w, qe, ke, pm, eg, proj, dn_g, name):
    T = u.shape[0]
    nc = T // CHUNK
    G = SCAN_GROUP

    def body(u_ref, w_ref, qe_ref, ke_ref, p_ref, eg_ref, z_ref, ng_ref, y_ref, o_ref, vn_ref, ss_ref, s_ref):
        n = pl.program_id(0)

        @pl.when(n == 0)
        def _():
            s_ref[...] = jnp.zeros_like(s_ref)

        def head(j, h):
            rows, sl = slice(CHUNK * j, CHUNK * j + CHUNK), slice(128 * h, 128 * h + 128)
            s = s_ref[h]
            sb = s.astype(BF16)
            vn = u_ref[rows, sl] - _dot(w_ref[rows, sl], sb)
            qs = _dot(qe_ref[rows, sl], sb)
            yield
            vb = vn.astype(BF16)
            o = qs + _dot(p_ref[rows, 64 * h:64 * h + 64], vb)
            s_ref[h] = s * eg_ref[j, h:h + 1, :] + _dot(ke_ref[rows, sl], vb, TN)
            yield
            vn_ref[rows, sl] = vb
            o_ref[rows, sl] = o
            y_ref[rows, sl] = _rms(o, None)[0] * ng_ref[...] * _silu(z_ref[rows, sl])

        for j in range(G):
            ss_ref[j] = s_ref[...]
            _lockstep([head(j, h) for h in range(DN_HEADS)])

    blk = pl.BlockSpec((CHUNK * G, 512), lambda n: (n, 0))
    return pl.pallas_call(
        body, name=name, grid=(nc // G,),
        in_specs=[blk, blk, blk, blk, pl.BlockSpec((CHUNK * G, 256), lambda n: (n, 0)),
                  pl.BlockSpec((G, 8, 128), lambda n: (n, 0, 0)),
                  pl.BlockSpec((CHUNK * G, 512), lambda n: (n, C_ZC // 512)), pl.BlockSpec((1, 128), lambda n: (0, 0))],
        out_specs=[blk, blk, blk, pl.BlockSpec((G, DN_HEADS, 128, 128), lambda n: (n, 0, 0, 0))],
        out_shape=[jax.ShapeDtypeStruct((T, 512), F32), jax.ShapeDtypeStruct((T, 512), F32),
                   jax.ShapeDtypeStruct((T, 512), BF16), jax.ShapeDtypeStruct((nc, DN_HEADS, 128, 128), F32)],
        scratch_shapes=[pltpu.VMEM((DN_HEADS, 128, 128), F32)],
        compiler_params=_cparams(("arbitrary",)),
    )(u, w, qe, ke, pm, eg, proj, dn_g)


def _gdn_scan_bwd(dproj, w, qe, ke, pm, eg, o, proj, dyc, dn_g, name):
    T = o.shape[0]
    nc = T // CHUNK
    G = SCAN_GROUP
    rev = lambda n: nc // G - 1 - n

    def body(dp_any, w_ref, qe_ref, ke_ref, p_ref, eg_ref, o_ref, z_ref, dy_ref, ng_ref,
             dz_ref, do_ref, dvn_ref, dsn_ref, gng_ref, ds_ref):
        n = pl.program_id(0)

        @pl.when(n == 0)
        def _():
            ds_ref[...] = jnp.zeros_like(ds_ref)
            gng_ref[...] = jnp.zeros_like(gng_ref)

        def head(j, h):
            rows, sl = slice(CHUNK * j, CHUNK * j + CHUNK), slice(128 * h, 128 * h + 128)
            oh, r = _rms(o_ref[rows, sl], None)
            z, dy = z_ref[rows, sl], dy_ref[rows, sl]
            dz_ref[rows, sl] = (dy * (oh * ng_ref[...]) * _dsilu(z)).astype(BF16)
            do, gg = _rms_bwd(dy * _silu(z), oh, r, ng_ref[...])
            dob = do.astype(BF16)
            ds = ds_ref[h]
            dvn = _dot(p_ref[rows, 64 * h:64 * h + 64], dob, TN) + _dot(ke_ref[rows, sl], ds.astype(BF16))
            qd = _dot(qe_ref[rows, sl], dob, TN)
            yield
            dvb = dvn.astype(BF16)
            ds_ref[h] = qd + eg_ref[j, h:h + 1, :] * ds - _dot(w_ref[rows, sl], dvb, TN)
            do_ref[rows, sl] = dob
            dvn_ref[rows, sl] = dvb
            return jnp.sum(gg, axis=0, keepdims=True)

        for j in reversed(range(G)):
            dsn_ref[j] = ds_ref[...]
            gng = _lockstep([head(j, h) for h in range(DN_HEADS)])
            gng_ref[...] += (gng[0] + gng[1]) + (gng[2] + gng[3])

    blk = pl.BlockSpec((CHUNK * G, 512), lambda n: (rev(n), 0))
    state = pl.BlockSpec((G, DN_HEADS, 128, 128), lambda n: (rev(n), 0, 0, 0))
    return pl.pallas_call(
        body, name=name, grid=(nc // G,),
        in_specs=[pl.BlockSpec(memory_space=pl.ANY), blk, blk, blk,
                  pl.BlockSpec((CHUNK * G, 256), lambda n: (rev(n), 0)),
                  pl.BlockSpec((G, 8, 128), lambda n: (rev(n), 0, 0)), blk,
                  pl.BlockSpec((CHUNK * G, 512), lambda n: (rev(n), C_ZC // 512)), blk,
                  pl.BlockSpec((1, 128), lambda n: (0, 0))],
        out_specs=[pl.BlockSpec((CHUNK * G, 512), lambda n: (rev(n), C_ZC // 512)), blk, blk, state,
                   pl.BlockSpec((1, 128), lambda n: (0, 0))],
        out_shape=[jax.ShapeDtypeStruct(dproj.shape, BF16), jax.ShapeDtypeStruct((T, 512), BF16),
                   jax.ShapeDtypeStruct((T, 512), BF16), jax.ShapeDtypeStruct((nc, DN_HEADS, 128, 128), F32),
                   jax.ShapeDtypeStruct((1, 128), F32)],
        scratch_shapes=[pltpu.VMEM((DN_HEADS, 128, 128), F32)],
        input_output_aliases={0: 0},
        compiler_params=_cparams(("arbitrary",)),
    )(dproj, w, qe, ke, pm, eg, o, proj, dyc, dn_g)


def _gdn_chunk_grad(qd, kd, vd, gb, gbt, tmi, ssave, dsn, do, dvn, vn, name):
    T = qd.shape[0]
    G = GROUP
    ng = T // (CHUNK * G)

    def body(q_ref, k_ref, v_ref, gb_ref, gt_ref, t_ref, ss_ref, dsn_ref, do_ref, dvn_ref, vn_ref,
             dq_ref, dk_ref, dv_ref, dgb_ref):
        lower, strict, _ = _tri_consts()
        lmat = lower.astype(F32)
        ones = jnp.ones((CHUNK, 128), F32)
        lane = lax.broadcasted_iota(jnp.int32, (CHUNK, 128), 1)
        decay = [_chunk_decay(gb_ref, gt_ref, lmat, g) for g in range(G)]
        chains = [(g, h) for g in range(G) for h in range(DN_HEADS)]
        gens = []
        for g, h in chains:
            rows, gcs, grs = decay[g]
            sl = slice(128 * h, 128 * h + 128)
            c = _gdn_local(q_ref[rows, sl], k_ref[rows, sl], v_ref[rows, sl], gcs[:, h:h + 1], grs[h:h + 1, :],
                           gb_ref[rows, DN_HEADS + h:DN_HEADS + h + 1], lower, strict)
            c.update(tm=t_ref[rows, 64 * h:64 * h + 64], s=ss_ref[g, h], vn=vn_ref[rows, sl])
            gens.append(_gdn_chunk_bwd(c, do_ref[rows, sl], dvn_ref[rows, sl], dsn_ref[g, h], lower, strict, ones))
        dgb = [jnp.zeros((CHUNK, 128), F32) for _ in range(G)]
        for (g, h), (dq, dk, dv, dg, dbeta) in zip(chains, _lockstep(gens)):
            rows, sl = decay[g][0], slice(128 * h, 128 * h + 128)
            dq_ref[rows, sl], dk_ref[rows, sl], dv_ref[rows, sl] = dq, dk, dv
            dgb[g] = dgb[g] + jnp.where(lane == h, dg, 0.0) + jnp.where(lane == DN_HEADS + h, dbeta, 0.0)
        for g in range(G):
            dgb_ref[decay[g][0], :] = dgb[g]

    blk = pl.BlockSpec((CHUNK * G, 512), lambda n: (n, 0))
    half = pl.BlockSpec((CHUNK * G, 256), lambda n: (n, 0))
    nar = pl.BlockSpec((CHUNK * G, 128), lambda n: (n, 0))
    state = pl.BlockSpec((G, DN_HEADS, 128, 128), lambda n: (n, 0, 0, 0))
    return pl.pallas_call(
        body, name=name, grid=(ng,),
        in_specs=[blk, blk, blk, nar, pl.BlockSpec((G, 8, CHUNK), lambda n: (n, 0, 0)), half, state, state,
                  blk, blk, blk],
        out_specs=[blk, blk, blk, nar],
        out_shape=[jax.ShapeDtypeStruct((T, 512), F32)] * 3 + [jax.ShapeDtypeStruct((T, 128), F32)],
        compiler_params=_cparams(("parallel",)),
    )(qd, kd, vd, gb, gbt, tmi, ssave, dsn, do, dvn, vn)


def _gdn_prep_bwd1(dproj, proj, dqd, dkd, dvd, dgb, dkv_a, sconv_w, alog_v, dtb_v, name):
    T = proj.shape[0]
    tm = min(512, T)
    cur, prev, ab = _gdn_specs(T, tm)

    def body(dp_any, x_ref, xp_ref, ab_ref, dq_ref, dk_ref, dv_ref, dgb_ref, dkv_ref, w_ref, al_ref, dt_ref,
             o_ref, dpre_ref, st_ref, ext_ref):
        i = pl.program_id(0)
        pre = _gdn_conv(i, tm, x_ref, xp_ref, w_ref, ext_ref)
        y, dsl = _silu(pre), _dsilu(pre)
        for h in range(DN_HEADS):
            for base, g_ref, scale in ((0, dq_ref, 128 ** -0.5), (512, dk_ref, 1.0)):
                sl = slice(base + 128 * h, base + 128 * h + 128)
                xh = y[:, sl]
                r = lax.rsqrt(jnp.sum(xh * xh, axis=-1, keepdims=True) + EPS)
                xn = xh * r
                gy = g_ref[:, 128 * h:128 * h + 128]
                dpre_ref[:, sl] = (scale * r) * (gy - xn * jnp.sum(gy * xn, axis=-1, keepdims=True)) * dsl[:, sl]
        dpre_ref[:, 1024:] = dv_ref[...] * dsl[:, 1024:]
        abv, dgb = ab_ref[...], dgb_ref[...]
        lane = lax.broadcasted_iota(jnp.int32, (tm, 128), 1)
        na = -jnp.exp(al_ref[...])
        xs = abv + dt_ref[...]
        da = dgb * na * _sig(xs)
        b = _sig(abv)
        o_ref[:, :256] = dkv_ref[...].astype(BF16)
        o_ref[:, 256:] = jnp.where(lane < DN_HEADS, da,
                                   jnp.where(lane < 2 * DN_HEADS, dgb * b * (1.0 - b), 0.0)).astype(BF16)
        head = lane < DN_HEADS
        upd = jnp.concatenate([jnp.sum(jnp.where(head, dgb * na * _softplus(xs), 0.0), axis=0, keepdims=True),
                               jnp.sum(jnp.where(head, da, 0.0), axis=0, keepdims=True), jnp.zeros((6, 128), F32)],
                              axis=0)

        @pl.when(i == 0)
        def _():
            st_ref[...] = upd

        @pl.when(i > 0)
        def _():
            st_ref[...] += upd

    full = lambda s: pl.BlockSpec(s, lambda i: (0, 0))
    blk = pl.BlockSpec((tm, 512), lambda i: (i, 0))
    return pl.pallas_call(
        body, name=name, grid=(T // tm,),
        in_specs=[pl.BlockSpec(memory_space=pl.ANY), cur, prev, ab, blk, blk, blk,
                  pl.BlockSpec((tm, 128), lambda i: (i, 0)), pl.BlockSpec((tm, 256), lambda i: (i, 0)),
                  full((DN_K, QKV_C)), full((1, 128)), full((1, 128))],
        out_specs=[pl.BlockSpec((tm, 384), lambda i: (i, C_KA // 384)),
                   pl.BlockSpec((tm, QKV_C), lambda i: (i, 0)), full((8, 128))],
        out_shape=[jax.ShapeDtypeStruct(dproj.shape, BF16), jax.ShapeDtypeStruct((T, QKV_C), F32),
                   jax.ShapeDtypeStruct((8, 128), F32)],
        scratch_shapes=[pltpu.VMEM((tm + HALO_C, QKV_C), F32)],
        input_output_aliases={0: 0},
        compiler_params=_cparams(("arbitrary",)),
    )(dproj, proj, proj, proj, dqd, dkd, dvd, dgb, dkv_a, sconv_w, alog_v, dtb_v)


def _gdn_prep_bwd2(dproj, proj, dpre, sconv_w, name):
    T = proj.shape[0]
    tm = min(512, T)
    nt = T // tm
    r = tm // HALO_C
    cur, prev, _ = _gdn_specs(T, tm)

    def body(dp_any, x_ref, xp_ref, d_ref, dn_ref, w_ref, dx_ref, gw_ref, extx_ref, extd_ref):
        i = pl.program_id(0)
        extx_ref[:HALO_C] = jnp.where(i > 0, xp_ref[...], 0.0)
        extx_ref[HALO_C:] = x_ref[...]
        d = d_ref[...]
        extd_ref[:tm] = d
        extd_ref[tm:] = jnp.where(i < nt - 1, dn_ref[...], 0.0)
        dx = jnp.zeros((tm, QKV_C), F32)
        rows = []
        for k in range(DN_K):
            dx = dx + w_ref[k:k + 1, :] * extd_ref[pl.ds(DN_K - 1 - k, tm), :]
            rows.append(jnp.sum(d * extx_ref[pl.ds(HALO_C - DN_K + 1 + k, tm), :], axis=0, keepdims=True))
        rows.append(jnp.zeros((8 - DN_K, QKV_C), F32))
        gw = jnp.concatenate(rows, axis=0)
        dx_ref[...] = dx.astype(BF16)

        @pl.when(i == 0)
        def _():
            gw_ref[...] = gw

        @pl.when(i > 0)
        def _():
            gw_ref[...] += gw

    full = lambda s: pl.BlockSpec(s, lambda i: (0, 0))
    return pl.pallas_call(
        body, name=name, grid=(nt,),
        in_specs=[pl.BlockSpec(memory_space=pl.ANY), cur, prev, pl.BlockSpec((tm, QKV_C), lambda i: (i, 0)),
                  pl.BlockSpec((HALO_C, QKV_C), lambda i: (jnp.minimum((i + 1) * r, T // HALO_C - 1), 0)),
                  full((DN_K, QKV_C))],
        out_specs=[cur, full((8, QKV_C))],
        out_shape=[jax.ShapeDtypeStruct(dproj.shape, BF16), jax.ShapeDtypeStruct((8, QKV_C), F32)],
        scratch_shapes=[pltpu.VMEM((tm + HALO_C, QKV_C), F32), pltpu.VMEM((tm + HALO_C, QKV_C), F32)],
        input_output_aliases={0: 0},
        compiler_params=_cparams(("arbitrary",)),
    )(dproj, proj, proj, dpre, dpre, sconv_w)


def _merge_fwd(x, proj, ya, yb, yc, wa, wb, wc, wo, gate, name):
    T = x.shape[0]
    tm = min(256, T)

    def body(x_ref, mg_ref, ya_ref, yb_ref, yc_ref, wa_ref, wb_ref, wc_ref, wo_ref, gate_ref, o_ref):
        merged = (_sig(mg_ref[:, :D]) * _bdot(ya_ref[...], wa_ref[...])
                  + _sig(mg_ref[:, D:2 * D]) * _bdot(yb_ref[...], wb_ref[...])
                  + _sig(mg_ref[:, 2 * D:]) * _bdot(yc_ref[...], wc_ref[...]))
        o_ref[...] = x_ref[...] + gate_ref[...] * _bdot(merged, wo_ref[...])

    full = lambda s: pl.BlockSpec(s, lambda i: (0, 0))
    yb_ = pl.BlockSpec((tm, 512), lambda i: (i, 0))
    return pl.pallas_call(
        body, name=name, grid=(T // tm,),
        in_specs=[pl.BlockSpec((tm, D), lambda i: (i, 0)), pl.BlockSpec((tm, 3 * D), lambda i: (i, 0)), yb_, yb_, yb_,
                  full((512, D)), full((512, D)), full((512, D)), full((D, D)), full((1, D))],
        out_specs=pl.BlockSpec((tm, D), lambda i: (i, 0)),
        out_shape=jax.ShapeDtypeStruct((T, D), F32),
        compiler_params=_cparams(("parallel",)),
    )(x, proj, ya, yb, yc, wa, wb, wc, wo, _row(gate))


def _merge_bwd(dout, proj, ya, yb, yc, wa, wb, wc, wo, gate, name):
    T = dout.shape[0]
    tm = min(256, T)
    nt = T // tm

    def body(do_ref, mg_ref, ya_ref, yb_ref, yc_ref, wa_ref, wb_ref, wc_ref, wo_ref, gate_ref,
             dmg_ref, dya_ref, dyb_ref, dyc_ref, gwa_hbm, gwb_hbm, gwc_hbm, gwo_hbm, gg_ref,
             gwa_ref, gwb_ref, gwc_ref, gwo_ref):
        i = pl.program_id(0)

        @pl.when(i == 0)
        def _():
            for r in (gwa_ref, gwb_ref, gwc_ref, gwo_ref, gg_ref):
                r[...] = jnp.zeros_like(r)

        ys = (ya_ref[...], yb_ref[...], yc_ref[...])
        ws = (wa_ref, wb_ref, wc_ref)
        gs = tuple(_sig(mg_ref[:, j * D:(j + 1) * D]) for j in range(3))
        ps = tuple(_bdot(ys[j], ws[j][...]) for j in range(3))
        merged = gs[0] * ps[0] + gs[1] * ps[1] + gs[2] * ps[2]
        mo = _bdot(merged, wo_ref[...])
        do = do_ref[...]
        gg_ref[...] += jnp.sum(do * mo, axis=0, keepdims=True)
        dmo = do * gate_ref[...]
        dmerged = _bdot(dmo, wo_ref[...], NT)
        gwo_ref[...] += _bdot(merged, dmo, TN)
        for j, (dy_ref, gw_ref) in enumerate(((dya_ref, gwa_ref), (dyb_ref, gwb_ref), (dyc_ref, gwc_ref))):
            dp = dmerged * gs[j]
            dmg_ref[:, j * D:(j + 1) * D] = (dmerged * ps[j] * gs[j] * (1.0 - gs[j])).astype(BF16)
            dy_ref[...] = _bdot(dp, ws[j][...], NT)
            gw_ref[...] += _bdot(ys[j], dp, TN)

        @pl.when(i == nt - 1)
        def _():
            for src, dst in ((gwa_ref, gwa_hbm), (gwb_ref, gwb_hbm), (gwc_ref, gwc_hbm), (gwo_ref, gwo_hbm)):
                pltpu.sync_copy(src, dst)

    full = lambda s: pl.BlockSpec(s, lambda i: (0, 0))
    yb_ = pl.BlockSpec((tm, 512), lambda i: (i, 0))
    anyspec = pl.BlockSpec(memory_space=pl.ANY)
    return pl.pallas_call(
        body, name=name, grid=(nt,),
        in_specs=[pl.BlockSpec((tm, D), lambda i: (i, 0)), pl.BlockSpec((tm, 3 * D), lambda i: (i, 0)), yb_, yb_, yb_,
                  full((512, D)), full((512, D)), full((512, D)), full((D, D)), full((1, D))],
        out_specs=[pl.BlockSpec((tm, 3 * D), lambda i: (i, 0)), yb_, yb_, yb_, anyspec, anyspec, anyspec, anyspec,
                   full((1, D))],
        out_shape=[jax.ShapeDtypeStruct((T, NP), BF16)] + [jax.ShapeDtypeStruct((T, 512), F32)] * 3
        + [jax.ShapeDtypeStruct((512, D), F32)] * 3 + [jax.ShapeDtypeStruct((D, D), F32), jax.ShapeDtypeStruct((1, D), F32)],
        scratch_shapes=[pltpu.VMEM((512, D), F32)] * 3 + [pltpu.VMEM((D, D), F32)],
        compiler_params=_cparams(("arbitrary",)),
    )(dout, proj, ya, yb, yc, wa, wb, wc, wo, _row(gate))


def _loss_head(y, tgt, name):
    T = y.shape[0]
    tm = min(512, T)

    def body(y_ref, t_ref, dy_ref, l_ref):
        i = pl.program_id(0)
        diff = y_ref[...] - t_ref[...]
        dy_ref[...] = diff * (1.0 / D)
        part = jnp.sum(diff * diff, axis=0, keepdims=True)

        @pl.when(i == 0)
        def _():
            l_ref[...] = part

        @pl.when(i > 0)
        def _():
            l_ref[...] += part

    blk = pl.BlockSpec((tm, D), lambda i: (i, 0))
    return pl.pallas_call(
        body, name=name, grid=(T // tm,), in_specs=[blk, blk],
        out_specs=[blk, pl.BlockSpec((1, D), lambda i: (0, 0))],
        out_shape=[jax.ShapeDtypeStruct((T, D), F32), jax.ShapeDtypeStruct((1, D), F32)],
        compiler_params=_cparams(("arbitrary",)),
    )(y, tgt)


def _ada_fwd(c_all, w_ada, b_my, name):
    def body(c_ref, w_ref, b_ref, o_ref):
        sc = _silu(c_ref[...])
        for l in range(DEPTH):
            o_ref[l] = _bdot(sc, w_ref[l]) + b_ref[l:l + 1, :]

    return pl.pallas_call(body, name=name, out_shape=jax.ShapeDtypeStruct((DEPTH, N_DEV, w_ada.shape[2]), F32),
                          compiler_params=_cparams())(c_all, w_ada, b_my)


def _ada_bwd(c_all, dmod_my, name):
    def body(c_ref, d_ref, o_ref):
        sc = _silu(c_ref[...])
        for l in range(DEPTH):
            o_ref[l] = _bdot(sc, d_ref[l], TN)

    return pl.pallas_call(body, name=name, out_shape=jax.ShapeDtypeStruct((DEPTH, D, dmod_my.shape[2]), F32),
                          compiler_params=_cparams())(c_all, dmod_my)


def _adam_math(w, g, m, v):
    m = ADAM_B1 * m + (1.0 - ADAM_B1) * g
    v = ADAM_B2 * v + (1.0 - ADAM_B2) * (g * g)
    m_hat = m / (1.0 - ADAM_B1 ** ADAM_STEP)
    v_hat = v / (1.0 - ADAM_B2 ** ADAM_STEP)
    return -ADAM_LR * (m_hat / (jnp.sqrt(v_hat) + ADAM_EPS) + ADAM_WD * w), m, v


def _row_tile(rows, cap):
    best = rows
    for t in range(8, min(rows, cap) + 1, 8):
        if rows % t == 0:
            best = t
    return best if best <= cap else rows


def _adamw(w, g, m, v, name):
    R, C = w.shape
    tr = _row_tile(R, 256)

    def body(w_ref, g_ref, m_ref, v_ref, d_ref, mo_ref, vo_ref):
        d_ref[...], mo_ref[...], vo_ref[...] = _adam_math(w_ref[...], g_ref[...], m_ref[...], v_ref[...])

    blk = pl.BlockSpec((tr, C), lambda i: (i, 0))
    return pl.pallas_call(body, name=name, grid=(R // tr,), in_specs=[blk] * 4, out_specs=[blk] * 3,
                          out_shape=[jax.ShapeDtypeStruct((R, C), F32)] * 3,
                          compiler_params=_cparams(("parallel",)))(w, g, m, v)


def _sum_adamw_many(parts, ws, ms, vs, name):
    n = len(ws)

    def body(*refs):
        ins, outs = refs[:4 * n], refs[4 * n:]
        for i in range(n):
            g = ins[i][0]
            for j in range(1, N_DEV):
                g = g + ins[i][j]
            d, m, v = _adam_math(ins[n + i][...], g, ins[2 * n + i][...], ins[3 * n + i][...])
            outs[i][...], outs[n + i][...], outs[2 * n + i][...], outs[3 * n + i][...] = g, d, m, v

    shapes = [jax.ShapeDtypeStruct(w.shape, F32) for w in ws]
    out = pl.pallas_call(body, name=name, out_shape=shapes * 4, compiler_params=_cparams())(*parts, *ws, *ms, *vs)
    return out[:n], out[n:2 * n], out[2 * n:3 * n], out[3 * n:]


def _sum_adamw(parts0, parts1, w, m, v, name):
    P, R, C = parts0.shape
    tr = _row_tile(R, 128)
    nt = R // tr

    def body(p0_ref, p1_ref, w_ref, m_ref, v_ref, g_ref, d_ref, mo_ref, vo_ref):
        def emit(p_ref):
            g = p_ref[0].astype(F32)
            for j in range(1, P):
                g = g + p_ref[j].astype(F32)
            g_ref[...] = g
            d_ref[...], mo_ref[...], vo_ref[...] = _adam_math(w_ref[...], g, m_ref[...], v_ref[...])

        @pl.when(pl.program_id(0) == 0)
        def _():
            emit(p0_ref)

        @pl.when(pl.program_id(0) == 1)
        def _():
            emit(p1_ref)

    blk = pl.BlockSpec((tr, C), lambda l, i: (l * nt + i, 0))
    return pl.pallas_call(
        body, name=name, grid=(DEPTH, nt),
        in_specs=[pl.BlockSpec((P, tr, C), lambda l, i: (0, i * (1 - l) + (nt - 1) * l, 0)),
                  pl.BlockSpec((P, tr, C), lambda l, i: (0, i * l, 0)), blk, blk, blk],
        out_specs=[blk] * 4, out_shape=[jax.ShapeDtypeStruct((DEPTH * R, C), F32)] * 4,
        compiler_params=_cparams(("arbitrary", "arbitrary")))(parts0, parts1, w, m, v)


def _pair_sum(core, buf, recv, name):
    _, _, R, C = buf.shape
    tr = _row_tile(R, 128)

    def body(c_ref, a_ref, b_ref, o_ref):
        o_ref[...] = (a_ref[:, 0].astype(F32) + b_ref[...].astype(F32)).astype(BF16)

    return pl.pallas_call(
        body, name=name,
        grid_spec=pltpu.PrefetchScalarGridSpec(
            num_scalar_prefetch=1, grid=(R // tr,),
            in_specs=[pl.BlockSpec((4, 1, tr, C), lambda i, c: (0, c[0], i, 0)),
                      pl.BlockSpec((4, tr, C), lambda i, c: (0, i, 0))],
            out_specs=pl.BlockSpec((4, tr, C), lambda i, c: (0, i, 0))),
        out_shape=jax.ShapeDtypeStruct((4, R, C), BF16),
        compiler_params=_cparams(("parallel",)))(core, buf, recv)


SHARD_IN = D_IN // N_DEV


def _w_in_pieces():
    out, p = [], 0
    for a, b in _PAD_FROM:
        for j in range(N_DEV):
            lo, hi = max(a, SHARD_IN * j), min(b, SHARD_IN * (j + 1))
            if lo < hi:
                out.append((j, lo - SHARD_IN * j, hi - SHARD_IN * j, p + lo - a))
        p += b - a
    return out


def _assemble_w_in(gw, name):
    tr = 256
    nt = D // tr

    def body(x_ref, o_ref):
        for j, s0, s1, d0 in _w_in_pieces():
            o_ref[:, d0:d0 + s1 - s0] = x_ref[j, :, s0:s1]
        o_ref[:, D_IN:] = jnp.zeros((tr, NP - D_IN), gw.dtype)

    return pl.pallas_call(
        body, name=name, grid=(nt,),
        in_specs=[pl.BlockSpec((N_DEV, tr, SHARD_IN), lambda i: (0, i, 0))],
        out_specs=pl.BlockSpec((tr, NP), lambda i: (i, 0)),
        out_shape=jax.ShapeDtypeStruct((D, NP), gw.dtype),
        compiler_params=_cparams(("parallel",)))(gw)


def _split_w_in_grad(g, name):
    tr = 256

    def body(g_ref, o_ref):
        for j, s0, s1, d0 in _w_in_pieces():
            o_ref[j, :, s0:s1] = g_ref[:, d0:d0 + s1 - s0].astype(BF16)

    return pl.pallas_call(
        body, name=name, grid=(D // tr,),
        in_specs=[pl.BlockSpec((tr, NP), lambda i: (i, 0))],
        out_specs=pl.BlockSpec((N_DEV, tr, SHARD_IN), lambda i: (0, i, 0)),
        out_shape=jax.ShapeDtypeStruct((N_DEV, D, SHARD_IN), BF16),
        compiler_params=_cparams(("parallel",)))(g)


def _mesh_pos():
    return lax.axis_index("x"), lax.axis_index("y"), lax.axis_index("c")


def _launch(copies, peers, bufs, out_structs, sems, name, collective_id):
    n = len(bufs)
    if collective_id is None:
        anyspec = pl.BlockSpec(memory_space=pl.ANY)
        return list(pl.pallas_call(
            lambda *refs: copies(refs[:n], refs[n:n + len(out_structs)], *refs[n + len(out_structs):]),
            name=name, in_specs=[anyspec] * n, out_specs=[anyspec] * len(out_structs), out_shape=list(out_structs),
            scratch_shapes=list(sems))(*bufs))
    ins = [jax.new_ref(b, memory_space=pltpu.MemorySpace.HBM) for b in bufs]
    outs = [jax.empty_ref(s, memory_space=pltpu.MemorySpace.HBM) for s in out_structs]

    @pl.kernel(mesh=plsc.ScalarSubcoreMesh(axis_name="sequencer", num_cores=1), name=name, scratch_types=tuple(sems),
               compiler_params=pltpu.CompilerParams(collective_id=collective_id))
    def on_sequencer(*sem_refs):
        barrier = pltpu.get_barrier_semaphore()
        targets = peers()
        for p in targets:
            pl.semaphore_signal(barrier, inc=1, device_id=p, device_id_type=pl.DeviceIdType.MESH)
        pl.semaphore_wait(barrier, len(targets))
        copies(ins, outs, *sem_refs)

    on_sequencer()
    return [r[...] for r in outs]


def _all_gather(blocks, name, collective_id=None):
    n = len(blocks)

    def peers():
        x, y, c = _mesh_pos()
        return [(x, y, 1 - c), (1 - x, y, c), (x, 1 - y, c), (1 - x, 1 - y, c)]

    def copies(ins, outs, send_sems, recv_sems, local_sems):
        x, y, c = _mesh_pos()
        me, sibling = (x, y, c), (x, y, 1 - c)
        chips = [(1 - x, y), (x, 1 - y), (1 - x, 1 - y)]
        idx = lambda p: 4 * p[0] + 2 * p[1] + p[2]

        def copy(a, k, block, to, src=None):
            dst = outs[a].at[idx(block)]
            return pltpu.make_async_remote_copy(
                src_ref=dst if src is None else src, dst_ref=dst, send_sem=send_sems.at[a, k],
                recv_sem=recv_sems.at[a, k], device_id=to, device_id_type=pl.DeviceIdType.MESH)

        mine = [pltpu.make_async_copy(ins[a], outs[a].at[idx(me)], local_sems.at[a]) for a in range(n)]
        for cp in mine:
            cp.start()
        first = []
        for a in range(n):
            first.append(copy(a, 0, me, sibling, src=ins[a]))
            first += [copy(a, 1 + j, me, (*chip, c), src=ins[a]) for j, chip in enumerate(chips)]
        for cp in first:
            cp.start()
        passed = []
        for j, chip in enumerate(chips):
            for a in range(n):
                copy(a, 1 + j, (*chip, c), me).wait_recv()
                cp = copy(a, 4 + j, (*chip, c), sibling)
                cp.start()
                passed.append(cp)
        for a in range(n):
            copy(a, 0, sibling, me).wait_recv()
            for j, chip in enumerate(chips):
                copy(a, 4 + j, (*chip, 1 - c), me).wait_recv()
        for cp in first + passed:
            cp.wait_send()
        for cp in mine:
            cp.wait()

    return _launch(copies, peers, blocks, [jax.ShapeDtypeStruct((N_DEV,) + b.shape, b.dtype) for b in blocks],
                   [pltpu.SemaphoreType.DMA((n, 7)), pltpu.SemaphoreType.DMA((n, 7)), pltpu.SemaphoreType.DMA((n,))],
                   name, collective_id)


def _exchange_core(bufs, name, collective_id=None):
    n = len(bufs)

    def peers():
        x, y, c = _mesh_pos()
        return [(x, y, 1 - c)]

    def copies(ins, outs, send_sems, recv_sems):
        x, y, c = _mesh_pos()
        started = []
        for a in range(n):
            for q in range(4):
                cp = pltpu.make_async_remote_copy(
                    src_ref=ins[a].at[q, 1 - c], dst_ref=outs[a].at[q], send_sem=send_sems.at[a, q],
                    recv_sem=recv_sems.at[a, q], device_id=(x, y, 1 - c), device_id_type=pl.DeviceIdType.MESH)
                cp.start()
                started.append(cp)
        for cp in started:
            cp.wait()

    return _launch(copies, peers, bufs, [jax.ShapeDtypeStruct((4,) + b.shape[2:], b.dtype) for b in bufs],
                   [pltpu.SemaphoreType.DMA((n, 4)), pltpu.SemaphoreType.DMA((n, 4))], name, collective_id)


def _exchange_chips(bufs, name, collective_id=None):
    n = len(bufs)

    def peers():
        x, y, c = _mesh_pos()
        return [(1 - x, y, c), (x, 1 - y, c), (1 - x, 1 - y, c)]

    def copies(ins, outs, send_sems, recv_sems, local_sems):
        x, y, c = _mesh_pos()
        chip = 2 * x + y
        local = [pltpu.make_async_copy(ins[a].at[chip], outs[a].at[chip], local_sems.at[a]) for a in range(n)]
        for cp in local:
            cp.start()
        started = []
        for k in range(1, 4):
            px = 1 - x if k & 2 else x
            py = 1 - y if k & 1 else y
            for a in range(n):
                cp = pltpu.make_async_remote_copy(
                    src_ref=ins[a].at[2 * px + py], dst_ref=outs[a].at[chip], send_sem=send_sems.at[a, k - 1],
                    recv_sem=recv_sems.at[a, k - 1], device_id=(px, py, c), device_id_type=pl.DeviceIdType.MESH)
                cp.start()
                started.append(cp)
        for cp in started:
            cp.wait()
        for cp in local:
            cp.wait()

    return _launch(copies, peers, bufs, [jax.ShapeDtypeStruct(b.shape, b.dtype) for b in bufs],
                   [pltpu.SemaphoreType.DMA((n, 3)), pltpu.SemaphoreType.DMA((n, 3)), pltpu.SemaphoreType.DMA((n,))],
                   name, collective_id)


_SMALL = ("b_ada", "norm_g", "q_norm_g", "k_norm_g", "sinks", "dw_b", "ln_g", "ln_b", "pw2_b", "a_log", "dt_bias",
          "dn_norm_g", "dw_w", "sconv_w")


def _lane4(v):
    return jnp.pad(v, (0, 124)).reshape(1, 128)


def kernel(x, c, w_ada, b_ada, norm_g, w_in, q_norm_g, k_norm_g, sinks, dw_w, dw_b, ln_g, ln_b, pw2_w, pw2_b, sconv_w, a_log, dt_bias, dn_norm_g, w_proj_a, w_proj_b, w_proj_c, w_out, loss_target, m_w_ada, m_b_ada, m_norm_g, m_w_in, m_q_norm_g, m_k_norm_g, m_sinks, m_dw_w, m_dw_b, m_ln_g, m_ln_b, m_pw2_w, m_pw2_b, m_sconv_w, m_a_log, m_dt_bias, m_dn_norm_g, m_w_proj_a, m_w_proj_b, m_w_proj_c, m_w_out, v_w_ada, v_b_ada, v_norm_g, v_w_in, v_q_norm_g, v_k_norm_g, v_sinks, v_dw_w, v_dw_b, v_ln_g, v_ln_b, v_pw2_w, v_pw2_b, v_sconv_w, v_a_log, v_dt_bias, v_dn_norm_g, v_w_proj_a, v_w_proj_b, v_w_proj_c, v_w_out):
    T = x.shape[1]
    nc = T // CHUNK
    xi, yi, ci = _mesh_pos()
    me = 4 * xi + 2 * yi + ci
    big_w = (w_in, pw2_w, w_proj_a, w_proj_b, w_proj_c, w_out)
    big_m = (m_w_in, m_pw2_w, m_w_proj_a, m_w_proj_b, m_w_proj_c, m_w_out)
    big_v = (v_w_in, v_pw2_w, v_w_proj_a, v_w_proj_b, v_w_proj_c, v_w_out)

    ada_cols = w_ada.shape[2]
    dw_cols, sc_cols = dw_w.shape[2], sconv_w.shape[2]
    flat2 = lambda a: a.reshape(-1, a.shape[-1])
    big16 = [a.astype(BF16) for a in big_w]
    gw_in0, c_all, gdw, gsc = _all_gather([big16[0][0], c, dw_w, sconv_w], "gather_first", collective_id=0)
    c_all = c_all.reshape(N_DEV, D)
    dw_f = gdw.transpose(1, 2, 0, 3).reshape(DEPTH, CONV_K, 512)
    sc_f = gsc.transpose(1, 2, 0, 3).reshape(DEPTH, DN_K, QKV_C)

    b_my = lax.dynamic_slice(b_ada, (0, me * ada_cols), (DEPTH, ada_cols))
    mod_part = _ada_fwd(c_all, w_ada, b_my, "ada_fwd")
    (gmod,) = _all_gather([mod_part.reshape(-1, 128)], "gather_mod")

    rest0 = [a[0] for a in big16[1:]]
    all1 = [a[1] for a in big16]
    (rest0, all1), gmod = lax.optimization_barrier(((rest0, all1), gmod))
    got0 = [gw_in0] + _all_gather(rest0, "gather_rest0", collective_id=1)
    got1 = _all_gather(all1, "gather_weights1", collective_id=6)
    wp, pw2_f, wa_f, wb_f, wc_f, wo_f = [], [], [], [], [], []
    for l, (gw_in, gpw2, gpa, gpb, gpc, gwo) in enumerate((got0, got1)):
        wp.append(_assemble_w_in(gw_in, f"assemble_w_in{l}"))
        pw2_f.append(gpw2.reshape(512, 512))
        for dst, g in ((wa_f, gpa), (wb_f, gpb), (wc_f, gpc)):
            dst.append(g.transpose(1, 0, 2).reshape(512, D))
        wo_f.append(gwo.reshape(D, D))
    mod_all = gmod.reshape(N_DEV, DEPTH, N_DEV, ada_cols).transpose(1, 2, 0, 3).reshape(DEPTH, N_DEV, 3 * D)
    mod = lax.dynamic_index_in_dim(mod_all, me, axis=1, keepdims=False)
    shift, scale, gate = mod[:, :D], mod[:, D:2 * D], mod[:, 2 * D:]

    xs, saved = [x[0]], []
    for l in range(DEPTH):
        xl = xs[-1]
        h = _norm_fwd(xl, norm_g[l], scale[l], shift[l], f"norm_fwd{l}")
        proj = _mm(h, wp[l], tm=min(1024, T), tn=1152, tk=D, name=f"in_proj{l}")
        ya = _attn_fwd(proj, q_norm_g[l], k_norm_g[l], sinks[l], f"attn_fwd{l}")
        yb = _conf_fwd(proj, dw_f[l], dw_b[l], ln_g[l], ln_b[l], pw2_f[l], pw2_b[l], f"conf_fwd{l}")
        alv, dtv, dng = _lane4(a_log[l]), _lane4(dt_bias[l]), _row(dn_norm_g[l])
        qd, kd, vd, gb = _gdn_prep_fwd(proj, sc_f[l], alv, dtv, f"gdn_prep_fwd{l}")
        gbt = gb[:, :8].reshape(nc, CHUNK, 8).transpose(0, 2, 1)
        u, w, qe, ke, pm, tmi, eg = _gdn_chunk_fwd(qd, kd, vd, gb, gbt, f"gdn_chunk_fwd{l}")
        yc, o, vn, ss = _gdn_scan_fwd(u, w, qe, ke, pm, eg, proj, dng, f"gdn_scan_fwd{l}")
        xs.append(_merge_fwd(xl, proj, ya, yb, yc, wa_f[l], wb_f[l], wc_f[l], wo_f[l], gate[l], f"merge_fwd{l}"))
        saved.append((h, proj, ya, yb, yc, qd, kd, vd, gb, gbt, ss, alv, dtv, dng, w, qe, ke, pm, tmi, eg, o, vn))

    dout, lsum = _loss_head(xs[-1], loss_target[0], "loss_head")

    small = {name: [None] * DEPTH for name in _SMALL}
    big_parts = [None] * DEPTH
    core = jnp.reshape(ci, (1,)).astype(jnp.int32)
    for l in reversed(range(DEPTH)):
        h, proj, ya, yb, yc, qd, kd, vd, gb, gbt, ss, alv, dtv, dng, w, qe, ke, pm, tmi, eg, o, vn = saved[l]
        dproj, dya, dyb, dyc, g_wa, g_wb, g_wc, g_wo, g_gate = _merge_bwd(
            dout, proj, ya, yb, yc, wa_f[l], wb_f[l], wc_f[l], wo_f[l], gate[l], f"merge_bwd{l}")
        dproj, dkv_a, g_q, g_k, g_s = _attn_bwd(dproj, proj, dya, q_norm_g[l], k_norm_g[l], sinks[l], f"attn_bwd{l}")
        dproj, du1, g_pw2, st_b = _conf_bwd1(dproj, proj, dyb, dw_f[l], dw_b[l], ln_g[l], ln_b[l], pw2_f[l], pw2_b[l],
                                             f"conf_bwd_a{l}")
        dproj, g_dw = _conf_bwd2(dproj, proj, du1, dw_f[l], f"conf_bwd_b{l}")
        dproj, do, dvn, dsn, g_dn = _gdn_scan_bwd(dproj, w, qe, ke, pm, eg, o, proj, dyc, dng, f"gdn_scan_bwd{l}")
        dqd, dkd, dvd, dgb = _gdn_chunk_grad(qd, kd, vd, gb, gbt, tmi, ss, dsn, do, dvn, vn, f"gdn_chunk_bwd{l}")
        dproj, dpre, st_c = _gdn_prep_bwd1(dproj, proj, dqd, dkd, dvd, dgb, dkv_a, sc_f[l], alv, dtv,
                                           f"gdn_prep_bwd_a{l}")
        dproj, g_sc = _gdn_prep_bwd2(dproj, proj, dpre, sc_f[l], f"gdn_prep_bwd_b{l}")
        g_wp = _mm(h, dproj, ta=True, tm=D, tn=1152, tk=min(1024, T), name=f"d_w_in{l}")
        by_dest = [_split_w_in_grad(g_wp, f"split_w_in_grad{l}"), g_pw2.reshape(N_DEV, -1, 512).astype(BF16)]
        by_dest += [g.reshape(512, N_DEV, -1).transpose(1, 0, 2).astype(BF16) for g in (g_wa, g_wb, g_wc)]
        by_dest.append(g_wo.reshape(N_DEV, -1, D).astype(BF16))
        by_dest = [b.reshape(4, 2, -1, b.shape[-1]) for b in by_dest]
        from_sibling = _exchange_core(by_dest, f"exchange_grads_core{l}", collective_id=2 + 2 * l)
        chip_sums = [_pair_sum(core, b, r, f"pair_sum{l}_{i}") for i, (b, r) in enumerate(zip(by_dest, from_sibling))]
        big_parts[l] = _exchange_chips(chip_sums, f"exchange_grads_chips{l}", collective_id=3 + 2 * l)
        if l == 0:
            dproj, chip_sums = lax.optimization_barrier((dproj, chip_sums))
        dh = _mm(dproj, wp[l], tb=True, tm=min(1024, T), tn=D, tk=1152, name=f"d_h{l}")
        dout, st_n = _norm_bwd(dh, xs[l], dout, norm_g[l], scale[l], f"norm_bwd{l}")
        if l > 0:
            dout, chip_sums = lax.optimization_barrier((dout, chip_sums))
        for name, g in (("b_ada", jnp.concatenate([st_n[0], st_n[1], g_gate[0]])), ("norm_g", st_n[2]),
                        ("q_norm_g", g_q.reshape(ATT_HEADS, ATT_HD).sum(0)), ("k_norm_g", g_k.reshape(2, ATT_HD).sum(0)),
                        ("sinks", g_s[0]), ("dw_b", st_b[3]),
                        ("ln_g", st_b[1]), ("ln_b", st_b[2]), ("pw2_b", st_b[0]), ("a_log", st_c[0, :4]),
                        ("dt_bias", st_c[1, :4]), ("dn_norm_g", g_dn[0]), ("dw_w", g_dw[:CONV_K]),
                        ("sconv_w", g_sc[:DN_K])):
            small[name][l] = g
    grad_x = dout[None]

    names = list(_SMALL)
    gathered = _all_gather([jnp.stack(small[n]) for n in names] + [lsum], "gather_small_grads")
    gparts = dict(zip(names, gathered))
    loss = 0.5 * jnp.sum(jnp.sum(gathered[-1], axis=(1, 2))) / D
    dmod_my = lax.dynamic_slice(gparts["b_ada"], (0, 0, me * ada_cols), (N_DEV, DEPTH, ada_cols)).transpose(1, 0, 2)
    g_w_ada = _ada_bwd(c_all, dmod_my, "ada_bwd")
    gparts["dw_w"] = lax.dynamic_slice(gparts["dw_w"], (0, 0, 0, me * dw_cols), (N_DEV, DEPTH, CONV_K, dw_cols))
    gparts["sconv_w"] = lax.dynamic_slice(gparts["sconv_w"], (0, 0, 0, me * sc_cols), (N_DEV, DEPTH, DN_K, sc_cols))
    env = dict(b_ada=(b_ada, m_b_ada, v_b_ada), norm_g=(norm_g, m_norm_g, v_norm_g),
               q_norm_g=(q_norm_g, m_q_norm_g, v_q_norm_g), k_norm_g=(k_norm_g, m_k_norm_g, v_k_norm_g),
               sinks=(sinks, m_sinks, v_sinks), dw_b=(dw_b, m_dw_b, v_dw_b), ln_g=(ln_g, m_ln_g, v_ln_g),
               ln_b=(ln_b, m_ln_b, v_ln_b), pw2_b=(pw2_b, m_pw2_b, v_pw2_b), a_log=(a_log, m_a_log, v_a_log),
               dt_bias=(dt_bias, m_dt_bias, v_dt_bias), dn_norm_g=(dn_norm_g, m_dn_norm_g, v_dn_norm_g),
               dw_w=(dw_w, m_dw_w, v_dw_w), sconv_w=(sconv_w, m_sconv_w, v_sconv_w))
    upd = _sum_adamw_many([gparts[n] for n in names], [env[n][0] for n in names], [env[n][1] for n in names],
                          [env[n][2] for n in names], "sum_adamw_small")
    g_small, d_small, m_small, v_small = (dict(zip(names, u)) for u in upd)

    d_ada, nm_ada, nv_ada = (u.reshape(w_ada.shape) for u in
                             _adamw(flat2(w_ada), flat2(g_w_ada), flat2(m_w_ada), flat2(v_w_ada), "adamw_w_ada"))

    res = [_sum_adamw(p0, p1, flat2(w), flat2(m), flat2(v), f"sum_adamw{i}")
           for i, (p0, p1, w, m, v) in enumerate(zip(big_parts[0], big_parts[1], big_w, big_m, big_v))]
    g_big, d_big, m_big, v_big = ([r[k].reshape(w.shape) for r, w in zip(res, big_w)] for k in range(4))

    order = ("w_ada", "b_ada", "norm_g", "w_in", "q_norm_g", "k_norm_g", "sinks", "dw_w", "dw_b", "ln_g", "ln_b",
             "pw2_w", "pw2_b", "sconv_w", "a_log", "dt_bias", "dn_norm_g", "w_proj_a", "w_proj_b", "w_proj_c", "w_out")
    big_names = ("w_in", "pw2_w", "w_proj_a", "w_proj_b", "w_proj_c", "w_out")

    def pick(kind):
        src_small = (g_small, d_small, m_small, v_small)[kind]
        src_big = (g_big, d_big, m_big, v_big)[kind]
        src_ada = (g_w_ada, d_ada, nm_ada, nv_ada)[kind]
        return [src_ada if n == "w_ada" else src_big[big_names.index(n)] if n in big_names else src_small[n]
                for n in order]

    return (loss, grad_x, *pick(0), *pick(1), *pick(2), *pick(3))
```

```python
import functools
import math

import jax
import jax.numpy as jnp
import numpy as np
from jax import lax
from jax.experimental import pallas as pl
from jax.experimental.pallas import tpu as pltpu
from jax.experimental.pallas import tpu_sc as plsc

F32 = jnp.float32
BF16 = jnp.bfloat16
HI = lax.Precision.HIGHEST

N_DEV = 8
D = 1024
DEPTH = 2
EPS = 1e-6
NEG_INF = -1e30
WINDOW = 128
ATT_HEADS = 8
ATT_HD = 64
CONV_K = 31
DN_HEADS = 4
DN_K = 4
CHUNK = 64
D_IN = 7944
VMEM_LIMIT = 56 * 1024 * 1024

C_MG, C_QA, C_ZA, C_ZB, C_QC, C_KC, C_VC, C_GV, C_GG, C_ZC, C_KA, C_VA, C_AB, NP = (
    0, 3072, 3584, 4096, 4608, 5120, 5632, 6144, 6656, 7168, 7680, 7808, 7936, 8064)
_PAD_FROM = ((4872, 7944), (0, 512), (768, 1280), (2304, 2816), (2816, 4352), (1280, 2304), (4360, 4872),
             (512, 768), (4352, 4360))

ALIBI = tuple(float(2.0 ** (-8.0 * (h + 1) / ATT_HEADS)) for h in range(ATT_HEADS))

ADAM_LR, ADAM_B1, ADAM_B2, ADAM_EPS, ADAM_WD, ADAM_STEP = 0.001, 0.9, 0.999, 1e-08, 0.01, 10


def _cparams(sem=None):
    return pltpu.CompilerParams(dimension_semantics=sem, vmem_limit_bytes=VMEM_LIMIT)


def _sig(x):
    return jax.nn.sigmoid(x)


def _silu(x):
    return x * _sig(x)


def _dsilu(x):
    s = _sig(x)
    return s * (1.0 + x * (1.0 - s))


def _dot(a, b, dims=((1,), (0,)), precision=None):
    return lax.dot_general(a, b, (dims, ((), ())), preferred_element_type=F32, precision=precision)


def _bdot(a, b, dims=((1,), (0,))):
    return _dot(a.astype(BF16), b.astype(BF16), dims)


NN, NT, TN = ((1,), (0,)), ((1,), (1,)), ((0,), (0,))


def _row(v):
    return v.reshape(1, -1)


def _mm(a, b, *, ta=False, tb=False, tm, tn, tk, name):
    M, K = (a.shape[1], a.shape[0]) if ta else a.shape
    N = b.shape[0] if tb else b.shape[1]
    assert M % tm == 0 and N % tn == 0 and K % tk == 0, (M, N, K, tm, tn, tk)
    nk = K // tk
    dims = ((0 if ta else 1,), (1 if tb else 0,))

    def body(a_ref, b_ref, o_ref):
        k = pl.program_id(2)
        part = _bdot(a_ref[...], b_ref[...], dims)

        @pl.when(k == 0)
        def _():
            o_ref[...] = part

        @pl.when(k > 0)
        def _():
            o_ref[...] += part

    a_spec = pl.BlockSpec((tk, tm), lambda i, j, k: (k, i)) if ta else pl.BlockSpec((tm, tk), lambda i, j, k: (i, k))
    b_spec = pl.BlockSpec((tn, tk), lambda i, j, k: (j, k)) if tb else pl.BlockSpec((tk, tn), lambda i, j, k: (k, j))
    return pl.pallas_call(
        body, name=name, grid=(M // tm, N // tn, nk),
        in_specs=[a_spec, b_spec], out_specs=pl.BlockSpec((tm, tn), lambda i, j, k: (i, j)),
        out_shape=jax.ShapeDtypeStruct((M, N), F32),
        compiler_params=_cparams(("parallel", "parallel", "arbitrary")),
    )(a, b)


def _norm_fwd(x, norm_g, scale, shift, name):
    T = x.shape[0]
    tm = min(512, T)

    def body(x_ref, g_ref, sc_ref, sh_ref, h_ref):
        xv = x_ref[...]
        r = lax.rsqrt(jnp.mean(xv * xv, axis=-1, keepdims=True) + EPS)
        h_ref[...] = ((xv * r) * g_ref[...] * (1.0 + sc_ref[...]) + sh_ref[...]).astype(BF16)

    vec = pl.BlockSpec((1, D), lambda i: (0, 0))
    return pl.pallas_call(
        body, name=name, grid=(T // tm,),
        in_specs=[pl.BlockSpec((tm, D), lambda i: (i, 0)), vec, vec, vec],
        out_specs=pl.BlockSpec((tm, D), lambda i: (i, 0)),
        out_shape=jax.ShapeDtypeStruct((T, D), BF16),
        compiler_params=_cparams(("parallel",)),
    )(x, _row(norm_g), _row(scale), _row(shift))


def _norm_bwd(dh, x, dres, norm_g, scale, name):
    T = x.shape[0]
    tm = min(512, T)

    def body(dh_ref, x_ref, dr_ref, g_ref, sc_ref, dx_ref, st_ref):
        i = pl.program_id(0)
        xv, dhv = x_ref[...], dh_ref[...]
        r = lax.rsqrt(jnp.mean(xv * xv, axis=-1, keepdims=True) + EPS)
        xh = xv * r
        g, s1 = g_ref[...], 1.0 + sc_ref[...]
        dxh = dhv * (g * s1)
        dx_ref[...] = dr_ref[...] + r * (dxh - xh * jnp.mean(dxh * xh, axis=-1, keepdims=True))
        dhx = dhv * xh
        upd = jnp.concatenate([jnp.sum(dhv, axis=0, keepdims=True), jnp.sum(dhx * g, axis=0, keepdims=True),
                               jnp.sum(dhx * s1, axis=0, keepdims=True), jnp.zeros((5, D), F32)], axis=0)

        @pl.when(i == 0)
        def _():
            st_ref[...] = upd

        @pl.when(i > 0)
        def _():
            st_ref[...] += upd

    vec = pl.BlockSpec((1, D), lambda i: (0, 0))
    blk = pl.BlockSpec((tm, D), lambda i: (i, 0))
    return pl.pallas_call(
        body, name=name, grid=(T // tm,),
        in_specs=[blk, blk, blk, vec, vec],
        out_specs=[blk, pl.BlockSpec((8, D), lambda i: (0, 0))],
        out_shape=[jax.ShapeDtypeStruct((T, D), F32), jax.ShapeDtypeStruct((8, D), F32)],
        compiler_params=_cparams(("arbitrary",)),
    )(dh, x, dres, _row(norm_g), _row(scale))


def _rms(x, g):
    r = lax.rsqrt(jnp.mean(x * x, axis=-1, keepdims=True) + EPS)
    return x * r, r


def _head_mean_matrix():
    head = np.arange(ATT_HEADS * ATT_HD) // ATT_HD
    return jnp.asarray((head[:, None] == head[None, :]) * (1.0 / ATT_HD), BF16)


def _head_rms(x, hm):
    r = lax.rsqrt(_dot_exact(x * x, hm) + EPS)
    return x * r, r


def _head_rms_bwd(dy, xh, r, g, hm):
    dxh = dy * g
    return r * (dxh - xh * _dot_exact(dxh * xh, hm)), dy * xh


def _attn_mask(n):
    qi = lax.broadcasted_iota(jnp.int32, (WINDOW, 2 * WINDOW), 0)
    kj = lax.broadcasted_iota(jnp.int32, (WINDOW, 2 * WINDOW), 1)
    dist = qi + WINDOW - kj
    valid = (dist >= 0) & (dist < WINDOW) & ((n > 0) | (kj >= WINDOW))
    return valid, dist.astype(F32)


def _attn_probs(s, h, sink, valid, distf):
    s = s - ALIBI[h] * distf
    s = jnp.where(valid, s, NEG_INF)
    m = jnp.maximum(jnp.max(s, axis=-1, keepdims=True), sink)
    p = jnp.exp(s - m)
    es = jnp.exp(sink - m)
    den = jnp.sum(p, axis=-1, keepdims=True) + es
    return p / den, es / den


def _attn_fwd(proj, q_norm_g, k_norm_g, sinks, name):
    T = proj.shape[0]
    nb = T // WINDOW

    def body(sink_ref, q_ref, z_ref, kc_ref, kp_ref, vc_ref, vp_ref, qg_ref, kg_ref, hm_ref, o_ref):
        n = pl.program_id(0)
        valid, distf = _attn_mask(n)
        k2 = jnp.concatenate([kp_ref[...], kc_ref[...]], axis=0)
        v2 = jnp.concatenate([vp_ref[...], vc_ref[...]], axis=0).astype(BF16)
        kn = (_head_rms(k2, hm_ref[:128, :128])[0] * kg_ref[...]).astype(BF16)
        qn = ((_head_rms(q_ref[...], hm_ref[...])[0] * qg_ref[...]) * (ATT_HD ** -0.5)).astype(BF16)

        def head(h):
            sl, gsl = slice(64 * h, 64 * h + 64), slice(64 * (h // 4), 64 * (h // 4) + 64)
            s = _dot(qn[:, sl], kn[:, gsl], NT)
            yield
            p, _ = _attn_probs(s, h, sink_ref[h], valid, distf)
            o_ref[:, sl] = _dot(p.astype(BF16), v2[:, gsl])
            yield

        _lockstep([head(h) for h in range(ATT_HEADS)])
        o_ref[...] = o_ref[...] * _silu(z_ref[...])

    prev = lambda n: jnp.maximum(n - 1, 0)
    return pl.pallas_call(
        body, name=name, grid=(nb,),
        in_specs=[pl.BlockSpec(memory_space=pltpu.SMEM),
                  pl.BlockSpec((WINDOW, 512), lambda n: (n, C_QA // 512)),
                  pl.BlockSpec((WINDOW, 512), lambda n: (n, C_ZA // 512)),
                  pl.BlockSpec((WINDOW, 128), lambda n: (n, C_KA // 128)),
                  pl.BlockSpec((WINDOW, 128), lambda n: (prev(n), C_KA // 128)),
                  pl.BlockSpec((WINDOW, 128), lambda n: (n, C_VA // 128)),
                  pl.BlockSpec((WINDOW, 128), lambda n: (prev(n), C_VA // 128)),
                  pl.BlockSpec((1, 512), lambda n: (0, 0)), pl.BlockSpec((1, 128), lambda n: (0, 0)),
                  pl.BlockSpec((512, 512), lambda n: (0, 0))],
        out_specs=pl.BlockSpec((WINDOW, 512), lambda n: (n, 0)),
        out_shape=jax.ShapeDtypeStruct((T, 512), F32),
        compiler_params=_cparams(("parallel",)),
    )(sinks, proj, proj, proj, proj, proj, proj, _row(jnp.tile(q_norm_g, ATT_HEADS)), _row(jnp.tile(k_norm_g, 2)),
      _head_mean_matrix())


def _rms_bwd(dy, xh, r, g):
    dxh = dy * g
    return r * (dxh - xh * jnp.mean(dxh * xh, axis=-1, keepdims=True)), dy * xh


def _attn_bwd(dproj, proj, dya, q_norm_g, k_norm_g, sinks, name):
    T = proj.shape[0]
    nb = T // WINDOW

    def body(sink_ref, dp_any, q_ref, z_ref, kc_ref, kp_ref, vc_ref, vp_ref, dy_ref, qg_ref, kg_ref, hm_ref,
             dqz_ref, dkv_ref, gq_ref, gk_ref, gs_ref, ck_ref, cv_ref, o_sc, dq_sc):
        n = pl.program_id(0)

        @pl.when(n == 0)
        def _():
            gq_ref[...] = jnp.zeros_like(gq_ref)
            gk_ref[...] = jnp.zeros_like(gk_ref)
            gs_ref[...] = jnp.zeros_like(gs_ref)
            ck_ref[...] = jnp.zeros_like(ck_ref)
            cv_ref[...] = jnp.zeros_like(cv_ref)

        lane8 = lax.broadcasted_iota(jnp.int32, (1, 8), 1)

        @pl.when(n < nb)
        def _():
            valid, distf = _attn_mask(n)
            k2 = jnp.concatenate([kp_ref[...], kc_ref[...]], axis=0)
            v2 = jnp.concatenate([vp_ref[...], vc_ref[...]], axis=0).astype(BF16)
            kn = (_head_rms(k2, hm_ref[:128, :128])[0] * kg_ref[...]).astype(BF16)
            qh, qr = _head_rms(q_ref[...], hm_ref[...])
            qn = ((qh * qg_ref[...]) * (ATT_HD ** -0.5)).astype(BF16)
            zs = z_ref[...]
            do_all = dy_ref[...] * _silu(zs)
            dob_all = do_all.astype(BF16)

            def head(h):
                sl, gsl = slice(64 * h, 64 * h + 64), slice(64 * (h // 4), 64 * (h // 4) + 64)
                s = _dot(qn[:, sl], kn[:, gsl], NT)
                dpm = _dot(dob_all[:, sl], v2[:, gsl], NT)
                yield
                p, ps = _attn_probs(s, h, sink_ref[h], valid, distf)
                pb = p.astype(BF16)
                o_sc[:, sl] = _dot(pb, v2[:, gsl])
                dvg = _dot(pb, dob_all[:, sl], TN)
                delta = jnp.sum(p * dpm, axis=-1, keepdims=True)
                ds = (p * (dpm - delta)).astype(BF16)
                gs = jnp.where(lane8 == h, -jnp.sum(ps * delta, axis=0, keepdims=True), 0.0)
                yield
                dkn = _dot(ds, qn[:, sl], TN)
                dq_sc[:, sl] = _dot(ds, kn[:, gsl])
                yield
                return dkn, dvg, gs

            res = _lockstep([head(h) for h in range(ATT_HEADS)])
            dqz_ref[:, 512:] = (dy_ref[...] * o_sc[...] * _dsilu(zs)).astype(BF16)
            dq, gq = _head_rms_bwd(dq_sc[...] * (ATT_HD ** -0.5), qh, qr, qg_ref[...], hm_ref[...])
            dqz_ref[:, :512] = dq.astype(BF16)
            gq_acc = jnp.sum(gq, axis=0, keepdims=True)
            gs_acc = sum(r[2] for r in res[1:]) + res[0][2]
            for g in range(2):
                dkn = (res[4 * g][0] + res[4 * g + 1][0]) + (res[4 * g + 2][0] + res[4 * g + 3][0])
                dvg = (res[4 * g][1] + res[4 * g + 1][1]) + (res[4 * g + 2][1] + res[4 * g + 3][1])
                ksl = slice(64 * g, 64 * g + 64)
                vsl = slice(128 + 64 * g, 128 + 64 * g + 64)
                dkv_ref[:, ksl] = ck_ref[:, ksl] + dkn[:WINDOW]
                dkv_ref[:, vsl] = cv_ref[:, ksl] + dvg[:WINDOW]
                ck_ref[:, ksl] = dkn[WINDOW:]
                cv_ref[:, ksl] = dvg[WINDOW:]
            gq_ref[...] += gq_acc
            gs_ref[...] += gs_acc

        @pl.when(n == nb)
        def _():
            dkv_ref[:, :128] = ck_ref[...]
            dkv_ref[:, 128:] = cv_ref[...]

        @pl.when(n > 0)
        def _():
            hm = hm_ref[:128, :128]
            kh, kr = _head_rms(kp_ref[...], hm)
            dk, gk = _head_rms_bwd(dkv_ref[:, :128], kh, kr, kg_ref[...], hm)
            dkv_ref[:, :128] = dk
            gk_ref[...] += jnp.sum(gk, axis=0, keepdims=True)

    cur = lambda n: jnp.minimum(n, nb - 1)
    prev = lambda n: jnp.maximum(n - 1, 0)
    small = lambda w: pl.BlockSpec((1, w), lambda n: (0, 0))
    return pl.pallas_call(
        body, name=name, grid=(nb + 1,),
        in_specs=[pl.BlockSpec(memory_space=pltpu.SMEM), pl.BlockSpec(memory_space=pl.ANY),
                  pl.BlockSpec((WINDOW, 512), lambda n: (cur(n), C_QA // 512)),
                  pl.BlockSpec((WINDOW, 512), lambda n: (cur(n), C_ZA // 512)),
                  pl.BlockSpec((WINDOW, 128), lambda n: (cur(n), C_KA // 128)),
                  pl.BlockSpec((WINDOW, 128), lambda n: (prev(n), C_KA // 128)),
                  pl.BlockSpec((WINDOW, 128), lambda n: (cur(n), C_VA // 128)),
                  pl.BlockSpec((WINDOW, 128), lambda n: (prev(n), C_VA // 128)),
                  pl.BlockSpec((WINDOW, 512), lambda n: (cur(n), 0)),
                  small(512), small(128), pl.BlockSpec((512, 512), lambda n: (0, 0))],
        out_specs=[pl.BlockSpec((WINDOW, 1024), lambda n: (cur(n), C_QA // 1024)),
                   pl.BlockSpec((WINDOW, 256), lambda n: (prev(n), 0)),
                   small(512), small(128), small(8)],
        out_shape=[jax.ShapeDtypeStruct(dproj.shape, BF16), jax.ShapeDtypeStruct((T, 256), F32),
                   jax.ShapeDtypeStruct((1, 512), F32), jax.ShapeDtypeStruct((1, 128), F32),
                   jax.ShapeDtypeStruct((1, 8), F32)],
        scratch_shapes=[pltpu.VMEM((WINDOW, 128), F32), pltpu.VMEM((WINDOW, 128), F32),
                        pltpu.VMEM((WINDOW, 512), F32), pltpu.VMEM((WINDOW, 512), F32)],
        input_output_aliases={1: 0},
        compiler_params=_cparams(("arbitrary",)),
    )(sinks, dproj, proj, proj, proj, proj, proj, proj, dya, _row(jnp.tile(q_norm_g, ATT_HEADS)),
      _row(jnp.tile(k_norm_g, 2)), _head_mean_matrix())


HALO_B = 32


def _conf_specs(T, tm):
    r = tm // HALO_B
    cur = lambda c: pl.BlockSpec((tm, 512), lambda i: (i, c // 512))
    prev = lambda c: pl.BlockSpec((HALO_B, 512), lambda i: (jnp.maximum(i * r - 1, 0), c // 512))
    return cur, prev


SUB = 8
ROW_CHUNK = 64


def _shifted_copies(ext_ref, sh_ref):
    total = ext_ref.shape[0]
    for r in range(SUB):
        rows = total if r == 0 else total - SUB
        sh_ref[r, :rows, :] = ext_ref[pl.ds(r, rows), :]


def _taps_by_shift(offsets):
    groups = {}
    for k, o in enumerate(offsets):
        q, r = divmod(o, SUB)
        groups.setdefault(r, []).append((k, q))
    return groups


def _conv_taps(sh_ref, w_ref, offsets, out_ref, init):
    groups = _taps_by_shift(offsets)

    def chunk(ci, carry):
        r0 = pl.multiple_of(ci * ROW_CHUNK, ROW_CHUNK)
        acc = jnp.zeros((ROW_CHUNK, out_ref.shape[1]), F32) + init
        for r, taps in groups.items():
            win = sh_ref[r, pl.ds(r0, ROW_CHUNK + SUB * max(q for _, q in taps)), :]
            for k, q in taps:
                acc = acc + w_ref[k:k + 1, :] * win[SUB * q:SUB * q + ROW_CHUNK]
        out_ref[pl.ds(r0, ROW_CHUNK), :] = acc
        return carry

    lax.fori_loop(0, out_ref.shape[0] // ROW_CHUNK, chunk, 0)


def _conv_weight_grad(sh_ref, d_ref, offsets):
    tm, width = d_ref.shape
    out = [None] * len(offsets)
    for r, taps in _taps_by_shift(offsets).items():
        def chunk(ci, accs, r=r, taps=taps):
            r0 = pl.multiple_of(ci * ROW_CHUNK, ROW_CHUNK)
            d = d_ref[pl.ds(r0, ROW_CHUNK), :]
            win = sh_ref[r, pl.ds(r0, ROW_CHUNK + SUB * max(q for _, q in taps)), :]
            return tuple(a + jnp.sum((d * win[SUB * q:SUB * q + ROW_CHUNK]).reshape(ROW_CHUNK // SUB, SUB, width),
                                     axis=0) for a, (_, q) in zip(accs, taps))

        accs = lax.fori_loop(0, tm // ROW_CHUNK, chunk, tuple(jnp.zeros((SUB, width), F32) for _ in taps))
        for a, (k, _) in zip(accs, taps):
            out[k] = jnp.sum(a, axis=0, keepdims=True)
    return out


def _conf_scratch(tm):
    return [pltpu.VMEM((tm + HALO_B, 512), F32), pltpu.VMEM((SUB, tm + HALO_B, 512), F32), pltpu.VMEM((tm, 512), F32)]


def _conf_core(i, tm, gv_ref, gg_ref, gvp_ref, ggp_ref, w_ref, b_ref, lg_ref, lb_ref, pw_ref, pb_ref, ext_ref, sh_ref,
               cv_ref):
    up = gvp_ref[...] * _sig(ggp_ref[...])
    ext_ref[:HALO_B] = jnp.where(i > 0, up, 0.0)
    ext_ref[HALO_B:] = gv_ref[...] * _sig(gg_ref[...])
    _shifted_copies(ext_ref, sh_ref)
    _conv_taps(sh_ref, w_ref, [HALO_B - CONV_K + 1 + k for k in range(CONV_K)], cv_ref, b_ref[...])
    acc = cv_ref[...]
    mu = jnp.mean(acc, axis=-1, keepdims=True)
    xc = acc - mu
    rstd = lax.rsqrt(jnp.mean(xc * xc, axis=-1, keepdims=True) + EPS)
    xh = xc * rstd
    u2 = xh * lg_ref[...] + lb_ref[...]
    u3 = _silu(u2)
    ypre = _bdot(u3, pw_ref[...]) + pb_ref[...]
    return xh, rstd, u2, u3, ypre


def _conf_fwd(proj, dw_w, dw_b, ln_g, ln_b, pw2, pw2_b, name):
    T = proj.shape[0]
    tm = min(512, T)
    cur, prev = _conf_specs(T, tm)

    def body(gv_ref, gg_ref, gvp_ref, ggp_ref, zb_ref, w_ref, b_ref, lg_ref, lb_ref, pw_ref, pb_ref, o_ref, *scratch):
        i = pl.program_id(0)
        ypre = _conf_core(i, tm, gv_ref, gg_ref, gvp_ref, ggp_ref, w_ref, b_ref, lg_ref, lb_ref, pw_ref, pb_ref,
                          *scratch)[4]
        o_ref[...] = ypre * _silu(zb_ref[...])

    full = lambda s: pl.BlockSpec(s, lambda i: (0, 0))
    return pl.pallas_call(
        body, name=name, grid=(T // tm,),
        in_specs=[cur(C_GV), cur(C_GG), prev(C_GV), prev(C_GG), cur(C_ZB), full((CONV_K, 512)), full((1, 512)),
                  full((1, 512)), full((1, 512)), full((512, 512)), full((1, 512))],
        out_specs=pl.BlockSpec((tm, 512), lambda i: (i, 0)),
        out_shape=jax.ShapeDtypeStruct((T, 512), F32),
        scratch_shapes=_conf_scratch(tm),
        compiler_params=_cparams(("parallel",)),
    )(proj, proj, proj, proj, proj, dw_w, _row(dw_b), _row(ln_g), _row(ln_b), pw2, _row(pw2_b))


def _conf_bwd1(dproj, proj, dyb, dw_w, dw_b, ln_g, ln_b, pw2, pw2_b, name):
    T = proj.shape[0]
    tm = min(512, T)
    cur, prev = _conf_specs(T, tm)

    def body(dp_any, gv_ref, gg_ref, gvp_ref, ggp_ref, zb_ref, dy_ref, w_ref, b_ref, lg_ref, lb_ref, pw_ref, pb_ref,
             dzb_ref, du1_ref, gpw_ref, st_ref, *scratch):
        i = pl.program_id(0)
        xh, rstd, u2, u3, ypre = _conf_core(i, tm, gv_ref, gg_ref, gvp_ref, ggp_ref, w_ref, b_ref, lg_ref, lb_ref,
                                            pw_ref, pb_ref, *scratch)
        zb, dy = zb_ref[...], dy_ref[...]
        dzb_ref[...] = (dy * ypre * _dsilu(zb)).astype(BF16)
        dyp = dy * _silu(zb)
        du2 = _bdot(dyp, pw_ref[...], NT) * _dsilu(u2)
        dxh = du2 * lg_ref[...]
        du1 = rstd * (dxh - jnp.mean(dxh, axis=-1, keepdims=True) - xh * jnp.mean(dxh * xh, axis=-1, keepdims=True))
        du1_ref[...] = du1
        gpw = _bdot(u3, dyp, TN)
        rs = lambda a: jnp.sum(a, axis=0, keepdims=True)
        upd = jnp.concatenate([rs(dyp), rs(du2 * xh), rs(du2), rs(du1), jnp.zeros((4, 512), F32)], axis=0)

        @pl.when(i == 0)
        def _():
            gpw_ref[...] = gpw
            st_ref[...] = upd

        @pl.when(i > 0)
        def _():
            gpw_ref[...] += gpw
            st_ref[...] += upd

    full = lambda s: pl.BlockSpec(s, lambda i: (0, 0))
    blk = pl.BlockSpec((tm, 512), lambda i: (i, 0))
    return pl.pallas_call(
        body, name=name, grid=(T // tm,),
        in_specs=[pl.BlockSpec(memory_space=pl.ANY), cur(C_GV), cur(C_GG), prev(C_GV), prev(C_GG), cur(C_ZB), blk,
                  full((CONV_K, 512)), full((1, 512)), full((1, 512)), full((1, 512)), full((512, 512)), full((1, 512))],
        out_specs=[cur(C_ZB), blk, full((512, 512)), full((8, 512))],
        out_shape=[jax.ShapeDtypeStruct(dproj.shape, BF16), jax.ShapeDtypeStruct((T, 512), F32),
                   jax.ShapeDtypeStruct((512, 512), F32), jax.ShapeDtypeStruct((8, 512), F32)],
        scratch_shapes=_conf_scratch(tm),
        input_output_aliases={0: 0},
        compiler_params=_cparams(("arbitrary",)),
    )(dproj, proj, proj, proj, proj, proj, dyb, dw_w, _row(dw_b), _row(ln_g), _row(ln_b), pw2, _row(pw2_b))


def _conf_bwd2(dproj, proj, du1, dw_w, name):
    T = proj.shape[0]
    tm = min(512, T)
    nt = T // tm
    r = tm // HALO_B
    cur, prev = _conf_specs(T, tm)

    def body(dp_any, gv_ref, gg_ref, gvp_ref, ggp_ref, du_ref, dun_ref, w_ref, dglu_ref, gw_ref, ext_ref, sh_ref,
             cv_ref):
        i = pl.program_id(0)
        gv, sg = gv_ref[...], _sig(gg_ref[...])
        ext_ref[:HALO_B] = jnp.where(i > 0, gvp_ref[...] * _sig(ggp_ref[...]), 0.0)
        ext_ref[HALO_B:] = gv * sg
        _shifted_copies(ext_ref, sh_ref)
        rows = _conv_weight_grad(sh_ref, du_ref, [HALO_B - CONV_K + 1 + k for k in range(CONV_K)])
        rows.append(jnp.zeros((1, 512), F32))
        gw = jnp.concatenate(rows, axis=0)
        ext_ref[:tm] = du_ref[...]
        ext_ref[tm:] = jnp.where(i < nt - 1, dun_ref[...], 0.0)
        _shifted_copies(ext_ref, sh_ref)
        _conv_taps(sh_ref, w_ref, [CONV_K - 1 - k for k in range(CONV_K)], cv_ref, 0.0)
        du0 = cv_ref[...]
        dglu_ref[:, :512] = (du0 * sg).astype(BF16)
        dglu_ref[:, 512:] = (du0 * gv * sg * (1.0 - sg)).astype(BF16)

        @pl.when(i == 0)
        def _():
            gw_ref[...] = gw

        @pl.when(i > 0)
        def _():
            gw_ref[...] += gw

    full = lambda s: pl.BlockSpec(s, lambda i: (0, 0))
    return pl.pallas_call(
        body, name=name, grid=(nt,),
        in_specs=[pl.BlockSpec(memory_space=pl.ANY), cur(C_GV), cur(C_GG), prev(C_GV), prev(C_GG),
                  pl.BlockSpec((tm, 512), lambda i: (i, 0)),
                  pl.BlockSpec((HALO_B, 512), lambda i: (jnp.minimum((i + 1) * r, T // HALO_B - 1), 0)),
                  full((CONV_K, 512))],
        out_specs=[pl.BlockSpec((tm, 1024), lambda i: (i, C_GV // 1024)), full((32, 512))],
        out_shape=[jax.ShapeDtypeStruct(dproj.shape, BF16), jax.ShapeDtypeStruct((32, 512), F32)],
        scratch_shapes=_conf_scratch(tm),
        input_output_aliases={0: 0},
        compiler_params=_cparams(("arbitrary",)),
    )(dproj, proj, proj, proj, proj, du1, du1, dw_w)


HALO_C = 8
QKV_C = 1536


def _softplus(x):
    return jnp.maximum(x, 0.0) + jnp.log1p(jnp.exp(-jnp.abs(x)))


def _gdn_conv(i, tm, x_ref, xp_ref, w_ref, ext_ref):
    ext_ref[:HALO_C] = jnp.where(i > 0, xp_ref[...], 0.0)
    ext_ref[HALO_C:] = x_ref[...]
    pre = jnp.zeros((tm, QKV_C), F32)
    for k in range(DN_K):
        pre = pre + w_ref[k:k + 1, :] * ext_ref[pl.ds(HALO_C - DN_K + 1 + k, tm), :]
    return pre


def _gdn_specs(T, tm):
    r = tm // HALO_C
    cur = pl.BlockSpec((tm, QKV_C), lambda i: (i, C_QC // QKV_C))
    prev = pl.BlockSpec((HALO_C, QKV_C), lambda i: (jnp.maximum(i * r - 1, 0), C_QC // QKV_C))
    ab = pl.BlockSpec((tm, 128), lambda i: (i, C_AB // 128))
    return cur, prev, ab


def _gdn_prep_fwd(proj, sconv_w, alog_v, dtb_v, name):
    T = proj.shape[0]
    tm = min(512, T)
    cur, prev, ab = _gdn_specs(T, tm)

    def body(x_ref, xp_ref, ab_ref, w_ref, al_ref, dt_ref, q_ref, k_ref, v_ref, gb_ref, ext_ref):
        i = pl.program_id(0)
        y = _silu(_gdn_conv(i, tm, x_ref, xp_ref, w_ref, ext_ref))
        for h in range(DN_HEADS):
            sl = slice(128 * h, 128 * h + 128)
            qh, kh = y[:, sl], y[:, 512 + 128 * h:512 + 128 * h + 128]
            q_ref[:, sl] = qh * lax.rsqrt(jnp.sum(qh * qh, axis=-1, keepdims=True) + EPS) * (128 ** -0.5)
            k_ref[:, sl] = kh * lax.rsqrt(jnp.sum(kh * kh, axis=-1, keepdims=True) + EPS)
        v_ref[...] = y[:, 1024:]
        abv = ab_ref[...]
        lane = lax.broadcasted_iota(jnp.int32, (tm, 128), 1)
        g = -jnp.exp(al_ref[...]) * _softplus(abv + dt_ref[...])
        gb_ref[...] = jnp.where(lane < DN_HEADS, g, _sig(abv))

    full = lambda s: pl.BlockSpec(s, lambda i: (0, 0))
    blk = pl.BlockSpec((tm, 512), lambda i: (i, 0))
    return pl.pallas_call(
        body, name=name, grid=(T // tm,),
        in_specs=[cur, prev, ab, full((DN_K, QKV_C)), full((1, 128)), full((1, 128))],
        out_specs=[blk, blk, blk, pl.BlockSpec((tm, 128), lambda i: (i, 0))],
        out_shape=[jax.ShapeDtypeStruct((T, 512), F32)] * 3 + [jax.ShapeDtypeStruct((T, 128), F32)],
        scratch_shapes=[pltpu.VMEM((tm + HALO_C, QKV_C), F32)],
        compiler_params=_cparams(("parallel",)),
    )(proj, proj, proj, sconv_w, alog_v, dtb_v)


def _hdot(a, b, dims=NN):
    return _dot(a, b, dims, precision=HI)


def _lockstep(gens):
    results, live = [None] * len(gens), list(range(len(gens)))
    while live:
        for i in list(live):
            try:
                next(gens[i])
            except StopIteration as stop:
                results[i] = stop.value
                live.remove(i)
    return results


def _split(a):
    hi = a.astype(BF16)
    return hi, (a - hi.astype(F32)).astype(BF16)


def _dot_exact(a, b, dims=NN, split_left=True):
    x = (a if split_left else b).astype(F32)
    hi = x.astype(BF16)
    r = x - hi.astype(F32)
    mid = r.astype(BF16)
    lo = (r - mid.astype(F32)).astype(BF16)
    other = (b if split_left else a).astype(BF16)
    one = (lambda p: _dot(p, other, dims)) if split_left else (lambda p: _dot(other, p, dims))
    return (one(lo) + one(mid)) + one(hi)


def _dot3(a, b):
    (ah, al), (bh, bl) = a, b
    return _dot(ah, bh) + (_dot(ah, bl) + _dot(al, bh))


def _tri_inv(mats, eye):
    ps = [-a for a in mats]
    ts = [eye + p for p in ps]
    for _ in range(5):
        sp = [_split(p) for p in ps]
        ps = [_dot3(s, s) for s in sp]
        sp = [_split(p) for p in ps]
        ts = [t + _dot3(_split(t), s) for t, s in zip(ts, sp)]
    return ts


def _tri_consts():
    ii = lax.broadcasted_iota(jnp.int32, (CHUNK, CHUNK), 0)
    jj = lax.broadcasted_iota(jnp.int32, (CHUNK, CHUNK), 1)
    return ii >= jj, ii > jj, (ii == jj).astype(F32)


def _gdn_local(q, k, v, gcol, grow, bcol, lower, strict):
    dm = jnp.where(lower, jnp.exp(jnp.where(lower, gcol - grow, 0.0)), 0.0)
    kb = k * bcol
    a = jnp.where(strict, _bdot(kb, k, NT) * dm, 0.0)
    gc = jnp.exp(gcol)
    glast = grow[:, CHUNK - 1:CHUNK]
    return dict(q=q, k=k, v=v, bcol=bcol, gcol=gcol, glast=glast, dm=dm, kb=kb, a=a, gc=gc, vb=v * bcol,
                kbg=kb * gc, p=_bdot(q, k, NT) * dm, qe=q * gc, ke=k * jnp.exp(glast - gcol))


def _gdn_chunk_bwd(c, do, dvn, ds_new, lower, strict, ones):
    rs = lambda m: jnp.sum(m, axis=-1, keepdims=True)
    colsum = lambda m: _dot_exact(m, ones, TN)[:, :1]
    q, k, v, bcol, dm, tm, gc, s = c["q"], c["k"], c["v"], c["bcol"], c["dm"], c["tm"], c["gc"], c["s"]
    eg = jnp.exp(c["glast"])
    dqe = _bdot(do, s, NT)
    dp = jnp.where(lower, _bdot(do, c["vn"], NT), 0.0)
    dw = -_bdot(dvn, s, NT)
    dke = _bdot(c["vn"], ds_new, NT)
    dvb = _bdot(tm, dvn, TN)
    yield
    dglast = jnp.sum(rs(ds_new * s), axis=0, keepdims=True) * eg
    dk = dke * jnp.exp(c["glast"] - c["gcol"])
    r_ke = rs(dke * c["ke"])
    dglast = dglast + jnp.sum(r_ke, axis=0, keepdims=True)
    dgam = rs(dqe * c["qe"]) - r_ke
    dq = dqe * gc
    dpm = dp * dm
    mp = dp * c["p"]
    dq = dq + _bdot(dpm, k)
    dk = dk + _bdot(dpm, q, TN)
    dt = _bdot(dvn, c["vb"], NT) + _bdot(dw, c["kbg"], NT)
    dkbg = _bdot(tm, dw, TN)
    dgam = dgam + rs(mp) - colsum(mp)
    yield
    dkb = dkbg * gc
    dgam = dgam + rs(dkbg * c["kbg"])
    dat = _bdot(tm, dt, TN)
    yield
    da = jnp.where(strict, -_bdot(dat, tm, NT), 0.0)
    yield
    dam = da * dm
    ma = da * c["a"]
    dkb = dkb + _bdot(dam, k)
    dk = dk + _bdot(dam, c["kb"], TN)
    dgam = dgam + rs(ma) - colsum(ma)
    yield
    dk = dk + dkb * bcol
    dbeta = rs(dkb * k) + rs(dvb * v)
    dv = dvb * bcol
    row = lax.broadcasted_iota(jnp.int32, (CHUNK, 1), 0)
    dgam = dgam + jnp.where(row == CHUNK - 1, dglast, 0.0)
    dg = _dot_exact(lower, dgam, TN, split_left=False)
    return dq, dk, dv, dg, dbeta


SCAN_GROUP = 4
GROUP = 4


def _chunk_decay(gb_ref, gt_ref, lmat, g):
    rows = slice(CHUNK * g, CHUNK * g + CHUNK)
    return rows, _dot_exact(lmat, gb_ref[rows, :], split_left=False), _dot_exact(gt_ref[g], lmat, NT)


def _gdn_chunk_fwd(qd, kd, vd, gb, gbt, name):
    T = qd.shape[0]
    G = GROUP
    ng = T // (CHUNK * G)

    def body(q_ref, k_ref, v_ref, gb_ref, gt_ref, u_ref, w_ref, qe_ref, ke_ref, p_ref, t_ref, eg_ref):
        lower, strict, eye = _tri_consts()
        lmat = lower.astype(F32)
        decay = [_chunk_decay(gb_ref, gt_ref, lmat, g) for g in range(G)]
        chains = [(g, h) for g in range(G) for h in range(DN_HEADS)]
        cs = []
        for g, h in chains:
            rows, gcs, grs = decay[g]
            sl = slice(128 * h, 128 * h + 128)
            c = _gdn_local(q_ref[rows, sl], k_ref[rows, sl], v_ref[rows, sl], gcs[:, h:h + 1], grs[h:h + 1, :],
                           gb_ref[rows, DN_HEADS + h:DN_HEADS + h + 1], lower, strict)
            qe_ref[rows, sl] = c["qe"].astype(BF16)
            ke_ref[rows, sl] = c["ke"].astype(BF16)
            p_ref[rows, 64 * h:64 * h + 64] = c["p"].astype(BF16)
            eg_ref[g, h:h + 1, :] = jnp.broadcast_to(jnp.exp(c["glast"]), (1, 128))
            cs.append(c)
        tms = [t.astype(BF16) for t in _tri_inv([c["a"] for c in cs], eye)]
        us = [_dot(t, c["vb"].astype(BF16)) for t, c in zip(tms, cs)]
        ws = [_dot(t, c["kbg"].astype(BF16)) for t, c in zip(tms, cs)]
        for (g, h), tm, u, w in zip(chains, tms, us, ws):
            rows, sl = decay[g][0], slice(128 * h, 128 * h + 128)
            u_ref[rows, sl] = u
            w_ref[rows, sl] = w.astype(BF16)
            t_ref[rows, 64 * h:64 * h + 64] = tm
        for g in range(G):
            eg_ref[g, DN_HEADS:, :] = jnp.zeros((8 - DN_HEADS, 128), F32)

    blk = pl.BlockSpec((CHUNK * G, 512), lambda n: (n, 0))
    half = pl.BlockSpec((CHUNK * G, 256), lambda n: (n, 0))
    return pl.pallas_call(
        body, name=name, grid=(ng,),
        in_specs=[blk, blk, blk, pl.BlockSpec((CHUNK * G, 128), lambda n: (n, 0)),
                  pl.BlockSpec((G, 8, CHUNK), lambda n: (n, 0, 0))],
        out_specs=[blk, blk, blk, blk, half, half, pl.BlockSpec((G, 8, 128), lambda n: (n, 0, 0))],
        out_shape=[jax.ShapeDtypeStruct((T, 512), F32)] + [jax.ShapeDtypeStruct((T, 512), BF16)] * 3
        + [jax.ShapeDtypeStruct((T, 256), BF16)] * 2 + [jax.ShapeDtypeStruct((T // CHUNK, 8, 128), F32)],
        compiler_params=_cparams(("parallel",)),
    )(qd, kd, vd, gb, gbt)


def _gdn_scan_fwd(u, w, qe, ke, pm, eg, proj, dn_g, name):
    T = u.shape[0]
    nc = T // CHUNK
    G = SCAN_GROUP

    def body(u_ref, w_ref, qe_ref, ke_ref, p_ref, eg_ref, z_ref, ng_ref, y_ref, o_ref, vn_ref, ss_ref, s_ref):
        n = pl.program_id(0)

        @pl.when(n == 0)
        def _():
            s_ref[...] = jnp.zeros_like(s_ref)

        def head(j, h):
            rows, sl = slice(CHUNK * j, CHUNK * j + CHUNK), slice(128 * h, 128 * h + 128)
            s = s_ref[h]
            sb = s.astype(BF16)
            vn = u_ref[rows, sl] - _dot(w_ref[rows, sl], sb)
            qs = _dot(qe_ref[rows, sl], sb)
            yield
            vb = vn.astype(BF16)
            o = qs + _dot(p_ref[rows, 64 * h:64 * h + 64], vb)
            s_ref[h] = s * eg_ref[j, h:h + 1, :] + _dot(ke_ref[rows, sl], vb, TN)
            yield
            vn_ref[rows, sl] = vb
            o_ref[rows, sl] = o
            y_ref[rows, sl] = _rms(o, None)[0] * ng_ref[...] * _silu(z_ref[rows, sl])

        for j in range(G):
            ss_ref[j] = s_ref[...]
            _lockstep([head(j, h) for h in range(DN_HEADS)])

    blk = pl.BlockSpec((CHUNK * G, 512), lambda n: (n, 0))
    return pl.pallas_call(
        body, name=name, grid=(nc // G,),
        in_specs=[blk, blk, blk, blk, pl.BlockSpec((CHUNK * G, 256), lambda n: (n, 0)),
                  pl.BlockSpec((G, 8, 128), lambda n: (n, 0, 0)),
                  pl.BlockSpec((CHUNK * G, 512), lambda n: (n, C_ZC // 512)), pl.BlockSpec((1, 128), lambda n: (0, 0))],
        out_specs=[blk, blk, blk, pl.BlockSpec((G, DN_HEADS, 128, 128), lambda n: (n, 0, 0, 0))],
        out_shape=[jax.ShapeDtypeStruct((T, 512), F32), jax.ShapeDtypeStruct((T, 512), F32),
                   jax.ShapeDtypeStruct((T, 512), BF16), jax.ShapeDtypeStruct((nc, DN_HEADS, 128, 128), F32)],
        scratch_shapes=[pltpu.VMEM((DN_HEADS, 128, 128), F32)],
        compiler_params=_cparams(("arbitrary",)),
    )(u, w, qe, ke, pm, eg, proj, dn_g)


def _gdn_scan_bwd(dproj, w, qe, ke, pm, eg, o, proj, dyc, dn_g, name):
    T = o.shape[0]
    nc = T // CHUNK
    G = SCAN_GROUP
    rev = lambda n: nc // G - 1 - n

    def body(dp_any, w_ref, qe_ref, ke_ref, p_ref, eg_ref, o_ref, z_ref, dy_ref, ng_ref,
             dz_ref, do_ref, dvn_ref, dsn_ref, gng_ref, ds_ref):
        n = pl.program_id(0)

        @pl.when(n == 0)
        def _():
            ds_ref[...] = jnp.zeros_like(ds_ref)
            gng_ref[...] = jnp.zeros_like(gng_ref)

        def head(j, h):
            rows, sl = slice(CHUNK * j, CHUNK * j + CHUNK), slice(128 * h, 128 * h + 128)
            oh, r = _rms(o_ref[rows, sl], None)
            z, dy = z_ref[rows, sl], dy_ref[rows, sl]
            dz_ref[rows, sl] = (dy * (oh * ng_ref[...]) * _dsilu(z)).astype(BF16)
            do, gg = _rms_bwd(dy * _silu(z), oh, r, ng_ref[...])
            dob = do.astype(BF16)
            ds = ds_ref[h]
            dvn = _dot(p_ref[rows, 64 * h:64 * h + 64], dob, TN) + _dot(ke_ref[rows, sl], ds.astype(BF16))
            qd = _dot(qe_ref[rows, sl], dob, TN)
            yield
            dvb = dvn.astype(BF16)
            ds_ref[h] = qd + eg_ref[j, h:h + 1, :] * ds - _dot(w_ref[rows, sl], dvb, TN)
            do_ref[rows, sl] = dob
            dvn_ref[rows, sl] = dvb
            return jnp.sum(gg, axis=0, keepdims=True)

        for j in reversed(range(G)):
            dsn_ref[j] = ds_ref[...]
            gng = _lockstep([head(j, h) for h in range(DN_HEADS)])
            gng_ref[...] += (gng[0] + gng[1]) + (gng[2] + gng[3])

    blk = pl.BlockSpec((CHUNK * G, 512), lambda n: (rev(n), 0))
    state = pl.BlockSpec((G, DN_HEADS, 128, 128), lambda n: (rev(n), 0, 0, 0))
    return pl.pallas_call(
        body, name=name, grid=(nc // G,),
        in_specs=[pl.BlockSpec(memory_space=pl.ANY), blk, blk, blk,
                  pl.BlockSpec((CHUNK * G, 256), lambda n: (rev(n), 0)),
                  pl.BlockSpec((G, 8, 128), lambda n: (rev(n), 0, 0)), blk,
                  pl.BlockSpec((CHUNK * G, 512), lambda n: (rev(n), C_ZC // 512)), blk,
                  pl.BlockSpec((1, 128), lambda n: (0, 0))],
        out_specs=[pl.BlockSpec((CHUNK * G, 512), lambda n: (rev(n), C_ZC // 512)), blk, blk, state,
                   pl.BlockSpec((1, 128), lambda n: (0, 0))],
        out_shape=[jax.ShapeDtypeStruct(dproj.shape, BF16), jax.ShapeDtypeStruct((T, 512), BF16),
                   jax.ShapeDtypeStruct((T, 512), BF16), jax.ShapeDtypeStruct((nc, DN_HEADS, 128, 128), F32),
                   jax.ShapeDtypeStruct((1, 128), F32)],
        scratch_shapes=[pltpu.VMEM((DN_HEADS, 128, 128), F32)],
        input_output_aliases={0: 0},
        compiler_params=_cparams(("arbitrary",)),
    )(dproj, w, qe, ke, pm, eg, o, proj, dyc, dn_g)


def _gdn_chunk_grad(qd, kd, vd, gb, gbt, tmi, ssave, dsn, do, dvn, vn, name):
    T = qd.shape[0]
    G = GROUP
    ng = T // (CHUNK * G)

    def body(q_ref, k_ref, v_ref, gb_ref, gt_ref, t_ref, ss_ref, dsn_ref, do_ref, dvn_ref, vn_ref,
             dq_ref, dk_ref, dv_ref, dgb_ref):
        lower, strict, _ = _tri_consts()
        lmat = lower.astype(F32)
        ones = jnp.ones((CHUNK, 128), F32)
        lane = lax.broadcasted_iota(jnp.int32, (CHUNK, 128), 1)
        decay = [_chunk_decay(gb_ref, gt_ref, lmat, g) for g in range(G)]
        chains = [(g, h) for g in range(G) for h in range(DN_HEADS)]
        gens = []
        for g, h in chains:
            rows, gcs, grs = decay[g]
            sl = slice(128 * h, 128 * h + 128)
            c = _gdn_local(q_ref[rows, sl], k_ref[rows, sl], v_ref[rows, sl], gcs[:, h:h + 1], grs[h:h + 1, :],
                           gb_ref[rows, DN_HEADS + h:DN_HEADS + h + 1], lower, strict)
            c.update(tm=t_ref[rows, 64 * h:64 * h + 64], s=ss_ref[g, h], vn=vn_ref[rows, sl])
            gens.append(_gdn_chunk_bwd(c, do_ref[rows, sl], dvn_ref[rows, sl], dsn_ref[g, h], lower, strict, ones))
        dgb = [jnp.zeros((CHUNK, 128), F32) for _ in range(G)]
        for (g, h), (dq, dk, dv, dg, dbeta) in zip(chains, _lockstep(gens)):
            rows, sl = decay[g][0], slice(128 * h, 128 * h + 128)
            dq_ref[rows, sl], dk_ref[rows, sl], dv_ref[rows, sl] = dq, dk, dv
            dgb[g] = dgb[g] + jnp.where(lane == h, dg, 0.0) + jnp.where(lane == DN_HEADS + h, dbeta, 0.0)
        for g in range(G):
            dgb_ref[decay[g][0], :] = dgb[g]

    blk = pl.BlockSpec((CHUNK * G, 512), lambda n: (n, 0))
    half = pl.BlockSpec((CHUNK * G, 256), lambda n: (n, 0))
    nar = pl.BlockSpec((CHUNK * G, 128), lambda n: (n, 0))
    state = pl.BlockSpec((G, DN_HEADS, 128, 128), lambda n: (n, 0, 0, 0))
    return pl.pallas_call(
        body, name=name, grid=(ng,),
        in_specs=[blk, blk, blk, nar, pl.BlockSpec((G, 8, CHUNK), lambda n: (n, 0, 0)), half, state, state,
                  blk, blk, blk],
        out_specs=[blk, blk, blk, nar],
        out_shape=[jax.ShapeDtypeStruct((T, 512), F32)] * 3 + [jax.ShapeDtypeStruct((T, 128), F32)],
        compiler_params=_cparams(("parallel",)),
    )(qd, kd, vd, gb, gbt, tmi, ssave, dsn, do, dvn, vn)


def _gdn_prep_bwd1(dproj, proj, dqd, dkd, dvd, dgb, dkv_a, sconv_w, alog_v, dtb_v, name):
    T = proj.shape[0]
    tm = min(512, T)
    cur, prev, ab = _gdn_specs(T, tm)

    def body(dp_any, x_ref, xp_ref, ab_ref, dq_ref, dk_ref, dv_ref, dgb_ref, dkv_ref, w_ref, al_ref, dt_ref,
             o_ref, dpre_ref, st_ref, ext_ref):
        i = pl.program_id(0)
        pre = _gdn_conv(i, tm, x_ref, xp_ref, w_ref, ext_ref)
        y, dsl = _silu(pre), _dsilu(pre)
        for h in range(DN_HEADS):
            for base, g_ref, scale in ((0, dq_ref, 128 ** -0.5), (512, dk_ref, 1.0)):
                sl = slice(base + 128 * h, base + 128 * h + 128)
                xh = y[:, sl]
                r = lax.rsqrt(jnp.sum(xh * xh, axis=-1, keepdims=True) + EPS)
                xn = xh * r
                gy = g_ref[:, 128 * h:128 * h + 128]
                dpre_ref[:, sl] = (scale * r) * (gy - xn * jnp.sum(gy * xn, axis=-1, keepdims=True)) * dsl[:, sl]
        dpre_ref[:, 1024:] = dv_ref[...] * dsl[:, 1024:]
        abv, dgb = ab_ref[...], dgb_ref[...]
        lane = lax.broadcasted_iota(jnp.int32, (tm, 128), 1)
        na = -jnp.exp(al_ref[...])
        xs = abv + dt_ref[...]
        da = dgb * na * _sig(xs)
        b = _sig(abv)
        o_ref[:, :256] = dkv_ref[...].astype(BF16)
        o_ref[:, 256:] = jnp.where(lane < DN_HEADS, da,
                                   jnp.where(lane < 2 * DN_HEADS, dgb * b * (1.0 - b), 0.0)).astype(BF16)
        head = lane < DN_HEADS
        upd = jnp.concatenate([jnp.sum(jnp.where(head, dgb * na * _softplus(xs), 0.0), axis=0, keepdims=True),
                               jnp.sum(jnp.where(head, da, 0.0), axis=0, keepdims=True), jnp.zeros((6, 128), F32)],
                              axis=0)

        @pl.when(i == 0)
        def _():
            st_ref[...] = upd

        @pl.when(i > 0)
        def _():
            st_ref[...] += upd

    full = lambda s: pl.BlockSpec(s, lambda i: (0, 0))
    blk = pl.BlockSpec((tm, 512), lambda i: (i, 0))
    return pl.pallas_call(
        body, name=name, grid=(T // tm,),
        in_specs=[pl.BlockSpec(memory_space=pl.ANY), cur, prev, ab, blk, blk, blk,
                  pl.BlockSpec((tm, 128), lambda i: (i, 0)), pl.BlockSpec((tm, 256), lambda i: (i, 0)),
                  full((DN_K, QKV_C)), full((1, 128)), full((1, 128))],
        out_specs=[pl.BlockSpec((tm, 384), lambda i: (i, C_KA // 384)),
                   pl.BlockSpec((tm, QKV_C), lambda i: (i, 0)), full((8, 128))],
        out_shape=[jax.ShapeDtypeStruct(dproj.shape, BF16), jax.ShapeDtypeStruct((T, QKV_C), F32),
                   jax.ShapeDtypeStruct((8, 128), F32)],
        scratch_shapes=[pltpu.VMEM((tm + HALO_C, QKV_C), F32)],
        input_output_aliases={0: 0},
        compiler_params=_cparams(("arbitrary",)),
    )(dproj, proj, proj, proj, dqd, dkd, dvd, dgb, dkv_a, sconv_w, alog_v, dtb_v)


def _gdn_prep_bwd2(dproj, proj, dpre, sconv_w, name):
    T = proj.shape[0]
    tm = min(512, T)
    nt = T // tm
    r = tm // HALO_C
    cur, prev, _ = _gdn_specs(T, tm)

    def body(dp_any, x_ref, xp_ref, d_ref, dn_ref, w_ref, dx_ref, gw_ref, extx_ref, extd_ref):
        i = pl.program_id(0)
        extx_ref[:HALO_C] = jnp.where(i > 0, xp_ref[...], 0.0)
        extx_ref[HALO_C:] = x_ref[...]
        d = d_ref[...]
        extd_ref[:tm] = d
        extd_ref[tm:] = jnp.where(i < nt - 1, dn_ref[...], 0.0)
        dx = jnp.zeros((tm, QKV_C), F32)
        rows = []
        for k in range(DN_K):
            dx = dx + w_ref[k:k + 1, :] * extd_ref[pl.ds(DN_K - 1 - k, tm), :]
            rows.append(jnp.sum(d * extx_ref[pl.ds(HALO_C - DN_K + 1 + k, tm), :], axis=0, keepdims=True))
        rows.append(jnp.zeros((8 - DN_K, QKV_C), F32))
        gw = jnp.concatenate(rows, axis=0)
        dx_ref[...] = dx.astype(BF16)

        @pl.when(i == 0)
        def _():
            gw_ref[...] = gw

        @pl.when(i > 0)
        def _():
            gw_ref[...] += gw

    full = lambda s: pl.BlockSpec(s, lambda i: (0, 0))
    return pl.pallas_call(
        body, name=name, grid=(nt,),
        in_specs=[pl.BlockSpec(memory_space=pl.ANY), cur, prev, pl.BlockSpec((tm, QKV_C), lambda i: (i, 0)),
                  pl.BlockSpec((HALO_C, QKV_C), lambda i: (jnp.minimum((i + 1) * r, T // HALO_C - 1), 0)),
                  full((DN_K, QKV_C))],
        out_specs=[cur, full((8, QKV_C))],
        out_shape=[jax.ShapeDtypeStruct(dproj.shape, BF16), jax.ShapeDtypeStruct((8, QKV_C), F32)],
        scratch_shapes=[pltpu.VMEM((tm + HALO_C, QKV_C), F32), pltpu.VMEM((tm + HALO_C, QKV_C), F32)],
        input_output_aliases={0: 0},
        compiler_params=_cparams(("arbitrary",)),
    )(dproj, proj, proj, dpre, dpre, sconv_w)


def _merge_fwd(x, proj, ya, yb, yc, wa, wb, wc, wo, gate, name):
    T = x.shape[0]
    tm = min(256, T)

    def body(x_ref, mg_ref, ya_ref, yb_ref, yc_ref, wa_ref, wb_ref, wc_ref, wo_ref, gate_ref, o_ref):
        merged = (_sig(mg_ref[:, :D]) * _bdot(ya_ref[...], wa_ref[...])
                  + _sig(mg_ref[:, D:2 * D]) * _bdot(yb_ref[...], wb_ref[...])
                  + _sig(mg_ref[:, 2 * D:]) * _bdot(yc_ref[...], wc_ref[...]))
        o_ref[...] = x_ref[...] + gate_ref[...] * _bdot(merged, wo_ref[...])

    full = lambda s: pl.BlockSpec(s, lambda i: (0, 0))
    yb_ = pl.BlockSpec((tm, 512), lambda i: (i, 0))
    return pl.pallas_call(
        body, name=name, grid=(T // tm,),
        in_specs=[pl.BlockSpec((tm, D), lambda i: (i, 0)), pl.BlockSpec((tm, 3 * D), lambda i: (i, 0)), yb_, yb_, yb_,
                  full((512, D)), full((512, D)), full((512, D)), full((D, D)), full((1, D))],
        out_specs=pl.BlockSpec((tm, D), lambda i: (i, 0)),
        out_shape=jax.ShapeDtypeStruct((T, D), F32),
        compiler_params=_cparams(("parallel",)),
    )(x, proj, ya, yb, yc, wa, wb, wc, wo, _row(gate))


def _merge_bwd(dout, proj, ya, yb, yc, wa, wb, wc, wo, gate, name):
    T = dout.shape[0]
    tm = min(256, T)
    nt = T // tm

    def body(do_ref, mg_ref, ya_ref, yb_ref, yc_ref, wa_ref, wb_ref, wc_ref, wo_ref, gate_ref,
             dmg_ref, dya_ref, dyb_ref, dyc_ref, gwa_hbm, gwb_hbm, gwc_hbm, gwo_hbm, gg_ref,
             gwa_ref, gwb_ref, gwc_ref, gwo_ref):
        i = pl.program_id(0)

        @pl.when(i == 0)
        def _():
            for r in (gwa_ref, gwb_ref, gwc_ref, gwo_ref, gg_ref):
                r[...] = jnp.zeros_like(r)

        ys = (ya_ref[...], yb_ref[...], yc_ref[...])
        ws = (wa_ref, wb_ref, wc_ref)
        gs = tuple(_sig(mg_ref[:, j * D:(j + 1) * D]) for j in range(3))
        ps = tuple(_bdot(ys[j], ws[j][...]) for j in range(3))
        merged = gs[0] * ps[0] + gs[1] * ps[1] + gs[2] * ps[2]
        mo = _bdot(merged, wo_ref[...])
        do = do_ref[...]
        gg_ref[...] += jnp.sum(do * mo, axis=0, keepdims=True)
        dmo = do * gate_ref[...]
        dmerged = _bdot(dmo, wo_ref[...], NT)
        gwo_ref[...] += _bdot(merged, dmo, TN)
        for j, (dy_ref, gw_ref) in enumerate(((dya_ref, gwa_ref), (dyb_ref, gwb_ref), (dyc_ref, gwc_ref))):
            dp = dmerged * gs[j]
            dmg_ref[:, j * D:(j + 1) * D] = (dmerged * ps[j] * gs[j] * (1.0 - gs[j])).astype(BF16)
            dy_ref[...] = _bdot(dp, ws[j][...], NT)
            gw_ref[...] += _bdot(ys[j], dp, TN)

        @pl.when(i == nt - 1)
        def _():
            for src, dst in ((gwa_ref, gwa_hbm), (gwb_ref, gwb_hbm), (gwc_ref, gwc_hbm), (gwo_ref, gwo_hbm)):
                pltpu.sync_copy(src, dst)

    full = lambda s: pl.BlockSpec(s, lambda i: (0, 0))
    yb_ = pl.BlockSpec((tm, 512), lambda i: (i, 0))
    anyspec = pl.BlockSpec(memory_space=pl.ANY)
    return pl.pallas_call(
        body, name=name, grid=(nt,),
        in_specs=[pl.BlockSpec((tm, D), lambda i: (i, 0)), pl.BlockSpec((tm, 3 * D), lambda i: (i, 0)), yb_, yb_, yb_,
                  full((512, D)), full((512, D)), full((512, D)), full((D, D)), full((1, D))],
        out_specs=[pl.BlockSpec((tm, 3 * D), lambda i: (i, 0)), yb_, yb_, yb_, anyspec, anyspec, anyspec, anyspec,
                   full((1, D))],
        out_shape=[jax.ShapeDtypeStruct((T, NP), BF16)] + [jax.ShapeDtypeStruct((T, 512), F32)] * 3
        + [jax.ShapeDtypeStruct((512, D), F32)] * 3 + [jax.ShapeDtypeStruct((D, D), F32), jax.ShapeDtypeStruct((1, D), F32)],
        scratch_shapes=[pltpu.VMEM((512, D), F32)] * 3 + [pltpu.VMEM((D, D), F32)],
        compiler_params=_cparams(("arbitrary",)),
    )(dout, proj, ya, yb, yc, wa, wb, wc, wo, _row(gate))


def _loss_head(y, tgt, name):
    T = y.shape[0]
    tm = min(512, T)

    def body(y_ref, t_ref, dy_ref, l_ref):
        i = pl.program_id(0)
        diff = y_ref[...] - t_ref[...]
        dy_ref[...] = diff * (1.0 / D)
        part = jnp.sum(diff * diff, axis=0, keepdims=True)

        @pl.when(i == 0)
        def _():
            l_ref[...] = part

        @pl.when(i > 0)
        def _():
            l_ref[...] += part

    blk = pl.BlockSpec((tm, D), lambda i: (i, 0))
    return pl.pallas_call(
        body, name=name, grid=(T // tm,), in_specs=[blk, blk],
        out_specs=[blk, pl.BlockSpec((1, D), lambda i: (0, 0))],
        out_shape=[jax.ShapeDtypeStruct((T, D), F32), jax.ShapeDtypeStruct((1, D), F32)],
        compiler_params=_cparams(("arbitrary",)),
    )(y, tgt)


def _ada_fwd(c_all, w_ada, b_my, name):
    def body(c_ref, w_ref, b_ref, o_ref):
        sc = _silu(c_ref[...])
        for l in range(DEPTH):
            o_ref[l] = _bdot(sc, w_ref[l]) + b_ref[l:l + 1, :]

    return pl.pallas_call(body, name=name, out_shape=jax.ShapeDtypeStruct((DEPTH, N_DEV, w_ada.shape[2]), F32),
                          compiler_params=_cparams())(c_all, w_ada, b_my)


def _ada_bwd(c_all, dmod_my, name):
    def body(c_ref, d_ref, o_ref):
        sc = _silu(c_ref[...])
        for l in range(DEPTH):
            o_ref[l] = _bdot(sc, d_ref[l], TN)

    return pl.pallas_call(body, name=name, out_shape=jax.ShapeDtypeStruct((DEPTH, D, dmod_my.shape[2]), F32),
                          compiler_params=_cparams())(c_all, dmod_my)


def _adam_math(w, g, m, v):
    m = ADAM_B1 * m + (1.0 - ADAM_B1) * g
    v = ADAM_B2 * v + (1.0 - ADAM_B2) * (g * g)
    m_hat = m / (1.0 - ADAM_B1 ** ADAM_STEP)
    v_hat = v / (1.0 - ADAM_B2 ** ADAM_STEP)
    return -ADAM_LR * (m_hat / (jnp.sqrt(v_hat) + ADAM_EPS) + ADAM_WD * w), m, v


def _row_tile(rows, cap):
    best = rows
    for t in range(8, min(rows, cap) + 1, 8):
        if rows % t == 0:
            best = t
    return best if best <= cap else rows


def _adamw(w, g, m, v, name):
    R, C = w.shape
    tr = _row_tile(R, 256)

    def body(w_ref, g_ref, m_ref, v_ref, d_ref, mo_ref, vo_ref):
        d_ref[...], mo_ref[...], vo_ref[...] = _adam_math(w_ref[...], g_ref[...], m_ref[...], v_ref[...])

    blk = pl.BlockSpec((tr, C), lambda i: (i, 0))
    return pl.pallas_call(body, name=name, grid=(R // tr,), in_specs=[blk] * 4, out_specs=[blk] * 3,
                          out_shape=[jax.ShapeDtypeStruct((R, C), F32)] * 3,
                          compiler_params=_cparams(("parallel",)))(w, g, m, v)


def _sum_adamw_many(parts, ws, ms, vs, name):
    n = len(ws)

    def body(*refs):
        ins, outs = refs[:4 * n], refs[4 * n:]
        for i in range(n):
            g = ins[i][0]
            for j in range(1, N_DEV):
                g = g + ins[i][j]
            d, m, v = _adam_math(ins[n + i][...], g, ins[2 * n + i][...], ins[3 * n + i][...])
            outs[i][...], outs[n + i][...], outs[2 * n + i][...], outs[3 * n + i][...] = g, d, m, v

    shapes = [jax.ShapeDtypeStruct(w.shape, F32) for w in ws]
    out = pl.pallas_call(body, name=name, out_shape=shapes * 4, compiler_params=_cparams())(*parts, *ws, *ms, *vs)
    return out[:n], out[n:2 * n], out[2 * n:3 * n], out[3 * n:]


def _sum_adamw(parts0, parts1, w, m, v, name):
    P, R, C = parts0.shape
    tr = _row_tile(R, 128)
    nt = R // tr

    def body(p0_ref, p1_ref, w_ref, m_ref, v_ref, g_ref, d_ref, mo_ref, vo_ref):
        def emit(p_ref):
            g = p_ref[0].astype(F32)
            for j in range(1, P):
                g = g + p_ref[j].astype(F32)
            g_ref[...] = g
            d_ref[...], mo_ref[...], vo_ref[...] = _adam_math(w_ref[...], g, m_ref[...], v_ref[...])

        @pl.when(pl.program_id(0) == 0)
        def _():
            emit(p0_ref)

        @pl.when(pl.program_id(0) == 1)
        def _():
            emit(p1_ref)

    blk = pl.BlockSpec((tr, C), lambda l, i: (l * nt + i, 0))
    return pl.pallas_call(
        body, name=name, grid=(DEPTH, nt),
        in_specs=[pl.BlockSpec((P, tr, C), lambda l, i: (0, i * (1 - l) + (nt - 1) * l, 0)),
                  pl.BlockSpec((P, tr, C), lambda l, i: (0, i * l, 0)), blk, blk, blk],
        out_specs=[blk] * 4, out_shape=[jax.ShapeDtypeStruct((DEPTH * R, C), F32)] * 4,
        compiler_params=_cparams(("arbitrary", "arbitrary")))(parts0, parts1, w, m, v)


def _pair_sum(core, buf, recv, name):
    _, _, R, C = buf.shape
    tr = _row_tile(R, 128)

    def body(c_ref, a_ref, b_ref, o_ref):
        o_ref[...] = (a_ref[:, 0].astype(F32) + b_ref[...].astype(F32)).astype(BF16)

    return pl.pallas_call(
        body, name=name,
        grid_spec=pltpu.PrefetchScalarGridSpec(
            num_scalar_prefetch=1, grid=(R // tr,),
            in_specs=[pl.BlockSpec((4, 1, tr, C), lambda i, c: (0, c[0], i, 0)),
                      pl.BlockSpec((4, tr, C), lambda i, c: (0, i, 0))],
            out_specs=pl.BlockSpec((4, tr, C), lambda i, c: (0, i, 0))),
        out_shape=jax.ShapeDtypeStruct((4, R, C), BF16),
        compiler_params=_cparams(("parallel",)))(core, buf, recv)


SHARD_IN = D_IN // N_DEV


def _w_in_pieces():
    out, p = [], 0
    for a, b in _PAD_FROM:
        for j in range(N_DEV):
            lo, hi = max(a, SHARD_IN * j), min(b, SHARD_IN * (j + 1))
            if lo < hi:
                out.append((j, lo - SHARD_IN * j, hi - SHARD_IN * j, p + lo - a))
        p += b - a
    return out


def _assemble_w_in(gw, name):
    tr = 256
    nt = D // tr

    def body(x_ref, o_ref):
        for j, s0, s1, d0 in _w_in_pieces():
            o_ref[:, d0:d0 + s1 - s0] = x_ref[j, :, s0:s1]
        o_ref[:, D_IN:] = jnp.zeros((tr, NP - D_IN), gw.dtype)

    return pl.pallas_call(
        body, name=name, grid=(nt,),
        in_specs=[pl.BlockSpec((N_DEV, tr, SHARD_IN), lambda i: (0, i, 0))],
        out_specs=pl.BlockSpec((tr, NP), lambda i: (i, 0)),
        out_shape=jax.ShapeDtypeStruct((D, NP), gw.dtype),
        compiler_params=_cparams(("parallel",)))(gw)


def _split_w_in_grad(g, name):
    tr = 256

    def body(g_ref, o_ref):
        for j, s0, s1, d0 in _w_in_pieces():
            o_ref[j, :, s0:s1] = g_ref[:, d0:d0 + s1 - s0].astype(BF16)

    return pl.pallas_call(
        body, name=name, grid=(D // tr,),
        in_specs=[pl.BlockSpec((tr, NP), lambda i: (i, 0))],
        out_specs=pl.BlockSpec((N_DEV, tr, SHARD_IN), lambda i: (0, i, 0)),
        out_shape=jax.ShapeDtypeStruct((N_DEV, D, SHARD_IN), BF16),
        compiler_params=_cparams(("parallel",)))(g)


def _mesh_pos():
    return lax.axis_index("x"), lax.axis_index("y"), lax.axis_index("c")


def _launch(copies, peers, bufs, out_structs, sems, name, collective_id):
    n = len(bufs)
    if collective_id is None:
        anyspec = pl.BlockSpec(memory_space=pl.ANY)
        return list(pl.pallas_call(
            lambda *refs: copies(refs[:n], refs[n:n + len(out_structs)], *refs[n + len(out_structs):]),
            name=name, in_specs=[anyspec] * n, out_specs=[anyspec] * len(out_structs), out_shape=list(out_structs),
            scratch_shapes=list(sems))(*bufs))
    ins = [jax.new_ref(b, memory_space=pltpu.MemorySpace.HBM) for b in bufs]
    outs = [jax.empty_ref(s, memory_space=pltpu.MemorySpace.HBM) for s in out_structs]

    @pl.kernel(mesh=plsc.ScalarSubcoreMesh(axis_name="sequencer", num_cores=1), name=name, scratch_types=tuple(sems),
               compiler_params=pltpu.CompilerParams(collective_id=collective_id))
    def on_sequencer(*sem_refs):
        barrier = pltpu.get_barrier_semaphore()
        targets = peers()
        for p in targets:
            pl.semaphore_signal(barrier, inc=1, device_id=p, device_id_type=pl.DeviceIdType.MESH)
        pl.semaphore_wait(barrier, len(targets))
        copies(ins, outs, *sem_refs)

    on_sequencer()
    return [r[...] for r in outs]


def _all_gather(blocks, name, collective_id=None):
    n = len(blocks)

    def peers():
        x, y, c = _mesh_pos()
        return [(x, y, 1 - c), (1 - x, y, c), (x, 1 - y, c), (1 - x, 1 - y, c)]

    def copies(ins, outs, send_sems, recv_sems, local_sems):
        x, y, c = _mesh_pos()
        me, sibling = (x, y, c), (x, y, 1 - c)
        chips = [(1 - x, y), (x, 1 - y), (1 - x, 1 - y)]
        idx = lambda p: 4 * p[0] + 2 * p[1] + p[2]

        def copy(a, k, block, to, src=None):
            dst = outs[a].at[idx(block)]
            return pltpu.make_async_remote_copy(
                src_ref=dst if src is None else src, dst_ref=dst, send_sem=send_sems.at[a, k],
                recv_sem=recv_sems.at[a, k], device_id=to, device_id_type=pl.DeviceIdType.MESH)

        mine = [pltpu.make_async_copy(ins[a], outs[a].at[idx(me)], local_sems.at[a]) for a in range(n)]
        for cp in mine:
            cp.start()
        first = []
        for a in range(n):
            first.append(copy(a, 0, me, sibling, src=ins[a]))
            first += [copy(a, 1 + j, me, (*chip, c), src=ins[a]) for j, chip in enumerate(chips)]
        for cp in first:
            cp.start()
        passed = []
        for j, chip in enumerate(chips):
            for a in range(n):
                copy(a, 1 + j, (*chip, c), me).wait_recv()
                cp = copy(a, 4 + j, (*chip, c), sibling)
                cp.start()
                passed.append(cp)
        for a in range(n):
            copy(a, 0, sibling, me).wait_recv()
            for j, chip in enumerate(chips):
                copy(a, 4 + j, (*chip, 1 - c), me).wait_recv()
        for cp in first + passed:
            cp.wait_send()
        for cp in mine:
            cp.wait()

    return _launch(copies, peers, blocks, [jax.ShapeDtypeStruct((N_DEV,) + b.shape, b.dtype) for b in blocks],
                   [pltpu.SemaphoreType.DMA((n, 7)), pltpu.SemaphoreType.DMA((n, 7)), pltpu.SemaphoreType.DMA((n,))],
                   name, collective_id)


def _exchange_core(bufs, name, collective_id=None):
    n = len(bufs)

    def peers():
        x, y, c = _mesh_pos()
        return [(x, y, 1 - c)]

    def copies(ins, outs, send_sems, recv_sems):
        x, y, c = _mesh_pos()
        started = []
        for a in range(n):
            for q in range(4):
                cp = pltpu.make_async_remote_copy(
                    src_ref=ins[a].at[q, 1 - c], dst_ref=outs[a].at[q], send_sem=send_sems.at[a, q],
                    recv_sem=recv_sems.at[a, q], device_id=(x, y, 1 - c), device_id_type=pl.DeviceIdType.MESH)
                cp.start()
                started.append(cp)
        for cp in started:
            cp.wait()

    return _launch(copies, peers, bufs, [jax.ShapeDtypeStruct((4,) + b.shape[2:], b.dtype) for b in bufs],
                   [pltpu.SemaphoreType.DMA((n, 4)), pltpu.SemaphoreType.DMA((n, 4))], name, collective_id)


def _exchange_chips(bufs, name, collective_id=None):
    n = len(bufs)

    def peers():
        x, y, c = _mesh_pos()
        return [(1 - x, y, c), (x, 1 - y, c), (1 - x, 1 - y, c)]

    def copies(ins, outs, send_sems, recv_sems, local_sems):
        x, y, c = _mesh_pos()
        chip = 2 * x + y
        local = [pltpu.make_async_copy(ins[a].at[chip], outs[a].at[chip], local_sems.at[a]) for a in range(n)]
        for cp in local:
            cp.start()
        started = []
        for k in range(1, 4):
            px = 1 - x if k & 2 else x
            py = 1 - y if k & 1 else y
            for a in range(n):
                cp = pltpu.make_async_remote_copy(
                    src_ref=ins[a].at[2 * px + py], dst_ref=outs[a].at[chip], send_sem=send_sems.at[a, k - 1],
                    recv_sem=recv_sems.at[a, k - 1], device_id=(px, py, c), device_id_type=pl.DeviceIdType.MESH)
                cp.start()
                started.append(cp)
        for cp in started:
            cp.wait()
        for cp in local:
            cp.wait()

    return _launch(copies, peers, bufs, [jax.ShapeDtypeStruct(b.shape, b.dtype) for b in bufs],
                   [pltpu.SemaphoreType.DMA((n, 3)), pltpu.SemaphoreType.DMA((n, 3)), pltpu.SemaphoreType.DMA((n,))],
                   name, collective_id)


_SMALL = ("b_ada", "norm_g", "q_norm_g", "k_norm_g", "sinks", "dw_b", "ln_g", "ln_b", "pw2_b", "a_log", "dt_bias",
          "dn_norm_g", "dw_w", "sconv_w")


def _lane4(v):
    return jnp.pad(v, (0, 124)).reshape(1, 128)


def kernel(x, c, w_ada, b_ada, norm_g, w_in, q_norm_g, k_norm_g, sinks, dw_w, dw_b, ln_g, ln_b, pw2_w, pw2_b, sconv_w, a_log, dt_bias, dn_norm_g, w_proj_a, w_proj_b, w_proj_c, w_out, loss_target, m_w_ada, m_b_ada, m_norm_g, m_w_in, m_q_norm_g, m_k_norm_g, m_sinks, m_dw_w, m_dw_b, m_ln_g, m_ln_b, m_pw2_w, m_pw2_b, m_sconv_w, m_a_log, m_dt_bias, m_dn_norm_g, m_w_proj_a, m_w_proj_b, m_w_proj_c, m_w_out, v_w_ada, v_b_ada, v_norm_g, v_w_in, v_q_norm_g, v_k_norm_g, v_sinks, v_dw_w, v_dw_b, v_ln_g, v_ln_b, v_pw2_w, v_pw2_b, v_sconv_w, v_a_log, v_dt_bias, v_dn_norm_g, v_w_proj_a, v_w_proj_b, v_w_proj_c, v_w_out):
    T = x.shape[1]
    nc = T // CHUNK
    xi, yi, ci = _mesh_pos()
    me = 4 * xi + 2 * yi + ci
    big_w = (w_in, pw2_w, w_proj_a, w_proj_b, w_proj_c, w_out)
    big_m = (m_w_in, m_pw2_w, m_w_proj_a, m_w_proj_b, m_w_proj_c, m_w_out)
    big_v = (v_w_in, v_pw2_w, v_w_proj_a, v_w_proj_b, v_w_proj_c, v_w_out)

    ada_cols = w_ada.shape[2]
    dw_cols, sc_cols = dw_w.shape[2], sconv_w.shape[2]
    flat2 = lambda a: a.reshape(-1, a.shape[-1])
    big16 = [[a[l].astype(BF16) for l in range(DEPTH)] for a in big_w]
    gw_in0, c_all, gdw, gsc = _all_gather([big16[0][0], c, dw_w, sconv_w], "gather_first", collective_id=0)
    c_all = c_all.reshape(N_DEV, D)
    dw_f = gdw.transpose(1, 2, 0, 3).reshape(DEPTH, CONV_K, 512)
    sc_f = gsc.transpose(1, 2, 0, 3).reshape(DEPTH, DN_K, QKV_C)

    b_my = lax.dynamic_slice(b_ada, (0, me * ada_cols), (DEPTH, ada_cols))
    mod_part = _ada_fwd(c_all, w_ada, b_my, "ada_fwd")
    (gmod,) = _all_gather([mod_part.reshape(-1, 128)], "gather_mod")

    rest0 = [a[0] for a in big16[1:]]
    all1 = [a[1] for a in big16]
    (rest0, all1), gmod = lax.optimization_barrier(((rest0, all1), gmod))
    got0 = [gw_in0] + _all_gather(rest0, "gather_rest0", collective_id=1)
    got1 = _all_gather(all1, "gather_weights1", collective_id=6)
    wp, pw2_f, wa_f, wb_f, wc_f, wo_f = [], [], [], [], [], []
    for l, (gw_in, gpw2, gpa, gpb, gpc, gwo) in enumerate((got0, got1)):
        wp.append(_assemble_w_in(gw_in, f"assemble_w_in{l}"))
        pw2_f.append(gpw2.reshape(512, 512))
        for dst, g in ((wa_f, gpa), (wb_f, gpb), (wc_f, gpc)):
            dst.append(g.transpose(1, 0, 2).reshape(512, D))
        wo_f.append(gwo.reshape(D, D))
    mod_all = gmod.reshape(N_DEV, DEPTH, N_DEV, ada_cols).transpose(1, 2, 0, 3).reshape(DEPTH, N_DEV, 3 * D)
    mod = lax.dynamic_index_in_dim(mod_all, me, axis=1, keepdims=False)
    shift, scale, gate = mod[:, :D], mod[:, D:2 * D], mod[:, 2 * D:]

    xs, saved = [x[0]], []
    for l in range(DEPTH):
        xl = xs[-1]
        h = _norm_fwd(xl, norm_g[l], scale[l], shift[l], f"norm_fwd{l}")
        proj = _mm(h, wp[l], tm=min(1024, T), tn=1152, tk=D, name=f"in_proj{l}")
        ya = _attn_fwd(proj, q_norm_g[l], k_norm_g[l], sinks[l], f"attn_fwd{l}")
        yb = _conf_fwd(proj, dw_f[l], dw_b[l], ln_g[l], ln_b[l], pw2_f[l], pw2_b[l], f"conf_fwd{l}")
        alv, dtv, dng = _lane4(a_log[l]), _lane4(dt_bias[l]), _row(dn_norm_g[l])
        qd, kd, vd, gb = _gdn_prep_fwd(proj, sc_f[l], alv, dtv, f"gdn_prep_fwd{l}")
        gbt = gb[:, :8].reshape(nc, CHUNK, 8).transpose(0, 2, 1)
        u, w, qe, ke, pm, tmi, eg = _gdn_chunk_fwd(qd, kd, vd, gb, gbt, f"gdn_chunk_fwd{l}")
        yc, o, vn, ss = _gdn_scan_fwd(u, w, qe, ke, pm, eg, proj, dng, f"gdn_scan_fwd{l}")
        xs.append(_merge_fwd(xl, proj, ya, yb, yc, wa_f[l], wb_f[l], wc_f[l], wo_f[l], gate[l], f"merge_fwd{l}"))
        saved.append((h, proj, ya, yb, yc, qd, kd, vd, gb, gbt, ss, alv, dtv, dng, w, qe, ke, pm, tmi, eg, o, vn))

    dout, lsum = _loss_head(xs[-1], loss_target[0], "loss_head")

    small = {name: [None] * DEPTH for name in _SMALL}
    big_parts = [None] * DEPTH
    core = jnp.reshape(ci, (1,)).astype(jnp.int32)
    for l in reversed(range(DEPTH)):
        h, proj, ya, yb, yc, qd, kd, vd, gb, gbt, ss, alv, dtv, dng, w, qe, ke, pm, tmi, eg, o, vn = saved[l]
        dproj, dya, dyb, dyc, g_wa, g_wb, g_wc, g_wo, g_gate = _merge_bwd(
            dout, proj, ya, yb, yc, wa_f[l], wb_f[l], wc_f[l], wo_f[l], gate[l], f"merge_bwd{l}")
        dproj, dkv_a, g_q, g_k, g_s = _attn_bwd(dproj, proj, dya, q_norm_g[l], k_norm_g[l], sinks[l], f"attn_bwd{l}")
        dproj, du1, g_pw2, st_b = _conf_bwd1(dproj, proj, dyb, dw_f[l], dw_b[l], ln_g[l], ln_b[l], pw2_f[l], pw2_b[l],
                                             f"conf_bwd_a{l}")
        dproj, g_dw = _conf_bwd2(dproj, proj, du1, dw_f[l], f"conf_bwd_b{l}")
        dproj, do, dvn, dsn, g_dn = _gdn_scan_bwd(dproj, w, qe, ke, pm, eg, o, proj, dyc, dng, f"gdn_scan_bwd{l}")
        dqd, dkd, dvd, dgb = _gdn_chunk_grad(qd, kd, vd, gb, gbt, tmi, ss, dsn, do, dvn, vn, f"gdn_chunk_bwd{l}")
        dproj, dpre, st_c = _gdn_prep_bwd1(dproj, proj, dqd, dkd, dvd, dgb, dkv_a, sc_f[l], alv, dtv,
                                           f"gdn_prep_bwd_a{l}")
        dproj, g_sc = _gdn_prep_bwd2(dproj, proj, dpre, sc_f[l], f"gdn_prep_bwd_b{l}")
        g_wp = _mm(h, dproj, ta=True, tm=D, tn=1152, tk=min(1024, T), name=f"d_w_in{l}")
        by_dest = [_split_w_in_grad(g_wp, f"split_w_in_grad{l}"), g_pw2.reshape(N_DEV, -1, 512).astype(BF16)]
        by_dest += [g.reshape(512, N_DEV, -1).transpose(1, 0, 2).astype(BF16) for g in (g_wa, g_wb, g_wc)]
        by_dest.append(g_wo.reshape(N_DEV, -1, D).astype(BF16))
        by_dest = [b.reshape(4, 2, -1, b.shape[-1]) for b in by_dest]
        if l < DEPTH - 1:
            by_dest, big_parts[l + 1] = lax.optimization_barrier((by_dest, big_parts[l + 1]))
        from_sibling = _exchange_core(by_dest, f"exchange_grads_core{l}", collective_id=2 + 2 * l)
        chip_sums = [_pair_sum(core, b, r, f"pair_sum{l}_{i}") for i, (b, r) in enumerate(zip(by_dest, from_sibling))]
        big_parts[l] = _exchange_chips(chip_sums, f"exchange_grads_chips{l}", collective_id=3 + 2 * l)
        if l == 0:
            dproj, chip_sums = lax.optimization_barrier((dproj, chip_sums))
        dh = _mm(dproj, wp[l], tb=True, tm=min(1024, T), tn=D, tk=1152, name=f"d_h{l}")
        dout, st_n = _norm_bwd(dh, xs[l], dout, norm_g[l], scale[l], f"norm_bwd{l}")
        if l > 0:
            dout, chip_sums = lax.optimization_barrier((dout, chip_sums))
        for name, g in (("b_ada", jnp.concatenate([st_n[0], st_n[1], g_gate[0]])), ("norm_g", st_n[2]),
                        ("q_norm_g", g_q.reshape(ATT_HEADS, ATT_HD).sum(0)), ("k_norm_g", g_k.reshape(2, ATT_HD).sum(0)),
                        ("sinks", g_s[0]), ("dw_b", st_b[3]),
                        ("ln_g", st_b[1]), ("ln_b", st_b[2]), ("pw2_b", st_b[0]), ("a_log", st_c[0, :4]),
                        ("dt_bias", st_c[1, :4]), ("dn_norm_g", g_dn[0]), ("dw_w", g_dw[:CONV_K]),
                        ("sconv_w", g_sc[:DN_K])):
            small[name][l] = g
    grad_x = dout[None]

    names = list(_SMALL)
    gathered = _all_gather([jnp.stack(small[n]) for n in names] + [lsum], "gather_small_grads")
    gparts = dict(zip(names, gathered))
    loss = 0.5 * jnp.sum(jnp.sum(gathered[-1], axis=(1, 2))) / D
    dmod_my = lax.dynamic_slice(gparts["b_ada"], (0, 0, me * ada_cols), (N_DEV, DEPTH, ada_cols)).transpose(1, 0, 2)
    g_w_ada = _ada_bwd(c_all, dmod_my, "ada_bwd")
    gparts["dw_w"] = lax.dynamic_slice(gparts["dw_w"], (0, 0, 0, me * dw_cols), (N_DEV, DEPTH, CONV_K, dw_cols))
    gparts["sconv_w"] = lax.dynamic_slice(gparts["sconv_w"], (0, 0, 0, me * sc_cols), (N_DEV, DEPTH, DN_K, sc_cols))
    env = dict(b_ada=(b_ada, m_b_ada, v_b_ada), norm_g=(norm_g, m_norm_g, v_norm_g),
               q_norm_g=(q_norm_g, m_q_norm_g, v_q_norm_g), k_norm_g=(k_norm_g, m_k_norm_g, v_k_norm_g),
               sinks=(sinks, m_sinks, v_sinks), dw_b=(dw_b, m_dw_b, v_dw_b), ln_g=(ln_g, m_ln_g, v_ln_g),
               ln_b=(ln_b, m_ln_b, v_ln_b), pw2_b=(pw2_b, m_pw2_b, v_pw2_b), a_log=(a_log, m_a_log, v_a_log),
               dt_bias=(dt_bias, m_dt_bias, v_dt_bias), dn_norm_g=(dn_norm_g, m_dn_norm_g, v_dn_norm_g),
               dw_w=(dw_w, m_dw_w, v_dw_w), sconv_w=(sconv_w, m_sconv_w, v_sconv_w))
    upd = _sum_adamw_many([gparts[n] for n in names], [env[n][0] for n in names], [env[n][1] for n in names],
                          [env[n][2] for n in names], "sum_adamw_small")

    d_ada, nm_ada, nv_ada = (u.reshape(w_ada.shape) for u in
                             _adamw(flat2(w_ada), flat2(g_w_ada), flat2(m_w_ada), flat2(v_w_ada), "adamw_w_ada"))

    big_parts, (dout, upd, d_ada, nm_ada, nv_ada) = lax.optimization_barrier(
        (big_parts, (dout, upd, d_ada, nm_ada, nv_ada)))
    g_small, d_small, m_small, v_small = (dict(zip(names, u)) for u in upd)
    res = [_sum_adamw(p0, p1, flat2(w), flat2(m), flat2(v), f"sum_adamw{i}")
           for i, (p0, p1, w, m, v) in enumerate(zip(big_parts[0], big_parts[1], big_w, big_m, big_v))]
    g_big, d_big, m_big, v_big = ([r[k].reshape(w.shape) for r, w in zip(res, big_w)] for k in range(4))

    order = ("w_ada", "b_ada", "norm_g", "w_in", "q_norm_g", "k_norm_g", "sinks", "dw_w", "dw_b", "ln_g", "ln_b",
             "pw2_w", "pw2_b", "sconv_w", "a_log", "dt_bias", "dn_norm_g", "w_proj_a", "w_proj_b", "w_proj_c", "w_out")
    big_names = ("w_in", "pw2_w", "w_proj_a", "w_proj_b", "w_proj_c", "w_out")

    def pick(kind):
        src_small = (g_small, d_small, m_small, v_small)[kind]
        src_big = (g_big, d_big, m_big, v_big)[kind]
        src_ada = (g_w_ada, d_ada, nm_ada, nv_ada)[kind]
        return [src_ada if n == "w_ada" else src_big[big_names.index(n)] if n in big_names else src_small[n]
                for n in order]

    return (loss, grad_x, *pick(0), *pick(1), *pick(2), *pick(3))
```

```python
import functools
import math

import jax
import jax.numpy as jnp
import numpy as np
from jax import lax
from jax.experimental import pallas as pl
from jax.experimental.pallas import tpu as pltpu
from jax.experimental.pallas import tpu_sc as plsc

F32 = jnp.float32
BF16 = jnp.bfloat16
HI = lax.Precision.HIGHEST

N_DEV = 8
D = 1024
DEPTH = 2
EPS = 1e-6
NEG_INF = -1e30
WINDOW = 128
ATT_HEADS = 8
ATT_HD = 64
CONV_K = 31
DN_HEADS = 4
DN_K = 4
CHUNK = 64
D_IN = 7944
VMEM_LIMIT = 56 * 1024 * 1024

C_MG, C_QA, C_ZA, C_ZB, C_QC, C_KC, C_VC, C_GV, C_GG, C_ZC, C_KA, C_VA, C_AB, NP = (
    0, 3072, 3584, 4096, 4608, 5120, 5632, 6144, 6656, 7168, 7680, 7808, 7936, 8064)
_PAD_FROM = ((4872, 7944), (0, 512), (768, 1280), (2304, 2816), (2816, 4352), (1280, 2304), (4360, 4872),
             (512, 768), (4352, 4360))

ALIBI = tuple(float(2.0 ** (-8.0 * (h + 1) / ATT_HEADS)) for h in range(ATT_HEADS))

ADAM_LR, ADAM_B1, ADAM_B2, ADAM_EPS, ADAM_WD, ADAM_STEP = 0.001, 0.9, 0.999, 1e-08, 0.01, 10


def _cparams(sem=None):
    return pltpu.CompilerParams(dimension_semantics=sem, vmem_limit_bytes=VMEM_LIMIT)


def _sig(x):
    return jax.nn.sigmoid(x)


def _silu(x):
    return x * _sig(x)


def _dsilu(x):
    s = _sig(x)
    return s * (1.0 + x * (1.0 - s))


def _dot(a, b, dims=((1,), (0,)), precision=None):
    return lax.dot_general(a, b, (dims, ((), ())), preferred_element_type=F32, precision=precision)


def _bdot(a, b, dims=((1,), (0,))):
    return _dot(a.astype(BF16), b.astype(BF16), dims)


NN, NT, TN = ((1,), (0,)), ((1,), (1,)), ((0,), (0,))


def _row(v):
    return v.reshape(1, -1)


def _mm(a, b, *, ta=False, tb=False, tm, tn, tk, name):
    M, K = (a.shape[1], a.shape[0]) if ta else a.shape
    N = b.shape[0] if tb else b.shape[1]
    assert M % tm == 0 and N % tn == 0 and K % tk == 0, (M, N, K, tm, tn, tk)
    nk = K // tk
    dims = ((0 if ta else 1,), (1 if tb else 0,))

    def body(a_ref, b_ref, o_ref):
        k = pl.program_id(2)
        part = _bdot(a_ref[...], b_ref[...], dims)

        @pl.when(k == 0)
        def _():
            o_ref[...] = part

        @pl.when(k > 0)
        def _():
            o_ref[...] += part

    a_spec = pl.BlockSpec((tk, tm), lambda i, j, k: (k, i)) if ta else pl.BlockSpec((tm, tk), lambda i, j, k: (i, k))
    b_spec = pl.BlockSpec((tn, tk), lambda i, j, k: (j, k)) if tb else pl.BlockSpec((tk, tn), lambda i, j, k: (k, j))
    return pl.pallas_call(
        body, name=name, grid=(M // tm, N // tn, nk),
        in_specs=[a_spec, b_spec], out_specs=pl.BlockSpec((tm, tn), lambda i, j, k: (i, j)),
        out_shape=jax.ShapeDtypeStruct((M, N), F32),
        compiler_params=_cparams(("parallel", "parallel", "arbitrary")),
    )(a, b)


def _norm_fwd(x, norm_g, scale, shift, name):
    T = x.shape[0]
    tm = min(512, T)

    def body(x_ref, g_ref, sc_ref, sh_ref, h_ref):
        xv = x_ref[...]
        r = lax.rsqrt(jnp.mean(xv * xv, axis=-1, keepdims=True) + EPS)
        h_ref[...] = ((xv * r) * g_ref[...] * (1.0 + sc_ref[...]) + sh_ref[...]).astype(BF16)

    vec = pl.BlockSpec((1, D), lambda i: (0, 0))
    return pl.pallas_call(
        body, name=name, grid=(T // tm,),
        in_specs=[pl.BlockSpec((tm, D), lambda i: (i, 0)), vec, vec, vec],
        out_specs=pl.BlockSpec((tm, D), lambda i: (i, 0)),
        out_shape=jax.ShapeDtypeStruct((T, D), BF16),
        compiler_params=_cparams(("parallel",)),
    )(x, _row(norm_g), _row(scale), _row(shift))


def _norm_bwd(dh, x, dres, norm_g, scale, name):
    T = x.shape[0]
    tm = min(512, T)

    def body(dh_ref, x_ref, dr_ref, g_ref, sc_ref, dx_ref, st_ref):
        i = pl.program_id(0)
        xv, dhv = x_ref[...], dh_ref[...]
        r = lax.rsqrt(jnp.mean(xv * xv, axis=-1, keepdims=True) + EPS)
        xh = xv * r
        g, s1 = g_ref[...], 1.0 + sc_ref[...]
        dxh = dhv * (g * s1)
        dx_ref[...] = dr_ref[...] + r * (dxh - xh * jnp.mean(dxh * xh, axis=-1, keepdims=True))
        dhx = dhv * xh
        upd = jnp.concatenate([jnp.sum(dhv, axis=0, keepdims=True), jnp.sum(dhx * g, axis=0, keepdims=True),
                               jnp.sum(dhx * s1, axis=0, keepdims=True), jnp.zeros((5, D), F32)], axis=0)

        @pl.when(i == 0)
        def _():
            st_ref[...] = upd

        @pl.when(i > 0)
        def _():
            st_ref[...] += upd

    vec = pl.BlockSpec((1, D), lambda i: (0, 0))
    blk = pl.BlockSpec((tm, D), lambda i: (i, 0))
    return pl.pallas_call(
        body, name=name, grid=(T // tm,),
        in_specs=[blk, blk, blk, vec, vec],
        out_specs=[blk, pl.BlockSpec((8, D), lambda i: (0, 0))],
        out_shape=[jax.ShapeDtypeStruct((T, D), F32), jax.ShapeDtypeStruct((8, D), F32)],
        compiler_params=_cparams(("arbitrary",)),
    )(dh, x, dres, _row(norm_g), _row(scale))


def _rms(x, g):
    r = lax.rsqrt(jnp.mean(x * x, axis=-1, keepdims=True) + EPS)
    return x * r, r


def _head_mean_matrix():
    head = np.arange(ATT_HEADS * ATT_HD) // ATT_HD
    return jnp.asarray((head[:, None] == head[None, :]) * (1.0 / ATT_HD), BF16)


def _head_rms(x, hm):
    r = lax.rsqrt(_dot_exact(x * x, hm) + EPS)
    return x * r, r


def _head_rms_bwd(dy, xh, r, g, hm):
    dxh = dy * g
    return r * (dxh - xh * _dot_exact(dxh * xh, hm)), dy * xh


def _attn_mask(n):
    qi = lax.broadcasted_iota(jnp.int32, (WINDOW, 2 * WINDOW), 0)
    kj = lax.broadcasted_iota(jnp.int32, (WINDOW, 2 * WINDOW), 1)
    dist = qi + WINDOW - kj
    valid = (dist >= 0) & (dist < WINDOW) & ((n > 0) | (kj >= WINDOW))
    return valid, dist.astype(F32)


def _attn_probs(s, h, sink, valid, distf):
    s = s - ALIBI[h] * distf
    s = jnp.where(valid, s, NEG_INF)
    m = jnp.maximum(jnp.max(s, axis=-1, keepdims=True), sink)
    p = jnp.exp(s - m)
    es = jnp.exp(sink - m)
    den = jnp.sum(p, axis=-1, keepdims=True) + es
    return p / den, es / den


def _attn_fwd(proj, q_norm_g, k_norm_g, sinks, name):
    T = proj.shape[0]
    nb = T // WINDOW

    def body(sink_ref, q_ref, z_ref, kc_ref, kp_ref, vc_ref, vp_ref, qg_ref, kg_ref, hm_ref, o_ref):
        n = pl.program_id(0)
        valid, distf = _attn_mask(n)
        k2 = jnp.concatenate([kp_ref[...], kc_ref[...]], axis=0)
        v2 = jnp.concatenate([vp_ref[...], vc_ref[...]], axis=0).astype(BF16)
        kn = (_head_rms(k2, hm_ref[:128, :128])[0] * kg_ref[...]).astype(BF16)
        qn = ((_head_rms(q_ref[...], hm_ref[...])[0] * qg_ref[...]) * (ATT_HD ** -0.5)).astype(BF16)

        def head(h):
            sl, gsl = slice(64 * h, 64 * h + 64), slice(64 * (h // 4), 64 * (h // 4) + 64)
            s = _dot(qn[:, sl], kn[:, gsl], NT)
            yield
            p, _ = _attn_probs(s, h, sink_ref[h], valid, distf)
            o_ref[:, sl] = _dot(p.astype(BF16), v2[:, gsl])
            yield

        _lockstep([head(h) for h in range(ATT_HEADS)])
        o_ref[...] = o_ref[...] * _silu(z_ref[...])

    prev = lambda n: jnp.maximum(n - 1, 0)
    return pl.pallas_call(
        body, name=name, grid=(nb,),
        in_specs=[pl.BlockSpec(memory_space=pltpu.SMEM),
                  pl.BlockSpec((WINDOW, 512), lambda n: (n, C_QA // 512)),
                  pl.BlockSpec((WINDOW, 512), lambda n: (n, C_ZA // 512)),
                  pl.BlockSpec((WINDOW, 128), lambda n: (n, C_KA // 128)),
                  pl.BlockSpec((WINDOW, 128), lambda n: (prev(n), C_KA // 128)),
                  pl.BlockSpec((WINDOW, 128), lambda n: (n, C_VA // 128)),
                  pl.BlockSpec((WINDOW, 128), lambda n: (prev(n), C_VA // 128)),
                  pl.BlockSpec((1, 512), lambda n: (0, 0)), pl.BlockSpec((1, 128), lambda n: (0, 0)),
                  pl.BlockSpec((512, 512), lambda n: (0, 0))],
        out_specs=pl.BlockSpec((WINDOW, 512), lambda n: (n, 0)),
        out_shape=jax.ShapeDtypeStruct((T, 512), F32),
        compiler_params=_cparams(("parallel",)),
    )(sinks, proj, proj, proj, proj, proj, proj, _row(jnp.tile(q_norm_g, ATT_HEADS)), _row(jnp.tile(k_norm_g, 2)),
      _head_mean_matrix())


def _rms_bwd(dy, xh, r, g):
    dxh = dy * g
    return r * (dxh - xh * jnp.mean(dxh * xh, axis=-1, keepdims=True)), dy * xh


def _attn_bwd(dproj, proj, dya, q_norm_g, k_norm_g, sinks, name):
    T = proj.shape[0]
    nb = T // WINDOW

    def body(sink_ref, dp_any, q_ref, z_ref, kc_ref, kp_ref, vc_ref, vp_ref, dy_ref, qg_ref, kg_ref, hm_ref,
             dqz_ref, dkv_ref, gq_ref, gk_ref, gs_ref, ck_ref, cv_ref, o_sc, dq_sc):
        n = pl.program_id(0)

        @pl.when(n == 0)
        def _():
            gq_ref[...] = jnp.zeros_like(gq_ref)
            gk_ref[...] = jnp.zeros_like(gk_ref)
            gs_ref[...] = jnp.zeros_like(gs_ref)
            ck_ref[...] = jnp.zeros_like(ck_ref)
            cv_ref[...] = jnp.zeros_like(cv_ref)

        lane8 = lax.broadcasted_iota(jnp.int32, (1, 8), 1)

        @pl.when(n < nb)
        def _():
            valid, distf = _attn_mask(n)
            k2 = jnp.concatenate([kp_ref[...], kc_ref[...]], axis=0)
            v2 = jnp.concatenate([vp_ref[...], vc_ref[...]], axis=0).astype(BF16)
            kn = (_head_rms(k2, hm_ref[:128, :128])[0] * kg_ref[...]).astype(BF16)
            qh, qr = _head_rms(q_ref[...], hm_ref[...])
            qn = ((qh * qg_ref[...]) * (ATT_HD ** -0.5)).astype(BF16)
            zs = z_ref[...]
            do_all = dy_ref[...] * _silu(zs)
            dob_all = do_all.astype(BF16)

            def head(h):
                sl, gsl = slice(64 * h, 64 * h + 64), slice(64 * (h // 4), 64 * (h // 4) + 64)
                s = _dot(qn[:, sl], kn[:, gsl], NT)
                dpm = _dot(dob_all[:, sl], v2[:, gsl], NT)
                yield
                p, ps = _attn_probs(s, h, sink_ref[h], valid, distf)
                pb = p.astype(BF16)
                o_sc[:, sl] = _dot(pb, v2[:, gsl])
                dvg = _dot(pb, dob_all[:, sl], TN)
                delta = jnp.sum(p * dpm, axis=-1, keepdims=True)
                ds = (p * (dpm - delta)).astype(BF16)
                gs = jnp.where(lane8 == h, -jnp.sum(ps * delta, axis=0, keepdims=True), 0.0)
                yield
                dkn = _dot(ds, qn[:, sl], TN)
                dq_sc[:, sl] = _dot(ds, kn[:, gsl])
                yield
                return dkn, dvg, gs

            res = _lockstep([head(h) for h in range(ATT_HEADS)])
            dqz_ref[:, 512:] = (dy_ref[...] * o_sc[...] * _dsilu(zs)).astype(BF16)
            dq, gq = _head_rms_bwd(dq_sc[...] * (ATT_HD ** -0.5), qh, qr, qg_ref[...], hm_ref[...])
            dqz_ref[:, :512] = dq.astype(BF16)
            gq_acc = jnp.sum(gq, axis=0, keepdims=True)
            gs_acc = sum(r[2] for r in res[1:]) + res[0][2]
            for g in range(2):
                dkn = (res[4 * g][0] + res[4 * g + 1][0]) + (res[4 * g + 2][0] + res[4 * g + 3][0])
                dvg = (res[4 * g][1] + res[4 * g + 1][1]) + (res[4 * g + 2][1] + res[4 * g + 3][1])
                ksl = slice(64 * g, 64 * g + 64)
                vsl = slice(128 + 64 * g, 128 + 64 * g + 64)
                dkv_ref[:, ksl] = ck_ref[:, ksl] + dkn[:WINDOW]
                dkv_ref[:, vsl] = cv_ref[:, ksl] + dvg[:WINDOW]
                ck_ref[:, ksl] = dkn[WINDOW:]
                cv_ref[:, ksl] = dvg[WINDOW:]
            gq_ref[...] += gq_acc
            gs_ref[...] += gs_acc

        @pl.when(n == nb)
        def _():
            dkv_ref[:, :128] = ck_ref[...]
            dkv_ref[:, 128:] = cv_ref[...]

        @pl.when(n > 0)
        def _():
            hm = hm_ref[:128, :128]
            kh, kr = _head_rms(kp_ref[...], hm)
            dk, gk = _head_rms_bwd(dkv_ref[:, :128], kh, kr, kg_ref[...], hm)
            dkv_ref[:, :128] = dk
            gk_ref[...] += jnp.sum(gk, axis=0, keepdims=True)

    cur = lambda n: jnp.minimum(n, nb - 1)
    prev = lambda n: jnp.maximum(n - 1, 0)
    small = lambda w: pl.BlockSpec((1, w), lambda n: (0, 0))
    return pl.pallas_call(
        body, name=name, grid=(nb + 1,),
        in_specs=[pl.BlockSpec(memory_space=pltpu.SMEM), pl.BlockSpec(memory_space=pl.ANY),
                  pl.BlockSpec((WINDOW, 512), lambda n: (cur(n), C_QA // 512)),
                  pl.BlockSpec((WINDOW, 512), lambda n: (cur(n), C_ZA // 512)),
                  pl.BlockSpec((WINDOW, 128), lambda n: (cur(n), C_KA // 128)),
                  pl.BlockSpec((WINDOW, 128), lambda n: (prev(n), C_KA // 128)),
                  pl.BlockSpec((WINDOW, 128), lambda n: (cur(n), C_VA // 128)),
                  pl.BlockSpec((WINDOW, 128), lambda n: (prev(n), C_VA // 128)),
                  pl.BlockSpec((WINDOW, 512), lambda n: (cur(n), 0)),
                  small(512), small(128), pl.BlockSpec((512, 512), lambda n: (0, 0))],
        out_specs=[pl.BlockSpec((WINDOW, 1024), lambda n: (cur(n), C_QA // 1024)),
                   pl.BlockSpec((WINDOW, 256), lambda n: (prev(n), 0)),
                   small(512), small(128), small(8)],
        out_shape=[jax.ShapeDtypeStruct(dproj.shape, BF16), jax.ShapeDtypeStruct((T, 256), F32),
                   jax.ShapeDtypeStruct((1, 512), F32), jax.ShapeDtypeStruct((1, 128), F32),
                   jax.ShapeDtypeStruct((1, 8), F32)],
        scratch_shapes=[pltpu.VMEM((WINDOW, 128), F32), pltpu.VMEM((WINDOW, 128), F32),
                        pltpu.VMEM((WINDOW, 512), F32), pltpu.VMEM((WINDOW, 512), F32)],
        input_output_aliases={1: 0},
        compiler_params=_cparams(("arbitrary",)),
    )(sinks, dproj, proj, proj, proj, proj, proj, proj, dya, _row(jnp.tile(q_norm_g, ATT_HEADS)),
      _row(jnp.tile(k_norm_g, 2)), _head_mean_matrix())


HALO_B = 32


def _conf_specs(T, tm):
    r = tm // HALO_B
    cur = lambda c: pl.BlockSpec((tm, 512), lambda i: (i, c // 512))
    prev = lambda c: pl.BlockSpec((HALO_B, 512), lambda i: (jnp.maximum(i * r - 1, 0), c // 512))
    return cur, prev


SUB = 8
ROW_CHUNK = 64


def _shifted_copies(ext_ref, sh_ref):
    total = ext_ref.shape[0]
    for r in range(SUB):
        rows = total if r == 0 else total - SUB
        sh_ref[r, :rows, :] = ext_ref[pl.ds(r, rows), :]


def _taps_by_shift(offsets):
    groups = {}
    for k, o in enumerate(offsets):
        q, r = divmod(o, SUB)
        groups.setdefault(r, []).append((k, q))
    return groups


def _conv_taps(sh_ref, w_ref, offsets, out_ref, init):
    groups = _taps_by_shift(offsets)

    def chunk(ci, carry):
        r0 = pl.multiple_of(ci * ROW_CHUNK, ROW_CHUNK)
        acc = jnp.zeros((ROW_CHUNK, out_ref.shape[1]), F32) + init
        for r, taps in groups.items():
            win = sh_ref[r, pl.ds(r0, ROW_CHUNK + SUB * max(q for _, q in taps)), :]
            for k, q in taps:
                acc = acc + w_ref[k:k + 1, :] * win[SUB * q:SUB * q + ROW_CHUNK]
        out_ref[pl.ds(r0, ROW_CHUNK), :] = acc
        return carry

    lax.fori_loop(0, out_ref.shape[0] // ROW_CHUNK, chunk, 0)


def _conv_weight_grad(sh_ref, d_ref, offsets):
    tm, width = d_ref.shape
    out = [None] * len(offsets)
    for r, taps in _taps_by_shift(offsets).items():
        def chunk(ci, accs, r=r, taps=taps):
            r0 = pl.multiple_of(ci * ROW_CHUNK, ROW_CHUNK)
            d = d_ref[pl.ds(r0, ROW_CHUNK), :]
            win = sh_ref[r, pl.ds(r0, ROW_CHUNK + SUB * max(q for _, q in taps)), :]
            return tuple(a + jnp.sum((d * win[SUB * q:SUB * q + ROW_CHUNK]).reshape(ROW_CHUNK // SUB, SUB, width),
                                     axis=0) for a, (_, q) in zip(accs, taps))

        accs = lax.fori_loop(0, tm // ROW_CHUNK, chunk, tuple(jnp.zeros((SUB, width), F32) for _ in taps))
        for a, (k, _) in zip(accs, taps):
            out[k] = jnp.sum(a, axis=0, keepdims=True)
    return out


def _conf_scratch(tm):
    return [pltpu.VMEM((tm + HALO_B, 512), F32), pltpu.VMEM((SUB, tm + HALO_B, 512), F32), pltpu.VMEM((tm, 512), F32)]


def _conf_core(i, tm, gv_ref, gg_ref, gvp_ref, ggp_ref, w_ref, b_ref, lg_ref, lb_ref, pw_ref, pb_ref, ext_ref, sh_ref,
               cv_ref):
    up = gvp_ref[...] * _sig(ggp_ref[...])
    ext_ref[:HALO_B] = jnp.where(i > 0, up, 0.0)
    ext_ref[HALO_B:] = gv_ref[...] * _sig(gg_ref[...])
    _shifted_copies(ext_ref, sh_ref)
    _conv_taps(sh_ref, w_ref, [HALO_B - CONV_K + 1 + k for k in range(CONV_K)], cv_ref, b_ref[...])
    acc = cv_ref[...]
    mu = jnp.mean(acc, axis=-1, keepdims=True)
    xc = acc - mu
    rstd = lax.rsqrt(jnp.mean(xc * xc, axis=-1, keepdims=True) + EPS)
    xh = xc * rstd
    u2 = xh * lg_ref[...] + lb_ref[...]
    u3 = _silu(u2)
    ypre = _bdot(u3, pw_ref[...]) + pb_ref[...]
    return xh, rstd, u2, u3, ypre


def _conf_fwd(proj, dw_w, dw_b, ln_g, ln_b, pw2, pw2_b, name):
    T = proj.shape[0]
    tm = min(512, T)
    cur, prev = _conf_specs(T, tm)

    def body(gv_ref, gg_ref, gvp_ref, ggp_ref, zb_ref, w_ref, b_ref, lg_ref, lb_ref, pw_ref, pb_ref, o_ref, *scratch):
        i = pl.program_id(0)
        ypre = _conf_core(i, tm, gv_ref, gg_ref, gvp_ref, ggp_ref, w_ref, b_ref, lg_ref, lb_ref, pw_ref, pb_ref,
                          *scratch)[4]
        o_ref[...] = ypre * _silu(zb_ref[...])

    full = lambda s: pl.BlockSpec(s, lambda i: (0, 0))
    return pl.pallas_call(
        body, name=name, grid=(T // tm,),
        in_specs=[cur(C_GV), cur(C_GG), prev(C_GV), prev(C_GG), cur(C_ZB), full((CONV_K, 512)), full((1, 512)),
                  full((1, 512)), full((1, 512)), full((512, 512)), full((1, 512))],
        out_specs=pl.BlockSpec((tm, 512), lambda i: (i, 0)),
        out_shape=jax.ShapeDtypeStruct((T, 512), F32),
        scratch_shapes=_conf_scratch(tm),
        compiler_params=_cparams(("parallel",)),
    )(proj, proj, proj, proj, proj, dw_w, _row(dw_b), _row(ln_g), _row(ln_b), pw2, _row(pw2_b))


def _conf_bwd1(dproj, proj, dyb, dw_w, dw_b, ln_g, ln_b, pw2, pw2_b, name):
    T = proj.shape[0]
    tm = min(512, T)
    cur, prev = _conf_specs(T, tm)

    def body(dp_any, gv_ref, gg_ref, gvp_ref, ggp_ref, zb_ref, dy_ref, w_ref, b_ref, lg_ref, lb_ref, pw_ref, pb_ref,
             dzb_ref, du1_ref, gpw_ref, st_ref, *scratch):
        i = pl.program_id(0)
        xh, rstd, u2, u3, ypre = _conf_core(i, tm, gv_ref, gg_ref, gvp_ref, ggp_ref, w_ref, b_ref, lg_ref, lb_ref,
                                            pw_ref, pb_ref, *scratch)
        zb, dy = zb_ref[...], dy_ref[...]
        dzb_ref[...] = (dy * ypre * _dsilu(zb)).astype(BF16)
        dyp = dy * _silu(zb)
        du2 = _bdot(dyp, pw_ref[...], NT) * _dsilu(u2)
        dxh = du2 * lg_ref[...]
        du1 = rstd * (dxh - jnp.mean(dxh, axis=-1, keepdims=True) - xh * jnp.mean(dxh * xh, axis=-1, keepdims=True))
        du1_ref[...] = du1
        gpw = _bdot(u3, dyp, TN)
        rs = lambda a: jnp.sum(a, axis=0, keepdims=True)
        upd = jnp.concatenate([rs(dyp), rs(du2 * xh), rs(du2), rs(du1), jnp.zeros((4, 512), F32)], axis=0)

        @pl.when(i == 0)
        def _():
            gpw_ref[...] = gpw
            st_ref[...] = upd

        @pl.when(i > 0)
        def _():
            gpw_ref[...] += gpw
            st_ref[...] += upd

    full = lambda s: pl.BlockSpec(s, lambda i: (0, 0))
    blk = pl.BlockSpec((tm, 512), lambda i: (i, 0))
    return pl.pallas_call(
        body, name=name, grid=(T // tm,),
        in_specs=[pl.BlockSpec(memory_space=pl.ANY), cur(C_GV), cur(C_GG), prev(C_GV), prev(C_GG), cur(C_ZB), blk,
                  full((CONV_K, 512)), full((1, 512)), full((1, 512)), full((1, 512)), full((512, 512)), full((1, 512))],
        out_specs=[cur(C_ZB), blk, full((512, 512)), full((8, 512))],
        out_shape=[jax.ShapeDtypeStruct(dproj.shape, BF16), jax.ShapeDtypeStruct((T, 512), F32),
                   jax.ShapeDtypeStruct((512, 512), F32), jax.ShapeDtypeStruct((8, 512), F32)],
        scratch_shapes=_conf_scratch(tm),
        input_output_aliases={0: 0},
        compiler_params=_cparams(("arbitrary",)),
    )(dproj, proj, proj, proj, proj, proj, dyb, dw_w, _row(dw_b), _row(ln_g), _row(ln_b), pw2, _row(pw2_b))


def _conf_bwd2(dproj, proj, du1, dw_w, name):
    T = proj.shape[0]
    tm = min(512, T)
    nt = T // tm
    r = tm // HALO_B
    cur, prev = _conf_specs(T, tm)

    def body(dp_any, gv_ref, gg_ref, gvp_ref, ggp_ref, du_ref, dun_ref, w_ref, dglu_ref, gw_ref, ext_ref, sh_ref,
             cv_ref):
        i = pl.program_id(0)
        gv, sg = gv_ref[...], _sig(gg_ref[...])
        ext_ref[:HALO_B] = jnp.where(i > 0, gvp_ref[...] * _sig(ggp_ref[...]), 0.0)
        ext_ref[HALO_B:] = gv * sg
        _shifted_copies(ext_ref, sh_ref)
        rows = _conv_weight_grad(sh_ref, du_ref, [HALO_B - CONV_K + 1 + k for k in range(CONV_K)])
        rows.append(jnp.zeros((1, 512), F32))
        gw = jnp.concatenate(rows, axis=0)
        ext_ref[:tm] = du_ref[...]
        ext_ref[tm:] = jnp.where(i < nt - 1, dun_ref[...], 0.0)
        _shifted_copies(ext_ref, sh_ref)
        _conv_taps(sh_ref, w_ref, [CONV_K - 1 - k for k in range(CONV_K)], cv_ref, 0.0)
        du0 = cv_ref[...]
        dglu_ref[:, :512] = (du0 * sg).astype(BF16)
        dglu_ref[:, 512:] = (du0 * gv * sg * (1.0 - sg)).astype(BF16)

        @pl.when(i == 0)
        def _():
            gw_ref[...] = gw

        @pl.when(i > 0)
        def _():
            gw_ref[...] += gw

    full = lambda s: pl.BlockSpec(s, lambda i: (0, 0))
    return pl.pallas_call(
        body, name=name, grid=(nt,),
        in_specs=[pl.BlockSpec(memory_space=pl.ANY), cur(C_GV), cur(C_GG), prev(C_GV), prev(C_GG),
                  pl.BlockSpec((tm, 512), lambda i: (i, 0)),
                  pl.BlockSpec((HALO_B, 512), lambda i: (jnp.minimum((i + 1) * r, T // HALO_B - 1), 0)),
                  full((CONV_K, 512))],
        out_specs=[pl.BlockSpec((tm, 1024), lambda i: (i, C_GV // 1024)), full((32, 512))],
        out_shape=[jax.ShapeDtypeStruct(dproj.shape, BF16), jax.ShapeDtypeStruct((32, 512), F32)],
        scratch_shapes=_conf_scratch(tm),
        input_output_aliases={0: 0},
        compiler_params=_cparams(("arbitrary",)),
    )(dproj, proj, proj, proj, proj, du1, du1, dw_w)


HALO_C = 8
QKV_C = 1536


def _softplus(x):
    return jnp.maximum(x, 0.0) + jnp.log1p(jnp.exp(-jnp.abs(x)))


def _gdn_conv(i, tm, x_ref, xp_ref, w_ref, ext_ref):
    ext_ref[:HALO_C] = jnp.where(i > 0, xp_ref[...], 0.0)
    ext_ref[HALO_C:] = x_ref[...]
    pre = jnp.zeros((tm, QKV_C), F32)
    for k in range(DN_K):
        pre = pre + w_ref[k:k + 1, :] * ext_ref[pl.ds(HALO_C - DN_K + 1 + k, tm), :]
    return pre


def _gdn_specs(T, tm):
    r = tm // HALO_C
    cur = pl.BlockSpec((tm, QKV_C), lambda i: (i, C_QC // QKV_C))
    prev = pl.BlockSpec((HALO_C, QKV_C), lambda i: (jnp.maximum(i * r - 1, 0), C_QC // QKV_C))
    ab = pl.BlockSpec((tm, 128), lambda i: (i, C_AB // 128))
    return cur, prev, ab


def _gdn_prep_fwd(proj, sconv_w, alog_v, dtb_v, name):
    T = proj.shape[0]
    tm = min(512, T)
    cur, prev, ab = _gdn_specs(T, tm)

    def body(x_ref, xp_ref, ab_ref, w_ref, al_ref, dt_ref, q_ref, k_ref, v_ref, gb_ref, ext_ref):
        i = pl.program_id(0)
        y = _silu(_gdn_conv(i, tm, x_ref, xp_ref, w_ref, ext_ref))
        for h in range(DN_HEADS):
            sl = slice(128 * h, 128 * h + 128)
            qh, kh = y[:, sl], y[:, 512 + 128 * h:512 + 128 * h + 128]
            q_ref[:, sl] = qh * lax.rsqrt(jnp.sum(qh * qh, axis=-1, keepdims=True) + EPS) * (128 ** -0.5)
            k_ref[:, sl] = kh * lax.rsqrt(jnp.sum(kh * kh, axis=-1, keepdims=True) + EPS)
        v_ref[...] = y[:, 1024:]
        abv = ab_ref[...]
        lane = lax.broadcasted_iota(jnp.int32, (tm, 128), 1)
        g = -jnp.exp(al_ref[...]) * _softplus(abv + dt_ref[...])
        gb_ref[...] = jnp.where(lane < DN_HEADS, g, _sig(abv))

    full = lambda s: pl.BlockSpec(s, lambda i: (0, 0))
    blk = pl.BlockSpec((tm, 512), lambda i: (i, 0))
    return pl.pallas_call(
        body, name=name, grid=(T // tm,),
        in_specs=[cur, prev, ab, full((DN_K, QKV_C)), full((1, 128)), full((1, 128))],
        out_specs=[blk, blk, blk, pl.BlockSpec((tm, 128), lambda i: (i, 0))],
        out_shape=[jax.ShapeDtypeStruct((T, 512), F32)] * 3 + [jax.ShapeDtypeStruct((T, 128), F32)],
        scratch_shapes=[pltpu.VMEM((tm + HALO_C, QKV_C), F32)],
        compiler_params=_cparams(("parallel",)),
    )(proj, proj, proj, sconv_w, alog_v, dtb_v)


def _hdot(a, b, dims=NN):
    return _dot(a, b, dims, precision=HI)


def _lockstep(gens):
    results, live = [None] * len(gens), list(range(len(gens)))
    while live:
        for i in list(live):
            try:
                next(gens[i])
            except StopIteration as stop:
                results[i] = stop.value
                live.remove(i)
    return results


def _split(a):
    hi = a.astype(BF16)
    return hi, (a - hi.astype(F32)).astype(BF16)


def _dot_exact(a, b, dims=NN, split_left=True):
    x = (a if split_left else b).astype(F32)
    hi = x.astype(BF16)
    r = x - hi.astype(F32)
    mid = r.astype(BF16)
    lo = (r - mid.astype(F32)).astype(BF16)
    other = (b if split_left else a).astype(BF16)
    one = (lambda p: _dot(p, other, dims)) if split_left else (lambda p: _dot(other, p, dims))
    return (one(lo) + one(mid)) + one(hi)


def _dot3(a, b):
    (ah, al), (bh, bl) = a, b
    return _dot(ah, bh) + (_dot(ah, bl) + _dot(al, bh))


def _tri_inv(mats, eye):
    ps = [-a for a in mats]
    ts = [eye + p for p in ps]
    for _ in range(5):
        sp = [_split(p) for p in ps]
        ps = [_dot3(s, s) for s in sp]
        sp = [_split(p) for p in ps]
        ts = [t + _dot3(_split(t), s) for t, s in zip(ts, sp)]
    return ts


def _tri_consts():
    ii = lax.broadcasted_iota(jnp.int32, (CHUNK, CHUNK), 0)
    jj = lax.broadcasted_iota(jnp.int32, (CHUNK, CHUNK), 1)
    return ii >= jj, ii > jj, (ii == jj).astype(F32)


def _gdn_local(q, k, v, gcol, grow, bcol, lower, strict):
    dm = jnp.where(lower, jnp.exp(jnp.where(lower, gcol - grow, 0.0)), 0.0)
    kb = k * bcol
    a = jnp.where(strict, _bdot(kb, k, NT) * dm, 0.0)
    gc = jnp.exp(gcol)
    glast = grow[:, CHUNK - 1:CHUNK]
    return dict(q=q, k=k, v=v, bcol=bcol, gcol=gcol, glast=glast, dm=dm, kb=kb, a=a, gc=gc, vb=v * bcol,
                kbg=kb * gc, p=_bdot(q, k, NT) * dm, qe=q * gc, ke=k * jnp.exp(glast - gcol))


def _gdn_chunk_bwd(c, do, dvn, ds_new, lower, strict, ones):
    rs = lambda m: jnp.sum(m, axis=-1, keepdims=True)
    colsum = lambda m: _dot_exact(m, ones, TN)[:, :1]
    q, k, v, bcol, dm, tm, gc, s = c["q"], c["k"], c["v"], c["bcol"], c["dm"], c["tm"], c["gc"], c["s"]
    eg = jnp.exp(c["glast"])
    dqe = _bdot(do, s, NT)
    dp = jnp.where(lower, _bdot(do, c["vn"], NT), 0.0)
    dw = -_bdot(dvn, s, NT)
    dke = _bdot(c["vn"], ds_new, NT)
    dvb = _bdot(tm, dvn, TN)
    yield
    dglast = jnp.sum(rs(ds_new * s), axis=0, keepdims=True) * eg
    dk = dke * jnp.exp(c["glast"] - c["gcol"])
    r_ke = rs(dke * c["ke"])
    dglast = dglast + jnp.sum(r_ke, axis=0, keepdims=True)
    dgam = rs(dqe * c["qe"]) - r_ke
    dq = dqe * gc
    dpm = dp * dm
    mp = dp * c["p"]
    dq = dq + _bdot(dpm, k)
    dk = dk + _bdot(dpm, q, TN)
    dt = _bdot(dvn, c["vb"], NT) + _bdot(dw, c["kbg"], NT)
    dkbg = _bdot(tm, dw, TN)
    dgam = dgam + rs(mp) - colsum(mp)
    yield
    dkb = dkbg * gc
    dgam = dgam + rs(dkbg * c["kbg"])
    dat = _bdot(tm, dt, TN)
    yield
    da = jnp.where(strict, -_bdot(dat, tm, NT), 0.0)
    yield
    dam = da * dm
    ma = da * c["a"]
    dkb = dkb + _bdot(dam, k)
    dk = dk + _bdot(dam, c["kb"], TN)
    dgam = dgam + rs(ma) - colsum(ma)
    yield
    dk = dk + dkb * bcol
    dbeta = rs(dkb * k) + rs(dvb * v)
    dv = dvb * bcol
    row = lax.broadcasted_iota(jnp.int32, (CHUNK, 1), 0)
    dgam = dgam + jnp.where(row == CHUNK - 1, dglast, 0.0)
    dg = _dot_exact(lower, dgam, TN, split_left=False)
    return dq, dk, dv, dg, dbeta


SCAN_GROUP = 4
GROUP = 4


def _chunk_decay(gb_ref, gt_ref, lmat, g):
    rows = slice(CHUNK * g, CHUNK * g + CHUNK)
    return rows, _dot_exact(lmat, gb_ref[rows, :], split_left=False), _dot_exact(gt_ref[g], lmat, NT)


def _gdn_chunk_fwd(qd, kd, vd, gb, gbt, name):
    T = qd.shape[0]
    G = GROUP
    ng = T // (CHUNK * G)

    def body(q_ref, k_ref, v_ref, gb_ref, gt_ref, u_ref, w_ref, qe_ref, ke_ref, p_ref, t_ref, eg_ref):
        lower, strict, eye = _tri_consts()
        lmat = lower.astype(F32)
        decay = [_chunk_decay(gb_ref, gt_ref, lmat, g) for g in range(G)]
        chains = [(g, h) for g in range(G) for h in range(DN_HEADS)]
        cs = []
        for g, h in chains:
            rows, gcs, grs = decay[g]
            sl = slice(128 * h, 128 * h + 128)
            c = _gdn_local(q_ref[rows, sl], k_ref[rows, sl], v_ref[rows, sl], gcs[:, h:h + 1], grs[h:h + 1, :],
                           gb_ref[rows, DN_HEADS + h:DN_HEADS + h + 1], lower, strict)
            qe_ref[rows, sl] = c["qe"].astype(BF16)
            ke_ref[rows, sl] = c["ke"].astype(BF16)
            p_ref[rows, 64 * h:64 * h + 64] = c["p"].astype(BF16)
            eg_ref[g, h:h + 1, :] = jnp.broadcast_to(jnp.exp(c["glast"]), (1, 128))
            cs.append(c)
        tms = [t.astype(BF16) for t in _tri_inv([c["a"] for c in cs], eye)]
        us = [_dot(t, c["vb"].astype(BF16)) for t, c in zip(tms, cs)]
        ws = [_dot(t, c["kbg"].astype(BF16)) for t, c in zip(tms, cs)]
        for (g, h), tm, u, w in zip(chains, tms, us, ws):
            rows, sl = decay[g][0], slice(128 * h, 128 * h + 128)
            u_ref[rows, sl] = u
            w_ref[rows, sl] = w.astype(BF16)
            t_ref[rows, 64 * h:64 * h + 64] = tm
        for g in range(G):
            eg_ref[g, DN_HEADS:, :] = jnp.zeros((8 - DN_HEADS, 128), F32)

    blk = pl.BlockSpec((CHUNK * G, 512), lambda n: (n, 0))
    half = pl.BlockSpec((CHUNK * G, 256), lambda n: (n, 0))
    return pl.pallas_call(
        body, name=name, grid=(ng,),
        in_specs=[blk, blk, blk, pl.BlockSpec((CHUNK * G, 128), lambda n: (n, 0)),
                  pl.BlockSpec((G, 8, CHUNK), lambda n: (n, 0, 0))],
        out_specs=[blk, blk, blk, blk, half, half, pl.BlockSpec((G, 8, 128), lambda n: (n, 0, 0))],
        out_shape=[jax.ShapeDtypeStruct((T, 512), F32)] + [jax.ShapeDtypeStruct((T, 512), BF16)] * 3
        + [jax.ShapeDtypeStruct((T, 256), BF16)] * 2 + [jax.ShapeDtypeStruct((T // CHUNK, 8, 128), F32)],
        compiler_params=_cparams(("parallel",)),
    )(qd, kd, vd, gb, gbt)


def _gdn_scan_fwd(u, w, qe, ke, pm, eg, proj, dn_g, name):
    T = u.shape[0]
    nc = T // CHUNK
    G = SCAN_GROUP

    def body(u_ref, w_ref, qe_ref, ke_ref, p_ref, eg_ref, z_ref, ng_ref, y_ref, o_ref, vn_ref, ss_ref, s_ref):
        n = pl.program_id(0)

        @pl.when(n == 0)
        def _():
            s_ref[...] = jnp.zeros_like(s_ref)

        def head(j, h):
            rows, sl = slice(CHUNK * j, CHUNK * j + CHUNK), slice(128 * h, 128 * h + 128)
            s = s_ref[h]
            sb = s.astype(BF16)
            vn = u_ref[rows, sl] - _dot(w_ref[rows, sl], sb)
            qs = _dot(qe_ref[rows, sl], sb)
            yield
            vb = vn.astype(BF16)
            o = qs + _dot(p_ref[rows, 64 * h:64 * h + 64], vb)
            s_ref[h] = s * eg_ref[j, h:h + 1, :] + _dot(ke_ref[rows, sl], vb, TN)
            yield
            vn_ref[rows, sl] = vb
            o_ref[rows, sl] = o
            y_ref[rows, sl] = _rms(o, None)[0] * ng_ref[...] * _silu(z_ref[rows, sl])

        for j in range(G):
            ss_ref[j] = s_ref[...]
            _lockstep([head(j, h) for h in range(DN_HEADS)])

    blk = pl.BlockSpec((CHUNK * G, 512), lambda n: (n, 0))
    return pl.pallas_call(
        body, name=name, grid=(nc // G,),
        in_specs=[blk, blk, blk, blk, pl.BlockSpec((CHUNK * G, 256), lambda n: (n, 0)),
                  pl.BlockSpec((G, 8, 128), lambda n: (n, 0, 0)),
                  pl.BlockSpec((CHUNK * G, 512), lambda n: (n, C_ZC // 512)), pl.BlockSpec((1, 128), lambda n: (0, 0))],
        out_specs=[blk, blk, blk, pl.BlockSpec((G, DN_HEADS, 128, 128), lambda n: (n, 0, 0, 0))],
        out_shape=[jax.ShapeDtypeStruct((T, 512), F32), jax.ShapeDtypeStruct((T, 512), F32),
                   jax.ShapeDtypeStruct((T, 512), BF16), jax.ShapeDtypeStruct((nc, DN_HEADS, 128, 128), F32)],
        scratch_shapes=[pltpu.VMEM((DN_HEADS, 128, 128), F32)],
        compiler_params=_cparams(("arbitrary",)),
    )(u, w, qe, ke, pm, eg, proj, dn_g)


def _gdn_scan_bwd(dproj, w, qe, ke, pm, eg, o, proj, dyc, dn_g, name):
    T = o.shape[0]
    nc = T // CHUNK
    G = SCAN_GROUP
    rev = lambda n: nc // G - 1 - n

    def body(dp_any, w_ref, qe_ref, ke_ref, p_ref, eg_ref, o_ref, z_ref, dy_ref, ng_ref,
             dz_ref, do_ref, dvn_ref, dsn_ref, gng_ref, ds_ref):
        n = pl.program_id(0)

        @pl.when(n == 0)
        def _():
            ds_ref[...] = jnp.zeros_like(ds_ref)
            gng_ref[...] = jnp.zeros_like(gng_ref)

        def head(j, h):
            rows, sl = slice(CHUNK * j, CHUNK * j + CHUNK), slice(128 * h, 128 * h + 128)
            oh, r = _rms(o_ref[rows, sl], None)
            z, dy = z_ref[rows, sl], dy_ref[rows, sl]
            dz_ref[rows, sl] = (dy * (oh * ng_ref[...]) * _dsilu(z)).astype(BF16)
            do, gg = _rms_bwd(dy * _silu(z), oh, r, ng_ref[...])
            dob = do.astype(BF16)
            ds = ds_ref[h]
            dvn = _dot(p_ref[rows, 64 * h:64 * h + 64], dob, TN) + _dot(ke_ref[rows, sl], ds.astype(BF16))
            qd = _dot(qe_ref[rows, sl], dob, TN)
            yield
            dvb = dvn.astype(BF16)
            ds_ref[h] = qd + eg_ref[j, h:h + 1, :] * ds - _dot(w_ref[rows, sl], dvb, TN)
            do_ref[rows, sl] = dob
            dvn_ref[rows, sl] = dvb
            return jnp.sum(gg, axis=0, keepdims=True)

        for j in reversed(range(G)):
            dsn_ref[j] = ds_ref[...]
            gng = _lockstep([head(j, h) for h in range(DN_HEADS)])
            gng_ref[...] += (gng[0] + gng[1]) + (gng[2] + gng[3])

    blk = pl.BlockSpec((CHUNK * G, 512), lambda n: (rev(n), 0))
    state = pl.BlockSpec((G, DN_HEADS, 128, 128), lambda n: (rev(n), 0, 0, 0))
    return pl.pallas_call(
        body, name=name, grid=(nc // G,),
        in_specs=[pl.BlockSpec(memory_space=pl.ANY), blk, blk, blk,
                  pl.BlockSpec((CHUNK * G, 256), lambda n: (rev(n), 0)),
                  pl.BlockSpec((G, 8, 128), lambda n: (rev(n), 0, 0)), blk,
                  pl.BlockSpec((CHUNK * G, 512), lambda n: (rev(n), C_ZC // 512)), blk,
                  pl.BlockSpec((1, 128), lambda n: (0, 0))],
        out_specs=[pl.BlockSpec((CHUNK * G, 512), lambda n: (rev(n), C_ZC // 512)), blk, blk, state,
                   pl.BlockSpec((1, 128), lambda n: (0, 0))],
        out_shape=[jax.ShapeDtypeStruct(dproj.shape, BF16), jax.ShapeDtypeStruct((T, 512), BF16),
                   jax.ShapeDtypeStruct((T, 512), BF16), jax.ShapeDtypeStruct((nc, DN_HEADS, 128, 128), F32),
                   jax.ShapeDtypeStruct((1, 128), F32)],
        scratch_shapes=[pltpu.VMEM((DN_HEADS, 128, 128), F32)],
        input_output_aliases={0: 0},
        compiler_params=_cparams(("arbitrary",)),
    )(dproj, w, qe, ke, pm, eg, o, proj, dyc, dn_g)


def _gdn_chunk_grad(qd, kd, vd, gb, gbt, tmi, ssave, dsn, do, dvn, vn, name):
    T = qd.shape[0]
    G = GROUP
    ng = T // (CHUNK * G)

    def body(q_ref, k_ref, v_ref, gb_ref, gt_ref, t_ref, ss_ref, dsn_ref, do_ref, dvn_ref, vn_ref,
             dq_ref, dk_ref, dv_ref, dgb_ref):
        lower, strict, _ = _tri_consts()
        lmat = lower.astype(F32)
        ones = jnp.ones((CHUNK, 128), F32)
        lane = lax.broadcasted_iota(jnp.int32, (CHUNK, 128), 1)
        decay = [_chunk_decay(gb_ref, gt_ref, lmat, g) for g in range(G)]
        chains = [(g, h) for g in range(G) for h in range(DN_HEADS)]
        gens = []
        for g, h in chains:
            rows, gcs, grs = decay[g]
            sl = slice(128 * h, 128 * h + 128)
            c = _gdn_local(q_ref[rows, sl], k_ref[rows, sl], v_ref[rows, sl], gcs[:, h:h + 1], grs[h:h + 1, :],
                           gb_ref[rows, DN_HEADS + h:DN_HEADS + h + 1], lower, strict)
            c.update(tm=t_ref[rows, 64 * h:64 * h + 64], s=ss_ref[g, h], vn=vn_ref[rows, sl])
            gens.append(_gdn_chunk_bwd(c, do_ref[rows, sl], dvn_ref[rows, sl], dsn_ref[g, h], lower, strict, ones))
        dgb = [jnp.zeros((CHUNK, 128), F32) for _ in range(G)]
        for (g, h), (dq, dk, dv, dg, dbeta) in zip(chains, _lockstep(gens)):
            rows, sl = decay[g][0], slice(128 * h, 128 * h + 128)
            dq_ref[rows, sl], dk_ref[rows, sl], dv_ref[rows, sl] = dq, dk, dv
            dgb[g] = dgb[g] + jnp.where(lane == h, dg, 0.0) + jnp.where(lane == DN_HEADS + h, dbeta, 0.0)
        for g in range(G):
            dgb_ref[decay[g][0], :] = dgb[g]

    blk = pl.BlockSpec((CHUNK * G, 512), lambda n: (n, 0))
    half = pl.BlockSpec((CHUNK * G, 256), lambda n: (n, 0))
    nar = pl.BlockSpec((CHUNK * G, 128), lambda n: (n, 0))
    state = pl.BlockSpec((G, DN_HEADS, 128, 128), lambda n: (n, 0, 0, 0))
    return pl.pallas_call(
        body, name=name, grid=(ng,),
        in_specs=[blk, blk, blk, nar, pl.BlockSpec((G, 8, CHUNK), lambda n: (n, 0, 0)), half, state, state,
                  blk, blk, blk],
        out_specs=[blk, blk, blk, nar],
        out_shape=[jax.ShapeDtypeStruct((T, 512), F32)] * 3 + [jax.ShapeDtypeStruct((T, 128), F32)],
        compiler_params=_cparams(("parallel",)),
    )(qd, kd, vd, gb, gbt, tmi, ssave, dsn, do, dvn, vn)


def _gdn_prep_bwd1(dproj, proj, dqd, dkd, dvd, dgb, dkv_a, sconv_w, alog_v, dtb_v, name):
    T = proj.shape[0]
    tm = min(512, T)
    cur, prev, ab = _gdn_specs(T, tm)

    def body(dp_any, x_ref, xp_ref, ab_ref, dq_ref, dk_ref, dv_ref, dgb_ref, dkv_ref, w_ref, al_ref, dt_ref,
             o_ref, dpre_ref, st_ref, ext_ref):
        i = pl.program_id(0)
        pre = _gdn_conv(i, tm, x_ref, xp_ref, w_ref, ext_ref)
        y, dsl = _silu(pre), _dsilu(pre)
        for h in range(DN_HEADS):
            for base, g_ref, scale in ((0, dq_ref, 128 ** -0.5), (512, dk_ref, 1.0)):
                sl = slice(base + 128 * h, base + 128 * h + 128)
                xh = y[:, sl]
                r = lax.rsqrt(jnp.sum(xh * xh, axis=-1, keepdims=True) + EPS)
                xn = xh * r
                gy = g_ref[:, 128 * h:128 * h + 128]
                dpre_ref[:, sl] = (scale * r) * (gy - xn * jnp.sum(gy * xn, axis=-1, keepdims=True)) * dsl[:, sl]
        dpre_ref[:, 1024:] = dv_ref[...] * dsl[:, 1024:]
        abv, dgb = ab_ref[...], dgb_ref[...]
        lane = lax.broadcasted_iota(jnp.int32, (tm, 128), 1)
        na = -jnp.exp(al_ref[...])
        xs = abv + dt_ref[...]
        da = dgb * na * _sig(xs)
        b = _sig(abv)
        o_ref[:, :256] = dkv_ref[...].astype(BF16)
        o_ref[:, 256:] = jnp.where(lane < DN_HEADS, da,
                                   jnp.where(lane < 2 * DN_HEADS, dgb * b * (1.0 - b), 0.0)).astype(BF16)
        head = lane < DN_HEADS
        upd = jnp.concatenate([jnp.sum(jnp.where(head, dgb * na * _softplus(xs), 0.0), axis=0, keepdims=True),
                               jnp.sum(jnp.where(head, da, 0.0), axis=0, keepdims=True), jnp.zeros((6, 128), F32)],
                              axis=0)

        @pl.when(i == 0)
        def _():
            st_ref[...] = upd

        @pl.when(i > 0)
        def _():
            st_ref[...] += upd

    full = lambda s: pl.BlockSpec(s, lambda i: (0, 0))
    blk = pl.BlockSpec((tm, 512), lambda i: (i, 0))
    return pl.pallas_call(
        body, name=name, grid=(T // tm,),
        in_specs=[pl.BlockSpec(memory_space=pl.ANY), cur, prev, ab, blk, blk, blk,
                  pl.BlockSpec((tm, 128), lambda i: (i, 0)), pl.BlockSpec((tm, 256), lambda i: (i, 0)),
                  full((DN_K, QKV_C)), full((1, 128)), full((1, 128))],
        out_specs=[pl.BlockSpec((tm, 384), lambda i: (i, C_KA // 384)),
                   pl.BlockSpec((tm, QKV_C), lambda i: (i, 0)), full((8, 128))],
        out_shape=[jax.ShapeDtypeStruct(dproj.shape, BF16), jax.ShapeDtypeStruct((T, QKV_C), F32),
                   jax.ShapeDtypeStruct((8, 128), F32)],
        scratch_shapes=[pltpu.VMEM((tm + HALO_C, QKV_C), F32)],
        input_output_aliases={0: 0},
        compiler_params=_cparams(("arbitrary",)),
    )(dproj, proj, proj, proj, dqd, dkd, dvd, dgb, dkv_a, sconv_w, alog_v, dtb_v)


def _gdn_prep_bwd2(dproj, proj, dpre, sconv_w, name):
    T = proj.shape[0]
    tm = min(512, T)
    nt = T // tm
    r = tm // HALO_C
    cur, prev, _ = _gdn_specs(T, tm)

    def body(dp_any, x_ref, xp_ref, d_ref, dn_ref, w_ref, dx_ref, gw_ref, extx_ref, extd_ref):
        i = pl.program_id(0)
        extx_ref[:HALO_C] = jnp.where(i > 0, xp_ref[...], 0.0)
        extx_ref[HALO_C:] = x_ref[...]
        d = d_ref[...]
        extd_ref[:tm] = d
        extd_ref[tm:] = jnp.where(i < nt - 1, dn_ref[...], 0.0)
        dx = jnp.zeros((tm, QKV_C), F32)
        rows = []
        for k in range(DN_K):
            dx = dx + w_ref[k:k + 1, :] * extd_ref[pl.ds(DN_K - 1 - k, tm), :]
            rows.append(jnp.sum(d * extx_ref[pl.ds(HALO_C - DN_K + 1 + k, tm), :], axis=0, keepdims=True))
        rows.append(jnp.zeros((8 - DN_K, QKV_C), F32))
        gw = jnp.concatenate(rows, axis=0)
        dx_ref[...] = dx.astype(BF16)

        @pl.when(i == 0)
        def _():
            gw_ref[...] = gw

        @pl.when(i > 0)
        def _():
            gw_ref[...] += gw

    full = lambda s: pl.BlockSpec(s, lambda i: (0, 0))
    return pl.pallas_call(
        body, name=name, grid=(nt,),
        in_specs=[pl.BlockSpec(memory_space=pl.ANY), cur, prev, pl.BlockSpec((tm, QKV_C), lambda i: (i, 0)),
                  pl.BlockSpec((HALO_C, QKV_C), lambda i: (jnp.minimum((i + 1) * r, T // HALO_C - 1), 0)),
                  full((DN_K, QKV_C))],
        out_specs=[cur, full((8, QKV_C))],
        out_shape=[jax.ShapeDtypeStruct(dproj.shape, BF16), jax.ShapeDtypeStruct((8, QKV_C), F32)],
        scratch_shapes=[pltpu.VMEM((tm + HALO_C, QKV_C), F32), pltpu.VMEM((tm + HALO_C, QKV_C), F32)],
        input_output_aliases={0: 0},
        compiler_params=_cparams(("arbitrary",)),
    )(dproj, proj, proj, dpre, dpre, sconv_w)


def _merge_fwd(x, proj, ya, yb, yc, wa, wb, wc, wo, gate, name):
    T = x.shape[0]
    tm = min(256, T)

    def body(x_ref, mg_ref, ya_ref, yb_ref, yc_ref, wa_ref, wb_ref, wc_ref, wo_ref, gate_ref, o_ref):
        merged = (_sig(mg_ref[:, :D]) * _bdot(ya_ref[...], wa_ref[...])
                  + _sig(mg_ref[:, D:2 * D]) * _bdot(yb_ref[...], wb_ref[...])
                  + _sig(mg_ref[:, 2 * D:]) * _bdot(yc_ref[...], wc_ref[...]))
        o_ref[...] = x_ref[...] + gate_ref[...] * _bdot(merged, wo_ref[...])

    full = lambda s: pl.BlockSpec(s, lambda i: (0, 0))
    yb_ = pl.BlockSpec((tm, 512), lambda i: (i, 0))
    return pl.pallas_call(
        body, name=name, grid=(T // tm,),
        in_specs=[pl.BlockSpec((tm, D), lambda i: (i, 0)), pl.BlockSpec((tm, 3 * D), lambda i: (i, 0)), yb_, yb_, yb_,
                  full((512, D)), full((512, D)), full((512, D)), full((D, D)), full((1, D))],
        out_specs=pl.BlockSpec((tm, D), lambda i: (i, 0)),
        out_shape=jax.ShapeDtypeStruct((T, D), F32),
        compiler_params=_cparams(("parallel",)),
    )(x, proj, ya, yb, yc, wa, wb, wc, wo, _row(gate))


def _merge_bwd(dout, proj, ya, yb, yc, wa, wb, wc, wo, gate, name):
    T = dout.shape[0]
    tm = min(256, T)
    nt = T // tm

    def body(do_ref, mg_ref, ya_ref, yb_ref, yc_ref, wa_ref, wb_ref, wc_ref, wo_ref, gate_ref,
             dmg_ref, dya_ref, dyb_ref, dyc_ref, gwa_hbm, gwb_hbm, gwc_hbm, gwo_hbm, gg_ref,
             gwa_ref, gwb_ref, gwc_ref, gwo_ref):
        i = pl.program_id(0)

        @pl.when(i == 0)
        def _():
            for r in (gwa_ref, gwb_ref, gwc_ref, gwo_ref, gg_ref):
                r[...] = jnp.zeros_like(r)

        ys = (ya_ref[...], yb_ref[...], yc_ref[...])
        ws = (wa_ref, wb_ref, wc_ref)
        gs = tuple(_sig(mg_ref[:, j * D:(j + 1) * D]) for j in range(3))
        ps = tuple(_bdot(ys[j], ws[j][...]) for j in range(3))
        merged = gs[0] * ps[0] + gs[1] * ps[1] + gs[2] * ps[2]
        mo = _bdot(merged, wo_ref[...])
        do = do_ref[...]
        gg_ref[...] += jnp.sum(do * mo, axis=0, keepdims=True)
        dmo = do * gate_ref[...]
        dmerged = _bdot(dmo, wo_ref[...], NT)
        gwo_ref[...] += _bdot(merged, dmo, TN)
        for j, (dy_ref, gw_ref) in enumerate(((dya_ref, gwa_ref), (dyb_ref, gwb_ref), (dyc_ref, gwc_ref))):
            dp = dmerged * gs[j]
            dmg_ref[:, j * D:(j + 1) * D] = (dmerged * ps[j] * gs[j] * (1.0 - gs[j])).astype(BF16)
            dy_ref[...] = _bdot(dp, ws[j][...], NT)
            gw_ref[...] += _bdot(ys[j], dp, TN)

        @pl.when(i == nt - 1)
        def _():
            for src, dst in ((gwa_ref, gwa_hbm), (gwb_ref, gwb_hbm), (gwc_ref, gwc_hbm), (gwo_ref, gwo_hbm)):
                pltpu.sync_copy(src, dst)

    full = lambda s: pl.BlockSpec(s, lambda i: (0, 0))
    yb_ = pl.BlockSpec((tm, 512), lambda i: (i, 0))
    anyspec = pl.BlockSpec(memory_space=pl.ANY)
    return pl.pallas_call(
        body, name=name, grid=(nt,),
        in_specs=[pl.BlockSpec((tm, D), lambda i: (i, 0)), pl.BlockSpec((tm, 3 * D), lambda i: (i, 0)), yb_, yb_, yb_,
                  full((512, D)), full((512, D)), full((512, D)), full((D, D)), full((1, D))],
        out_specs=[pl.BlockSpec((tm, 3 * D), lambda i: (i, 0)), yb_, yb_, yb_, anyspec, anyspec, anyspec, anyspec,
                   full((1, D))],
        out_shape=[jax.ShapeDtypeStruct((T, NP), BF16)] + [jax.ShapeDtypeStruct((T, 512), F32)] * 3
        + [jax.ShapeDtypeStruct((512, D), F32)] * 3 + [jax.ShapeDtypeStruct((D, D), F32), jax.ShapeDtypeStruct((1, D), F32)],
        scratch_shapes=[pltpu.VMEM((512, D), F32)] * 3 + [pltpu.VMEM((D, D), F32)],
        compiler_params=_cparams(("arbitrary",)),
    )(dout, proj, ya, yb, yc, wa, wb, wc, wo, _row(gate))


def _loss_head(y, tgt, name):
    T = y.shape[0]
    tm = min(512, T)

    def body(y_ref, t_ref, dy_ref, l_ref):
        i = pl.program_id(0)
        diff = y_ref[...] - t_ref[...]
        dy_ref[...] = diff * (1.0 / D)
        part = jnp.sum(diff * diff, axis=0, keepdims=True)

        @pl.when(i == 0)
        def _():
            l_ref[...] = part

        @pl.when(i > 0)
        def _():
            l_ref[...] += part

    blk = pl.BlockSpec((tm, D), lambda i: (i, 0))
    return pl.pallas_call(
        body, name=name, grid=(T // tm,), in_specs=[blk, blk],
        out_specs=[blk, pl.BlockSpec((1, D), lambda i: (0, 0))],
        out_shape=[jax.ShapeDtypeStruct((T, D), F32), jax.ShapeDtypeStruct((1, D), F32)],
        compiler_params=_cparams(("arbitrary",)),
    )(y, tgt)


def _ada_fwd(c_all, w_ada, b_my, name):
    def body(c_ref, w_ref, b_ref, o_ref):
        sc = _silu(c_ref[...])
        for l in range(DEPTH):
            o_ref[l] = _bdot(sc, w_ref[l]) + b_ref[l:l + 1, :]

    return pl.pallas_call(body, name=name, out_shape=jax.ShapeDtypeStruct((DEPTH, N_DEV, w_ada.shape[2]), F32),
                          compiler_params=_cparams())(c_all, w_ada, b_my)


def _ada_bwd(c_all, dmod_my, name):
    def body(c_ref, d_ref, o_ref):
        sc = _silu(c_ref[...])
        for l in range(DEPTH):
            o_ref[l] = _bdot(sc, d_ref[l], TN)

    return pl.pallas_call(body, name=name, out_shape=jax.ShapeDtypeStruct((DEPTH, D, dmod_my.shape[2]), F32),
                          compiler_params=_cparams())(c_all, dmod_my)


def _adam_math(w, g, m, v):
    m = ADAM_B1 * m + (1.0 - ADAM_B1) * g
    v = ADAM_B2 * v + (1.0 - ADAM_B2) * (g * g)
    m_hat = m / (1.0 - ADAM_B1 ** ADAM_STEP)
    v_hat = v / (1.0 - ADAM_B2 ** ADAM_STEP)
    return -ADAM_LR * (m_hat / (jnp.sqrt(v_hat) + ADAM_EPS) + ADAM_WD * w), m, v


def _row_tile(rows, cap):
    best = rows
    for t in range(8, min(rows, cap) + 1, 8):
        if rows % t == 0:
            best = t
    return best if best <= cap else rows


def _adamw(w, g, m, v, name):
    R, C = w.shape
    tr = _row_tile(R, 256)

    def body(w_ref, g_ref, m_ref, v_ref, d_ref, mo_ref, vo_ref):
        d_ref[...], mo_ref[...], vo_ref[...] = _adam_math(w_ref[...], g_ref[...], m_ref[...], v_ref[...])

    blk = pl.BlockSpec((tr, C), lambda i: (i, 0))
    return pl.pallas_call(body, name=name, grid=(R // tr,), in_specs=[blk] * 4, out_specs=[blk] * 3,
                          out_shape=[jax.ShapeDtypeStruct((R, C), F32)] * 3,
                          compiler_params=_cparams(("parallel",)))(w, g, m, v)


def _sum_adamw_many(parts, ws, ms, vs, name):
    n = len(ws)

    def body(*refs):
        ins, outs = refs[:4 * n], refs[4 * n:]
        for i in range(n):
            g = ins[i][0]
            for j in range(1, N_DEV):
                g = g + ins[i][j]
            d, m, v = _adam_math(ins[n + i][...], g, ins[2 * n + i][...], ins[3 * n + i][...])
            outs[i][...], outs[n + i][...], outs[2 * n + i][...], outs[3 * n + i][...] = g, d, m, v

    shapes = [jax.ShapeDtypeStruct(w.shape, F32) for w in ws]
    out = pl.pallas_call(body, name=name, out_shape=shapes * 4, compiler_params=_cparams())(*parts, *ws, *ms, *vs)
    return out[:n], out[n:2 * n], out[2 * n:3 * n], out[3 * n:]


def _sum_adamw(parts0, parts1, w, m, v, name):
    P, R, C = parts0.shape
    tr = _row_tile(R, 128)
    nt = R // tr

    def body(p0_ref, p1_ref, w_ref, m_ref, v_ref, g_ref, d_ref, mo_ref, vo_ref):
        def emit(p_ref):
            g = p_ref[0].astype(F32)
            for j in range(1, P):
                g = g + p_ref[j].astype(F32)
            g_ref[...] = g
            d_ref[...], mo_ref[...], vo_ref[...] = _adam_math(w_ref[...], g, m_ref[...], v_ref[...])

        @pl.when(pl.program_id(0) == 0)
        def _():
            emit(p0_ref)

        @pl.when(pl.program_id(0) == 1)
        def _():
            emit(p1_ref)

    blk = pl.BlockSpec((tr, C), lambda l, i: (l * nt + i, 0))
    return pl.pallas_call(
        body, name=name, grid=(DEPTH, nt),
        in_specs=[pl.BlockSpec((P, tr, C), lambda l, i: (0, i * (1 - l) + (nt - 1) * l, 0)),
                  pl.BlockSpec((P, tr, C), lambda l, i: (0, i * l, 0)), blk, blk, blk],
        out_specs=[blk] * 4, out_shape=[jax.ShapeDtypeStruct((DEPTH * R, C), F32)] * 4,
        compiler_params=_cparams(("arbitrary", "arbitrary")))(parts0, parts1, w, m, v)


def _pair_sum(core, buf, recv, name):
    _, _, R, C = buf.shape
    tr = _row_tile(R, 128)

    def body(c_ref, a_ref, b_ref, o_ref):
        o_ref[...] = (a_ref[:, 0].astype(F32) + b_ref[...].astype(F32)).astype(BF16)

    return pl.pallas_call(
        body, name=name,
        grid_spec=pltpu.PrefetchScalarGridSpec(
            num_scalar_prefetch=1, grid=(R // tr,),
            in_specs=[pl.BlockSpec((4, 1, tr, C), lambda i, c: (0, c[0], i, 0)),
                      pl.BlockSpec((4, tr, C), lambda i, c: (0, i, 0))],
            out_specs=pl.BlockSpec((4, tr, C), lambda i, c: (0, i, 0))),
        out_shape=jax.ShapeDtypeStruct((4, R, C), BF16),
        compiler_params=_cparams(("parallel",)))(core, buf, recv)


SHARD_IN = D_IN // N_DEV


def _w_in_pieces():
    out, p = [], 0
    for a, b in _PAD_FROM:
        for j in range(N_DEV):
            lo, hi = max(a, SHARD_IN * j), min(b, SHARD_IN * (j + 1))
            if lo < hi:
                out.append((j, lo - SHARD_IN * j, hi - SHARD_IN * j, p + lo - a))
        p += b - a
    return out


def _assemble_w_in(gw, name):
    tr = 256
    nt = D // tr

    def body(x_ref, o_ref):
        for j, s0, s1, d0 in _w_in_pieces():
            o_ref[:, d0:d0 + s1 - s0] = x_ref[j, :, s0:s1]
        o_ref[:, D_IN:] = jnp.zeros((tr, NP - D_IN), gw.dtype)

    return pl.pallas_call(
        body, name=name, grid=(nt,),
        in_specs=[pl.BlockSpec((N_DEV, tr, SHARD_IN), lambda i: (0, i, 0))],
        out_specs=pl.BlockSpec((tr, NP), lambda i: (i, 0)),
        out_shape=jax.ShapeDtypeStruct((D, NP), gw.dtype),
        compiler_params=_cparams(("parallel",)))(gw)


def _split_w_in_grad(g, name):
    tr = 256

    def body(g_ref, o_ref):
        for j, s0, s1, d0 in _w_in_pieces():
            o_ref[j, :, s0:s1] = g_ref[:, d0:d0 + s1 - s0].astype(BF16)

    return pl.pallas_call(
        body, name=name, grid=(D // tr,),
        in_specs=[pl.BlockSpec((tr, NP), lambda i: (i, 0))],
        out_specs=pl.BlockSpec((N_DEV, tr, SHARD_IN), lambda i: (0, i, 0)),
        out_shape=jax.ShapeDtypeStruct((N_DEV, D, SHARD_IN), BF16),
        compiler_params=_cparams(("parallel",)))(g)


def _mesh_pos():
    return lax.axis_index("x"), lax.axis_index("y"), lax.axis_index("c")


def _launch(copies, peers, bufs, out_structs, sems, name, collective_id):
    n = len(bufs)
    if collective_id is None:
        anyspec = pl.BlockSpec(memory_space=pl.ANY)
        return list(pl.pallas_call(
            lambda *refs: copies(refs[:n], refs[n:n + len(out_structs)], *refs[n + len(out_structs):]),
            name=name, in_specs=[anyspec] * n, out_specs=[anyspec] * len(out_structs), out_shape=list(out_structs),
            scratch_shapes=list(sems))(*bufs))
    ins = [jax.new_ref(b, memory_space=pltpu.MemorySpace.HBM) for b in bufs]
    outs = [jax.empty_ref(s, memory_space=pltpu.MemorySpace.HBM) for s in out_structs]

    @pl.kernel(mesh=plsc.ScalarSubcoreMesh(axis_name="sequencer", num_cores=1), name=name, scratch_types=tuple(sems),
               compiler_params=pltpu.CompilerParams(collective_id=collective_id))
    def on_sequencer(*sem_refs):
        barrier = pltpu.get_barrier_semaphore()
        targets = peers()
        for p in targets:
            pl.semaphore_signal(barrier, inc=1, device_id=p, device_id_type=pl.DeviceIdType.MESH)
        pl.semaphore_wait(barrier, len(targets))
        copies(ins, outs, *sem_refs)

    on_sequencer()
    return [r[...] for r in outs]


def _all_gather(blocks, name, collective_id=None):
    n = len(blocks)

    def peers():
        x, y, c = _mesh_pos()
        return [(x, y, 1 - c), (1 - x, y, c), (x, 1 - y, c), (1 - x, 1 - y, c)]

    def copies(ins, outs, send_sems, recv_sems, local_sems):
        x, y, c = _mesh_pos()
        me, sibling = (x, y, c), (x, y, 1 - c)
        chips = [(1 - x, y), (x, 1 - y), (1 - x, 1 - y)]
        idx = lambda p: 4 * p[0] + 2 * p[1] + p[2]

        def copy(a, k, block, to, src=None):
            dst = outs[a].at[idx(block)]
            return pltpu.make_async_remote_copy(
                src_ref=dst if src is None else src, dst_ref=dst, send_sem=send_sems.at[a, k],
                recv_sem=recv_sems.at[a, k], device_id=to, device_id_type=pl.DeviceIdType.MESH)

        mine = [pltpu.make_async_copy(ins[a], outs[a].at[idx(me)], local_sems.at[a]) for a in range(n)]
        for cp in mine:
            cp.start()
        first = []
        for a in range(n):
            first.append(copy(a, 0, me, sibling, src=ins[a]))
            first += [copy(a, 1 + j, me, (*chip, c), src=ins[a]) for j, chip in enumerate(chips)]
        for cp in first:
            cp.start()
        passed = []
        for j, chip in enumerate(chips):
            for a in range(n):
                copy(a, 1 + j, (*chip, c), me).wait_recv()
                cp = copy(a, 4 + j, (*chip, c), sibling)
                cp.start()
                passed.append(cp)
        for a in range(n):
            copy(a, 0, sibling, me).wait_recv()
            for j, chip in enumerate(chips):
                copy(a, 4 + j, (*chip, 1 - c), me).wait_recv()
        for cp in first + passed:
            cp.wait_send()
        for cp in mine:
            cp.wait()

    return _launch(copies, peers, blocks, [jax.ShapeDtypeStruct((N_DEV,) + b.shape, b.dtype) for b in blocks],
                   [pltpu.SemaphoreType.DMA((n, 7)), pltpu.SemaphoreType.DMA((n, 7)), pltpu.SemaphoreType.DMA((n,))],
                   name, collective_id)


def _exchange_core(bufs, name, collective_id=None):
    n = len(bufs)

    def peers():
        x, y, c = _mesh_pos()
        return [(x, y, 1 - c)]

    def copies(ins, outs, send_sems, recv_sems):
        x, y, c = _mesh_pos()
        started = []
        for a in range(n):
            for q in range(4):
                cp = pltpu.make_async_remote_copy(
                    src_ref=ins[a].at[q, 1 - c], dst_ref=outs[a].at[q], send_sem=send_sems.at[a, q],
                    recv_sem=recv_sems.at[a, q], device_id=(x, y, 1 - c), device_id_type=pl.DeviceIdType.MESH)
                cp.start()
                started.append(cp)
        for cp in started:
            cp.wait()

    return _launch(copies, peers, bufs, [jax.ShapeDtypeStruct((4,) + b.shape[2:], b.dtype) for b in bufs],
                   [pltpu.SemaphoreType.DMA((n, 4)), pltpu.SemaphoreType.DMA((n, 4))], name, collective_id)


def _exchange_chips(bufs, name, collective_id=None):
    n = len(bufs)

    def peers():
        x, y, c = _mesh_pos()
        return [(1 - x, y, c), (x, 1 - y, c), (1 - x, 1 - y, c)]

    def copies(ins, outs, send_sems, recv_sems, local_sems):
        x, y, c = _mesh_pos()
        chip = 2 * x + y
        local = [pltpu.make_async_copy(ins[a].at[chip], outs[a].at[chip], local_sems.at[a]) for a in range(n)]
        for cp in local:
            cp.start()
        started = []
        for k in range(1, 4):
            px = 1 - x if k & 2 else x
            py = 1 - y if k & 1 else y
            for a in range(n):
                cp = pltpu.make_async_remote_copy(
                    src_ref=ins[a].at[2 * px + py], dst_ref=outs[a].at[chip], send_sem=send_sems.at[a, k - 1],
                    recv_sem=recv_sems.at[a, k - 1], device_id=(px, py, c), device_id_type=pl.DeviceIdType.MESH)
                cp.start()
                started.append(cp)
        for cp in started:
            cp.wait()
        for cp in local:
            cp.wait()

    return _launch(copies, peers, bufs, [jax.ShapeDtypeStruct(b.shape, b.dtype) for b in bufs],
                   [pltpu.SemaphoreType.DMA((n, 3)), pltpu.SemaphoreType.DMA((n, 3)), pltpu.SemaphoreType.DMA((n,))],
                   name, collective_id)


_SMALL = ("b_ada", "norm_g", "q_norm_g", "k_norm_g", "sinks", "dw_b", "ln_g", "ln_b", "pw2_b", "a_log", "dt_bias",
          "dn_norm_g", "dw_w", "sconv_w")


def _lane4(v):
    return jnp.pad(v, (0, 124)).reshape(1, 128)


def kernel(x, c, w_ada, b_ada, norm_g, w_in, q_norm_g, k_norm_g, sinks, dw_w, dw_b, ln_g, ln_b, pw2_w, pw2_b, sconv_w, a_log, dt_bias, dn_norm_g, w_proj_a, w_proj_b, w_proj_c, w_out, loss_target, m_w_ada, m_b_ada, m_norm_g, m_w_in, m_q_norm_g, m_k_norm_g, m_sinks, m_dw_w, m_dw_b, m_ln_g, m_ln_b, m_pw2_w, m_pw2_b, m_sconv_w, m_a_log, m_dt_bias, m_dn_norm_g, m_w_proj_a, m_w_proj_b, m_w_proj_c, m_w_out, v_w_ada, v_b_ada, v_norm_g, v_w_in, v_q_norm_g, v_k_norm_g, v_sinks, v_dw_w, v_dw_b, v_ln_g, v_ln_b, v_pw2_w, v_pw2_b, v_sconv_w, v_a_log, v_dt_bias, v_dn_norm_g, v_w_proj_a, v_w_proj_b, v_w_proj_c, v_w_out):
    T = x.shape[1]
    nc = T // CHUNK
    xi, yi, ci = _mesh_pos()
    me = 4 * xi + 2 * yi + ci
    big_w = (w_in, pw2_w, w_proj_a, w_proj_b, w_proj_c, w_out)
    big_m = (m_w_in, m_pw2_w, m_w_proj_a, m_w_proj_b, m_w_proj_c, m_w_out)
    big_v = (v_w_in, v_pw2_w, v_w_proj_a, v_w_proj_b, v_w_proj_c, v_w_out)

    ada_cols = w_ada.shape[2]
    dw_cols, sc_cols = dw_w.shape[2], sconv_w.shape[2]
    flat2 = lambda a: a.reshape(-1, a.shape[-1])
    big16 = [[a[l].astype(BF16) for l in range(DEPTH)] for a in big_w]
    c_all, gdw, gsc = _all_gather([c, dw_w, sconv_w], "gather_small", collective_id=0)
    (gw_in0,) = _all_gather([big16[0][0]], "gather_w_in0", collective_id=7)
    c_all = c_all.reshape(N_DEV, D)
    dw_f = gdw.transpose(1, 2, 0, 3).reshape(DEPTH, CONV_K, 512)
    sc_f = gsc.transpose(1, 2, 0, 3).reshape(DEPTH, DN_K, QKV_C)

    b_my = lax.dynamic_slice(b_ada, (0, me * ada_cols), (DEPTH, ada_cols))
    mod_part = _ada_fwd(c_all, w_ada, b_my, "ada_fwd")
    (gmod,) = _all_gather([mod_part.reshape(-1, 128)], "gather_mod")

    rest0 = [a[0] for a in big16[1:]]
    all1 = [a[1] for a in big16]
    (rest0, all1), gmod = lax.optimization_barrier(((rest0, all1), gmod))
    got0 = [gw_in0] + _all_gather(rest0, "gather_rest0", collective_id=1)
    got1 = _all_gather(all1, "gather_weights1", collective_id=6)
    wp, pw2_f, wa_f, wb_f, wc_f, wo_f = [], [], [], [], [], []
    for l, (gw_in, gpw2, gpa, gpb, gpc, gwo) in enumerate((got0, got1)):
        wp.append(_assemble_w_in(gw_in, f"assemble_w_in{l}"))
        pw2_f.append(gpw2.reshape(512, 512))
        for dst, g in ((wa_f, gpa), (wb_f, gpb), (wc_f, gpc)):
            dst.append(g.transpose(1, 0, 2).reshape(512, D))
        wo_f.append(gwo.reshape(D, D))
    mod_all = gmod.reshape(N_DEV, DEPTH, N_DEV, ada_cols).transpose(1, 2, 0, 3).reshape(DEPTH, N_DEV, 3 * D)
    mod = lax.dynamic_index_in_dim(mod_all, me, axis=1, keepdims=False)
    shift, scale, gate = mod[:, :D], mod[:, D:2 * D], mod[:, 2 * D:]

    xs, saved = [x[0]], []
    for l in range(DEPTH):
        xl = xs[-1]
        h = _norm_fwd(xl, norm_g[l], scale[l], shift[l], f"norm_fwd{l}")
        proj = _mm(h, wp[l], tm=min(1024, T), tn=1152, tk=D, name=f"in_proj{l}")
        ya = _attn_fwd(proj, q_norm_g[l], k_norm_g[l], sinks[l], f"attn_fwd{l}")
        yb = _conf_fwd(proj, dw_f[l], dw_b[l], ln_g[l], ln_b[l], pw2_f[l], pw2_b[l], f"conf_fwd{l}")
        alv, dtv, dng = _lane4(a_log[l]), _lane4(dt_bias[l]), _row(dn_norm_g[l])
        qd, kd, vd, gb = _gdn_prep_fwd(proj, sc_f[l], alv, dtv, f"gdn_prep_fwd{l}")
        gbt = gb[:, :8].reshape(nc, CHUNK, 8).transpose(0, 2, 1)
        u, w, qe, ke, pm, tmi, eg = _gdn_chunk_fwd(qd, kd, vd, gb, gbt, f"gdn_chunk_fwd{l}")
        yc, o, vn, ss = _gdn_scan_fwd(u, w, qe, ke, pm, eg, proj, dng, f"gdn_scan_fwd{l}")
        xs.append(_merge_fwd(xl, proj, ya, yb, yc, wa_f[l], wb_f[l], wc_f[l], wo_f[l], gate[l], f"merge_fwd{l}"))
        saved.append((h, proj, ya, yb, yc, qd, kd, vd, gb, gbt, ss, alv, dtv, dng, w, qe, ke, pm, tmi, eg, o, vn))

    dout, lsum = _loss_head(xs[-1], loss_target[0], "loss_head")

    small = {name: [None] * DEPTH for name in _SMALL}
    big_parts = [None] * DEPTH
    core = jnp.reshape(ci, (1,)).astype(jnp.int32)
    for l in reversed(range(DEPTH)):
        h, proj, ya, yb, yc, qd, kd, vd, gb, gbt, ss, alv, dtv, dng, w, qe, ke, pm, tmi, eg, o, vn = saved[l]
        dproj, dya, dyb, dyc, g_wa, g_wb, g_wc, g_wo, g_gate = _merge_bwd(
            dout, proj, ya, yb, yc, wa_f[l], wb_f[l], wc_f[l], wo_f[l], gate[l], f"merge_bwd{l}")
        dproj, dkv_a, g_q, g_k, g_s = _attn_bwd(dproj, proj, dya, q_norm_g[l], k_norm_g[l], sinks[l], f"attn_bwd{l}")
        dproj, du1, g_pw2, st_b = _conf_bwd1(dproj, proj, dyb, dw_f[l], dw_b[l], ln_g[l], ln_b[l], pw2_f[l], pw2_b[l],
                                             f"conf_bwd_a{l}")
        dproj, g_dw = _conf_bwd2(dproj, proj, du1, dw_f[l], f"conf_bwd_b{l}")
        dproj, do, dvn, dsn, g_dn = _gdn_scan_bwd(dproj, w, qe, ke, pm, eg, o, proj, dyc, dng, f"gdn_scan_bwd{l}")
        dqd, dkd, dvd, dgb = _gdn_chunk_grad(qd, kd, vd, gb, gbt, tmi, ss, dsn, do, dvn, vn, f"gdn_chunk_bwd{l}")
        dproj, dpre, st_c = _gdn_prep_bwd1(dproj, proj, dqd, dkd, dvd, dgb, dkv_a, sc_f[l], alv, dtv,
                                           f"gdn_prep_bwd_a{l}")
        dproj, g_sc = _gdn_prep_bwd2(dproj, proj, dpre, sc_f[l], f"gdn_prep_bwd_b{l}")
        g_wp = _mm(h, dproj, ta=True, tm=D, tn=1152, tk=min(2048, T), name=f"d_w_in{l}")
        by_dest = [_split_w_in_grad(g_wp, f"split_w_in_grad{l}"), g_pw2.reshape(N_DEV, -1, 512).astype(BF16)]
        by_dest += [g.reshape(512, N_DEV, -1).transpose(1, 0, 2).astype(BF16) for g in (g_wa, g_wb, g_wc)]
        by_dest.append(g_wo.reshape(N_DEV, -1, D).astype(BF16))
        by_dest = [b.reshape(4, 2, -1, b.shape[-1]) for b in by_dest]
        if l < DEPTH - 1:
            by_dest, big_parts[l + 1] = lax.optimization_barrier((by_dest, big_parts[l + 1]))
        from_sibling = _exchange_core(by_dest, f"exchange_grads_core{l}", collective_id=2 + 2 * l)

        def input_grad(dproj, dout):
            dh = _mm(dproj, wp[l], tb=True, tm=min(1024, T), tn=D, tk=2688, name=f"d_h{l}")
            return _norm_bwd(dh, xs[l], dout, norm_g[l], scale[l], f"norm_bwd{l}")

        if l > 0:
            dout, st_n = input_grad(dproj, dout)
            from_sibling, dout = lax.optimization_barrier((from_sibling, dout))
        chip_sums = [_pair_sum(core, b, r, f"pair_sum{l}_{i}") for i, (b, r) in enumerate(zip(by_dest, from_sibling))]
        big_parts[l] = _exchange_chips(chip_sums, f"exchange_grads_chips{l}", collective_id=3 + 2 * l)
        if l > 0:
            dout, chip_sums = lax.optimization_barrier((dout, chip_sums))
        else:
            dproj, chip_sums = lax.optimization_barrier((dproj, chip_sums))
            dout, st_n = input_grad(dproj, dout)
        for name, g in (("b_ada", jnp.concatenate([st_n[0], st_n[1], g_gate[0]])), ("norm_g", st_n[2]),
                        ("q_norm_g", g_q.reshape(ATT_HEADS, ATT_HD).sum(0)), ("k_norm_g", g_k.reshape(2, ATT_HD).sum(0)),
                        ("sinks", g_s[0]), ("dw_b", st_b[3]),
                        ("ln_g", st_b[1]), ("ln_b", st_b[2]), ("pw2_b", st_b[0]), ("a_log", st_c[0, :4]),
                        ("dt_bias", st_c[1, :4]), ("dn_norm_g", g_dn[0]), ("dw_w", g_dw[:CONV_K]),
                        ("sconv_w", g_sc[:DN_K])):
            small[name][l] = g
    grad_x = dout[None]

    names = list(_SMALL)
    gathered = _all_gather([jnp.stack(small[n]) for n in names] + [lsum], "gather_small_grads")
    gparts = dict(zip(names, gathered))
    loss = 0.5 * jnp.sum(jnp.sum(gathered[-1], axis=(1, 2))) / D
    dmod_my = lax.dynamic_slice(gparts["b_ada"], (0, 0, me * ada_cols), (N_DEV, DEPTH, ada_cols)).transpose(1, 0, 2)
    g_w_ada = _ada_bwd(c_all, dmod_my, "ada_bwd")
    gparts["dw_w"] = lax.dynamic_slice(gparts["dw_w"], (0, 0, 0, me * dw_cols), (N_DEV, DEPTH, CONV_K, dw_cols))
    gparts["sconv_w"] = lax.dynamic_slice(gparts["sconv_w"], (0, 0, 0, me * sc_cols), (N_DEV, DEPTH, DN_K, sc_cols))
    env = dict(b_ada=(b_ada, m_b_ada, v_b_ada), norm_g=(norm_g, m_norm_g, v_norm_g),
               q_norm_g=(q_norm_g, m_q_norm_g, v_q_norm_g), k_norm_g=(k_norm_g, m_k_norm_g, v_k_norm_g),
               sinks=(sinks, m_sinks, v_sinks), dw_b=(dw_b, m_dw_b, v_dw_b), ln_g=(ln_g, m_ln_g, v_ln_g),
               ln_b=(ln_b, m_ln_b, v_ln_b), pw2_b=(pw2_b, m_pw2_b, v_pw2_b), a_log=(a_log, m_a_log, v_a_log),
               dt_bias=(dt_bias, m_dt_bias, v_dt_bias), dn_norm_g=(dn_norm_g, m_dn_norm_g, v_dn_norm_g),
               dw_w=(dw_w, m_dw_w, v_dw_w), sconv_w=(sconv_w, m_sconv_w, v_sconv_w))
    upd = _sum_adamw_many([gparts[n] for n in names], [env[n][0] for n in names], [env[n][1] for n in names],
                          [env[n][2] for n in names], "sum_adamw_small")

    d_ada, nm_ada, nv_ada = (u.reshape(w_ada.shape) for u in
                             _adamw(flat2(w_ada), flat2(g_w_ada), flat2(m_w_ada), flat2(v_w_ada), "adamw_w_ada"))

    big_parts, (dout, upd, d_ada, nm_ada, nv_ada) = lax.optimization_barrier(
        (big_parts, (dout, upd, d_ada, nm_ada, nv_ada)))
    g_small, d_small, m_small, v_small = (dict(zip(names, u)) for u in upd)
    res = [_sum_adamw(p0, p1, flat2(w), flat2(m), flat2(v), f"sum_adamw{i}")
           for i, (p0, p1, w, m, v) in enumerate(zip(big_parts[0], big_parts[1], big_w, big_m, big_v))]
    g_big, d_big, m_big, v_big = ([r[k].reshape(w.shape) for r, w in zip(res, big_w)] for k in range(4))

    order = ("w_ada", "b_ada", "norm_g", "w_in", "q_norm_g", "k_norm_g", "sinks", "dw_w", "dw_b", "ln_g", "ln_b",
             "pw2_w", "pw2_b", "sconv_w", "a_log", "dt_bias", "dn_norm_g", "w_proj_a", "w_proj_b", "w_proj_c", "w_out")
    big_names = ("w_in", "pw2_w", "w_proj_a", "w_proj_b", "w_proj_c", "w_out")

    def pick(kind):
        src_small = (g_small, d_small, m_small, v_small)[kind]
        src_big = (g_big, d_big, m_big, v_big)[kind]
        src_ada = (g_w_ada, d_ada, nm_ada, nv_ada)[kind]
        return [src_ada if n == "w_ada" else src_big[big_names.index(n)] if n in big_names else src_small[n]
                for n in order]

    return (loss, grad_x, *pick(0), *pick(1), *pick(2), *pick(3))
```

```python
import functools
import math

import jax
import jax.numpy as jnp
import numpy as np
from jax import lax
from jax.experimental import pallas as pl
from jax.experimental.pallas import tpu as pltpu
from jax.experimental.pallas import tpu_sc as plsc

F32 = jnp.float32
BF16 = jnp.bfloat16
HI = lax.Precision.HIGHEST

N_DEV = 8
D = 1024
DEPTH = 2
EPS = 1e-6
NEG_INF = -1e30
WINDOW = 128
ATT_HEADS = 8
ATT_HD = 64
CONV_K = 31
DN_HEADS = 4
DN_K = 4
CHUNK = 64
D_IN = 7944
VMEM_LIMIT = 56 * 1024 * 1024

C_MG, C_QA, C_ZA, C_ZB, C_QC, C_KC, C_VC, C_GV, C_GG, C_ZC, C_KA, C_VA, C_AB, NP = (
    0, 3072, 3584, 4096, 4608, 5120, 5632, 6144, 6656, 7168, 7680, 7808, 7936, 8064)
_PAD_FROM = ((4872, 7944), (0, 512), (768, 1280), (2304, 2816), (2816, 4352), (1280, 2304), (4360, 4872),
             (512, 768), (4352, 4360))

ALIBI = tuple(float(2.0 ** (-8.0 * (h + 1) / ATT_HEADS)) for h in range(ATT_HEADS))

ADAM_LR, ADAM_B1, ADAM_B2, ADAM_EPS, ADAM_WD, ADAM_STEP = 0.001, 0.9, 0.999, 1e-08, 0.01, 10


def _cparams(sem=None):
    return pltpu.CompilerParams(dimension_semantics=sem, vmem_limit_bytes=VMEM_LIMIT)


def _sig(x):
    return jax.nn.sigmoid(x)


def _silu(x):
    return x * _sig(x)


def _dsilu(x):
    s = _sig(x)
    return s * (1.0 + x * (1.0 - s))


def _dot(a, b, dims=((1,), (0,)), precision=None):
    return lax.dot_general(a, b, (dims, ((), ())), preferred_element_type=F32, precision=precision)


def _bdot(a, b, dims=((1,), (0,))):
    return _dot(a.astype(BF16), b.astype(BF16), dims)


NN, NT, TN = ((1,), (0,)), ((1,), (1,)), ((0,), (0,))


def _row(v):
    return v.reshape(1, -1)


def _mm(a, b, *, ta=False, tb=False, tm, tn, tk, name):
    M, K = (a.shape[1], a.shape[0]) if ta else a.shape
    N = b.shape[0] if tb else b.shape[1]
    assert M % tm == 0 and N % tn == 0 and K % tk == 0, (M, N, K, tm, tn, tk)
    nk = K // tk
    dims = ((0 if ta else 1,), (1 if tb else 0,))

    def body(a_ref, b_ref, o_ref):
        k = pl.program_id(2)
        part = _bdot(a_ref[...], b_ref[...], dims)

        @pl.when(k == 0)
        def _():
            o_ref[...] = part

        @pl.when(k > 0)
        def _():
            o_ref[...] += part

    a_spec = pl.BlockSpec((tk, tm), lambda i, j, k: (k, i)) if ta else pl.BlockSpec((tm, tk), lambda i, j, k: (i, k))
    b_spec = pl.BlockSpec((tn, tk), lambda i, j, k: (j, k)) if tb else pl.BlockSpec((tk, tn), lambda i, j, k: (k, j))
    return pl.pallas_call(
        body, name=name, grid=(M // tm, N // tn, nk),
        in_specs=[a_spec, b_spec], out_specs=pl.BlockSpec((tm, tn), lambda i, j, k: (i, j)),
        out_shape=jax.ShapeDtypeStruct((M, N), F32),
        compiler_params=_cparams(("parallel", "parallel", "arbitrary")),
    )(a, b)


def _norm_fwd(x, norm_g, scale, shift, name):
    T = x.shape[0]
    tm = min(512, T)

    def body(x_ref, g_ref, sc_ref, sh_ref, h_ref):
        xv = x_ref[...]
        r = lax.rsqrt(jnp.mean(xv * xv, axis=-1, keepdims=True) + EPS)
        h_ref[...] = ((xv * r) * g_ref[...] * (1.0 + sc_ref[...]) + sh_ref[...]).astype(BF16)

    vec = pl.BlockSpec((1, D), lambda i: (0, 0))
    return pl.pallas_call(
        body, name=name, grid=(T // tm,),
        in_specs=[pl.BlockSpec((tm, D), lambda i: (i, 0)), vec, vec, vec],
        out_specs=pl.BlockSpec((tm, D), lambda i: (i, 0)),
        out_shape=jax.ShapeDtypeStruct((T, D), BF16),
        compiler_params=_cparams(("parallel",)),
    )(x, _row(norm_g), _row(scale), _row(shift))


def _norm_bwd(dh, x, dres, norm_g, scale, name):
    T = x.shape[0]
    tm = min(512, T)

    def body(dh_ref, x_ref, dr_ref, g_ref, sc_ref, dx_ref, st_ref):
        i = pl.program_id(0)
        xv, dhv = x_ref[...], dh_ref[...]
        r = lax.rsqrt(jnp.mean(xv * xv, axis=-1, keepdims=True) + EPS)
        xh = xv * r
        g, s1 = g_ref[...], 1.0 + sc_ref[...]
        dxh = dhv * (g * s1)
        dx_ref[...] = dr_ref[...] + r * (dxh - xh * jnp.mean(dxh * xh, axis=-1, keepdims=True))
        dhx = dhv * xh
        upd = jnp.concatenate([jnp.sum(dhv, axis=0, keepdims=True), jnp.sum(dhx * g, axis=0, keepdims=True),
                               jnp.sum(dhx * s1, axis=0, keepdims=True), jnp.zeros((5, D), F32)], axis=0)

        @pl.when(i == 0)
        def _():
            st_ref[...] = upd

        @pl.when(i > 0)
        def _():
            st_ref[...] += upd

    vec = pl.BlockSpec((1, D), lambda i: (0, 0))
    blk = pl.BlockSpec((tm, D), lambda i: (i, 0))
    return pl.pallas_call(
        body, name=name, grid=(T // tm,),
        in_specs=[blk, blk, blk, vec, vec],
        out_specs=[blk, pl.BlockSpec((8, D), lambda i: (0, 0))],
        out_shape=[jax.ShapeDtypeStruct((T, D), F32), jax.ShapeDtypeStruct((8, D), F32)],
        compiler_params=_cparams(("arbitrary",)),
    )(dh, x, dres, _row(norm_g), _row(scale))


def _rms(x, g):
    r = lax.rsqrt(jnp.mean(x * x, axis=-1, keepdims=True) + EPS)
    return x * r, r


def _head_mean_matrix():
    head = np.arange(ATT_HEADS * ATT_HD) // ATT_HD
    return jnp.asarray((head[:, None] == head[None, :]) * (1.0 / ATT_HD), BF16)


def _head_rms(x, hm):
    r = lax.rsqrt(_dot_exact(x * x, hm) + EPS)
    return x * r, r


def _head_rms_bwd(dy, xh, r, g, hm):
    dxh = dy * g
    return r * (dxh - xh * _dot_exact(dxh * xh, hm)), dy * xh


def _attn_mask(n):
    qi = lax.broadcasted_iota(jnp.int32, (WINDOW, 2 * WINDOW), 0)
    kj = lax.broadcasted_iota(jnp.int32, (WINDOW, 2 * WINDOW), 1)
    dist = qi + WINDOW - kj
    valid = (dist >= 0) & (dist < WINDOW) & ((n > 0) | (kj >= WINDOW))
    return valid, dist.astype(F32)


def _attn_probs(s, h, sink, valid, distf):
    s = s - ALIBI[h] * distf
    s = jnp.where(valid, s, NEG_INF)
    m = jnp.maximum(jnp.max(s, axis=-1, keepdims=True), sink)
    p = jnp.exp(s - m)
    es = jnp.exp(sink - m)
    den = jnp.sum(p, axis=-1, keepdims=True) + es
    return p / den, es / den


def _attn_fwd(proj, q_norm_g, k_norm_g, sinks, name):
    T = proj.shape[0]
    nb = T // WINDOW

    def body(sink_ref, q_ref, z_ref, kc_ref, kp_ref, vc_ref, vp_ref, qg_ref, kg_ref, hm_ref, o_ref):
        n = pl.program_id(0)
        valid, distf = _attn_mask(n)
        k2 = jnp.concatenate([kp_ref[...], kc_ref[...]], axis=0)
        v2 = jnp.concatenate([vp_ref[...], vc_ref[...]], axis=0).astype(BF16)
        kn = (_head_rms(k2, hm_ref[:128, :128])[0] * kg_ref[...]).astype(BF16)
        qn = ((_head_rms(q_ref[...], hm_ref[...])[0] * qg_ref[...]) * (ATT_HD ** -0.5)).astype(BF16)

        def head(h):
            sl, gsl = slice(64 * h, 64 * h + 64), slice(64 * (h // 4), 64 * (h // 4) + 64)
            s = _dot(qn[:, sl], kn[:, gsl], NT)
            yield
            p, _ = _attn_probs(s, h, sink_ref[h], valid, distf)
            o_ref[:, sl] = _dot(p.astype(BF16), v2[:, gsl])
            yield

        _lockstep([head(h) for h in range(ATT_HEADS)])
        o_ref[...] = o_ref[...] * _silu(z_ref[...])

    prev = lambda n: jnp.maximum(n - 1, 0)
    return pl.pallas_call(
        body, name=name, grid=(nb,),
        in_specs=[pl.BlockSpec(memory_space=pltpu.SMEM),
                  pl.BlockSpec((WINDOW, 512), lambda n: (n, C_QA // 512)),
                  pl.BlockSpec((WINDOW, 512), lambda n: (n, C_ZA // 512)),
                  pl.BlockSpec((WINDOW, 128), lambda n: (n, C_KA // 128)),
                  pl.BlockSpec((WINDOW, 128), lambda n: (prev(n), C_KA // 128)),
                  pl.BlockSpec((WINDOW, 128), lambda n: (n, C_VA // 128)),
                  pl.BlockSpec((WINDOW, 128), lambda n: (prev(n), C_VA // 128)),
                  pl.BlockSpec((1, 512), lambda n: (0, 0)), pl.BlockSpec((1, 128), lambda n: (0, 0)),
                  pl.BlockSpec((512, 512), lambda n: (0, 0))],
        out_specs=pl.BlockSpec((WINDOW, 512), lambda n: (n, 0)),
        out_shape=jax.ShapeDtypeStruct((T, 512), F32),
        compiler_params=_cparams(("parallel",)),
    )(sinks, proj, proj, proj, proj, proj, proj, _row(jnp.tile(q_norm_g, ATT_HEADS)), _row(jnp.tile(k_norm_g, 2)),
      _head_mean_matrix())


def _rms_bwd(dy, xh, r, g):
    dxh = dy * g
    return r * (dxh - xh * jnp.mean(dxh * xh, axis=-1, keepdims=True)), dy * xh


def _attn_bwd(dproj, proj, dya, q_norm_g, k_norm_g, sinks, name):
    T = proj.shape[0]
    nb = T // WINDOW

    def body(sink_ref, dp_any, q_ref, z_ref, kc_ref, kp_ref, vc_ref, vp_ref, dy_ref, qg_ref, kg_ref, hm_ref,
             dqz_ref, dkv_ref, gq_ref, gk_ref, gs_ref, ck_ref, cv_ref, o_sc, dq_sc):
        n = pl.program_id(0)

        @pl.when(n == 0)
        def _():
            gq_ref[...] = jnp.zeros_like(gq_ref)
            gk_ref[...] = jnp.zeros_like(gk_ref)
            gs_ref[...] = jnp.zeros_like(gs_ref)
            ck_ref[...] = jnp.zeros_like(ck_ref)
            cv_ref[...] = jnp.zeros_like(cv_ref)

        lane8 = lax.broadcasted_iota(jnp.int32, (1, 8), 1)

        @pl.when(n < nb)
        def _():
            valid, distf = _attn_mask(n)
            k2 = jnp.concatenate([kp_ref[...], kc_ref[...]], axis=0)
            v2 = jnp.concatenate([vp_ref[...], vc_ref[...]], axis=0).astype(BF16)
            kn = (_head_rms(k2, hm_ref[:128, :128])[0] * kg_ref[...]).astype(BF16)
            qh, qr = _head_rms(q_ref[...], hm_ref[...])
            qn = ((qh * qg_ref[...]) * (ATT_HD ** -0.5)).astype(BF16)
            zs = z_ref[...]
            do_all = dy_ref[...] * _silu(zs)
            dob_all = do_all.astype(BF16)

            def head(h):
                sl, gsl = slice(64 * h, 64 * h + 64), slice(64 * (h // 4), 64 * (h // 4) + 64)
                s = _dot(qn[:, sl], kn[:, gsl], NT)
                dpm = _dot(dob_all[:, sl], v2[:, gsl], NT)
                yield
                p, ps = _attn_probs(s, h, sink_ref[h], valid, distf)
                pb = p.astype(BF16)
                o_sc[:, sl] = _dot(pb, v2[:, gsl])
                dvg = _dot(pb, dob_all[:, sl], TN)
                delta = jnp.sum(p * dpm, axis=-1, keepdims=True)
                ds = (p * (dpm - delta)).astype(BF16)
                gs = jnp.where(lane8 == h, -jnp.sum(ps * delta, axis=0, keepdims=True), 0.0)
                yield
                dkn = _dot(ds, qn[:, sl], TN)
                dq_sc[:, sl] = _dot(ds, kn[:, gsl])
                yield
                return dkn, dvg, gs

            res = _lockstep([head(h) for h in range(ATT_HEADS)])
            dqz_ref[:, 512:] = (dy_ref[...] * o_sc[...] * _dsilu(zs)).astype(BF16)
            dq, gq = _head_rms_bwd(dq_sc[...] * (ATT_HD ** -0.5), qh, qr, qg_ref[...], hm_ref[...])
            dqz_ref[:, :512] = dq.astype(BF16)
            gq_acc = jnp.sum(gq, axis=0, keepdims=True)
            gs_acc = sum(r[2] for r in res[1:]) + res[0][2]
            for g in range(2):
                dkn = (res[4 * g][0] + res[4 * g + 1][0]) + (res[4 * g + 2][0] + res[4 * g + 3][0])
                dvg = (res[4 * g][1] + res[4 * g + 1][1]) + (res[4 * g + 2][1] + res[4 * g + 3][1])
                ksl = slice(64 * g, 64 * g + 64)
                vsl = slice(128 + 64 * g, 128 + 64 * g + 64)
                dkv_ref[:, ksl] = ck_ref[:, ksl] + dkn[:WINDOW]
                dkv_ref[:, vsl] = cv_ref[:, ksl] + dvg[:WINDOW]
                ck_ref[:, ksl] = dkn[WINDOW:]
                cv_ref[:, ksl] = dvg[WINDOW:]
            gq_ref[...] += gq_acc
            gs_ref[...] += gs_acc

        @pl.when(n == nb)
        def _():
            dkv_ref[:, :128] = ck_ref[...]
            dkv_ref[:, 128:] = cv_ref[...]

        @pl.when(n > 0)
        def _():
            hm = hm_ref[:128, :128]
            kh, kr = _head_rms(kp_ref[...], hm)
            dk, gk = _head_rms_bwd(dkv_ref[:, :128], kh, kr, kg_ref[...], hm)
            dkv_ref[:, :128] = dk
            gk_ref[...] += jnp.sum(gk, axis=0, keepdims=True)

    cur = lambda n: jnp.minimum(n, nb - 1)
    prev = lambda n: jnp.maximum(n - 1, 0)
    small = lambda w: pl.BlockSpec((1, w), lambda n: (0, 0))
    return pl.pallas_call(
        body, name=name, grid=(nb + 1,),
        in_specs=[pl.BlockSpec(memory_space=pltpu.SMEM), pl.BlockSpec(memory_space=pl.ANY),
                  pl.BlockSpec((WINDOW, 512), lambda n: (cur(n), C_QA // 512)),
                  pl.BlockSpec((WINDOW, 512), lambda n: (cur(n), C_ZA // 512)),
                  pl.BlockSpec((WINDOW, 128), lambda n: (cur(n), C_KA // 128)),
                  pl.BlockSpec((WINDOW, 128), lambda n: (prev(n), C_KA // 128)),
                  pl.BlockSpec((WINDOW, 128), lambda n: (cur(n), C_VA // 128)),
                  pl.BlockSpec((WINDOW, 128), lambda n: (prev(n), C_VA // 128)),
                  pl.BlockSpec((WINDOW, 512), lambda n: (cur(n), 0)),
                  small(512), small(128), pl.BlockSpec((512, 512), lambda n: (0, 0))],
        out_specs=[pl.BlockSpec((WINDOW, 1024), lambda n: (cur(n), C_QA // 1024)),
                   pl.BlockSpec((WINDOW, 256), lambda n: (prev(n), 0)),
                   small(512), small(128), small(8)],
        out_shape=[jax.ShapeDtypeStruct(dproj.shape, BF16), jax.ShapeDtypeStruct((T, 256), F32),
                   jax.ShapeDtypeStruct((1, 512), F32), jax.ShapeDtypeStruct((1, 128), F32),
                   jax.ShapeDtypeStruct((1, 8), F32)],
        scratch_shapes=[pltpu.VMEM((WINDOW, 128), F32), pltpu.VMEM((WINDOW, 128), F32),
                        pltpu.VMEM((WINDOW, 512), F32), pltpu.VMEM((WINDOW, 512), F32)],
        input_output_aliases={1: 0},
        compiler_params=_cparams(("arbitrary",)),
    )(sinks, dproj, proj, proj, proj, proj, proj, proj, dya, _row(jnp.tile(q_norm_g, ATT_HEADS)),
      _row(jnp.tile(k_norm_g, 2)), _head_mean_matrix())


HALO_B = 32


def _conf_specs(T, tm):
    r = tm // HALO_B
    cur = lambda c: pl.BlockSpec((tm, 512), lambda i: (i, c // 512))
    prev = lambda c: pl.BlockSpec((HALO_B, 512), lambda i: (jnp.maximum(i * r - 1, 0), c // 512))
    return cur, prev


SUB = 8
ROW_CHUNK = 64


def _shifted_copies(ext_ref, sh_ref):
    total = ext_ref.shape[0]
    for r in range(SUB):
        rows = total if r == 0 else total - SUB
        sh_ref[r, :rows, :] = ext_ref[pl.ds(r, rows), :]


def _taps_by_shift(offsets):
    groups = {}
    for k, o in enumerate(offsets):
        q, r = divmod(o, SUB)
        groups.setdefault(r, []).append((k, q))
    return groups


def _conv_taps(sh_ref, w_ref, offsets, out_ref, init):
    groups = _taps_by_shift(offsets)

    def chunk(ci, carry):
        r0 = pl.multiple_of(ci * ROW_CHUNK, ROW_CHUNK)
        acc = jnp.zeros((ROW_CHUNK, out_ref.shape[1]), F32) + init
        for r, taps in groups.items():
            win = sh_ref[r, pl.ds(r0, ROW_CHUNK + SUB * max(q for _, q in taps)), :]
            for k, q in taps:
                acc = acc + w_ref[k:k + 1, :] * win[SUB * q:SUB * q + ROW_CHUNK]
        out_ref[pl.ds(r0, ROW_CHUNK), :] = acc
        return carry

    lax.fori_loop(0, out_ref.shape[0] // ROW_CHUNK, chunk, 0)


def _conv_weight_grad(sh_ref, d_ref, offsets):
    tm, width = d_ref.shape
    out = [None] * len(offsets)
    for r, taps in _taps_by_shift(offsets).items():
        def chunk(ci, accs, r=r, taps=taps):
            r0 = pl.multiple_of(ci * ROW_CHUNK, ROW_CHUNK)
            d = d_ref[pl.ds(r0, ROW_CHUNK), :]
            win = sh_ref[r, pl.ds(r0, ROW_CHUNK + SUB * max(q for _, q in taps)), :]
            return tuple(a + jnp.sum((d * win[SUB * q:SUB * q + ROW_CHUNK]).reshape(ROW_CHUNK // SUB, SUB, width),
                                     axis=0) for a, (_, q) in zip(accs, taps))

        accs = lax.fori_loop(0, tm // ROW_CHUNK, chunk, tuple(jnp.zeros((SUB, width), F32) for _ in taps))
        for a, (k, _) in zip(accs, taps):
            out[k] = jnp.sum(a, axis=0, keepdims=True)
    return out


def _conf_scratch(tm):
    return [pltpu.VMEM((tm + HALO_B, 512), F32), pltpu.VMEM((SUB, tm + HALO_B, 512), F32), pltpu.VMEM((tm, 512), F32)]


def _conf_core(i, tm, gv_ref, gg_ref, gvp_ref, ggp_ref, w_ref, b_ref, lg_ref, lb_ref, pw_ref, pb_ref, ext_ref, sh_ref,
               cv_ref):
    up = gvp_ref[...] * _sig(ggp_ref[...])
    ext_ref[:HALO_B] = jnp.where(i > 0, up, 0.0)
    ext_ref[HALO_B:] = gv_ref[...] * _sig(gg_ref[...])
    _shifted_copies(ext_ref, sh_ref)
    _conv_taps(sh_ref, w_ref, [HALO_B - CONV_K + 1 + k for k in range(CONV_K)], cv_ref, b_ref[...])
    acc = cv_ref[...]
    mu = jnp.mean(acc, axis=-1, keepdims=True)
    xc = acc - mu
    rstd = lax.rsqrt(jnp.mean(xc * xc, axis=-1, keepdims=True) + EPS)
    xh = xc * rstd
    u2 = xh * lg_ref[...] + lb_ref[...]
    u3 = _silu(u2)
    ypre = _bdot(u3, pw_ref[...]) + pb_ref[...]
    return xh, rstd, u2, u3, ypre


def _conf_fwd(proj, dw_w, dw_b, ln_g, ln_b, pw2, pw2_b, name):
    T = proj.shape[0]
    tm = min(512, T)
    cur, prev = _conf_specs(T, tm)

    def body(gv_ref, gg_ref, gvp_ref, ggp_ref, zb_ref, w_ref, b_ref, lg_ref, lb_ref, pw_ref, pb_ref, o_ref, *scratch):
        i = pl.program_id(0)
        ypre = _conf_core(i, tm, gv_ref, gg_ref, gvp_ref, ggp_ref, w_ref, b_ref, lg_ref, lb_ref, pw_ref, pb_ref,
                          *scratch)[4]
        o_ref[...] = ypre * _silu(zb_ref[...])

    full = lambda s: pl.BlockSpec(s, lambda i: (0, 0))
    return pl.pallas_call(
        body, name=name, grid=(T // tm,),
        in_specs=[cur(C_GV), cur(C_GG), prev(C_GV), prev(C_GG), cur(C_ZB), full((CONV_K, 512)), full((1, 512)),
                  full((1, 512)), full((1, 512)), full((512, 512)), full((1, 512))],
        out_specs=pl.BlockSpec((tm, 512), lambda i: (i, 0)),
        out_shape=jax.ShapeDtypeStruct((T, 512), F32),
        scratch_shapes=_conf_scratch(tm),
        compiler_params=_cparams(("parallel",)),
    )(proj, proj, proj, proj, proj, dw_w, _row(dw_b), _row(ln_g), _row(ln_b), pw2, _row(pw2_b))


def _conf_bwd1(dproj, proj, dyb, dw_w, dw_b, ln_g, ln_b, pw2, pw2_b, name):
    T = proj.shape[0]
    tm = min(512, T)
    cur, prev = _conf_specs(T, tm)

    def body(dp_any, gv_ref, gg_ref, gvp_ref, ggp_ref, zb_ref, dy_ref, w_ref, b_ref, lg_ref, lb_ref, pw_ref, pb_ref,
             dzb_ref, du1_ref, gpw_ref, st_ref, *scratch):
        i = pl.program_id(0)
        xh, rstd, u2, u3, ypre = _conf_core(i, tm, gv_ref, gg_ref, gvp_ref, ggp_ref, w_ref, b_ref, lg_ref, lb_ref,
                                            pw_ref, pb_ref, *scratch)
        zb, dy = zb_ref[...], dy_ref[...]
        dzb_ref[...] = (dy * ypre * _dsilu(zb)).astype(BF16)
        dyp = dy * _silu(zb)
        du2 = _bdot(dyp, pw_ref[...], NT) * _dsilu(u2)
        dxh = du2 * lg_ref[...]
        du1 = rstd * (dxh - jnp.mean(dxh, axis=-1, keepdims=True) - xh * jnp.mean(dxh * xh, axis=-1, keepdims=True))
        du1_ref[...] = du1
        gpw = _bdot(u3, dyp, TN)
        rs = lambda a: jnp.sum(a, axis=0, keepdims=True)
        upd = jnp.concatenate([rs(dyp), rs(du2 * xh), rs(du2), rs(du1), jnp.zeros((4, 512), F32)], axis=0)

        @pl.when(i == 0)
        def _():
            gpw_ref[...] = gpw
            st_ref[...] = upd

        @pl.when(i > 0)
        def _():
            gpw_ref[...] += gpw
            st_ref[...] += upd

    full = lambda s: pl.BlockSpec(s, lambda i: (0, 0))
    blk = pl.BlockSpec((tm, 512), lambda i: (i, 0))
    return pl.pallas_call(
        body, name=name, grid=(T // tm,),
        in_specs=[pl.BlockSpec(memory_space=pl.ANY), cur(C_GV), cur(C_GG), prev(C_GV), prev(C_GG), cur(C_ZB), blk,
                  full((CONV_K, 512)), full((1, 512)), full((1, 512)), full((1, 512)), full((512, 512)), full((1, 512))],
        out_specs=[cur(C_ZB), blk, full((512, 512)), full((8, 512))],
        out_shape=[jax.ShapeDtypeStruct(dproj.shape, BF16), jax.ShapeDtypeStruct((T, 512), F32),
                   jax.ShapeDtypeStruct((512, 512), F32), jax.ShapeDtypeStruct((8, 512), F32)],
        scratch_shapes=_conf_scratch(tm),
        input_output_aliases={0: 0},
        compiler_params=_cparams(("arbitrary",)),
    )(dproj, proj, proj, proj, proj, proj, dyb, dw_w, _row(dw_b), _row(ln_g), _row(ln_b), pw2, _row(pw2_b))


def _conf_bwd2(dproj, proj, du1, dw_w, name):
    T = proj.shape[0]
    tm = min(512, T)
    nt = T // tm
    r = tm // HALO_B
    cur, prev = _conf_specs(T, tm)

    def body(dp_any, gv_ref, gg_ref, gvp_ref, ggp_ref, du_ref, dun_ref, w_ref, dglu_ref, gw_ref, ext_ref, sh_ref,
             cv_ref):
        i = pl.program_id(0)
        gv, sg = gv_ref[...], _sig(gg_ref[...])
        ext_ref[:HALO_B] = jnp.where(i > 0, gvp_ref[...] * _sig(ggp_ref[...]), 0.0)
        ext_ref[HALO_B:] = gv * sg
        _shifted_copies(ext_ref, sh_ref)
        rows = _conv_weight_grad(sh_ref, du_ref, [HALO_B - CONV_K + 1 + k for k in range(CONV_K)])
        rows.append(jnp.zeros((1, 512), F32))
        gw = jnp.concatenate(rows, axis=0)
        ext_ref[:tm] = du_ref[...]
        ext_ref[tm:] = jnp.where(i < nt - 1, dun_ref[...], 0.0)
        _shifted_copies(ext_ref, sh_ref)
        _conv_taps(sh_ref, w_ref, [CONV_K - 1 - k for k in range(CONV_K)], cv_ref, 0.0)
        du0 = cv_ref[...]
        dglu_ref[:, :512] = (du0 * sg).astype(BF16)
        dglu_ref[:, 512:] = (du0 * gv * sg * (1.0 - sg)).astype(BF16)

        @pl.when(i == 0)
        def _():
            gw_ref[...] = gw

        @pl.when(i > 0)
        def _():
            gw_ref[...] += gw

    full = lambda s: pl.BlockSpec(s, lambda i: (0, 0))
    return pl.pallas_call(
        body, name=name, grid=(nt,),
        in_specs=[pl.BlockSpec(memory_space=pl.ANY), cur(C_GV), cur(C_GG), prev(C_GV), prev(C_GG),
                  pl.BlockSpec((tm, 512), lambda i: (i, 0)),
                  pl.BlockSpec((HALO_B, 512), lambda i: (jnp.minimum((i + 1) * r, T // HALO_B - 1), 0)),
                  full((CONV_K, 512))],
        out_specs=[pl.BlockSpec((tm, 1024), lambda i: (i, C_GV // 1024)), full((32, 512))],
        out_shape=[jax.ShapeDtypeStruct(dproj.shape, BF16), jax.ShapeDtypeStruct((32, 512), F32)],
        scratch_shapes=_conf_scratch(tm),
        input_output_aliases={0: 0},
        compiler_params=_cparams(("arbitrary",)),
    )(dproj, proj, proj, proj, proj, du1, du1, dw_w)


HALO_C = 8
QKV_C = 1536


def _softplus(x):
    return jnp.maximum(x, 0.0) + jnp.log1p(jnp.exp(-jnp.abs(x)))


def _gdn_conv(i, tm, x_ref, xp_ref, w_ref, ext_ref):
    ext_ref[:HALO_C] = jnp.where(i > 0, xp_ref[...], 0.0)
    ext_ref[HALO_C:] = x_ref[...]
    pre = jnp.zeros((tm, QKV_C), F32)
    for k in range(DN_K):
        pre = pre + w_ref[k:k + 1, :] * ext_ref[pl.ds(HALO_C - DN_K + 1 + k, tm), :]
    return pre


def _gdn_specs(T, tm):
    r = tm // HALO_C
    cur = pl.BlockSpec((tm, QKV_C), lambda i: (i, C_QC // QKV_C))
    prev = pl.BlockSpec((HALO_C, QKV_C), lambda i: (jnp.maximum(i * r - 1, 0), C_QC // QKV_C))
    ab = pl.BlockSpec((tm, 128), lambda i: (i, C_AB // 128))
    return cur, prev, ab


def _gdn_prep_fwd(proj, sconv_w, alog_v, dtb_v, name):
    T = proj.shape[0]
    tm = min(512, T)
    cur, prev, ab = _gdn_specs(T, tm)

    def body(x_ref, xp_ref, ab_ref, w_ref, al_ref, dt_ref, q_ref, k_ref, v_ref, gb_ref, ext_ref):
        i = pl.program_id(0)
        y = _silu(_gdn_conv(i, tm, x_ref, xp_ref, w_ref, ext_ref))
        for h in range(DN_HEADS):
            sl = slice(128 * h, 128 * h + 128)
            qh, kh = y[:, sl], y[:, 512 + 128 * h:512 + 128 * h + 128]
            q_ref[:, sl] = qh * lax.rsqrt(jnp.sum(qh * qh, axis=-1, keepdims=True) + EPS) * (128 ** -0.5)
            k_ref[:, sl] = kh * lax.rsqrt(jnp.sum(kh * kh, axis=-1, keepdims=True) + EPS)
        v_ref[...] = y[:, 1024:]
        abv = ab_ref[...]
        lane = lax.broadcasted_iota(jnp.int32, (tm, 128), 1)
        g = -jnp.exp(al_ref[...]) * _softplus(abv + dt_ref[...])
        gb_ref[...] = jnp.where(lane < DN_HEADS, g, _sig(abv))

    full = lambda s: pl.BlockSpec(s, lambda i: (0, 0))
    blk = pl.BlockSpec((tm, 512), lambda i: (i, 0))
    return pl.pallas_call(
        body, name=name, grid=(T // tm,),
        in_specs=[cur, prev, ab, full((DN_K, QKV_C)), full((1, 128)), full((1, 128))],
        out_specs=[blk, blk, blk, pl.BlockSpec((tm, 128), lambda i: (i, 0))],
        out_shape=[jax.ShapeDtypeStruct((T, 512), F32)] * 3 + [jax.ShapeDtypeStruct((T, 128), F32)],
        scratch_shapes=[pltpu.VMEM((tm + HALO_C, QKV_C), F32)],
        compiler_params=_cparams(("parallel",)),
    )(proj, proj, proj, sconv_w, alog_v, dtb_v)


def _hdot(a, b, dims=NN):
    return _dot(a, b, dims, precision=HI)


def _lockstep(gens):
    results, live = [None] * len(gens), list(range(len(gens)))
    while live:
        for i in list(live):
            try:
                next(gens[i])
            except StopIteration as stop:
                results[i] = stop.value
                live.remove(i)
    return results


def _split(a):
    hi = a.astype(BF16)
    return hi, (a - hi.astype(F32)).astype(BF16)


def _dot_exact(a, b, dims=NN, split_left=True):
    x = (a if split_left else b).astype(F32)
    hi = x.astype(BF16)
    r = x - hi.astype(F32)
    mid = r.astype(BF16)
    lo = (r - mid.astype(F32)).astype(BF16)
    other = (b if split_left else a).astype(BF16)
    one = (lambda p: _dot(p, other, dims)) if split_left else (lambda p: _dot(other, p, dims))
    return (one(lo) + one(mid)) + one(hi)


def _dot3(a, b):
    (ah, al), (bh, bl) = a, b
    return _dot(ah, bh) + (_dot(ah, bl) + _dot(al, bh))


def _tri_inv(mats, eye):
    ps = [-a for a in mats]
    ts = [eye + p for p in ps]
    for _ in range(5):
        sp = [_split(p) for p in ps]
        ps = [_dot3(s, s) for s in sp]
        sp = [_split(p) for p in ps]
        ts = [t + _dot3(_split(t), s) for t, s in zip(ts, sp)]
    return ts


def _tri_consts():
    ii = lax.broadcasted_iota(jnp.int32, (CHUNK, CHUNK), 0)
    jj = lax.broadcasted_iota(jnp.int32, (CHUNK, CHUNK), 1)
    return ii >= jj, ii > jj, (ii == jj).astype(F32)


def _gdn_local(q, k, v, gcol, grow, bcol, lower, strict):
    dm = jnp.where(lower, jnp.exp(jnp.where(lower, gcol - grow, 0.0)), 0.0)
    kb = k * bcol
    a = jnp.where(strict, _bdot(kb, k, NT) * dm, 0.0)
    gc = jnp.exp(gcol)
    glast = grow[:, CHUNK - 1:CHUNK]
    return dict(q=q, k=k, v=v, bcol=bcol, gcol=gcol, glast=glast, dm=dm, kb=kb, a=a, gc=gc, vb=v * bcol,
                kbg=kb * gc, p=_bdot(q, k, NT) * dm, qe=q * gc, ke=k * jnp.exp(glast - gcol))


def _gdn_chunk_bwd(c, do, dvn, ds_new, lower, strict, ones):
    rs = lambda m: jnp.sum(m, axis=-1, keepdims=True)
    colsum = lambda m: _dot_exact(m, ones, TN)[:, :1]
    q, k, v, bcol, dm, tm, gc, s = c["q"], c["k"], c["v"], c["bcol"], c["dm"], c["tm"], c["gc"], c["s"]
    eg = jnp.exp(c["glast"])
    dqe = _bdot(do, s, NT)
    dp = jnp.where(lower, _bdot(do, c["vn"], NT), 0.0)
    dw = -_bdot(dvn, s, NT)
    dke = _bdot(c["vn"], ds_new, NT)
    dvb = _bdot(tm, dvn, TN)
    yield
    dglast = jnp.sum(rs(ds_new * s), axis=0, keepdims=True) * eg
    dk = dke * jnp.exp(c["glast"] - c["gcol"])
    r_ke = rs(dke * c["ke"])
    dglast = dglast + jnp.sum(r_ke, axis=0, keepdims=True)
    dgam = rs(dqe * c["qe"]) - r_ke
    dq = dqe * gc
    dpm = dp * dm
    mp = dp * c["p"]
    dq = dq + _bdot(dpm, k)
    dk = dk + _bdot(dpm, q, TN)
    dt = _bdot(dvn, c["vb"], NT) + _bdot(dw, c["kbg"], NT)
    dkbg = _bdot(tm, dw, TN)
    dgam = dgam + rs(mp) - colsum(mp)
    yield
    dkb = dkbg * gc
    dgam = dgam + rs(dkbg * c["kbg"])
    dat = _bdot(tm, dt, TN)
    yield
    da = jnp.where(strict, -_bdot(dat, tm, NT), 0.0)
    yield
    dam = da * dm
    ma = da * c["a"]
    dkb = dkb + _bdot(dam, k)
    dk = dk + _bdot(dam, c["kb"], TN)
    dgam = dgam + rs(ma) - colsum(ma)
    yield
    dk = dk + dkb * bcol
    dbeta = rs(dkb * k) + rs(dvb * v)
    dv = dvb * bcol
    row = lax.broadcasted_iota(jnp.int32, (CHUNK, 1), 0)
    dgam = dgam + jnp.where(row == CHUNK - 1, dglast, 0.0)
    dg = _dot_exact(lower, dgam, TN, split_left=False)
    return dq, dk, dv, dg, dbeta


SCAN_GROUP = 4
GROUP = 4


def _chunk_decay(gb_ref, gt_ref, lmat, g):
    rows = slice(CHUNK * g, CHUNK * g + CHUNK)
    return rows, _dot_exact(lmat, gb_ref[rows, :], split_left=False), _dot_exact(gt_ref[g], lmat, NT)


def _gdn_chunk_fwd(qd, kd, vd, gb, gbt, name):
    T = qd.shape[0]
    G = GROUP
    ng = T // (CHUNK * G)

    def body(q_ref, k_ref, v_ref, gb_ref, gt_ref, u_ref, w_ref, qe_ref, ke_ref, p_ref, t_ref, eg_ref):
        lower, strict, eye = _tri_consts()
        lmat = lower.astype(F32)
        decay = [_chunk_decay(gb_ref, gt_ref, lmat, g) for g in range(G)]
        chains = [(g, h) for g in range(G) for h in range(DN_HEADS)]
        cs = []
        for g, h in chains:
            rows, gcs, grs = decay[g]
            sl = slice(128 * h, 128 * h + 128)
            c = _gdn_local(q_ref[rows, sl], k_ref[rows, sl], v_ref[rows, sl], gcs[:, h:h + 1], grs[h:h + 1, :],
                           gb_ref[rows, DN_HEADS + h:DN_HEADS + h + 1], lower, strict)
            qe_ref[rows, sl] = c["qe"].astype(BF16)
            ke_ref[rows, sl] = c["ke"].astype(BF16)
            p_ref[rows, 64 * h:64 * h + 64] = c["p"].astype(BF16)
            eg_ref[g, h:h + 1, :] = jnp.broadcast_to(jnp.exp(c["glast"]), (1, 128))
            cs.append(c)
        tms = [t.astype(BF16) for t in _tri_inv([c["a"] for c in cs], eye)]
        us = [_dot(t, c["vb"].astype(BF16)) for t, c in zip(tms, cs)]
        ws = [_dot(t, c["kbg"].astype(BF16)) for t, c in zip(tms, cs)]
        for (g, h), tm, u, w in zip(chains, tms, us, ws):
            rows, sl = decay[g][0], slice(128 * h, 128 * h + 128)
            u_ref[rows, sl] = u
            w_ref[rows, sl] = w.astype(BF16)
            t_ref[rows, 64 * h:64 * h + 64] = tm
        for g in range(G):
            eg_ref[g, DN_HEADS:, :] = jnp.zeros((8 - DN_HEADS, 128), F32)

    blk = pl.BlockSpec((CHUNK * G, 512), lambda n: (n, 0))
    half = pl.BlockSpec((CHUNK * G, 256), lambda n: (n, 0))
    return pl.pallas_call(
        body, name=name, grid=(ng,),
        in_specs=[blk, blk, blk, pl.BlockSpec((CHUNK * G, 128), lambda n: (n, 0)),
                  pl.BlockSpec((G, 8, CHUNK), lambda n: (n, 0, 0))],
        out_specs=[blk, blk, blk, blk, half, half, pl.BlockSpec((G, 8, 128), lambda n: (n, 0, 0))],
        out_shape=[jax.ShapeDtypeStruct((T, 512), F32)] + [jax.ShapeDtypeStruct((T, 512), BF16)] * 3
        + [jax.ShapeDtypeStruct((T, 256), BF16)] * 2 + [jax.ShapeDtypeStruct((T // CHUNK, 8, 128), F32)],
        compiler_params=_cparams(("parallel",)),
    )(qd, kd, vd, gb, gbt)


def _gdn_scan_fwd(u, w, qe, ke, pm, eg, proj, dn_g, name):
    T = u.shape[0]
    nc = T // CHUNK
    G = SCAN_GROUP

    def body(u_ref, w_ref, qe_ref, ke_ref, p_ref, eg_ref, z_ref, ng_ref, y_ref, o_ref, vn_ref, ss_ref, s_ref):
        n = pl.program_id(0)

        @pl.when(n == 0)
        def _():
            s_ref[...] = jnp.zeros_like(s_ref)

        def head(j, h):
            rows, sl = slice(CHUNK * j, CHUNK * j + CHUNK), slice(128 * h, 128 * h + 128)
            s = s_ref[h]
            sb = s.astype(BF16)
            vn = u_ref[rows, sl] - _dot(w_ref[rows, sl], sb)
            qs = _dot(qe_ref[rows, sl], sb)
            yield
            vb = vn.astype(BF16)
            o = qs + _dot(p_ref[rows, 64 * h:64 * h + 64], vb)
            s_ref[h] = s * eg_ref[j, h:h + 1, :] + _dot(ke_ref[rows, sl], vb, TN)
            yield
            vn_ref[rows, sl] = vb
            o_ref[rows, sl] = o
            y_ref[rows, sl] = _rms(o, None)[0] * ng_ref[...] * _silu(z_ref[rows, sl])

        for j in range(G):
            ss_ref[j] = s_ref[...]
            _lockstep([head(j, h) for h in range(DN_HEADS)])

    blk = pl.BlockSpec((CHUNK * G, 512), lambda n: (n, 0))
    return pl.pallas_call(
        body, name=name, grid=(nc // G,),
        in_specs=[blk, blk, blk, blk, pl.BlockSpec((CHUNK * G, 256), lambda n: (n, 0)),
                  pl.BlockSpec((G, 8, 128), lambda n: (n, 0, 0)),
                  pl.BlockSpec((CHUNK * G, 512), lambda n: (n, C_ZC // 512)), pl.BlockSpec((1, 128), lambda n: (0, 0))],
        out_specs=[blk, blk, blk, pl.BlockSpec((G, DN_HEADS, 128, 128), lambda n: (n, 0, 0, 0))],
        out_shape=[jax.ShapeDtypeStruct((T, 512), F32), jax.ShapeDtypeStruct((T, 512), F32),
                   jax.ShapeDtypeStruct((T, 512), BF16), jax.ShapeDtypeStruct((nc, DN_HEADS, 128, 128), F32)],
        scratch_shapes=[pltpu.VMEM((DN_HEADS, 128, 128), F32)],
        compiler_params=_cparams(("arbitrary",)),
    )(u, w, qe, ke, pm, eg, proj, dn_g)


def _gdn_scan_bwd(dproj, w, qe, ke, pm, eg, o, proj, dyc, dn_g, name):
    T = o.shape[0]
    nc = T // CHUNK
    G = SCAN_GROUP
    rev = lambda n: nc // G - 1 - n

    def body(dp_any, w_ref, qe_ref, ke_ref, p_ref, eg_ref, o_ref, z_ref, dy_ref, ng_ref,
             dz_ref, do_ref, dvn_ref, dsn_ref, gng_ref, ds_ref):
        n = pl.program_id(0)

        @pl.when(n == 0)
        def _():
            ds_ref[...] = jnp.zeros_like(ds_ref)
            gng_ref[...] = jnp.zeros_like(gng_ref)

        def head(j, h):
            rows, sl = slice(CHUNK * j, CHUNK * j + CHUNK), slice(128 * h, 128 * h + 128)
            oh, r = _rms(o_ref[rows, sl], None)
            z, dy = z_ref[rows, sl], dy_ref[rows, sl]
            dz_ref[rows, sl] = (dy * (oh * ng_ref[...]) * _dsilu(z)).astype(BF16)
            do, gg = _rms_bwd(dy * _silu(z), oh, r, ng_ref[...])
            dob = do.astype(BF16)
            ds = ds_ref[h]
            dvn = _dot(p_ref[rows, 64 * h:64 * h + 64], dob, TN) + _dot(ke_ref[rows, sl], ds.astype(BF16))
            qd = _dot(qe_ref[rows, sl], dob, TN)
            yield
            dvb = dvn.astype(BF16)
            ds_ref[h] = qd + eg_ref[j, h:h + 1, :] * ds - _dot(w_ref[rows, sl], dvb, TN)
            do_ref[rows, sl] = dob
            dvn_ref[rows, sl] = dvb
            return jnp.sum(gg, axis=0, keepdims=True)

        for j in reversed(range(G)):
            dsn_ref[j] = ds_ref[...]
            gng = _lockstep([head(j, h) for h in range(DN_HEADS)])
            gng_ref[...] += (gng[0] + gng[1]) + (gng[2] + gng[3])

    blk = pl.BlockSpec((CHUNK * G, 512), lambda n: (rev(n), 0))
    state = pl.BlockSpec((G, DN_HEADS, 128, 128), lambda n: (rev(n), 0, 0, 0))
    return pl.pallas_call(
        body, name=name, grid=(nc // G,),
        in_specs=[pl.BlockSpec(memory_space=pl.ANY), blk, blk, blk,
                  pl.BlockSpec((CHUNK * G, 256), lambda n: (rev(n), 0)),
                  pl.BlockSpec((G, 8, 128), lambda n: (rev(n), 0, 0)), blk,
                  pl.BlockSpec((CHUNK * G, 512), lambda n: (rev(n), C_ZC // 512)), blk,
                  pl.BlockSpec((1, 128), lambda n: (0, 0))],
        out_specs=[pl.BlockSpec((CHUNK * G, 512), lambda n: (rev(n), C_ZC // 512)), blk, blk, state,
                   pl.BlockSpec((1, 128), lambda n: (0, 0))],
        out_shape=[jax.ShapeDtypeStruct(dproj.shape, BF16), jax.ShapeDtypeStruct((T, 512), BF16),
                   jax.ShapeDtypeStruct((T, 512), BF16), jax.ShapeDtypeStruct((nc, DN_HEADS, 128, 128), F32),
                   jax.ShapeDtypeStruct((1, 128), F32)],
        scratch_shapes=[pltpu.VMEM((DN_HEADS, 128, 128), F32)],
        input_output_aliases={0: 0},
        compiler_params=_cparams(("arbitrary",)),
    )(dproj, w, qe, ke, pm, eg, o, proj, dyc, dn_g)


def _gdn_chunk_grad(qd, kd, vd, gb, gbt, tmi, ssave, dsn, do, dvn, vn, name):
    T = qd.shape[0]
    G = GROUP
    ng = T // (CHUNK * G)

    def body(q_ref, k_ref, v_ref, gb_ref, gt_ref, t_ref, ss_ref, dsn_ref, do_ref, dvn_ref, vn_ref,
             dq_ref, dk_ref, dv_ref, dgb_ref):
        lower, strict, _ = _tri_consts()
        lmat = lower.astype(F32)
        ones = jnp.ones((CHUNK, 128), F32)
        lane = lax.broadcasted_iota(jnp.int32, (CHUNK, 128), 1)
        decay = [_chunk_decay(gb_ref, gt_ref, lmat, g) for g in range(G)]
        chains = [(g, h) for g in range(G) for h in range(DN_HEADS)]
        gens = []
        for g, h in chains:
            rows, gcs, grs = decay[g]
            sl = slice(128 * h, 128 * h + 128)
            c = _gdn_local(q_ref[rows, sl], k_ref[rows, sl], v_ref[rows, sl], gcs[:, h:h + 1], grs[h:h + 1, :],
                           gb_ref[rows, DN_HEADS + h:DN_HEADS + h + 1], lower, strict)
            c.update(tm=t_ref[rows, 64 * h:64 * h + 64], s=ss_ref[g, h], vn=vn_ref[rows, sl])
            gens.append(_gdn_chunk_bwd(c, do_ref[rows, sl], dvn_ref[rows, sl], dsn_ref[g, h], lower, strict, ones))
        dgb = [jnp.zeros((CHUNK, 128), F32) for _ in range(G)]
        for (g, h), (dq, dk, dv, dg, dbeta) in zip(chains, _lockstep(gens)):
            rows, sl = decay[g][0], slice(128 * h, 128 * h + 128)
            dq_ref[rows, sl], dk_ref[rows, sl], dv_ref[rows, sl] = dq, dk, dv
            dgb[g] = dgb[g] + jnp.where(lane == h, dg, 0.0) + jnp.where(lane == DN_HEADS + h, dbeta, 0.0)
        for g in range(G):
            dgb_ref[decay[g][0], :] = dgb[g]

    blk = pl.BlockSpec((CHUNK * G, 512), lambda n: (n, 0))
    half = pl.BlockSpec((CHUNK * G, 256), lambda n: (n, 0))
    nar = pl.BlockSpec((CHUNK * G, 128), lambda n: (n, 0))
    state = pl.BlockSpec((G, DN_HEADS, 128, 128), lambda n: (n, 0, 0, 0))
    return pl.pallas_call(
        body, name=name, grid=(ng,),
        in_specs=[blk, blk, blk, nar, pl.BlockSpec((G, 8, CHUNK), lambda n: (n, 0, 0)), half, state, state,
                  blk, blk, blk],
        out_specs=[blk, blk, blk, nar],
        out_shape=[jax.ShapeDtypeStruct((T, 512), F32)] * 3 + [jax.ShapeDtypeStruct((T, 128), F32)],
        compiler_params=_cparams(("parallel",)),
    )(qd, kd, vd, gb, gbt, tmi, ssave, dsn, do, dvn, vn)


def _gdn_prep_bwd1(dproj, proj, dqd, dkd, dvd, dgb, dkv_a, sconv_w, alog_v, dtb_v, name):
    T = proj.shape[0]
    tm = min(512, T)
    cur, prev, ab = _gdn_specs(T, tm)

    def body(dp_any, x_ref, xp_ref, ab_ref, dq_ref, dk_ref, dv_ref, dgb_ref, dkv_ref, w_ref, al_ref, dt_ref,
             o_ref, dpre_ref, st_ref, ext_ref):
        i = pl.program_id(0)
        pre = _gdn_conv(i, tm, x_ref, xp_ref, w_ref, ext_ref)
        y, dsl = _silu(pre), _dsilu(pre)
        for h in range(DN_HEADS):
            for base, g_ref, scale in ((0, dq_ref, 128 ** -0.5), (512, dk_ref, 1.0)):
                sl = slice(base + 128 * h, base + 128 * h + 128)
                xh = y[:, sl]
                r = lax.rsqrt(jnp.sum(xh * xh, axis=-1, keepdims=True) + EPS)
                xn = xh * r
                gy = g_ref[:, 128 * h:128 * h + 128]
                dpre_ref[:, sl] = (scale * r) * (gy - xn * jnp.sum(gy * xn, axis=-1, keepdims=True)) * dsl[:, sl]
        dpre_ref[:, 1024:] = dv_ref[...] * dsl[:, 1024:]
        abv, dgb = ab_ref[...], dgb_ref[...]
        lane = lax.broadcasted_iota(jnp.int32, (tm, 128), 1)
        na = -jnp.exp(al_ref[...])
        xs = abv + dt_ref[...]
        da = dgb * na * _sig(xs)
        b = _sig(abv)
        o_ref[:, :256] = dkv_ref[...].astype(BF16)
        o_ref[:, 256:] = jnp.where(lane < DN_HEADS, da,
                                   jnp.where(lane < 2 * DN_HEADS, dgb * b * (1.0 - b), 0.0)).astype(BF16)
        head = lane < DN_HEADS
        upd = jnp.concatenate([jnp.sum(jnp.where(head, dgb * na * _softplus(xs), 0.0), axis=0, keepdims=True),
                               jnp.sum(jnp.where(head, da, 0.0), axis=0, keepdims=True), jnp.zeros((6, 128), F32)],
                              axis=0)

        @pl.when(i == 0)
        def _():
            st_ref[...] = upd

        @pl.when(i > 0)
        def _():
            st_ref[...] += upd

    full = lambda s: pl.BlockSpec(s, lambda i: (0, 0))
    blk = pl.BlockSpec((tm, 512), lambda i: (i, 0))
    return pl.pallas_call(
        body, name=name, grid=(T // tm,),
        in_specs=[pl.BlockSpec(memory_space=pl.ANY), cur, prev, ab, blk, blk, blk,
                  pl.BlockSpec((tm, 128), lambda i: (i, 0)), pl.BlockSpec((tm, 256), lambda i: (i, 0)),
                  full((DN_K, QKV_C)), full((1, 128)), full((1, 128))],
        out_specs=[pl.BlockSpec((tm, 384), lambda i: (i, C_KA // 384)),
                   pl.BlockSpec((tm, QKV_C), lambda i: (i, 0)), full((8, 128))],
        out_shape=[jax.ShapeDtypeStruct(dproj.shape, BF16), jax.ShapeDtypeStruct((T, QKV_C), F32),
                   jax.ShapeDtypeStruct((8, 128), F32)],
        scratch_shapes=[pltpu.VMEM((tm + HALO_C, QKV_C), F32)],
        input_output_aliases={0: 0},
        compiler_params=_cparams(("arbitrary",)),
    )(dproj, proj, proj, proj, dqd, dkd, dvd, dgb, dkv_a, sconv_w, alog_v, dtb_v)


def _gdn_prep_bwd2(dproj, proj, dpre, sconv_w, name):
    T = proj.shape[0]
    tm = min(512, T)
    nt = T // tm
    r = tm // HALO_C
    cur, prev, _ = _gdn_specs(T, tm)

    def body(dp_any, x_ref, xp_ref, d_ref, dn_ref, w_ref, dx_ref, gw_ref, extx_ref, extd_ref):
        i = pl.program_id(0)
        extx_ref[:HALO_C] = jnp.where(i > 0, xp_ref[...], 0.0)
        extx_ref[HALO_C:] = x_ref[...]
        d = d_ref[...]
        extd_ref[:tm] = d
        extd_ref[tm:] = jnp.where(i < nt - 1, dn_ref[...], 0.0)
        dx = jnp.zeros((tm, QKV_C), F32)
        rows = []
        for k in range(DN_K):
            dx = dx + w_ref[k:k + 1, :] * extd_ref[pl.ds(DN_K - 1 - k, tm), :]
            rows.append(jnp.sum(d * extx_ref[pl.ds(HALO_C - DN_K + 1 + k, tm), :], axis=0, keepdims=True))
        rows.append(jnp.zeros((8 - DN_K, QKV_C), F32))
        gw = jnp.concatenate(rows, axis=0)
        dx_ref[...] = dx.astype(BF16)

        @pl.when(i == 0)
        def _():
            gw_ref[...] = gw

        @pl.when(i > 0)
        def _():
            gw_ref[...] += gw

    full = lambda s: pl.BlockSpec(s, lambda i: (0, 0))
    return pl.pallas_call(
        body, name=name, grid=(nt,),
        in_specs=[pl.BlockSpec(memory_space=pl.ANY), cur, prev, pl.BlockSpec((tm, QKV_C), lambda i: (i, 0)),
                  pl.BlockSpec((HALO_C, QKV_C), lambda i: (jnp.minimum((i + 1) * r, T // HALO_C - 1), 0)),
                  full((DN_K, QKV_C))],
        out_specs=[cur, full((8, QKV_C))],
        out_shape=[jax.ShapeDtypeStruct(dproj.shape, BF16), jax.ShapeDtypeStruct((8, QKV_C), F32)],
        scratch_shapes=[pltpu.VMEM((tm + HALO_C, QKV_C), F32), pltpu.VMEM((tm + HALO_C, QKV_C), F32)],
        input_output_aliases={0: 0},
        compiler_params=_cparams(("arbitrary",)),
    )(dproj, proj, proj, dpre, dpre, sconv_w)


def _merge_fwd(x, proj, ya, yb, yc, wa, wb, wc, wo, gate, name):
    T = x.shape[0]
    tm = min(256, T)

    def body(x_ref, mg_ref, ya_ref, yb_ref, yc_ref, wa_ref, wb_ref, wc_ref, wo_ref, gate_ref, o_ref):
        merged = (_sig(mg_ref[:, :D]) * _bdot(ya_ref[...], wa_ref[...])
                  + _sig(mg_ref[:, D:2 * D]) * _bdot(yb_ref[...], wb_ref[...])
                  + _sig(mg_ref[:, 2 * D:]) * _bdot(yc_ref[...], wc_ref[...]))
        o_ref[...] = x_ref[...] + gate_ref[...] * _bdot(merged, wo_ref[...])

    full = lambda s: pl.BlockSpec(s, lambda i: (0, 0))
    yb_ = pl.BlockSpec((tm, 512), lambda i: (i, 0))
    return pl.pallas_call(
        body, name=name, grid=(T // tm,),
        in_specs=[pl.BlockSpec((tm, D), lambda i: (i, 0)), pl.BlockSpec((tm, 3 * D), lambda i: (i, 0)), yb_, yb_, yb_,
                  full((512, D)), full((512, D)), full((512, D)), full((D, D)), full((1, D))],
        out_specs=pl.BlockSpec((tm, D), lambda i: (i, 0)),
        out_shape=jax.ShapeDtypeStruct((T, D), F32),
        compiler_params=_cparams(("parallel",)),
    )(x, proj, ya, yb, yc, wa, wb, wc, wo, _row(gate))


def _merge_bwd(dout, proj, ya, yb, yc, wa, wb, wc, wo, gate, name):
    T = dout.shape[0]
    tm = min(256, T)
    nt = T // tm

    def body(do_ref, mg_ref, ya_ref, yb_ref, yc_ref, wa_ref, wb_ref, wc_ref, wo_ref, gate_ref,
             dmg_ref, dya_ref, dyb_ref, dyc_ref, gwa_hbm, gwb_hbm, gwc_hbm, gwo_hbm, gg_ref,
             gwa_ref, gwb_ref, gwc_ref, gwo_ref):
        i = pl.program_id(0)

        @pl.when(i == 0)
        def _():
            for r in (gwa_ref, gwb_ref, gwc_ref, gwo_ref, gg_ref):
                r[...] = jnp.zeros_like(r)

        ys = (ya_ref[...], yb_ref[...], yc_ref[...])
        ws = (wa_ref, wb_ref, wc_ref)
        gs = tuple(_sig(mg_ref[:, j * D:(j + 1) * D]) for j in range(3))
        ps = tuple(_bdot(ys[j], ws[j][...]) for j in range(3))
        merged = gs[0] * ps[0] + gs[1] * ps[1] + gs[2] * ps[2]
        mo = _bdot(merged, wo_ref[...])
        do = do_ref[...]
        gg_ref[...] += jnp.sum(do * mo, axis=0, keepdims=True)
        dmo = do * gate_ref[...]
        dmerged = _bdot(dmo, wo_ref[...], NT)
        gwo_ref[...] += _bdot(merged, dmo, TN)
        for j, (dy_ref, gw_ref) in enumerate(((dya_ref, gwa_ref), (dyb_ref, gwb_ref), (dyc_ref, gwc_ref))):
            dp = dmerged * gs[j]
            dmg_ref[:, j * D:(j + 1) * D] = (dmerged * ps[j] * gs[j] * (1.0 - gs[j])).astype(BF16)
            dy_ref[...] = _bdot(dp, ws[j][...], NT)
            gw_ref[...] += _bdot(ys[j], dp, TN)

        @pl.when(i == nt - 1)
        def _():
            for src, dst in ((gwa_ref, gwa_hbm), (gwb_ref, gwb_hbm), (gwc_ref, gwc_hbm), (gwo_ref, gwo_hbm)):
                pltpu.sync_copy(src, dst)

    full = lambda s: pl.BlockSpec(s, lambda i: (0, 0))
    yb_ = pl.BlockSpec((tm, 512), lambda i: (i, 0))
    anyspec = pl.BlockSpec(memory_space=pl.ANY)
    return pl.pallas_call(
        body, name=name, grid=(nt,),
        in_specs=[pl.BlockSpec((tm, D), lambda i: (i, 0)), pl.BlockSpec((tm, 3 * D), lambda i: (i, 0)), yb_, yb_, yb_,
                  full((512, D)), full((512, D)), full((512, D)), full((D, D)), full((1, D))],
        out_specs=[pl.BlockSpec((tm, 3 * D), lambda i: (i, 0)), yb_, yb_, yb_, anyspec, anyspec, anyspec, anyspec,
                   full((1, D))],
        out_shape=[jax.ShapeDtypeStruct((T, NP), BF16)] + [jax.ShapeDtypeStruct((T, 512), F32)] * 3
        + [jax.ShapeDtypeStruct((512, D), F32)] * 3 + [jax.ShapeDtypeStruct((D, D), F32), jax.ShapeDtypeStruct((1, D), F32)],
        scratch_shapes=[pltpu.VMEM((512, D), F32)] * 3 + [pltpu.VMEM((D, D), F32)],
        compiler_params=_cparams(("arbitrary",)),
    )(dout, proj, ya, yb, yc, wa, wb, wc, wo, _row(gate))


def _loss_head(y, tgt, name):
    T = y.shape[0]
    tm = min(512, T)

    def body(y_ref, t_ref, dy_ref, l_ref):
        i = pl.program_id(0)
        diff = y_ref[...] - t_ref[...]
        dy_ref[...] = diff * (1.0 / D)
        part = jnp.sum(diff * diff, axis=0, keepdims=True)

        @pl.when(i == 0)
        def _():
            l_ref[...] = part

        @pl.when(i > 0)
        def _():
            l_ref[...] += part

    blk = pl.BlockSpec((tm, D), lambda i: (i, 0))
    return pl.pallas_call(
        body, name=name, grid=(T // tm,), in_specs=[blk, blk],
        out_specs=[blk, pl.BlockSpec((1, D), lambda i: (0, 0))],
        out_shape=[jax.ShapeDtypeStruct((T, D), F32), jax.ShapeDtypeStruct((1, D), F32)],
        compiler_params=_cparams(("arbitrary",)),
    )(y, tgt)


def _ada_fwd(c_all, w_ada, b_my, name):
    def body(c_ref, w_ref, b_ref, o_ref):
        sc = _silu(c_ref[...])
        for l in range(DEPTH):
            o_ref[l] = _bdot(sc, w_ref[l]) + b_ref[l:l + 1, :]

    return pl.pallas_call(body, name=name, out_shape=jax.ShapeDtypeStruct((DEPTH, N_DEV, w_ada.shape[2]), F32),
                          compiler_params=_cparams())(c_all, w_ada, b_my)


def _ada_bwd(c_all, dmod_my, name):
    def body(c_ref, d_ref, o_ref):
        sc = _silu(c_ref[...])
        for l in range(DEPTH):
            o_ref[l] = _bdot(sc, d_ref[l], TN)

    return pl.pallas_call(body, name=name, out_shape=jax.ShapeDtypeStruct((DEPTH, D, dmod_my.shape[2]), F32),
                          compiler_params=_cparams())(c_all, dmod_my)


def _adam_math(w, g, m, v):
    m = ADAM_B1 * m + (1.0 - ADAM_B1) * g
    v = ADAM_B2 * v + (1.0 - ADAM_B2) * (g * g)
    m_hat = m / (1.0 - ADAM_B1 ** ADAM_STEP)
    v_hat = v / (1.0 - ADAM_B2 ** ADAM_STEP)
    return -ADAM_LR * (m_hat / (jnp.sqrt(v_hat) + ADAM_EPS) + ADAM_WD * w), m, v


def _row_tile(rows, cap):
    best = rows
    for t in range(8, min(rows, cap) + 1, 8):
        if rows % t == 0:
            best = t
    return best if best <= cap else rows


def _adamw(w, g, m, v, name):
    R, C = w.shape
    tr = _row_tile(R, 256)

    def body(w_ref, g_ref, m_ref, v_ref, d_ref, mo_ref, vo_ref):
        d_ref[...], mo_ref[...], vo_ref[...] = _adam_math(w_ref[...], g_ref[...], m_ref[...], v_ref[...])

    blk = pl.BlockSpec((tr, C), lambda i: (i, 0))
    return pl.pallas_call(body, name=name, grid=(R // tr,), in_specs=[blk] * 4, out_specs=[blk] * 3,
                          out_shape=[jax.ShapeDtypeStruct((R, C), F32)] * 3,
                          compiler_params=_cparams(("parallel",)))(w, g, m, v)


def _sum_adamw_many(parts, ws, ms, vs, name):
    n = len(ws)

    def body(*refs):
        ins, outs = refs[:4 * n], refs[4 * n:]
        for i in range(n):
            g = ins[i][0]
            for j in range(1, N_DEV):
                g = g + ins[i][j]
            d, m, v = _adam_math(ins[n + i][...], g, ins[2 * n + i][...], ins[3 * n + i][...])
            outs[i][...], outs[n + i][...], outs[2 * n + i][...], outs[3 * n + i][...] = g, d, m, v

    shapes = [jax.ShapeDtypeStruct(w.shape, F32) for w in ws]
    out = pl.pallas_call(body, name=name, out_shape=shapes * 4, compiler_params=_cparams())(*parts, *ws, *ms, *vs)
    return out[:n], out[n:2 * n], out[2 * n:3 * n], out[3 * n:]


def _sum_adamw(parts0, parts1, w, m, v, name):
    P, R, C = parts0.shape
    tr = _row_tile(R, 128)
    nt = R // tr

    def body(p0_ref, p1_ref, w_ref, m_ref, v_ref, g_ref, d_ref, mo_ref, vo_ref):
        def emit(p_ref):
            g = p_ref[0].astype(F32)
            for j in range(1, P):
                g = g + p_ref[j].astype(F32)
            g_ref[...] = g
            d_ref[...], mo_ref[...], vo_ref[...] = _adam_math(w_ref[...], g, m_ref[...], v_ref[...])

        @pl.when(pl.program_id(0) == 0)
        def _():
            emit(p0_ref)

        @pl.when(pl.program_id(0) == 1)
        def _():
            emit(p1_ref)

    blk = pl.BlockSpec((tr, C), lambda l, i: (l * nt + i, 0))
    return pl.pallas_call(
        body, name=name, grid=(DEPTH, nt),
        in_specs=[pl.BlockSpec((P, tr, C), lambda l, i: (0, i * (1 - l) + (nt - 1) * l, 0)),
                  pl.BlockSpec((P, tr, C), lambda l, i: (0, i * l, 0)), blk, blk, blk],
        out_specs=[blk] * 4, out_shape=[jax.ShapeDtypeStruct((DEPTH * R, C), F32)] * 4,
        compiler_params=_cparams(("arbitrary", "arbitrary")))(parts0, parts1, w, m, v)


def _pair_sum(core, buf, recv, name):
    _, _, R, C = buf.shape
    tr = _row_tile(R, 128)

    def body(c_ref, a_ref, b_ref, o_ref):
        o_ref[...] = (a_ref[:, 0].astype(F32) + b_ref[...].astype(F32)).astype(BF16)

    return pl.pallas_call(
        body, name=name,
        grid_spec=pltpu.PrefetchScalarGridSpec(
            num_scalar_prefetch=1, grid=(R // tr,),
            in_specs=[pl.BlockSpec((4, 1, tr, C), lambda i, c: (0, c[0], i, 0)),
                      pl.BlockSpec((4, tr, C), lambda i, c: (0, i, 0))],
            out_specs=pl.BlockSpec((4, tr, C), lambda i, c: (0, i, 0))),
        out_shape=jax.ShapeDtypeStruct((4, R, C), BF16),
        compiler_params=_cparams(("parallel",)))(core, buf, recv)


SHARD_IN = D_IN // N_DEV


def _w_in_pieces():
    out, p = [], 0
    for a, b in _PAD_FROM:
        for j in range(N_DEV):
            lo, hi = max(a, SHARD_IN * j), min(b, SHARD_IN * (j + 1))
            if lo < hi:
                out.append((j, lo - SHARD_IN * j, hi - SHARD_IN * j, p + lo - a))
        p += b - a
    return out


def _assemble_w_in(gw, name):
    tr = 256
    nt = D // tr

    def body(x_ref, o_ref):
        for j, s0, s1, d0 in _w_in_pieces():
            o_ref[:, d0:d0 + s1 - s0] = x_ref[j, :, s0:s1]
        o_ref[:, D_IN:] = jnp.zeros((tr, NP - D_IN), gw.dtype)

    return pl.pallas_call(
        body, name=name, grid=(nt,),
        in_specs=[pl.BlockSpec((N_DEV, tr, SHARD_IN), lambda i: (0, i, 0))],
        out_specs=pl.BlockSpec((tr, NP), lambda i: (i, 0)),
        out_shape=jax.ShapeDtypeStruct((D, NP), gw.dtype),
        compiler_params=_cparams(("parallel",)))(gw)


def _split_w_in_grad(g, name):
    tr = 256

    def body(g_ref, o_ref):
        for j, s0, s1, d0 in _w_in_pieces():
            o_ref[j, :, s0:s1] = g_ref[:, d0:d0 + s1 - s0].astype(BF16)

    return pl.pallas_call(
        body, name=name, grid=(D // tr,),
        in_specs=[pl.BlockSpec((tr, NP), lambda i: (i, 0))],
        out_specs=pl.BlockSpec((N_DEV, tr, SHARD_IN), lambda i: (0, i, 0)),
        out_shape=jax.ShapeDtypeStruct((N_DEV, D, SHARD_IN), BF16),
        compiler_params=_cparams(("parallel",)))(g)


def _mesh_pos():
    return lax.axis_index("x"), lax.axis_index("y"), lax.axis_index("c")


def _launch(copies, peers, bufs, out_structs, sems, name, collective_id):
    n = len(bufs)
    if collective_id is None:
        anyspec = pl.BlockSpec(memory_space=pl.ANY)
        return list(pl.pallas_call(
            lambda *refs: copies(refs[:n], refs[n:n + len(out_structs)], *refs[n + len(out_structs):]),
            name=name, in_specs=[anyspec] * n, out_specs=[anyspec] * len(out_structs), out_shape=list(out_structs),
            scratch_shapes=list(sems))(*bufs))
    ins = [jax.new_ref(b, memory_space=pltpu.MemorySpace.HBM) for b in bufs]
    outs = [jax.empty_ref(s, memory_space=pltpu.MemorySpace.HBM) for s in out_structs]

    @pl.kernel(mesh=plsc.ScalarSubcoreMesh(axis_name="sequencer", num_cores=1), name=name, scratch_types=tuple(sems),
               compiler_params=pltpu.CompilerParams(collective_id=collective_id))
    def on_sequencer(*sem_refs):
        barrier = pltpu.get_barrier_semaphore()
        targets = peers()
        for p in targets:
            pl.semaphore_signal(barrier, inc=1, device_id=p, device_id_type=pl.DeviceIdType.MESH)
        pl.semaphore_wait(barrier, len(targets))
        copies(ins, outs, *sem_refs)

    on_sequencer()
    return [r[...] for r in outs]


def _all_gather(blocks, name, collective_id=None):
    n = len(blocks)

    def peers():
        x, y, c = _mesh_pos()
        return [(x, y, 1 - c), (1 - x, y, c), (x, 1 - y, c), (1 - x, 1 - y, c)]

    def copies(ins, outs, send_sems, recv_sems, local_sems):
        x, y, c = _mesh_pos()
        me, sibling = (x, y, c), (x, y, 1 - c)
        chips = [(1 - x, y), (x, 1 - y), (1 - x, 1 - y)]
        idx = lambda p: 4 * p[0] + 2 * p[1] + p[2]

        def copy(a, k, block, to, src=None):
            dst = outs[a].at[idx(block)]
            return pltpu.make_async_remote_copy(
                src_ref=dst if src is None else src, dst_ref=dst, send_sem=send_sems.at[a, k],
                recv_sem=recv_sems.at[a, k], device_id=to, device_id_type=pl.DeviceIdType.MESH)

        mine = [pltpu.make_async_copy(ins[a], outs[a].at[idx(me)], local_sems.at[a]) for a in range(n)]
        for cp in mine:
            cp.start()
        first = []
        for a in range(n):
            first.append(copy(a, 0, me, sibling, src=ins[a]))
            first += [copy(a, 1 + j, me, (*chip, c), src=ins[a]) for j, chip in enumerate(chips)]
        for cp in first:
            cp.start()
        passed = []
        for j, chip in enumerate(chips):
            for a in range(n):
                copy(a, 1 + j, (*chip, c), me).wait_recv()
                cp = copy(a, 4 + j, (*chip, c), sibling)
                cp.start()
                passed.append(cp)
        for a in range(n):
            copy(a, 0, sibling, me).wait_recv()
            for j, chip in enumerate(chips):
                copy(a, 4 + j, (*chip, 1 - c), me).wait_recv()
        for cp in first + passed:
            cp.wait_send()
        for cp in mine:
            cp.wait()

    return _launch(copies, peers, blocks, [jax.ShapeDtypeStruct((N_DEV,) + b.shape, b.dtype) for b in blocks],
                   [pltpu.SemaphoreType.DMA((n, 7)), pltpu.SemaphoreType.DMA((n, 7)), pltpu.SemaphoreType.DMA((n,))],
                   name, collective_id)


def _exchange_core(bufs, name, collective_id=None):
    n = len(bufs)

    def peers():
        x, y, c = _mesh_pos()
        return [(x, y, 1 - c)]

    def copies(ins, outs, send_sems, recv_sems):
        x, y, c = _mesh_pos()
        started = []
        for a in range(n):
            for q in range(4):
                cp = pltpu.make_async_remote_copy(
                    src_ref=ins[a].at[q, 1 - c], dst_ref=outs[a].at[q], send_sem=send_sems.at[a, q],
                    recv_sem=recv_sems.at[a, q], device_id=(x, y, 1 - c), device_id_type=pl.DeviceIdType.MESH)
                cp.start()
                started.append(cp)
        for cp in started:
            cp.wait()

    return _launch(copies, peers, bufs, [jax.ShapeDtypeStruct((4,) + b.shape[2:], b.dtype) for b in bufs],
                   [pltpu.SemaphoreType.DMA((n, 4)), pltpu.SemaphoreType.DMA((n, 4))], name, collective_id)


def _exchange_chips(bufs, name, collective_id=None):
    n = len(bufs)

    def peers():
        x, y, c = _mesh_pos()
        return [(1 - x, y, c), (x, 1 - y, c), (1 - x, 1 - y, c)]

    def copies(ins, outs, send_sems, recv_sems, local_sems):
        x, y, c = _mesh_pos()
        chip = 2 * x + y
        local = [pltpu.make_async_copy(ins[a].at[chip], outs[a].at[chip], local_sems.at[a]) for a in range(n)]
        for cp in local:
            cp.start()
        started = []
        for k in range(1, 4):
            px = 1 - x if k & 2 else x
            py = 1 - y if k & 1 else y
            for a in range(n):
                cp = pltpu.make_async_remote_copy(
                    src_ref=ins[a].at[2 * px + py], dst_ref=outs[a].at[chip], send_sem=send_sems.at[a, k - 1],
                    recv_sem=recv_sems.at[a, k - 1], device_id=(px, py, c), device_id_type=pl.DeviceIdType.MESH)
                cp.start()
                started.append(cp)
        for cp in started:
            cp.wait()
        for cp in local:
            cp.wait()

    return _launch(copies, peers, bufs, [jax.ShapeDtypeStruct(b.shape, b.dtype) for b in bufs],
                   [pltpu.SemaphoreType.DMA((n, 3)), pltpu.SemaphoreType.DMA((n, 3)), pltpu.SemaphoreType.DMA((n,))],
                   name, collective_id)


_SMALL = ("b_ada", "norm_g", "q_norm_g", "k_norm_g", "sinks", "dw_b", "ln_g", "ln_b", "pw2_b", "a_log", "dt_bias",
          "dn_norm_g", "dw_w", "sconv_w")


def _lane4(v):
    return jnp.pad(v, (0, 124)).reshape(1, 128)


def kernel(x, c, w_ada, b_ada, norm_g, w_in, q_norm_g, k_norm_g, sinks, dw_w, dw_b, ln_g, ln_b, pw2_w, pw2_b, sconv_w, a_log, dt_bias, dn_norm_g, w_proj_a, w_proj_b, w_proj_c, w_out, loss_target, m_w_ada, m_b_ada, m_norm_g, m_w_in, m_q_norm_g, m_k_norm_g, m_sinks, m_dw_w, m_dw_b, m_ln_g, m_ln_b, m_pw2_w, m_pw2_b, m_sconv_w, m_a_log, m_dt_bias, m_dn_norm_g, m_w_proj_a, m_w_proj_b, m_w_proj_c, m_w_out, v_w_ada, v_b_ada, v_norm_g, v_w_in, v_q_norm_g, v_k_norm_g, v_sinks, v_dw_w, v_dw_b, v_ln_g, v_ln_b, v_pw2_w, v_pw2_b, v_sconv_w, v_a_log, v_dt_bias, v_dn_norm_g, v_w_proj_a, v_w_proj_b, v_w_proj_c, v_w_out):
    T = x.shape[1]
    nc = T // CHUNK
    xi, yi, ci = _mesh_pos()
    me = 4 * xi + 2 * yi + ci
    big_w = (w_in, pw2_w, w_proj_a, w_proj_b, w_proj_c, w_out)
    big_m = (m_w_in, m_pw2_w, m_w_proj_a, m_w_proj_b, m_w_proj_c, m_w_out)
    big_v = (v_w_in, v_pw2_w, v_w_proj_a, v_w_proj_b, v_w_proj_c, v_w_out)

    ada_cols = w_ada.shape[2]
    dw_cols, sc_cols = dw_w.shape[2], sconv_w.shape[2]
    flat2 = lambda a: a.reshape(-1, a.shape[-1])
    big16 = [[a[l].astype(BF16) for l in range(DEPTH)] for a in big_w]
    c_all, gdw, gsc = _all_gather([c, dw_w, sconv_w], "gather_small", collective_id=0)
    (gw_in0,) = _all_gather([big16[0][0]], "gather_w_in0", collective_id=7)
    c_all = c_all.reshape(N_DEV, D)
    dw_f = gdw.transpose(1, 2, 0, 3).reshape(DEPTH, CONV_K, 512)
    sc_f = gsc.transpose(1, 2, 0, 3).reshape(DEPTH, DN_K, QKV_C)

    b_my = lax.dynamic_slice(b_ada, (0, me * ada_cols), (DEPTH, ada_cols))
    mod_part = _ada_fwd(c_all, w_ada, b_my, "ada_fwd")
    (gmod,) = _all_gather([mod_part.reshape(-1, 128)], "gather_mod")

    rest0 = [a[0] for a in big16[1:]]
    all1 = [a[1] for a in big16]
    (rest0, all1), gmod = lax.optimization_barrier(((rest0, all1), gmod))
    got0 = [gw_in0] + _all_gather(rest0, "gather_rest0", collective_id=1)
    got1 = _all_gather(all1, "gather_weights1", collective_id=6)
    wp, pw2_f, wa_f, wb_f, wc_f, wo_f = [], [], [], [], [], []
    for l, (gw_in, gpw2, gpa, gpb, gpc, gwo) in enumerate((got0, got1)):
        wp.append(_assemble_w_in(gw_in, f"assemble_w_in{l}"))
        pw2_f.append(gpw2.reshape(512, 512))
        for dst, g in ((wa_f, gpa), (wb_f, gpb), (wc_f, gpc)):
            dst.append(g.transpose(1, 0, 2).reshape(512, D))
        wo_f.append(gwo.reshape(D, D))
    mod_all = gmod.reshape(N_DEV, DEPTH, N_DEV, ada_cols).transpose(1, 2, 0, 3).reshape(DEPTH, N_DEV, 3 * D)
    mod = lax.dynamic_index_in_dim(mod_all, me, axis=1, keepdims=False)
    shift, scale, gate = mod[:, :D], mod[:, D:2 * D], mod[:, 2 * D:]

    xs, saved = [x[0]], []
    for l in range(DEPTH):
        xl = xs[-1]
        h = _norm_fwd(xl, norm_g[l], scale[l], shift[l], f"norm_fwd{l}")
        proj = _mm(h, wp[l], tm=min(1024, T), tn=1152, tk=D, name=f"in_proj{l}")
        ya = _attn_fwd(proj, q_norm_g[l], k_norm_g[l], sinks[l], f"attn_fwd{l}")
        yb = _conf_fwd(proj, dw_f[l], dw_b[l], ln_g[l], ln_b[l], pw2_f[l], pw2_b[l], f"conf_fwd{l}")
        alv, dtv, dng = _lane4(a_log[l]), _lane4(dt_bias[l]), _row(dn_norm_g[l])
        qd, kd, vd, gb = _gdn_prep_fwd(proj, sc_f[l], alv, dtv, f"gdn_prep_fwd{l}")
        gbt = gb[:, :8].reshape(nc, CHUNK, 8).transpose(0, 2, 1)
        u, w, qe, ke, pm, tmi, eg = _gdn_chunk_fwd(qd, kd, vd, gb, gbt, f"gdn_chunk_fwd{l}")
        yc, o, vn, ss = _gdn_scan_fwd(u, w, qe, ke, pm, eg, proj, dng, f"gdn_scan_fwd{l}")
        xs.append(_merge_fwd(xl, proj, ya, yb, yc, wa_f[l], wb_f[l], wc_f[l], wo_f[l], gate[l], f"merge_fwd{l}"))
        saved.append((h, proj, ya, yb, yc, qd, kd, vd, gb, gbt, ss, alv, dtv, dng, w, qe, ke, pm, tmi, eg, o, vn))

    dout, lsum = _loss_head(xs[-1], loss_target[0], "loss_head")

    small = {name: [None] * DEPTH for name in _SMALL}
    big_parts = [None] * DEPTH
    core = jnp.reshape(ci, (1,)).astype(jnp.int32)
    for l in reversed(range(DEPTH)):
        h, proj, ya, yb, yc, qd, kd, vd, gb, gbt, ss, alv, dtv, dng, w, qe, ke, pm, tmi, eg, o, vn = saved[l]
        dproj, dya, dyb, dyc, g_wa, g_wb, g_wc, g_wo, g_gate = _merge_bwd(
            dout, proj, ya, yb, yc, wa_f[l], wb_f[l], wc_f[l], wo_f[l], gate[l], f"merge_bwd{l}")
        dproj, dkv_a, g_q, g_k, g_s = _attn_bwd(dproj, proj, dya, q_norm_g[l], k_norm_g[l], sinks[l], f"attn_bwd{l}")
        dproj, du1, g_pw2, st_b = _conf_bwd1(dproj, proj, dyb, dw_f[l], dw_b[l], ln_g[l], ln_b[l], pw2_f[l], pw2_b[l],
                                             f"conf_bwd_a{l}")
        dproj, g_dw = _conf_bwd2(dproj, proj, du1, dw_f[l], f"conf_bwd_b{l}")
        dproj, do, dvn, dsn, g_dn = _gdn_scan_bwd(dproj, w, qe, ke, pm, eg, o, proj, dyc, dng, f"gdn_scan_bwd{l}")
        dqd, dkd, dvd, dgb = _gdn_chunk_grad(qd, kd, vd, gb, gbt, tmi, ss, dsn, do, dvn, vn, f"gdn_chunk_bwd{l}")
        dproj, dpre, st_c = _gdn_prep_bwd1(dproj, proj, dqd, dkd, dvd, dgb, dkv_a, sc_f[l], alv, dtv,
                                           f"gdn_prep_bwd_a{l}")
        dproj, g_sc = _gdn_prep_bwd2(dproj, proj, dpre, sc_f[l], f"gdn_prep_bwd_b{l}")
        g_wp = _mm(h, dproj, ta=True, tm=D, tn=1152, tk=min(2048, T), name=f"d_w_in{l}")
        by_dest = [_split_w_in_grad(g_wp, f"split_w_in_grad{l}"), g_pw2.reshape(N_DEV, -1, 512).astype(BF16)]
        by_dest += [g.reshape(512, N_DEV, -1).transpose(1, 0, 2).astype(BF16) for g in (g_wa, g_wb, g_wc)]
        by_dest.append(g_wo.reshape(N_DEV, -1, D).astype(BF16))
        by_dest = [b.reshape(4, 2, -1, b.shape[-1]) for b in by_dest]
        if l < DEPTH - 1:
            by_dest, big_parts[l + 1] = lax.optimization_barrier((by_dest, big_parts[l + 1]))
        from_sibling = _exchange_core(by_dest, f"exchange_grads_core{l}", collective_id=2 + 2 * l)

        def input_grad(dproj, dout):
            dh = _mm(dproj, wp[l], tb=True, tm=min(1024, T), tn=D, tk=2688, name=f"d_h{l}")
            return _norm_bwd(dh, xs[l], dout, norm_g[l], scale[l], f"norm_bwd{l}")

        if l > 0:
            dout, st_n = input_grad(dproj, dout)
            from_sibling, dout = lax.optimization_barrier((from_sibling, dout))
        else:
            from_sibling, _ = lax.optimization_barrier((from_sibling, (flat2(m_w_in), flat2(v_w_in))))
        chip_sums = [_pair_sum(core, b, r, f"pair_sum{l}_{i}") for i, (b, r) in enumerate(zip(by_dest, from_sibling))]
        big_parts[l] = _exchange_chips(chip_sums, f"exchange_grads_chips{l}", collective_id=3 + 2 * l)
        if l > 0:
            dout, chip_sums = lax.optimization_barrier((dout, chip_sums))
        else:
            dproj, chip_sums = lax.optimization_barrier((dproj, chip_sums))
            dout, st_n = input_grad(dproj, dout)
        for name, g in (("b_ada", jnp.concatenate([st_n[0], st_n[1], g_gate[0]])), ("norm_g", st_n[2]),
                        ("q_norm_g", g_q.reshape(ATT_HEADS, ATT_HD).sum(0)), ("k_norm_g", g_k.reshape(2, ATT_HD).sum(0)),
                        ("sinks", g_s[0]), ("dw_b", st_b[3]),
                        ("ln_g", st_b[1]), ("ln_b", st_b[2]), ("pw2_b", st_b[0]), ("a_log", st_c[0, :4]),
                        ("dt_bias", st_c[1, :4]), ("dn_norm_g", g_dn[0]), ("dw_w", g_dw[:CONV_K]),
                        ("sconv_w", g_sc[:DN_K])):
            small[name][l] = g
    grad_x = dout[None]

    big_parts, dout = lax.optimization_barrier((big_parts, dout))
    sum_big = lambda i: _sum_adamw(big_parts[0][i], big_parts[1][i], flat2(big_w[i]), flat2(big_m[i]),
                                   flat2(big_v[i]), f"sum_adamw{i}")
    res = [sum_big(0)]

    names = list(_SMALL)
    gathered = _all_gather([jnp.stack(small[n]) for n in names] + [lsum], "gather_small_grads")
    gparts = dict(zip(names, gathered))
    loss = 0.5 * jnp.sum(jnp.sum(gathered[-1], axis=(1, 2))) / D
    dmod_my = lax.dynamic_slice(gparts["b_ada"], (0, 0, me * ada_cols), (N_DEV, DEPTH, ada_cols)).transpose(1, 0, 2)
    g_w_ada = _ada_bwd(c_all, dmod_my, "ada_bwd")
    gparts["dw_w"] = lax.dynamic_slice(gparts["dw_w"], (0, 0, 0, me * dw_cols), (N_DEV, DEPTH, CONV_K, dw_cols))
    gparts["sconv_w"] = lax.dynamic_slice(gparts["sconv_w"], (0, 0, 0, me * sc_cols), (N_DEV, DEPTH, DN_K, sc_cols))
    env = dict(b_ada=(b_ada, m_b_ada, v_b_ada), norm_g=(norm_g, m_norm_g, v_norm_g),
               q_norm_g=(q_norm_g, m_q_norm_g, v_q_norm_g), k_norm_g=(k_norm_g, m_k_norm_g, v_k_norm_g),
               sinks=(sinks, m_sinks, v_sinks), dw_b=(dw_b, m_dw_b, v_dw_b), ln_g=(ln_g, m_ln_g, v_ln_g),
               ln_b=(ln_b, m_ln_b, v_ln_b), pw2_b=(pw2_b, m_pw2_b, v_pw2_b), a_log=(a_log, m_a_log, v_a_log),
               dt_bias=(dt_bias, m_dt_bias, v_dt_bias), dn_norm_g=(dn_norm_g, m_dn_norm_g, v_dn_norm_g),
               dw_w=(dw_w, m_dw_w, v_dw_w), sconv_w=(sconv_w, m_sconv_w, v_sconv_w))
    upd = _sum_adamw_many([gparts[n] for n in names], [env[n][0] for n in names], [env[n][1] for n in names],
                          [env[n][2] for n in names], "sum_adamw_small")

    d_ada, nm_ada, nv_ada = (u.reshape(w_ada.shape) for u in
                             _adamw(flat2(w_ada), flat2(g_w_ada), flat2(m_w_ada), flat2(v_w_ada), "adamw_w_ada"))

    g_small, d_small, m_small, v_small = (dict(zip(names, u)) for u in upd)
    res += [sum_big(i) for i in range(1, len(big_w))]
    g_big, d_big, m_big, v_big =([r[k].reshape(w.shape) for r, w in zip(res, big_w)] for k in range(4))

    order = ("w_ada", "b_ada", "norm_g", "w_in", "q_norm_g", "k_norm_g", "sinks", "dw_w", "dw_b", "ln_g", "ln_b",
             "pw2_w", "pw2_b", "sconv_w", "a_log", "dt_bias", "dn_norm_g", "w_proj_a", "w_proj_b", "w_proj_c", "w_out")
    big_names = ("w_in", "pw2_w", "w_proj_a", "w_proj_b", "w_proj_c", "w_out")

    def pick(kind):
        src_small = (g_small, d_small, m_small, v_small)[kind]
        src_big = (g_big, d_big, m_big, v_big)[kind]
        src_ada = (g_w_ada, d_ada, nm_ada, nv_ada)[kind]
        return [src_ada if n == "w_ada" else src_big[big_names.index(n)] if n in big_names else src_small[n]
                for n in order]

    return (loss, grad_x, *pick(0), *pick(1), *pick(2), *pick(3))
```

```python
import functools
import math

import jax
import jax.numpy as jnp
import numpy as np
from jax import lax
from jax.experimental import pallas as pl
from jax.experimental.pallas import tpu as pltpu
from jax.experimental.pallas import tpu_sc as plsc

F32 = jnp.float32
BF16 = jnp.bfloat16
HI = lax.Precision.HIGHEST

N_DEV = 8
D = 1024
DEPTH = 2
EPS = 1e-6
NEG_INF = -1e30
WINDOW = 128
ATT_HEADS = 8
ATT_HD = 64
CONV_K = 31
DN_HEADS = 4
DN_K = 4
CHUNK = 64
D_IN = 7944
VMEM_LIMIT = 56 * 1024 * 1024

C_MG, C_QA, C_ZA, C_ZB, C_QC, C_KC, C_VC, C_GV, C_GG, C_ZC, C_KA, C_VA, C_AB, NP = (
    0, 3072, 3584, 4096, 4608, 5120, 5632, 6144, 6656, 7168, 7680, 7808, 7936, 8064)
_PAD_FROM = ((4872, 7944), (0, 512), (768, 1280), (2304, 2816), (2816, 4352), (1280, 2304), (4360, 4872),
             (512, 768), (4352, 4360))

ALIBI = tuple(float(2.0 ** (-8.0 * (h + 1) / ATT_HEADS)) for h in range(ATT_HEADS))

ADAM_LR, ADAM_B1, ADAM_B2, ADAM_EPS, ADAM_WD, ADAM_STEP = 0.001, 0.9, 0.999, 1e-08, 0.01, 10


def _cparams(sem=None):
    return pltpu.CompilerParams(dimension_semantics=sem, vmem_limit_bytes=VMEM_LIMIT)


def _sig(x):
    return jax.nn.sigmoid(x)


def _silu(x):
    return x * _sig(x)


def _dsilu(x):
    s = _sig(x)
    return s * (1.0 + x * (1.0 - s))


def _dot(a, b, dims=((1,), (0,)), precision=None):
    return lax.dot_general(a, b, (dims, ((), ())), preferred_element_type=F32, precision=precision)


def _bdot(a, b, dims=((1,), (0,))):
    return _dot(a.astype(BF16), b.astype(BF16), dims)


NN, NT, TN = ((1,), (0,)), ((1,), (1,)), ((0,), (0,))


def _row(v):
    return v.reshape(1, -1)


def _mm(a, b, *, ta=False, tb=False, tm, tn, tk, name):
    M, K = (a.shape[1], a.shape[0]) if ta else a.shape
    N = b.shape[0] if tb else b.shape[1]
    assert M % tm == 0 and N % tn == 0 and K % tk == 0, (M, N, K, tm, tn, tk)
    nk = K // tk
    dims = ((0 if ta else 1,), (1 if tb else 0,))

    def body(a_ref, b_ref, o_ref):
        k = pl.program_id(2)
        part = _bdot(a_ref[...], b_ref[...], dims)

        @pl.when(k == 0)
        def _():
            o_ref[...] = part

        @pl.when(k > 0)
        def _():
            o_ref[...] += part

    a_spec = pl.BlockSpec((tk, tm), lambda i, j, k: (k, i)) if ta else pl.BlockSpec((tm, tk), lambda i, j, k: (i, k))
    b_spec = pl.BlockSpec((tn, tk), lambda i, j, k: (j, k)) if tb else pl.BlockSpec((tk, tn), lambda i, j, k: (k, j))
    return pl.pallas_call(
        body, name=name, grid=(M // tm, N // tn, nk),
        in_specs=[a_spec, b_spec], out_specs=pl.BlockSpec((tm, tn), lambda i, j, k: (i, j)),
        out_shape=jax.ShapeDtypeStruct((M, N), F32),
        compiler_params=_cparams(("parallel", "parallel", "arbitrary")),
    )(a, b)


def _norm_fwd(x, norm_g, scale, shift, name):
    T = x.shape[0]
    tm = min(512, T)

    def body(x_ref, g_ref, sc_ref, sh_ref, h_ref):
        xv = x_ref[...]
        r = lax.rsqrt(jnp.mean(xv * xv, axis=-1, keepdims=True) + EPS)
        h_ref[...] = ((xv * r) * g_ref[...] * (1.0 + sc_ref[...]) + sh_ref[...]).astype(BF16)

    vec = pl.BlockSpec((1, D), lambda i: (0, 0))
    return pl.pallas_call(
        body, name=name, grid=(T // tm,),
        in_specs=[pl.BlockSpec((tm, D), lambda i: (i, 0)), vec, vec, vec],
        out_specs=pl.BlockSpec((tm, D), lambda i: (i, 0)),
        out_shape=jax.ShapeDtypeStruct((T, D), BF16),
        compiler_params=_cparams(("parallel",)),
    )(x, _row(norm_g), _row(scale), _row(shift))


def _norm_bwd(dh, x, dres, norm_g, scale, name):
    T = x.shape[0]
    tm = min(512, T)

    def body(dh_ref, x_ref, dr_ref, g_ref, sc_ref, dx_ref, st_ref):
        i = pl.program_id(0)
        xv, dhv = x_ref[...], dh_ref[...]
        r = lax.rsqrt(jnp.mean(xv * xv, axis=-1, keepdims=True) + EPS)
        xh = xv * r
        g, s1 = g_ref[...], 1.0 + sc_ref[...]
        dxh = dhv * (g * s1)
        dx_ref[...] = dr_ref[...] + r * (dxh - xh * jnp.mean(dxh * xh, axis=-1, keepdims=True))
        dhx = dhv * xh
        upd = jnp.concatenate([jnp.sum(dhv, axis=0, keepdims=True), jnp.sum(dhx * g, axis=0, keepdims=True),
                               jnp.sum(dhx * s1, axis=0, keepdims=True), jnp.zeros((5, D), F32)], axis=0)

        @pl.when(i == 0)
        def _():
            st_ref[...] = upd

        @pl.when(i > 0)
        def _():
            st_ref[...] += upd

    vec = pl.BlockSpec((1, D), lambda i: (0, 0))
    blk = pl.BlockSpec((tm, D), lambda i: (i, 0))
    return pl.pallas_call(
        body, name=name, grid=(T // tm,),
        in_specs=[blk, blk, blk, vec, vec],
        out_specs=[blk, pl.BlockSpec((8, D), lambda i: (0, 0))],
        out_shape=[jax.ShapeDtypeStruct((T, D), F32), jax.ShapeDtypeStruct((8, D), F32)],
        compiler_params=_cparams(("arbitrary",)),
    )(dh, x, dres, _row(norm_g), _row(scale))


def _rms(x, g):
    r = lax.rsqrt(jnp.mean(x * x, axis=-1, keepdims=True) + EPS)
    return x * r, r


def _head_mean_matrix():
    head = np.arange(ATT_HEADS * ATT_HD) // ATT_HD
    return jnp.asarray((head[:, None] == head[None, :]) * (1.0 / ATT_HD), BF16)


def _head_rms(x, hm):
    r = lax.rsqrt(_dot_exact(x * x, hm) + EPS)
    return x * r, r


def _head_rms_bwd(dy, xh, r, g, hm):
    dxh = dy * g
    return r * (dxh - xh * _dot_exact(dxh * xh, hm)), dy * xh


def _attn_mask(n):
    qi = lax.broadcasted_iota(jnp.int32, (WINDOW, 2 * WINDOW), 0)
    kj = lax.broadcasted_iota(jnp.int32, (WINDOW, 2 * WINDOW), 1)
    dist = qi + WINDOW - kj
    valid = (dist >= 0) & (dist < WINDOW) & ((n > 0) | (kj >= WINDOW))
    return valid, dist.astype(F32)


def _attn_probs(s, h, sink, valid, distf):
    s = s - ALIBI[h] * distf
    s = jnp.where(valid, s, NEG_INF)
    m = jnp.maximum(jnp.max(s, axis=-1, keepdims=True), sink)
    p = jnp.exp(s - m)
    es = jnp.exp(sink - m)
    den = jnp.sum(p, axis=-1, keepdims=True) + es
    return p / den, es / den


def _attn_fwd(proj, q_norm_g, k_norm_g, sinks, name):
    T = proj.shape[0]
    nb = T // WINDOW

    def body(sink_ref, q_ref, z_ref, kc_ref, kp_ref, vc_ref, vp_ref, qg_ref, kg_ref, hm_ref, o_ref):
        n = pl.program_id(0)
        valid, distf = _attn_mask(n)
        k2 = jnp.concatenate([kp_ref[...], kc_ref[...]], axis=0)
        v2 = jnp.concatenate([vp_ref[...], vc_ref[...]], axis=0).astype(BF16)
        kn = (_head_rms(k2, hm_ref[:128, :128])[0] * kg_ref[...]).astype(BF16)
        qn = ((_head_rms(q_ref[...], hm_ref[...])[0] * qg_ref[...]) * (ATT_HD ** -0.5)).astype(BF16)

        def head(h):
            sl, gsl = slice(64 * h, 64 * h + 64), slice(64 * (h // 4), 64 * (h // 4) + 64)
            s = _dot(qn[:, sl], kn[:, gsl], NT)
            yield
            p, _ = _attn_probs(s, h, sink_ref[h], valid, distf)
            o_ref[:, sl] = _dot(p.astype(BF16), v2[:, gsl])
            yield

        _lockstep([head(h) for h in range(ATT_HEADS)])
        o_ref[...] = o_ref[...] * _silu(z_ref[...])

    prev = lambda n: jnp.maximum(n - 1, 0)
    return pl.pallas_call(
        body, name=name, grid=(nb,),
        in_specs=[pl.BlockSpec(memory_space=pltpu.SMEM),
                  pl.BlockSpec((WINDOW, 512), lambda n: (n, C_QA // 512)),
                  pl.BlockSpec((WINDOW, 512), lambda n: (n, C_ZA // 512)),
                  pl.BlockSpec((WINDOW, 128), lambda n: (n, C_KA // 128)),
                  pl.BlockSpec((WINDOW, 128), lambda n: (prev(n), C_KA // 128)),
                  pl.BlockSpec((WINDOW, 128), lambda n: (n, C_VA // 128)),
                  pl.BlockSpec((WINDOW, 128), lambda n: (prev(n), C_VA // 128)),
                  pl.BlockSpec((1, 512), lambda n: (0, 0)), pl.BlockSpec((1, 128), lambda n: (0, 0)),
                  pl.BlockSpec((512, 512), lambda n: (0, 0))],
        out_specs=pl.BlockSpec((WINDOW, 512), lambda n: (n, 0)),
        out_shape=jax.ShapeDtypeStruct((T, 512), F32),
        compiler_params=_cparams(("parallel",)),
    )(sinks, proj, proj, proj, proj, proj, proj, _row(jnp.tile(q_norm_g, ATT_HEADS)), _row(jnp.tile(k_norm_g, 2)),
      _head_mean_matrix())


def _rms_bwd(dy, xh, r, g):
    dxh = dy * g
    return r * (dxh - xh * jnp.mean(dxh * xh, axis=-1, keepdims=True)), dy * xh


def _attn_bwd(dproj, proj, dya, q_norm_g, k_norm_g, sinks, name):
    T = proj.shape[0]
    nb = T // WINDOW

    def body(sink_ref, dp_any, q_ref, z_ref, kc_ref, kp_ref, vc_ref, vp_ref, dy_ref, qg_ref, kg_ref, hm_ref,
             dqz_ref, dkv_ref, gq_ref, gk_ref, gs_ref, ck_ref, cv_ref, o_sc, dq_sc):
        n = pl.program_id(0)

        @pl.when(n == 0)
        def _():
            gq_ref[...] = jnp.zeros_like(gq_ref)
            gk_ref[...] = jnp.zeros_like(gk_ref)
            gs_ref[...] = jnp.zeros_like(gs_ref)
            ck_ref[...] = jnp.zeros_like(ck_ref)
            cv_ref[...] = jnp.zeros_like(cv_ref)

        lane8 = lax.broadcasted_iota(jnp.int32, (1, 8), 1)

        @pl.when(n < nb)
        def _():
            valid, distf = _attn_mask(n)
            k2 = jnp.concatenate([kp_ref[...], kc_ref[...]], axis=0)
            v2 = jnp.concatenate([vp_ref[...], vc_ref[...]], axis=0).astype(BF16)
            kn = (_head_rms(k2, hm_ref[:128, :128])[0] * kg_ref[...]).astype(BF16)
            qh, qr = _head_rms(q_ref[...], hm_ref[...])
            qn = ((qh * qg_ref[...]) * (ATT_HD ** -0.5)).astype(BF16)
            zs = z_ref[...]
            do_all = dy_ref[...] * _silu(zs)
            dob_all = do_all.astype(BF16)

            def head(h):
                sl, gsl = slice(64 * h, 64 * h + 64), slice(64 * (h // 4), 64 * (h // 4) + 64)
                s = _dot(qn[:, sl], kn[:, gsl], NT)
                dpm = _dot(dob_all[:, sl], v2[:, gsl], NT)
                yield
                p, ps = _attn_probs(s, h, sink_ref[h], valid, distf)
                pb = p.astype(BF16)
                o_sc[:, sl] = _dot(pb, v2[:, gsl])
                dvg = _dot(pb, dob_all[:, sl], TN)
                delta = jnp.sum(p * dpm, axis=-1, keepdims=True)
                ds = (p * (dpm - delta)).astype(BF16)
                gs = jnp.where(lane8 == h, -jnp.sum(ps * delta, axis=0, keepdims=True), 0.0)
                yield
                dkn = _dot(ds, qn[:, sl], TN)
                dq_sc[:, sl] = _dot(ds, kn[:, gsl])
                yield
                return dkn, dvg, gs

            res = _lockstep([head(h) for h in range(ATT_HEADS)])
            dqz_ref[:, 512:] = (dy_ref[...] * o_sc[...] * _dsilu(zs)).astype(BF16)
            dq, gq = _head_rms_bwd(dq_sc[...] * (ATT_HD ** -0.5), qh, qr, qg_ref[...], hm_ref[...])
            dqz_ref[:, :512] = dq.astype(BF16)
            gq_acc = jnp.sum(gq, axis=0, keepdims=True)
            gs_acc = sum(r[2] for r in res[1:]) + res[0][2]
            for g in range(2):
                dkn = (res[4 * g][0] + res[4 * g + 1][0]) + (res[4 * g + 2][0] + res[4 * g + 3][0])
                dvg = (res[4 * g][1] + res[4 * g + 1][1]) + (res[4 * g + 2][1] + res[4 * g + 3][1])
                ksl = slice(64 * g, 64 * g + 64)
                vsl = slice(128 + 64 * g, 128 + 64 * g + 64)
                dkv_ref[:, ksl] = ck_ref[:, ksl] + dkn[:WINDOW]
                dkv_ref[:, vsl] = cv_ref[:, ksl] + dvg[:WINDOW]
                ck_ref[:, ksl] = dkn[WINDOW:]
                cv_ref[:, ksl] = dvg[WINDOW:]
            gq_ref[...] += gq_acc
            gs_ref[...] += gs_acc

        @pl.when(n == nb)
        def _():
            dkv_ref[:, :128] = ck_ref[...]
            dkv_ref[:, 128:] = cv_ref[...]

        @pl.when(n > 0)
        def _():
            hm = hm_ref[:128, :128]
            kh, kr = _head_rms(kp_ref[...], hm)
            dk, gk = _head_rms_bwd(dkv_ref[:, :128], kh, kr, kg_ref[...], hm)
            dkv_ref[:, :128] = dk
            gk_ref[...] += jnp.sum(gk, axis=0, keepdims=True)

    cur = lambda n: jnp.minimum(n, nb - 1)
    prev = lambda n: jnp.maximum(n - 1, 0)
    small = lambda w: pl.BlockSpec((1, w), lambda n: (0, 0))
    return pl.pallas_call(
        body, name=name, grid=(nb + 1,),
        in_specs=[pl.BlockSpec(memory_space=pltpu.SMEM), pl.BlockSpec(memory_space=pl.ANY),
                  pl.BlockSpec((WINDOW, 512), lambda n: (cur(n), C_QA // 512)),
                  pl.BlockSpec((WINDOW, 512), lambda n: (cur(n), C_ZA // 512)),
                  pl.BlockSpec((WINDOW, 128), lambda n: (cur(n), C_KA // 128)),
                  pl.BlockSpec((WINDOW, 128), lambda n: (prev(n), C_KA // 128)),
                  pl.BlockSpec((WINDOW, 128), lambda n: (cur(n), C_VA // 128)),
                  pl.BlockSpec((WINDOW, 128), lambda n: (prev(n), C_VA // 128)),
                  pl.BlockSpec((WINDOW, 512), lambda n: (cur(n), 0)),
                  small(512), small(128), pl.BlockSpec((512, 512), lambda n: (0, 0))],
        out_specs=[pl.BlockSpec((WINDOW, 1024), lambda n: (cur(n), C_QA // 1024)),
                   pl.BlockSpec((WINDOW, 256), lambda n: (prev(n), 0)),
                   small(512), small(128), small(8)],
        out_shape=[jax.ShapeDtypeStruct(dproj.shape, BF16), jax.ShapeDtypeStruct((T, 256), F32),
                   jax.ShapeDtypeStruct((1, 512), F32), jax.ShapeDtypeStruct((1, 128), F32),
                   jax.ShapeDtypeStruct((1, 8), F32)],
        scratch_shapes=[pltpu.VMEM((WINDOW, 128), F32), pltpu.VMEM((WINDOW, 128), F32),
                        pltpu.VMEM((WINDOW, 512), F32), pltpu.VMEM((WINDOW, 512), F32)],
        input_output_aliases={1: 0},
        compiler_params=_cparams(("arbitrary",)),
    )(sinks, dproj, proj, proj, proj, proj, proj, proj, dya, _row(jnp.tile(q_norm_g, ATT_HEADS)),
      _row(jnp.tile(k_norm_g, 2)), _head_mean_matrix())


HALO_B = 32


def _conf_specs(T, tm):
    r = tm // HALO_B
    cur = lambda c: pl.BlockSpec((tm, 512), lambda i: (i, c // 512))
    prev = lambda c: pl.BlockSpec((HALO_B, 512), lambda i: (jnp.maximum(i * r - 1, 0), c // 512))
    return cur, prev


SUB = 8
ROW_CHUNK = 64


def _shifted_copies(ext_ref, sh_ref):
    total = ext_ref.shape[0]
    for r in range(SUB):
        rows = total if r == 0 else total - SUB
        sh_ref[r, :rows, :] = ext_ref[pl.ds(r, rows), :]


def _taps_by_shift(offsets):
    groups = {}
    for k, o in enumerate(offsets):
        q, r = divmod(o, SUB)
        groups.setdefault(r, []).append((k, q))
    return groups


def _conv_taps(sh_ref, w_ref, offsets, out_ref, init):
    groups = _taps_by_shift(offsets)

    def chunk(ci, carry):
        r0 = pl.multiple_of(ci * ROW_CHUNK, ROW_CHUNK)
        acc = jnp.zeros((ROW_CHUNK, out_ref.shape[1]), F32) + init
        for r, taps in groups.items():
            win = sh_ref[r, pl.ds(r0, ROW_CHUNK + SUB * max(q for _, q in taps)), :]
            for k, q in taps:
                acc = acc + w_ref[k:k + 1, :] * win[SUB * q:SUB * q + ROW_CHUNK]
        out_ref[pl.ds(r0, ROW_CHUNK), :] = acc
        return carry

    lax.fori_loop(0, out_ref.shape[0] // ROW_CHUNK, chunk, 0)


def _conv_weight_grad(sh_ref, d_ref, offsets):
    tm, width = d_ref.shape
    out = [None] * len(offsets)
    for r, taps in _taps_by_shift(offsets).items():
        def chunk(ci, accs, r=r, taps=taps):
            r0 = pl.multiple_of(ci * ROW_CHUNK, ROW_CHUNK)
            d = d_ref[pl.ds(r0, ROW_CHUNK), :]
            win = sh_ref[r, pl.ds(r0, ROW_CHUNK + SUB * max(q for _, q in taps)), :]
            return tuple(a + jnp.sum((d * win[SUB * q:SUB * q + ROW_CHUNK]).reshape(ROW_CHUNK // SUB, SUB, width),
                                     axis=0) for a, (_, q) in zip(accs, taps))

        accs = lax.fori_loop(0, tm // ROW_CHUNK, chunk, tuple(jnp.zeros((SUB, width), F32) for _ in taps))
        for a, (k, _) in zip(accs, taps):
            out[k] = jnp.sum(a, axis=0, keepdims=True)
    return out


def _conf_scratch(tm):
    return [pltpu.VMEM((tm + HALO_B, 512), F32), pltpu.VMEM((SUB, tm + HALO_B, 512), F32), pltpu.VMEM((tm, 512), F32)]


def _conf_core(i, tm, gv_ref, gg_ref, gvp_ref, ggp_ref, w_ref, b_ref, lg_ref, lb_ref, pw_ref, pb_ref, ext_ref, sh_ref,
               cv_ref):
    up = gvp_ref[...] * _sig(ggp_ref[...])
    ext_ref[:HALO_B] = jnp.where(i > 0, up, 0.0)
    ext_ref[HALO_B:] = gv_ref[...] * _sig(gg_ref[...])
    _shifted_copies(ext_ref, sh_ref)
    _conv_taps(sh_ref, w_ref, [HALO_B - CONV_K + 1 + k for k in range(CONV_K)], cv_ref, b_ref[...])
    acc = cv_ref[...]
    mu = jnp.mean(acc, axis=-1, keepdims=True)
    xc = acc - mu
    rstd = lax.rsqrt(jnp.mean(xc * xc, axis=-1, keepdims=True) + EPS)
    xh = xc * rstd
    u2 = xh * lg_ref[...] + lb_ref[...]
    u3 = _silu(u2)
    ypre = _bdot(u3, pw_ref[...]) + pb_ref[...]
    return xh, rstd, u2, u3, ypre


def _conf_fwd(proj, dw_w, dw_b, ln_g, ln_b, pw2, pw2_b, name):
    T = proj.shape[0]
    tm = min(512, T)
    cur, prev = _conf_specs(T, tm)

    def body(gv_ref, gg_ref, gvp_ref, ggp_ref, zb_ref, w_ref, b_ref, lg_ref, lb_ref, pw_ref, pb_ref, o_ref, *scratch):
        i = pl.program_id(0)
        ypre = _conf_core(i, tm, gv_ref, gg_ref, gvp_ref, ggp_ref, w_ref, b_ref, lg_ref, lb_ref, pw_ref, pb_ref,
                          *scratch)[4]
        o_ref[...] = ypre * _silu(zb_ref[...])

    full = lambda s: pl.BlockSpec(s, lambda i: (0, 0))
    return pl.pallas_call(
        body, name=name, grid=(T // tm,),
        in_specs=[cur(C_GV), cur(C_GG), prev(C_GV), prev(C_GG), cur(C_ZB), full((CONV_K, 512)), full((1, 512)),
                  full((1, 512)), full((1, 512)), full((512, 512)), full((1, 512))],
        out_specs=pl.BlockSpec((tm, 512), lambda i: (i, 0)),
        out_shape=jax.ShapeDtypeStruct((T, 512), F32),
        scratch_shapes=_conf_scratch(tm),
        compiler_params=_cparams(("parallel",)),
    )(proj, proj, proj, proj, proj, dw_w, _row(dw_b), _row(ln_g), _row(ln_b), pw2, _row(pw2_b))


def _conf_bwd1(dproj, proj, dyb, dw_w, dw_b, ln_g, ln_b, pw2, pw2_b, name):
    T = proj.shape[0]
    tm = min(512, T)
    cur, prev = _conf_specs(T, tm)

    def body(dp_any, gv_ref, gg_ref, gvp_ref, ggp_ref, zb_ref, dy_ref, w_ref, b_ref, lg_ref, lb_ref, pw_ref, pb_ref,
             dzb_ref, du1_ref, gpw_ref, st_ref, *scratch):
        i = pl.program_id(0)
        xh, rstd, u2, u3, ypre = _conf_core(i, tm, gv_ref, gg_ref, gvp_ref, ggp_ref, w_ref, b_ref, lg_ref, lb_ref,
                                            pw_ref, pb_ref, *scratch)
        zb, dy = zb_ref[...], dy_ref[...]
        dzb_ref[...] = (dy * ypre * _dsilu(zb)).astype(BF16)
        dyp = dy * _silu(zb)
        du2 = _bdot(dyp, pw_ref[...], NT) * _dsilu(u2)
        dxh = du2 * lg_ref[...]
        du1 = rstd * (dxh - jnp.mean(dxh, axis=-1, keepdims=True) - xh * jnp.mean(dxh * xh, axis=-1, keepdims=True))
        du1_ref[...] = du1
        gpw = _bdot(u3, dyp, TN)
        rs = lambda a: jnp.sum(a, axis=0, keepdims=True)
        upd = jnp.concatenate([rs(dyp), rs(du2 * xh), rs(du2), rs(du1), jnp.zeros((4, 512), F32)], axis=0)

        @pl.when(i == 0)
        def _():
            gpw_ref[...] = gpw
            st_ref[...] = upd

        @pl.when(i > 0)
        def _():
            gpw_ref[...] += gpw
            st_ref[...] += upd

    full = lambda s: pl.BlockSpec(s, lambda i: (0, 0))
    blk = pl.BlockSpec((tm, 512), lambda i: (i, 0))
    return pl.pallas_call(
        body, name=name, grid=(T // tm,),
        in_specs=[pl.BlockSpec(memory_space=pl.ANY), cur(C_GV), cur(C_GG), prev(C_GV), prev(C_GG), cur(C_ZB), blk,
                  full((CONV_K, 512)), full((1, 512)), full((1, 512)), full((1, 512)), full((512, 512)), full((1, 512))],
        out_specs=[cur(C_ZB), blk, full((512, 512)), full((8, 512))],
        out_shape=[jax.ShapeDtypeStruct(dproj.shape, BF16), jax.ShapeDtypeStruct((T, 512), F32),
                   jax.ShapeDtypeStruct((512, 512), F32), jax.ShapeDtypeStruct((8, 512), F32)],
        scratch_shapes=_conf_scratch(tm),
        input_output_aliases={0: 0},
        compiler_params=_cparams(("arbitrary",)),
    )(dproj, proj, proj, proj, proj, proj, dyb, dw_w, _row(dw_b), _row(ln_g), _row(ln_b), pw2, _row(pw2_b))


def _conf_bwd2(dproj, proj, du1, dw_w, name):
    T = proj.shape[0]
    tm = min(512, T)
    nt = T // tm
    r = tm // HALO_B
    cur, prev = _conf_specs(T, tm)

    def body(dp_any, gv_ref, gg_ref, gvp_ref, ggp_ref, du_ref, dun_ref, w_ref, dglu_ref, gw_ref, ext_ref, sh_ref,
             cv_ref):
        i = pl.program_id(0)
        gv, sg = gv_ref[...], _sig(gg_ref[...])
        ext_ref[:HALO_B] = jnp.where(i > 0, gvp_ref[...] * _sig(ggp_ref[...]), 0.0)
        ext_ref[HALO_B:] = gv * sg
        _shifted_copies(ext_ref, sh_ref)
        rows = _conv_weight_grad(sh_ref, du_ref, [HALO_B - CONV_K + 1 + k for k in range(CONV_K)])
        rows.append(jnp.zeros((1, 512), F32))
        gw = jnp.concatenate(rows, axis=0)
        ext_ref[:tm] = du_ref[...]
        ext_ref[tm:] = jnp.where(i < nt - 1, dun_ref[...], 0.0)
        _shifted_copies(ext_ref, sh_ref)
        _conv_taps(sh_ref, w_ref, [CONV_K - 1 - k for k in range(CONV_K)], cv_ref, 0.0)
        du0 = cv_ref[...]
        dglu_ref[:, :512] = (du0 * sg).astype(BF16)
        dglu_ref[:, 512:] = (du0 * gv * sg * (1.0 - sg)).astype(BF16)

        @pl.when(i == 0)
        def _():
            gw_ref[...] = gw

        @pl.when(i > 0)
        def _():
            gw_ref[...] += gw

    full = lambda s: pl.BlockSpec(s, lambda i: (0, 0))
    return pl.pallas_call(
        body, name=name, grid=(nt,),
        in_specs=[pl.BlockSpec(memory_space=pl.ANY), cur(C_GV), cur(C_GG), prev(C_GV), prev(C_GG),
                  pl.BlockSpec((tm, 512), lambda i: (i, 0)),
                  pl.BlockSpec((HALO_B, 512), lambda i: (jnp.minimum((i + 1) * r, T // HALO_B - 1), 0)),
                  full((CONV_K, 512))],
        out_specs=[pl.BlockSpec((tm, 1024), lambda i: (i, C_GV // 1024)), full((32, 512))],
        out_shape=[jax.ShapeDtypeStruct(dproj.shape, BF16), jax.ShapeDtypeStruct((32, 512), F32)],
        scratch_shapes=_conf_scratch(tm),
        input_output_aliases={0: 0},
        compiler_params=_cparams(("arbitrary",)),
    )(dproj, proj, proj, proj, proj, du1, du1, dw_w)


HALO_C = 8
QKV_C = 1536


def _softplus(x):
    return jnp.maximum(x, 0.0) + jnp.log1p(jnp.exp(-jnp.abs(x)))


def _gdn_conv(i, tm, x_ref, xp_ref, w_ref, ext_ref):
    ext_ref[:HALO_C] = jnp.where(i > 0, xp_ref[...], 0.0)
    ext_ref[HALO_C:] = x_ref[...]
    pre = jnp.zeros((tm, QKV_C), F32)
    for k in range(DN_K):
        pre = pre + w_ref[k:k + 1, :] * ext_ref[pl.ds(HALO_C - DN_K + 1 + k, tm), :]
    return pre


def _gdn_specs(T, tm):
    r = tm // HALO_C
    cur = pl.BlockSpec((tm, QKV_C), lambda i: (i, C_QC // QKV_C))
    prev = pl.BlockSpec((HALO_C, QKV_C), lambda i: (jnp.maximum(i * r - 1, 0), C_QC // QKV_C))
    ab = pl.BlockSpec((tm, 128), lambda i: (i, C_AB // 128))
    return cur, prev, ab


def _gdn_prep_fwd(proj, sconv_w, alog_v, dtb_v, name):
    T = proj.shape[0]
    tm = min(512, T)
    cur, prev, ab = _gdn_specs(T, tm)

    def body(x_ref, xp_ref, ab_ref, w_ref, al_ref, dt_ref, q_ref, k_ref, v_ref, gb_ref, ext_ref):
        i = pl.program_id(0)
        y = _silu(_gdn_conv(i, tm, x_ref, xp_ref, w_ref, ext_ref))
        for h in range(DN_HEADS):
            sl = slice(128 * h, 128 * h + 128)
            qh, kh = y[:, sl], y[:, 512 + 128 * h:512 + 128 * h + 128]
            q_ref[:, sl] = qh * lax.rsqrt(jnp.sum(qh * qh, axis=-1, keepdims=True) + EPS) * (128 ** -0.5)
            k_ref[:, sl] = kh * lax.rsqrt(jnp.sum(kh * kh, axis=-1, keepdims=True) + EPS)
        v_ref[...] = y[:, 1024:]
        abv = ab_ref[...]
        lane = lax.broadcasted_iota(jnp.int32, (tm, 128), 1)
        g = -jnp.exp(al_ref[...]) * _softplus(abv + dt_ref[...])
        gb_ref[...] = jnp.where(lane < DN_HEADS, g, _sig(abv))

    full = lambda s: pl.BlockSpec(s, lambda i: (0, 0))
    blk = pl.BlockSpec((tm, 512), lambda i: (i, 0))
    return pl.pallas_call(
        body, name=name, grid=(T // tm,),
        in_specs=[cur, prev, ab, full((DN_K, QKV_C)), full((1, 128)), full((1, 128))],
        out_specs=[blk, blk, blk, pl.BlockSpec((tm, 128), lambda i: (i, 0))],
        out_shape=[jax.ShapeDtypeStruct((T, 512), F32)] * 3 + [jax.ShapeDtypeStruct((T, 128), F32)],
        scratch_shapes=[pltpu.VMEM((tm + HALO_C, QKV_C), F32)],
        compiler_params=_cparams(("parallel",)),
    )(proj, proj, proj, sconv_w, alog_v, dtb_v)


def _hdot(a, b, dims=NN):
    return _dot(a, b, dims, precision=HI)


def _lockstep(gens):
    results, live = [None] * len(gens), list(range(len(gens)))
    while live:
        for i in list(live):
            try:
                next(gens[i])
            except StopIteration as stop:
                results[i] = stop.value
                live.remove(i)
    return results


def _split(a):
    hi = a.astype(BF16)
    return hi, (a - hi.astype(F32)).astype(BF16)


def _dot_exact(a, b, dims=NN, split_left=True):
    x = (a if split_left else b).astype(F32)
    hi = x.astype(BF16)
    r = x - hi.astype(F32)
    mid = r.astype(BF16)
    lo = (r - mid.astype(F32)).astype(BF16)
    other = (b if split_left else a).astype(BF16)
    one = (lambda p: _dot(p, other, dims)) if split_left else (lambda p: _dot(other, p, dims))
    return (one(lo) + one(mid)) + one(hi)


def _dot3(a, b):
    (ah, al), (bh, bl) = a, b
    return _dot(ah, bh) + (_dot(ah, bl) + _dot(al, bh))


def _tri_inv(mats, eye):
    ps = [-a for a in mats]
    ts = [eye + p for p in ps]
    for _ in range(5):
        sp = [_split(p) for p in ps]
        ps = [_dot3(s, s) for s in sp]
        sp = [_split(p) for p in ps]
        ts = [t + _dot3(_split(t), s) for t, s in zip(ts, sp)]
    return ts


def _tri_consts():
    ii = lax.broadcasted_iota(jnp.int32, (CHUNK, CHUNK), 0)
    jj = lax.broadcasted_iota(jnp.int32, (CHUNK, CHUNK), 1)
    return ii >= jj, ii > jj, (ii == jj).astype(F32)


def _gdn_local(q, k, v, gcol, grow, bcol, lower, strict):
    dm = jnp.where(lower, jnp.exp(jnp.where(lower, gcol - grow, 0.0)), 0.0)
    kb = k * bcol
    a = jnp.where(strict, _bdot(kb, k, NT) * dm, 0.0)
    gc = jnp.exp(gcol)
    glast = grow[:, CHUNK - 1:CHUNK]
    return dict(q=q, k=k, v=v, bcol=bcol, gcol=gcol, glast=glast, dm=dm, kb=kb, a=a, gc=gc, vb=v * bcol,
                kbg=kb * gc, p=_bdot(q, k, NT) * dm, qe=q * gc, ke=k * jnp.exp(glast - gcol))


def _gdn_chunk_bwd(c, do, dvn, ds_new, lower, strict, ones):
    rs = lambda m: jnp.sum(m, axis=-1, keepdims=True)
    colsum = lambda m: _dot_exact(m, ones, TN)[:, :1]
    q, k, v, bcol, dm, tm, gc, s = c["q"], c["k"], c["v"], c["bcol"], c["dm"], c["tm"], c["gc"], c["s"]
    eg = jnp.exp(c["glast"])
    dqe = _bdot(do, s, NT)
    dp = jnp.where(lower, _bdot(do, c["vn"], NT), 0.0)
    dw = -_bdot(dvn, s, NT)
    dke = _bdot(c["vn"], ds_new, NT)
    dvb = _bdot(tm, dvn, TN)
    yield
    dglast = jnp.sum(rs(ds_new * s), axis=0, keepdims=True) * eg
    dk = dke * jnp.exp(c["glast"] - c["gcol"])
    r_ke = rs(dke * c["ke"])
    dglast = dglast + jnp.sum(r_ke, axis=0, keepdims=True)
    dgam = rs(dqe * c["qe"]) - r_ke
    dq = dqe * gc
    dpm = dp * dm
    mp = dp * c["p"]
    dq = dq + _bdot(dpm, k)
    dk = dk + _bdot(dpm, q, TN)
    dt = _bdot(dvn, c["vb"], NT) + _bdot(dw, c["kbg"], NT)
    dkbg = _bdot(tm, dw, TN)
    dgam = dgam + rs(mp) - colsum(mp)
    yield
    dkb = dkbg * gc
    dgam = dgam + rs(dkbg * c["kbg"])
    dat = _bdot(tm, dt, TN)
    yield
    da = jnp.where(strict, -_bdot(dat, tm, NT), 0.0)
    yield
    dam = da * dm
    ma = da * c["a"]
    dkb = dkb + _bdot(dam, k)
    dk = dk + _bdot(dam, c["kb"], TN)
    dgam = dgam + rs(ma) - colsum(ma)
    yield
    dk = dk + dkb * bcol
    dbeta = rs(dkb * k) + rs(dvb * v)
    dv = dvb * bcol
    row = lax.broadcasted_iota(jnp.int32, (CHUNK, 1), 0)
    dgam = dgam + jnp.where(row == CHUNK - 1, dglast, 0.0)
    dg = _dot_exact(lower, dgam, TN, split_left=False)
    return dq, dk, dv, dg, dbeta


SCAN_GROUP = 4
GROUP = 8


def _chunk_decay(gb_ref, gt_ref, lmat, g):
    rows = slice(CHUNK * g, CHUNK * g + CHUNK)
    return rows, _dot_exact(lmat, gb_ref[rows, :], split_left=False), _dot_exact(gt_ref[g], lmat, NT)


def _gdn_chunk_fwd(qd, kd, vd, gb, gbt, name):
    T = qd.shape[0]
    G = min(GROUP, T // CHUNK)
    ng = T // (CHUNK * G)

    def body(q_ref, k_ref, v_ref, gb_ref, gt_ref, u_ref, w_ref, qe_ref, ke_ref, p_ref, t_ref, eg_ref):
        lower, strict, eye = _tri_consts()
        lmat = lower.astype(F32)
        decay = [_chunk_decay(gb_ref, gt_ref, lmat, g) for g in range(G)]
        chains = [(g, h) for g in range(G) for h in range(DN_HEADS)]
        cs = []
        for g, h in chains:
            rows, gcs, grs = decay[g]
            sl = slice(128 * h, 128 * h + 128)
            c = _gdn_local(q_ref[rows, sl], k_ref[rows, sl], v_ref[rows, sl], gcs[:, h:h + 1], grs[h:h + 1, :],
                           gb_ref[rows, DN_HEADS + h:DN_HEADS + h + 1], lower, strict)
            qe_ref[rows, sl] = c["qe"].astype(BF16)
            ke_ref[rows, sl] = c["ke"].astype(BF16)
            p_ref[rows, 64 * h:64 * h + 64] = c["p"].astype(BF16)
            eg_ref[g, h:h + 1, :] = jnp.broadcast_to(jnp.exp(c["glast"]), (1, 128))
            cs.append(c)
        tms = [t.astype(BF16) for t in _tri_inv([c["a"] for c in cs], eye)]
        us = [_dot(t, c["vb"].astype(BF16)) for t, c in zip(tms, cs)]
        ws = [_dot(t, c["kbg"].astype(BF16)) for t, c in zip(tms, cs)]
        for (g, h), tm, u, w in zip(chains, tms, us, ws):
            rows, sl = decay[g][0], slice(128 * h, 128 * h + 128)
            u_ref[rows, sl] = u
            w_ref[rows, sl] = w.astype(BF16)
            t_ref[rows, 64 * h:64 * h + 64] = tm
        for g in range(G):
            eg_ref[g, DN_HEADS:, :] = jnp.zeros((8 - DN_HEADS, 128), F32)

    blk = pl.BlockSpec((CHUNK * G, 512), lambda n: (n, 0))
    half = pl.BlockSpec((CHUNK * G, 256), lambda n: (n, 0))
    return pl.pallas_call(
        body, name=name, grid=(ng,),
        in_specs=[blk, blk, blk, pl.BlockSpec((CHUNK * G, 128), lambda n: (n, 0)),
                  pl.BlockSpec((G, 8, CHUNK), lambda n: (n, 0, 0))],
        out_specs=[blk, blk, blk, blk, half, half, pl.BlockSpec((G, 8, 128), lambda n: (n, 0, 0))],
        out_shape=[jax.ShapeDtypeStruct((T, 512), F32)] + [jax.ShapeDtypeStruct((T, 512), BF16)] * 3
        + [jax.ShapeDtypeStruct((T, 256), BF16)] * 2 + [jax.ShapeDtypeStruct((T // CHUNK, 8, 128), F32)],
        compiler_params=_cparams(("parallel",)),
    )(qd, kd, vd, gb, gbt)


def _gdn_scan_fwd(u, w, qe, ke, pm, eg, proj, dn_g, name):
    T = u.shape[0]
    nc = T // CHUNK
    G = min(SCAN_GROUP, T // CHUNK)

    def body(u_ref, w_ref, qe_ref, ke_ref, p_ref, eg_ref, z_ref, ng_ref, y_ref, o_ref, vn_ref, ss_ref, s_ref):
        n = pl.program_id(0)

        @pl.when(n == 0)
        def _():
            s_ref[...] = jnp.zeros_like(s_ref)

        def head(j, h):
            rows, sl = slice(CHUNK * j, CHUNK * j + CHUNK), slice(128 * h, 128 * h + 128)
            s = s_ref[h]
            sb = s.astype(BF16)
            vn = u_ref[rows, sl] - _dot(w_ref[rows, sl], sb)
            qs = _dot(qe_ref[rows, sl], sb)
            yield
            vb = vn.astype(BF16)
            o = qs + _dot(p_ref[rows, 64 * h:64 * h + 64], vb)
            s_ref[h] = s * eg_ref[j, h:h + 1, :] + _dot(ke_ref[rows, sl], vb, TN)
            yield
            vn_ref[rows, sl] = vb
            o_ref[rows, sl] = o
            y_ref[rows, sl] = _rms(o, None)[0] * ng_ref[...] * _silu(z_ref[rows, sl])

        for j in range(G):
            ss_ref[j] = s_ref[...]
            _lockstep([head(j, h) for h in range(DN_HEADS)])

    blk = pl.BlockSpec((CHUNK * G, 512), lambda n: (n, 0))
    return pl.pallas_call(
        body, name=name, grid=(nc // G,),
        in_specs=[blk, blk, blk, blk, pl.BlockSpec((CHUNK * G, 256), lambda n: (n, 0)),
                  pl.BlockSpec((G, 8, 128), lambda n: (n, 0, 0)),
                  pl.BlockSpec((CHUNK * G, 512), lambda n: (n, C_ZC // 512)), pl.BlockSpec((1, 128), lambda n: (0, 0))],
        out_specs=[blk, blk, blk, pl.BlockSpec((G, DN_HEADS, 128, 128), lambda n: (n, 0, 0, 0))],
        out_shape=[jax.ShapeDtypeStruct((T, 512), F32), jax.ShapeDtypeStruct((T, 512), F32),
                   jax.ShapeDtypeStruct((T, 512), BF16), jax.ShapeDtypeStruct((nc, DN_HEADS, 128, 128), F32)],
        scratch_shapes=[pltpu.VMEM((DN_HEADS, 128, 128), F32)],
        compiler_params=_cparams(("arbitrary",)),
    )(u, w, qe, ke, pm, eg, proj, dn_g)


def _gdn_scan_bwd(dproj, w, qe, ke, pm, eg, o, proj, dyc, dn_g, name):
    T = o.shape[0]
    nc = T // CHUNK
    G = min(SCAN_GROUP, T // CHUNK)
    rev = lambda n: nc // G - 1 - n

    def body(dp_any, w_ref, qe_ref, ke_ref, p_ref, eg_ref, o_ref, z_ref, dy_ref, ng_ref,
             dz_ref, do_ref, dvn_ref, dsn_ref, gng_ref, ds_ref):
        n = pl.program_id(0)

        @pl.when(n == 0)
        def _():
            ds_ref[...] = jnp.zeros_like(ds_ref)
            gng_ref[...] = jnp.zeros_like(gng_ref)

        def head(j, h):
            rows, sl = slice(CHUNK * j, CHUNK * j + CHUNK), slice(128 * h, 128 * h + 128)
            oh, r = _rms(o_ref[rows, sl], None)
            z, dy = z_ref[rows, sl], dy_ref[rows, sl]
            dz_ref[rows, sl] = (dy * (oh * ng_ref[...]) * _dsilu(z)).astype(BF16)
            do, gg = _rms_bwd(dy * _silu(z), oh, r, ng_ref[...])
            dob = do.astype(BF16)
            ds = ds_ref[h]
            dvn = _dot(p_ref[rows, 64 * h:64 * h + 64], dob, TN) + _dot(ke_ref[rows, sl], ds.astype(BF16))
            qd = _dot(qe_ref[rows, sl], dob, TN)
            yield
            dvb = dvn.astype(BF16)
            ds_ref[h] = qd + eg_ref[j, h:h + 1, :] * ds - _dot(w_ref[rows, sl], dvb, TN)
            do_ref[rows, sl] = dob
            dvn_ref[rows, sl] = dvb
            return jnp.sum(gg, axis=0, keepdims=True)

        for j in reversed(range(G)):
            dsn_ref[j] = ds_ref[...]
            gng = _lockstep([head(j, h) for h in range(DN_HEADS)])
            gng_ref[...] += (gng[0] + gng[1]) + (gng[2] + gng[3])

    blk = pl.BlockSpec((CHUNK * G, 512), lambda n: (rev(n), 0))
    state = pl.BlockSpec((G, DN_HEADS, 128, 128), lambda n: (rev(n), 0, 0, 0))
    return pl.pallas_call(
        body, name=name, grid=(nc // G,),
        in_specs=[pl.BlockSpec(memory_space=pl.ANY), blk, blk, blk,
                  pl.BlockSpec((CHUNK * G, 256), lambda n: (rev(n), 0)),
                  pl.BlockSpec((G, 8, 128), lambda n: (rev(n), 0, 0)), blk,
                  pl.BlockSpec((CHUNK * G, 512), lambda n: (rev(n), C_ZC // 512)), blk,
                  pl.BlockSpec((1, 128), lambda n: (0, 0))],
        out_specs=[pl.BlockSpec((CHUNK * G, 512), lambda n: (rev(n), C_ZC // 512)), blk, blk, state,
                   pl.BlockSpec((1, 128), lambda n: (0, 0))],
        out_shape=[jax.ShapeDtypeStruct(dproj.shape, BF16), jax.ShapeDtypeStruct((T, 512), BF16),
                   jax.ShapeDtypeStruct((T, 512), BF16), jax.ShapeDtypeStruct((nc, DN_HEADS, 128, 128), F32),
                   jax.ShapeDtypeStruct((1, 128), F32)],
        scratch_shapes=[pltpu.VMEM((DN_HEADS, 128, 128), F32)],
        input_output_aliases={0: 0},
        compiler_params=_cparams(("arbitrary",)),
    )(dproj, w, qe, ke, pm, eg, o, proj, dyc, dn_g)


def _gdn_chunk_grad(qd, kd, vd, gb, gbt, tmi, ssave, dsn, do, dvn, vn, name):
    T = qd.shape[0]
    G = min(GROUP, T // CHUNK)
    ng = T // (CHUNK * G)

    def body(q_ref, k_ref, v_ref, gb_ref, gt_ref, t_ref, ss_ref, dsn_ref, do_ref, dvn_ref, vn_ref,
             dq_ref, dk_ref, dv_ref, dgb_ref):
        lower, strict, _ = _tri_consts()
        lmat = lower.astype(F32)
        ones = jnp.ones((CHUNK, 128), F32)
        lane = lax.broadcasted_iota(jnp.int32, (CHUNK, 128), 1)
        decay = [_chunk_decay(gb_ref, gt_ref, lmat, g) for g in range(G)]
        chains = [(g, h) for g in range(G) for h in range(DN_HEADS)]
        gens = []
        for g, h in chains:
            rows, gcs, grs = decay[g]
            sl = slice(128 * h, 128 * h + 128)
            c = _gdn_local(q_ref[rows, sl], k_ref[rows, sl], v_ref[rows, sl], gcs[:, h:h + 1], grs[h:h + 1, :],
                           gb_ref[rows, DN_HEADS + h:DN_HEADS + h + 1], lower, strict)
            c.update(tm=t_ref[rows, 64 * h:64 * h + 64], s=ss_ref[g, h], vn=vn_ref[rows, sl])
            gens.append(_gdn_chunk_bwd(c, do_ref[rows, sl], dvn_ref[rows, sl], dsn_ref[g, h], lower, strict, ones))
        dgb = [jnp.zeros((CHUNK, 128), F32) for _ in range(G)]
        for (g, h), (dq, dk, dv, dg, dbeta) in zip(chains, _lockstep(gens)):
            rows, sl = decay[g][0], slice(128 * h, 128 * h + 128)
            dq_ref[rows, sl], dk_ref[rows, sl], dv_ref[rows, sl] = dq, dk, dv
            dgb[g] = dgb[g] + jnp.where(lane == h, dg, 0.0) + jnp.where(lane == DN_HEADS + h, dbeta, 0.0)
        for g in range(G):
            dgb_ref[decay[g][0], :] = dgb[g]

    blk = pl.BlockSpec((CHUNK * G, 512), lambda n: (n, 0))
    half = pl.BlockSpec((CHUNK * G, 256), lambda n: (n, 0))
    nar = pl.BlockSpec((CHUNK * G, 128), lambda n: (n, 0))
    state = pl.BlockSpec((G, DN_HEADS, 128, 128), lambda n: (n, 0, 0, 0))
    return pl.pallas_call(
        body, name=name, grid=(ng,),
        in_specs=[blk, blk, blk, nar, pl.BlockSpec((G, 8, CHUNK), lambda n: (n, 0, 0)), half, state, state,
                  blk, blk, blk],
        out_specs=[blk, blk, blk, nar],
        out_shape=[jax.ShapeDtypeStruct((T, 512), F32)] * 3 + [jax.ShapeDtypeStruct((T, 128), F32)],
        compiler_params=_cparams(("parallel",)),
    )(qd, kd, vd, gb, gbt, tmi, ssave, dsn, do, dvn, vn)


def _gdn_prep_bwd1(dproj, proj, dqd, dkd, dvd, dgb, dkv_a, sconv_w, alog_v, dtb_v, name):
    T = proj.shape[0]
    tm = min(512, T)
    cur, prev, ab = _gdn_specs(T, tm)

    def body(dp_any, x_ref, xp_ref, ab_ref, dq_ref, dk_ref, dv_ref, dgb_ref, dkv_ref, w_ref, al_ref, dt_ref,
             o_ref, dpre_ref, st_ref, ext_ref):
        i = pl.program_id(0)
        pre = _gdn_conv(i, tm, x_ref, xp_ref, w_ref, ext_ref)
        y, dsl = _silu(pre), _dsilu(pre)
        for h in range(DN_HEADS):
            for base, g_ref, scale in ((0, dq_ref, 128 ** -0.5), (512, dk_ref, 1.0)):
                sl = slice(base + 128 * h, base + 128 * h + 128)
                xh = y[:, sl]
                r = lax.rsqrt(jnp.sum(xh * xh, axis=-1, keepdims=True) + EPS)
                xn = xh * r
                gy = g_ref[:, 128 * h:128 * h + 128]
                dpre_ref[:, sl] = (scale * r) * (gy - xn * jnp.sum(gy * xn, axis=-1, keepdims=True)) * dsl[:, sl]
        dpre_ref[:, 1024:] = dv_ref[...] * dsl[:, 1024:]
        abv, dgb = ab_ref[...], dgb_ref[...]
        lane = lax.broadcasted_iota(jnp.int32, (tm, 128), 1)
        na = -jnp.exp(al_ref[...])
        xs = abv + dt_ref[...]
        da = dgb * na * _sig(xs)
        b = _sig(abv)
        o_ref[:, :256] = dkv_ref[...].astype(BF16)
        o_ref[:, 256:] = jnp.where(lane < DN_HEADS, da,
                                   jnp.where(lane < 2 * DN_HEADS, dgb * b * (1.0 - b), 0.0)).astype(BF16)
        head = lane < DN_HEADS
        upd = jnp.concatenate([jnp.sum(jnp.where(head, dgb * na * _softplus(xs), 0.0), axis=0, keepdims=True),
                               jnp.sum(jnp.where(head, da, 0.0), axis=0, keepdims=True), jnp.zeros((6, 128), F32)],
                              axis=0)

        @pl.when(i == 0)
        def _():
            st_ref[...] = upd

        @pl.when(i > 0)
        def _():
            st_ref[...] += upd

    full = lambda s: pl.BlockSpec(s, lambda i: (0, 0))
    blk = pl.BlockSpec((tm, 512), lambda i: (i, 0))
    return pl.pallas_call(
        body, name=name, grid=(T // tm,),
        in_specs=[pl.BlockSpec(memory_space=pl.ANY), cur, prev, ab, blk, blk, blk,
                  pl.BlockSpec((tm, 128), lambda i: (i, 0)), pl.BlockSpec((tm, 256), lambda i: (i, 0)),
                  full((DN_K, QKV_C)), full((1, 128)), full((1, 128))],
        out_specs=[pl.BlockSpec((tm, 384), lambda i: (i, C_KA // 384)),
                   pl.BlockSpec((tm, QKV_C), lambda i: (i, 0)), full((8, 128))],
        out_shape=[jax.ShapeDtypeStruct(dproj.shape, BF16), jax.ShapeDtypeStruct((T, QKV_C), F32),
                   jax.ShapeDtypeStruct((8, 128), F32)],
        scratch_shapes=[pltpu.VMEM((tm + HALO_C, QKV_C), F32)],
        input_output_aliases={0: 0},
        compiler_params=_cparams(("arbitrary",)),
    )(dproj, proj, proj, proj, dqd, dkd, dvd, dgb, dkv_a, sconv_w, alog_v, dtb_v)


def _gdn_prep_bwd2(dproj, proj, dpre, sconv_w, name):
    T = proj.shape[0]
    tm = min(512, T)
    nt = T // tm
    r = tm // HALO_C
    cur, prev, _ = _gdn_specs(T, tm)

    def body(dp_any, x_ref, xp_ref, d_ref, dn_ref, w_ref, dx_ref, gw_ref, extx_ref, extd_ref):
        i = pl.program_id(0)
        extx_ref[:HALO_C] = jnp.where(i > 0, xp_ref[...], 0.0)
        extx_ref[HALO_C:] = x_ref[...]
        d = d_ref[...]
        extd_ref[:tm] = d
        extd_ref[tm:] = jnp.where(i < nt - 1, dn_ref[...], 0.0)
        dx = jnp.zeros((tm, QKV_C), F32)
        rows = []
        for k in range(DN_K):
            dx = dx + w_ref[k:k + 1, :] * extd_ref[pl.ds(DN_K - 1 - k, tm), :]
            rows.append(jnp.sum(d * extx_ref[pl.ds(HALO_C - DN_K + 1 + k, tm), :], axis=0, keepdims=True))
        rows.append(jnp.zeros((8 - DN_K, QKV_C), F32))
        gw = jnp.concatenate(rows, axis=0)
        dx_ref[...] = dx.astype(BF16)

        @pl.when(i == 0)
        def _():
            gw_ref[...] = gw

        @pl.when(i > 0)
        def _():
            gw_ref[...] += gw

    full = lambda s: pl.BlockSpec(s, lambda i: (0, 0))
    return pl.pallas_call(
        body, name=name, grid=(nt,),
        in_specs=[pl.BlockSpec(memory_space=pl.ANY), cur, prev, pl.BlockSpec((tm, QKV_C), lambda i: (i, 0)),
                  pl.BlockSpec((HALO_C, QKV_C), lambda i: (jnp.minimum((i + 1) * r, T // HALO_C - 1), 0)),
                  full((DN_K, QKV_C))],
        out_specs=[cur, full((8, QKV_C))],
        out_shape=[jax.ShapeDtypeStruct(dproj.shape, BF16), jax.ShapeDtypeStruct((8, QKV_C), F32)],
        scratch_shapes=[pltpu.VMEM((tm + HALO_C, QKV_C), F32), pltpu.VMEM((tm + HALO_C, QKV_C), F32)],
        input_output_aliases={0: 0},
        compiler_params=_cparams(("arbitrary",)),
    )(dproj, proj, proj, dpre, dpre, sconv_w)


def _merge_fwd(x, proj, ya, yb, yc, wa, wb, wc, wo, gate, name):
    T = x.shape[0]
    tm = min(256, T)

    def body(x_ref, mg_ref, ya_ref, yb_ref, yc_ref, wa_ref, wb_ref, wc_ref, wo_ref, gate_ref, o_ref):
        merged = (_sig(mg_ref[:, :D]) * _bdot(ya_ref[...], wa_ref[...])
                  + _sig(mg_ref[:, D:2 * D]) * _bdot(yb_ref[...], wb_ref[...])
                  + _sig(mg_ref[:, 2 * D:]) * _bdot(yc_ref[...], wc_ref[...]))
        o_ref[...] = x_ref[...] + gate_ref[...] * _bdot(merged, wo_ref[...])

    full = lambda s: pl.BlockSpec(s, lambda i: (0, 0))
    yb_ = pl.BlockSpec((tm, 512), lambda i: (i, 0))
    return pl.pallas_call(
        body, name=name, grid=(T // tm,),
        in_specs=[pl.BlockSpec((tm, D), lambda i: (i, 0)), pl.BlockSpec((tm, 3 * D), lambda i: (i, 0)), yb_, yb_, yb_,
                  full((512, D)), full((512, D)), full((512, D)), full((D, D)), full((1, D))],
        out_specs=pl.BlockSpec((tm, D), lambda i: (i, 0)),
        out_shape=jax.ShapeDtypeStruct((T, D), F32),
        compiler_params=_cparams(("parallel",)),
    )(x, proj, ya, yb, yc, wa, wb, wc, wo, _row(gate))


def _merge_bwd(dout, proj, ya, yb, yc, wa, wb, wc, wo, gate, name):
    T = dout.shape[0]
    tm = min(256, T)
    nt = T // tm

    def body(do_ref, mg_ref, ya_ref, yb_ref, yc_ref, wa_ref, wb_ref, wc_ref, wo_ref, gate_ref,
             dmg_ref, dya_ref, dyb_ref, dyc_ref, gwa_hbm, gwb_hbm, gwc_hbm, gwo_hbm, gg_ref,
             gwa_ref, gwb_ref, gwc_ref, gwo_ref):
        i = pl.program_id(0)

        @pl.when(i == 0)
        def _():
            for r in (gwa_ref, gwb_ref, gwc_ref, gwo_ref, gg_ref):
                r[...] = jnp.zeros_like(r)

        ys = (ya_ref[...], yb_ref[...], yc_ref[...])
        ws = (wa_ref, wb_ref, wc_ref)
        gs = tuple(_sig(mg_ref[:, j * D:(j + 1) * D]) for j in range(3))
        ps = tuple(_bdot(ys[j], ws[j][...]) for j in range(3))
        merged = gs[0] * ps[0] + gs[1] * ps[1] + gs[2] * ps[2]
        do = do_ref[...]
        dmerged = _bdot(do * gate_ref[...], wo_ref[...], NT)
        gwo_ref[...] += _bdot(merged, do, TN)
        for j, (dy_ref, gw_ref) in enumerate(((dya_ref, gwa_ref), (dyb_ref, gwb_ref), (dyc_ref, gwc_ref))):
            dp = dmerged * gs[j]
            dmg_ref[:, j * D:(j + 1) * D] = (dmerged * ps[j] * gs[j] * (1.0 - gs[j])).astype(BF16)
            dy_ref[...] = _bdot(dp, ws[j][...], NT)
            gw_ref[...] += _bdot(ys[j], dp, TN)

        @pl.when(i == nt - 1)
        def _():
            m = gwo_ref[...]
            gg_ref[...] = jnp.sum(wo_ref[...].astype(F32) * m, axis=0, keepdims=True)
            gwo_ref[...] = m * gate_ref[...]
            for src, dst in ((gwa_ref, gwa_hbm), (gwb_ref, gwb_hbm), (gwc_ref, gwc_hbm), (gwo_ref, gwo_hbm)):
                pltpu.sync_copy(src, dst)

    full = lambda s: pl.BlockSpec(s, lambda i: (0, 0))
    yb_ = pl.BlockSpec((tm, 512), lambda i: (i, 0))
    anyspec = pl.BlockSpec(memory_space=pl.ANY)
    return pl.pallas_call(
        body, name=name, grid=(nt,),
        in_specs=[pl.BlockSpec((tm, D), lambda i: (i, 0)), pl.BlockSpec((tm, 3 * D), lambda i: (i, 0)), yb_, yb_, yb_,
                  full((512, D)), full((512, D)), full((512, D)), full((D, D)), full((1, D))],
        out_specs=[pl.BlockSpec((tm, 3 * D), lambda i: (i, 0)), yb_, yb_, yb_, anyspec, anyspec, anyspec, anyspec,
                   full((1, D))],
        out_shape=[jax.ShapeDtypeStruct((T, NP), BF16)] + [jax.ShapeDtypeStruct((T, 512), F32)] * 3
        + [jax.ShapeDtypeStruct((512, D), F32)] * 3 + [jax.ShapeDtypeStruct((D, D), F32), jax.ShapeDtypeStruct((1, D), F32)],
        scratch_shapes=[pltpu.VMEM((512, D), F32)] * 3 + [pltpu.VMEM((D, D), F32)],
        compiler_params=_cparams(("arbitrary",)),
    )(dout, proj, ya, yb, yc, wa, wb, wc, wo, _row(gate))


def _loss_head(y, tgt, name):
    T = y.shape[0]
    tm = min(512, T)

    def body(y_ref, t_ref, dy_ref, l_ref):
        i = pl.program_id(0)
        diff = y_ref[...] - t_ref[...]
        dy_ref[...] = diff * (1.0 / D)
        part = jnp.sum(diff * diff, axis=0, keepdims=True)

        @pl.when(i == 0)
        def _():
            l_ref[...] = part

        @pl.when(i > 0)
        def _():
            l_ref[...] += part

    blk = pl.BlockSpec((tm, D), lambda i: (i, 0))
    return pl.pallas_call(
        body, name=name, grid=(T // tm,), in_specs=[blk, blk],
        out_specs=[blk, pl.BlockSpec((1, D), lambda i: (0, 0))],
        out_shape=[jax.ShapeDtypeStruct((T, D), F32), jax.ShapeDtypeStruct((1, D), F32)],
        compiler_params=_cparams(("arbitrary",)),
    )(y, tgt)


def _ada_fwd(c_all, w_ada, b_my, name):
    def body(c_ref, w_ref, b_ref, o_ref):
        sc = _silu(c_ref[...])
        for l in range(DEPTH):
            o_ref[l] = _bdot(sc, w_ref[l]) + b_ref[l:l + 1, :]

    return pl.pallas_call(body, name=name, out_shape=jax.ShapeDtypeStruct((DEPTH, N_DEV, w_ada.shape[2]), F32),
                          compiler_params=_cparams())(c_all, w_ada, b_my)


def _ada_bwd(c_all, dmod_my, name):
    def body(c_ref, d_ref, o_ref):
        sc = _silu(c_ref[...])
        for l in range(DEPTH):
            o_ref[l] = _bdot(sc, d_ref[l], TN)

    return pl.pallas_call(body, name=name, out_shape=jax.ShapeDtypeStruct((DEPTH, D, dmod_my.shape[2]), F32),
                          compiler_params=_cparams())(c_all, dmod_my)


def _adam_math(w, g, m, v):
    m = ADAM_B1 * m + (1.0 - ADAM_B1) * g
    v = ADAM_B2 * v + (1.0 - ADAM_B2) * (g * g)
    m_hat = m / (1.0 - ADAM_B1 ** ADAM_STEP)
    v_hat = v / (1.0 - ADAM_B2 ** ADAM_STEP)
    return -ADAM_LR * (m_hat / (jnp.sqrt(v_hat) + ADAM_EPS) + ADAM_WD * w), m, v


def _row_tile(rows, cap):
    best = rows
    for t in range(8, min(rows, cap) + 1, 8):
        if rows % t == 0:
            best = t
    return best if best <= cap else rows


def _adamw(w, g, m, v, name):
    R, C = w.shape
    tr = _row_tile(R, 256)

    def body(w_ref, g_ref, m_ref, v_ref, d_ref, mo_ref, vo_ref):
        d_ref[...], mo_ref[...], vo_ref[...] = _adam_math(w_ref[...], g_ref[...], m_ref[...], v_ref[...])

    blk = pl.BlockSpec((tr, C), lambda i: (i, 0))
    return pl.pallas_call(body, name=name, grid=(R // tr,), in_specs=[blk] * 4, out_specs=[blk] * 3,
                          out_shape=[jax.ShapeDtypeStruct((R, C), F32)] * 3,
                          compiler_params=_cparams(("parallel",)))(w, g, m, v)


def _sum_adamw_many(parts, ws, ms, vs, name):
    n = len(ws)

    def body(*refs):
        ins, outs = refs[:4 * n], refs[4 * n:]
        for i in range(n):
            g = ins[i][0]
            for j in range(1, N_DEV):
                g = g + ins[i][j]
            d, m, v = _adam_math(ins[n + i][...], g, ins[2 * n + i][...], ins[3 * n + i][...])
            outs[i][...], outs[n + i][...], outs[2 * n + i][...], outs[3 * n + i][...] = g, d, m, v

    shapes = [jax.ShapeDtypeStruct(w.shape, F32) for w in ws]
    out = pl.pallas_call(body, name=name, out_shape=shapes * 4, compiler_params=_cparams())(*parts, *ws, *ms, *vs)
    return out[:n], out[n:2 * n], out[2 * n:3 * n], out[3 * n:]


def _sum_adamw(parts0, parts1, w, m, v, name):
    P, R, C = parts0.shape
    tr = _row_tile(R, 128)
    nt = R // tr

    def body(p0_ref, p1_ref, w_ref, m_ref, v_ref, g_ref, d_ref, mo_ref, vo_ref):
        def emit(p_ref):
            g = p_ref[0].astype(F32)
            for j in range(1, P):
                g = g + p_ref[j].astype(F32)
            g_ref[...] = g
            d_ref[...], mo_ref[...], vo_ref[...] = _adam_math(w_ref[...], g, m_ref[...], v_ref[...])

        @pl.when(pl.program_id(0) == 0)
        def _():
            emit(p0_ref)

        @pl.when(pl.program_id(0) == 1)
        def _():
            emit(p1_ref)

    blk = pl.BlockSpec((tr, C), lambda l, i: (l * nt + i, 0))
    return pl.pallas_call(
        body, name=name, grid=(DEPTH, nt),
        in_specs=[pl.BlockSpec((P, tr, C), lambda l, i: (0, i * (1 - l) + (nt - 1) * l, 0)),
                  pl.BlockSpec((P, tr, C), lambda l, i: (0, i * l, 0)), blk, blk, blk],
        out_specs=[blk] * 4, out_shape=[jax.ShapeDtypeStruct((DEPTH * R, C), F32)] * 4,
        compiler_params=_cparams(("arbitrary", "arbitrary")))(parts0, parts1, w, m, v)


def _pair_sum(core, buf, recv, name):
    _, _, R, C = buf.shape
    tr = _row_tile(R, 128)

    def body(c_ref, a_ref, b_ref, o_ref):
        o_ref[...] = (a_ref[:, 0].astype(F32) + b_ref[...].astype(F32)).astype(BF16)

    return pl.pallas_call(
        body, name=name,
        grid_spec=pltpu.PrefetchScalarGridSpec(
            num_scalar_prefetch=1, grid=(R // tr,),
            in_specs=[pl.BlockSpec((4, 1, tr, C), lambda i, c: (0, c[0], i, 0)),
                      pl.BlockSpec((4, tr, C), lambda i, c: (0, i, 0))],
            out_specs=pl.BlockSpec((4, tr, C), lambda i, c: (0, i, 0))),
        out_shape=jax.ShapeDtypeStruct((4, R, C), BF16),
        compiler_params=_cparams(("parallel",)))(core, buf, recv)


SHARD_IN = D_IN // N_DEV


def _w_in_pieces():
    out, p = [], 0
    for a, b in _PAD_FROM:
        for j in range(N_DEV):
            lo, hi = max(a, SHARD_IN * j), min(b, SHARD_IN * (j + 1))
            if lo < hi:
                out.append((j, lo - SHARD_IN * j, hi - SHARD_IN * j, p + lo - a))
        p += b - a
    return out


def _assemble_w_in(gw, name):
    tr = 256
    nt = D // tr

    def body(x_ref, o_ref):
        for j, s0, s1, d0 in _w_in_pieces():
            o_ref[:, d0:d0 + s1 - s0] = x_ref[j, :, s0:s1]
        o_ref[:, D_IN:] = jnp.zeros((tr, NP - D_IN), gw.dtype)

    return pl.pallas_call(
        body, name=name, grid=(nt,),
        in_specs=[pl.BlockSpec((N_DEV, tr, SHARD_IN), lambda i: (0, i, 0))],
        out_specs=pl.BlockSpec((tr, NP), lambda i: (i, 0)),
        out_shape=jax.ShapeDtypeStruct((D, NP), gw.dtype),
        compiler_params=_cparams(("parallel",)))(gw)


def _split_w_in_grad(g, name):
    tr = 256

    def body(g_ref, o_ref):
        for j, s0, s1, d0 in _w_in_pieces():
            o_ref[j, :, s0:s1] = g_ref[:, d0:d0 + s1 - s0].astype(BF16)

    return pl.pallas_call(
        body, name=name, grid=(D // tr,),
        in_specs=[pl.BlockSpec((tr, NP), lambda i: (i, 0))],
        out_specs=pl.BlockSpec((N_DEV, tr, SHARD_IN), lambda i: (0, i, 0)),
        out_shape=jax.ShapeDtypeStruct((N_DEV, D, SHARD_IN), BF16),
        compiler_params=_cparams(("parallel",)))(g)


def _mesh_pos():
    return lax.axis_index("x"), lax.axis_index("y"), lax.axis_index("c")


def _launch(copies, peers, bufs, out_structs, sems, name, collective_id):
    n = len(bufs)
    if collective_id is None:
        anyspec = pl.BlockSpec(memory_space=pl.ANY)
        return list(pl.pallas_call(
            lambda *refs: copies(refs[:n], refs[n:n + len(out_structs)], *refs[n + len(out_structs):]),
            name=name, in_specs=[anyspec] * n, out_specs=[anyspec] * len(out_structs), out_shape=list(out_structs),
            scratch_shapes=list(sems))(*bufs))
    ins = [jax.new_ref(b, memory_space=pltpu.MemorySpace.HBM) for b in bufs]
    outs = [jax.empty_ref(s, memory_space=pltpu.MemorySpace.HBM) for s in out_structs]

    @pl.kernel(mesh=plsc.ScalarSubcoreMesh(axis_name="sequencer", num_cores=1), name=name, scratch_types=tuple(sems),
               compiler_params=pltpu.CompilerParams(collective_id=collective_id))
    def on_sequencer(*sem_refs):
        barrier = pltpu.get_barrier_semaphore()
        targets = peers()
        for p in targets:
            pl.semaphore_signal(barrier, inc=1, device_id=p, device_id_type=pl.DeviceIdType.MESH)
        pl.semaphore_wait(barrier, len(targets))
        copies(ins, outs, *sem_refs)

    on_sequencer()
    return [r[...] for r in outs]


def _all_gather(blocks, name, collective_id=None):
    n = len(blocks)

    def peers():
        x, y, c = _mesh_pos()
        return [(x, y, 1 - c), (1 - x, y, c), (x, 1 - y, c), (1 - x, 1 - y, c)]

    def copies(ins, outs, send_sems, recv_sems, local_sems):
        x, y, c = _mesh_pos()
        me, sibling = (x, y, c), (x, y, 1 - c)
        chips = [(1 - x, y), (x, 1 - y), (1 - x, 1 - y)]
        idx = lambda p: 4 * p[0] + 2 * p[1] + p[2]

        def copy(a, k, block, to, src=None):
            dst = outs[a].at[idx(block)]
            return pltpu.make_async_remote_copy(
                src_ref=dst if src is None else src, dst_ref=dst, send_sem=send_sems.at[a, k],
                recv_sem=recv_sems.at[a, k], device_id=to, device_id_type=pl.DeviceIdType.MESH)

        mine = [pltpu.make_async_copy(ins[a], outs[a].at[idx(me)], local_sems.at[a]) for a in range(n)]
        for cp in mine:
            cp.start()
        first = []
        for a in range(n):
            first.append(copy(a, 0, me, sibling, src=ins[a]))
            first += [copy(a, 1 + j, me, (*chip, c), src=ins[a]) for j, chip in enumerate(chips)]
        for cp in first:
            cp.start()
        passed = []
        for j, chip in enumerate(chips):
            for a in range(n):
                copy(a, 1 + j, (*chip, c), me).wait_recv()
                cp = copy(a, 4 + j, (*chip, c), sibling)
                cp.start()
                passed.append(cp)
        for a in range(n):
            copy(a, 0, sibling, me).wait_recv()
            for j, chip in enumerate(chips):
                copy(a, 4 + j, (*chip, 1 - c), me).wait_recv()
        for cp in first + passed:
            cp.wait_send()
        for cp in mine:
            cp.wait()

    return _launch(copies, peers, blocks, [jax.ShapeDtypeStruct((N_DEV,) + b.shape, b.dtype) for b in blocks],
                   [pltpu.SemaphoreType.DMA((n, 7)), pltpu.SemaphoreType.DMA((n, 7)), pltpu.SemaphoreType.DMA((n,))],
                   name, collective_id)


def _exchange_core(bufs, name, collective_id=None):
    n = len(bufs)

    def peers():
        x, y, c = _mesh_pos()
        return [(x, y, 1 - c)]

    def copies(ins, outs, send_sems, recv_sems):
        x, y, c = _mesh_pos()
        started = []
        for a in range(n):
            for q in range(4):
                cp = pltpu.make_async_remote_copy(
                    src_ref=ins[a].at[q, 1 - c], dst_ref=outs[a].at[q], send_sem=send_sems.at[a, q],
                    recv_sem=recv_sems.at[a, q], device_id=(x, y, 1 - c), device_id_type=pl.DeviceIdType.MESH)
                cp.start()
                started.append(cp)
        for cp in started:
            cp.wait()

    return _launch(copies, peers, bufs, [jax.ShapeDtypeStruct((4,) + b.shape[2:], b.dtype) for b in bufs],
                   [pltpu.SemaphoreType.DMA((n, 4)), pltpu.SemaphoreType.DMA((n, 4))], name, collective_id)


def _exchange_chips(bufs, name, collective_id=None):
    n = len(bufs)

    def peers():
        x, y, c = _mesh_pos()
        return [(1 - x, y, c), (x, 1 - y, c), (1 - x, 1 - y, c)]

    def copies(ins, outs, send_sems, recv_sems, local_sems):
        x, y, c = _mesh_pos()
        chip = 2 * x + y
        local = [pltpu.make_async_copy(ins[a].at[chip], outs[a].at[chip], local_sems.at[a]) for a in range(n)]
        for cp in local:
            cp.start()
        started = []
        for k in range(1, 4):
            px = 1 - x if k & 2 else x
            py = 1 - y if k & 1 else y
            for a in range(n):
                cp = pltpu.make_async_remote_copy(
                    src_ref=ins[a].at[2 * px + py], dst_ref=outs[a].at[chip], send_sem=send_sems.at[a, k - 1],
                    recv_sem=recv_sems.at[a, k - 1], device_id=(px, py, c), device_id_type=pl.DeviceIdType.MESH)
                cp.start()
                started.append(cp)
        for cp in started:
            cp.wait()
        for cp in local:
            cp.wait()

    return _launch(copies, peers, bufs, [jax.ShapeDtypeStruct(b.shape, b.dtype) for b in bufs],
                   [pltpu.SemaphoreType.DMA((n, 3)), pltpu.SemaphoreType.DMA((n, 3)), pltpu.SemaphoreType.DMA((n,))],
                   name, collective_id)


_SMALL = ("b_ada", "norm_g", "q_norm_g", "k_norm_g", "sinks", "dw_b", "ln_g", "ln_b", "pw2_b", "a_log", "dt_bias",
          "dn_norm_g", "dw_w", "sconv_w")


def _lane4(v):
    return jnp.pad(v, (0, 124)).reshape(1, 128)


def kernel(x, c, w_ada, b_ada, norm_g, w_in, q_norm_g, k_norm_g, sinks, dw_w, dw_b, ln_g, ln_b, pw2_w, pw2_b, sconv_w, a_log, dt_bias, dn_norm_g, w_proj_a, w_proj_b, w_proj_c, w_out, loss_target, m_w_ada, m_b_ada, m_norm_g, m_w_in, m_q_norm_g, m_k_norm_g, m_sinks, m_dw_w, m_dw_b, m_ln_g, m_ln_b, m_pw2_w, m_pw2_b, m_sconv_w, m_a_log, m_dt_bias, m_dn_norm_g, m_w_proj_a, m_w_proj_b, m_w_proj_c, m_w_out, v_w_ada, v_b_ada, v_norm_g, v_w_in, v_q_norm_g, v_k_norm_g, v_sinks, v_dw_w, v_dw_b, v_ln_g, v_ln_b, v_pw2_w, v_pw2_b, v_sconv_w, v_a_log, v_dt_bias, v_dn_norm_g, v_w_proj_a, v_w_proj_b, v_w_proj_c, v_w_out):
    T = x.shape[1]
    nc = T // CHUNK
    xi, yi, ci = _mesh_pos()
    me = 4 * xi + 2 * yi + ci
    big_w = (w_in, pw2_w, w_proj_a, w_proj_b, w_proj_c, w_out)
    big_m = (m_w_in, m_pw2_w, m_w_proj_a, m_w_proj_b, m_w_proj_c, m_w_out)
    big_v = (v_w_in, v_pw2_w, v_w_proj_a, v_w_proj_b, v_w_proj_c, v_w_out)

    ada_cols = w_ada.shape[2]
    dw_cols, sc_cols = dw_w.shape[2], sconv_w.shape[2]
    flat2 = lambda a: a.reshape(-1, a.shape[-1])
    big16 = [[a[l].astype(BF16) for l in range(DEPTH)] for a in big_w]
    c_all, gdw, gsc = _all_gather([c, dw_w, sconv_w], "gather_small", collective_id=0)
    (gw_in0,) = _all_gather([big16[0][0]], "gather_w_in0", collective_id=7)
    c_all = c_all.reshape(N_DEV, D)
    dw_f = gdw.transpose(1, 2, 0, 3).reshape(DEPTH, CONV_K, 512)
    sc_f = gsc.transpose(1, 2, 0, 3).reshape(DEPTH, DN_K, QKV_C)

    b_my = lax.dynamic_slice(b_ada, (0, me * ada_cols), (DEPTH, ada_cols))
    mod_part = _ada_fwd(c_all, w_ada, b_my, "ada_fwd")
    (gmod,) = _all_gather([mod_part.reshape(-1, 128)], "gather_mod")

    rest0 = [a[0] for a in big16[1:]]
    all1 = [a[1] for a in big16]
    (rest0, all1), gmod = lax.optimization_barrier(((rest0, all1), gmod))
    got0 = [gw_in0] + _all_gather(rest0, "gather_rest0", collective_id=1)
    got1 = _all_gather(all1, "gather_weights1", collective_id=6)
    wp, pw2_f, wa_f, wb_f, wc_f, wo_f = [], [], [], [], [], []
    for l, (gw_in, gpw2, gpa, gpb, gpc, gwo) in enumerate((got0, got1)):
        wp.append(_assemble_w_in(gw_in, f"assemble_w_in{l}"))
        pw2_f.append(gpw2.reshape(512, 512))
        for dst, g in ((wa_f, gpa), (wb_f, gpb), (wc_f, gpc)):
            dst.append(g.transpose(1, 0, 2).reshape(512, D))
        wo_f.append(gwo.reshape(D, D))
    mod_all = gmod.reshape(N_DEV, DEPTH, N_DEV, ada_cols).transpose(1, 2, 0, 3).reshape(DEPTH, N_DEV, 3 * D)
    mod = lax.dynamic_index_in_dim(mod_all, me, axis=1, keepdims=False)
    shift, scale, gate = mod[:, :D], mod[:, D:2 * D], mod[:, 2 * D:]

    xs, saved = [x[0]], []
    for l in range(DEPTH):
        xl = xs[-1]
        h = _norm_fwd(xl, norm_g[l], scale[l], shift[l], f"norm_fwd{l}")
        proj = _mm(h, wp[l], tm=min(2048, T), tn=1152, tk=D, name=f"in_proj{l}")
        ya = _attn_fwd(proj, q_norm_g[l], k_norm_g[l], sinks[l], f"attn_fwd{l}")
        yb = _conf_fwd(proj, dw_f[l], dw_b[l], ln_g[l], ln_b[l], pw2_f[l], pw2_b[l], f"conf_fwd{l}")
        alv, dtv, dng = _lane4(a_log[l]), _lane4(dt_bias[l]), _row(dn_norm_g[l])
        qd, kd, vd, gb = _gdn_prep_fwd(proj, sc_f[l], alv, dtv, f"gdn_prep_fwd{l}")
        gbt = gb[:, :8].reshape(nc, CHUNK, 8).transpose(0, 2, 1)
        u, w, qe, ke, pm, tmi, eg = _gdn_chunk_fwd(qd, kd, vd, gb, gbt, f"gdn_chunk_fwd{l}")
        yc, o, vn, ss = _gdn_scan_fwd(u, w, qe, ke, pm, eg, proj, dng, f"gdn_scan_fwd{l}")
        xs.append(_merge_fwd(xl, proj, ya, yb, yc, wa_f[l], wb_f[l], wc_f[l], wo_f[l], gate[l], f"merge_fwd{l}"))
        saved.append((h, proj, ya, yb, yc, qd, kd, vd, gb, gbt, ss, alv, dtv, dng, w, qe, ke, pm, tmi, eg, o, vn))

    dout, lsum = _loss_head(xs[-1], loss_target[0], "loss_head")

    small = {name: [None] * DEPTH for name in _SMALL}
    big_parts = [None] * DEPTH
    core = jnp.reshape(ci, (1,)).astype(jnp.int32)
    for l in reversed(range(DEPTH)):
        h, proj, ya, yb, yc, qd, kd, vd, gb, gbt, ss, alv, dtv, dng, w, qe, ke, pm, tmi, eg, o, vn = saved[l]
        dproj, dya, dyb, dyc, g_wa, g_wb, g_wc, g_wo, g_gate = _merge_bwd(
            dout, proj, ya, yb, yc, wa_f[l], wb_f[l], wc_f[l], wo_f[l], gate[l], f"merge_bwd{l}")
        dproj, dkv_a, g_q, g_k, g_s = _attn_bwd(dproj, proj, dya, q_norm_g[l], k_norm_g[l], sinks[l], f"attn_bwd{l}")
        dproj, du1, g_pw2, st_b = _conf_bwd1(dproj, proj, dyb, dw_f[l], dw_b[l], ln_g[l], ln_b[l], pw2_f[l], pw2_b[l],
                                             f"conf_bwd_a{l}")
        dproj, g_dw = _conf_bwd2(dproj, proj, du1, dw_f[l], f"conf_bwd_b{l}")
        dproj, do, dvn, dsn, g_dn = _gdn_scan_bwd(dproj, w, qe, ke, pm, eg, o, proj, dyc, dng, f"gdn_scan_bwd{l}")
        dqd, dkd, dvd, dgb = _gdn_chunk_grad(qd, kd, vd, gb, gbt, tmi, ss, dsn, do, dvn, vn, f"gdn_chunk_bwd{l}")
        dproj, dpre, st_c = _gdn_prep_bwd1(dproj, proj, dqd, dkd, dvd, dgb, dkv_a, sc_f[l], alv, dtv,
                                           f"gdn_prep_bwd_a{l}")
        dproj, g_sc = _gdn_prep_bwd2(dproj, proj, dpre, sc_f[l], f"gdn_prep_bwd_b{l}")
        g_wp = _mm(h, dproj, ta=True, tm=D, tn=1152, tk=min(2048, T), name=f"d_w_in{l}")
        by_dest = [_split_w_in_grad(g_wp, f"split_w_in_grad{l}"), g_pw2.reshape(N_DEV, -1, 512).astype(BF16)]
        by_dest += [g.reshape(512, N_DEV, -1).transpose(1, 0, 2).astype(BF16) for g in (g_wa, g_wb, g_wc)]
        by_dest.append(g_wo.reshape(N_DEV, -1, D).astype(BF16))
        by_dest = [b.reshape(4, 2, -1, b.shape[-1]) for b in by_dest]
        if l < DEPTH - 1:
            by_dest, big_parts[l + 1] = lax.optimization_barrier((by_dest, big_parts[l + 1]))
        from_sibling = _exchange_core(by_dest, f"exchange_grads_core{l}", collective_id=2 + 2 * l)

        def input_grad(dproj, dout):
            dh = _mm(dproj, wp[l], tb=True, tm=min(1024, T), tn=D, tk=2688, name=f"d_h{l}")
            return _norm_bwd(dh, xs[l], dout, norm_g[l], scale[l], f"norm_bwd{l}")

        if l > 0:
            dout, st_n = input_grad(dproj, dout)
            from_sibling, dout = lax.optimization_barrier((from_sibling, dout))
        else:
            from_sibling, _ = lax.optimization_barrier((from_sibling, (flat2(m_w_in), flat2(v_w_in))))
        chip_sums = [_pair_sum(core, b, r, f"pair_sum{l}_{i}") for i, (b, r) in enumerate(zip(by_dest, from_sibling))]
        big_parts[l] = _exchange_chips(chip_sums, f"exchange_grads_chips{l}", collective_id=3 + 2 * l)
        if l > 0:
            dout, chip_sums = lax.optimization_barrier((dout, chip_sums))
        else:
            dproj, chip_sums = lax.optimization_barrier((dproj, chip_sums))
            dout, st_n = input_grad(dproj, dout)
        for name, g in (("b_ada", jnp.concatenate([st_n[0], st_n[1], g_gate[0]])), ("norm_g", st_n[2]),
                        ("q_norm_g", g_q.reshape(ATT_HEADS, ATT_HD).sum(0)), ("k_norm_g", g_k.reshape(2, ATT_HD).sum(0)),
                        ("sinks", g_s[0]), ("dw_b", st_b[3]),
                        ("ln_g", st_b[1]), ("ln_b", st_b[2]), ("pw2_b", st_b[0]), ("a_log", st_c[0, :4]),
                        ("dt_bias", st_c[1, :4]), ("dn_norm_g", g_dn[0]), ("dw_w", g_dw[:CONV_K]),
                        ("sconv_w", g_sc[:DN_K])):
            small[name][l] = g
    grad_x = dout[None]

    big_parts, dout = lax.optimization_barrier((big_parts, dout))
    sum_big = lambda i: _sum_adamw(big_parts[0][i], big_parts[1][i], flat2(big_w[i]), flat2(big_m[i]),
                                   flat2(big_v[i]), f"sum_adamw{i}")
    res = [sum_big(0)]

    names = list(_SMALL)
    gathered = _all_gather([jnp.stack(small[n]) for n in names] + [lsum], "gather_small_grads")
    gparts = dict(zip(names, gathered))
    loss = 0.5 * jnp.sum(jnp.sum(gathered[-1], axis=(1, 2))) / D
    dmod_my = lax.dynamic_slice(gparts["b_ada"], (0, 0, me * ada_cols), (N_DEV, DEPTH, ada_cols)).transpose(1, 0, 2)
    g_w_ada = _ada_bwd(c_all, dmod_my, "ada_bwd")
    gparts["dw_w"] = lax.dynamic_slice(gparts["dw_w"], (0, 0, 0, me * dw_cols), (N_DEV, DEPTH, CONV_K, dw_cols))
    gparts["sconv_w"] = lax.dynamic_slice(gparts["sconv_w"], (0, 0, 0, me * sc_cols), (N_DEV, DEPTH, DN_K, sc_cols))
    env = dict(b_ada=(b_ada, m_b_ada, v_b_ada), norm_g=(norm_g, m_norm_g, v_norm_g),
               q_norm_g=(q_norm_g, m_q_norm_g, v_q_norm_g), k_norm_g=(k_norm_g, m_k_norm_g, v_k_norm_g),
               sinks=(sinks, m_sinks, v_sinks), dw_b=(dw_b, m_dw_b, v_dw_b), ln_g=(ln_g, m_ln_g, v_ln_g),
               ln_b=(ln_b, m_ln_b, v_ln_b), pw2_b=(pw2_b, m_pw2_b, v_pw2_b), a_log=(a_log, m_a_log, v_a_log),
               dt_bias=(dt_bias, m_dt_bias, v_dt_bias), dn_norm_g=(dn_norm_g, m_dn_norm_g, v_dn_norm_g),
               dw_w=(dw_w, m_dw_w, v_dw_w), sconv_w=(sconv_w, m_sconv_w, v_sconv_w))
    upd = _sum_adamw_many([gparts[n] for n in names], [env[n][0] for n in names], [env[n][1] for n in names],
                          [env[n][2] for n in names], "sum_adamw_small")

    d_ada, nm_ada, nv_ada = (u.reshape(w_ada.shape) for u in
                             _adamw(flat2(w_ada), flat2(g_w_ada), flat2(m_w_ada), flat2(v_w_ada), "adamw_w_ada"))

    g_small, d_small, m_small, v_small = (dict(zip(names, u)) for u in upd)
    res += [sum_big(i) for i in range(1, len(big_w))]
    g_big, d_big, m_big, v_big =([r[k].reshape(w.shape) for r, w in zip(res, big_w)] for k in range(4))

    order = ("w_ada", "b_ada", "norm_g", "w_in", "q_norm_g", "k_norm_g", "sinks", "dw_w", "dw_b", "ln_g", "ln_b",
             "pw2_w", "pw2_b", "sconv_w", "a_log", "dt_bias", "dn_norm_g", "w_proj_a", "w_proj_b", "w_proj_c", "w_out")
    big_names = ("w_in", "pw2_w", "w_proj_a", "w_proj_b", "w_proj_c", "w_out")

    def pick(kind):
        src_small = (g_small, d_small, m_small, v_small)[kind]
        src_big = (g_big, d_big, m_big, v_big)[kind]
        src_ada = (g_w_ada, d_ada, nm_ada, nv_ada)[kind]
        return [src_ada if n == "w_ada" else src_big[big_names.index(n)] if n in big_names else src_small[n]
                for n in order]

    return (loss, grad_x, *pick(0), *pick(1), *pick(2), *pick(3))
```

```python
import functools
import math

import jax
import jax.numpy as jnp
import numpy as np
from jax import lax
from jax.experimental import pallas as pl
from jax.experimental.pallas import tpu as pltpu
from jax.experimental.pallas import tpu_sc as plsc

F32 = jnp.float32
BF16 = jnp.bfloat16
HI = lax.Precision.HIGHEST

N_DEV = 8
D = 1024
DEPTH = 2
EPS = 1e-6
NEG_INF = -1e30
WINDOW = 128
ATT_HEADS = 8
ATT_HD = 64
CONV_K = 31
DN_HEADS = 4
DN_K = 4
CHUNK = 64
D_IN = 7944
VMEM_LIMIT = 56 * 1024 * 1024

C_MG, C_QA, C_ZA, C_ZB, C_QC, C_KC, C_VC, C_GV, C_GG, C_ZC, C_KA, C_VA, C_AB, NP = (
    0, 3072, 3584, 4096, 4608, 5120, 5632, 6144, 6656, 7168, 7680, 7808, 7936, 8064)
_PAD_FROM = ((4872, 7944), (0, 512), (768, 1280), (2304, 2816), (2816, 4352), (1280, 2304), (4360, 4872),
             (512, 768), (4352, 4360))

ALIBI = tuple(float(2.0 ** (-8.0 * (h + 1) / ATT_HEADS)) for h in range(ATT_HEADS))

ADAM_LR, ADAM_B1, ADAM_B2, ADAM_EPS, ADAM_WD, ADAM_STEP = 0.001, 0.9, 0.999, 1e-08, 0.01, 10


def _cparams(sem=None):
    return pltpu.CompilerParams(dimension_semantics=sem, vmem_limit_bytes=VMEM_LIMIT)


def _sig(x):
    return jax.nn.sigmoid(x)


def _silu(x):
    return x * _sig(x)


def _dsilu(x):
    s = _sig(x)
    return s * (1.0 + x * (1.0 - s))


def _dot(a, b, dims=((1,), (0,)), precision=None):
    return lax.dot_general(a, b, (dims, ((), ())), preferred_element_type=F32, precision=precision)


def _bdot(a, b, dims=((1,), (0,))):
    return _dot(a.astype(BF16), b.astype(BF16), dims)


NN, NT, TN = ((1,), (0,)), ((1,), (1,)), ((0,), (0,))


def _row(v):
    return v.reshape(1, -1)


def _mm(a, b, *, ta=False, tb=False, tm, tn, tk, name):
    M, K = (a.shape[1], a.shape[0]) if ta else a.shape
    N = b.shape[0] if tb else b.shape[1]
    assert M % tm == 0 and N % tn == 0 and K % tk == 0, (M, N, K, tm, tn, tk)
    nk = K // tk
    dims = ((0 if ta else 1,), (1 if tb else 0,))

    def body(a_ref, b_ref, o_ref):
        k = pl.program_id(2)
        part = _bdot(a_ref[...], b_ref[...], dims)

        @pl.when(k == 0)
        def _():
            o_ref[...] = part

        @pl.when(k > 0)
        def _():
            o_ref[...] += part

    a_spec = pl.BlockSpec((tk, tm), lambda i, j, k: (k, i)) if ta else pl.BlockSpec((tm, tk), lambda i, j, k: (i, k))
    b_spec = pl.BlockSpec((tn, tk), lambda i, j, k: (j, k)) if tb else pl.BlockSpec((tk, tn), lambda i, j, k: (k, j))
    return pl.pallas_call(
        body, name=name, grid=(M // tm, N // tn, nk),
        in_specs=[a_spec, b_spec], out_specs=pl.BlockSpec((tm, tn), lambda i, j, k: (i, j)),
        out_shape=jax.ShapeDtypeStruct((M, N), F32),
        compiler_params=_cparams(("parallel", "parallel", "arbitrary")),
    )(a, b)


def _norm_fwd(x, norm_g, scale, shift, name):
    T = x.shape[0]
    tm = min(512, T)

    def body(x_ref, g_ref, sc_ref, sh_ref, h_ref):
        xv = x_ref[...]
        r = lax.rsqrt(jnp.mean(xv * xv, axis=-1, keepdims=True) + EPS)
        h_ref[...] = ((xv * r) * g_ref[...] * (1.0 + sc_ref[...]) + sh_ref[...]).astype(BF16)

    vec = pl.BlockSpec((1, D), lambda i: (0, 0))
    return pl.pallas_call(
        body, name=name, grid=(T // tm,),
        in_specs=[pl.BlockSpec((tm, D), lambda i: (i, 0)), vec, vec, vec],
        out_specs=pl.BlockSpec((tm, D), lambda i: (i, 0)),
        out_shape=jax.ShapeDtypeStruct((T, D), BF16),
        compiler_params=_cparams(("parallel",)),
    )(x, _row(norm_g), _row(scale), _row(shift))


def _norm_bwd(dh, x, dres, norm_g, scale, name):
    T = x.shape[0]
    tm = min(512, T)

    def body(dh_ref, x_ref, dr_ref, g_ref, sc_ref, dx_ref, st_ref):
        i = pl.program_id(0)
        xv, dhv = x_ref[...], dh_ref[...]
        r = lax.rsqrt(jnp.mean(xv * xv, axis=-1, keepdims=True) + EPS)
        xh = xv * r
        g, s1 = g_ref[...], 1.0 + sc_ref[...]
        dxh = dhv * (g * s1)
        dx_ref[...] = dr_ref[...] + r * (dxh - xh * jnp.mean(dxh * xh, axis=-1, keepdims=True))
        dhx = dhv * xh
        upd = jnp.concatenate([jnp.sum(dhv, axis=0, keepdims=True), jnp.sum(dhx * g, axis=0, keepdims=True),
                               jnp.sum(dhx * s1, axis=0, keepdims=True), jnp.zeros((5, D), F32)], axis=0)

        @pl.when(i == 0)
        def _():
            st_ref[...] = upd

        @pl.when(i > 0)
        def _():
            st_ref[...] += upd

    vec = pl.BlockSpec((1, D), lambda i: (0, 0))
    blk = pl.BlockSpec((tm, D), lambda i: (i, 0))
    return pl.pallas_call(
        body, name=name, grid=(T // tm,),
        in_specs=[blk, blk, blk, vec, vec],
        out_specs=[blk, pl.BlockSpec((8, D), lambda i: (0, 0))],
        out_shape=[jax.ShapeDtypeStruct((T, D), F32), jax.ShapeDtypeStruct((8, D), F32)],
        compiler_params=_cparams(("arbitrary",)),
    )(dh, x, dres, _row(norm_g), _row(scale))


def _rms(x, g):
    r = lax.rsqrt(jnp.mean(x * x, axis=-1, keepdims=True) + EPS)
    return x * r, r


def _head_mean_matrix():
    head = np.arange(ATT_HEADS * ATT_HD) // ATT_HD
    return jnp.asarray((head[:, None] == head[None, :]) * (1.0 / ATT_HD), BF16)


def _head_rms(x, hm):
    r = lax.rsqrt(_dot_exact(x * x, hm) + EPS)
    return x * r, r


def _head_rms_bwd(dy, xh, r, g, hm):
    dxh = dy * g
    return r * (dxh - xh * _dot_exact(dxh * xh, hm)), dy * xh


def _attn_mask(n):
    qi = lax.broadcasted_iota(jnp.int32, (WINDOW, 2 * WINDOW), 0)
    kj = lax.broadcasted_iota(jnp.int32, (WINDOW, 2 * WINDOW), 1)
    dist = qi + WINDOW - kj
    valid = (dist >= 0) & (dist < WINDOW) & ((n > 0) | (kj >= WINDOW))
    return valid, dist.astype(F32)


def _attn_probs(s, h, sink, valid, distf):
    s = s - ALIBI[h] * distf
    s = jnp.where(valid, s, NEG_INF)
    m = jnp.maximum(jnp.max(s, axis=-1, keepdims=True), sink)
    p = jnp.exp(s - m)
    es = jnp.exp(sink - m)
    den = jnp.sum(p, axis=-1, keepdims=True) + es
    return p / den, es / den


def _attn_fwd(proj, q_norm_g, k_norm_g, sinks, name):
    T = proj.shape[0]
    nb = T // WINDOW

    def body(sink_ref, q_ref, z_ref, kc_ref, kp_ref, vc_ref, vp_ref, qg_ref, kg_ref, hm_ref, o_ref):
        n = pl.program_id(0)
        valid, distf = _attn_mask(n)
        k2 = jnp.concatenate([kp_ref[...], kc_ref[...]], axis=0)
        v2 = jnp.concatenate([vp_ref[...], vc_ref[...]], axis=0).astype(BF16)
        kn = (_head_rms(k2, hm_ref[:128, :128])[0] * kg_ref[...]).astype(BF16)
        qn = ((_head_rms(q_ref[...], hm_ref[...])[0] * qg_ref[...]) * (ATT_HD ** -0.5)).astype(BF16)

        def head(h):
            sl, gsl = slice(64 * h, 64 * h + 64), slice(64 * (h // 4), 64 * (h // 4) + 64)
            s = _dot(qn[:, sl], kn[:, gsl], NT)
            yield
            p, _ = _attn_probs(s, h, sink_ref[h], valid, distf)
            o_ref[:, sl] = _dot(p.astype(BF16), v2[:, gsl])
            yield

        _lockstep([head(h) for h in range(ATT_HEADS)])
        o_ref[...] = o_ref[...] * _silu(z_ref[...])

    prev = lambda n: jnp.maximum(n - 1, 0)
    return pl.pallas_call(
        body, name=name, grid=(nb,),
        in_specs=[pl.BlockSpec(memory_space=pltpu.SMEM),
                  pl.BlockSpec((WINDOW, 512), lambda n: (n, C_QA // 512)),
                  pl.BlockSpec((WINDOW, 512), lambda n: (n, C_ZA // 512)),
                  pl.BlockSpec((WINDOW, 128), lambda n: (n, C_KA // 128)),
                  pl.BlockSpec((WINDOW, 128), lambda n: (prev(n), C_KA // 128)),
                  pl.BlockSpec((WINDOW, 128), lambda n: (n, C_VA // 128)),
                  pl.BlockSpec((WINDOW, 128), lambda n: (prev(n), C_VA // 128)),
                  pl.BlockSpec((1, 512), lambda n: (0, 0)), pl.BlockSpec((1, 128), lambda n: (0, 0)),
                  pl.BlockSpec((512, 512), lambda n: (0, 0))],
        out_specs=pl.BlockSpec((WINDOW, 512), lambda n: (n, 0)),
        out_shape=jax.ShapeDtypeStruct((T, 512), F32),
        compiler_params=_cparams(("parallel",)),
    )(sinks, proj, proj, proj, proj, proj, proj, _row(jnp.tile(q_norm_g, ATT_HEADS)), _row(jnp.tile(k_norm_g, 2)),
      _head_mean_matrix())


def _rms_bwd(dy, xh, r, g):
    dxh = dy * g
    return r * (dxh - xh * jnp.mean(dxh * xh, axis=-1, keepdims=True)), dy * xh


def _attn_bwd(dproj, proj, dya, q_norm_g, k_norm_g, sinks, name):
    T = proj.shape[0]
    nb = T // WINDOW

    def body(sink_ref, dp_any, q_ref, z_ref, kc_ref, kp_ref, vc_ref, vp_ref, dy_ref, qg_ref, kg_ref, hm_ref,
             dqz_ref, dkv_ref, gq_ref, gk_ref, gs_ref, ck_ref, cv_ref, o_sc, dq_sc):
        n = pl.program_id(0)

        @pl.when(n == 0)
        def _():
            gq_ref[...] = jnp.zeros_like(gq_ref)
            gk_ref[...] = jnp.zeros_like(gk_ref)
            gs_ref[...] = jnp.zeros_like(gs_ref)
            ck_ref[...] = jnp.zeros_like(ck_ref)
            cv_ref[...] = jnp.zeros_like(cv_ref)

        lane8 = lax.broadcasted_iota(jnp.int32, (1, 8), 1)

        @pl.when(n < nb)
        def _():
            valid, distf = _attn_mask(n)
            k2 = jnp.concatenate([kp_ref[...], kc_ref[...]], axis=0)
            v2 = jnp.concatenate([vp_ref[...], vc_ref[...]], axis=0).astype(BF16)
            kn = (_head_rms(k2, hm_ref[:128, :128])[0] * kg_ref[...]).astype(BF16)
            qh, qr = _head_rms(q_ref[...], hm_ref[...])
            qn = ((qh * qg_ref[...]) * (ATT_HD ** -0.5)).astype(BF16)
            zs = z_ref[...]
            do_all = dy_ref[...] * _silu(zs)
            dob_all = do_all.astype(BF16)

            def head(h):
                sl, gsl = slice(64 * h, 64 * h + 64), slice(64 * (h // 4), 64 * (h // 4) + 64)
                s = _dot(qn[:, sl], kn[:, gsl], NT)
                dpm = _dot(dob_all[:, sl], v2[:, gsl], NT)
                yield
                p, ps = _attn_probs(s, h, sink_ref[h], valid, distf)
                pb = p.astype(BF16)
                o_sc[:, sl] = _dot(pb, v2[:, gsl])
                dvg = _dot(pb, dob_all[:, sl], TN)
                delta = jnp.sum(p * dpm, axis=-1, keepdims=True)
                ds = (p * (dpm - delta)).astype(BF16)
                gs = jnp.where(lane8 == h, -jnp.sum(ps * delta, axis=0, keepdims=True), 0.0)
                yield
                dkn = _dot(ds, qn[:, sl], TN)
                dq_sc[:, sl] = _dot(ds, kn[:, gsl])
                yield
                return dkn, dvg, gs

            res = _lockstep([head(h) for h in range(ATT_HEADS)])
            dqz_ref[:, 512:] = (dy_ref[...] * o_sc[...] * _dsilu(zs)).astype(BF16)
            dq, gq = _head_rms_bwd(dq_sc[...] * (ATT_HD ** -0.5), qh, qr, qg_ref[...], hm_ref[...])
            dqz_ref[:, :512] = dq.astype(BF16)
            gq_acc = jnp.sum(gq, axis=0, keepdims=True)
            gs_acc = sum(r[2] for r in res[1:]) + res[0][2]
            for g in range(2):
                dkn = (res[4 * g][0] + res[4 * g + 1][0]) + (res[4 * g + 2][0] + res[4 * g + 3][0])
                dvg = (res[4 * g][1] + res[4 * g + 1][1]) + (res[4 * g + 2][1] + res[4 * g + 3][1])
                ksl = slice(64 * g, 64 * g + 64)
                vsl = slice(128 + 64 * g, 128 + 64 * g + 64)
                dkv_ref[:, ksl] = ck_ref[:, ksl] + dkn[:WINDOW]
                dkv_ref[:, vsl] = cv_ref[:, ksl] + dvg[:WINDOW]
                ck_ref[:, ksl] = dkn[WINDOW:]
                cv_ref[:, ksl] = dvg[WINDOW:]
            gq_ref[...] += gq_acc
            gs_ref[...] += gs_acc

        @pl.when(n == nb)
        def _():
            dkv_ref[:, :128] = ck_ref[...]
            dkv_ref[:, 128:] = cv_ref[...]

        @pl.when(n > 0)
        def _():
            hm = hm_ref[:128, :128]
            kh, kr = _head_rms(kp_ref[...], hm)
            dk, gk = _head_rms_bwd(dkv_ref[:, :128], kh, kr, kg_ref[...], hm)
            dkv_ref[:, :128] = dk
            gk_ref[...] += jnp.sum(gk, axis=0, keepdims=True)

    cur = lambda n: jnp.minimum(n, nb - 1)
    prev = lambda n: jnp.maximum(n - 1, 0)
    small = lambda w: pl.BlockSpec((1, w), lambda n: (0, 0))
    return pl.pallas_call(
        body, name=name, grid=(nb + 1,),
        in_specs=[pl.BlockSpec(memory_space=pltpu.SMEM), pl.BlockSpec(memory_space=pl.ANY),
                  pl.BlockSpec((WINDOW, 512), lambda n: (cur(n), C_QA // 512)),
                  pl.BlockSpec((WINDOW, 512), lambda n: (cur(n), C_ZA // 512)),
                  pl.BlockSpec((WINDOW, 128), lambda n: (cur(n), C_KA // 128)),
                  pl.BlockSpec((WINDOW, 128), lambda n: (prev(n), C_KA // 128)),
                  pl.BlockSpec((WINDOW, 128), lambda n: (cur(n), C_VA // 128)),
                  pl.BlockSpec((WINDOW, 128), lambda n: (prev(n), C_VA // 128)),
                  pl.BlockSpec((WINDOW, 512), lambda n: (cur(n), 0)),
                  small(512), small(128), pl.BlockSpec((512, 512), lambda n: (0, 0))],
        out_specs=[pl.BlockSpec((WINDOW, 1024), lambda n: (cur(n), C_QA // 1024)),
                   pl.BlockSpec((WINDOW, 256), lambda n: (prev(n), 0)),
                   small(512), small(128), small(8)],
        out_shape=[jax.ShapeDtypeStruct(dproj.shape, BF16), jax.ShapeDtypeStruct((T, 256), F32),
                   jax.ShapeDtypeStruct((1, 512), F32), jax.ShapeDtypeStruct((1, 128), F32),
                   jax.ShapeDtypeStruct((1, 8), F32)],
        scratch_shapes=[pltpu.VMEM((WINDOW, 128), F32), pltpu.VMEM((WINDOW, 128), F32),
                        pltpu.VMEM((WINDOW, 512), F32), pltpu.VMEM((WINDOW, 512), F32)],
        input_output_aliases={1: 0},
        compiler_params=_cparams(("arbitrary",)),
    )(sinks, dproj, proj, proj, proj, proj, proj, proj, dya, _row(jnp.tile(q_norm_g, ATT_HEADS)),
      _row(jnp.tile(k_norm_g, 2)), _head_mean_matrix())


HALO_B = 32


def _conf_specs(T, tm):
    r = tm // HALO_B
    cur = lambda c: pl.BlockSpec((tm, 512), lambda i: (i, c // 512))
    prev = lambda c: pl.BlockSpec((HALO_B, 512), lambda i: (jnp.maximum(i * r - 1, 0), c // 512))
    return cur, prev


SUB = 8
ROW_CHUNK = 64


def _shifted_copies(ext_ref, sh_ref):
    total = ext_ref.shape[0]
    for r in range(SUB):
        rows = total if r == 0 else total - SUB
        sh_ref[r, :rows, :] = ext_ref[pl.ds(r, rows), :]


def _taps_by_shift(offsets):
    groups = {}
    for k, o in enumerate(offsets):
        q, r = divmod(o, SUB)
        groups.setdefault(r, []).append((k, q))
    return groups


def _conv_taps(sh_ref, w_ref, offsets, out_ref, init):
    groups = _taps_by_shift(offsets)

    def chunk(ci, carry):
        r0 = pl.multiple_of(ci * ROW_CHUNK, ROW_CHUNK)
        acc = jnp.zeros((ROW_CHUNK, out_ref.shape[1]), F32) + init
        for r, taps in groups.items():
            win = sh_ref[r, pl.ds(r0, ROW_CHUNK + SUB * max(q for _, q in taps)), :]
            for k, q in taps:
                acc = acc + w_ref[k:k + 1, :] * win[SUB * q:SUB * q + ROW_CHUNK]
        out_ref[pl.ds(r0, ROW_CHUNK), :] = acc
        return carry

    lax.fori_loop(0, out_ref.shape[0] // ROW_CHUNK, chunk, 0)


def _conv_weight_grad(sh_ref, d_ref, offsets):
    tm, width = d_ref.shape
    out = [None] * len(offsets)
    for r, taps in _taps_by_shift(offsets).items():
        def chunk(ci, accs, r=r, taps=taps):
            r0 = pl.multiple_of(ci * ROW_CHUNK, ROW_CHUNK)
            d = d_ref[pl.ds(r0, ROW_CHUNK), :]
            win = sh_ref[r, pl.ds(r0, ROW_CHUNK + SUB * max(q for _, q in taps)), :]
            return tuple(a + jnp.sum((d * win[SUB * q:SUB * q + ROW_CHUNK]).reshape(ROW_CHUNK // SUB, SUB, width),
                                     axis=0) for a, (_, q) in zip(accs, taps))

        accs = lax.fori_loop(0, tm // ROW_CHUNK, chunk, tuple(jnp.zeros((SUB, width), F32) for _ in taps))
        for a, (k, _) in zip(accs, taps):
            out[k] = jnp.sum(a, axis=0, keepdims=True)
    return out


def _conf_scratch(tm):
    return [pltpu.VMEM((tm + HALO_B, 512), F32), pltpu.VMEM((SUB, tm + HALO_B, 512), F32), pltpu.VMEM((tm, 512), F32)]


def _conf_core(i, tm, gv_ref, gg_ref, gvp_ref, ggp_ref, w_ref, b_ref, lg_ref, lb_ref, pw_ref, pb_ref, ext_ref, sh_ref,
               cv_ref):
    up = gvp_ref[...] * _sig(ggp_ref[...])
    ext_ref[:HALO_B] = jnp.where(i > 0, up, 0.0)
    ext_ref[HALO_B:] = gv_ref[...] * _sig(gg_ref[...])
    _shifted_copies(ext_ref, sh_ref)
    _conv_taps(sh_ref, w_ref, [HALO_B - CONV_K + 1 + k for k in range(CONV_K)], cv_ref, b_ref[...])
    acc = cv_ref[...]
    mu = jnp.mean(acc, axis=-1, keepdims=True)
    xc = acc - mu
    rstd = lax.rsqrt(jnp.mean(xc * xc, axis=-1, keepdims=True) + EPS)
    xh = xc * rstd
    u2 = xh * lg_ref[...] + lb_ref[...]
    u3 = _silu(u2)
    ypre = _bdot(u3, pw_ref[...]) + pb_ref[...]
    return xh, rstd, u2, u3, ypre


def _conf_fwd(proj, dw_w, dw_b, ln_g, ln_b, pw2, pw2_b, name):
    T = proj.shape[0]
    tm = min(512, T)
    cur, prev = _conf_specs(T, tm)

    def body(gv_ref, gg_ref, gvp_ref, ggp_ref, zb_ref, w_ref, b_ref, lg_ref, lb_ref, pw_ref, pb_ref, o_ref, *scratch):
        i = pl.program_id(0)
        ypre = _conf_core(i, tm, gv_ref, gg_ref, gvp_ref, ggp_ref, w_ref, b_ref, lg_ref, lb_ref, pw_ref, pb_ref,
                          *scratch)[4]
        o_ref[...] = ypre * _silu(zb_ref[...])

    full = lambda s: pl.BlockSpec(s, lambda i: (0, 0))
    return pl.pallas_call(
        body, name=name, grid=(T // tm,),
        in_specs=[cur(C_GV), cur(C_GG), prev(C_GV), prev(C_GG), cur(C_ZB), full((CONV_K, 512)), full((1, 512)),
                  full((1, 512)), full((1, 512)), full((512, 512)), full((1, 512))],
        out_specs=pl.BlockSpec((tm, 512), lambda i: (i, 0)),
        out_shape=jax.ShapeDtypeStruct((T, 512), F32),
        scratch_shapes=_conf_scratch(tm),
        compiler_params=_cparams(("parallel",)),
    )(proj, proj, proj, proj, proj, dw_w, _row(dw_b), _row(ln_g), _row(ln_b), pw2, _row(pw2_b))


def _conf_bwd1(dproj, proj, dyb, dw_w, dw_b, ln_g, ln_b, pw2, pw2_b, name):
    T = proj.shape[0]
    tm = min(512, T)
    cur, prev = _conf_specs(T, tm)

    def body(dp_any, gv_ref, gg_ref, gvp_ref, ggp_ref, zb_ref, dy_ref, w_ref, b_ref, lg_ref, lb_ref, pw_ref, pb_ref,
             dzb_ref, du1_ref, gpw_ref, st_ref, *scratch):
        i = pl.program_id(0)
        xh, rstd, u2, u3, ypre = _conf_core(i, tm, gv_ref, gg_ref, gvp_ref, ggp_ref, w_ref, b_ref, lg_ref, lb_ref,
                                            pw_ref, pb_ref, *scratch)
        zb, dy = zb_ref[...], dy_ref[...]
        dzb_ref[...] = (dy * ypre * _dsilu(zb)).astype(BF16)
        dyp = dy * _silu(zb)
        du2 = _bdot(dyp, pw_ref[...], NT) * _dsilu(u2)
        dxh = du2 * lg_ref[...]
        du1 = rstd * (dxh - jnp.mean(dxh, axis=-1, keepdims=True) - xh * jnp.mean(dxh * xh, axis=-1, keepdims=True))
        du1_ref[...] = du1
        gpw = _bdot(u3, dyp, TN)
        rs = lambda a: jnp.sum(a, axis=0, keepdims=True)
        upd = jnp.concatenate([rs(dyp), rs(du2 * xh), rs(du2), rs(du1), jnp.zeros((4, 512), F32)], axis=0)

        @pl.when(i == 0)
        def _():
            gpw_ref[...] = gpw
            st_ref[...] = upd

        @pl.when(i > 0)
        def _():
            gpw_ref[...] += gpw
            st_ref[...] += upd

    full = lambda s: pl.BlockSpec(s, lambda i: (0, 0))
    blk = pl.BlockSpec((tm, 512), lambda i: (i, 0))
    return pl.pallas_call(
        body, name=name, grid=(T // tm,),
        in_specs=[pl.BlockSpec(memory_space=pl.ANY), cur(C_GV), cur(C_GG), prev(C_GV), prev(C_GG), cur(C_ZB), blk,
                  full((CONV_K, 512)), full((1, 512)), full((1, 512)), full((1, 512)), full((512, 512)), full((1, 512))],
        out_specs=[cur(C_ZB), blk, full((512, 512)), full((8, 512))],
        out_shape=[jax.ShapeDtypeStruct(dproj.shape, BF16), jax.ShapeDtypeStruct((T, 512), F32),
                   jax.ShapeDtypeStruct((512, 512), F32), jax.ShapeDtypeStruct((8, 512), F32)],
        scratch_shapes=_conf_scratch(tm),
        input_output_aliases={0: 0},
        compiler_params=_cparams(("arbitrary",)),
    )(dproj, proj, proj, proj, proj, proj, dyb, dw_w, _row(dw_b), _row(ln_g), _row(ln_b), pw2, _row(pw2_b))


def _conf_bwd2(dproj, proj, du1, dw_w, name):
    T = proj.shape[0]
    tm = min(512, T)
    nt = T // tm
    r = tm // HALO_B
    cur, prev = _conf_specs(T, tm)

    def body(dp_any, gv_ref, gg_ref, gvp_ref, ggp_ref, du_ref, dun_ref, w_ref, dglu_ref, gw_ref, ext_ref, sh_ref,
             cv_ref):
        i = pl.program_id(0)
        gv, sg = gv_ref[...], _sig(gg_ref[...])
        ext_ref[:HALO_B] = jnp.where(i > 0, gvp_ref[...] * _sig(ggp_ref[...]), 0.0)
        ext_ref[HALO_B:] = gv * sg
        _shifted_copies(ext_ref, sh_ref)
        rows = _conv_weight_grad(sh_ref, du_ref, [HALO_B - CONV_K + 1 + k for k in range(CONV_K)])
        rows.append(jnp.zeros((1, 512), F32))
        gw = jnp.concatenate(rows, axis=0)
        ext_ref[:tm] = du_ref[...]
        ext_ref[tm:] = jnp.where(i < nt - 1, dun_ref[...], 0.0)
        _shifted_copies(ext_ref, sh_ref)
        _conv_taps(sh_ref, w_ref, [CONV_K - 1 - k for k in range(CONV_K)], cv_ref, 0.0)
        du0 = cv_ref[...]
        dglu_ref[:, :512] = (du0 * sg).astype(BF16)
        dglu_ref[:, 512:] = (du0 * gv * sg * (1.0 - sg)).astype(BF16)

        @pl.when(i == 0)
        def _():
            gw_ref[...] = gw

        @pl.when(i > 0)
        def _():
            gw_ref[...] += gw

    full = lambda s: pl.BlockSpec(s, lambda i: (0, 0))
    return pl.pallas_call(
        body, name=name, grid=(nt,),
        in_specs=[pl.BlockSpec(memory_space=pl.ANY), cur(C_GV), cur(C_GG), prev(C_GV), prev(C_GG),
                  pl.BlockSpec((tm, 512), lambda i: (i, 0)),
                  pl.BlockSpec((HALO_B, 512), lambda i: (jnp.minimum((i + 1) * r, T // HALO_B - 1), 0)),
                  full((CONV_K, 512))],
        out_specs=[pl.BlockSpec((tm, 1024), lambda i: (i, C_GV // 1024)), full((32, 512))],
        out_shape=[jax.ShapeDtypeStruct(dproj.shape, BF16), jax.ShapeDtypeStruct((32, 512), F32)],
        scratch_shapes=_conf_scratch(tm),
        input_output_aliases={0: 0},
        compiler_params=_cparams(("arbitrary",)),
    )(dproj, proj, proj, proj, proj, du1, du1, dw_w)


HALO_C = 8
QKV_C = 1536


def _softplus(x):
    return jnp.maximum(x, 0.0) + jnp.log1p(jnp.exp(-jnp.abs(x)))


def _gdn_conv(i, tm, x_ref, xp_ref, w_ref, ext_ref):
    ext_ref[:HALO_C] = jnp.where(i > 0, xp_ref[...], 0.0)
    ext_ref[HALO_C:] = x_ref[...]
    pre = jnp.zeros((tm, QKV_C), F32)
    for k in range(DN_K):
        pre = pre + w_ref[k:k + 1, :] * ext_ref[pl.ds(HALO_C - DN_K + 1 + k, tm), :]
    return pre


def _gdn_specs(T, tm):
    r = tm // HALO_C
    cur = pl.BlockSpec((tm, QKV_C), lambda i: (i, C_QC // QKV_C))
    prev = pl.BlockSpec((HALO_C, QKV_C), lambda i: (jnp.maximum(i * r - 1, 0), C_QC // QKV_C))
    ab = pl.BlockSpec((tm, 128), lambda i: (i, C_AB // 128))
    return cur, prev, ab


def _gdn_prep_fwd(proj, sconv_w, alog_v, dtb_v, name):
    T = proj.shape[0]
    tm = min(512, T)
    cur, prev, ab = _gdn_specs(T, tm)

    def body(x_ref, xp_ref, ab_ref, w_ref, al_ref, dt_ref, q_ref, k_ref, v_ref, gb_ref, ext_ref):
        i = pl.program_id(0)
        y = _silu(_gdn_conv(i, tm, x_ref, xp_ref, w_ref, ext_ref))
        for h in range(DN_HEADS):
            sl = slice(128 * h, 128 * h + 128)
            qh, kh = y[:, sl], y[:, 512 + 128 * h:512 + 128 * h + 128]
            q_ref[:, sl] = qh * lax.rsqrt(jnp.sum(qh * qh, axis=-1, keepdims=True) + EPS) * (128 ** -0.5)
            k_ref[:, sl] = kh * lax.rsqrt(jnp.sum(kh * kh, axis=-1, keepdims=True) + EPS)
        v_ref[...] = y[:, 1024:]
        abv = ab_ref[...]
        lane = lax.broadcasted_iota(jnp.int32, (tm, 128), 1)
        g = -jnp.exp(al_ref[...]) * _softplus(abv + dt_ref[...])
        gb_ref[...] = jnp.where(lane < DN_HEADS, g, _sig(abv))

    full = lambda s: pl.BlockSpec(s, lambda i: (0, 0))
    blk = pl.BlockSpec((tm, 512), lambda i: (i, 0))
    return pl.pallas_call(
        body, name=name, grid=(T // tm,),
        in_specs=[cur, prev, ab, full((DN_K, QKV_C)), full((1, 128)), full((1, 128))],
        out_specs=[blk, blk, blk, pl.BlockSpec((tm, 128), lambda i: (i, 0))],
        out_shape=[jax.ShapeDtypeStruct((T, 512), F32)] * 3 + [jax.ShapeDtypeStruct((T, 128), F32)],
        scratch_shapes=[pltpu.VMEM((tm + HALO_C, QKV_C), F32)],
        compiler_params=_cparams(("parallel",)),
    )(proj, proj, proj, sconv_w, alog_v, dtb_v)


def _hdot(a, b, dims=NN):
    return _dot(a, b, dims, precision=HI)


def _lockstep(gens):
    results, live = [None] * len(gens), list(range(len(gens)))
    while live:
        for i in list(live):
            try:
                next(gens[i])
            except StopIteration as stop:
                results[i] = stop.value
                live.remove(i)
    return results


def _dot_exact(a, b, dims=NN, split_left=True):
    x = (a if split_left else b).astype(F32)
    hi = x.astype(BF16)
    r = x - hi.astype(F32)
    mid = r.astype(BF16)
    lo = (r - mid.astype(F32)).astype(BF16)
    other = (b if split_left else a).astype(BF16)
    one = (lambda p: _dot(p, other, dims)) if split_left else (lambda p: _dot(other, p, dims))
    return (one(lo) + one(mid)) + one(hi)


def _split(a):
    hi = a.astype(BF16)
    return hi, (a - hi.astype(F32)).astype(BF16)


def _dot3(a, b):
    (ah, al), (bh, bl) = a, b
    return _dot(ah, bh) + (_dot(ah, bl) + _dot(al, bh))


INV_BLOCK = 16


def _tri_inv(mats, eye):
    ii = lax.broadcasted_iota(jnp.int32, (CHUNK, CHUNK), 0) // INV_BLOCK
    jj = lax.broadcasted_iota(jnp.int32, (CHUNK, CHUNK), 1) // INV_BLOCK
    diag = ii == jj
    mul = lambda xs, ys: [_dot3(_split(x), _split(y)) for x, y in zip(xs, ys)]
    ps = [jnp.where(diag, -a, 0.0) for a in mats]
    tds = [eye + p for p in ps]
    for _ in range(3):
        ps = mul(ps, ps)
        tds = [t + tp for t, tp in zip(tds, mul(tds, ps))]
    ms = mul(tds, [jnp.where(diag, 0.0, a) for a in mats])
    m2 = mul(ms, ms)
    inv = [(eye - m) + x for m, x in zip(ms, mul([eye - m for m in ms], m2))]
    return mul(inv, tds)


def _tri_consts():
    ii = lax.broadcasted_iota(jnp.int32, (CHUNK, CHUNK), 0)
    jj = lax.broadcasted_iota(jnp.int32, (CHUNK, CHUNK), 1)
    return ii >= jj, ii > jj, (ii == jj).astype(F32)


def _gdn_local(q, k, v, gcol, grow, bcol, lower, strict):
    dm = jnp.where(lower, jnp.exp(jnp.where(lower, gcol - grow, 0.0)), 0.0)
    kb = k * bcol
    a = jnp.where(strict, _bdot(kb, k, NT) * dm, 0.0)
    gc = jnp.exp(gcol)
    glast = grow[:, CHUNK - 1:CHUNK]
    return dict(q=q, k=k, v=v, bcol=bcol, gcol=gcol, glast=glast, dm=dm, kb=kb, a=a, gc=gc, vb=v * bcol,
                kbg=kb * gc, p=_bdot(q, k, NT) * dm, qe=q * gc, ke=k * jnp.exp(glast - gcol))


def _gdn_chunk_bwd(c, do, dvn, ds_new, lower, strict, ones):
    rs = lambda m: jnp.sum(m, axis=-1, keepdims=True)
    colsum = lambda m: _dot_exact(m, ones, TN)[:, :1]
    q, k, v, bcol, dm, tm, gc, s = c["q"], c["k"], c["v"], c["bcol"], c["dm"], c["tm"], c["gc"], c["s"]
    eg = jnp.exp(c["glast"])
    dqe = _bdot(do, s, NT)
    dp = jnp.where(lower, _bdot(do, c["vn"], NT), 0.0)
    dw = -_bdot(dvn, s, NT)
    dke = _bdot(c["vn"], ds_new, NT)
    dvb = _bdot(tm, dvn, TN)
    yield
    dglast = jnp.sum(rs(ds_new * s), axis=0, keepdims=True) * eg
    dk = dke * jnp.exp(c["glast"] - c["gcol"])
    r_ke = rs(dke * c["ke"])
    dglast = dglast + jnp.sum(r_ke, axis=0, keepdims=True)
    dgam = rs(dqe * c["qe"]) - r_ke
    dq = dqe * gc
    dpm = dp * dm
    mp = dp * c["p"]
    dq = dq + _bdot(dpm, k)
    dk = dk + _bdot(dpm, q, TN)
    dt = _bdot(dvn, c["vb"], NT) + _bdot(dw, c["kbg"], NT)
    dkbg = _bdot(tm, dw, TN)
    dgam = dgam + rs(mp) - colsum(mp)
    yield
    dkb = dkbg * gc
    dgam = dgam + rs(dkbg * c["kbg"])
    dat = _bdot(tm, dt, TN)
    yield
    da = jnp.where(strict, -_bdot(dat, tm, NT), 0.0)
    yield
    dam = da * dm
    ma = da * c["a"]
    dkb = dkb + _bdot(dam, k)
    dk = dk + _bdot(dam, c["kb"], TN)
    dgam = dgam + rs(ma) - colsum(ma)
    yield
    dk = dk + dkb * bcol
    dbeta = rs(dkb * k) + rs(dvb * v)
    dv = dvb * bcol
    row = lax.broadcasted_iota(jnp.int32, (CHUNK, 1), 0)
    dgam = dgam + jnp.where(row == CHUNK - 1, dglast, 0.0)
    dg = _dot_exact(lower, dgam, TN, split_left=False)
    return dq, dk, dv, dg, dbeta


SCAN_GROUP = 8
GROUP = 8


def _chunk_decay(gb_ref, gt_ref, lmat, g):
    rows = slice(CHUNK * g, CHUNK * g + CHUNK)
    return rows, _dot_exact(lmat, gb_ref[rows, :], split_left=False), _dot_exact(gt_ref[g], lmat, NT)


def _gdn_chunk_fwd(qd, kd, vd, gb, gbt, name):
    T = qd.shape[0]
    G = min(GROUP, T // CHUNK)
    ng = T // (CHUNK * G)

    def body(q_ref, k_ref, v_ref, gb_ref, gt_ref, u_ref, w_ref, qe_ref, ke_ref, p_ref, t_ref, eg_ref):
        lower, strict, eye = _tri_consts()
        lmat = lower.astype(F32)
        decay = [_chunk_decay(gb_ref, gt_ref, lmat, g) for g in range(G)]
        chains = [(g, h) for g in range(G) for h in range(DN_HEADS)]
        cs = []
        for g, h in chains:
            rows, gcs, grs = decay[g]
            sl = slice(128 * h, 128 * h + 128)
            c = _gdn_local(q_ref[rows, sl], k_ref[rows, sl], v_ref[rows, sl], gcs[:, h:h + 1], grs[h:h + 1, :],
                           gb_ref[rows, DN_HEADS + h:DN_HEADS + h + 1], lower, strict)
            qe_ref[rows, sl] = c["qe"].astype(BF16)
            ke_ref[rows, sl] = c["ke"].astype(BF16)
            p_ref[rows, 64 * h:64 * h + 64] = c["p"].astype(BF16)
            eg_ref[g, h:h + 1, :] = jnp.broadcast_to(jnp.exp(c["glast"]), (1, 128))
            cs.append(c)
        tms = [t.astype(BF16) for t in _tri_inv([c["a"] for c in cs], eye)]
        us = [_dot(t, c["vb"].astype(BF16)) for t, c in zip(tms, cs)]
        ws = [_dot(t, c["kbg"].astype(BF16)) for t, c in zip(tms, cs)]
        for (g, h), tm, u, w in zip(chains, tms, us, ws):
            rows, sl = decay[g][0], slice(128 * h, 128 * h + 128)
            u_ref[rows, sl] = u
            w_ref[rows, sl] = w.astype(BF16)
            t_ref[rows, 64 * h:64 * h + 64] = tm
        for g in range(G):
            eg_ref[g, DN_HEADS:, :] = jnp.zeros((8 - DN_HEADS, 128), F32)

    blk = pl.BlockSpec((CHUNK * G, 512), lambda n: (n, 0))
    half = pl.BlockSpec((CHUNK * G, 256), lambda n: (n, 0))
    return pl.pallas_call(
        body, name=name, grid=(ng,),
        in_specs=[blk, blk, blk, pl.BlockSpec((CHUNK * G, 128), lambda n: (n, 0)),
                  pl.BlockSpec((G, 8, CHUNK), lambda n: (n, 0, 0))],
        out_specs=[blk, blk, blk, blk, half, half, pl.BlockSpec((G, 8, 128), lambda n: (n, 0, 0))],
        out_shape=[jax.ShapeDtypeStruct((T, 512), F32)] + [jax.ShapeDtypeStruct((T, 512), BF16)] * 3
        + [jax.ShapeDtypeStruct((T, 256), BF16)] * 2 + [jax.ShapeDtypeStruct((T // CHUNK, 8, 128), F32)],
        compiler_params=_cparams(("parallel",)),
    )(qd, kd, vd, gb, gbt)


def _gdn_scan_fwd(u, w, qe, ke, pm, eg, proj, dn_g, name):
    T = u.shape[0]
    nc = T // CHUNK
    G = min(SCAN_GROUP, T // CHUNK)

    def body(u_ref, w_ref, qe_ref, ke_ref, p_ref, eg_ref, z_ref, ng_ref, y_ref, o_ref, vn_ref, ss_ref, s_ref):
        n = pl.program_id(0)

        @pl.when(n == 0)
        def _():
            s_ref[...] = jnp.zeros_like(s_ref)

        def head(j, h):
            rows, sl = slice(CHUNK * j, CHUNK * j + CHUNK), slice(128 * h, 128 * h + 128)
            s = s_ref[h]
            sb = s.astype(BF16)
            vn = u_ref[rows, sl] - _dot(w_ref[rows, sl], sb)
            qs = _dot(qe_ref[rows, sl], sb)
            yield
            vb = vn.astype(BF16)
            o = qs + _dot(p_ref[rows, 64 * h:64 * h + 64], vb)
            s_ref[h] = s * eg_ref[j, h:h + 1, :] + _dot(ke_ref[rows, sl], vb, TN)
            yield
            vn_ref[rows, sl] = vb
            o_ref[rows, sl] = o
            y_ref[rows, sl] = _rms(o, None)[0] * ng_ref[...] * _silu(z_ref[rows, sl])

        for j in range(G):
            ss_ref[j] = s_ref[...]
            _lockstep([head(j, h) for h in range(DN_HEADS)])

    blk = pl.BlockSpec((CHUNK * G, 512), lambda n: (n, 0))
    return pl.pallas_call(
        body, name=name, grid=(nc // G,),
        in_specs=[blk, blk, blk, blk, pl.BlockSpec((CHUNK * G, 256), lambda n: (n, 0)),
                  pl.BlockSpec((G, 8, 128), lambda n: (n, 0, 0)),
                  pl.BlockSpec((CHUNK * G, 512), lambda n: (n, C_ZC // 512)), pl.BlockSpec((1, 128), lambda n: (0, 0))],
        out_specs=[blk, blk, blk, pl.BlockSpec((G, DN_HEADS, 128, 128), lambda n: (n, 0, 0, 0))],
        out_shape=[jax.ShapeDtypeStruct((T, 512), F32), jax.ShapeDtypeStruct((T, 512), F32),
                   jax.ShapeDtypeStruct((T, 512), BF16), jax.ShapeDtypeStruct((nc, DN_HEADS, 128, 128), F32)],
        scratch_shapes=[pltpu.VMEM((DN_HEADS, 128, 128), F32)],
        compiler_params=_cparams(("arbitrary",)),
    )(u, w, qe, ke, pm, eg, proj, dn_g)


def _gdn_scan_bwd(dproj, w, qe, ke, pm, eg, o, proj, dyc, dn_g, name):
    T = o.shape[0]
    nc = T // CHUNK
    G = min(SCAN_GROUP, T // CHUNK)
    rev = lambda n: nc // G - 1 - n

    def body(dp_any, w_ref, qe_ref, ke_ref, p_ref, eg_ref, o_ref, z_ref, dy_ref, ng_ref,
             dz_ref, do_ref, dvn_ref, dsn_ref, gng_ref, ds_ref):
        n = pl.program_id(0)

        @pl.when(n == 0)
        def _():
            ds_ref[...] = jnp.zeros_like(ds_ref)
            gng_ref[...] = jnp.zeros_like(gng_ref)

        def head(j, h):
            rows, sl = slice(CHUNK * j, CHUNK * j + CHUNK), slice(128 * h, 128 * h + 128)
            oh, r = _rms(o_ref[rows, sl], None)
            z, dy = z_ref[rows, sl], dy_ref[rows, sl]
            dz_ref[rows, sl] = (dy * (oh * ng_ref[...]) * _dsilu(z)).astype(BF16)
            do, gg = _rms_bwd(dy * _silu(z), oh, r, ng_ref[...])
            dob = do.astype(BF16)
            ds = ds_ref[h]
            dvn = _dot(p_ref[rows, 64 * h:64 * h + 64], dob, TN) + _dot(ke_ref[rows, sl], ds.astype(BF16))
            qd = _dot(qe_ref[rows, sl], dob, TN)
            yield
            dvb = dvn.astype(BF16)
            ds_ref[h] = qd + eg_ref[j, h:h + 1, :] * ds - _dot(w_ref[rows, sl], dvb, TN)
            do_ref[rows, sl] = dob
            dvn_ref[rows, sl] = dvb
            return jnp.sum(gg, axis=0, keepdims=True)

        for j in reversed(range(G)):
            dsn_ref[j] = ds_ref[...]
            gng = _lockstep([head(j, h) for h in range(DN_HEADS)])
            gng_ref[...] += (gng[0] + gng[1]) + (gng[2] + gng[3])

    blk = pl.BlockSpec((CHUNK * G, 512), lambda n: (rev(n), 0))
    state = pl.BlockSpec((G, DN_HEADS, 128, 128), lambda n: (rev(n), 0, 0, 0))
    return pl.pallas_call(
        body, name=name, grid=(nc // G,),
        in_specs=[pl.BlockSpec(memory_space=pl.ANY), blk, blk, blk,
                  pl.BlockSpec((CHUNK * G, 256), lambda n: (rev(n), 0)),
                  pl.BlockSpec((G, 8, 128), lambda n: (rev(n), 0, 0)), blk,
                  pl.BlockSpec((CHUNK * G, 512), lambda n: (rev(n), C_ZC // 512)), blk,
                  pl.BlockSpec((1, 128), lambda n: (0, 0))],
        out_specs=[pl.BlockSpec((CHUNK * G, 512), lambda n: (rev(n), C_ZC // 512)), blk, blk, state,
                   pl.BlockSpec((1, 128), lambda n: (0, 0))],
        out_shape=[jax.ShapeDtypeStruct(dproj.shape, BF16), jax.ShapeDtypeStruct((T, 512), BF16),
                   jax.ShapeDtypeStruct((T, 512), BF16), jax.ShapeDtypeStruct((nc, DN_HEADS, 128, 128), F32),
                   jax.ShapeDtypeStruct((1, 128), F32)],
        scratch_shapes=[pltpu.VMEM((DN_HEADS, 128, 128), F32)],
        input_output_aliases={0: 0},
        compiler_params=_cparams(("arbitrary",)),
    )(dproj, w, qe, ke, pm, eg, o, proj, dyc, dn_g)


def _gdn_chunk_grad(qd, kd, vd, gb, gbt, tmi, ssave, dsn, do, dvn, vn, name):
    T = qd.shape[0]
    G = min(GROUP, T // CHUNK)
    ng = T // (CHUNK * G)

    def body(q_ref, k_ref, v_ref, gb_ref, gt_ref, t_ref, ss_ref, dsn_ref, do_ref, dvn_ref, vn_ref,
             dq_ref, dk_ref, dv_ref, dgb_ref):
        lower, strict, _ = _tri_consts()
        lmat = lower.astype(F32)
        ones = jnp.ones((CHUNK, 128), F32)
        lane = lax.broadcasted_iota(jnp.int32, (CHUNK, 128), 1)
        decay = [_chunk_decay(gb_ref, gt_ref, lmat, g) for g in range(G)]
        chains = [(g, h) for g in range(G) for h in range(DN_HEADS)]
        gens = []
        for g, h in chains:
            rows, gcs, grs = decay[g]
            sl = slice(128 * h, 128 * h + 128)
            c = _gdn_local(q_ref[rows, sl], k_ref[rows, sl], v_ref[rows, sl], gcs[:, h:h + 1], grs[h:h + 1, :],
                           gb_ref[rows, DN_HEADS + h:DN_HEADS + h + 1], lower, strict)
            c.update(tm=t_ref[rows, 64 * h:64 * h + 64], s=ss_ref[g, h], vn=vn_ref[rows, sl])
            gens.append(_gdn_chunk_bwd(c, do_ref[rows, sl], dvn_ref[rows, sl], dsn_ref[g, h], lower, strict, ones))
        dgb = [jnp.zeros((CHUNK, 128), F32) for _ in range(G)]
        for (g, h), (dq, dk, dv, dg, dbeta) in zip(chains, _lockstep(gens)):
            rows, sl = decay[g][0], slice(128 * h, 128 * h + 128)
            dq_ref[rows, sl], dk_ref[rows, sl], dv_ref[rows, sl] = dq, dk, dv
            dgb[g] = dgb[g] + jnp.where(lane == h, dg, 0.0) + jnp.where(lane == DN_HEADS + h, dbeta, 0.0)
        for g in range(G):
            dgb_ref[decay[g][0], :] = dgb[g]

    blk = pl.BlockSpec((CHUNK * G, 512), lambda n: (n, 0))
    half = pl.BlockSpec((CHUNK * G, 256), lambda n: (n, 0))
    nar = pl.BlockSpec((CHUNK * G, 128), lambda n: (n, 0))
    state = pl.BlockSpec((G, DN_HEADS, 128, 128), lambda n: (n, 0, 0, 0))
    return pl.pallas_call(
        body, name=name, grid=(ng,),
        in_specs=[blk, blk, blk, nar, pl.BlockSpec((G, 8, CHUNK), lambda n: (n, 0, 0)), half, state, state,
                  blk, blk, blk],
        out_specs=[blk, blk, blk, nar],
        out_shape=[jax.ShapeDtypeStruct((T, 512), F32)] * 3 + [jax.ShapeDtypeStruct((T, 128), F32)],
        compiler_params=_cparams(("parallel",)),
    )(qd, kd, vd, gb, gbt, tmi, ssave, dsn, do, dvn, vn)


def _gdn_prep_bwd1(dproj, proj, dqd, dkd, dvd, dgb, dkv_a, sconv_w, alog_v, dtb_v, name):
    T = proj.shape[0]
    tm = min(512, T)
    cur, prev, ab = _gdn_specs(T, tm)

    def body(dp_any, x_ref, xp_ref, ab_ref, dq_ref, dk_ref, dv_ref, dgb_ref, dkv_ref, w_ref, al_ref, dt_ref,
             o_ref, dpre_ref, st_ref, ext_ref):
        i = pl.program_id(0)
        pre = _gdn_conv(i, tm, x_ref, xp_ref, w_ref, ext_ref)
        y, dsl = _silu(pre), _dsilu(pre)
        for h in range(DN_HEADS):
            for base, g_ref, scale in ((0, dq_ref, 128 ** -0.5), (512, dk_ref, 1.0)):
                sl = slice(base + 128 * h, base + 128 * h + 128)
                xh = y[:, sl]
                r = lax.rsqrt(jnp.sum(xh * xh, axis=-1, keepdims=True) + EPS)
                xn = xh * r
                gy = g_ref[:, 128 * h:128 * h + 128]
                dpre_ref[:, sl] = (scale * r) * (gy - xn * jnp.sum(gy * xn, axis=-1, keepdims=True)) * dsl[:, sl]
        dpre_ref[:, 1024:] = dv_ref[...] * dsl[:, 1024:]
        abv, dgb = ab_ref[...], dgb_ref[...]
        lane = lax.broadcasted_iota(jnp.int32, (tm, 128), 1)
        na = -jnp.exp(al_ref[...])
        xs = abv + dt_ref[...]
        da = dgb * na * _sig(xs)
        b = _sig(abv)
        o_ref[:, :256] = dkv_ref[...].astype(BF16)
        o_ref[:, 256:] = jnp.where(lane < DN_HEADS, da,
                                   jnp.where(lane < 2 * DN_HEADS, dgb * b * (1.0 - b), 0.0)).astype(BF16)
        head = lane < DN_HEADS
        upd = jnp.concatenate([jnp.sum(jnp.where(head, dgb * na * _softplus(xs), 0.0), axis=0, keepdims=True),
                               jnp.sum(jnp.where(head, da, 0.0), axis=0, keepdims=True), jnp.zeros((6, 128), F32)],
                              axis=0)

        @pl.when(i == 0)
        def _():
            st_ref[...] = upd

        @pl.when(i > 0)
        def _():
            st_ref[...] += upd

    full = lambda s: pl.BlockSpec(s, lambda i: (0, 0))
    blk = pl.BlockSpec((tm, 512), lambda i: (i, 0))
    return pl.pallas_call(
        body, name=name, grid=(T // tm,),
        in_specs=[pl.BlockSpec(memory_space=pl.ANY), cur, prev, ab, blk, blk, blk,
                  pl.BlockSpec((tm, 128), lambda i: (i, 0)), pl.BlockSpec((tm, 256), lambda i: (i, 0)),
                  full((DN_K, QKV_C)), full((1, 128)), full((1, 128))],
        out_specs=[pl.BlockSpec((tm, 384), lambda i: (i, C_KA // 384)),
                   pl.BlockSpec((tm, QKV_C), lambda i: (i, 0)), full((8, 128))],
        out_shape=[jax.ShapeDtypeStruct(dproj.shape, BF16), jax.ShapeDtypeStruct((T, QKV_C), F32),
                   jax.ShapeDtypeStruct((8, 128), F32)],
        scratch_shapes=[pltpu.VMEM((tm + HALO_C, QKV_C), F32)],
        input_output_aliases={0: 0},
        compiler_params=_cparams(("arbitrary",)),
    )(dproj, proj, proj, proj, dqd, dkd, dvd, dgb, dkv_a, sconv_w, alog_v, dtb_v)


def _gdn_prep_bwd2(dproj, proj, dpre, sconv_w, name):
    T = proj.shape[0]
    tm = min(512, T)
    nt = T // tm
    r = tm // HALO_C
    cur, prev, _ = _gdn_specs(T, tm)

    def body(dp_any, x_ref, xp_ref, d_ref, dn_ref, w_ref, dx_ref, gw_ref, extx_ref, extd_ref):
        i = pl.program_id(0)
        extx_ref[:HALO_C] = jnp.where(i > 0, xp_ref[...], 0.0)
        extx_ref[HALO_C:] = x_ref[...]
        d = d_ref[...]
        extd_ref[:tm] = d
        extd_ref[tm:] = jnp.where(i < nt - 1, dn_ref[...], 0.0)
        dx = jnp.zeros((tm, QKV_C), F32)
        rows = []
        for k in range(DN_K):
            dx = dx + w_ref[k:k + 1, :] * extd_ref[pl.ds(DN_K - 1 - k, tm), :]
            rows.append(jnp.sum(d * extx_ref[pl.ds(HALO_C - DN_K + 1 + k, tm), :], axis=0, keepdims=True))
        rows.append(jnp.zeros((8 - DN_K, QKV_C), F32))
        gw = jnp.concatenate(rows, axis=0)
        dx_ref[...] = dx.astype(BF16)

        @pl.when(i == 0)
        def _():
            gw_ref[...] = gw

        @pl.when(i > 0)
        def _():
            gw_ref[...] += gw

    full = lambda s: pl.BlockSpec(s, lambda i: (0, 0))
    return pl.pallas_call(
        body, name=name, grid=(nt,),
        in_specs=[pl.BlockSpec(memory_space=pl.ANY), cur, prev, pl.BlockSpec((tm, QKV_C), lambda i: (i, 0)),
                  pl.BlockSpec((HALO_C, QKV_C), lambda i: (jnp.minimum((i + 1) * r, T // HALO_C - 1), 0)),
                  full((DN_K, QKV_C))],
        out_specs=[cur, full((8, QKV_C))],
        out_shape=[jax.ShapeDtypeStruct(dproj.shape, BF16), jax.ShapeDtypeStruct((8, QKV_C), F32)],
        scratch_shapes=[pltpu.VMEM((tm + HALO_C, QKV_C), F32), pltpu.VMEM((tm + HALO_C, QKV_C), F32)],
        input_output_aliases={0: 0},
        compiler_params=_cparams(("arbitrary",)),
    )(dproj, proj, proj, dpre, dpre, sconv_w)


def _merge_fwd(x, proj, ya, yb, yc, wa, wb, wc, wo, gate, name):
    T = x.shape[0]
    tm = min(256, T)

    def body(x_ref, mg_ref, ya_ref, yb_ref, yc_ref, wa_ref, wb_ref, wc_ref, wo_ref, gate_ref, o_ref):
        merged = (_sig(mg_ref[:, :D]) * _bdot(ya_ref[...], wa_ref[...])
                  + _sig(mg_ref[:, D:2 * D]) * _bdot(yb_ref[...], wb_ref[...])
                  + _sig(mg_ref[:, 2 * D:]) * _bdot(yc_ref[...], wc_ref[...]))
        o_ref[...] = x_ref[...] + gate_ref[...] * _bdot(merged, wo_ref[...])

    full = lambda s: pl.BlockSpec(s, lambda i: (0, 0))
    yb_ = pl.BlockSpec((tm, 512), lambda i: (i, 0))
    return pl.pallas_call(
        body, name=name, grid=(T // tm,),
        in_specs=[pl.BlockSpec((tm, D), lambda i: (i, 0)), pl.BlockSpec((tm, 3 * D), lambda i: (i, 0)), yb_, yb_, yb_,
                  full((512, D)), full((512, D)), full((512, D)), full((D, D)), full((1, D))],
        out_specs=pl.BlockSpec((tm, D), lambda i: (i, 0)),
        out_shape=jax.ShapeDtypeStruct((T, D), F32),
        compiler_params=_cparams(("parallel",)),
    )(x, proj, ya, yb, yc, wa, wb, wc, wo, _row(gate))


def _merge_bwd(dout, proj, ya, yb, yc, wa, wb, wc, wo, gate, name):
    T = dout.shape[0]
    tm = min(256, T)
    nt = T // tm

    def body(do_ref, mg_ref, ya_ref, yb_ref, yc_ref, wa_ref, wb_ref, wc_ref, wo_ref, gate_ref,
             dmg_ref, dya_ref, dyb_ref, dyc_ref, gwa_hbm, gwb_hbm, gwc_hbm, gwo_hbm, gg_ref,
             gwa_ref, gwb_ref, gwc_ref, gwo_ref):
        i = pl.program_id(0)

        @pl.when(i == 0)
        def _():
            for r in (gwa_ref, gwb_ref, gwc_ref, gwo_ref, gg_ref):
                r[...] = jnp.zeros_like(r)

        ys = (ya_ref[...], yb_ref[...], yc_ref[...])
        ws = (wa_ref, wb_ref, wc_ref)
        gs = tuple(_sig(mg_ref[:, j * D:(j + 1) * D]) for j in range(3))
        ps = tuple(_bdot(ys[j], ws[j][...]) for j in range(3))
        merged = gs[0] * ps[0] + gs[1] * ps[1] + gs[2] * ps[2]
        do = do_ref[...]
        dmerged = _bdot(do * gate_ref[...], wo_ref[...], NT)
        gwo_ref[...] += _bdot(merged, do, TN)
        for j, (dy_ref, gw_ref) in enumerate(((dya_ref, gwa_ref), (dyb_ref, gwb_ref), (dyc_ref, gwc_ref))):
            dp = dmerged * gs[j]
            dmg_ref[:, j * D:(j + 1) * D] = (dmerged * ps[j] * gs[j] * (1.0 - gs[j])).astype(BF16)
            dy_ref[...] = _bdot(dp, ws[j][...], NT)
            gw_ref[...] += _bdot(ys[j], dp, TN)

        @pl.when(i == nt - 1)
        def _():
            m = gwo_ref[...]
            gg_ref[...] = jnp.sum(wo_ref[...].astype(F32) * m, axis=0, keepdims=True)
            gwo_ref[...] = m * gate_ref[...]
            for src, dst in ((gwa_ref, gwa_hbm), (gwb_ref, gwb_hbm), (gwc_ref, gwc_hbm), (gwo_ref, gwo_hbm)):
                pltpu.sync_copy(src, dst)

    full = lambda s: pl.BlockSpec(s, lambda i: (0, 0))
    yb_ = pl.BlockSpec((tm, 512), lambda i: (i, 0))
    anyspec = pl.BlockSpec(memory_space=pl.ANY)
    return pl.pallas_call(
        body, name=name, grid=(nt,),
        in_specs=[pl.BlockSpec((tm, D), lambda i: (i, 0)), pl.BlockSpec((tm, 3 * D), lambda i: (i, 0)), yb_, yb_, yb_,
                  full((512, D)), full((512, D)), full((512, D)), full((D, D)), full((1, D))],
        out_specs=[pl.BlockSpec((tm, 3 * D), lambda i: (i, 0)), yb_, yb_, yb_, anyspec, anyspec, anyspec, anyspec,
                   full((1, D))],
        out_shape=[jax.ShapeDtypeStruct((T, NP), BF16)] + [jax.ShapeDtypeStruct((T, 512), F32)] * 3
        + [jax.ShapeDtypeStruct((512, D), F32)] * 3 + [jax.ShapeDtypeStruct((D, D), F32), jax.ShapeDtypeStruct((1, D), F32)],
        scratch_shapes=[pltpu.VMEM((512, D), F32)] * 3 + [pltpu.VMEM((D, D), F32)],
        compiler_params=_cparams(("arbitrary",)),
    )(dout, proj, ya, yb, yc, wa, wb, wc, wo, _row(gate))


def _loss_head(y, tgt, name):
    T = y.shape[0]
    tm = min(512, T)

    def body(y_ref, t_ref, dy_ref, l_ref):
        i = pl.program_id(0)
        diff = y_ref[...] - t_ref[...]
        dy_ref[...] = diff * (1.0 / D)
        part = jnp.sum(diff * diff, axis=0, keepdims=True)

        @pl.when(i == 0)
        def _():
            l_ref[...] = part

        @pl.when(i > 0)
        def _():
            l_ref[...] += part

    blk = pl.BlockSpec((tm, D), lambda i: (i, 0))
    return pl.pallas_call(
        body, name=name, grid=(T // tm,), in_specs=[blk, blk],
        out_specs=[blk, pl.BlockSpec((1, D), lambda i: (0, 0))],
        out_shape=[jax.ShapeDtypeStruct((T, D), F32), jax.ShapeDtypeStruct((1, D), F32)],
        compiler_params=_cparams(("arbitrary",)),
    )(y, tgt)


def _ada_fwd(c_all, w_ada, b_my, name):
    def body(c_ref, w_ref, b_ref, o_ref):
        sc = _silu(c_ref[...])
        for l in range(DEPTH):
            o_ref[l] = _bdot(sc, w_ref[l]) + b_ref[l:l + 1, :]

    return pl.pallas_call(body, name=name, out_shape=jax.ShapeDtypeStruct((DEPTH, N_DEV, w_ada.shape[2]), F32),
                          compiler_params=_cparams())(c_all, w_ada, b_my)


def _ada_bwd(c_all, dmod_my, name):
    def body(c_ref, d_ref, o_ref):
        sc = _silu(c_ref[...])
        for l in range(DEPTH):
            o_ref[l] = _bdot(sc, d_ref[l], TN)

    return pl.pallas_call(body, name=name, out_shape=jax.ShapeDtypeStruct((DEPTH, D, dmod_my.shape[2]), F32),
                          compiler_params=_cparams())(c_all, dmod_my)


def _adam_math(w, g, m, v):
    m = ADAM_B1 * m + (1.0 - ADAM_B1) * g
    v = ADAM_B2 * v + (1.0 - ADAM_B2) * (g * g)
    m_hat = m / (1.0 - ADAM_B1 ** ADAM_STEP)
    v_hat = v / (1.0 - ADAM_B2 ** ADAM_STEP)
    return -ADAM_LR * (m_hat / (jnp.sqrt(v_hat) + ADAM_EPS) + ADAM_WD * w), m, v


def _row_tile(rows, cap):
    best = rows
    for t in range(8, min(rows, cap) + 1, 8):
        if rows % t == 0:
            best = t
    return best if best <= cap else rows


def _adamw(w, g, m, v, name):
    R, C = w.shape
    tr = _row_tile(R, 256)

    def body(w_ref, g_ref, m_ref, v_ref, d_ref, mo_ref, vo_ref):
        d_ref[...], mo_ref[...], vo_ref[...] = _adam_math(w_ref[...], g_ref[...], m_ref[...], v_ref[...])

    blk = pl.BlockSpec((tr, C), lambda i: (i, 0))
    return pl.pallas_call(body, name=name, grid=(R // tr,), in_specs=[blk] * 4, out_specs=[blk] * 3,
                          out_shape=[jax.ShapeDtypeStruct((R, C), F32)] * 3,
                          compiler_params=_cparams(("parallel",)))(w, g, m, v)


def _sum_adamw_many(parts, ws, ms, vs, name):
    n = len(ws)

    def body(*refs):
        ins, outs = refs[:4 * n], refs[4 * n:]
        for i in range(n):
            g = ins[i][0]
            for j in range(1, N_DEV):
                g = g + ins[i][j]
            d, m, v = _adam_math(ins[n + i][...], g, ins[2 * n + i][...], ins[3 * n + i][...])
            outs[i][...], outs[n + i][...], outs[2 * n + i][...], outs[3 * n + i][...] = g, d, m, v

    shapes = [jax.ShapeDtypeStruct(w.shape, F32) for w in ws]
    out = pl.pallas_call(body, name=name, out_shape=shapes * 4, compiler_params=_cparams())(*parts, *ws, *ms, *vs)
    return out[:n], out[n:2 * n], out[2 * n:3 * n], out[3 * n:]


def _sum_adamw(parts0, parts1, w, m, v, name):
    P, R, C = parts0.shape
    tr = _row_tile(R, 128)
    nt = R // tr

    def body(p0_ref, p1_ref, w_ref, m_ref, v_ref, g_ref, d_ref, mo_ref, vo_ref):
        def emit(p_ref):
            g = p_ref[0].astype(F32)
            for j in range(1, P):
                g = g + p_ref[j].astype(F32)
            g_ref[...] = g
            d_ref[...], mo_ref[...], vo_ref[...] = _adam_math(w_ref[...], g, m_ref[...], v_ref[...])

        @pl.when(pl.program_id(0) == 0)
        def _():
            emit(p0_ref)

        @pl.when(pl.program_id(0) == 1)
        def _():
            emit(p1_ref)

    blk = pl.BlockSpec((tr, C), lambda l, i: (l * nt + i, 0))
    return pl.pallas_call(
        body, name=name, grid=(DEPTH, nt),
        in_specs=[pl.BlockSpec((P, tr, C), lambda l, i: (0, i * (1 - l) + (nt - 1) * l, 0)),
                  pl.BlockSpec((P, tr, C), lambda l, i: (0, i * l, 0)), blk, blk, blk],
        out_specs=[blk] * 4, out_shape=[jax.ShapeDtypeStruct((DEPTH * R, C), F32)] * 4,
        compiler_params=_cparams(("arbitrary", "arbitrary")))(parts0, parts1, w, m, v)


def _pair_sum(core, buf, recv, name):
    _, _, R, C = buf.shape
    tr = _row_tile(R, 128)

    def body(c_ref, a_ref, b_ref, o_ref):
        o_ref[...] = (a_ref[:, 0].astype(F32) + b_ref[...].astype(F32)).astype(BF16)

    return pl.pallas_call(
        body, name=name,
        grid_spec=pltpu.PrefetchScalarGridSpec(
            num_scalar_prefetch=1, grid=(R // tr,),
            in_specs=[pl.BlockSpec((4, 1, tr, C), lambda i, c: (0, c[0], i, 0)),
                      pl.BlockSpec((4, tr, C), lambda i, c: (0, i, 0))],
            out_specs=pl.BlockSpec((4, tr, C), lambda i, c: (0, i, 0))),
        out_shape=jax.ShapeDtypeStruct((4, R, C), BF16),
        compiler_params=_cparams(("parallel",)))(core, buf, recv)


SHARD_IN = D_IN // N_DEV


def _w_in_pieces():
    out, p = [], 0
    for a, b in _PAD_FROM:
        for j in range(N_DEV):
            lo, hi = max(a, SHARD_IN * j), min(b, SHARD_IN * (j + 1))
            if lo < hi:
                out.append((j, lo - SHARD_IN * j, hi - SHARD_IN * j, p + lo - a))
        p += b - a
    return out


def _assemble_w_in(gw, name):
    tr = 256
    nt = D // tr

    def body(x_ref, o_ref):
        for j, s0, s1, d0 in _w_in_pieces():
            o_ref[:, d0:d0 + s1 - s0] = x_ref[j, :, s0:s1]
        o_ref[:, D_IN:] = jnp.zeros((tr, NP - D_IN), gw.dtype)

    return pl.pallas_call(
        body, name=name, grid=(nt,),
        in_specs=[pl.BlockSpec((N_DEV, tr, SHARD_IN), lambda i: (0, i, 0))],
        out_specs=pl.BlockSpec((tr, NP), lambda i: (i, 0)),
        out_shape=jax.ShapeDtypeStruct((D, NP), gw.dtype),
        compiler_params=_cparams(("parallel",)))(gw)


def _split_w_in_grad(g, name):
    tr = 256

    def body(g_ref, o_ref):
        for j, s0, s1, d0 in _w_in_pieces():
            o_ref[j, :, s0:s1] = g_ref[:, d0:d0 + s1 - s0].astype(BF16)

    return pl.pallas_call(
        body, name=name, grid=(D // tr,),
        in_specs=[pl.BlockSpec((tr, NP), lambda i: (i, 0))],
        out_specs=pl.BlockSpec((N_DEV, tr, SHARD_IN), lambda i: (0, i, 0)),
        out_shape=jax.ShapeDtypeStruct((N_DEV, D, SHARD_IN), BF16),
        compiler_params=_cparams(("parallel",)))(g)


def _mesh_pos():
    return lax.axis_index("x"), lax.axis_index("y"), lax.axis_index("c")


def _launch(copies, peers, bufs, out_structs, sems, name, collective_id):
    n = len(bufs)
    if collective_id is None:
        anyspec = pl.BlockSpec(memory_space=pl.ANY)
        return list(pl.pallas_call(
            lambda *refs: copies(refs[:n], refs[n:n + len(out_structs)], *refs[n + len(out_structs):]),
            name=name, in_specs=[anyspec] * n, out_specs=[anyspec] * len(out_structs), out_shape=list(out_structs),
            scratch_shapes=list(sems))(*bufs))
    ins = [jax.new_ref(b, memory_space=pltpu.MemorySpace.HBM) for b in bufs]
    outs = [jax.empty_ref(s, memory_space=pltpu.MemorySpace.HBM) for s in out_structs]

    @pl.kernel(mesh=plsc.ScalarSubcoreMesh(axis_name="sequencer", num_cores=1), name=name, scratch_types=tuple(sems),
               compiler_params=pltpu.CompilerParams(collective_id=collective_id))
    def on_sequencer(*sem_refs):
        barrier = pltpu.get_barrier_semaphore()
        targets = peers()
        for p in targets:
            pl.semaphore_signal(barrier, inc=1, device_id=p, device_id_type=pl.DeviceIdType.MESH)
        pl.semaphore_wait(barrier, len(targets))
        copies(ins, outs, *sem_refs)

    on_sequencer()
    return [r[...] for r in outs]


def _all_gather(blocks, name, collective_id=None):
    n = len(blocks)

    def peers():
        x, y, c = _mesh_pos()
        return [(x, y, 1 - c), (1 - x, y, c), (x, 1 - y, c), (1 - x, 1 - y, c)]

    def copies(ins, outs, send_sems, recv_sems, local_sems):
        x, y, c = _mesh_pos()
        me, sibling = (x, y, c), (x, y, 1 - c)
        chips = [(1 - x, y), (x, 1 - y), (1 - x, 1 - y)]
        idx = lambda p: 4 * p[0] + 2 * p[1] + p[2]

        def copy(a, k, block, to, src=None):
            dst = outs[a].at[idx(block)]
            return pltpu.make_async_remote_copy(
                src_ref=dst if src is None else src, dst_ref=dst, send_sem=send_sems.at[a, k],
                recv_sem=recv_sems.at[a, k], device_id=to, device_id_type=pl.DeviceIdType.MESH)

        mine = [pltpu.make_async_copy(ins[a], outs[a].at[idx(me)], local_sems.at[a]) for a in range(n)]
        for cp in mine:
            cp.start()
        first = []
        for a in range(n):
            first.append(copy(a, 0, me, sibling, src=ins[a]))
            first += [copy(a, 1 + j, me, (*chip, c), src=ins[a]) for j, chip in enumerate(chips)]
        for cp in first:
            cp.start()
        passed = []
        for j, chip in enumerate(chips):
            for a in range(n):
                copy(a, 1 + j, (*chip, c), me).wait_recv()
                cp = copy(a, 4 + j, (*chip, c), sibling)
                cp.start()
                passed.append(cp)
        for a in range(n):
            copy(a, 0, sibling, me).wait_recv()
            for j, chip in enumerate(chips):
                copy(a, 4 + j, (*chip, 1 - c), me).wait_recv()
        for cp in first + passed:
            cp.wait_send()
        for cp in mine:
            cp.wait()

    return _launch(copies, peers, blocks, [jax.ShapeDtypeStruct((N_DEV,) + b.shape, b.dtype) for b in blocks],
                   [pltpu.SemaphoreType.DMA((n, 7)), pltpu.SemaphoreType.DMA((n, 7)), pltpu.SemaphoreType.DMA((n,))],
                   name, collective_id)


def _exchange_core(bufs, name, collective_id=None):
    n = len(bufs)

    def peers():
        x, y, c = _mesh_pos()
        return [(x, y, 1 - c)]

    def copies(ins, outs, send_sems, recv_sems):
        x, y, c = _mesh_pos()
        started = []
        for a in range(n):
            for q in range(4):
                cp = pltpu.make_async_remote_copy(
                    src_ref=ins[a].at[q, 1 - c], dst_ref=outs[a].at[q], send_sem=send_sems.at[a, q],
                    recv_sem=recv_sems.at[a, q], device_id=(x, y, 1 - c), device_id_type=pl.DeviceIdType.MESH)
                cp.start()
                started.append(cp)
        for cp in started:
            cp.wait()

    return _launch(copies, peers, bufs, [jax.ShapeDtypeStruct((4,) + b.shape[2:], b.dtype) for b in bufs],
                   [pltpu.SemaphoreType.DMA((n, 4)), pltpu.SemaphoreType.DMA((n, 4))], name, collective_id)


def _exchange_chips(bufs, name, collective_id=None):
    n = len(bufs)

    def peers():
        x, y, c = _mesh_pos()
        return [(1 - x, y, c), (x, 1 - y, c), (1 - x, 1 - y, c)]

    def copies(ins, outs, send_sems, recv_sems, local_sems):
        x, y, c = _mesh_pos()
        chip = 2 * x + y
        local = [pltpu.make_async_copy(ins[a].at[chip], outs[a].at[chip], local_sems.at[a]) for a in range(n)]
        for cp in local:
            cp.start()
        started = []
        for k in range(1, 4):
            px = 1 - x if k & 2 else x
            py = 1 - y if k & 1 else y
            for a in range(n):
                cp = pltpu.make_async_remote_copy(
                    src_ref=ins[a].at[2 * px + py], dst_ref=outs[a].at[chip], send_sem=send_sems.at[a, k - 1],
                    recv_sem=recv_sems.at[a, k - 1], device_id=(px, py, c), device_id_type=pl.DeviceIdType.MESH)
                cp.start()
                started.append(cp)
        for cp in started:
            cp.wait()
        for cp in local:
            cp.wait()

    return _launch(copies, peers, bufs, [jax.ShapeDtypeStruct(b.shape, b.dtype) for b in bufs],
                   [pltpu.SemaphoreType.DMA((n, 3)), pltpu.SemaphoreType.DMA((n, 3)), pltpu.SemaphoreType.DMA((n,))],
                   name, collective_id)


_SMALL = ("b_ada", "norm_g", "q_norm_g", "k_norm_g", "sinks", "dw_b", "ln_g", "ln_b", "pw2_b", "a_log", "dt_bias",
          "dn_norm_g", "dw_w", "sconv_w")


def _lane4(v):
    return jnp.pad(v, (0, 124)).reshape(1, 128)


def kernel(x, c, w_ada, b_ada, norm_g, w_in, q_norm_g, k_norm_g, sinks, dw_w, dw_b, ln_g, ln_b, pw2_w, pw2_b, sconv_w, a_log, dt_bias, dn_norm_g, w_proj_a, w_proj_b, w_proj_c, w_out, loss_target, m_w_ada, m_b_ada, m_norm_g, m_w_in, m_q_norm_g, m_k_norm_g, m_sinks, m_dw_w, m_dw_b, m_ln_g, m_ln_b, m_pw2_w, m_pw2_b, m_sconv_w, m_a_log, m_dt_bias, m_dn_norm_g, m_w_proj_a, m_w_proj_b, m_w_proj_c, m_w_out, v_w_ada, v_b_ada, v_norm_g, v_w_in, v_q_norm_g, v_k_norm_g, v_sinks, v_dw_w, v_dw_b, v_ln_g, v_ln_b, v_pw2_w, v_pw2_b, v_sconv_w, v_a_log, v_dt_bias, v_dn_norm_g, v_w_proj_a, v_w_proj_b, v_w_proj_c, v_w_out):
    T = x.shape[1]
    nc = T // CHUNK
    xi, yi, ci = _mesh_pos()
    me = 4 * xi + 2 * yi + ci
    big_w = (w_in, pw2_w, w_proj_a, w_proj_b, w_proj_c, w_out)
    big_m = (m_w_in, m_pw2_w, m_w_proj_a, m_w_proj_b, m_w_proj_c, m_w_out)
    big_v = (v_w_in, v_pw2_w, v_w_proj_a, v_w_proj_b, v_w_proj_c, v_w_out)

    ada_cols = w_ada.shape[2]
    dw_cols, sc_cols = dw_w.shape[2], sconv_w.shape[2]
    flat2 = lambda a: a.reshape(-1, a.shape[-1])
    big16 = [[a[l].astype(BF16) for l in range(DEPTH)] for a in big_w]
    c_all, gdw, gsc = _all_gather([c, dw_w, sconv_w], "gather_small", collective_id=0)
    (gw_in0,) = _all_gather([big16[0][0]], "gather_w_in0", collective_id=7)
    c_all = c_all.reshape(N_DEV, D)
    dw_f = gdw.transpose(1, 2, 0, 3).reshape(DEPTH, CONV_K, 512)
    sc_f = gsc.transpose(1, 2, 0, 3).reshape(DEPTH, DN_K, QKV_C)

    b_my = lax.dynamic_slice(b_ada, (0, me * ada_cols), (DEPTH, ada_cols))
    mod_part = _ada_fwd(c_all, w_ada, b_my, "ada_fwd")
    (gmod,) = _all_gather([mod_part.reshape(-1, 128)], "gather_mod")

    rest0 = [a[0] for a in big16[1:]]
    all1 = [a[1] for a in big16]
    (rest0, all1), gmod = lax.optimization_barrier(((rest0, all1), gmod))
    got0 = [gw_in0] + _all_gather(rest0, "gather_rest0", collective_id=1)
    got1 = _all_gather(all1, "gather_weights1", collective_id=6)
    wp, pw2_f, wa_f, wb_f, wc_f, wo_f = [], [], [], [], [], []
    for l, (gw_in, gpw2, gpa, gpb, gpc, gwo) in enumerate((got0, got1)):
        wp.append(_assemble_w_in(gw_in, f"assemble_w_in{l}"))
        pw2_f.append(gpw2.reshape(512, 512))
        for dst, g in ((wa_f, gpa), (wb_f, gpb), (wc_f, gpc)):
            dst.append(g.transpose(1, 0, 2).reshape(512, D))
        wo_f.append(gwo.reshape(D, D))
    mod_all = gmod.reshape(N_DEV, DEPTH, N_DEV, ada_cols).transpose(1, 2, 0, 3).reshape(DEPTH, N_DEV, 3 * D)
    mod = lax.dynamic_index_in_dim(mod_all, me, axis=1, keepdims=False)
    shift, scale, gate = mod[:, :D], mod[:, D:2 * D], mod[:, 2 * D:]

    xs, saved = [x[0]], []
    for l in range(DEPTH):
        xl = xs[-1]
        h = _norm_fwd(xl, norm_g[l], scale[l], shift[l], f"norm_fwd{l}")
        proj = _mm(h, wp[l], tm=min(2048, T), tn=1152, tk=D, name=f"in_proj{l}")
        ya = _attn_fwd(proj, q_norm_g[l], k_norm_g[l], sinks[l], f"attn_fwd{l}")
        yb = _conf_fwd(proj, dw_f[l], dw_b[l], ln_g[l], ln_b[l], pw2_f[l], pw2_b[l], f"conf_fwd{l}")
        alv, dtv, dng = _lane4(a_log[l]), _lane4(dt_bias[l]), _row(dn_norm_g[l])
        qd, kd, vd, gb = _gdn_prep_fwd(proj, sc_f[l], alv, dtv, f"gdn_prep_fwd{l}")
        gbt = gb[:, :8].reshape(nc, CHUNK, 8).transpose(0, 2, 1)
        u, w, qe, ke, pm, tmi, eg = _gdn_chunk_fwd(qd, kd, vd, gb, gbt, f"gdn_chunk_fwd{l}")
        yc, o, vn, ss = _gdn_scan_fwd(u, w, qe, ke, pm, eg, proj, dng, f"gdn_scan_fwd{l}")
        xs.append(_merge_fwd(xl, proj, ya, yb, yc, wa_f[l], wb_f[l], wc_f[l], wo_f[l], gate[l], f"merge_fwd{l}"))
        saved.append((h, proj, ya, yb, yc, qd, kd, vd, gb, gbt, ss, alv, dtv, dng, w, qe, ke, pm, tmi, eg, o, vn))

    dout, lsum = _loss_head(xs[-1], loss_target[0], "loss_head")

    small = {name: [None] * DEPTH for name in _SMALL}
    big_parts = [None] * DEPTH
    core = jnp.reshape(ci, (1,)).astype(jnp.int32)
    for l in reversed(range(DEPTH)):
        h, proj, ya, yb, yc, qd, kd, vd, gb, gbt, ss, alv, dtv, dng, w, qe, ke, pm, tmi, eg, o, vn = saved[l]
        dproj, dya, dyb, dyc, g_wa, g_wb, g_wc, g_wo, g_gate = _merge_bwd(
            dout, proj, ya, yb, yc, wa_f[l], wb_f[l], wc_f[l], wo_f[l], gate[l], f"merge_bwd{l}")
        dproj, dkv_a, g_q, g_k, g_s = _attn_bwd(dproj, proj, dya, q_norm_g[l], k_norm_g[l], sinks[l], f"attn_bwd{l}")
        dproj, du1, g_pw2, st_b = _conf_bwd1(dproj, proj, dyb, dw_f[l], dw_b[l], ln_g[l], ln_b[l], pw2_f[l], pw2_b[l],
                                             f"conf_bwd_a{l}")
        dproj, g_dw = _conf_bwd2(dproj, proj, du1, dw_f[l], f"conf_bwd_b{l}")
        dproj, do, dvn, dsn, g_dn = _gdn_scan_bwd(dproj, w, qe, ke, pm, eg, o, proj, dyc, dng, f"gdn_scan_bwd{l}")
        dqd, dkd, dvd, dgb = _gdn_chunk_grad(qd, kd, vd, gb, gbt, tmi, ss, dsn, do, dvn, vn, f"gdn_chunk_bwd{l}")
        dproj, dpre, st_c = _gdn_prep_bwd1(dproj, proj, dqd, dkd, dvd, dgb, dkv_a, sc_f[l], alv, dtv,
                                           f"gdn_prep_bwd_a{l}")
        dproj, g_sc = _gdn_prep_bwd2(dproj, proj, dpre, sc_f[l], f"gdn_prep_bwd_b{l}")
        g_wp = _mm(h, dproj, ta=True, tm=D, tn=1152, tk=min(2048, T), name=f"d_w_in{l}")
        by_dest = [_split_w_in_grad(g_wp, f"split_w_in_grad{l}"), g_pw2.reshape(N_DEV, -1, 512).astype(BF16)]
        by_dest += [g.reshape(512, N_DEV, -1).transpose(1, 0, 2).astype(BF16) for g in (g_wa, g_wb, g_wc)]
        by_dest.append(g_wo.reshape(N_DEV, -1, D).astype(BF16))
        by_dest = [b.reshape(4, 2, -1, b.shape[-1]) for b in by_dest]
        if l < DEPTH - 1:
            by_dest, big_parts[l + 1] = lax.optimization_barrier((by_dest, big_parts[l + 1]))
        from_sibling = _exchange_core(by_dest, f"exchange_grads_core{l}", collective_id=2 + 2 * l)

        def input_grad(dproj, dout):
            dh = _mm(dproj, wp[l], tb=True, tm=min(1024, T), tn=D, tk=2688, name=f"d_h{l}")
            return _norm_bwd(dh, xs[l], dout, norm_g[l], scale[l], f"norm_bwd{l}")

        if l > 0:
            dout, st_n = input_grad(dproj, dout)
            from_sibling, dout = lax.optimization_barrier((from_sibling, dout))
        else:
            from_sibling, _ = lax.optimization_barrier((from_sibling, (flat2(m_w_in), flat2(v_w_in))))
        chip_sums = [_pair_sum(core, b, r, f"pair_sum{l}_{i}") for i, (b, r) in enumerate(zip(by_dest, from_sibling))]
        big_parts[l] = _exchange_chips(chip_sums, f"exchange_grads_chips{l}", collective_id=3 + 2 * l)
        if l > 0:
            dout, chip_sums = lax.optimization_barrier((dout, chip_sums))
        else:
            dproj, chip_sums = lax.optimization_barrier((dproj, chip_sums))
            dout, st_n = input_grad(dproj, dout)
        for name, g in (("b_ada", jnp.concatenate([st_n[0], st_n[1], g_gate[0]])), ("norm_g", st_n[2]),
                        ("q_norm_g", g_q.reshape(ATT_HEADS, ATT_HD).sum(0)), ("k_norm_g", g_k.reshape(2, ATT_HD).sum(0)),
                        ("sinks", g_s[0]), ("dw_b", st_b[3]),
                        ("ln_g", st_b[1]), ("ln_b", st_b[2]), ("pw2_b", st_b[0]), ("a_log", st_c[0, :4]),
                        ("dt_bias", st_c[1, :4]), ("dn_norm_g", g_dn[0]), ("dw_w", g_dw[:CONV_K]),
                        ("sconv_w", g_sc[:DN_K])):
            small[name][l] = g
    grad_x = dout[None]

    big_parts, dout = lax.optimization_barrier((big_parts, dout))
    sum_big = lambda i: _sum_adamw(big_parts[0][i], big_parts[1][i], flat2(big_w[i]), flat2(big_m[i]),
                                   flat2(big_v[i]), f"sum_adamw{i}")
    res = [sum_big(0)]

    names = list(_SMALL)
    gathered = _all_gather([jnp.stack(small[n]) for n in names] + [lsum], "gather_small_grads")
    gparts = dict(zip(names, gathered))
    loss = 0.5 * jnp.sum(jnp.sum(gathered[-1], axis=(1, 2))) / D
    dmod_my = lax.dynamic_slice(gparts["b_ada"], (0, 0, me * ada_cols), (N_DEV, DEPTH, ada_cols)).transpose(1, 0, 2)
    g_w_ada = _ada_bwd(c_all, dmod_my, "ada_bwd")
    gparts["dw_w"] = lax.dynamic_slice(gparts["dw_w"], (0, 0, 0, me * dw_cols), (N_DEV, DEPTH, CONV_K, dw_cols))
    gparts["sconv_w"] = lax.dynamic_slice(gparts["sconv_w"], (0, 0, 0, me * sc_cols), (N_DEV, DEPTH, DN_K, sc_cols))
    env = dict(b_ada=(b_ada, m_b_ada, v_b_ada), norm_g=(norm_g, m_norm_g, v_norm_g),
               q_norm_g=(q_norm_g, m_q_norm_g, v_q_norm_g), k_norm_g=(k_norm_g, m_k_norm_g, v_k_norm_g),
               sinks=(sinks, m_sinks, v_sinks), dw_b=(dw_b, m_dw_b, v_dw_b), ln_g=(ln_g, m_ln_g, v_ln_g),
               ln_b=(ln_b, m_ln_b, v_ln_b), pw2_b=(pw2_b, m_pw2_b, v_pw2_b), a_log=(a_log, m_a_log, v_a_log),
               dt_bias=(dt_bias, m_dt_bias, v_dt_bias), dn_norm_g=(dn_norm_g, m_dn_norm_g, v_dn_norm_g),
               dw_w=(dw_w, m_dw_w, v_dw_w), sconv_w=(sconv_w, m_sconv_w, v_sconv_w))
    upd = _sum_adamw_many([gparts[n] for n in names], [env[n][0] for n in names], [env[n][1] for n in names],
                          [env[n][2] for n in names], "sum_adamw_small")

    d_ada, nm_ada, nv_ada = (u.reshape(w_ada.shape) for u in
                             _adamw(flat2(w_ada), flat2(g_w_ada), flat2(m_w_ada), flat2(v_w_ada), "adamw_w_ada"))

    g_small, d_small, m_small, v_small = (dict(zip(names, u)) for u in upd)
    res += [sum_big(i) for i in range(1, len(big_w))]
    g_big, d_big, m_big, v_big =([r[k].reshape(w.shape) for r, w in zip(res, big_w)] for k in range(4))

    order = ("w_ada", "b_ada", "norm_g", "w_in", "q_norm_g", "k_norm_g", "sinks", "dw_w", "dw_b", "ln_g", "ln_b",
             "pw2_w", "pw2_b", "sconv_w", "a_log", "dt_bias", "dn_norm_g", "w_proj_a", "w_proj_b", "w_proj_c", "w_out")
    big_names = ("w_in", "pw2_w", "w_proj_a", "w_proj_b", "w_proj_c", "w_out")

    def pick(kind):
        src_small = (g_small, d_small, m_small, v_small)[kind]
        src_big = (g_big, d_big, m_big, v_big)[kind]
        src_ada = (g_w_ada, d_ada, nm_ada, nv_ada)[kind]
        return [src_ada if n == "w_ada" else src_big[big_names.index(n)] if n in big_names else src_small[n]
                for n in order]

    return (loss, grad_x, *pick(0), *pick(1), *pick(2), *pick(3))
```

```python
import functools
import math

import jax
import jax.numpy as jnp
import numpy as np
from jax import lax
from jax.experimental import pallas as pl
from jax.experimental.pallas import tpu as pltpu
from jax.experimental.pallas import tpu_sc as plsc

F32 = jnp.float32
BF16 = jnp.bfloat16
HI = lax.Precision.HIGHEST

N_DEV = 8
D = 1024
DEPTH = 2
EPS = 1e-6
NEG_INF = -1e30
WINDOW = 128
ATT_HEADS = 8
ATT_HD = 64
CONV_K = 31
DN_HEADS = 4
DN_K = 4
CHUNK = 64
D_IN = 7944
VMEM_LIMIT = 56 * 1024 * 1024

C_MG, C_QA, C_ZA, C_ZB, C_QC, C_KC, C_VC, C_GV, C_GG, C_ZC, C_KA, C_VA, C_AB, NP = (
    0, 3072, 3584, 4096, 4608, 5120, 5632, 6144, 6656, 7168, 7680, 7808, 7936, 8064)
_PAD_FROM = ((4872, 7944), (0, 512), (768, 1280), (2304, 2816), (2816, 4352), (1280, 2304), (4360, 4872),
             (512, 768), (4352, 4360))

ALIBI = tuple(float(2.0 ** (-8.0 * (h + 1) / ATT_HEADS)) for h in range(ATT_HEADS))

ADAM_LR, ADAM_B1, ADAM_B2, ADAM_EPS, ADAM_WD, ADAM_STEP = 0.001, 0.9, 0.999, 1e-08, 0.01, 10


def _cparams(sem=None):
    return pltpu.CompilerParams(dimension_semantics=sem, vmem_limit_bytes=VMEM_LIMIT)


def _sig(x):
    return jax.nn.sigmoid(x)


def _silu(x):
    return x * _sig(x)


def _dsilu(x):
    s = _sig(x)
    return s * (1.0 + x * (1.0 - s))


def _dot(a, b, dims=((1,), (0,)), precision=None):
    return lax.dot_general(a, b, (dims, ((), ())), preferred_element_type=F32, precision=precision)


def _bdot(a, b, dims=((1,), (0,))):
    return _dot(a.astype(BF16), b.astype(BF16), dims)


NN, NT, TN = ((1,), (0,)), ((1,), (1,)), ((0,), (0,))


def _row(v):
    return v.reshape(1, -1)


def _mm(a, b, *, ta=False, tb=False, tm, tn, tk, name):
    M, K = (a.shape[1], a.shape[0]) if ta else a.shape
    N = b.shape[0] if tb else b.shape[1]
    assert M % tm == 0 and N % tn == 0 and K % tk == 0, (M, N, K, tm, tn, tk)
    nk = K // tk
    dims = ((0 if ta else 1,), (1 if tb else 0,))

    def body(a_ref, b_ref, o_ref):
        k = pl.program_id(2)
        part = _bdot(a_ref[...], b_ref[...], dims)

        @pl.when(k == 0)
        def _():
            o_ref[...] = part

        @pl.when(k > 0)
        def _():
            o_ref[...] += part

    a_spec = pl.BlockSpec((tk, tm), lambda i, j, k: (k, i)) if ta else pl.BlockSpec((tm, tk), lambda i, j, k: (i, k))
    b_spec = pl.BlockSpec((tn, tk), lambda i, j, k: (j, k)) if tb else pl.BlockSpec((tk, tn), lambda i, j, k: (k, j))
    return pl.pallas_call(
        body, name=name, grid=(M // tm, N // tn, nk),
        in_specs=[a_spec, b_spec], out_specs=pl.BlockSpec((tm, tn), lambda i, j, k: (i, j)),
        out_shape=jax.ShapeDtypeStruct((M, N), F32),
        compiler_params=_cparams(("parallel", "parallel", "arbitrary")),
    )(a, b)


def _norm_fwd(x, norm_g, scale, shift, name):
    T = x.shape[0]
    tm = min(512, T)

    def body(x_ref, g_ref, sc_ref, sh_ref, h_ref):
        xv = x_ref[...]
        r = lax.rsqrt(jnp.mean(xv * xv, axis=-1, keepdims=True) + EPS)
        h_ref[...] = ((xv * r) * g_ref[...] * (1.0 + sc_ref[...]) + sh_ref[...]).astype(BF16)

    vec = pl.BlockSpec((1, D), lambda i: (0, 0))
    return pl.pallas_call(
        body, name=name, grid=(T // tm,),
        in_specs=[pl.BlockSpec((tm, D), lambda i: (i, 0)), vec, vec, vec],
        out_specs=pl.BlockSpec((tm, D), lambda i: (i, 0)),
        out_shape=jax.ShapeDtypeStruct((T, D), BF16),
        compiler_params=_cparams(("parallel",)),
    )(x, _row(norm_g), _row(scale), _row(shift))


def _norm_bwd(dh, x, dres, norm_g, scale, name):
    T = x.shape[0]
    tm = min(512, T)

    def body(dh_ref, x_ref, dr_ref, g_ref, sc_ref, dx_ref, st_ref):
        i = pl.program_id(0)
        xv, dhv = x_ref[...], dh_ref[...]
        r = lax.rsqrt(jnp.mean(xv * xv, axis=-1, keepdims=True) + EPS)
        xh = xv * r
        g, s1 = g_ref[...], 1.0 + sc_ref[...]
        dxh = dhv * (g * s1)
        dx_ref[...] = dr_ref[...] + r * (dxh - xh * jnp.mean(dxh * xh, axis=-1, keepdims=True))
        dhx = dhv * xh
        upd = jnp.concatenate([jnp.sum(dhv, axis=0, keepdims=True), jnp.sum(dhx * g, axis=0, keepdims=True),
                               jnp.sum(dhx * s1, axis=0, keepdims=True), jnp.zeros((5, D), F32)], axis=0)

        @pl.when(i == 0)
        def _():
            st_ref[...] = upd

        @pl.when(i > 0)
        def _():
            st_ref[...] += upd

    vec = pl.BlockSpec((1, D), lambda i: (0, 0))
    blk = pl.BlockSpec((tm, D), lambda i: (i, 0))
    return pl.pallas_call(
        body, name=name, grid=(T // tm,),
        in_specs=[blk, blk, blk, vec, vec],
        out_specs=[blk, pl.BlockSpec((8, D), lambda i: (0, 0))],
        out_shape=[jax.ShapeDtypeStruct((T, D), F32), jax.ShapeDtypeStruct((8, D), F32)],
        compiler_params=_cparams(("arbitrary",)),
    )(dh, x, dres, _row(norm_g), _row(scale))


def _rms(x, g):
    r = lax.rsqrt(jnp.mean(x * x, axis=-1, keepdims=True) + EPS)
    return x * r, r


def _head_mean_matrix():
    head = np.arange(ATT_HEADS * ATT_HD) // ATT_HD
    return jnp.asarray((head[:, None] == head[None, :]) * (1.0 / ATT_HD), BF16)


def _head_rms(x, hm):
    r = lax.rsqrt(_dot_exact(x * x, hm) + EPS)
    return x * r, r


def _head_rms_bwd(dy, xh, r, g, hm):
    dxh = dy * g
    return r * (dxh - xh * _dot_exact(dxh * xh, hm)), dy * xh


def _attn_mask(n):
    qi = lax.broadcasted_iota(jnp.int32, (WINDOW, 2 * WINDOW), 0)
    kj = lax.broadcasted_iota(jnp.int32, (WINDOW, 2 * WINDOW), 1)
    dist = qi + WINDOW - kj
    valid = (dist >= 0) & (dist < WINDOW) & ((n > 0) | (kj >= WINDOW))
    return valid, dist.astype(F32)


def _attn_probs(s, h, sink, valid, distf):
    s = s - ALIBI[h] * distf
    s = jnp.where(valid, s, NEG_INF)
    m = jnp.maximum(jnp.max(s, axis=-1, keepdims=True), sink)
    p = jnp.exp(s - m)
    es = jnp.exp(sink - m)
    den = jnp.sum(p, axis=-1, keepdims=True) + es
    return p / den, es / den


def _attn_fwd(proj, q_norm_g, k_norm_g, sinks, name):
    T = proj.shape[0]
    nb = T // WINDOW

    def body(sink_ref, q_ref, z_ref, kc_ref, kp_ref, vc_ref, vp_ref, qg_ref, kg_ref, hm_ref, o_ref):
        n = pl.program_id(0)
        valid, distf = _attn_mask(n)
        k2 = jnp.concatenate([kp_ref[...], kc_ref[...]], axis=0)
        v2 = jnp.concatenate([vp_ref[...], vc_ref[...]], axis=0).astype(BF16)
        kn = (_head_rms(k2, hm_ref[:128, :128])[0] * kg_ref[...]).astype(BF16)
        qn = ((_head_rms(q_ref[...], hm_ref[...])[0] * qg_ref[...]) * (ATT_HD ** -0.5)).astype(BF16)

        def head(h):
            sl, gsl = slice(64 * h, 64 * h + 64), slice(64 * (h // 4), 64 * (h // 4) + 64)
            s = _dot(qn[:, sl], kn[:, gsl], NT)
            yield
            p, _ = _attn_probs(s, h, sink_ref[h], valid, distf)
            o_ref[:, sl] = _dot(p.astype(BF16), v2[:, gsl])
            yield

        _lockstep([head(h) for h in range(ATT_HEADS)])
        o_ref[...] = o_ref[...] * _silu(z_ref[...])

    prev = lambda n: jnp.maximum(n - 1, 0)
    return pl.pallas_call(
        body, name=name, grid=(nb,),
        in_specs=[pl.BlockSpec(memory_space=pltpu.SMEM),
                  pl.BlockSpec((WINDOW, 512), lambda n: (n, C_QA // 512)),
                  pl.BlockSpec((WINDOW, 512), lambda n: (n, C_ZA // 512)),
                  pl.BlockSpec((WINDOW, 128), lambda n: (n, C_KA // 128)),
                  pl.BlockSpec((WINDOW, 128), lambda n: (prev(n), C_KA // 128)),
                  pl.BlockSpec((WINDOW, 128), lambda n: (n, C_VA // 128)),
                  pl.BlockSpec((WINDOW, 128), lambda n: (prev(n), C_VA // 128)),
                  pl.BlockSpec((1, 512), lambda n: (0, 0)), pl.BlockSpec((1, 128), lambda n: (0, 0)),
                  pl.BlockSpec((512, 512), lambda n: (0, 0))],
        out_specs=pl.BlockSpec((WINDOW, 512), lambda n: (n, 0)),
        out_shape=jax.ShapeDtypeStruct((T, 512), F32),
        compiler_params=_cparams(("parallel",)),
    )(sinks, proj, proj, proj, proj, proj, proj, _row(jnp.tile(q_norm_g, ATT_HEADS)), _row(jnp.tile(k_norm_g, 2)),
      _head_mean_matrix())


def _rms_bwd(dy, xh, r, g):
    dxh = dy * g
    return r * (dxh - xh * jnp.mean(dxh * xh, axis=-1, keepdims=True)), dy * xh


def _attn_bwd(dproj, proj, dya, q_norm_g, k_norm_g, sinks, name):
    T = proj.shape[0]
    nb = T // WINDOW

    def body(sink_ref, dp_any, q_ref, z_ref, kc_ref, kp_ref, vc_ref, vp_ref, dy_ref, qg_ref, kg_ref, hm_ref,
             dqz_ref, dkv_ref, gq_ref, gk_ref, gs_ref, ck_ref, cv_ref, o_sc, dq_sc):
        n = pl.program_id(0)

        @pl.when(n == 0)
        def _():
            gq_ref[...] = jnp.zeros_like(gq_ref)
            gk_ref[...] = jnp.zeros_like(gk_ref)
            gs_ref[...] = jnp.zeros_like(gs_ref)
            ck_ref[...] = jnp.zeros_like(ck_ref)
            cv_ref[...] = jnp.zeros_like(cv_ref)

        lane8 = lax.broadcasted_iota(jnp.int32, (1, 8), 1)

        @pl.when(n < nb)
        def _():
            valid, distf = _attn_mask(n)
            k2 = jnp.concatenate([kp_ref[...], kc_ref[...]], axis=0)
            v2 = jnp.concatenate([vp_ref[...], vc_ref[...]], axis=0).astype(BF16)
            kn = (_head_rms(k2, hm_ref[:128, :128])[0] * kg_ref[...]).astype(BF16)
            qh, qr = _head_rms(q_ref[...], hm_ref[...])
            qn = ((qh * qg_ref[...]) * (ATT_HD ** -0.5)).astype(BF16)
            zs = z_ref[...]
            do_all = dy_ref[...] * _silu(zs)
            dob_all = do_all.astype(BF16)

            def head(h):
                sl, gsl = slice(64 * h, 64 * h + 64), slice(64 * (h // 4), 64 * (h // 4) + 64)
                s = _dot(qn[:, sl], kn[:, gsl], NT)
                dpm = _dot(dob_all[:, sl], v2[:, gsl], NT)
                yield
                p, ps = _attn_probs(s, h, sink_ref[h], valid, distf)
                pb = p.astype(BF16)
                o_sc[:, sl] = _dot(pb, v2[:, gsl])
                dvg = _dot(pb, dob_all[:, sl], TN)
                delta = jnp.sum(p * dpm, axis=-1, keepdims=True)
                ds = (p * (dpm - delta)).astype(BF16)
                gs = jnp.where(lane8 == h, -jnp.sum(ps * delta, axis=0, keepdims=True), 0.0)
                yield
                dkn = _dot(ds, qn[:, sl], TN)
                dq_sc[:, sl] = _dot(ds, kn[:, gsl])
                yield
                return dkn, dvg, gs

            res = _lockstep([head(h) for h in range(ATT_HEADS)])
            dqz_ref[:, 512:] = (dy_ref[...] * o_sc[...] * _dsilu(zs)).astype(BF16)
            dq, gq = _head_rms_bwd(dq_sc[...] * (ATT_HD ** -0.5), qh, qr, qg_ref[...], hm_ref[...])
            dqz_ref[:, :512] = dq.astype(BF16)
            gq_acc = jnp.sum(gq, axis=0, keepdims=True)
            gs_acc = sum(r[2] for r in res[1:]) + res[0][2]
            for g in range(2):
                dkn = (res[4 * g][0] + res[4 * g + 1][0]) + (res[4 * g + 2][0] + res[4 * g + 3][0])
                dvg = (res[4 * g][1] + res[4 * g + 1][1]) + (res[4 * g + 2][1] + res[4 * g + 3][1])
                ksl = slice(64 * g, 64 * g + 64)
                vsl = slice(128 + 64 * g, 128 + 64 * g + 64)
                dkv_ref[:, ksl] = ck_ref[:, ksl] + dkn[:WINDOW]
                dkv_ref[:, vsl] = cv_ref[:, ksl] + dvg[:WINDOW]
                ck_ref[:, ksl] = dkn[WINDOW:]
                cv_ref[:, ksl] = dvg[WINDOW:]
            gq_ref[...] += gq_acc
            gs_ref[...] += gs_acc

        @pl.when(n == nb)
        def _():
            dkv_ref[:, :128] = ck_ref[...]
            dkv_ref[:, 128:] = cv_ref[...]

        @pl.when(n > 0)
        def _():
            hm = hm_ref[:128, :128]
            kh, kr = _head_rms(kp_ref[...], hm)
            dk, gk = _head_rms_bwd(dkv_ref[:, :128], kh, kr, kg_ref[...], hm)
            dkv_ref[:, :128] = dk
            gk_ref[...] += jnp.sum(gk, axis=0, keepdims=True)

    cur = lambda n: jnp.minimum(n, nb - 1)
    prev = lambda n: jnp.maximum(n - 1, 0)
    small = lambda w: pl.BlockSpec((1, w), lambda n: (0, 0))
    return pl.pallas_call(
        body, name=name, grid=(nb + 1,),
        in_specs=[pl.BlockSpec(memory_space=pltpu.SMEM), pl.BlockSpec(memory_space=pl.ANY),
                  pl.BlockSpec((WINDOW, 512), lambda n: (cur(n), C_QA // 512)),
                  pl.BlockSpec((WINDOW, 512), lambda n: (cur(n), C_ZA // 512)),
                  pl.BlockSpec((WINDOW, 128), lambda n: (cur(n), C_KA // 128)),
                  pl.BlockSpec((WINDOW, 128), lambda n: (prev(n), C_KA // 128)),
                  pl.BlockSpec((WINDOW, 128), lambda n: (cur(n), C_VA // 128)),
                  pl.BlockSpec((WINDOW, 128), lambda n: (prev(n), C_VA // 128)),
                  pl.BlockSpec((WINDOW, 512), lambda n: (cur(n), 0)),
                  small(512), small(128), pl.BlockSpec((512, 512), lambda n: (0, 0))],
        out_specs=[pl.BlockSpec((WINDOW, 1024), lambda n: (cur(n), C_QA // 1024)),
                   pl.BlockSpec((WINDOW, 256), lambda n: (prev(n), 0)),
                   small(512), small(128), small(8)],
        out_shape=[jax.ShapeDtypeStruct(dproj.shape, BF16), jax.ShapeDtypeStruct((T, 256), F32),
                   jax.ShapeDtypeStruct((1, 512), F32), jax.ShapeDtypeStruct((1, 128), F32),
                   jax.ShapeDtypeStruct((1, 8), F32)],
        scratch_shapes=[pltpu.VMEM((WINDOW, 128), F32), pltpu.VMEM((WINDOW, 128), F32),
                        pltpu.VMEM((WINDOW, 512), F32), pltpu.VMEM((WINDOW, 512), F32)],
        input_output_aliases={1: 0},
        compiler_params=_cparams(("arbitrary",)),
    )(sinks, dproj, proj, proj, proj, proj, proj, proj, dya, _row(jnp.tile(q_norm_g, ATT_HEADS)),
      _row(jnp.tile(k_norm_g, 2)), _head_mean_matrix())


HALO_B = 32


def _conf_specs(T, tm):
    r = tm // HALO_B
    cur = lambda c: pl.BlockSpec((tm, 512), lambda i: (i, c // 512))
    prev = lambda c: pl.BlockSpec((HALO_B, 512), lambda i: (jnp.maximum(i * r - 1, 0), c // 512))
    return cur, prev


SUB = 8
ROW_CHUNK = 64


def _shifted_copies(ext_ref, sh_ref):
    total = ext_ref.shape[0]
    for r in range(SUB):
        rows = total if r == 0 else total - SUB
        sh_ref[r, :rows, :] = ext_ref[pl.ds(r, rows), :]


def _taps_by_shift(offsets):
    groups = {}
    for k, o in enumerate(offsets):
        q, r = divmod(o, SUB)
        groups.setdefault(r, []).append((k, q))
    return groups


def _conv_taps(sh_ref, w_ref, offsets, out_ref, init):
    groups = _taps_by_shift(offsets)

    def chunk(ci, carry):
        r0 = pl.multiple_of(ci * ROW_CHUNK, ROW_CHUNK)
        acc = jnp.zeros((ROW_CHUNK, out_ref.shape[1]), F32) + init
        for r, taps in groups.items():
            win = sh_ref[r, pl.ds(r0, ROW_CHUNK + SUB * max(q for _, q in taps)), :]
            for k, q in taps:
                acc = acc + w_ref[k:k + 1, :] * win[SUB * q:SUB * q + ROW_CHUNK]
        out_ref[pl.ds(r0, ROW_CHUNK), :] = acc
        return carry

    lax.fori_loop(0, out_ref.shape[0] // ROW_CHUNK, chunk, 0)


def _conv_weight_grad(sh_ref, d_ref, offsets):
    tm, width = d_ref.shape
    out = [None] * len(offsets)
    for r, taps in _taps_by_shift(offsets).items():
        def chunk(ci, accs, r=r, taps=taps):
            r0 = pl.multiple_of(ci * ROW_CHUNK, ROW_CHUNK)
            d = d_ref[pl.ds(r0, ROW_CHUNK), :]
            win = sh_ref[r, pl.ds(r0, ROW_CHUNK + SUB * max(q for _, q in taps)), :]
            return tuple(a + jnp.sum((d * win[SUB * q:SUB * q + ROW_CHUNK]).reshape(ROW_CHUNK // SUB, SUB, width),
                                     axis=0) for a, (_, q) in zip(accs, taps))

        accs = lax.fori_loop(0, tm // ROW_CHUNK, chunk, tuple(jnp.zeros((SUB, width), F32) for _ in taps))
        for a, (k, _) in zip(accs, taps):
            out[k] = jnp.sum(a, axis=0, keepdims=True)
    return out


def _conf_scratch(tm):
    return [pltpu.VMEM((tm + HALO_B, 512), F32), pltpu.VMEM((SUB, tm + HALO_B, 512), F32), pltpu.VMEM((tm, 512), F32)]


def _conf_core(i, tm, gv_ref, gg_ref, gvp_ref, ggp_ref, w_ref, b_ref, lg_ref, lb_ref, pw_ref, pb_ref, ext_ref, sh_ref,
               cv_ref):
    up = gvp_ref[...] * _sig(ggp_ref[...])
    ext_ref[:HALO_B] = jnp.where(i > 0, up, 0.0)
    ext_ref[HALO_B:] = gv_ref[...] * _sig(gg_ref[...])
    _shifted_copies(ext_ref, sh_ref)
    _conv_taps(sh_ref, w_ref, [HALO_B - CONV_K + 1 + k for k in range(CONV_K)], cv_ref, b_ref[...])
    acc = cv_ref[...]
    mu = jnp.mean(acc, axis=-1, keepdims=True)
    xc = acc - mu
    rstd = lax.rsqrt(jnp.mean(xc * xc, axis=-1, keepdims=True) + EPS)
    xh = xc * rstd
    u2 = xh * lg_ref[...] + lb_ref[...]
    u3 = _silu(u2)
    ypre = _bdot(u3, pw_ref[...]) + pb_ref[...]
    return xh, rstd, u2, u3, ypre


def _conf_fwd(proj, dw_w, dw_b, ln_g, ln_b, pw2, pw2_b, name):
    T = proj.shape[0]
    tm = min(512, T)
    cur, prev = _conf_specs(T, tm)

    def body(gv_ref, gg_ref, gvp_ref, ggp_ref, zb_ref, w_ref, b_ref, lg_ref, lb_ref, pw_ref, pb_ref, o_ref, *scratch):
        i = pl.program_id(0)
        ypre = _conf_core(i, tm, gv_ref, gg_ref, gvp_ref, ggp_ref, w_ref, b_ref, lg_ref, lb_ref, pw_ref, pb_ref,
                          *scratch)[4]
        o_ref[...] = ypre * _silu(zb_ref[...])

    full = lambda s: pl.BlockSpec(s, lambda i: (0, 0))
    return pl.pallas_call(
        body, name=name, grid=(T // tm,),
        in_specs=[cur(C_GV), cur(C_GG), prev(C_GV), prev(C_GG), cur(C_ZB), full((CONV_K, 512)), full((1, 512)),
                  full((1, 512)), full((1, 512)), full((512, 512)), full((1, 512))],
        out_specs=pl.BlockSpec((tm, 512), lambda i: (i, 0)),
        out_shape=jax.ShapeDtypeStruct((T, 512), F32),
        scratch_shapes=_conf_scratch(tm),
        compiler_params=_cparams(("parallel",)),
    )(proj, proj, proj, proj, proj, dw_w, _row(dw_b), _row(ln_g), _row(ln_b), pw2, _row(pw2_b))


def _conf_bwd1(dproj, proj, dyb, dw_w, dw_b, ln_g, ln_b, pw2, pw2_b, name):
    T = proj.shape[0]
    tm = min(512, T)
    cur, prev = _conf_specs(T, tm)

    def body(dp_any, gv_ref, gg_ref, gvp_ref, ggp_ref, zb_ref, dy_ref, w_ref, b_ref, lg_ref, lb_ref, pw_ref, pb_ref,
             dzb_ref, du1_ref, gpw_ref, st_ref, *scratch):
        i = pl.program_id(0)
        xh, rstd, u2, u3, ypre = _conf_core(i, tm, gv_ref, gg_ref, gvp_ref, ggp_ref, w_ref, b_ref, lg_ref, lb_ref,
                                            pw_ref, pb_ref, *scratch)
        zb, dy = zb_ref[...], dy_ref[...]
        dzb_ref[...] = (dy * ypre * _dsilu(zb)).astype(BF16)
        dyp = dy * _silu(zb)
        du2 = _bdot(dyp, pw_ref[...], NT) * _dsilu(u2)
        dxh = du2 * lg_ref[...]
        du1 = rstd * (dxh - jnp.mean(dxh, axis=-1, keepdims=True) - xh * jnp.mean(dxh * xh, axis=-1, keepdims=True))
        du1_ref[...] = du1
        gpw = _bdot(u3, dyp, TN)
        rs = lambda a: jnp.sum(a, axis=0, keepdims=True)
        upd = jnp.concatenate([rs(dyp), rs(du2 * xh), rs(du2), rs(du1), jnp.zeros((4, 512), F32)], axis=0)

        @pl.when(i == 0)
        def _():
            gpw_ref[...] = gpw
            st_ref[...] = upd

        @pl.when(i > 0)
        def _():
            gpw_ref[...] += gpw
            st_ref[...] += upd

    full = lambda s: pl.BlockSpec(s, lambda i: (0, 0))
    blk = pl.BlockSpec((tm, 512), lambda i: (i, 0))
    return pl.pallas_call(
        body, name=name, grid=(T // tm,),
        in_specs=[pl.BlockSpec(memory_space=pl.ANY), cur(C_GV), cur(C_GG), prev(C_GV), prev(C_GG), cur(C_ZB), blk,
                  full((CONV_K, 512)), full((1, 512)), full((1, 512)), full((1, 512)), full((512, 512)), full((1, 512))],
        out_specs=[cur(C_ZB), blk, full((512, 512)), full((8, 512))],
        out_shape=[jax.ShapeDtypeStruct(dproj.shape, BF16), jax.ShapeDtypeStruct((T, 512), F32),
                   jax.ShapeDtypeStruct((512, 512), F32), jax.ShapeDtypeStruct((8, 512), F32)],
        scratch_shapes=_conf_scratch(tm),
        input_output_aliases={0: 0},
        compiler_params=_cparams(("arbitrary",)),
    )(dproj, proj, proj, proj, proj, proj, dyb, dw_w, _row(dw_b), _row(ln_g), _row(ln_b), pw2, _row(pw2_b))


def _conf_bwd2(dproj, proj, du1, dw_w, name):
    T = proj.shape[0]
    tm = min(512, T)
    nt = T // tm
    r = tm // HALO_B
    cur, prev = _conf_specs(T, tm)

    def body(dp_any, gv_ref, gg_ref, gvp_ref, ggp_ref, du_ref, dun_ref, w_ref, dglu_ref, gw_ref, ext_ref, sh_ref,
             cv_ref):
        i = pl.program_id(0)
        gv, sg = gv_ref[...], _sig(gg_ref[...])
        ext_ref[:HALO_B] = jnp.where(i > 0, gvp_ref[...] * _sig(ggp_ref[...]), 0.0)
        ext_ref[HALO_B:] = gv * sg
        _shifted_copies(ext_ref, sh_ref)
        rows = _conv_weight_grad(sh_ref, du_ref, [HALO_B - CONV_K + 1 + k for k in range(CONV_K)])
        rows.append(jnp.zeros((1, 512), F32))
        gw = jnp.concatenate(rows, axis=0)
        ext_ref[:tm] = du_ref[...]
        ext_ref[tm:] = jnp.where(i < nt - 1, dun_ref[...], 0.0)
        _shifted_copies(ext_ref, sh_ref)
        _conv_taps(sh_ref, w_ref, [CONV_K - 1 - k for k in range(CONV_K)], cv_ref, 0.0)
        du0 = cv_ref[...]
        dglu_ref[:, :512] = (du0 * sg).astype(BF16)
        dglu_ref[:, 512:] = (du0 * gv * sg * (1.0 - sg)).astype(BF16)

        @pl.when(i == 0)
        def _():
            gw_ref[...] = gw

        @pl.when(i > 0)
        def _():
            gw_ref[...] += gw

    full = lambda s: pl.BlockSpec(s, lambda i: (0, 0))
    return pl.pallas_call(
        body, name=name, grid=(nt,),
        in_specs=[pl.BlockSpec(memory_space=pl.ANY), cur(C_GV), cur(C_GG), prev(C_GV), prev(C_GG),
                  pl.BlockSpec((tm, 512), lambda i: (i, 0)),
                  pl.BlockSpec((HALO_B, 512), lambda i: (jnp.minimum((i + 1) * r, T // HALO_B - 1), 0)),
                  full((CONV_K, 512))],
        out_specs=[pl.BlockSpec((tm, 1024), lambda i: (i, C_GV // 1024)), full((32, 512))],
        out_shape=[jax.ShapeDtypeStruct(dproj.shape, BF16), jax.ShapeDtypeStruct((32, 512), F32)],
        scratch_shapes=_conf_scratch(tm),
        input_output_aliases={0: 0},
        compiler_params=_cparams(("arbitrary",)),
    )(dproj, proj, proj, proj, proj, du1, du1, dw_w)


HALO_C = 8
QKV_C = 1536


def _softplus(x):
    return jnp.maximum(x, 0.0) + jnp.log1p(jnp.exp(-jnp.abs(x)))


def _gdn_conv(i, tm, x_ref, xp_ref, w_ref, ext_ref):
    ext_ref[:HALO_C] = jnp.where(i > 0, xp_ref[...], 0.0)
    ext_ref[HALO_C:] = x_ref[...]
    pre = jnp.zeros((tm, QKV_C), F32)
    for k in range(DN_K):
        pre = pre + w_ref[k:k + 1, :] * ext_ref[pl.ds(HALO_C - DN_K + 1 + k, tm), :]
    return pre


def _gdn_specs(T, tm):
    r = tm // HALO_C
    cur = pl.BlockSpec((tm, QKV_C), lambda i: (i, C_QC // QKV_C))
    prev = pl.BlockSpec((HALO_C, QKV_C), lambda i: (jnp.maximum(i * r - 1, 0), C_QC // QKV_C))
    ab = pl.BlockSpec((tm, 128), lambda i: (i, C_AB // 128))
    return cur, prev, ab


def _gdn_prep_fwd(proj, sconv_w, alog_v, dtb_v, name):
    T = proj.shape[0]
    tm = min(512, T)
    cur, prev, ab = _gdn_specs(T, tm)

    def body(x_ref, xp_ref, ab_ref, w_ref, al_ref, dt_ref, q_ref, k_ref, v_ref, gb_ref, ext_ref):
        i = pl.program_id(0)
        y = _silu(_gdn_conv(i, tm, x_ref, xp_ref, w_ref, ext_ref))
        for h in range(DN_HEADS):
            sl = slice(128 * h, 128 * h + 128)
            qh, kh = y[:, sl], y[:, 512 + 128 * h:512 + 128 * h + 128]
            q_ref[:, sl] = qh * lax.rsqrt(jnp.sum(qh * qh, axis=-1, keepdims=True) + EPS) * (128 ** -0.5)
            k_ref[:, sl] = kh * lax.rsqrt(jnp.sum(kh * kh, axis=-1, keepdims=True) + EPS)
        v_ref[...] = y[:, 1024:]
        abv = ab_ref[...]
        lane = lax.broadcasted_iota(jnp.int32, (tm, 128), 1)
        g = -jnp.exp(al_ref[...]) * _softplus(abv + dt_ref[...])
        gb_ref[...] = jnp.where(lane < DN_HEADS, g, _sig(abv))

    full = lambda s: pl.BlockSpec(s, lambda i: (0, 0))
    blk = pl.BlockSpec((tm, 512), lambda i: (i, 0))
    return pl.pallas_call(
        body, name=name, grid=(T // tm,),
        in_specs=[cur, prev, ab, full((DN_K, QKV_C)), full((1, 128)), full((1, 128))],
        out_specs=[blk, blk, blk, pl.BlockSpec((tm, 128), lambda i: (i, 0))],
        out_shape=[jax.ShapeDtypeStruct((T, 512), F32)] * 3 + [jax.ShapeDtypeStruct((T, 128), F32)],
        scratch_shapes=[pltpu.VMEM((tm + HALO_C, QKV_C), F32)],
        compiler_params=_cparams(("parallel",)),
    )(proj, proj, proj, sconv_w, alog_v, dtb_v)


def _hdot(a, b, dims=NN):
    return _dot(a, b, dims, precision=HI)


def _lockstep(gens):
    results, live = [None] * len(gens), list(range(len(gens)))
    while live:
        for i in list(live):
            try:
                next(gens[i])
            except StopIteration as stop:
                results[i] = stop.value
                live.remove(i)
    return results


def _dot_exact(a, b, dims=NN, split_left=True):
    x = (a if split_left else b).astype(F32)
    hi = x.astype(BF16)
    r = x - hi.astype(F32)
    mid = r.astype(BF16)
    lo = (r - mid.astype(F32)).astype(BF16)
    other = (b if split_left else a).astype(BF16)
    one = (lambda p: _dot(p, other, dims)) if split_left else (lambda p: _dot(other, p, dims))
    return (one(lo) + one(mid)) + one(hi)


def _split(a):
    hi = a.astype(BF16)
    return hi, (a - hi.astype(F32)).astype(BF16)


def _dot3(a, b):
    (ah, al), (bh, bl) = a, b
    return _dot(ah, bh) + (_dot(ah, bl) + _dot(al, bh))


INV_BLOCK = 16


def _tri_inv(mats, eye):
    ii = lax.broadcasted_iota(jnp.int32, (CHUNK, CHUNK), 0) // INV_BLOCK
    jj = lax.broadcasted_iota(jnp.int32, (CHUNK, CHUNK), 1) // INV_BLOCK
    diag = ii == jj
    mul = lambda xs, ys: [_dot3(_split(x), _split(y)) for x, y in zip(xs, ys)]
    ps = [jnp.where(diag, -a, 0.0) for a in mats]
    tds = [eye + p for p in ps]
    for _ in range(3):
        ps = mul(ps, ps)
        tds = [t + tp for t, tp in zip(tds, mul(tds, ps))]
    ms = mul(tds, [jnp.where(diag, 0.0, a) for a in mats])
    m2 = mul(ms, ms)
    inv = [(eye - m) + x for m, x in zip(ms, mul([eye - m for m in ms], m2))]
    return mul(inv, tds)


def _tri_consts():
    ii = lax.broadcasted_iota(jnp.int32, (CHUNK, CHUNK), 0)
    jj = lax.broadcasted_iota(jnp.int32, (CHUNK, CHUNK), 1)
    return ii >= jj, ii > jj, (ii == jj).astype(F32)


def _gdn_local(q, k, v, gcol, grow, bcol, lower, strict):
    dm = jnp.where(lower, jnp.exp(jnp.where(lower, gcol - grow, 0.0)), 0.0)
    kb = k * bcol
    a = jnp.where(strict, _bdot(kb, k, NT) * dm, 0.0)
    gc = jnp.exp(gcol)
    glast = grow[:, CHUNK - 1:CHUNK]
    return dict(q=q, k=k, v=v, bcol=bcol, gcol=gcol, glast=glast, dm=dm, kb=kb, a=a, gc=gc, vb=v * bcol,
                kbg=kb * gc, p=_bdot(q, k, NT) * dm, qe=q * gc, ke=k * jnp.exp(glast - gcol))


def _gdn_chunk_bwd(c, do, dvn, ds_new, lower, strict, ones):
    rs = lambda m: jnp.sum(m, axis=-1, keepdims=True)
    colsum = lambda m: _dot_exact(m, ones, TN)[:, :1]
    q, k, v, bcol, dm, tm, gc, s = c["q"], c["k"], c["v"], c["bcol"], c["dm"], c["tm"], c["gc"], c["s"]
    eg = jnp.exp(c["glast"])
    dqe = _bdot(do, s, NT)
    dp = jnp.where(lower, _bdot(do, c["vn"], NT), 0.0)
    dw = -_bdot(dvn, s, NT)
    dke = _bdot(c["vn"], ds_new, NT)
    dvb = _bdot(tm, dvn, TN)
    yield
    dglast = jnp.sum(rs(ds_new * s), axis=0, keepdims=True) * eg
    dk = dke * jnp.exp(c["glast"] - c["gcol"])
    r_ke = rs(dke * c["ke"])
    dglast = dglast + jnp.sum(r_ke, axis=0, keepdims=True)
    dgam = rs(dqe * c["qe"]) - r_ke
    dq = dqe * gc
    dpm = dp * dm
    mp = dp * c["p"]
    dq = dq + _bdot(dpm, k)
    dk = dk + _bdot(dpm, q, TN)
    dt = _bdot(dvn, c["vb"], NT) + _bdot(dw, c["kbg"], NT)
    dkbg = _bdot(tm, dw, TN)
    dgam = dgam + rs(mp) - colsum(mp)
    yield
    dkb = dkbg * gc
    dgam = dgam + rs(dkbg * c["kbg"])
    dat = _bdot(tm, dt, TN)
    yield
    da = jnp.where(strict, -_bdot(dat, tm, NT), 0.0)
    yield
    dam = da * dm
    ma = da * c["a"]
    dkb = dkb + _bdot(dam, k)
    dk = dk + _bdot(dam, c["kb"], TN)
    dgam = dgam + rs(ma) - colsum(ma)
    yield
    dk = dk + dkb * bcol
    dbeta = rs(dkb * k) + rs(dvb * v)
    dv = dvb * bcol
    row = lax.broadcasted_iota(jnp.int32, (CHUNK, 1), 0)
    dgam = dgam + jnp.where(row == CHUNK - 1, dglast, 0.0)
    dg = _dot_exact(lower, dgam, TN, split_left=False)
    return dq, dk, dv, dg, dbeta


SCAN_GROUP = 8
GROUP = 8


def _chunk_decay(gb_ref, gt_ref, lmat, g):
    rows = slice(CHUNK * g, CHUNK * g + CHUNK)
    return rows, _dot_exact(lmat, gb_ref[rows, :], split_left=False), _dot_exact(gt_ref[g], lmat, NT)


def _gdn_chunk_fwd(qd, kd, vd, gb, gbt, name):
    T = qd.shape[0]
    G = min(GROUP, T // CHUNK)
    ng = T // (CHUNK * G)

    def body(q_ref, k_ref, v_ref, gb_ref, gt_ref, u_ref, w_ref, qe_ref, ke_ref, p_ref, t_ref, eg_ref):
        lower, strict, eye = _tri_consts()
        lmat = lower.astype(F32)
        decay = [_chunk_decay(gb_ref, gt_ref, lmat, g) for g in range(G)]
        chains = [(g, h) for g in range(G) for h in range(DN_HEADS)]
        cs = []
        for g, h in chains:
            rows, gcs, grs = decay[g]
            sl = slice(128 * h, 128 * h + 128)
            c = _gdn_local(q_ref[rows, sl], k_ref[rows, sl], v_ref[rows, sl], gcs[:, h:h + 1], grs[h:h + 1, :],
                           gb_ref[rows, DN_HEADS + h:DN_HEADS + h + 1], lower, strict)
            qe_ref[rows, sl] = c["qe"].astype(BF16)
            ke_ref[rows, sl] = c["ke"].astype(BF16)
            p_ref[rows, 64 * h:64 * h + 64] = c["p"].astype(BF16)
            eg_ref[g, h:h + 1, :] = jnp.broadcast_to(jnp.exp(c["glast"]), (1, 128))
            cs.append(c)
        tms = [t.astype(BF16) for t in _tri_inv([c["a"] for c in cs], eye)]
        us = [_dot(t, c["vb"].astype(BF16)) for t, c in zip(tms, cs)]
        ws = [_dot(t, c["kbg"].astype(BF16)) for t, c in zip(tms, cs)]
        for (g, h), tm, u, w in zip(chains, tms, us, ws):
            rows, sl = decay[g][0], slice(128 * h, 128 * h + 128)
            u_ref[rows, sl] = u
            w_ref[rows, sl] = w.astype(BF16)
            t_ref[rows, 64 * h:64 * h + 64] = tm
        for g in range(G):
            eg_ref[g, DN_HEADS:, :] = jnp.zeros((8 - DN_HEADS, 128), F32)

    blk = pl.BlockSpec((CHUNK * G, 512), lambda n: (n, 0))
    half = pl.BlockSpec((CHUNK * G, 256), lambda n: (n, 0))
    return pl.pallas_call(
        body, name=name, grid=(ng,),
        in_specs=[blk, blk, blk, pl.BlockSpec((CHUNK * G, 128), lambda n: (n, 0)),
                  pl.BlockSpec((G, 8, CHUNK), lambda n: (n, 0, 0))],
        out_specs=[blk, blk, blk, blk, half, half, pl.BlockSpec((G, 8, 128), lambda n: (n, 0, 0))],
        out_shape=[jax.ShapeDtypeStruct((T, 512), F32)] + [jax.ShapeDtypeStruct((T, 512), BF16)] * 3
        + [jax.ShapeDtypeStruct((T, 256), BF16)] * 2 + [jax.ShapeDtypeStruct((T // CHUNK, 8, 128), F32)],
        compiler_params=_cparams(("parallel",)),
    )(qd, kd, vd, gb, gbt)


def _gdn_scan_fwd(u, w, qe, ke, pm, eg, proj, dn_g, name):
    T = u.shape[0]
    nc = T // CHUNK
    G = min(SCAN_GROUP, T // CHUNK)

    def body(u_ref, w_ref, qe_ref, ke_ref, p_ref, eg_ref, z_ref, ng_ref, y_ref, o_ref, vn_ref, ss_ref, s_ref):
        n = pl.program_id(0)

        @pl.when(n == 0)
        def _():
            s_ref[...] = jnp.zeros_like(s_ref)

        def head(j, h):
            rows, sl = slice(CHUNK * j, CHUNK * j + CHUNK), slice(128 * h, 128 * h + 128)
            s = s_ref[h]
            sb = s.astype(BF16)
            vn = u_ref[rows, sl] - _dot(w_ref[rows, sl], sb)
            qs = _dot(qe_ref[rows, sl], sb)
            yield
            vb = vn.astype(BF16)
            o = qs + _dot(p_ref[rows, 64 * h:64 * h + 64], vb)
            s_ref[h] = s * eg_ref[j, h:h + 1, :] + _dot(ke_ref[rows, sl], vb, TN)
            yield
            vn_ref[rows, sl] = vb
            o_ref[rows, sl] = o
            y_ref[rows, sl] = _rms(o, None)[0] * ng_ref[...] * _silu(z_ref[rows, sl])

        for j in range(G):
            ss_ref[j] = s_ref[...]
            _lockstep([head(j, h) for h in range(DN_HEADS)])

    blk = pl.BlockSpec((CHUNK * G, 512), lambda n: (n, 0))
    return pl.pallas_call(
        body, name=name, grid=(nc // G,),
        in_specs=[blk, blk, blk, blk, pl.BlockSpec((CHUNK * G, 256), lambda n: (n, 0)),
                  pl.BlockSpec((G, 8, 128), lambda n: (n, 0, 0)),
                  pl.BlockSpec((CHUNK * G, 512), lambda n: (n, C_ZC // 512)), pl.BlockSpec((1, 128), lambda n: (0, 0))],
        out_specs=[blk, blk, blk, pl.BlockSpec((G, DN_HEADS, 128, 128), lambda n: (n, 0, 0, 0))],
        out_shape=[jax.ShapeDtypeStruct((T, 512), F32), jax.ShapeDtypeStruct((T, 512), F32),
                   jax.ShapeDtypeStruct((T, 512), BF16), jax.ShapeDtypeStruct((nc, DN_HEADS, 128, 128), F32)],
        scratch_shapes=[pltpu.VMEM((DN_HEADS, 128, 128), F32)],
        compiler_params=_cparams(("arbitrary",)),
    )(u, w, qe, ke, pm, eg, proj, dn_g)


def _gdn_scan_bwd(dproj, w, qe, ke, pm, eg, o, proj, dyc, dn_g, name):
    T = o.shape[0]
    nc = T // CHUNK
    G = min(SCAN_GROUP, T // CHUNK)
    rev = lambda n: nc // G - 1 - n

    def body(dp_any, w_ref, qe_ref, ke_ref, p_ref, eg_ref, o_ref, z_ref, dy_ref, ng_ref,
             dz_ref, do_ref, dvn_ref, dsn_ref, gng_ref, ds_ref):
        n = pl.program_id(0)

        @pl.when(n == 0)
        def _():
            ds_ref[...] = jnp.zeros_like(ds_ref)
            gng_ref[...] = jnp.zeros_like(gng_ref)

        def head(j, h):
            rows, sl = slice(CHUNK * j, CHUNK * j + CHUNK), slice(128 * h, 128 * h + 128)
            oh, r = _rms(o_ref[rows, sl], None)
            z, dy = z_ref[rows, sl], dy_ref[rows, sl]
            dz_ref[rows, sl] = (dy * (oh * ng_ref[...]) * _dsilu(z)).astype(BF16)
            do, gg = _rms_bwd(dy * _silu(z), oh, r, ng_ref[...])
            dob = do.astype(BF16)
            ds = ds_ref[h]
            dvn = _dot(p_ref[rows, 64 * h:64 * h + 64], dob, TN) + _dot(ke_ref[rows, sl], ds.astype(BF16))
            qd = _dot(qe_ref[rows, sl], dob, TN)
            yield
            dvb = dvn.astype(BF16)
            ds_ref[h] = qd + eg_ref[j, h:h + 1, :] * ds - _dot(w_ref[rows, sl], dvb, TN)
            do_ref[rows, sl] = dob
            dvn_ref[rows, sl] = dvb
            return jnp.sum(gg, axis=0, keepdims=True)

        for j in reversed(range(G)):
            dsn_ref[j] = ds_ref[...]
            gng = _lockstep([head(j, h) for h in range(DN_HEADS)])
            gng_ref[...] += (gng[0] + gng[1]) + (gng[2] + gng[3])

    blk = pl.BlockSpec((CHUNK * G, 512), lambda n: (rev(n), 0))
    state = pl.BlockSpec((G, DN_HEADS, 128, 128), lambda n: (rev(n), 0, 0, 0))
    return pl.pallas_call(
        body, name=name, grid=(nc // G,),
        in_specs=[pl.BlockSpec(memory_space=pl.ANY), blk, blk, blk,
                  pl.BlockSpec((CHUNK * G, 256), lambda n: (rev(n), 0)),
                  pl.BlockSpec((G, 8, 128), lambda n: (rev(n), 0, 0)), blk,
                  pl.BlockSpec((CHUNK * G, 512), lambda n: (rev(n), C_ZC // 512)), blk,
                  pl.BlockSpec((1, 128), lambda n: (0, 0))],
        out_specs=[pl.BlockSpec((CHUNK * G, 512), lambda n: (rev(n), C_ZC // 512)), blk, blk, state,
                   pl.BlockSpec((1, 128), lambda n: (0, 0))],
        out_shape=[jax.ShapeDtypeStruct(dproj.shape, BF16), jax.ShapeDtypeStruct((T, 512), BF16),
                   jax.ShapeDtypeStruct((T, 512), BF16), jax.ShapeDtypeStruct((nc, DN_HEADS, 128, 128), F32),
                   jax.ShapeDtypeStruct((1, 128), F32)],
        scratch_shapes=[pltpu.VMEM((DN_HEADS, 128, 128), F32)],
        input_output_aliases={0: 0},
        compiler_params=_cparams(("arbitrary",)),
    )(dproj, w, qe, ke, pm, eg, o, proj, dyc, dn_g)


def _gdn_chunk_grad(qd, kd, vd, gb, gbt, tmi, ssave, dsn, do, dvn, vn, name):
    T = qd.shape[0]
    G = min(GROUP, T // CHUNK)
    ng = T // (CHUNK * G)

    def body(q_ref, k_ref, v_ref, gb_ref, gt_ref, t_ref, ss_ref, dsn_ref, do_ref, dvn_ref, vn_ref,
             dq_ref, dk_ref, dv_ref, dgb_ref):
        lower, strict, _ = _tri_consts()
        lmat = lower.astype(F32)
        ones = jnp.ones((CHUNK, 128), F32)
        lane = lax.broadcasted_iota(jnp.int32, (CHUNK, 128), 1)
        decay = [_chunk_decay(gb_ref, gt_ref, lmat, g) for g in range(G)]
        chains = [(g, h) for g in range(G) for h in range(DN_HEADS)]
        gens = []
        for g, h in chains:
            rows, gcs, grs = decay[g]
            sl = slice(128 * h, 128 * h + 128)
            c = _gdn_local(q_ref[rows, sl], k_ref[rows, sl], v_ref[rows, sl], gcs[:, h:h + 1], grs[h:h + 1, :],
                           gb_ref[rows, DN_HEADS + h:DN_HEADS + h + 1], lower, strict)
            c.update(tm=t_ref[rows, 64 * h:64 * h + 64], s=ss_ref[g, h], vn=vn_ref[rows, sl])
            gens.append(_gdn_chunk_bwd(c, do_ref[rows, sl], dvn_ref[rows, sl], dsn_ref[g, h], lower, strict, ones))
        dgb = [jnp.zeros((CHUNK, 128), F32) for _ in range(G)]
        for (g, h), (dq, dk, dv, dg, dbeta) in zip(chains, _lockstep(gens)):
            rows, sl = decay[g][0], slice(128 * h, 128 * h + 128)
            dq_ref[rows, sl], dk_ref[rows, sl], dv_ref[rows, sl] = dq, dk, dv
            dgb[g] = dgb[g] + jnp.where(lane == h, dg, 0.0) + jnp.where(lane == DN_HEADS + h, dbeta, 0.0)
        for g in range(G):
            dgb_ref[decay[g][0], :] = dgb[g]

    blk = pl.BlockSpec((CHUNK * G, 512), lambda n: (n, 0))
    half = pl.BlockSpec((CHUNK * G, 256), lambda n: (n, 0))
    nar = pl.BlockSpec((CHUNK * G, 128), lambda n: (n, 0))
    state = pl.BlockSpec((G, DN_HEADS, 128, 128), lambda n: (n, 0, 0, 0))
    return pl.pallas_call(
        body, name=name, grid=(ng,),
        in_specs=[blk, blk, blk, nar, pl.BlockSpec((G, 8, CHUNK), lambda n: (n, 0, 0)), half, state, state,
                  blk, blk, blk],
        out_specs=[blk, blk, blk, nar],
        out_shape=[jax.ShapeDtypeStruct((T, 512), F32)] * 3 + [jax.ShapeDtypeStruct((T, 128), F32)],
        compiler_params=_cparams(("parallel",)),
    )(qd, kd, vd, gb, gbt, tmi, ssave, dsn, do, dvn, vn)


def _gdn_prep_bwd1(dproj, proj, dqd, dkd, dvd, dgb, dkv_a, sconv_w, alog_v, dtb_v, name):
    T = proj.shape[0]
    tm = min(512, T)
    cur, prev, ab = _gdn_specs(T, tm)

    def body(dp_any, x_ref, xp_ref, ab_ref, dq_ref, dk_ref, dv_ref, dgb_ref, dkv_ref, w_ref, al_ref, dt_ref,
             o_ref, dpre_ref, st_ref, ext_ref):
        i = pl.program_id(0)
        pre = _gdn_conv(i, tm, x_ref, xp_ref, w_ref, ext_ref)
        y, dsl = _silu(pre), _dsilu(pre)
        for h in range(DN_HEADS):
            for base, g_ref, scale in ((0, dq_ref, 128 ** -0.5), (512, dk_ref, 1.0)):
                sl = slice(base + 128 * h, base + 128 * h + 128)
                xh = y[:, sl]
                r = lax.rsqrt(jnp.sum(xh * xh, axis=-1, keepdims=True) + EPS)
                xn = xh * r
                gy = g_ref[:, 128 * h:128 * h + 128]
                dpre_ref[:, sl] = (scale * r) * (gy - xn * jnp.sum(gy * xn, axis=-1, keepdims=True)) * dsl[:, sl]
        dpre_ref[:, 1024:] = dv_ref[...] * dsl[:, 1024:]
        abv, dgb = ab_ref[...], dgb_ref[...]
        lane = lax.broadcasted_iota(jnp.int32, (tm, 128), 1)
        na = -jnp.exp(al_ref[...])
        xs = abv + dt_ref[...]
        da = dgb * na * _sig(xs)
        b = _sig(abv)
        o_ref[:, :256] = dkv_ref[...].astype(BF16)
        o_ref[:, 256:] = jnp.where(lane < DN_HEADS, da,
                                   jnp.where(lane < 2 * DN_HEADS, dgb * b * (1.0 - b), 0.0)).astype(BF16)
        head = lane < DN_HEADS
        upd = jnp.concatenate([jnp.sum(jnp.where(head, dgb * na * _softplus(xs), 0.0), axis=0, keepdims=True),
                               jnp.sum(jnp.where(head, da, 0.0), axis=0, keepdims=True), jnp.zeros((6, 128), F32)],
                              axis=0)

        @pl.when(i == 0)
        def _():
            st_ref[...] = upd

        @pl.when(i > 0)
        def _():
            st_ref[...] += upd

    full = lambda s: pl.BlockSpec(s, lambda i: (0, 0))
    blk = pl.BlockSpec((tm, 512), lambda i: (i, 0))
    return pl.pallas_call(
        body, name=name, grid=(T // tm,),
        in_specs=[pl.BlockSpec(memory_space=pl.ANY), cur, prev, ab, blk, blk, blk,
                  pl.BlockSpec((tm, 128), lambda i: (i, 0)), pl.BlockSpec((tm, 256), lambda i: (i, 0)),
                  full((DN_K, QKV_C)), full((1, 128)), full((1, 128))],
        out_specs=[pl.BlockSpec((tm, 384), lambda i: (i, C_KA // 384)),
                   pl.BlockSpec((tm, QKV_C), lambda i: (i, 0)), full((8, 128))],
        out_shape=[jax.ShapeDtypeStruct(dproj.shape, BF16), jax.ShapeDtypeStruct((T, QKV_C), F32),
                   jax.ShapeDtypeStruct((8, 128), F32)],
        scratch_shapes=[pltpu.VMEM((tm + HALO_C, QKV_C), F32)],
        input_output_aliases={0: 0},
        compiler_params=_cparams(("arbitrary",)),
    )(dproj, proj, proj, proj, dqd, dkd, dvd, dgb, dkv_a, sconv_w, alog_v, dtb_v)


def _gdn_prep_bwd2(dproj, proj, dpre, sconv_w, name):
    T = proj.shape[0]
    tm = min(512, T)
    nt = T // tm
    r = tm // HALO_C
    cur, prev, _ = _gdn_specs(T, tm)

    def body(dp_any, x_ref, xp_ref, d_ref, dn_ref, w_ref, dx_ref, gw_ref, extx_ref, extd_ref):
        i = pl.program_id(0)
        extx_ref[:HALO_C] = jnp.where(i > 0, xp_ref[...], 0.0)
        extx_ref[HALO_C:] = x_ref[...]
        d = d_ref[...]
        extd_ref[:tm] = d
        extd_ref[tm:] = jnp.where(i < nt - 1, dn_ref[...], 0.0)
        dx = jnp.zeros((tm, QKV_C), F32)
        rows = []
        for k in range(DN_K):
            dx = dx + w_ref[k:k + 1, :] * extd_ref[pl.ds(DN_K - 1 - k, tm), :]
            rows.append(jnp.sum(d * extx_ref[pl.ds(HALO_C - DN_K + 1 + k, tm), :], axis=0, keepdims=True))
        rows.append(jnp.zeros((8 - DN_K, QKV_C), F32))
        gw = jnp.concatenate(rows, axis=0)
        dx_ref[...] = dx.astype(BF16)

        @pl.when(i == 0)
        def _():
            gw_ref[...] = gw

        @pl.when(i > 0)
        def _():
            gw_ref[...] += gw

    full = lambda s: pl.BlockSpec(s, lambda i: (0, 0))
    return pl.pallas_call(
        body, name=name, grid=(nt,),
        in_specs=[pl.BlockSpec(memory_space=pl.ANY), cur, prev, pl.BlockSpec((tm, QKV_C), lambda i: (i, 0)),
                  pl.BlockSpec((HALO_C, QKV_C), lambda i: (jnp.minimum((i + 1) * r, T // HALO_C - 1), 0)),
                  full((DN_K, QKV_C))],
        out_specs=[cur, full((8, QKV_C))],
        out_shape=[jax.ShapeDtypeStruct(dproj.shape, BF16), jax.ShapeDtypeStruct((8, QKV_C), F32)],
        scratch_shapes=[pltpu.VMEM((tm + HALO_C, QKV_C), F32), pltpu.VMEM((tm + HALO_C, QKV_C), F32)],
        input_output_aliases={0: 0},
        compiler_params=_cparams(("arbitrary",)),
    )(dproj, proj, proj, dpre, dpre, sconv_w)


def _merge_fwd(x, proj, ya, yb, yc, wa, wb, wc, wo, gate, name):
    T = x.shape[0]
    tm = min(256, T)

    def body(x_ref, mg_ref, ya_ref, yb_ref, yc_ref, wa_ref, wb_ref, wc_ref, wo_ref, gate_ref, o_ref):
        merged = (_sig(mg_ref[:, :D]) * _bdot(ya_ref[...], wa_ref[...])
                  + _sig(mg_ref[:, D:2 * D]) * _bdot(yb_ref[...], wb_ref[...])
                  + _sig(mg_ref[:, 2 * D:]) * _bdot(yc_ref[...], wc_ref[...]))
        o_ref[...] = x_ref[...] + gate_ref[...] * _bdot(merged, wo_ref[...])

    full = lambda s: pl.BlockSpec(s, lambda i: (0, 0))
    yb_ = pl.BlockSpec((tm, 512), lambda i: (i, 0))
    return pl.pallas_call(
        body, name=name, grid=(T // tm,),
        in_specs=[pl.BlockSpec((tm, D), lambda i: (i, 0)), pl.BlockSpec((tm, 3 * D), lambda i: (i, 0)), yb_, yb_, yb_,
                  full((512, D)), full((512, D)), full((512, D)), full((D, D)), full((1, D))],
        out_specs=pl.BlockSpec((tm, D), lambda i: (i, 0)),
        out_shape=jax.ShapeDtypeStruct((T, D), F32),
        compiler_params=_cparams(("parallel",)),
    )(x, proj, ya, yb, yc, wa, wb, wc, wo, _row(gate))


def _merge_bwd(dout, proj, ya, yb, yc, wa, wb, wc, wo, gate, name):
    T = dout.shape[0]
    tm = min(256, T)
    nt = T // tm

    def body(do_ref, mg_ref, ya_ref, yb_ref, yc_ref, wa_ref, wb_ref, wc_ref, wo_ref, gate_ref,
             dmg_ref, dya_ref, dyb_ref, dyc_ref, gwa_hbm, gwb_hbm, gwc_hbm, gwo_hbm, gg_ref,
             gwa_ref, gwb_ref, gwc_ref, gwo_ref):
        i = pl.program_id(0)

        @pl.when(i == 0)
        def _():
            for r in (gwa_ref, gwb_ref, gwc_ref, gwo_ref, gg_ref):
                r[...] = jnp.zeros_like(r)

        ys = (ya_ref[...], yb_ref[...], yc_ref[...])
        ws = (wa_ref, wb_ref, wc_ref)
        gs = tuple(_sig(mg_ref[:, j * D:(j + 1) * D]) for j in range(3))
        ps = tuple(_bdot(ys[j], ws[j][...]) for j in range(3))
        merged = gs[0] * ps[0] + gs[1] * ps[1] + gs[2] * ps[2]
        do = do_ref[...]
        dmerged = _bdot(do * gate_ref[...], wo_ref[...], NT)
        gwo_ref[...] += _bdot(merged, do, TN)
        for j, (dy_ref, gw_ref) in enumerate(((dya_ref, gwa_ref), (dyb_ref, gwb_ref), (dyc_ref, gwc_ref))):
            dp = dmerged * gs[j]
            dmg_ref[:, j * D:(j + 1) * D] = (dmerged * ps[j] * gs[j] * (1.0 - gs[j])).astype(BF16)
            dy_ref[...] = _bdot(dp, ws[j][...], NT)
            gw_ref[...] += _bdot(ys[j], dp, TN)

        @pl.when(i == nt - 1)
        def _():
            m = gwo_ref[...]
            gg_ref[...] = jnp.sum(wo_ref[...].astype(F32) * m, axis=0, keepdims=True)
            gwo_ref[...] = m * gate_ref[...]
            for src, dst in ((gwa_ref, gwa_hbm), (gwb_ref, gwb_hbm), (gwc_ref, gwc_hbm), (gwo_ref, gwo_hbm)):
                pltpu.sync_copy(src, dst)

    full = lambda s: pl.BlockSpec(s, lambda i: (0, 0))
    yb_ = pl.BlockSpec((tm, 512), lambda i: (i, 0))
    anyspec = pl.BlockSpec(memory_space=pl.ANY)
    return pl.pallas_call(
        body, name=name, grid=(nt,),
        in_specs=[pl.BlockSpec((tm, D), lambda i: (i, 0)), pl.BlockSpec((tm, 3 * D), lambda i: (i, 0)), yb_, yb_, yb_,
                  full((512, D)), full((512, D)), full((512, D)), full((D, D)), full((1, D))],
        out_specs=[pl.BlockSpec((tm, 3 * D), lambda i: (i, 0)), yb_, yb_, yb_, anyspec, anyspec, anyspec, anyspec,
                   full((1, D))],
        out_shape=[jax.ShapeDtypeStruct((T, NP), BF16)] + [jax.ShapeDtypeStruct((T, 512), F32)] * 3
        + [jax.ShapeDtypeStruct((512, D), F32)] * 3 + [jax.ShapeDtypeStruct((D, D), F32), jax.ShapeDtypeStruct((1, D), F32)],
        scratch_shapes=[pltpu.VMEM((512, D), F32)] * 3 + [pltpu.VMEM((D, D), F32)],
        compiler_params=_cparams(("arbitrary",)),
    )(dout, proj, ya, yb, yc, wa, wb, wc, wo, _row(gate))


def _loss_head(y, tgt, name):
    T = y.shape[0]
    tm = min(512, T)

    def body(y_ref, t_ref, dy_ref, l_ref):
        i = pl.program_id(0)
        diff = y_ref[...] - t_ref[...]
        dy_ref[...] = diff * (1.0 / D)
        part = jnp.sum(diff * diff, axis=0, keepdims=True)

        @pl.when(i == 0)
        def _():
            l_ref[...] = part

        @pl.when(i > 0)
        def _():
            l_ref[...] += part

    blk = pl.BlockSpec((tm, D), lambda i: (i, 0))
    return pl.pallas_call(
        body, name=name, grid=(T // tm,), in_specs=[blk, blk],
        out_specs=[blk, pl.BlockSpec((1, D), lambda i: (0, 0))],
        out_shape=[jax.ShapeDtypeStruct((T, D), F32), jax.ShapeDtypeStruct((1, D), F32)],
        compiler_params=_cparams(("arbitrary",)),
    )(y, tgt)


def _ada_fwd(c_all, w_ada, b_my, name):
    def body(c_ref, w_ref, b_ref, o_ref):
        sc = _silu(c_ref[...])
        for l in range(DEPTH):
            o_ref[l] = _bdot(sc, w_ref[l]) + b_ref[l:l + 1, :]

    return pl.pallas_call(body, name=name, out_shape=jax.ShapeDtypeStruct((DEPTH, N_DEV, w_ada.shape[2]), F32),
                          compiler_params=_cparams())(c_all, w_ada, b_my)


def _ada_bwd(c_all, dmod_my, name):
    def body(c_ref, d_ref, o_ref):
        sc = _silu(c_ref[...])
        for l in range(DEPTH):
            o_ref[l] = _bdot(sc, d_ref[l], TN)

    return pl.pallas_call(body, name=name, out_shape=jax.ShapeDtypeStruct((DEPTH, D, dmod_my.shape[2]), F32),
                          compiler_params=_cparams())(c_all, dmod_my)


def _adam_math(w, g, m, v):
    m = ADAM_B1 * m + (1.0 - ADAM_B1) * g
    v = ADAM_B2 * v + (1.0 - ADAM_B2) * (g * g)
    m_hat = m / (1.0 - ADAM_B1 ** ADAM_STEP)
    v_hat = v / (1.0 - ADAM_B2 ** ADAM_STEP)
    return -ADAM_LR * (m_hat / (jnp.sqrt(v_hat) + ADAM_EPS) + ADAM_WD * w), m, v


def _row_tile(rows, cap):
    best = rows
    for t in range(8, min(rows, cap) + 1, 8):
        if rows % t == 0:
            best = t
    return best if best <= cap else rows


def _adamw(w, g, m, v, name):
    R, C = w.shape
    tr = _row_tile(R, 256)

    def body(w_ref, g_ref, m_ref, v_ref, d_ref, mo_ref, vo_ref):
        d_ref[...], mo_ref[...], vo_ref[...] = _adam_math(w_ref[...], g_ref[...], m_ref[...], v_ref[...])

    blk = pl.BlockSpec((tr, C), lambda i: (i, 0))
    return pl.pallas_call(body, name=name, grid=(R // tr,), in_specs=[blk] * 4, out_specs=[blk] * 3,
                          out_shape=[jax.ShapeDtypeStruct((R, C), F32)] * 3,
                          compiler_params=_cparams(("parallel",)))(w, g, m, v)


def _sum_adamw_many(parts, ws, ms, vs, name):
    n = len(ws)

    def body(*refs):
        ins, outs = refs[:4 * n], refs[4 * n:]
        for i in range(n):
            g = ins[i][0]
            for j in range(1, N_DEV):
                g = g + ins[i][j]
            d, m, v = _adam_math(ins[n + i][...], g, ins[2 * n + i][...], ins[3 * n + i][...])
            outs[i][...], outs[n + i][...], outs[2 * n + i][...], outs[3 * n + i][...] = g, d, m, v

    shapes = [jax.ShapeDtypeStruct(w.shape, F32) for w in ws]
    out = pl.pallas_call(body, name=name, out_shape=shapes * 4, compiler_params=_cparams())(*parts, *ws, *ms, *vs)
    return out[:n], out[n:2 * n], out[2 * n:3 * n], out[3 * n:]


def _sum_adamw(parts0, parts1, w, m, v, name):
    P, R, C = parts0.shape
    tr = _row_tile(R, 128)
    nt = R // tr

    def body(p0_ref, p1_ref, w_ref, m_ref, v_ref, g_ref, d_ref, mo_ref, vo_ref):
        def emit(p_ref):
            g = p_ref[0].astype(F32)
            for j in range(1, P):
                g = g + p_ref[j].astype(F32)
            g_ref[...] = g
            d_ref[...], mo_ref[...], vo_ref[...] = _adam_math(w_ref[...], g, m_ref[...], v_ref[...])

        @pl.when(pl.program_id(0) == 0)
        def _():
            emit(p0_ref)

        @pl.when(pl.program_id(0) == 1)
        def _():
            emit(p1_ref)

    blk = pl.BlockSpec((tr, C), lambda l, i: (l * nt + i, 0))
    return pl.pallas_call(
        body, name=name, grid=(DEPTH, nt),
        in_specs=[pl.BlockSpec((P, tr, C), lambda l, i: (0, i * (1 - l) + (nt - 1) * l, 0)),
                  pl.BlockSpec((P, tr, C), lambda l, i: (0, i * l, 0)), blk, blk, blk],
        out_specs=[blk] * 4, out_shape=[jax.ShapeDtypeStruct((DEPTH * R, C), F32)] * 4,
        compiler_params=_cparams(("arbitrary", "arbitrary")))(parts0, parts1, w, m, v)


def _pair_sum_many(core, bufs, recvs, name):
    n = len(bufs)

    def body(c_ref, *refs):
        for a_ref, b_ref, o_ref in zip(refs[:n], refs[n:2 * n], refs[2 * n:]):
            o_ref[...] = (a_ref[:, 0].astype(F32) + b_ref[...].astype(F32)).astype(BF16)

    mine = [pl.BlockSpec((4, 1) + b.shape[2:], lambda i, c: (0, c[0], 0, 0)) for b in bufs]
    whole = [pl.BlockSpec(r.shape, lambda i, c: (0, 0, 0)) for r in recvs]
    return pl.pallas_call(
        body, name=name,
        grid_spec=pltpu.PrefetchScalarGridSpec(num_scalar_prefetch=1, grid=(1,), in_specs=mine + whole,
                                               out_specs=whole),
        out_shape=[jax.ShapeDtypeStruct(r.shape, BF16) for r in recvs],
        compiler_params=_cparams(("arbitrary",)))(core, *bufs, *recvs)


def _pair_sum(core, buf, recv, name):
    _, _, R, C = buf.shape
    tr = _row_tile(R, 128)

    def body(c_ref, a_ref, b_ref, o_ref):
        o_ref[...] = (a_ref[:, 0].astype(F32) + b_ref[...].astype(F32)).astype(BF16)

    return pl.pallas_call(
        body, name=name,
        grid_spec=pltpu.PrefetchScalarGridSpec(
            num_scalar_prefetch=1, grid=(R // tr,),
            in_specs=[pl.BlockSpec((4, 1, tr, C), lambda i, c: (0, c[0], i, 0)),
                      pl.BlockSpec((4, tr, C), lambda i, c: (0, i, 0))],
            out_specs=pl.BlockSpec((4, tr, C), lambda i, c: (0, i, 0))),
        out_shape=jax.ShapeDtypeStruct((4, R, C), BF16),
        compiler_params=_cparams(("parallel",)))(core, buf, recv)


SHARD_IN = D_IN // N_DEV


def _w_in_pieces():
    out, p = [], 0
    for a, b in _PAD_FROM:
        for j in range(N_DEV):
            lo, hi = max(a, SHARD_IN * j), min(b, SHARD_IN * (j + 1))
            if lo < hi:
                out.append((j, lo - SHARD_IN * j, hi - SHARD_IN * j, p + lo - a))
        p += b - a
    return out


def _assemble_w_in(gw, name):
    tr = 256
    nt = D // tr

    def body(x_ref, o_ref):
        for j, s0, s1, d0 in _w_in_pieces():
            o_ref[:, d0:d0 + s1 - s0] = x_ref[j, :, s0:s1]
        o_ref[:, D_IN:] = jnp.zeros((tr, NP - D_IN), gw.dtype)

    return pl.pallas_call(
        body, name=name, grid=(nt,),
        in_specs=[pl.BlockSpec((N_DEV, tr, SHARD_IN), lambda i: (0, i, 0))],
        out_specs=pl.BlockSpec((tr, NP), lambda i: (i, 0)),
        out_shape=jax.ShapeDtypeStruct((D, NP), gw.dtype),
        compiler_params=_cparams(("parallel",)))(gw)


def _split_w_in_grad(g, name):
    tr = 256

    def body(g_ref, o_ref):
        for j, s0, s1, d0 in _w_in_pieces():
            o_ref[j, :, s0:s1] = g_ref[:, d0:d0 + s1 - s0].astype(BF16)

    return pl.pallas_call(
        body, name=name, grid=(D // tr,),
        in_specs=[pl.BlockSpec((tr, NP), lambda i: (i, 0))],
        out_specs=pl.BlockSpec((N_DEV, tr, SHARD_IN), lambda i: (0, i, 0)),
        out_shape=jax.ShapeDtypeStruct((N_DEV, D, SHARD_IN), BF16),
        compiler_params=_cparams(("parallel",)))(g)


def _mesh_pos():
    return lax.axis_index("x"), lax.axis_index("y"), lax.axis_index("c")


def _launch(copies, peers, bufs, out_structs, sems, name, collective_id):
    n = len(bufs)
    if collective_id is None:
        anyspec = pl.BlockSpec(memory_space=pl.ANY)
        return list(pl.pallas_call(
            lambda *refs: copies(refs[:n], refs[n:n + len(out_structs)], *refs[n + len(out_structs):]),
            name=name, in_specs=[anyspec] * n, out_specs=[anyspec] * len(out_structs), out_shape=list(out_structs),
            scratch_shapes=list(sems))(*bufs))
    ins = [jax.new_ref(b, memory_space=pltpu.MemorySpace.HBM) for b in bufs]
    outs = [jax.empty_ref(s, memory_space=pltpu.MemorySpace.HBM) for s in out_structs]

    @pl.kernel(mesh=plsc.ScalarSubcoreMesh(axis_name="sequencer", num_cores=1), name=name, scratch_types=tuple(sems),
               compiler_params=pltpu.CompilerParams(collective_id=collective_id))
    def on_sequencer(*sem_refs):
        barrier = pltpu.get_barrier_semaphore()
        targets = peers()
        for p in targets:
            pl.semaphore_signal(barrier, inc=1, device_id=p, device_id_type=pl.DeviceIdType.MESH)
        pl.semaphore_wait(barrier, len(targets))
        copies(ins, outs, *sem_refs)

    on_sequencer()
    return [r[...] for r in outs]


def _all_gather(blocks, name, collective_id=None):
    n = len(blocks)

    def peers():
        x, y, c = _mesh_pos()
        return [(x, y, 1 - c), (1 - x, y, c), (x, 1 - y, c), (1 - x, 1 - y, c)]

    def copies(ins, outs, send_sems, recv_sems, local_sems):
        x, y, c = _mesh_pos()
        me, sibling = (x, y, c), (x, y, 1 - c)
        chips = [(1 - x, y), (x, 1 - y), (1 - x, 1 - y)]
        idx = lambda p: 4 * p[0] + 2 * p[1] + p[2]

        def copy(a, k, block, to, src=None):
            dst = outs[a].at[idx(block)]
            return pltpu.make_async_remote_copy(
                src_ref=dst if src is None else src, dst_ref=dst, send_sem=send_sems.at[a, k],
                recv_sem=recv_sems.at[a, k], device_id=to, device_id_type=pl.DeviceIdType.MESH)

        mine = [pltpu.make_async_copy(ins[a], outs[a].at[idx(me)], local_sems.at[a]) for a in range(n)]
        for cp in mine:
            cp.start()
        first = []
        for a in range(n):
            first.append(copy(a, 0, me, sibling, src=ins[a]))
            first += [copy(a, 1 + j, me, (*chip, c), src=ins[a]) for j, chip in enumerate(chips)]
        for cp in first:
            cp.start()
        passed = []
        for j, chip in enumerate(chips):
            for a in range(n):
                copy(a, 1 + j, (*chip, c), me).wait_recv()
                cp = copy(a, 4 + j, (*chip, c), sibling)
                cp.start()
                passed.append(cp)
        for a in range(n):
            copy(a, 0, sibling, me).wait_recv()
            for j, chip in enumerate(chips):
                copy(a, 4 + j, (*chip, 1 - c), me).wait_recv()
        for cp in first + passed:
            cp.wait_send()
        for cp in mine:
            cp.wait()

    return _launch(copies, peers, blocks, [jax.ShapeDtypeStruct((N_DEV,) + b.shape, b.dtype) for b in blocks],
                   [pltpu.SemaphoreType.DMA((n, 7)), pltpu.SemaphoreType.DMA((n, 7)), pltpu.SemaphoreType.DMA((n,))],
                   name, collective_id)


def _exchange_core(bufs, name, collective_id=None):
    n = len(bufs)

    def peers():
        x, y, c = _mesh_pos()
        return [(x, y, 1 - c)]

    def copies(ins, outs, send_sems, recv_sems):
        x, y, c = _mesh_pos()
        started = []
        for a in range(n):
            for q in range(4):
                cp = pltpu.make_async_remote_copy(
                    src_ref=ins[a].at[q, 1 - c], dst_ref=outs[a].at[q], send_sem=send_sems.at[a, q],
                    recv_sem=recv_sems.at[a, q], device_id=(x, y, 1 - c), device_id_type=pl.DeviceIdType.MESH)
                cp.start()
                started.append(cp)
        for cp in started:
            cp.wait()

    return _launch(copies, peers, bufs, [jax.ShapeDtypeStruct((4,) + b.shape[2:], b.dtype) for b in bufs],
                   [pltpu.SemaphoreType.DMA((n, 4)), pltpu.SemaphoreType.DMA((n, 4))], name, collective_id)


def _exchange_chips(bufs, name, collective_id=None):
    n = len(bufs)

    def peers():
        x, y, c = _mesh_pos()
        return [(1 - x, y, c), (x, 1 - y, c), (1 - x, 1 - y, c)]

    def copies(ins, outs, send_sems, recv_sems, local_sems):
        x, y, c = _mesh_pos()
        chip = 2 * x + y
        local = [pltpu.make_async_copy(ins[a].at[chip], outs[a].at[chip], local_sems.at[a]) for a in range(n)]
        for cp in local:
            cp.start()
        started = []
        for k in range(1, 4):
            px = 1 - x if k & 2 else x
            py = 1 - y if k & 1 else y
            for a in range(n):
                cp = pltpu.make_async_remote_copy(
                    src_ref=ins[a].at[2 * px + py], dst_ref=outs[a].at[chip], send_sem=send_sems.at[a, k - 1],
                    recv_sem=recv_sems.at[a, k - 1], device_id=(px, py, c), device_id_type=pl.DeviceIdType.MESH)
                cp.start()
                started.append(cp)
        for cp in started:
            cp.wait()
        for cp in local:
            cp.wait()

    return _launch(copies, peers, bufs, [jax.ShapeDtypeStruct(b.shape, b.dtype) for b in bufs],
                   [pltpu.SemaphoreType.DMA((n, 3)), pltpu.SemaphoreType.DMA((n, 3)), pltpu.SemaphoreType.DMA((n,))],
                   name, collective_id)


_SMALL = ("b_ada", "norm_g", "q_norm_g", "k_norm_g", "sinks", "dw_b", "ln_g", "ln_b", "pw2_b", "a_log", "dt_bias",
          "dn_norm_g", "dw_w", "sconv_w")


def _lane4(v):
    return jnp.pad(v, (0, 124)).reshape(1, 128)


def kernel(x, c, w_ada, b_ada, norm_g, w_in, q_norm_g, k_norm_g, sinks, dw_w, dw_b, ln_g, ln_b, pw2_w, pw2_b, sconv_w, a_log, dt_bias, dn_norm_g, w_proj_a, w_proj_b, w_proj_c, w_out, loss_target, m_w_ada, m_b_ada, m_norm_g, m_w_in, m_q_norm_g, m_k_norm_g, m_sinks, m_dw_w, m_dw_b, m_ln_g, m_ln_b, m_pw2_w, m_pw2_b, m_sconv_w, m_a_log, m_dt_bias, m_dn_norm_g, m_w_proj_a, m_w_proj_b, m_w_proj_c, m_w_out, v_w_ada, v_b_ada, v_norm_g, v_w_in, v_q_norm_g, v_k_norm_g, v_sinks, v_dw_w, v_dw_b, v_ln_g, v_ln_b, v_pw2_w, v_pw2_b, v_sconv_w, v_a_log, v_dt_bias, v_dn_norm_g, v_w_proj_a, v_w_proj_b, v_w_proj_c, v_w_out):
    T = x.shape[1]
    nc = T // CHUNK
    xi, yi, ci = _mesh_pos()
    me = 4 * xi + 2 * yi + ci
    big_w = (w_in, pw2_w, w_proj_a, w_proj_b, w_proj_c, w_out)
    big_m = (m_w_in, m_pw2_w, m_w_proj_a, m_w_proj_b, m_w_proj_c, m_w_out)
    big_v = (v_w_in, v_pw2_w, v_w_proj_a, v_w_proj_b, v_w_proj_c, v_w_out)

    ada_cols = w_ada.shape[2]
    dw_cols, sc_cols = dw_w.shape[2], sconv_w.shape[2]
    flat2 = lambda a: a.reshape(-1, a.shape[-1])
    big16 = [[a[l].astype(BF16) for l in range(DEPTH)] for a in big_w]
    c_all, gdw, gsc = _all_gather([c, dw_w, sconv_w], "gather_small", collective_id=0)
    (gw_in0,) = _all_gather([big16[0][0]], "gather_w_in0", collective_id=7)
    gw_in0, _ = lax.optimization_barrier((gw_in0, (flat2(m_w_in), flat2(v_w_in))))
    c_all = c_all.reshape(N_DEV, D)
    dw_f = gdw.transpose(1, 2, 0, 3).reshape(DEPTH, CONV_K, 512)
    sc_f = gsc.transpose(1, 2, 0, 3).reshape(DEPTH, DN_K, QKV_C)

    b_my = lax.dynamic_slice(b_ada, (0, me * ada_cols), (DEPTH, ada_cols))
    mod_part = _ada_fwd(c_all, w_ada, b_my, "ada_fwd")
    (gmod,) = _all_gather([mod_part.reshape(-1, 128)], "gather_mod")

    rest0 = [a[0] for a in big16[1:]]
    all1 = [a[1] for a in big16]
    (rest0, all1), gmod = lax.optimization_barrier(((rest0, all1), gmod))
    got0 = [gw_in0] + _all_gather(rest0, "gather_rest0", collective_id=1)
    got1 = _all_gather(all1, "gather_weights1", collective_id=6)
    wp, pw2_f, wa_f, wb_f, wc_f, wo_f = [], [], [], [], [], []
    for l, (gw_in, gpw2, gpa, gpb, gpc, gwo) in enumerate((got0, got1)):
        wp.append(_assemble_w_in(gw_in, f"assemble_w_in{l}"))
        pw2_f.append(gpw2.reshape(512, 512))
        for dst, g in ((wa_f, gpa), (wb_f, gpb), (wc_f, gpc)):
            dst.append(g.transpose(1, 0, 2).reshape(512, D))
        wo_f.append(gwo.reshape(D, D))
    mod_all = gmod.reshape(N_DEV, DEPTH, N_DEV, ada_cols).transpose(1, 2, 0, 3).reshape(DEPTH, N_DEV, 3 * D)
    mod = lax.dynamic_index_in_dim(mod_all, me, axis=1, keepdims=False)
    shift, scale, gate = mod[:, :D], mod[:, D:2 * D], mod[:, 2 * D:]

    xs, saved = [x[0]], []
    for l in range(DEPTH):
        xl = xs[-1]
        h = _norm_fwd(xl, norm_g[l], scale[l], shift[l], f"norm_fwd{l}")
        proj = _mm(h, wp[l], tm=min(2048, T), tn=1152, tk=D, name=f"in_proj{l}")
        ya = _attn_fwd(proj, q_norm_g[l], k_norm_g[l], sinks[l], f"attn_fwd{l}")
        yb = _conf_fwd(proj, dw_f[l], dw_b[l], ln_g[l], ln_b[l], pw2_f[l], pw2_b[l], f"conf_fwd{l}")
        alv, dtv, dng = _lane4(a_log[l]), _lane4(dt_bias[l]), _row(dn_norm_g[l])
        qd, kd, vd, gb = _gdn_prep_fwd(proj, sc_f[l], alv, dtv, f"gdn_prep_fwd{l}")
        gbt = gb[:, :8].reshape(nc, CHUNK, 8).transpose(0, 2, 1)
        u, w, qe, ke, pm, tmi, eg = _gdn_chunk_fwd(qd, kd, vd, gb, gbt, f"gdn_chunk_fwd{l}")
        yc, o, vn, ss = _gdn_scan_fwd(u, w, qe, ke, pm, eg, proj, dng, f"gdn_scan_fwd{l}")
        xs.append(_merge_fwd(xl, proj, ya, yb, yc, wa_f[l], wb_f[l], wc_f[l], wo_f[l], gate[l], f"merge_fwd{l}"))
        saved.append((h, proj, ya, yb, yc, qd, kd, vd, gb, gbt, ss, alv, dtv, dng, w, qe, ke, pm, tmi, eg, o, vn))

    dout, lsum = _loss_head(xs[-1], loss_target[0], "loss_head")

    small = {name: [None] * DEPTH for name in _SMALL}
    big_parts = [None] * DEPTH
    core = jnp.reshape(ci, (1,)).astype(jnp.int32)
    for l in reversed(range(DEPTH)):
        h, proj, ya, yb, yc, qd, kd, vd, gb, gbt, ss, alv, dtv, dng, w, qe, ke, pm, tmi, eg, o, vn = saved[l]
        dproj, dya, dyb, dyc, g_wa, g_wb, g_wc, g_wo, g_gate = _merge_bwd(
            dout, proj, ya, yb, yc, wa_f[l], wb_f[l], wc_f[l], wo_f[l], gate[l], f"merge_bwd{l}")
        dproj, dkv_a, g_q, g_k, g_s = _attn_bwd(dproj, proj, dya, q_norm_g[l], k_norm_g[l], sinks[l], f"attn_bwd{l}")
        dproj, du1, g_pw2, st_b = _conf_bwd1(dproj, proj, dyb, dw_f[l], dw_b[l], ln_g[l], ln_b[l], pw2_f[l], pw2_b[l],
                                             f"conf_bwd_a{l}")
        dproj, g_dw = _conf_bwd2(dproj, proj, du1, dw_f[l], f"conf_bwd_b{l}")
        dproj, do, dvn, dsn, g_dn = _gdn_scan_bwd(dproj, w, qe, ke, pm, eg, o, proj, dyc, dng, f"gdn_scan_bwd{l}")
        dqd, dkd, dvd, dgb = _gdn_chunk_grad(qd, kd, vd, gb, gbt, tmi, ss, dsn, do, dvn, vn, f"gdn_chunk_bwd{l}")
        dproj, dpre, st_c = _gdn_prep_bwd1(dproj, proj, dqd, dkd, dvd, dgb, dkv_a, sc_f[l], alv, dtv,
                                           f"gdn_prep_bwd_a{l}")
        dproj, g_sc = _gdn_prep_bwd2(dproj, proj, dpre, sc_f[l], f"gdn_prep_bwd_b{l}")
        g_wp = _mm(h, dproj, ta=True, tm=D, tn=1152, tk=min(2048, T), name=f"d_w_in{l}")
        by_dest = [_split_w_in_grad(g_wp, f"split_w_in_grad{l}"), g_pw2.reshape(N_DEV, -1, 512).astype(BF16)]
        by_dest += [g.reshape(512, N_DEV, -1).transpose(1, 0, 2).astype(BF16) for g in (g_wa, g_wb, g_wc)]
        by_dest.append(g_wo.reshape(N_DEV, -1, D).astype(BF16))
        by_dest = [b.reshape(4, 2, -1, b.shape[-1]) for b in by_dest]
        if l < DEPTH - 1:
            by_dest, big_parts[l + 1] = lax.optimization_barrier((by_dest, big_parts[l + 1]))
        from_sibling = _exchange_core(by_dest, f"exchange_grads_core{l}", collective_id=2 + 2 * l)

        def input_grad(dproj, dout):
            dh = _mm(dproj, wp[l], tb=True, tm=min(1024, T), tn=D, tk=2688, name=f"d_h{l}")
            return _norm_bwd(dh, xs[l], dout, norm_g[l], scale[l], f"norm_bwd{l}")

        if l > 0:
            dout, st_n = input_grad(dproj, dout)
            from_sibling, dout = lax.optimization_barrier((from_sibling, dout))
        chip_sums = [_pair_sum(core, by_dest[0], from_sibling[0], f"pair_sum_w_in{l}")]
        chip_sums += _pair_sum_many(core, by_dest[1:], from_sibling[1:], f"pair_sum_rest{l}")
        big_parts[l] = _exchange_chips(chip_sums, f"exchange_grads_chips{l}", collective_id=3 + 2 * l)
        if l > 0:
            dout, chip_sums = lax.optimization_barrier((dout, chip_sums))
        else:
            dproj, chip_sums = lax.optimization_barrier((dproj, chip_sums))
            dout, st_n = input_grad(dproj, dout)
        for name, g in (("b_ada", jnp.concatenate([st_n[0], st_n[1], g_gate[0]])), ("norm_g", st_n[2]),
                        ("q_norm_g", g_q.reshape(ATT_HEADS, ATT_HD).sum(0)), ("k_norm_g", g_k.reshape(2, ATT_HD).sum(0)),
                        ("sinks", g_s[0]), ("dw_b", st_b[3]),
                        ("ln_g", st_b[1]), ("ln_b", st_b[2]), ("pw2_b", st_b[0]), ("a_log", st_c[0, :4]),
                        ("dt_bias", st_c[1, :4]), ("dn_norm_g", g_dn[0]), ("dw_w", g_dw[:CONV_K]),
                        ("sconv_w", g_sc[:DN_K])):
            small[name][l] = g
    grad_x = dout[None]

    big_parts, dout = lax.optimization_barrier((big_parts, dout))
    sum_big = lambda i: _sum_adamw(big_parts[0][i], big_parts[1][i], flat2(big_w[i]), flat2(big_m[i]),
                                   flat2(big_v[i]), f"sum_adamw{i}")
    res = [sum_big(0)]

    names = list(_SMALL)
    gathered = _all_gather([jnp.stack(small[n]) for n in names] + [lsum], "gather_small_grads")
    gparts = dict(zip(names, gathered))
    loss = 0.5 * jnp.sum(jnp.sum(gathered[-1], axis=(1, 2))) / D
    dmod_my = lax.dynamic_slice(gparts["b_ada"], (0, 0, me * ada_cols), (N_DEV, DEPTH, ada_cols)).transpose(1, 0, 2)
    g_w_ada = _ada_bwd(c_all, dmod_my, "ada_bwd")
    gparts["dw_w"] = lax.dynamic_slice(gparts["dw_w"], (0, 0, 0, me * dw_cols), (N_DEV, DEPTH, CONV_K, dw_cols))
    gparts["sconv_w"] = lax.dynamic_slice(gparts["sconv_w"], (0, 0, 0, me * sc_cols), (N_DEV, DEPTH, DN_K, sc_cols))
    env = dict(b_ada=(b_ada, m_b_ada, v_b_ada), norm_g=(norm_g, m_norm_g, v_norm_g),
               q_norm_g=(q_norm_g, m_q_norm_g, v_q_norm_g), k_norm_g=(k_norm_g, m_k_norm_g, v_k_norm_g),
               sinks=(sinks, m_sinks, v_sinks), dw_b=(dw_b, m_dw_b, v_dw_b), ln_g=(ln_g, m_ln_g, v_ln_g),
               ln_b=(ln_b, m_ln_b, v_ln_b), pw2_b=(pw2_b, m_pw2_b, v_pw2_b), a_log=(a_log, m_a_log, v_a_log),
               dt_bias=(dt_bias, m_dt_bias, v_dt_bias), dn_norm_g=(dn_norm_g, m_dn_norm_g, v_dn_norm_g),
               dw_w=(dw_w, m_dw_w, v_dw_w), sconv_w=(sconv_w, m_sconv_w, v_sconv_w))
    upd = _sum_adamw_many([gparts[n] for n in names], [env[n][0] for n in names], [env[n][1] for n in names],
                          [env[n][2] for n in names], "sum_adamw_small")

    d_ada, nm_ada, nv_ada = (u.reshape(w_ada.shape) for u in
                             _adamw(flat2(w_ada), flat2(g_w_ada), flat2(m_w_ada), flat2(v_w_ada), "adamw_w_ada"))

    g_small, d_small, m_small, v_small = (dict(zip(names, u)) for u in upd)
    res += [sum_big(i) for i in range(1, len(big_w))]
    g_big, d_big, m_big, v_big =([r[k].reshape(w.shape) for r, w in zip(res, big_w)] for k in range(4))

    order = ("w_ada", "b_ada", "norm_g", "w_in", "q_norm_g", "k_norm_g", "sinks", "dw_w", "dw_b", "ln_g", "ln_b",
             "pw2_w", "pw2_b", "sconv_w", "a_log", "dt_bias", "dn_norm_g", "w_proj_a", "w_proj_b", "w_proj_c", "w_out")
    big_names = ("w_in", "pw2_w", "w_proj_a", "w_proj_b", "w_proj_c", "w_out")

    def pick(kind):
        src_small = (g_small, d_small, m_small, v_small)[kind]
        src_big = (g_big, d_big, m_big, v_big)[kind]
        src_ada = (g_w_ada, d_ada, nm_ada, nv_ada)[kind]
        return [src_ada if n == "w_ada" else src_big[big_names.index(n)] if n in big_names else src_small[n]
                for n in order]

    return (loss, grad_x, *pick(0), *pick(1), *pick(2), *pick(3))
```

```python
import functools
import math

import jax
import jax.numpy as jnp
import numpy as np
from jax import lax
from jax.experimental import pallas as pl
from jax.experimental.pallas import tpu as pltpu
from jax.experimental.pallas import tpu_sc as plsc

F32 = jnp.float32
BF16 = jnp.bfloat16
HI = lax.Precision.HIGHEST

N_DEV = 8
D = 1024
DEPTH = 2
EPS = 1e-6
NEG_INF = -1e30
WINDOW = 128
ATT_HEADS = 8
ATT_HD = 64
CONV_K = 31
DN_HEADS = 4
DN_K = 4
CHUNK = 64
D_IN = 7944
VMEM_LIMIT = 56 * 1024 * 1024

C_MG, C_QA, C_ZA, C_ZB, C_QC, C_KC, C_VC, C_GV, C_GG, C_ZC, C_KA, C_VA, C_AB, NP = (
    0, 3072, 3584, 4096, 4608, 5120, 5632, 6144, 6656, 7168, 7680, 7808, 7936, 8064)
_PAD_FROM = ((4872, 7944), (0, 512), (768, 1280), (2304, 2816), (2816, 4352), (1280, 2304), (4360, 4872),
             (512, 768), (4352, 4360))

ALIBI = tuple(float(2.0 ** (-8.0 * (h + 1) / ATT_HEADS)) for h in range(ATT_HEADS))

ADAM_LR, ADAM_B1, ADAM_B2, ADAM_EPS, ADAM_WD, ADAM_STEP = 0.001, 0.9, 0.999, 1e-08, 0.01, 10


def _cparams(sem=None):
    return pltpu.CompilerParams(dimension_semantics=sem, vmem_limit_bytes=VMEM_LIMIT)


def _sig(x):
    return jax.nn.sigmoid(x)


def _silu(x):
    return x * _sig(x)


def _dsilu(x):
    s = _sig(x)
    return s * (1.0 + x * (1.0 - s))


def _dot(a, b, dims=((1,), (0,)), precision=None):
    return lax.dot_general(a, b, (dims, ((), ())), preferred_element_type=F32, precision=precision)


def _bdot(a, b, dims=((1,), (0,))):
    return _dot(a.astype(BF16), b.astype(BF16), dims)


NN, NT, TN = ((1,), (0,)), ((1,), (1,)), ((0,), (0,))


def _row(v):
    return v.reshape(1, -1)


def _mm(a, b, *, ta=False, tb=False, tm, tn, tk, name):
    M, K = (a.shape[1], a.shape[0]) if ta else a.shape
    N = b.shape[0] if tb else b.shape[1]
    assert M % tm == 0 and N % tn == 0 and K % tk == 0, (M, N, K, tm, tn, tk)
    nk = K // tk
    dims = ((0 if ta else 1,), (1 if tb else 0,))

    def body(a_ref, b_ref, o_ref):
        k = pl.program_id(2)
        part = _bdot(a_ref[...], b_ref[...], dims)

        @pl.when(k == 0)
        def _():
            o_ref[...] = part

        @pl.when(k > 0)
        def _():
            o_ref[...] += part

    a_spec = pl.BlockSpec((tk, tm), lambda i, j, k: (k, i)) if ta else pl.BlockSpec((tm, tk), lambda i, j, k: (i, k))
    b_spec = pl.BlockSpec((tn, tk), lambda i, j, k: (j, k)) if tb else pl.BlockSpec((tk, tn), lambda i, j, k: (k, j))
    return pl.pallas_call(
        body, name=name, grid=(M // tm, N // tn, nk),
        in_specs=[a_spec, b_spec], out_specs=pl.BlockSpec((tm, tn), lambda i, j, k: (i, j)),
        out_shape=jax.ShapeDtypeStruct((M, N), F32),
        compiler_params=_cparams(("parallel", "parallel", "arbitrary")),
    )(a, b)


def _norm_fwd(x, norm_g, scale, shift, name):
    T = x.shape[0]
    tm = min(512, T)

    def body(x_ref, g_ref, sc_ref, sh_ref, h_ref):
        xv = x_ref[...]
        r = lax.rsqrt(jnp.mean(xv * xv, axis=-1, keepdims=True) + EPS)
        h_ref[...] = ((xv * r) * g_ref[...] * (1.0 + sc_ref[...]) + sh_ref[...]).astype(BF16)

    vec = pl.BlockSpec((1, D), lambda i: (0, 0))
    return pl.pallas_call(
        body, name=name, grid=(T // tm,),
        in_specs=[pl.BlockSpec((tm, D), lambda i: (i, 0)), vec, vec, vec],
        out_specs=pl.BlockSpec((tm, D), lambda i: (i, 0)),
        out_shape=jax.ShapeDtypeStruct((T, D), BF16),
        compiler_params=_cparams(("parallel",)),
    )(x, _row(norm_g), _row(scale), _row(shift))


def _norm_bwd(dh, x, dres, norm_g, scale, name):
    T = x.shape[0]
    tm = min(512, T)

    def body(dh_ref, x_ref, dr_ref, g_ref, sc_ref, dx_ref, st_ref):
        i = pl.program_id(0)
        xv, dhv = x_ref[...], dh_ref[...]
        r = lax.rsqrt(jnp.mean(xv * xv, axis=-1, keepdims=True) + EPS)
        xh = xv * r
        g, s1 = g_ref[...], 1.0 + sc_ref[...]
        dxh = dhv * (g * s1)
        dx_ref[...] = dr_ref[...] + r * (dxh - xh * jnp.mean(dxh * xh, axis=-1, keepdims=True))
        dhx = dhv * xh
        upd = jnp.concatenate([jnp.sum(dhv, axis=0, keepdims=True), jnp.sum(dhx * g, axis=0, keepdims=True),
                               jnp.sum(dhx * s1, axis=0, keepdims=True), jnp.zeros((5, D), F32)], axis=0)

        @pl.when(i == 0)
        def _():
            st_ref[...] = upd

        @pl.when(i > 0)
        def _():
            st_ref[...] += upd

    vec = pl.BlockSpec((1, D), lambda i: (0, 0))
    blk = pl.BlockSpec((tm, D), lambda i: (i, 0))
    return pl.pallas_call(
        body, name=name, grid=(T // tm,),
        in_specs=[blk, blk, blk, vec, vec],
        out_specs=[blk, pl.BlockSpec((8, D), lambda i: (0, 0))],
        out_shape=[jax.ShapeDtypeStruct((T, D), F32), jax.ShapeDtypeStruct((8, D), F32)],
        compiler_params=_cparams(("arbitrary",)),
    )(dh, x, dres, _row(norm_g), _row(scale))


def _rms(x, g):
    r = lax.rsqrt(jnp.mean(x * x, axis=-1, keepdims=True) + EPS)
    return x * r, r


def _head_mean_matrix():
    head = np.arange(ATT_HEADS * ATT_HD) // ATT_HD
    return jnp.asarray((head[:, None] == head[None, :]) * (1.0 / ATT_HD), BF16)


def _head_rms(x, hm):
    r = lax.rsqrt(_dot_exact(x * x, hm) + EPS)
    return x * r, r


def _head_rms_bwd(dy, xh, r, g, hm):
    dxh = dy * g
    return r * (dxh - xh * _dot_exact(dxh * xh, hm)), dy * xh


def _attn_mask(n):
    qi = lax.broadcasted_iota(jnp.int32, (WINDOW, 2 * WINDOW), 0)
    kj = lax.broadcasted_iota(jnp.int32, (WINDOW, 2 * WINDOW), 1)
    dist = qi + WINDOW - kj
    valid = (dist >= 0) & (dist < WINDOW) & ((n > 0) | (kj >= WINDOW))
    return valid, dist.astype(F32)


def _attn_probs(s, h, sink, valid, distf):
    s = s - ALIBI[h] * distf
    s = jnp.where(valid, s, NEG_INF)
    m = jnp.maximum(jnp.max(s, axis=-1, keepdims=True), sink)
    p = jnp.exp(s - m)
    es = jnp.exp(sink - m)
    den = jnp.sum(p, axis=-1, keepdims=True) + es
    return p / den, es / den


def _attn_fwd(proj, q_norm_g, k_norm_g, sinks, name):
    T = proj.shape[0]
    nb = T // WINDOW

    def body(sink_ref, q_ref, z_ref, kc_ref, kp_ref, vc_ref, vp_ref, qg_ref, kg_ref, hm_ref, o_ref):
        n = pl.program_id(0)
        valid, distf = _attn_mask(n)
        k2 = jnp.concatenate([kp_ref[...], kc_ref[...]], axis=0)
        v2 = jnp.concatenate([vp_ref[...], vc_ref[...]], axis=0).astype(BF16)
        kn = (_head_rms(k2, hm_ref[:128, :128])[0] * kg_ref[...]).astype(BF16)
        qn = ((_head_rms(q_ref[...], hm_ref[...])[0] * qg_ref[...]) * (ATT_HD ** -0.5)).astype(BF16)

        def head(h):
            sl, gsl = slice(64 * h, 64 * h + 64), slice(64 * (h // 4), 64 * (h // 4) + 64)
            s = _dot(qn[:, sl], kn[:, gsl], NT)
            yield
            p, _ = _attn_probs(s, h, sink_ref[h], valid, distf)
            o_ref[:, sl] = _dot(p.astype(BF16), v2[:, gsl])
            yield

        _lockstep([head(h) for h in range(ATT_HEADS)])
        o_ref[...] = o_ref[...] * _silu(z_ref[...])

    prev = lambda n: jnp.maximum(n - 1, 0)
    return pl.pallas_call(
        body, name=name, grid=(nb,),
        in_specs=[pl.BlockSpec(memory_space=pltpu.SMEM),
                  pl.BlockSpec((WINDOW, 512), lambda n: (n, C_QA // 512)),
                  pl.BlockSpec((WINDOW, 512), lambda n: (n, C_ZA // 512)),
                  pl.BlockSpec((WINDOW, 128), lambda n: (n, C_KA // 128)),
                  pl.BlockSpec((WINDOW, 128), lambda n: (prev(n), C_KA // 128)),
                  pl.BlockSpec((WINDOW, 128), lambda n: (n, C_VA // 128)),
                  pl.BlockSpec((WINDOW, 128), lambda n: (prev(n), C_VA // 128)),
                  pl.BlockSpec((1, 512), lambda n: (0, 0)), pl.BlockSpec((1, 128), lambda n: (0, 0)),
                  pl.BlockSpec((512, 512), lambda n: (0, 0))],
        out_specs=pl.BlockSpec((WINDOW, 512), lambda n: (n, 0)),
        out_shape=jax.ShapeDtypeStruct((T, 512), F32),
        compiler_params=_cparams(("parallel",)),
    )(sinks, proj, proj, proj, proj, proj, proj, _row(jnp.tile(q_norm_g, ATT_HEADS)), _row(jnp.tile(k_norm_g, 2)),
      _head_mean_matrix())


def _rms_bwd(dy, xh, r, g):
    dxh = dy * g
    return r * (dxh - xh * jnp.mean(dxh * xh, axis=-1, keepdims=True)), dy * xh


def _attn_bwd(dproj, proj, dya, q_norm_g, k_norm_g, sinks, name):
    T = proj.shape[0]
    nb = T // WINDOW

    def body(sink_ref, dp_any, q_ref, z_ref, kc_ref, kp_ref, vc_ref, vp_ref, dy_ref, qg_ref, kg_ref, hm_ref,
             dqz_ref, dkv_ref, gq_ref, gk_ref, gs_ref, ck_ref, cv_ref, o_sc, dq_sc):
        n = pl.program_id(0)

        @pl.when(n == 0)
        def _():
            gq_ref[...] = jnp.zeros_like(gq_ref)
            gk_ref[...] = jnp.zeros_like(gk_ref)
            gs_ref[...] = jnp.zeros_like(gs_ref)
            ck_ref[...] = jnp.zeros_like(ck_ref)
            cv_ref[...] = jnp.zeros_like(cv_ref)

        lane8 = lax.broadcasted_iota(jnp.int32, (1, 8), 1)

        @pl.when(n < nb)
        def _():
            valid, distf = _attn_mask(n)
            k2 = jnp.concatenate([kp_ref[...], kc_ref[...]], axis=0)
            v2 = jnp.concatenate([vp_ref[...], vc_ref[...]], axis=0).astype(BF16)
            kn = (_head_rms(k2, hm_ref[:128, :128])[0] * kg_ref[...]).astype(BF16)
            qh, qr = _head_rms(q_ref[...], hm_ref[...])
            qn = ((qh * qg_ref[...]) * (ATT_HD ** -0.5)).astype(BF16)
            zs = z_ref[...]
            do_all = dy_ref[...] * _silu(zs)
            dob_all = do_all.astype(BF16)

            def head(h):
                sl, gsl = slice(64 * h, 64 * h + 64), slice(64 * (h // 4), 64 * (h // 4) + 64)
                s = _dot(qn[:, sl], kn[:, gsl], NT)
                dpm = _dot(dob_all[:, sl], v2[:, gsl], NT)
                yield
                p, ps = _attn_probs(s, h, sink_ref[h], valid, distf)
                pb = p.astype(BF16)
                o_sc[:, sl] = _dot(pb, v2[:, gsl])
                dvg = _dot(pb, dob_all[:, sl], TN)
                delta = jnp.sum(p * dpm, axis=-1, keepdims=True)
                ds = (p * (dpm - delta)).astype(BF16)
                gs = jnp.where(lane8 == h, -jnp.sum(ps * delta, axis=0, keepdims=True), 0.0)
                yield
                dkn = _dot(ds, qn[:, sl], TN)
                dq_sc[:, sl] = _dot(ds, kn[:, gsl])
                yield
                return dkn, dvg, gs

            res = _lockstep([head(h) for h in range(ATT_HEADS)])
            dqz_ref[:, 512:] = (dy_ref[...] * o_sc[...] * _dsilu(zs)).astype(BF16)
            dq, gq = _head_rms_bwd(dq_sc[...] * (ATT_HD ** -0.5), qh, qr, qg_ref[...], hm_ref[...])
            dqz_ref[:, :512] = dq.astype(BF16)
            gq_acc = jnp.sum(gq, axis=0, keepdims=True)
            gs_acc = sum(r[2] for r in res[1:]) + res[0][2]
            for g in range(2):
                dkn = (res[4 * g][0] + res[4 * g + 1][0]) + (res[4 * g + 2][0] + res[4 * g + 3][0])
                dvg = (res[4 * g][1] + res[4 * g + 1][1]) + (res[4 * g + 2][1] + res[4 * g + 3][1])
                ksl = slice(64 * g, 64 * g + 64)
                vsl = slice(128 + 64 * g, 128 + 64 * g + 64)
                dkv_ref[:, ksl] = ck_ref[:, ksl] + dkn[:WINDOW]
                dkv_ref[:, vsl] = cv_ref[:, ksl] + dvg[:WINDOW]
                ck_ref[:, ksl] = dkn[WINDOW:]
                cv_ref[:, ksl] = dvg[WINDOW:]
            gq_ref[...] += gq_acc
            gs_ref[...] += gs_acc

        @pl.when(n == nb)
        def _():
            dkv_ref[:, :128] = ck_ref[...]
            dkv_ref[:, 128:] = cv_ref[...]

        @pl.when(n > 0)
        def _():
            hm = hm_ref[:128, :128]
            kh, kr = _head_rms(kp_ref[...], hm)
            dk, gk = _head_rms_bwd(dkv_ref[:, :128], kh, kr, kg_ref[...], hm)
            dkv_ref[:, :128] = dk
            gk_ref[...] += jnp.sum(gk, axis=0, keepdims=True)

    cur = lambda n: jnp.minimum(n, nb - 1)
    prev = lambda n: jnp.maximum(n - 1, 0)
    small = lambda w: pl.BlockSpec((1, w), lambda n: (0, 0))
    return pl.pallas_call(
        body, name=name, grid=(nb + 1,),
        in_specs=[pl.BlockSpec(memory_space=pltpu.SMEM), pl.BlockSpec(memory_space=pl.ANY),
                  pl.BlockSpec((WINDOW, 512), lambda n: (cur(n), C_QA // 512)),
                  pl.BlockSpec((WINDOW, 512), lambda n: (cur(n), C_ZA // 512)),
                  pl.BlockSpec((WINDOW, 128), lambda n: (cur(n), C_KA // 128)),
                  pl.BlockSpec((WINDOW, 128), lambda n: (prev(n), C_KA // 128)),
                  pl.BlockSpec((WINDOW, 128), lambda n: (cur(n), C_VA // 128)),
                  pl.BlockSpec((WINDOW, 128), lambda n: (prev(n), C_VA // 128)),
                  pl.BlockSpec((WINDOW, 512), lambda n: (cur(n), 0)),
                  small(512), small(128), pl.BlockSpec((512, 512), lambda n: (0, 0))],
        out_specs=[pl.BlockSpec((WINDOW, 1024), lambda n: (cur(n), C_QA // 1024)),
                   pl.BlockSpec((WINDOW, 256), lambda n: (prev(n), 0)),
                   small(512), small(128), small(8)],
        out_shape=[jax.ShapeDtypeStruct(dproj.shape, BF16), jax.ShapeDtypeStruct((T, 256), F32),
                   jax.ShapeDtypeStruct((1, 512), F32), jax.ShapeDtypeStruct((1, 128), F32),
                   jax.ShapeDtypeStruct((1, 8), F32)],
        scratch_shapes=[pltpu.VMEM((WINDOW, 128), F32), pltpu.VMEM((WINDOW, 128), F32),
                        pltpu.VMEM((WINDOW, 512), F32), pltpu.VMEM((WINDOW, 512), F32)],
        input_output_aliases={1: 0},
        compiler_params=_cparams(("arbitrary",)),
    )(sinks, dproj, proj, proj, proj, proj, proj, proj, dya, _row(jnp.tile(q_norm_g, ATT_HEADS)),
      _row(jnp.tile(k_norm_g, 2)), _head_mean_matrix())


HALO_B = 32


def _conf_specs(T, tm):
    r = tm // HALO_B
    cur = lambda c: pl.BlockSpec((tm, 512), lambda i: (i, c // 512))
    prev = lambda c: pl.BlockSpec((HALO_B, 512), lambda i: (jnp.maximum(i * r - 1, 0), c // 512))
    return cur, prev


SUB = 8
ROW_CHUNK = 64


def _shifted_copies(ext_ref, sh_ref):
    total = ext_ref.shape[0]
    for r in range(SUB):
        rows = total if r == 0 else total - SUB
        sh_ref[r, :rows, :] = ext_ref[pl.ds(r, rows), :]


def _taps_by_shift(offsets):
    groups = {}
    for k, o in enumerate(offsets):
        q, r = divmod(o, SUB)
        groups.setdefault(r, []).append((k, q))
    return groups


def _conv_taps(sh_ref, w_ref, offsets, out_ref, init):
    groups = _taps_by_shift(offsets)

    def chunk(ci, carry):
        r0 = pl.multiple_of(ci * ROW_CHUNK, ROW_CHUNK)
        acc = jnp.zeros((ROW_CHUNK, out_ref.shape[1]), F32) + init
        for r, taps in groups.items():
            win = sh_ref[r, pl.ds(r0, ROW_CHUNK + SUB * max(q for _, q in taps)), :]
            for k, q in taps:
                acc = acc + w_ref[k:k + 1, :] * win[SUB * q:SUB * q + ROW_CHUNK]
        out_ref[pl.ds(r0, ROW_CHUNK), :] = acc
        return carry

    lax.fori_loop(0, out_ref.shape[0] // ROW_CHUNK, chunk, 0)


def _conv_weight_grad(sh_ref, d_ref, offsets):
    tm, width = d_ref.shape
    out = [None] * len(offsets)
    for r, taps in _taps_by_shift(offsets).items():
        def chunk(ci, accs, r=r, taps=taps):
            r0 = pl.multiple_of(ci * ROW_CHUNK, ROW_CHUNK)
            d = d_ref[pl.ds(r0, ROW_CHUNK), :]
            win = sh_ref[r, pl.ds(r0, ROW_CHUNK + SUB * max(q for _, q in taps)), :]
            return tuple(a + jnp.sum((d * win[SUB * q:SUB * q + ROW_CHUNK]).reshape(ROW_CHUNK // SUB, SUB, width),
                                     axis=0) for a, (_, q) in zip(accs, taps))

        accs = lax.fori_loop(0, tm // ROW_CHUNK, chunk, tuple(jnp.zeros((SUB, width), F32) for _ in taps))
        for a, (k, _) in zip(accs, taps):
            out[k] = jnp.sum(a, axis=0, keepdims=True)
    return out


def _conf_scratch(tm):
    return [pltpu.VMEM((tm + HALO_B, 512), F32), pltpu.VMEM((SUB, tm + HALO_B, 512), F32), pltpu.VMEM((tm, 512), F32)]


def _conf_core(i, tm, gv_ref, gg_ref, gvp_ref, ggp_ref, w_ref, b_ref, lg_ref, lb_ref, pw_ref, pb_ref, ext_ref, sh_ref,
               cv_ref):
    up = gvp_ref[...] * _sig(ggp_ref[...])
    ext_ref[:HALO_B] = jnp.where(i > 0, up, 0.0)
    ext_ref[HALO_B:] = gv_ref[...] * _sig(gg_ref[...])
    _shifted_copies(ext_ref, sh_ref)
    _conv_taps(sh_ref, w_ref, [HALO_B - CONV_K + 1 + k for k in range(CONV_K)], cv_ref, b_ref[...])
    return _conf_post(cv_ref[...], lg_ref, lb_ref, pw_ref, pb_ref)


def _conf_post(acc, lg_ref, lb_ref, pw_ref, pb_ref):
    mu = jnp.mean(acc, axis=-1, keepdims=True)
    xc = acc - mu
    rstd = lax.rsqrt(jnp.mean(xc * xc, axis=-1, keepdims=True) + EPS)
    xh = xc * rstd
    u2 = xh * lg_ref[...] + lb_ref[...]
    u3 = _silu(u2)
    ypre = _bdot(u3, pw_ref[...]) + pb_ref[...]
    return xh, rstd, u2, u3, ypre


def _conf_fwd(proj, dw_w, dw_b, ln_g, ln_b, pw2, pw2_b, name):
    T = proj.shape[0]
    tm = min(512, T)
    cur, prev = _conf_specs(T, tm)

    def body(gv_ref, gg_ref, gvp_ref, ggp_ref, zb_ref, w_ref, b_ref, lg_ref, lb_ref, pw_ref, pb_ref, o_ref, cv_ref,
             ext_ref, sh_ref):
        i = pl.program_id(0)
        ypre = _conf_core(i, tm, gv_ref, gg_ref, gvp_ref, ggp_ref, w_ref, b_ref, lg_ref, lb_ref, pw_ref, pb_ref,
                          ext_ref, sh_ref, cv_ref)[4]
        o_ref[...] = ypre * _silu(zb_ref[...])

    full = lambda s: pl.BlockSpec(s, lambda i: (0, 0))
    blk = pl.BlockSpec((tm, 512), lambda i: (i, 0))
    return pl.pallas_call(
        body, name=name, grid=(T // tm,),
        in_specs=[cur(C_GV), cur(C_GG), prev(C_GV), prev(C_GG), cur(C_ZB), full((CONV_K, 512)), full((1, 512)),
                  full((1, 512)), full((1, 512)), full((512, 512)), full((1, 512))],
        out_specs=[blk, blk],
        out_shape=[jax.ShapeDtypeStruct((T, 512), F32)] * 2,
        scratch_shapes=_conf_scratch(tm)[:2],
        compiler_params=_cparams(("parallel",)),
    )(proj, proj, proj, proj, proj, dw_w, _row(dw_b), _row(ln_g), _row(ln_b), pw2, _row(pw2_b))


def _conf_bwd1(dproj, proj, cv, dyb, ln_g, ln_b, pw2, pw2_b, name):
    T = proj.shape[0]
    tm = min(512, T)
    cur, _ = _conf_specs(T, tm)

    def body(dp_any, zb_ref, cv_ref, dy_ref, lg_ref, lb_ref, pw_ref, pb_ref, dzb_ref, du1_ref, gpw_ref, st_ref):
        i = pl.program_id(0)
        xh, rstd, u2, u3, ypre = _conf_post(cv_ref[...], lg_ref, lb_ref, pw_ref, pb_ref)
        zb, dy = zb_ref[...], dy_ref[...]
        dzb_ref[...] = (dy * ypre * _dsilu(zb)).astype(BF16)
        dyp = dy * _silu(zb)
        du2 = _bdot(dyp, pw_ref[...], NT) * _dsilu(u2)
        dxh = du2 * lg_ref[...]
        du1 = rstd * (dxh - jnp.mean(dxh, axis=-1, keepdims=True) - xh * jnp.mean(dxh * xh, axis=-1, keepdims=True))
        du1_ref[...] = du1
        gpw = _bdot(u3, dyp, TN)
        rs = lambda a: jnp.sum(a, axis=0, keepdims=True)
        upd = jnp.concatenate([rs(dyp), rs(du2 * xh), rs(du2), rs(du1), jnp.zeros((4, 512), F32)], axis=0)

        @pl.when(i == 0)
        def _():
            gpw_ref[...] = gpw
            st_ref[...] = upd

        @pl.when(i > 0)
        def _():
            gpw_ref[...] += gpw
            st_ref[...] += upd

    full = lambda s: pl.BlockSpec(s, lambda i: (0, 0))
    blk = pl.BlockSpec((tm, 512), lambda i: (i, 0))
    return pl.pallas_call(
        body, name=name, grid=(T // tm,),
        in_specs=[pl.BlockSpec(memory_space=pl.ANY), cur(C_ZB), blk, blk,
                  full((1, 512)), full((1, 512)), full((512, 512)), full((1, 512))],
        out_specs=[cur(C_ZB), blk, full((512, 512)), full((8, 512))],
        out_shape=[jax.ShapeDtypeStruct(dproj.shape, BF16), jax.ShapeDtypeStruct((T, 512), F32),
                   jax.ShapeDtypeStruct((512, 512), F32), jax.ShapeDtypeStruct((8, 512), F32)],
        input_output_aliases={0: 0},
        compiler_params=_cparams(("arbitrary",)),
    )(dproj, proj, cv, dyb, _row(ln_g), _row(ln_b), pw2, _row(pw2_b))


def _conf_bwd2(dproj, proj, du1, dw_w, name):
    T = proj.shape[0]
    tm = min(512, T)
    nt = T // tm
    r = tm // HALO_B
    cur, prev = _conf_specs(T, tm)

    def body(dp_any, gv_ref, gg_ref, gvp_ref, ggp_ref, du_ref, dun_ref, w_ref, dglu_ref, gw_ref, ext_ref, sh_ref,
             cv_ref):
        i = pl.program_id(0)
        gv, sg = gv_ref[...], _sig(gg_ref[...])
        ext_ref[:HALO_B] = jnp.where(i > 0, gvp_ref[...] * _sig(ggp_ref[...]), 0.0)
        ext_ref[HALO_B:] = gv * sg
        _shifted_copies(ext_ref, sh_ref)
        rows = _conv_weight_grad(sh_ref, du_ref, [HALO_B - CONV_K + 1 + k for k in range(CONV_K)])
        rows.append(jnp.zeros((1, 512), F32))
        gw = jnp.concatenate(rows, axis=0)
        ext_ref[:tm] = du_ref[...]
        ext_ref[tm:] = jnp.where(i < nt - 1, dun_ref[...], 0.0)
        _shifted_copies(ext_ref, sh_ref)
        _conv_taps(sh_ref, w_ref, [CONV_K - 1 - k for k in range(CONV_K)], cv_ref, 0.0)
        du0 = cv_ref[...]
        dglu_ref[:, :512] = (du0 * sg).astype(BF16)
        dglu_ref[:, 512:] = (du0 * gv * sg * (1.0 - sg)).astype(BF16)

        @pl.when(i == 0)
        def _():
            gw_ref[...] = gw

        @pl.when(i > 0)
        def _():
            gw_ref[...] += gw

    full = lambda s: pl.BlockSpec(s, lambda i: (0, 0))
    return pl.pallas_call(
        body, name=name, grid=(nt,),
        in_specs=[pl.BlockSpec(memory_space=pl.ANY), cur(C_GV), cur(C_GG), prev(C_GV), prev(C_GG),
                  pl.BlockSpec((tm, 512), lambda i: (i, 0)),
                  pl.BlockSpec((HALO_B, 512), lambda i: (jnp.minimum((i + 1) * r, T // HALO_B - 1), 0)),
                  full((CONV_K, 512))],
        out_specs=[pl.BlockSpec((tm, 1024), lambda i: (i, C_GV // 1024)), full((32, 512))],
        out_shape=[jax.ShapeDtypeStruct(dproj.shape, BF16), jax.ShapeDtypeStruct((32, 512), F32)],
        scratch_shapes=_conf_scratch(tm),
        input_output_aliases={0: 0},
        compiler_params=_cparams(("arbitrary",)),
    )(dproj, proj, proj, proj, proj, du1, du1, dw_w)


HALO_C = 8
QKV_C = 1536


def _softplus(x):
    return jnp.maximum(x, 0.0) + jnp.log1p(jnp.exp(-jnp.abs(x)))


def _gdn_conv(i, tm, x_ref, xp_ref, w_ref, ext_ref):
    ext_ref[:HALO_C] = jnp.where(i > 0, xp_ref[...], 0.0)
    ext_ref[HALO_C:] = x_ref[...]
    pre = jnp.zeros((tm, QKV_C), F32)
    for k in range(DN_K):
        pre = pre + w_ref[k:k + 1, :] * ext_ref[pl.ds(HALO_C - DN_K + 1 + k, tm), :]
    return pre


def _gdn_specs(T, tm):
    r = tm // HALO_C
    cur = pl.BlockSpec((tm, QKV_C), lambda i: (i, C_QC // QKV_C))
    prev = pl.BlockSpec((HALO_C, QKV_C), lambda i: (jnp.maximum(i * r - 1, 0), C_QC // QKV_C))
    ab = pl.BlockSpec((tm, 128), lambda i: (i, C_AB // 128))
    return cur, prev, ab


def _gdn_prep_fwd(proj, sconv_w, alog_v, dtb_v, name):
    T = proj.shape[0]
    tm = min(512, T)
    cur, prev, ab = _gdn_specs(T, tm)

    def body(x_ref, xp_ref, ab_ref, w_ref, al_ref, dt_ref, q_ref, k_ref, v_ref, gb_ref, ext_ref):
        i = pl.program_id(0)
        y = _silu(_gdn_conv(i, tm, x_ref, xp_ref, w_ref, ext_ref))
        for h in range(DN_HEADS):
            sl = slice(128 * h, 128 * h + 128)
            qh, kh = y[:, sl], y[:, 512 + 128 * h:512 + 128 * h + 128]
            q_ref[:, sl] = qh * lax.rsqrt(jnp.sum(qh * qh, axis=-1, keepdims=True) + EPS) * (128 ** -0.5)
            k_ref[:, sl] = kh * lax.rsqrt(jnp.sum(kh * kh, axis=-1, keepdims=True) + EPS)
        v_ref[...] = y[:, 1024:]
        abv = ab_ref[...]
        lane = lax.broadcasted_iota(jnp.int32, (tm, 128), 1)
        g = -jnp.exp(al_ref[...]) * _softplus(abv + dt_ref[...])
        gb_ref[...] = jnp.where(lane < DN_HEADS, g, _sig(abv))

    full = lambda s: pl.BlockSpec(s, lambda i: (0, 0))
    blk = pl.BlockSpec((tm, 512), lambda i: (i, 0))
    return pl.pallas_call(
        body, name=name, grid=(T // tm,),
        in_specs=[cur, prev, ab, full((DN_K, QKV_C)), full((1, 128)), full((1, 128))],
        out_specs=[blk, blk, blk, pl.BlockSpec((tm, 128), lambda i: (i, 0))],
        out_shape=[jax.ShapeDtypeStruct((T, 512), F32)] * 3 + [jax.ShapeDtypeStruct((T, 128), F32)],
        scratch_shapes=[pltpu.VMEM((tm + HALO_C, QKV_C), F32)],
        compiler_params=_cparams(("parallel",)),
    )(proj, proj, proj, sconv_w, alog_v, dtb_v)


def _hdot(a, b, dims=NN):
    return _dot(a, b, dims, precision=HI)


def _lockstep(gens):
    results, live = [None] * len(gens), list(range(len(gens)))
    while live:
        for i in list(live):
            try:
                next(gens[i])
            except StopIteration as stop:
                results[i] = stop.value
                live.remove(i)
    return results


def _dot_exact(a, b, dims=NN, split_left=True):
    x = (a if split_left else b).astype(F32)
    hi = x.astype(BF16)
    r = x - hi.astype(F32)
    mid = r.astype(BF16)
    lo = (r - mid.astype(F32)).astype(BF16)
    other = (b if split_left else a).astype(BF16)
    one = (lambda p: _dot(p, other, dims)) if split_left else (lambda p: _dot(other, p, dims))
    return (one(lo) + one(mid)) + one(hi)


def _split(a):
    hi = a.astype(BF16)
    return hi, (a - hi.astype(F32)).astype(BF16)


def _dot3(a, b):
    (ah, al), (bh, bl) = a, b
    return _dot(ah, bh) + (_dot(ah, bl) + _dot(al, bh))


INV_BLOCK = 16


def _tri_inv(mats, eye):
    ii = lax.broadcasted_iota(jnp.int32, (CHUNK, CHUNK), 0) // INV_BLOCK
    jj = lax.broadcasted_iota(jnp.int32, (CHUNK, CHUNK), 1) // INV_BLOCK
    diag = ii == jj
    mul = lambda xs, ys: [_dot3(_split(x), _split(y)) for x, y in zip(xs, ys)]
    ps = [jnp.where(diag, -a, 0.0) for a in mats]
    tds = [eye + p for p in ps]
    for _ in range(3):
        ps = mul(ps, ps)
        tds = [t + tp for t, tp in zip(tds, mul(tds, ps))]
    ms = mul(tds, [jnp.where(diag, 0.0, a) for a in mats])
    m2 = mul(ms, ms)
    inv = [(eye - m) + x for m, x in zip(ms, mul([eye - m for m in ms], m2))]
    return mul(inv, tds)


def _tri_consts():
    ii = lax.broadcasted_iota(jnp.int32, (CHUNK, CHUNK), 0)
    jj = lax.broadcasted_iota(jnp.int32, (CHUNK, CHUNK), 1)
    return ii >= jj, ii > jj, (ii == jj).astype(F32)


def _gdn_local(q, k, v, gcol, grow, bcol, lower, strict):
    dm = jnp.where(lower, jnp.exp(jnp.where(lower, gcol - grow, 0.0)), 0.0)
    kb = k * bcol
    a = jnp.where(strict, _bdot(kb, k, NT) * dm, 0.0)
    gc = jnp.exp(gcol)
    glast = grow[:, CHUNK - 1:CHUNK]
    return dict(q=q, k=k, v=v, bcol=bcol, gcol=gcol, glast=glast, dm=dm, kb=kb, a=a, gc=gc, vb=v * bcol,
                kbg=kb * gc, p=_bdot(q, k, NT) * dm, qe=q * gc, ke=k * jnp.exp(glast - gcol))


def _gdn_chunk_bwd(c, do, dvn, ds_new, lower, strict, ones):
    rs = lambda m: jnp.sum(m, axis=-1, keepdims=True)
    colsum = lambda m: _dot_exact(m, ones, TN)[:, :1]
    q, k, v, bcol, dm, tm, gc, s = c["q"], c["k"], c["v"], c["bcol"], c["dm"], c["tm"], c["gc"], c["s"]
    eg = jnp.exp(c["glast"])
    dqe = _bdot(do, s, NT)
    dp = jnp.where(lower, _bdot(do, c["vn"], NT), 0.0)
    dw = -_bdot(dvn, s, NT)
    dke = _bdot(c["vn"], ds_new, NT)
    dvb = _bdot(tm, dvn, TN)
    yield
    dglast = jnp.sum(rs(ds_new * s), axis=0, keepdims=True) * eg
    dk = dke * jnp.exp(c["glast"] - c["gcol"])
    r_ke = rs(dke * c["ke"])
    dglast = dglast + jnp.sum(r_ke, axis=0, keepdims=True)
    dgam = rs(dqe * c["qe"]) - r_ke
    dq = dqe * gc
    dpm = dp * dm
    mp = dp * c["p"]
    dq = dq + _bdot(dpm, k)
    dk = dk + _bdot(dpm, q, TN)
    dt = _bdot(dvn, c["vb"], NT) + _bdot(dw, c["kbg"], NT)
    dkbg = _bdot(tm, dw, TN)
    dgam = dgam + rs(mp) - colsum(mp)
    yield
    dkb = dkbg * gc
    dgam = dgam + rs(dkbg * c["kbg"])
    dat = _bdot(tm, dt, TN)
    yield
    da = jnp.where(strict, -_bdot(dat, tm, NT), 0.0)
    yield
    dam = da * dm
    ma = da * c["a"]
    dkb = dkb + _bdot(dam, k)
    dk = dk + _bdot(dam, c["kb"], TN)
    dgam = dgam + rs(ma) - colsum(ma)
    yield
    dk = dk + dkb * bcol
    dbeta = rs(dkb * k) + rs(dvb * v)
    dv = dvb * bcol
    row = lax.broadcasted_iota(jnp.int32, (CHUNK, 1), 0)
    dgam = dgam + jnp.where(row == CHUNK - 1, dglast, 0.0)
    dg = _dot_exact(lower, dgam, TN, split_left=False)
    return dq, dk, dv, dg, dbeta


SCAN_GROUP = 8
GROUP = 8


def _chunk_decay(gb_ref, gt_ref, lmat, g):
    rows = slice(CHUNK * g, CHUNK * g + CHUNK)
    return rows, _dot_exact(lmat, gb_ref[rows, :], split_left=False), _dot_exact(gt_ref[g], lmat, NT)


def _gdn_chunk_fwd(qd, kd, vd, gb, gbt, name):
    T = qd.shape[0]
    G = min(GROUP, T // CHUNK)
    ng = T // (CHUNK * G)

    def body(q_ref, k_ref, v_ref, gb_ref, gt_ref, u_ref, w_ref, qe_ref, ke_ref, p_ref, t_ref, eg_ref):
        lower, strict, eye = _tri_consts()
        lmat = lower.astype(F32)
        decay = [_chunk_decay(gb_ref, gt_ref, lmat, g) for g in range(G)]
        chains = [(g, h) for g in range(G) for h in range(DN_HEADS)]
        cs = []
        for g, h in chains:
            rows, gcs, grs = decay[g]
            sl = slice(128 * h, 128 * h + 128)
            c = _gdn_local(q_ref[rows, sl], k_ref[rows, sl], v_ref[rows, sl], gcs[:, h:h + 1], grs[h:h + 1, :],
                           gb_ref[rows, DN_HEADS + h:DN_HEADS + h + 1], lower, strict)
            qe_ref[rows, sl] = c["qe"].astype(BF16)
            ke_ref[rows, sl] = c["ke"].astype(BF16)
            p_ref[rows, 64 * h:64 * h + 64] = c["p"].astype(BF16)
            eg_ref[g, h:h + 1, :] = jnp.broadcast_to(jnp.exp(c["glast"]), (1, 128))
            cs.append(c)
        tms = [t.astype(BF16) for t in _tri_inv([c["a"] for c in cs], eye)]
        us = [_dot(t, c["vb"].astype(BF16)) for t, c in zip(tms, cs)]
        ws = [_dot(t, c["kbg"].astype(BF16)) for t, c in zip(tms, cs)]
        for (g, h), tm, u, w in zip(chains, tms, us, ws):
            rows, sl = decay[g][0], slice(128 * h, 128 * h + 128)
            u_ref[rows, sl] = u
            w_ref[rows, sl] = w.astype(BF16)
            t_ref[rows, 64 * h:64 * h + 64] = tm
        for g in range(G):
            eg_ref[g, DN_HEADS:, :] = jnp.zeros((8 - DN_HEADS, 128), F32)

    blk = pl.BlockSpec((CHUNK * G, 512), lambda n: (n, 0))
    half = pl.BlockSpec((CHUNK * G, 256), lambda n: (n, 0))
    return pl.pallas_call(
        body, name=name, grid=(ng,),
        in_specs=[blk, blk, blk, pl.BlockSpec((CHUNK * G, 128), lambda n: (n, 0)),
                  pl.BlockSpec((G, 8, CHUNK), lambda n: (n, 0, 0))],
        out_specs=[blk, blk, blk, blk, half, half, pl.BlockSpec((G, 8, 128), lambda n: (n, 0, 0))],
        out_shape=[jax.ShapeDtypeStruct((T, 512), F32)] + [jax.ShapeDtypeStruct((T, 512), BF16)] * 3
        + [jax.ShapeDtypeStruct((T, 256), BF16)] * 2 + [jax.ShapeDtypeStruct((T // CHUNK, 8, 128), F32)],
        compiler_params=_cparams(("parallel",)),
    )(qd, kd, vd, gb, gbt)


def _gdn_scan_fwd(u, w, qe, ke, pm, eg, proj, dn_g, name):
    T = u.shape[0]
    nc = T // CHUNK
    G = min(SCAN_GROUP, T // CHUNK)

    def body(u_ref, w_ref, qe_ref, ke_ref, p_ref, eg_ref, z_ref, ng_ref, y_ref, o_ref, vn_ref, ss_ref, s_ref):
        n = pl.program_id(0)

        @pl.when(n == 0)
        def _():
            s_ref[...] = jnp.zeros_like(s_ref)

        def head(j, h):
            rows, sl = slice(CHUNK * j, CHUNK * j + CHUNK), slice(128 * h, 128 * h + 128)
            s = s_ref[h]
            sb = s.astype(BF16)
            vn = u_ref[rows, sl] - _dot(w_ref[rows, sl], sb)
            qs = _dot(qe_ref[rows, sl], sb)
            yield
            vb = vn.astype(BF16)
            o = qs + _dot(p_ref[rows, 64 * h:64 * h + 64], vb)
            s_ref[h] = s * eg_ref[j, h:h + 1, :] + _dot(ke_ref[rows, sl], vb, TN)
            yield
            vn_ref[rows, sl] = vb
            o_ref[rows, sl] = o
            y_ref[rows, sl] = _rms(o, None)[0] * ng_ref[...] * _silu(z_ref[rows, sl])

        for j in range(G):
            ss_ref[j] = s_ref[...]
            _lockstep([head(j, h) for h in range(DN_HEADS)])

    blk = pl.BlockSpec((CHUNK * G, 512), lambda n: (n, 0))
    return pl.pallas_call(
        body, name=name, grid=(nc // G,),
        in_specs=[blk, blk, blk, blk, pl.BlockSpec((CHUNK * G, 256), lambda n: (n, 0)),
                  pl.BlockSpec((G, 8, 128), lambda n: (n, 0, 0)),
                  pl.BlockSpec((CHUNK * G, 512), lambda n: (n, C_ZC // 512)), pl.BlockSpec((1, 128), lambda n: (0, 0))],
        out_specs=[blk, blk, blk, pl.BlockSpec((G, DN_HEADS, 128, 128), lambda n: (n, 0, 0, 0))],
        out_shape=[jax.ShapeDtypeStruct((T, 512), F32), jax.ShapeDtypeStruct((T, 512), F32),
                   jax.ShapeDtypeStruct((T, 512), BF16), jax.ShapeDtypeStruct((nc, DN_HEADS, 128, 128), F32)],
        scratch_shapes=[pltpu.VMEM((DN_HEADS, 128, 128), F32)],
        compiler_params=_cparams(("arbitrary",)),
    )(u, w, qe, ke, pm, eg, proj, dn_g)


def _gdn_scan_bwd(dproj, w, qe, ke, pm, eg, o, proj, dyc, dn_g, name):
    T = o.shape[0]
    nc = T // CHUNK
    G = min(SCAN_GROUP, T // CHUNK)
    rev = lambda n: nc // G - 1 - n

    def body(dp_any, w_ref, qe_ref, ke_ref, p_ref, eg_ref, o_ref, z_ref, dy_ref, ng_ref,
             dz_ref, do_ref, dvn_ref, dsn_ref, gng_ref, ds_ref):
        n = pl.program_id(0)

        @pl.when(n == 0)
        def _():
            ds_ref[...] = jnp.zeros_like(ds_ref)
            gng_ref[...] = jnp.zeros_like(gng_ref)

        def head(j, h):
            rows, sl = slice(CHUNK * j, CHUNK * j + CHUNK), slice(128 * h, 128 * h + 128)
            oh, r = _rms(o_ref[rows, sl], None)
            z, dy = z_ref[rows, sl], dy_ref[rows, sl]
            dz_ref[rows, sl] = (dy * (oh * ng_ref[...]) * _dsilu(z)).astype(BF16)
            do, gg = _rms_bwd(dy * _silu(z), oh, r, ng_ref[...])
            dob = do.astype(BF16)
            ds = ds_ref[h]
            dvn = _dot(p_ref[rows, 64 * h:64 * h + 64], dob, TN) + _dot(ke_ref[rows, sl], ds.astype(BF16))
            qd = _dot(qe_ref[rows, sl], dob, TN)
            yield
            dvb = dvn.astype(BF16)
            ds_ref[h] = qd + eg_ref[j, h:h + 1, :] * ds - _dot(w_ref[rows, sl], dvb, TN)
            do_ref[rows, sl] = dob
            dvn_ref[rows, sl] = dvb
            return jnp.sum(gg, axis=0, keepdims=True)

        for j in reversed(range(G)):
            dsn_ref[j] = ds_ref[...]
            gng = _lockstep([head(j, h) for h in range(DN_HEADS)])
            gng_ref[...] += (gng[0] + gng[1]) + (gng[2] + gng[3])

    blk = pl.BlockSpec((CHUNK * G, 512), lambda n: (rev(n), 0))
    state = pl.BlockSpec((G, DN_HEADS, 128, 128), lambda n: (rev(n), 0, 0, 0))
    return pl.pallas_call(
        body, name=name, grid=(nc // G,),
        in_specs=[pl.BlockSpec(memory_space=pl.ANY), blk, blk, blk,
                  pl.BlockSpec((CHUNK * G, 256), lambda n: (rev(n), 0)),
                  pl.BlockSpec((G, 8, 128), lambda n: (rev(n), 0, 0)), blk,
                  pl.BlockSpec((CHUNK * G, 512), lambda n: (rev(n), C_ZC // 512)), blk,
                  pl.BlockSpec((1, 128), lambda n: (0, 0))],
        out_specs=[pl.BlockSpec((CHUNK * G, 512), lambda n: (rev(n), C_ZC // 512)), blk, blk, state,
                   pl.BlockSpec((1, 128), lambda n: (0, 0))],
        out_shape=[jax.ShapeDtypeStruct(dproj.shape, BF16), jax.ShapeDtypeStruct((T, 512), BF16),
                   jax.ShapeDtypeStruct((T, 512), BF16), jax.ShapeDtypeStruct((nc, DN_HEADS, 128, 128), F32),
                   jax.ShapeDtypeStruct((1, 128), F32)],
        scratch_shapes=[pltpu.VMEM((DN_HEADS, 128, 128), F32)],
        input_output_aliases={0: 0},
        compiler_params=_cparams(("arbitrary",)),
    )(dproj, w, qe, ke, pm, eg, o, proj, dyc, dn_g)


def _gdn_chunk_grad(qd, kd, vd, gb, gbt, tmi, ssave, dsn, do, dvn, vn, name):
    T = qd.shape[0]
    G = min(GROUP, T // CHUNK)
    ng = T // (CHUNK * G)

    def body(q_ref, k_ref, v_ref, gb_ref, gt_ref, t_ref, ss_ref, dsn_ref, do_ref, dvn_ref, vn_ref,
             dq_ref, dk_ref, dv_ref, dgb_ref):
        lower, strict, _ = _tri_consts()
        lmat = lower.astype(F32)
        ones = jnp.ones((CHUNK, 128), F32)
        lane = lax.broadcasted_iota(jnp.int32, (CHUNK, 128), 1)
        decay = [_chunk_decay(gb_ref, gt_ref, lmat, g) for g in range(G)]
        chains = [(g, h) for g in range(G) for h in range(DN_HEADS)]
        gens = []
        for g, h in chains:
            rows, gcs, grs = decay[g]
            sl = slice(128 * h, 128 * h + 128)
            c = _gdn_local(q_ref[rows, sl], k_ref[rows, sl], v_ref[rows, sl], gcs[:, h:h + 1], grs[h:h + 1, :],
                           gb_ref[rows, DN_HEADS + h:DN_HEADS + h + 1], lower, strict)
            c.update(tm=t_ref[rows, 64 * h:64 * h + 64], s=ss_ref[g, h], vn=vn_ref[rows, sl])
            gens.append(_gdn_chunk_bwd(c, do_ref[rows, sl], dvn_ref[rows, sl], dsn_ref[g, h], lower, strict, ones))
        dgb = [jnp.zeros((CHUNK, 128), F32) for _ in range(G)]
        for (g, h), (dq, dk, dv, dg, dbeta) in zip(chains, _lockstep(gens)):
            rows, sl = decay[g][0], slice(128 * h, 128 * h + 128)
            dq_ref[rows, sl], dk_ref[rows, sl], dv_ref[rows, sl] = dq, dk, dv
            dgb[g] = dgb[g] + jnp.where(lane == h, dg, 0.0) + jnp.where(lane == DN_HEADS + h, dbeta, 0.0)
        for g in range(G):
            dgb_ref[decay[g][0], :] = dgb[g]

    blk = pl.BlockSpec((CHUNK * G, 512), lambda n: (n, 0))
    half = pl.BlockSpec((CHUNK * G, 256), lambda n: (n, 0))
    nar = pl.BlockSpec((CHUNK * G, 128), lambda n: (n, 0))
    state = pl.BlockSpec((G, DN_HEADS, 128, 128), lambda n: (n, 0, 0, 0))
    return pl.pallas_call(
        body, name=name, grid=(ng,),
        in_specs=[blk, blk, blk, nar, pl.BlockSpec((G, 8, CHUNK), lambda n: (n, 0, 0)), half, state, state,
                  blk, blk, blk],
        out_specs=[blk, blk, blk, nar],
        out_shape=[jax.ShapeDtypeStruct((T, 512), F32)] * 3 + [jax.ShapeDtypeStruct((T, 128), F32)],
        compiler_params=_cparams(("parallel",)),
    )(qd, kd, vd, gb, gbt, tmi, ssave, dsn, do, dvn, vn)


def _gdn_prep_bwd1(dproj, proj, dqd, dkd, dvd, dgb, dkv_a, sconv_w, alog_v, dtb_v, name):
    T = proj.shape[0]
    tm = min(512, T)
    cur, prev, ab = _gdn_specs(T, tm)

    def body(dp_any, x_ref, xp_ref, ab_ref, dq_ref, dk_ref, dv_ref, dgb_ref, dkv_ref, w_ref, al_ref, dt_ref,
             o_ref, dpre_ref, st_ref, ext_ref):
        i = pl.program_id(0)
        pre = _gdn_conv(i, tm, x_ref, xp_ref, w_ref, ext_ref)
        y, dsl = _silu(pre), _dsilu(pre)
        for h in range(DN_HEADS):
            for base, g_ref, scale in ((0, dq_ref, 128 ** -0.5), (512, dk_ref, 1.0)):
                sl = slice(base + 128 * h, base + 128 * h + 128)
                xh = y[:, sl]
                r = lax.rsqrt(jnp.sum(xh * xh, axis=-1, keepdims=True) + EPS)
                xn = xh * r
                gy = g_ref[:, 128 * h:128 * h + 128]
                dpre_ref[:, sl] = (scale * r) * (gy - xn * jnp.sum(gy * xn, axis=-1, keepdims=True)) * dsl[:, sl]
        dpre_ref[:, 1024:] = dv_ref[...] * dsl[:, 1024:]
        abv, dgb = ab_ref[...], dgb_ref[...]
        lane = lax.broadcasted_iota(jnp.int32, (tm, 128), 1)
        na = -jnp.exp(al_ref[...])
        xs = abv + dt_ref[...]
        da = dgb * na * _sig(xs)
        b = _sig(abv)
        o_ref[:, :256] = dkv_ref[...].astype(BF16)
        o_ref[:, 256:] = jnp.where(lane < DN_HEADS, da,
                                   jnp.where(lane < 2 * DN_HEADS, dgb * b * (1.0 - b), 0.0)).astype(BF16)
        head = lane < DN_HEADS
        upd = jnp.concatenate([jnp.sum(jnp.where(head, dgb * na * _softplus(xs), 0.0), axis=0, keepdims=True),
                               jnp.sum(jnp.where(head, da, 0.0), axis=0, keepdims=True), jnp.zeros((6, 128), F32)],
                              axis=0)

        @pl.when(i == 0)
        def _():
            st_ref[...] = upd

        @pl.when(i > 0)
        def _():
            st_ref[...] += upd

    full = lambda s: pl.BlockSpec(s, lambda i: (0, 0))
    blk = pl.BlockSpec((tm, 512), lambda i: (i, 0))
    return pl.pallas_call(
        body, name=name, grid=(T // tm,),
        in_specs=[pl.BlockSpec(memory_space=pl.ANY), cur, prev, ab, blk, blk, blk,
                  pl.BlockSpec((tm, 128), lambda i: (i, 0)), pl.BlockSpec((tm, 256), lambda i: (i, 0)),
                  full((DN_K, QKV_C)), full((1, 128)), full((1, 128))],
        out_specs=[pl.BlockSpec((tm, 384), lambda i: (i, C_KA // 384)),
                   pl.BlockSpec((tm, QKV_C), lambda i: (i, 0)), full((8, 128))],
        out_shape=[jax.ShapeDtypeStruct(dproj.shape, BF16), jax.ShapeDtypeStruct((T, QKV_C), F32),
                   jax.ShapeDtypeStruct((8, 128), F32)],
        scratch_shapes=[pltpu.VMEM((tm + HALO_C, QKV_C), F32)],
        input_output_aliases={0: 0},
        compiler_params=_cparams(("arbitrary",)),
    )(dproj, proj, proj, proj, dqd, dkd, dvd, dgb, dkv_a, sconv_w, alog_v, dtb_v)


def _gdn_prep_bwd2(dproj, proj, dpre, sconv_w, name):
    T = proj.shape[0]
    tm = min(512, T)
    nt = T // tm
    r = tm // HALO_C
    cur, prev, _ = _gdn_specs(T, tm)

    def body(dp_any, x_ref, xp_ref, d_ref, dn_ref, w_ref, dx_ref, gw_ref, extx_ref, extd_ref):
        i = pl.program_id(0)
        extx_ref[:HALO_C] = jnp.where(i > 0, xp_ref[...], 0.0)
        extx_ref[HALO_C:] = x_ref[...]
        d = d_ref[...]
        extd_ref[:tm] = d
        extd_ref[tm:] = jnp.where(i < nt - 1, dn_ref[...], 0.0)
        dx = jnp.zeros((tm, QKV_C), F32)
        rows = []
        for k in range(DN_K):
            dx = dx + w_ref[k:k + 1, :] * extd_ref[pl.ds(DN_K - 1 - k, tm), :]
            rows.append(jnp.sum(d * extx_ref[pl.ds(HALO_C - DN_K + 1 + k, tm), :], axis=0, keepdims=True))
        rows.append(jnp.zeros((8 - DN_K, QKV_C), F32))
        gw = jnp.concatenate(rows, axis=0)
        dx_ref[...] = dx.astype(BF16)

        @pl.when(i == 0)
        def _():
            gw_ref[...] = gw

        @pl.when(i > 0)
        def _():
            gw_ref[...] += gw

    full = lambda s: pl.BlockSpec(s, lambda i: (0, 0))
    return pl.pallas_call(
        body, name=name, grid=(nt,),
        in_specs=[pl.BlockSpec(memory_space=pl.ANY), cur, prev, pl.BlockSpec((tm, QKV_C), lambda i: (i, 0)),
                  pl.BlockSpec((HALO_C, QKV_C), lambda i: (jnp.minimum((i + 1) * r, T // HALO_C - 1), 0)),
                  full((DN_K, QKV_C))],
        out_specs=[cur, full((8, QKV_C))],
        out_shape=[jax.ShapeDtypeStruct(dproj.shape, BF16), jax.ShapeDtypeStruct((8, QKV_C), F32)],
        scratch_shapes=[pltpu.VMEM((tm + HALO_C, QKV_C), F32), pltpu.VMEM((tm + HALO_C, QKV_C), F32)],
        input_output_aliases={0: 0},
        compiler_params=_cparams(("arbitrary",)),
    )(dproj, proj, proj, dpre, dpre, sconv_w)


def _merge_fwd(x, proj, ya, yb, yc, wa, wb, wc, wo, gate, name):
    T = x.shape[0]
    tm = min(256, T)

    def body(x_ref, mg_ref, ya_ref, yb_ref, yc_ref, wa_ref, wb_ref, wc_ref, wo_ref, gate_ref, o_ref):
        merged = (_sig(mg_ref[:, :D]) * _bdot(ya_ref[...], wa_ref[...])
                  + _sig(mg_ref[:, D:2 * D]) * _bdot(yb_ref[...], wb_ref[...])
                  + _sig(mg_ref[:, 2 * D:]) * _bdot(yc_ref[...], wc_ref[...]))
        o_ref[...] = x_ref[...] + gate_ref[...] * _bdot(merged, wo_ref[...])

    full = lambda s: pl.BlockSpec(s, lambda i: (0, 0))
    yb_ = pl.BlockSpec((tm, 512), lambda i: (i, 0))
    return pl.pallas_call(
        body, name=name, grid=(T // tm,),
        in_specs=[pl.BlockSpec((tm, D), lambda i: (i, 0)), pl.BlockSpec((tm, 3 * D), lambda i: (i, 0)), yb_, yb_, yb_,
                  full((512, D)), full((512, D)), full((512, D)), full((D, D)), full((1, D))],
        out_specs=pl.BlockSpec((tm, D), lambda i: (i, 0)),
        out_shape=jax.ShapeDtypeStruct((T, D), F32),
        compiler_params=_cparams(("parallel",)),
    )(x, proj, ya, yb, yc, wa, wb, wc, wo, _row(gate))


def _merge_bwd(dout, proj, ya, yb, yc, wa, wb, wc, wo, gate, name):
    T = dout.shape[0]
    tm = min(256, T)
    nt = T // tm

    def body(do_ref, mg_ref, ya_ref, yb_ref, yc_ref, wa_ref, wb_ref, wc_ref, wo_ref, gate_ref,
             dmg_ref, dya_ref, dyb_ref, dyc_ref, gwa_hbm, gwb_hbm, gwc_hbm, gwo_hbm, gg_ref,
             gwa_ref, gwb_ref, gwc_ref, gwo_ref):
        i = pl.program_id(0)

        @pl.when(i == 0)
        def _():
            for r in (gwa_ref, gwb_ref, gwc_ref, gwo_ref, gg_ref):
                r[...] = jnp.zeros_like(r)

        ys = (ya_ref[...], yb_ref[...], yc_ref[...])
        ws = (wa_ref, wb_ref, wc_ref)
        gs = tuple(_sig(mg_ref[:, j * D:(j + 1) * D]) for j in range(3))
        ps = tuple(_bdot(ys[j], ws[j][...]) for j in range(3))
        merged = gs[0] * ps[0] + gs[1] * ps[1] + gs[2] * ps[2]
        do = do_ref[...]
        dmerged = _bdot(do * gate_ref[...], wo_ref[...], NT)
        gwo_ref[...] += _bdot(merged, do, TN)
        for j, (dy_ref, gw_ref) in enumerate(((dya_ref, gwa_ref), (dyb_ref, gwb_ref), (dyc_ref, gwc_ref))):
            dp = dmerged * gs[j]
            dmg_ref[:, j * D:(j + 1) * D] = (dmerged * ps[j] * gs[j] * (1.0 - gs[j])).astype(BF16)
            dy_ref[...] = _bdot(dp, ws[j][...], NT)
            gw_ref[...] += _bdot(ys[j], dp, TN)

        @pl.when(i == nt - 1)
        def _():
            m = gwo_ref[...]
            gg_ref[...] = jnp.sum(wo_ref[...].astype(F32) * m, axis=0, keepdims=True)
            gwo_ref[...] = m * gate_ref[...]
            for src, dst in ((gwa_ref, gwa_hbm), (gwb_ref, gwb_hbm), (gwc_ref, gwc_hbm), (gwo_ref, gwo_hbm)):
                pltpu.sync_copy(src, dst)

    full = lambda s: pl.BlockSpec(s, lambda i: (0, 0))
    yb_ = pl.BlockSpec((tm, 512), lambda i: (i, 0))
    anyspec = pl.BlockSpec(memory_space=pl.ANY)
    return pl.pallas_call(
        body, name=name, grid=(nt,),
        in_specs=[pl.BlockSpec((tm, D), lambda i: (i, 0)), pl.BlockSpec((tm, 3 * D), lambda i: (i, 0)), yb_, yb_, yb_,
                  full((512, D)), full((512, D)), full((512, D)), full((D, D)), full((1, D))],
        out_specs=[pl.BlockSpec((tm, 3 * D), lambda i: (i, 0)), yb_, yb_, yb_, anyspec, anyspec, anyspec, anyspec,
                   full((1, D))],
        out_shape=[jax.ShapeDtypeStruct((T, NP), BF16)] + [jax.ShapeDtypeStruct((T, 512), F32)] * 3
        + [jax.ShapeDtypeStruct((512, D), F32)] * 3 + [jax.ShapeDtypeStruct((D, D), F32), jax.ShapeDtypeStruct((1, D), F32)],
        scratch_shapes=[pltpu.VMEM((512, D), F32)] * 3 + [pltpu.VMEM((D, D), F32)],
        compiler_params=_cparams(("arbitrary",)),
    )(dout, proj, ya, yb, yc, wa, wb, wc, wo, _row(gate))


def _loss_head(y, tgt, name):
    T = y.shape[0]
    tm = min(512, T)

    def body(y_ref, t_ref, dy_ref, l_ref):
        i = pl.program_id(0)
        diff = y_ref[...] - t_ref[...]
        dy_ref[...] = diff * (1.0 / D)
        part = jnp.sum(diff * diff, axis=0, keepdims=True)

        @pl.when(i == 0)
        def _():
            l_ref[...] = part

        @pl.when(i > 0)
        def _():
            l_ref[...] += part

    blk = pl.BlockSpec((tm, D), lambda i: (i, 0))
    return pl.pallas_call(
        body, name=name, grid=(T // tm,), in_specs=[blk, blk],
        out_specs=[blk, pl.BlockSpec((1, D), lambda i: (0, 0))],
        out_shape=[jax.ShapeDtypeStruct((T, D), F32), jax.ShapeDtypeStruct((1, D), F32)],
        compiler_params=_cparams(("arbitrary",)),
    )(y, tgt)


def _ada_fwd(c_all, w_ada, b_my, name):
    def body(c_ref, w_ref, b_ref, o_ref):
        sc = _silu(c_ref[...])
        for l in range(DEPTH):
            o_ref[l] = _bdot(sc, w_ref[l]) + b_ref[l:l + 1, :]

    return pl.pallas_call(body, name=name, out_shape=jax.ShapeDtypeStruct((DEPTH, N_DEV, w_ada.shape[2]), F32),
                          compiler_params=_cparams())(c_all, w_ada, b_my)


def _ada_bwd(c_all, dmod_my, name):
    def body(c_ref, d_ref, o_ref):
        sc = _silu(c_ref[...])
        for l in range(DEPTH):
            o_ref[l] = _bdot(sc, d_ref[l], TN)

    return pl.pallas_call(body, name=name, out_shape=jax.ShapeDtypeStruct((DEPTH, D, dmod_my.shape[2]), F32),
                          compiler_params=_cparams())(c_all, dmod_my)


def _adam_math(w, g, m, v):
    m = ADAM_B1 * m + (1.0 - ADAM_B1) * g
    v = ADAM_B2 * v + (1.0 - ADAM_B2) * (g * g)
    m_hat = m / (1.0 - ADAM_B1 ** ADAM_STEP)
    v_hat = v / (1.0 - ADAM_B2 ** ADAM_STEP)
    return -ADAM_LR * (m_hat / (jnp.sqrt(v_hat) + ADAM_EPS) + ADAM_WD * w), m, v


def _row_tile(rows, cap):
    best = rows
    for t in range(8, min(rows, cap) + 1, 8):
        if rows % t == 0:
            best = t
    return best if best <= cap else rows


def _adamw(w, g, m, v, name):
    R, C = w.shape
    tr = _row_tile(R, 256)

    def body(w_ref, g_ref, m_ref, v_ref, d_ref, mo_ref, vo_ref):
        d_ref[...], mo_ref[...], vo_ref[...] = _adam_math(w_ref[...], g_ref[...], m_ref[...], v_ref[...])

    blk = pl.BlockSpec((tr, C), lambda i: (i, 0))
    return pl.pallas_call(body, name=name, grid=(R // tr,), in_specs=[blk] * 4, out_specs=[blk] * 3,
                          out_shape=[jax.ShapeDtypeStruct((R, C), F32)] * 3,
                          compiler_params=_cparams(("parallel",)))(w, g, m, v)


def _sum_adamw_many(parts, ws, ms, vs, name):
    n = len(ws)

    def body(*refs):
        ins, outs = refs[:4 * n], refs[4 * n:]
        for i in range(n):
            g = ins[i][0]
            for j in range(1, N_DEV):
                g = g + ins[i][j]
            d, m, v = _adam_math(ins[n + i][...], g, ins[2 * n + i][...], ins[3 * n + i][...])
            outs[i][...], outs[n + i][...], outs[2 * n + i][...], outs[3 * n + i][...] = g, d, m, v

    shapes = [jax.ShapeDtypeStruct(w.shape, F32) for w in ws]
    out = pl.pallas_call(body, name=name, out_shape=shapes * 4, compiler_params=_cparams())(*parts, *ws, *ms, *vs)
    return out[:n], out[n:2 * n], out[2 * n:3 * n], out[3 * n:]


def _sum_adamw(parts0, parts1, w, m, v, name):
    P, R, C = parts0.shape
    tr = _row_tile(R, 128)
    nt = R // tr

    def body(p0_ref, p1_ref, w_ref, m_ref, v_ref, g_ref, d_ref, mo_ref, vo_ref):
        def emit(p_ref):
            g = p_ref[0].astype(F32)
            for j in range(1, P):
                g = g + p_ref[j].astype(F32)
            g_ref[...] = g
            d_ref[...], mo_ref[...], vo_ref[...] = _adam_math(w_ref[...], g, m_ref[...], v_ref[...])

        @pl.when(pl.program_id(0) == 0)
        def _():
            emit(p0_ref)

        @pl.when(pl.program_id(0) == 1)
        def _():
            emit(p1_ref)

    blk = pl.BlockSpec((tr, C), lambda l, i: (l * nt + i, 0))
    return pl.pallas_call(
        body, name=name, grid=(DEPTH, nt),
        in_specs=[pl.BlockSpec((P, tr, C), lambda l, i: (0, i * (1 - l) + (nt - 1) * l, 0)),
                  pl.BlockSpec((P, tr, C), lambda l, i: (0, i * l, 0)), blk, blk, blk],
        out_specs=[blk] * 4, out_shape=[jax.ShapeDtypeStruct((DEPTH * R, C), F32)] * 4,
        compiler_params=_cparams(("arbitrary", "arbitrary")))(parts0, parts1, w, m, v)


def _pair_sum_many(core, bufs, recvs, name):
    n = len(bufs)

    def body(c_ref, *refs):
        for a_ref, b_ref, o_ref in zip(refs[:n], refs[n:2 * n], refs[2 * n:]):
            o_ref[...] = (a_ref[:, 0].astype(F32) + b_ref[...].astype(F32)).astype(BF16)

    mine = [pl.BlockSpec((4, 1) + b.shape[2:], lambda i, c: (0, c[0], 0, 0)) for b in bufs]
    whole = [pl.BlockSpec(r.shape, lambda i, c: (0, 0, 0)) for r in recvs]
    return pl.pallas_call(
        body, name=name,
        grid_spec=pltpu.PrefetchScalarGridSpec(num_scalar_prefetch=1, grid=(1,), in_specs=mine + whole,
                                               out_specs=whole),
        out_shape=[jax.ShapeDtypeStruct(r.shape, BF16) for r in recvs],
        compiler_params=_cparams(("arbitrary",)))(core, *bufs, *recvs)


def _pair_sum(core, buf, recv, name):
    _, _, R, C = buf.shape
    tr = _row_tile(R, 128)

    def body(c_ref, a_ref, b_ref, o_ref):
        o_ref[...] = (a_ref[:, 0].astype(F32) + b_ref[...].astype(F32)).astype(BF16)

    return pl.pallas_call(
        body, name=name,
        grid_spec=pltpu.PrefetchScalarGridSpec(
            num_scalar_prefetch=1, grid=(R // tr,),
            in_specs=[pl.BlockSpec((4, 1, tr, C), lambda i, c: (0, c[0], i, 0)),
                      pl.BlockSpec((4, tr, C), lambda i, c: (0, i, 0))],
            out_specs=pl.BlockSpec((4, tr, C), lambda i, c: (0, i, 0))),
        out_shape=jax.ShapeDtypeStruct((4, R, C), BF16),
        compiler_params=_cparams(("parallel",)))(core, buf, recv)


SHARD_IN = D_IN // N_DEV


def _w_in_pieces():
    out, p = [], 0
    for a, b in _PAD_FROM:
        for j in range(N_DEV):
            lo, hi = max(a, SHARD_IN * j), min(b, SHARD_IN * (j + 1))
            if lo < hi:
                out.append((j, lo - SHARD_IN * j, hi - SHARD_IN * j, p + lo - a))
        p += b - a
    return out


def _assemble_w_in(gw, name):
    tr = 256
    nt = D // tr

    def body(x_ref, o_ref):
        for j, s0, s1, d0 in _w_in_pieces():
            o_ref[:, d0:d0 + s1 - s0] = x_ref[j, :, s0:s1]
        o_ref[:, D_IN:] = jnp.zeros((tr, NP - D_IN), gw.dtype)

    return pl.pallas_call(
        body, name=name, grid=(nt,),
        in_specs=[pl.BlockSpec((N_DEV, tr, SHARD_IN), lambda i: (0, i, 0))],
        out_specs=pl.BlockSpec((tr, NP), lambda i: (i, 0)),
        out_shape=jax.ShapeDtypeStruct((D, NP), gw.dtype),
        compiler_params=_cparams(("parallel",)))(gw)


def _split_w_in_grad(g, name):
    tr = 256

    def body(g_ref, o_ref):
        for j, s0, s1, d0 in _w_in_pieces():
            o_ref[j, :, s0:s1] = g_ref[:, d0:d0 + s1 - s0].astype(BF16)

    return pl.pallas_call(
        body, name=name, grid=(D // tr,),
        in_specs=[pl.BlockSpec((tr, NP), lambda i: (i, 0))],
        out_specs=pl.BlockSpec((N_DEV, tr, SHARD_IN), lambda i: (0, i, 0)),
        out_shape=jax.ShapeDtypeStruct((N_DEV, D, SHARD_IN), BF16),
        compiler_params=_cparams(("parallel",)))(g)


def _mesh_pos():
    return lax.axis_index("x"), lax.axis_index("y"), lax.axis_index("c")


def _launch(copies, peers, bufs, out_structs, sems, name, collective_id):
    n = len(bufs)
    if collective_id is None:
        anyspec = pl.BlockSpec(memory_space=pl.ANY)
        return list(pl.pallas_call(
            lambda *refs: copies(refs[:n], refs[n:n + len(out_structs)], *refs[n + len(out_structs):]),
            name=name, in_specs=[anyspec] * n, out_specs=[anyspec] * len(out_structs), out_shape=list(out_structs),
            scratch_shapes=list(sems))(*bufs))
    ins = [jax.new_ref(b, memory_space=pltpu.MemorySpace.HBM) for b in bufs]
    outs = [jax.empty_ref(s, memory_space=pltpu.MemorySpace.HBM) for s in out_structs]

    @pl.kernel(mesh=plsc.ScalarSubcoreMesh(axis_name="sequencer", num_cores=1), name=name, scratch_types=tuple(sems),
               compiler_params=pltpu.CompilerParams(collective_id=collective_id))
    def on_sequencer(*sem_refs):
        barrier = pltpu.get_barrier_semaphore()
        targets = peers()
        for p in targets:
            pl.semaphore_signal(barrier, inc=1, device_id=p, device_id_type=pl.DeviceIdType.MESH)
        pl.semaphore_wait(barrier, len(targets))
        copies(ins, outs, *sem_refs)

    on_sequencer()
    return [r[...] for r in outs]


def _all_gather(blocks, name, collective_id=None):
    n = len(blocks)

    def peers():
        x, y, c = _mesh_pos()
        return [(x, y, 1 - c), (1 - x, y, c), (x, 1 - y, c), (1 - x, 1 - y, c)]

    def copies(ins, outs, send_sems, recv_sems, local_sems):
        x, y, c = _mesh_pos()
        me, sibling = (x, y, c), (x, y, 1 - c)
        chips = [(1 - x, y), (x, 1 - y), (1 - x, 1 - y)]
        idx = lambda p: 4 * p[0] + 2 * p[1] + p[2]

        def copy(a, k, block, to, src=None):
            dst = outs[a].at[idx(block)]
            return pltpu.make_async_remote_copy(
                src_ref=dst if src is None else src, dst_ref=dst, send_sem=send_sems.at[a, k],
                recv_sem=recv_sems.at[a, k], device_id=to, device_id_type=pl.DeviceIdType.MESH)

        mine = [pltpu.make_async_copy(ins[a], outs[a].at[idx(me)], local_sems.at[a]) for a in range(n)]
        for cp in mine:
            cp.start()
        first = []
        for a in range(n):
            first.append(copy(a, 0, me, sibling, src=ins[a]))
            first += [copy(a, 1 + j, me, (*chip, c), src=ins[a]) for j, chip in enumerate(chips)]
        for cp in first:
            cp.start()
        passed = []
        for j, chip in enumerate(chips):
            for a in range(n):
                copy(a, 1 + j, (*chip, c), me).wait_recv()
                cp = copy(a, 4 + j, (*chip, c), sibling)
                cp.start()
                passed.append(cp)
        for a in range(n):
            copy(a, 0, sibling, me).wait_recv()
            for j, chip in enumerate(chips):
                copy(a, 4 + j, (*chip, 1 - c), me).wait_recv()
        for cp in first + passed:
            cp.wait_send()
        for cp in mine:
            cp.wait()

    return _launch(copies, peers, blocks, [jax.ShapeDtypeStruct((N_DEV,) + b.shape, b.dtype) for b in blocks],
                   [pltpu.SemaphoreType.DMA((n, 7)), pltpu.SemaphoreType.DMA((n, 7)), pltpu.SemaphoreType.DMA((n,))],
                   name, collective_id)


def _exchange_core(bufs, name, collective_id=None):
    n = len(bufs)

    def peers():
        x, y, c = _mesh_pos()
        return [(x, y, 1 - c)]

    def copies(ins, outs, send_sems, recv_sems):
        x, y, c = _mesh_pos()
        started = []
        for a in range(n):
            for q in range(4):
                cp = pltpu.make_async_remote_copy(
                    src_ref=ins[a].at[q, 1 - c], dst_ref=outs[a].at[q], send_sem=send_sems.at[a, q],
                    recv_sem=recv_sems.at[a, q], device_id=(x, y, 1 - c), device_id_type=pl.DeviceIdType.MESH)
                cp.start()
                started.append(cp)
        for cp in started:
            cp.wait()

    return _launch(copies, peers, bufs, [jax.ShapeDtypeStruct((4,) + b.shape[2:], b.dtype) for b in bufs],
                   [pltpu.SemaphoreType.DMA((n, 4)), pltpu.SemaphoreType.DMA((n, 4))], name, collective_id)


def _exchange_chips(bufs, name, collective_id=None):
    n = len(bufs)

    def peers():
        x, y, c = _mesh_pos()
        return [(1 - x, y, c), (x, 1 - y, c), (1 - x, 1 - y, c)]

    def copies(ins, outs, send_sems, recv_sems, local_sems):
        x, y, c = _mesh_pos()
        chip = 2 * x + y
        local = [pltpu.make_async_copy(ins[a].at[chip], outs[a].at[chip], local_sems.at[a]) for a in range(n)]
        for cp in local:
            cp.start()
        started = []
        for k in range(1, 4):
            px = 1 - x if k & 2 else x
            py = 1 - y if k & 1 else y
            for a in range(n):
                cp = pltpu.make_async_remote_copy(
                    src_ref=ins[a].at[2 * px + py], dst_ref=outs[a].at[chip], send_sem=send_sems.at[a, k - 1],
                    recv_sem=recv_sems.at[a, k - 1], device_id=(px, py, c), device_id_type=pl.DeviceIdType.MESH)
                cp.start()
                started.append(cp)
        for cp in started:
            cp.wait()
        for cp in local:
            cp.wait()

    return _launch(copies, peers, bufs, [jax.ShapeDtypeStruct(b.shape, b.dtype) for b in bufs],
                   [pltpu.SemaphoreType.DMA((n, 3)), pltpu.SemaphoreType.DMA((n, 3)), pltpu.SemaphoreType.DMA((n,))],
                   name, collective_id)


_SMALL = ("b_ada", "norm_g", "q_norm_g", "k_norm_g", "sinks", "dw_b", "ln_g", "ln_b", "pw2_b", "a_log", "dt_bias",
          "dn_norm_g", "dw_w", "sconv_w")


def _lane4(v):
    return jnp.pad(v, (0, 124)).reshape(1, 128)


def kernel(x, c, w_ada, b_ada, norm_g, w_in, q_norm_g, k_norm_g, sinks, dw_w, dw_b, ln_g, ln_b, pw2_w, pw2_b, sconv_w, a_log, dt_bias, dn_norm_g, w_proj_a, w_proj_b, w_proj_c, w_out, loss_target, m_w_ada, m_b_ada, m_norm_g, m_w_in, m_q_norm_g, m_k_norm_g, m_sinks, m_dw_w, m_dw_b, m_ln_g, m_ln_b, m_pw2_w, m_pw2_b, m_sconv_w, m_a_log, m_dt_bias, m_dn_norm_g, m_w_proj_a, m_w_proj_b, m_w_proj_c, m_w_out, v_w_ada, v_b_ada, v_norm_g, v_w_in, v_q_norm_g, v_k_norm_g, v_sinks, v_dw_w, v_dw_b, v_ln_g, v_ln_b, v_pw2_w, v_pw2_b, v_sconv_w, v_a_log, v_dt_bias, v_dn_norm_g, v_w_proj_a, v_w_proj_b, v_w_proj_c, v_w_out):
    T = x.shape[1]
    nc = T // CHUNK
    xi, yi, ci = _mesh_pos()
    me = 4 * xi + 2 * yi + ci
    big_w = (w_in, pw2_w, w_proj_a, w_proj_b, w_proj_c, w_out)
    big_m = (m_w_in, m_pw2_w, m_w_proj_a, m_w_proj_b, m_w_proj_c, m_w_out)
    big_v = (v_w_in, v_pw2_w, v_w_proj_a, v_w_proj_b, v_w_proj_c, v_w_out)

    ada_cols = w_ada.shape[2]
    dw_cols, sc_cols = dw_w.shape[2], sconv_w.shape[2]
    flat2 = lambda a: a.reshape(-1, a.shape[-1])
    big16 = [[a[l].astype(BF16) for l in range(DEPTH)] for a in big_w]
    c_all, gdw, gsc = _all_gather([c, dw_w, sconv_w], "gather_small", collective_id=0)
    (gw_in0,) = _all_gather([big16[0][0]], "gather_w_in0", collective_id=7)
    gw_in0, _ = lax.optimization_barrier((gw_in0, (flat2(m_w_in), flat2(v_w_in))))
    c_all = c_all.reshape(N_DEV, D)
    dw_f = gdw.transpose(1, 2, 0, 3).reshape(DEPTH, CONV_K, 512)
    sc_f = gsc.transpose(1, 2, 0, 3).reshape(DEPTH, DN_K, QKV_C)

    b_my = lax.dynamic_slice(b_ada, (0, me * ada_cols), (DEPTH, ada_cols))
    mod_part = _ada_fwd(c_all, w_ada, b_my, "ada_fwd")
    (gmod,) = _all_gather([mod_part.reshape(-1, 128)], "gather_mod")

    rest0 = [a[0] for a in big16[1:]]
    all1 = [a[1] for a in big16]
    (rest0, all1), gmod = lax.optimization_barrier(((rest0, all1), gmod))
    got0 = [gw_in0] + _all_gather(rest0, "gather_rest0", collective_id=1)
    got1 = _all_gather(all1, "gather_weights1", collective_id=6)
    wp, pw2_f, wa_f, wb_f, wc_f, wo_f = [], [], [], [], [], []
    for l, (gw_in, gpw2, gpa, gpb, gpc, gwo) in enumerate((got0, got1)):
        wp.append(_assemble_w_in(gw_in, f"assemble_w_in{l}"))
        pw2_f.append(gpw2.reshape(512, 512))
        for dst, g in ((wa_f, gpa), (wb_f, gpb), (wc_f, gpc)):
            dst.append(g.transpose(1, 0, 2).reshape(512, D))
        wo_f.append(gwo.reshape(D, D))
    mod_all = gmod.reshape(N_DEV, DEPTH, N_DEV, ada_cols).transpose(1, 2, 0, 3).reshape(DEPTH, N_DEV, 3 * D)
    mod = lax.dynamic_index_in_dim(mod_all, me, axis=1, keepdims=False)
    shift, scale, gate = mod[:, :D], mod[:, D:2 * D], mod[:, 2 * D:]

    xs, saved = [x[0]], []
    for l in range(DEPTH):
        xl = xs[-1]
        h = _norm_fwd(xl, norm_g[l], scale[l], shift[l], f"norm_fwd{l}")
        proj = _mm(h, wp[l], tm=min(2048, T), tn=1152, tk=D, name=f"in_proj{l}")
        ya = _attn_fwd(proj, q_norm_g[l], k_norm_g[l], sinks[l], f"attn_fwd{l}")
        yb, cv = _conf_fwd(proj, dw_f[l], dw_b[l], ln_g[l], ln_b[l], pw2_f[l], pw2_b[l], f"conf_fwd{l}")
        alv, dtv, dng = _lane4(a_log[l]), _lane4(dt_bias[l]), _row(dn_norm_g[l])
        qd, kd, vd, gb = _gdn_prep_fwd(proj, sc_f[l], alv, dtv, f"gdn_prep_fwd{l}")
        gbt = gb[:, :8].reshape(nc, CHUNK, 8).transpose(0, 2, 1)
        u, w, qe, ke, pm, tmi, eg = _gdn_chunk_fwd(qd, kd, vd, gb, gbt, f"gdn_chunk_fwd{l}")
        yc, o, vn, ss = _gdn_scan_fwd(u, w, qe, ke, pm, eg, proj, dng, f"gdn_scan_fwd{l}")
        xs.append(_merge_fwd(xl, proj, ya, yb, yc, wa_f[l], wb_f[l], wc_f[l], wo_f[l], gate[l], f"merge_fwd{l}"))
        saved.append((h, proj, ya, yb, yc, qd, kd, vd, gb, gbt, ss, alv, dtv, dng, w, qe, ke, pm, tmi, eg, o, vn, cv))

    dout, lsum = _loss_head(xs[-1], loss_target[0], "loss_head")

    small = {name: [None] * DEPTH for name in _SMALL}
    big_parts = [None] * DEPTH
    core = jnp.reshape(ci, (1,)).astype(jnp.int32)
    for l in reversed(range(DEPTH)):
        h, proj, ya, yb, yc, qd, kd, vd, gb, gbt, ss, alv, dtv, dng, w, qe, ke, pm, tmi, eg, o, vn, cv = saved[l]
        dproj, dya, dyb, dyc, g_wa, g_wb, g_wc, g_wo, g_gate = _merge_bwd(
            dout, proj, ya, yb, yc, wa_f[l], wb_f[l], wc_f[l], wo_f[l], gate[l], f"merge_bwd{l}")
        dproj, dkv_a, g_q, g_k, g_s = _attn_bwd(dproj, proj, dya, q_norm_g[l], k_norm_g[l], sinks[l], f"attn_bwd{l}")
        dproj, du1, g_pw2, st_b = _conf_bwd1(dproj, proj, cv, dyb, ln_g[l], ln_b[l], pw2_f[l], pw2_b[l],
                                             f"conf_bwd_a{l}")
        dproj, g_dw = _conf_bwd2(dproj, proj, du1, dw_f[l], f"conf_bwd_b{l}")
        dproj, do, dvn, dsn, g_dn = _gdn_scan_bwd(dproj, w, qe, ke, pm, eg, o, proj, dyc, dng, f"gdn_scan_bwd{l}")
        dqd, dkd, dvd, dgb = _gdn_chunk_grad(qd, kd, vd, gb, gbt, tmi, ss, dsn, do, dvn, vn, f"gdn_chunk_bwd{l}")
        dproj, dpre, st_c = _gdn_prep_bwd1(dproj, proj, dqd, dkd, dvd, dgb, dkv_a, sc_f[l], alv, dtv,
                                           f"gdn_prep_bwd_a{l}")
        dproj, g_sc = _gdn_prep_bwd2(dproj, proj, dpre, sc_f[l], f"gdn_prep_bwd_b{l}")
        g_wp = _mm(h, dproj, ta=True, tm=D, tn=1152, tk=min(2048, T), name=f"d_w_in{l}")
        by_dest = [_split_w_in_grad(g_wp, f"split_w_in_grad{l}"), g_pw2.reshape(N_DEV, -1, 512).astype(BF16)]
        by_dest += [g.reshape(512, N_DEV, -1).transpose(1, 0, 2).astype(BF16) for g in (g_wa, g_wb, g_wc)]
        by_dest.append(g_wo.reshape(N_DEV, -1, D).astype(BF16))
        by_dest = [b.reshape(4, 2, -1, b.shape[-1]) for b in by_dest]
        if l < DEPTH - 1:
            by_dest, big_parts[l + 1] = lax.optimization_barrier((by_dest, big_parts[l + 1]))
        from_sibling = _exchange_core(by_dest, f"exchange_grads_core{l}", collective_id=2 + 2 * l)

        def input_grad(dproj, dout):
            dh = _mm(dproj, wp[l], tb=True, tm=min(1024, T), tn=D, tk=2688, name=f"d_h{l}")
            return _norm_bwd(dh, xs[l], dout, norm_g[l], scale[l], f"norm_bwd{l}")

        if l > 0:
            dout, st_n = input_grad(dproj, dout)
            from_sibling, dout = lax.optimization_barrier((from_sibling, dout))
        chip_sums = [_pair_sum(core, by_dest[0], from_sibling[0], f"pair_sum_w_in{l}")]
        chip_sums += _pair_sum_many(core, by_dest[1:], from_sibling[1:], f"pair_sum_rest{l}")
        big_parts[l] = _exchange_chips(chip_sums, f"exchange_grads_chips{l}", collective_id=3 + 2 * l)
        if l > 0:
            dout, chip_sums = lax.optimization_barrier((dout, chip_sums))
        else:
            dproj, chip_sums = lax.optimization_barrier((dproj, chip_sums))
            dout, st_n = input_grad(dproj, dout)
        for name, g in (("b_ada", jnp.concatenate([st_n[0], st_n[1], g_gate[0]])), ("norm_g", st_n[2]),
                        ("q_norm_g", g_q.reshape(ATT_HEADS, ATT_HD).sum(0)), ("k_norm_g", g_k.reshape(2, ATT_HD).sum(0)),
                        ("sinks", g_s[0]), ("dw_b", st_b[3]),
                        ("ln_g", st_b[1]), ("ln_b", st_b[2]), ("pw2_b", st_b[0]), ("a_log", st_c[0, :4]),
                        ("dt_bias", st_c[1, :4]), ("dn_norm_g", g_dn[0]), ("dw_w", g_dw[:CONV_K]),
                        ("sconv_w", g_sc[:DN_K])):
            small[name][l] = g
    grad_x = dout[None]

    big_parts, dout = lax.optimization_barrier((big_parts, dout))
    sum_big = lambda i: _sum_adamw(big_parts[0][i], big_parts[1][i], flat2(big_w[i]), flat2(big_m[i]),
                                   flat2(big_v[i]), f"sum_adamw{i}")
    res = [sum_big(0)]

    names = list(_SMALL)
    gathered = _all_gather([jnp.stack(small[n]) for n in names] + [lsum], "gather_small_grads")
    gparts = dict(zip(names, gathered))
    loss = 0.5 * jnp.sum(jnp.sum(gathered[-1], axis=(1, 2))) / D
    dmod_my = lax.dynamic_slice(gparts["b_ada"], (0, 0, me * ada_cols), (N_DEV, DEPTH, ada_cols)).transpose(1, 0, 2)
    g_w_ada = _ada_bwd(c_all, dmod_my, "ada_bwd")
    gparts["dw_w"] = lax.dynamic_slice(gparts["dw_w"], (0, 0, 0, me * dw_cols), (N_DEV, DEPTH, CONV_K, dw_cols))
    gparts["sconv_w"] = lax.dynamic_slice(gparts["sconv_w"], (0, 0, 0, me * sc_cols), (N_DEV, DEPTH, DN_K, sc_cols))
    env = dict(b_ada=(b_ada, m_b_ada, v_b_ada), norm_g=(norm_g, m_norm_g, v_norm_g),
               q_norm_g=(q_norm_g, m_q_norm_g, v_q_norm_g), k_norm_g=(k_norm_g, m_k_norm_g, v_k_norm_g),
               sinks=(sinks, m_sinks, v_sinks), dw_b=(dw_b, m_dw_b, v_dw_b), ln_g=(ln_g, m_ln_g, v_ln_g),
               ln_b=(ln_b, m_ln_b, v_ln_b), pw2_b=(pw2_b, m_pw2_b, v_pw2_b), a_log=(a_log, m_a_log, v_a_log),
               dt_bias=(dt_bias, m_dt_bias, v_dt_bias), dn_norm_g=(dn_norm_g, m_dn_norm_g, v_dn_norm_g),
               dw_w=(dw_w, m_dw_w, v_dw_w), sconv_w=(sconv_w, m_sconv_w, v_sconv_w))
    upd = _sum_adamw_many([gparts[n] for n in names], [env[n][0] for n in names], [env[n][1] for n in names],
                          [env[n][2] for n in names], "sum_adamw_small")

    d_ada, nm_ada, nv_ada = (u.reshape(w_ada.shape) for u in
                             _adamw(flat2(w_ada), flat2(g_w_ada), flat2(m_w_ada), flat2(v_w_ada), "adamw_w_ada"))

    g_small, d_small, m_small, v_small = (dict(zip(names, u)) for u in upd)
    res += [sum_big(i) for i in range(1, len(big_w))]
    g_big, d_big, m_big, v_big =([r[k].reshape(w.shape) for r, w in zip(res, big_w)] for k in range(4))

    order = ("w_ada", "b_ada", "norm_g", "w_in", "q_norm_g", "k_norm_g", "sinks", "dw_w", "dw_b", "ln_g", "ln_b",
             "pw2_w", "pw2_b", "sconv_w", "a_log", "dt_bias", "dn_norm_g", "w_proj_a", "w_proj_b", "w_proj_c", "w_out")
    big_names = ("w_in", "pw2_w", "w_proj_a", "w_proj_b", "w_proj_c", "w_out")

    def pick(kind):
        src_small = (g_small, d_small, m_small, v_small)[kind]
        src_big = (g_big, d_big, m_big, v_big)[kind]
        src_ada = (g_w_ada, d_ada, nm_ada, nv_ada)[kind]
        return [src_ada if n == "w_ada" else src_big[big_names.index(n)] if n in big_names else src_small[n]
                for n in order]

    return (loss, grad_x, *pick(0), *pick(1), *pick(2), *pick(3))
```

```python
import functools
import math

import jax
import jax.numpy as jnp
import numpy as np
from jax import lax
from jax.experimental import pallas as pl
from jax.experimental.pallas import tpu as pltpu
from jax.experimental.pallas import tpu_sc as plsc

F32 = jnp.float32
BF16 = jnp.bfloat16
HI = lax.Precision.HIGHEST

N_DEV = 8
D = 1024
DEPTH = 2
EPS = 1e-6
NEG_INF = -1e30
WINDOW = 128
ATT_HEADS = 8
ATT_HD = 64
CONV_K = 31
DN_HEADS = 4
DN_K = 4
CHUNK = 64
D_IN = 7944
VMEM_LIMIT = 56 * 1024 * 1024

C_MG, C_QA, C_ZA, C_ZB, C_QC, C_KC, C_VC, C_GV, C_GG, C_ZC, C_KA, C_VA, C_AB, NP = (
    0, 3072, 3584, 4096, 4608, 5120, 5632, 6144, 6656, 7168, 7680, 7808, 7936, 8064)
_PAD_FROM = ((4872, 7944), (0, 512), (768, 1280), (2304, 2816), (2816, 4352), (1280, 2304), (4360, 4872),
             (512, 768), (4352, 4360))

ALIBI = tuple(float(2.0 ** (-8.0 * (h + 1) / ATT_HEADS)) for h in range(ATT_HEADS))

ADAM_LR, ADAM_B1, ADAM_B2, ADAM_EPS, ADAM_WD, ADAM_STEP = 0.001, 0.9, 0.999, 1e-08, 0.01, 10


def _cparams(sem=None):
    return pltpu.CompilerParams(dimension_semantics=sem, vmem_limit_bytes=VMEM_LIMIT)


def _sig(x):
    return jax.nn.sigmoid(x)


def _silu(x):
    return x * _sig(x)


def _dsilu(x):
    s = _sig(x)
    return s * (1.0 + x * (1.0 - s))


def _dot(a, b, dims=((1,), (0,)), precision=None):
    return lax.dot_general(a, b, (dims, ((), ())), preferred_element_type=F32, precision=precision)


def _bdot(a, b, dims=((1,), (0,))):
    return _dot(a.astype(BF16), b.astype(BF16), dims)


NN, NT, TN = ((1,), (0,)), ((1,), (1,)), ((0,), (0,))


def _row(v):
    return v.reshape(1, -1)


def _mm(a, b, *, ta=False, tb=False, tm, tn, tk, name):
    M, K = (a.shape[1], a.shape[0]) if ta else a.shape
    N = b.shape[0] if tb else b.shape[1]
    assert M % tm == 0 and N % tn == 0 and K % tk == 0, (M, N, K, tm, tn, tk)
    nk = K // tk
    dims = ((0 if ta else 1,), (1 if tb else 0,))

    def body(a_ref, b_ref, o_ref):
        k = pl.program_id(2)
        part = _bdot(a_ref[...], b_ref[...], dims)

        @pl.when(k == 0)
        def _():
            o_ref[...] = part

        @pl.when(k > 0)
        def _():
            o_ref[...] += part

    a_spec = pl.BlockSpec((tk, tm), lambda i, j, k: (k, i)) if ta else pl.BlockSpec((tm, tk), lambda i, j, k: (i, k))
    b_spec = pl.BlockSpec((tn, tk), lambda i, j, k: (j, k)) if tb else pl.BlockSpec((tk, tn), lambda i, j, k: (k, j))
    return pl.pallas_call(
        body, name=name, grid=(M // tm, N // tn, nk),
        in_specs=[a_spec, b_spec], out_specs=pl.BlockSpec((tm, tn), lambda i, j, k: (i, j)),
        out_shape=jax.ShapeDtypeStruct((M, N), F32),
        compiler_params=_cparams(("parallel", "parallel", "arbitrary")),
    )(a, b)


def _norm_fwd(x, norm_g, scale, shift, name):
    T = x.shape[0]
    tm = min(512, T)

    def body(x_ref, g_ref, sc_ref, sh_ref, h_ref):
        xv = x_ref[...]
        r = lax.rsqrt(jnp.mean(xv * xv, axis=-1, keepdims=True) + EPS)
        h_ref[...] = ((xv * r) * g_ref[...] * (1.0 + sc_ref[...]) + sh_ref[...]).astype(BF16)

    vec = pl.BlockSpec((1, D), lambda i: (0, 0))
    return pl.pallas_call(
        body, name=name, grid=(T // tm,),
        in_specs=[pl.BlockSpec((tm, D), lambda i: (i, 0)), vec, vec, vec],
        out_specs=pl.BlockSpec((tm, D), lambda i: (i, 0)),
        out_shape=jax.ShapeDtypeStruct((T, D), BF16),
        compiler_params=_cparams(("parallel",)),
    )(x, _row(norm_g), _row(scale), _row(shift))


def _norm_bwd(dh, x, dres, norm_g, scale, name):
    T = x.shape[0]
    tm = min(512, T)

    def body(dh_ref, x_ref, dr_ref, g_ref, sc_ref, dx_ref, st_ref):
        i = pl.program_id(0)
        xv, dhv = x_ref[...], dh_ref[...]
        r = lax.rsqrt(jnp.mean(xv * xv, axis=-1, keepdims=True) + EPS)
        xh = xv * r
        g, s1 = g_ref[...], 1.0 + sc_ref[...]
        dxh = dhv * (g * s1)
        dx_ref[...] = dr_ref[...] + r * (dxh - xh * jnp.mean(dxh * xh, axis=-1, keepdims=True))
        dhx = dhv * xh
        upd = jnp.concatenate([jnp.sum(dhv, axis=0, keepdims=True), jnp.sum(dhx * g, axis=0, keepdims=True),
                               jnp.sum(dhx * s1, axis=0, keepdims=True), jnp.zeros((5, D), F32)], axis=0)

        @pl.when(i == 0)
        def _():
            st_ref[...] = upd

        @pl.when(i > 0)
        def _():
            st_ref[...] += upd

    vec = pl.BlockSpec((1, D), lambda i: (0, 0))
    blk = pl.BlockSpec((tm, D), lambda i: (i, 0))
    return pl.pallas_call(
        body, name=name, grid=(T // tm,),
        in_specs=[blk, blk, blk, vec, vec],
        out_specs=[blk, pl.BlockSpec((8, D), lambda i: (0, 0))],
        out_shape=[jax.ShapeDtypeStruct((T, D), F32), jax.ShapeDtypeStruct((8, D), F32)],
        compiler_params=_cparams(("arbitrary",)),
    )(dh, x, dres, _row(norm_g), _row(scale))


def _rms(x, g):
    r = lax.rsqrt(jnp.mean(x * x, axis=-1, keepdims=True) + EPS)
    return x * r, r


def _head_mean_matrix():
    head = np.arange(ATT_HEADS * ATT_HD) // ATT_HD
    return jnp.asarray((head[:, None] == head[None, :]) * (1.0 / ATT_HD), BF16)


def _head_rms(x, hm):
    r = lax.rsqrt(_dot_exact(x * x, hm) + EPS)
    return x * r, r


def _head_rms_bwd(dy, xh, r, g, hm):
    dxh = dy * g
    return r * (dxh - xh * _dot_exact(dxh * xh, hm)), dy * xh


def _attn_mask(n):
    qi = lax.broadcasted_iota(jnp.int32, (WINDOW, 2 * WINDOW), 0)
    kj = lax.broadcasted_iota(jnp.int32, (WINDOW, 2 * WINDOW), 1)
    dist = qi + WINDOW - kj
    valid = (dist >= 0) & (dist < WINDOW) & ((n > 0) | (kj >= WINDOW))
    return valid, dist.astype(F32)


def _attn_probs(s, h, sink, valid, distf):
    s = s - ALIBI[h] * distf
    s = jnp.where(valid, s, NEG_INF)
    m = jnp.maximum(jnp.max(s, axis=-1, keepdims=True), sink)
    p = jnp.exp(s - m)
    es = jnp.exp(sink - m)
    den = jnp.sum(p, axis=-1, keepdims=True) + es
    return p / den, es / den


def _attn_fwd(proj, q_norm_g, k_norm_g, sinks, name):
    T = proj.shape[0]
    nb = T // WINDOW

    def body(sink_ref, q_ref, z_ref, kc_ref, kp_ref, vc_ref, vp_ref, qg_ref, kg_ref, hm_ref, o_ref):
        n = pl.program_id(0)
        valid, distf = _attn_mask(n)
        k2 = jnp.concatenate([kp_ref[...], kc_ref[...]], axis=0)
        v2 = jnp.concatenate([vp_ref[...], vc_ref[...]], axis=0).astype(BF16)
        kn = (_head_rms(k2, hm_ref[:128, :128])[0] * kg_ref[...]).astype(BF16)
        qn = ((_head_rms(q_ref[...], hm_ref[...])[0] * qg_ref[...]) * (ATT_HD ** -0.5)).astype(BF16)

        def head(h):
            sl, gsl = slice(64 * h, 64 * h + 64), slice(64 * (h // 4), 64 * (h // 4) + 64)
            s = _dot(qn[:, sl], kn[:, gsl], NT)
            yield
            p, _ = _attn_probs(s, h, sink_ref[h], valid, distf)
            o_ref[:, sl] = _dot(p.astype(BF16), v2[:, gsl])
            yield

        _lockstep([head(h) for h in range(ATT_HEADS)])
        o_ref[...] = o_ref[...] * _silu(z_ref[...])

    prev = lambda n: jnp.maximum(n - 1, 0)
    return pl.pallas_call(
        body, name=name, grid=(nb,),
        in_specs=[pl.BlockSpec(memory_space=pltpu.SMEM),
                  pl.BlockSpec((WINDOW, 512), lambda n: (n, C_QA // 512)),
                  pl.BlockSpec((WINDOW, 512), lambda n: (n, C_ZA // 512)),
                  pl.BlockSpec((WINDOW, 128), lambda n: (n, C_KA // 128)),
                  pl.BlockSpec((WINDOW, 128), lambda n: (prev(n), C_KA // 128)),
                  pl.BlockSpec((WINDOW, 128), lambda n: (n, C_VA // 128)),
                  pl.BlockSpec((WINDOW, 128), lambda n: (prev(n), C_VA // 128)),
                  pl.BlockSpec((1, 512), lambda n: (0, 0)), pl.BlockSpec((1, 128), lambda n: (0, 0)),
                  pl.BlockSpec((512, 512), lambda n: (0, 0))],
        out_specs=pl.BlockSpec((WINDOW, 512), lambda n: (n, 0)),
        out_shape=jax.ShapeDtypeStruct((T, 512), F32),
        compiler_params=_cparams(("parallel",)),
    )(sinks, proj, proj, proj, proj, proj, proj, _row(jnp.tile(q_norm_g, ATT_HEADS)), _row(jnp.tile(k_norm_g, 2)),
      _head_mean_matrix())


def _rms_bwd(dy, xh, r, g):
    dxh = dy * g
    return r * (dxh - xh * jnp.mean(dxh * xh, axis=-1, keepdims=True)), dy * xh


def _attn_bwd(dproj, proj, dya, q_norm_g, k_norm_g, sinks, name):
    T = proj.shape[0]
    nb = T // WINDOW

    def body(sink_ref, dp_any, q_ref, z_ref, kc_ref, kp_ref, vc_ref, vp_ref, dy_ref, qg_ref, kg_ref, hm_ref,
             dqz_ref, dkv_ref, gq_ref, gk_ref, gs_ref, ck_ref, cv_ref, o_sc, dq_sc):
        n = pl.program_id(0)

        @pl.when(n == 0)
        def _():
            gq_ref[...] = jnp.zeros_like(gq_ref)
            gk_ref[...] = jnp.zeros_like(gk_ref)
            gs_ref[...] = jnp.zeros_like(gs_ref)
            ck_ref[...] = jnp.zeros_like(ck_ref)
            cv_ref[...] = jnp.zeros_like(cv_ref)

        lane8 = lax.broadcasted_iota(jnp.int32, (1, 8), 1)

        @pl.when(n < nb)
        def _():
            valid, distf = _attn_mask(n)
            k2 = jnp.concatenate([kp_ref[...], kc_ref[...]], axis=0)
            v2 = jnp.concatenate([vp_ref[...], vc_ref[...]], axis=0).astype(BF16)
            kn = (_head_rms(k2, hm_ref[:128, :128])[0] * kg_ref[...]).astype(BF16)
            qh, qr = _head_rms(q_ref[...], hm_ref[...])
            qn = ((qh * qg_ref[...]) * (ATT_HD ** -0.5)).astype(BF16)
            zs = z_ref[...]
            do_all = dy_ref[...] * _silu(zs)
            dob_all = do_all.astype(BF16)

            def head(h):
                sl, gsl = slice(64 * h, 64 * h + 64), slice(64 * (h // 4), 64 * (h // 4) + 64)
                s = _dot(qn[:, sl], kn[:, gsl], NT)
                dpm = _dot(dob_all[:, sl], v2[:, gsl], NT)
                yield
                p, ps = _attn_probs(s, h, sink_ref[h], valid, distf)
                pb = p.astype(BF16)
                o_sc[:, sl] = _dot(pb, v2[:, gsl])
                dvg = _dot(pb, dob_all[:, sl], TN)
                delta = jnp.sum(p * dpm, axis=-1, keepdims=True)
                ds = (p * (dpm - delta)).astype(BF16)
                gs = jnp.where(lane8 == h, -jnp.sum(ps * delta, axis=0, keepdims=True), 0.0)
                yield
                dkn = _dot(ds, qn[:, sl], TN)
                dq_sc[:, sl] = _dot(ds, kn[:, gsl])
                yield
                return dkn, dvg, gs

            res = _lockstep([head(h) for h in range(ATT_HEADS)])
            dqz_ref[:, 512:] = (dy_ref[...] * o_sc[...] * _dsilu(zs)).astype(BF16)
            dq, gq = _head_rms_bwd(dq_sc[...] * (ATT_HD ** -0.5), qh, qr, qg_ref[...], hm_ref[...])
            dqz_ref[:, :512] = dq.astype(BF16)
            gq_acc = jnp.sum(gq, axis=0, keepdims=True)
            gs_acc = sum(r[2] for r in res[1:]) + res[0][2]
            for g in range(2):
                dkn = (res[4 * g][0] + res[4 * g + 1][0]) + (res[4 * g + 2][0] + res[4 * g + 3][0])
                dvg = (res[4 * g][1] + res[4 * g + 1][1]) + (res[4 * g + 2][1] + res[4 * g + 3][1])
                ksl = slice(64 * g, 64 * g + 64)
                vsl = slice(128 + 64 * g, 128 + 64 * g + 64)
                dkv_ref[:, ksl] = ck_ref[:, ksl] + dkn[:WINDOW]
                dkv_ref[:, vsl] = cv_ref[:, ksl] + dvg[:WINDOW]
                ck_ref[:, ksl] = dkn[WINDOW:]
                cv_ref[:, ksl] = dvg[WINDOW:]
            gq_ref[...] += gq_acc
            gs_ref[...] += gs_acc

        @pl.when(n == nb)
        def _():
            dkv_ref[:, :128] = ck_ref[...]
            dkv_ref[:, 128:] = cv_ref[...]

        @pl.when(n > 0)
        def _():
            hm = hm_ref[:128, :128]
            kh, kr = _head_rms(kp_ref[...], hm)
            dk, gk = _head_rms_bwd(dkv_ref[:, :128], kh, kr, kg_ref[...], hm)
            dkv_ref[:, :128] = dk
            gk_ref[...] += jnp.sum(gk, axis=0, keepdims=True)

    cur = lambda n: jnp.minimum(n, nb - 1)
    prev = lambda n: jnp.maximum(n - 1, 0)
    small = lambda w: pl.BlockSpec((1, w), lambda n: (0, 0))
    return pl.pallas_call(
        body, name=name, grid=(nb + 1,),
        in_specs=[pl.BlockSpec(memory_space=pltpu.SMEM), pl.BlockSpec(memory_space=pl.ANY),
                  pl.BlockSpec((WINDOW, 512), lambda n: (cur(n), C_QA // 512)),
                  pl.BlockSpec((WINDOW, 512), lambda n: (cur(n), C_ZA // 512)),
                  pl.BlockSpec((WINDOW, 128), lambda n: (cur(n), C_KA // 128)),
                  pl.BlockSpec((WINDOW, 128), lambda n: (prev(n), C_KA // 128)),
                  pl.BlockSpec((WINDOW, 128), lambda n: (cur(n), C_VA // 128)),
                  pl.BlockSpec((WINDOW, 128), lambda n: (prev(n), C_VA // 128)),
                  pl.BlockSpec((WINDOW, 512), lambda n: (cur(n), 0)),
                  small(512), small(128), pl.BlockSpec((512, 512), lambda n: (0, 0))],
        out_specs=[pl.BlockSpec((WINDOW, 1024), lambda n: (cur(n), C_QA // 1024)),
                   pl.BlockSpec((WINDOW, 256), lambda n: (prev(n), 0)),
                   small(512), small(128), small(8)],
        out_shape=[jax.ShapeDtypeStruct(dproj.shape, BF16), jax.ShapeDtypeStruct((T, 256), F32),
                   jax.ShapeDtypeStruct((1, 512), F32), jax.ShapeDtypeStruct((1, 128), F32),
                   jax.ShapeDtypeStruct((1, 8), F32)],
        scratch_shapes=[pltpu.VMEM((WINDOW, 128), F32), pltpu.VMEM((WINDOW, 128), F32),
                        pltpu.VMEM((WINDOW, 512), F32), pltpu.VMEM((WINDOW, 512), F32)],
        input_output_aliases={1: 0},
        compiler_params=_cparams(("arbitrary",)),
    )(sinks, dproj, proj, proj, proj, proj, proj, proj, dya, _row(jnp.tile(q_norm_g, ATT_HEADS)),
      _row(jnp.tile(k_norm_g, 2)), _head_mean_matrix())


HALO_B = 32


def _conf_specs(T, tm):
    r = tm // HALO_B
    cur = lambda c: pl.BlockSpec((tm, 512), lambda i: (i, c // 512))
    prev = lambda c: pl.BlockSpec((HALO_B, 512), lambda i: (jnp.maximum(i * r - 1, 0), c // 512))
    return cur, prev


SUB = 8
ROW_CHUNK = 64


def _shifted_copies(ext_ref, sh_ref):
    total = ext_ref.shape[0]
    for r in range(SUB):
        rows = total if r == 0 else total - SUB
        sh_ref[r, :rows, :] = ext_ref[pl.ds(r, rows), :]


def _taps_by_shift(offsets):
    groups = {}
    for k, o in enumerate(offsets):
        q, r = divmod(o, SUB)
        groups.setdefault(r, []).append((k, q))
    return groups


def _conv_taps(sh_ref, w_ref, offsets, out_ref, init):
    groups = _taps_by_shift(offsets)

    def chunk(ci, carry):
        r0 = pl.multiple_of(ci * ROW_CHUNK, ROW_CHUNK)
        acc = jnp.zeros((ROW_CHUNK, out_ref.shape[1]), F32) + init
        for r, taps in groups.items():
            win = sh_ref[r, pl.ds(r0, ROW_CHUNK + SUB * max(q for _, q in taps)), :]
            for k, q in taps:
                acc = acc + w_ref[k:k + 1, :] * win[SUB * q:SUB * q + ROW_CHUNK]
        out_ref[pl.ds(r0, ROW_CHUNK), :] = acc
        return carry

    lax.fori_loop(0, out_ref.shape[0] // ROW_CHUNK, chunk, 0)


def _conv_weight_grad(sh_ref, d_ref, offsets):
    tm, width = d_ref.shape
    out = [None] * len(offsets)
    for r, taps in _taps_by_shift(offsets).items():
        def chunk(ci, accs, r=r, taps=taps):
            r0 = pl.multiple_of(ci * ROW_CHUNK, ROW_CHUNK)
            d = d_ref[pl.ds(r0, ROW_CHUNK), :]
            win = sh_ref[r, pl.ds(r0, ROW_CHUNK + SUB * max(q for _, q in taps)), :]
            return tuple(a + jnp.sum((d * win[SUB * q:SUB * q + ROW_CHUNK]).reshape(ROW_CHUNK // SUB, SUB, width),
                                     axis=0) for a, (_, q) in zip(accs, taps))

        accs = lax.fori_loop(0, tm // ROW_CHUNK, chunk, tuple(jnp.zeros((SUB, width), F32) for _ in taps))
        for a, (k, _) in zip(accs, taps):
            out[k] = jnp.sum(a, axis=0, keepdims=True)
    return out


def _conf_scratch(tm):
    return [pltpu.VMEM((tm + HALO_B, 512), F32), pltpu.VMEM((SUB, tm + HALO_B, 512), F32), pltpu.VMEM((tm, 512), F32)]


def _conf_core(i, tm, gv_ref, gg_ref, gvp_ref, ggp_ref, w_ref, b_ref, lg_ref, lb_ref, pw_ref, pb_ref, ext_ref, sh_ref,
               cv_ref):
    up = gvp_ref[...] * _sig(ggp_ref[...])
    ext_ref[:HALO_B] = jnp.where(i > 0, up, 0.0)
    ext_ref[HALO_B:] = gv_ref[...] * _sig(gg_ref[...])
    _shifted_copies(ext_ref, sh_ref)
    _conv_taps(sh_ref, w_ref, [HALO_B - CONV_K + 1 + k for k in range(CONV_K)], cv_ref, b_ref[...])
    return _conf_post(cv_ref[...], lg_ref, lb_ref, pw_ref, pb_ref)


def _conf_post(acc, lg_ref, lb_ref, pw_ref, pb_ref):
    mu = jnp.mean(acc, axis=-1, keepdims=True)
    xc = acc - mu
    rstd = lax.rsqrt(jnp.mean(xc * xc, axis=-1, keepdims=True) + EPS)
    xh = xc * rstd
    u2 = xh * lg_ref[...] + lb_ref[...]
    u3 = _silu(u2)
    ypre = _bdot(u3, pw_ref[...]) + pb_ref[...]
    return xh, rstd, u2, u3, ypre


def _conf_fwd(proj, dw_w, dw_b, ln_g, ln_b, pw2, pw2_b, name):
    T = proj.shape[0]
    tm = min(512, T)
    cur, prev = _conf_specs(T, tm)

    def body(gv_ref, gg_ref, gvp_ref, ggp_ref, zb_ref, w_ref, b_ref, lg_ref, lb_ref, pw_ref, pb_ref, o_ref, cv_ref,
             ext_ref, sh_ref):
        i = pl.program_id(0)
        ypre = _conf_core(i, tm, gv_ref, gg_ref, gvp_ref, ggp_ref, w_ref, b_ref, lg_ref, lb_ref, pw_ref, pb_ref,
                          ext_ref, sh_ref, cv_ref)[4]
        o_ref[...] = ypre * _silu(zb_ref[...])

    full = lambda s: pl.BlockSpec(s, lambda i: (0, 0))
    blk = pl.BlockSpec((tm, 512), lambda i: (i, 0))
    return pl.pallas_call(
        body, name=name, grid=(T // tm,),
        in_specs=[cur(C_GV), cur(C_GG), prev(C_GV), prev(C_GG), cur(C_ZB), full((CONV_K, 512)), full((1, 512)),
                  full((1, 512)), full((1, 512)), full((512, 512)), full((1, 512))],
        out_specs=[blk, blk],
        out_shape=[jax.ShapeDtypeStruct((T, 512), F32)] * 2,
        scratch_shapes=_conf_scratch(tm)[:2],
        compiler_params=_cparams(("parallel",)),
    )(proj, proj, proj, proj, proj, dw_w, _row(dw_b), _row(ln_g), _row(ln_b), pw2, _row(pw2_b))


def _conf_bwd1(dproj, proj, cv, dyb, ln_g, ln_b, pw2, pw2_b, name):
    T = proj.shape[0]
    tm = min(512, T)
    cur, _ = _conf_specs(T, tm)

    def body(dp_any, zb_ref, cv_ref, dy_ref, lg_ref, lb_ref, pw_ref, pb_ref, dzb_ref, du1_ref, gpw_ref, st_ref):
        i = pl.program_id(0)
        xh, rstd, u2, u3, ypre = _conf_post(cv_ref[...], lg_ref, lb_ref, pw_ref, pb_ref)
        zb, dy = zb_ref[...], dy_ref[...]
        dzb_ref[...] = (dy * ypre * _dsilu(zb)).astype(BF16)
        dyp = dy * _silu(zb)
        du2 = _bdot(dyp, pw_ref[...], NT) * _dsilu(u2)
        dxh = du2 * lg_ref[...]
        du1 = rstd * (dxh - jnp.mean(dxh, axis=-1, keepdims=True) - xh * jnp.mean(dxh * xh, axis=-1, keepdims=True))
        du1_ref[...] = du1
        gpw = _bdot(u3, dyp, TN)
        rs = lambda a: jnp.sum(a, axis=0, keepdims=True)
        upd = jnp.concatenate([rs(dyp), rs(du2 * xh), rs(du2), rs(du1), jnp.zeros((4, 512), F32)], axis=0)

        @pl.when(i == 0)
        def _():
            gpw_ref[...] = gpw
            st_ref[...] = upd

        @pl.when(i > 0)
        def _():
            gpw_ref[...] += gpw
            st_ref[...] += upd

    full = lambda s: pl.BlockSpec(s, lambda i: (0, 0))
    blk = pl.BlockSpec((tm, 512), lambda i: (i, 0))
    return pl.pallas_call(
        body, name=name, grid=(T // tm,),
        in_specs=[pl.BlockSpec(memory_space=pl.ANY), cur(C_ZB), blk, blk,
                  full((1, 512)), full((1, 512)), full((512, 512)), full((1, 512))],
        out_specs=[cur(C_ZB), blk, full((512, 512)), full((8, 512))],
        out_shape=[jax.ShapeDtypeStruct(dproj.shape, BF16), jax.ShapeDtypeStruct((T, 512), F32),
                   jax.ShapeDtypeStruct((512, 512), F32), jax.ShapeDtypeStruct((8, 512), F32)],
        input_output_aliases={0: 0},
        compiler_params=_cparams(("arbitrary",)),
    )(dproj, proj, cv, dyb, _row(ln_g), _row(ln_b), pw2, _row(pw2_b))


def _conf_bwd2(dproj, proj, du1, dw_w, name):
    T = proj.shape[0]
    tm = min(512, T)
    nt = T // tm
    r = tm // HALO_B
    cur, prev = _conf_specs(T, tm)

    def body(dp_any, gv_ref, gg_ref, gvp_ref, ggp_ref, du_ref, dun_ref, w_ref, dglu_ref, gw_ref, ext_ref, sh_ref,
             cv_ref):
        i = pl.program_id(0)
        gv, sg = gv_ref[...], _sig(gg_ref[...])
        ext_ref[:HALO_B] = jnp.where(i > 0, gvp_ref[...] * _sig(ggp_ref[...]), 0.0)
        ext_ref[HALO_B:] = gv * sg
        _shifted_copies(ext_ref, sh_ref)
        rows = _conv_weight_grad(sh_ref, du_ref, [HALO_B - CONV_K + 1 + k for k in range(CONV_K)])
        rows.append(jnp.zeros((1, 512), F32))
        gw = jnp.concatenate(rows, axis=0)
        ext_ref[:tm] = du_ref[...]
        ext_ref[tm:] = jnp.where(i < nt - 1, dun_ref[...], 0.0)
        _shifted_copies(ext_ref, sh_ref)
        _conv_taps(sh_ref, w_ref, [CONV_K - 1 - k for k in range(CONV_K)], cv_ref, 0.0)
        du0 = cv_ref[...]
        dglu_ref[:, :512] = (du0 * sg).astype(BF16)
        dglu_ref[:, 512:] = (du0 * gv * sg * (1.0 - sg)).astype(BF16)

        @pl.when(i == 0)
        def _():
            gw_ref[...] = gw

        @pl.when(i > 0)
        def _():
            gw_ref[...] += gw

    full = lambda s: pl.BlockSpec(s, lambda i: (0, 0))
    return pl.pallas_call(
        body, name=name, grid=(nt,),
        in_specs=[pl.BlockSpec(memory_space=pl.ANY), cur(C_GV), cur(C_GG), prev(C_GV), prev(C_GG),
                  pl.BlockSpec((tm, 512), lambda i: (i, 0)),
                  pl.BlockSpec((HALO_B, 512), lambda i: (jnp.minimum((i + 1) * r, T // HALO_B - 1), 0)),
                  full((CONV_K, 512))],
        out_specs=[pl.BlockSpec((tm, 1024), lambda i: (i, C_GV // 1024)), full((32, 512))],
        out_shape=[jax.ShapeDtypeStruct(dproj.shape, BF16), jax.ShapeDtypeStruct((32, 512), F32)],
        scratch_shapes=_conf_scratch(tm),
        input_output_aliases={0: 0},
        compiler_params=_cparams(("arbitrary",)),
    )(dproj, proj, proj, proj, proj, du1, du1, dw_w)


HALO_C = 8
QKV_C = 1536


def _softplus(x):
    return jnp.maximum(x, 0.0) + jnp.log1p(jnp.exp(-jnp.abs(x)))


def _gdn_conv(i, tm, x_ref, xp_ref, w_ref, ext_ref):
    ext_ref[:HALO_C] = jnp.where(i > 0, xp_ref[...], 0.0)
    ext_ref[HALO_C:] = x_ref[...]
    pre = jnp.zeros((tm, QKV_C), F32)
    for k in range(DN_K):
        pre = pre + w_ref[k:k + 1, :] * ext_ref[pl.ds(HALO_C - DN_K + 1 + k, tm), :]
    return pre


def _gdn_specs(T, tm):
    r = tm // HALO_C
    cur = pl.BlockSpec((tm, QKV_C), lambda i: (i, C_QC // QKV_C))
    prev = pl.BlockSpec((HALO_C, QKV_C), lambda i: (jnp.maximum(i * r - 1, 0), C_QC // QKV_C))
    ab = pl.BlockSpec((tm, 128), lambda i: (i, C_AB // 128))
    return cur, prev, ab


def _gdn_prep_fwd(proj, sconv_w, alog_v, dtb_v, name):
    T = proj.shape[0]
    tm = min(512, T)
    cur, prev, ab = _gdn_specs(T, tm)

    def body(x_ref, xp_ref, ab_ref, w_ref, al_ref, dt_ref, q_ref, k_ref, v_ref, gb_ref, ext_ref):
        i = pl.program_id(0)
        y = _silu(_gdn_conv(i, tm, x_ref, xp_ref, w_ref, ext_ref))
        for h in range(DN_HEADS):
            sl = slice(128 * h, 128 * h + 128)
            qh, kh = y[:, sl], y[:, 512 + 128 * h:512 + 128 * h + 128]
            q_ref[:, sl] = qh * lax.rsqrt(jnp.sum(qh * qh, axis=-1, keepdims=True) + EPS) * (128 ** -0.5)
            k_ref[:, sl] = kh * lax.rsqrt(jnp.sum(kh * kh, axis=-1, keepdims=True) + EPS)
        v_ref[...] = y[:, 1024:]
        abv = ab_ref[...]
        lane = lax.broadcasted_iota(jnp.int32, (tm, 128), 1)
        g = -jnp.exp(al_ref[...]) * _softplus(abv + dt_ref[...])
        gb_ref[...] = jnp.where(lane < DN_HEADS, g, _sig(abv))

    full = lambda s: pl.BlockSpec(s, lambda i: (0, 0))
    blk = pl.BlockSpec((tm, 512), lambda i: (i, 0))
    return pl.pallas_call(
        body, name=name, grid=(T // tm,),
        in_specs=[cur, prev, ab, full((DN_K, QKV_C)), full((1, 128)), full((1, 128))],
        out_specs=[blk, blk, blk, pl.BlockSpec((tm, 128), lambda i: (i, 0))],
        out_shape=[jax.ShapeDtypeStruct((T, 512), F32)] * 3 + [jax.ShapeDtypeStruct((T, 128), F32)],
        scratch_shapes=[pltpu.VMEM((tm + HALO_C, QKV_C), F32)],
        compiler_params=_cparams(("parallel",)),
    )(proj, proj, proj, sconv_w, alog_v, dtb_v)


def _hdot(a, b, dims=NN):
    return _dot(a, b, dims, precision=HI)


def _lockstep(gens):
    results, live = [None] * len(gens), list(range(len(gens)))
    while live:
        for i in list(live):
            try:
                next(gens[i])
            except StopIteration as stop:
                results[i] = stop.value
                live.remove(i)
    return results


def _dot_exact(a, b, dims=NN, split_left=True):
    x = (a if split_left else b).astype(F32)
    hi = x.astype(BF16)
    r = x - hi.astype(F32)
    mid = r.astype(BF16)
    lo = (r - mid.astype(F32)).astype(BF16)
    other = (b if split_left else a).astype(BF16)
    one = (lambda p: _dot(p, other, dims)) if split_left else (lambda p: _dot(other, p, dims))
    return (one(lo) + one(mid)) + one(hi)


def _split(a):
    hi = a.astype(BF16)
    return hi, (a - hi.astype(F32)).astype(BF16)


def _dot3(a, b):
    (ah, al), (bh, bl) = a, b
    return _dot(ah, bh) + (_dot(ah, bl) + _dot(al, bh))


INV_BLOCK = 16


def _tri_inv(mats, eye):
    ii = lax.broadcasted_iota(jnp.int32, (CHUNK, CHUNK), 0) // INV_BLOCK
    jj = lax.broadcasted_iota(jnp.int32, (CHUNK, CHUNK), 1) // INV_BLOCK
    diag = ii == jj
    mul = lambda xs, ys: [_dot3(_split(x), _split(y)) for x, y in zip(xs, ys)]
    ps = [jnp.where(diag, -a, 0.0) for a in mats]
    tds = [eye + p for p in ps]
    for _ in range(3):
        ps = mul(ps, ps)
        tds = [t + tp for t, tp in zip(tds, mul(tds, ps))]
    ms = mul(tds, [jnp.where(diag, 0.0, a) for a in mats])
    m2 = mul(ms, ms)
    inv = [(eye - m) + x for m, x in zip(ms, mul([eye - m for m in ms], m2))]
    return mul(inv, tds)


def _tri_consts():
    ii = lax.broadcasted_iota(jnp.int32, (CHUNK, CHUNK), 0)
    jj = lax.broadcasted_iota(jnp.int32, (CHUNK, CHUNK), 1)
    return ii >= jj, ii > jj, (ii == jj).astype(F32)


def _gdn_local(q, k, v, gcol, grow, bcol, lower, strict):
    dm = jnp.where(lower, jnp.exp(jnp.where(lower, gcol - grow, 0.0)), 0.0)
    kb = k * bcol
    a = jnp.where(strict, _bdot(kb, k, NT) * dm, 0.0)
    gc = jnp.exp(gcol)
    glast = grow[:, CHUNK - 1:CHUNK]
    return dict(q=q, k=k, v=v, bcol=bcol, gcol=gcol, glast=glast, dm=dm, kb=kb, a=a, gc=gc, vb=v * bcol,
                kbg=kb * gc, p=_bdot(q, k, NT) * dm, qe=q * gc, ke=k * jnp.exp(glast - gcol))


def _gdn_chunk_bwd(c, do, dvn, ds_new, lower, strict, ones):
    rs = lambda m: jnp.sum(m, axis=-1, keepdims=True)
    colsum = lambda m: _dot_exact(m, ones, TN)[:, :1]
    q, k, v, bcol, dm, tm, gc, s = c["q"], c["k"], c["v"], c["bcol"], c["dm"], c["tm"], c["gc"], c["s"]
    eg = jnp.exp(c["glast"])
    dqe = _bdot(do, s, NT)
    dp = jnp.where(lower, _bdot(do, c["vn"], NT), 0.0)
    dw = -_bdot(dvn, s, NT)
    dke = _bdot(c["vn"], ds_new, NT)
    dvb = _bdot(tm, dvn, TN)
    yield
    dglast = jnp.sum(rs(ds_new * s), axis=0, keepdims=True) * eg
    dk = dke * jnp.exp(c["glast"] - c["gcol"])
    r_ke = rs(dke * c["ke"])
    dglast = dglast + jnp.sum(r_ke, axis=0, keepdims=True)
    dgam = rs(dqe * c["qe"]) - r_ke
    dq = dqe * gc
    dpm = dp * dm
    mp = dp * c["p"]
    dq = dq + _bdot(dpm, k)
    dk = dk + _bdot(dpm, q, TN)
    dt = _bdot(dvn, c["vb"], NT) + _bdot(dw, c["kbg"], NT)
    dkbg = _bdot(tm, dw, TN)
    dgam = dgam + rs(mp) - colsum(mp)
    yield
    dkb = dkbg * gc
    dgam = dgam + rs(dkbg * c["kbg"])
    dat = _bdot(tm, dt, TN)
    yield
    da = jnp.where(strict, -_bdot(dat, tm, NT), 0.0)
    yield
    dam = da * dm
    ma = da * c["a"]
    dkb = dkb + _bdot(dam, k)
    dk = dk + _bdot(dam, c["kb"], TN)
    dgam = dgam + rs(ma) - colsum(ma)
    yield
    dk = dk + dkb * bcol
    dbeta = rs(dkb * k) + rs(dvb * v)
    dv = dvb * bcol
    row = lax.broadcasted_iota(jnp.int32, (CHUNK, 1), 0)
    dgam = dgam + jnp.where(row == CHUNK - 1, dglast, 0.0)
    dg = _dot_exact(lower, dgam, TN, split_left=False)
    return dq, dk, dv, dg, dbeta


SCAN_GROUP = 8
GROUP = 8


def _chunk_decay(gb_ref, gt_ref, lmat, g):
    rows = slice(CHUNK * g, CHUNK * g + CHUNK)
    return rows, _dot_exact(lmat, gb_ref[rows, :], split_left=False), _dot_exact(gt_ref[g], lmat, NT)


def _gdn_chunk_fwd(qd, kd, vd, gb, gbt, name):
    T = qd.shape[0]
    G = min(GROUP, T // CHUNK)
    ng = T // (CHUNK * G)

    def body(q_ref, k_ref, v_ref, gb_ref, gt_ref, u_ref, w_ref, qe_ref, ke_ref, p_ref, t_ref, eg_ref):
        lower, strict, eye = _tri_consts()
        lmat = lower.astype(F32)
        decay = [_chunk_decay(gb_ref, gt_ref, lmat, g) for g in range(G)]
        chains = [(g, h) for g in range(G) for h in range(DN_HEADS)]
        cs = []
        for g, h in chains:
            rows, gcs, grs = decay[g]
            sl = slice(128 * h, 128 * h + 128)
            c = _gdn_local(q_ref[rows, sl], k_ref[rows, sl], v_ref[rows, sl], gcs[:, h:h + 1], grs[h:h + 1, :],
                           gb_ref[rows, DN_HEADS + h:DN_HEADS + h + 1], lower, strict)
            qe_ref[rows, sl] = c["qe"].astype(BF16)
            ke_ref[rows, sl] = c["ke"].astype(BF16)
            p_ref[rows, 64 * h:64 * h + 64] = c["p"].astype(BF16)
            eg_ref[g, h:h + 1, :] = jnp.broadcast_to(jnp.exp(c["glast"]), (1, 128))
            cs.append(c)
        tms = [t.astype(BF16) for t in _tri_inv([c["a"] for c in cs], eye)]
        us = [_dot(t, c["vb"].astype(BF16)) for t, c in zip(tms, cs)]
        ws = [_dot(t, c["kbg"].astype(BF16)) for t, c in zip(tms, cs)]
        for (g, h), tm, u, w in zip(chains, tms, us, ws):
            rows, sl = decay[g][0], slice(128 * h, 128 * h + 128)
            u_ref[rows, sl] = u
            w_ref[rows, sl] = w.astype(BF16)
            t_ref[rows, 64 * h:64 * h + 64] = tm
        for g in range(G):
            eg_ref[g, DN_HEADS:, :] = jnp.zeros((8 - DN_HEADS, 128), F32)

    blk = pl.BlockSpec((CHUNK * G, 512), lambda n: (n, 0))
    half = pl.BlockSpec((CHUNK * G, 256), lambda n: (n, 0))
    return pl.pallas_call(
        body, name=name, grid=(ng,),
        in_specs=[blk, blk, blk, pl.BlockSpec((CHUNK * G, 128), lambda n: (n, 0)),
                  pl.BlockSpec((G, 8, CHUNK), lambda n: (n, 0, 0))],
        out_specs=[blk, blk, blk, blk, half, half, pl.BlockSpec((G, 8, 128), lambda n: (n, 0, 0))],
        out_shape=[jax.ShapeDtypeStruct((T, 512), F32)] + [jax.ShapeDtypeStruct((T, 512), BF16)] * 3
        + [jax.ShapeDtypeStruct((T, 256), BF16)] * 2 + [jax.ShapeDtypeStruct((T // CHUNK, 8, 128), F32)],
        compiler_params=_cparams(("parallel",)),
    )(qd, kd, vd, gb, gbt)


def _gdn_scan_fwd(u, w, qe, ke, pm, eg, proj, dn_g, name):
    T = u.shape[0]
    nc = T // CHUNK
    G = min(SCAN_GROUP, T // CHUNK)

    def body(u_ref, w_ref, qe_ref, ke_ref, p_ref, eg_ref, z_ref, ng_ref, y_ref, o_ref, vn_ref, ss_ref, s_ref):
        n = pl.program_id(0)

        @pl.when(n == 0)
        def _():
            s_ref[...] = jnp.zeros_like(s_ref)

        def head(j, h):
            rows, sl = slice(CHUNK * j, CHUNK * j + CHUNK), slice(128 * h, 128 * h + 128)
            s = s_ref[h]
            sb = s.astype(BF16)
            vn = u_ref[rows, sl] - _dot(w_ref[rows, sl], sb)
            qs = _dot(qe_ref[rows, sl], sb)
            yield
            vb = vn.astype(BF16)
            o = qs + _dot(p_ref[rows, 64 * h:64 * h + 64], vb)
            s_ref[h] = s * eg_ref[j, h:h + 1, :] + _dot(ke_ref[rows, sl], vb, TN)
            yield
            vn_ref[rows, sl] = vb
            o_ref[rows, sl] = o
            y_ref[rows, sl] = _rms(o, None)[0] * ng_ref[...] * _silu(z_ref[rows, sl])

        for j in range(G):
            ss_ref[j] = s_ref[...]
            _lockstep([head(j, h) for h in range(DN_HEADS)])

    blk = pl.BlockSpec((CHUNK * G, 512), lambda n: (n, 0))
    return pl.pallas_call(
        body, name=name, grid=(nc // G,),
        in_specs=[blk, blk, blk, blk, pl.BlockSpec((CHUNK * G, 256), lambda n: (n, 0)),
                  pl.BlockSpec((G, 8, 128), lambda n: (n, 0, 0)),
                  pl.BlockSpec((CHUNK * G, 512), lambda n: (n, C_ZC // 512)), pl.BlockSpec((1, 128), lambda n: (0, 0))],
        out_specs=[blk, blk, blk, pl.BlockSpec((G, DN_HEADS, 128, 128), lambda n: (n, 0, 0, 0))],
        out_shape=[jax.ShapeDtypeStruct((T, 512), F32), jax.ShapeDtypeStruct((T, 512), F32),
                   jax.ShapeDtypeStruct((T, 512), BF16), jax.ShapeDtypeStruct((nc, DN_HEADS, 128, 128), F32)],
        scratch_shapes=[pltpu.VMEM((DN_HEADS, 128, 128), F32)],
        compiler_params=_cparams(("arbitrary",)),
    )(u, w, qe, ke, pm, eg, proj, dn_g)


def _gdn_scan_bwd(dproj, w, qe, ke, pm, eg, o, proj, dyc, dn_g, name):
    T = o.shape[0]
    nc = T // CHUNK
    G = min(SCAN_GROUP, T // CHUNK)
    rev = lambda n: nc // G - 1 - n

    def body(dp_any, w_ref, qe_ref, ke_ref, p_ref, eg_ref, o_ref, z_ref, dy_ref, ng_ref,
             dz_ref, do_ref, dvn_ref, dsn_ref, gng_ref, ds_ref):
        n = pl.program_id(0)

        @pl.when(n == 0)
        def _():
            ds_ref[...] = jnp.zeros_like(ds_ref)
            gng_ref[...] = jnp.zeros_like(gng_ref)

        def head(j, h):
            rows, sl = slice(CHUNK * j, CHUNK * j + CHUNK), slice(128 * h, 128 * h + 128)
            oh, r = _rms(o_ref[rows, sl], None)
            z, dy = z_ref[rows, sl], dy_ref[rows, sl]
            dz_ref[rows, sl] = (dy * (oh * ng_ref[...]) * _dsilu(z)).astype(BF16)
            do, gg = _rms_bwd(dy * _silu(z), oh, r, ng_ref[...])
            dob = do.astype(BF16)
            ds = ds_ref[h]
            dvn = _dot(p_ref[rows, 64 * h:64 * h + 64], dob, TN) + _dot(ke_ref[rows, sl], ds.astype(BF16))
            qd = _dot(qe_ref[rows, sl], dob, TN)
            yield
            dvb = dvn.astype(BF16)
            ds_ref[h] = qd + eg_ref[j, h:h + 1, :] * ds - _dot(w_ref[rows, sl], dvb, TN)
            do_ref[rows, sl] = dob
            dvn_ref[rows, sl] = dvb
            return jnp.sum(gg, axis=0, keepdims=True)

        for j in reversed(range(G)):
            dsn_ref[j] = ds_ref[...]
            gng = _lockstep([head(j, h) for h in range(DN_HEADS)])
            gng_ref[...] += (gng[0] + gng[1]) + (gng[2] + gng[3])

    blk = pl.BlockSpec((CHUNK * G, 512), lambda n: (rev(n), 0))
    state = pl.BlockSpec((G, DN_HEADS, 128, 128), lambda n: (rev(n), 0, 0, 0))
    return pl.pallas_call(
        body, name=name, grid=(nc // G,),
        in_specs=[pl.BlockSpec(memory_space=pl.ANY), blk, blk, blk,
                  pl.BlockSpec((CHUNK * G, 256), lambda n: (rev(n), 0)),
                  pl.BlockSpec((G, 8, 128), lambda n: (rev(n), 0, 0)), blk,
                  pl.BlockSpec((CHUNK * G, 512), lambda n: (rev(n), C_ZC // 512)), blk,
                  pl.BlockSpec((1, 128), lambda n: (0, 0))],
        out_specs=[pl.BlockSpec((CHUNK * G, 512), lambda n: (rev(n), C_ZC // 512)), blk, blk, state,
                   pl.BlockSpec((1, 128), lambda n: (0, 0))],
        out_shape=[jax.ShapeDtypeStruct(dproj.shape, BF16), jax.ShapeDtypeStruct((T, 512), BF16),
                   jax.ShapeDtypeStruct((T, 512), BF16), jax.ShapeDtypeStruct((nc, DN_HEADS, 128, 128), F32),
                   jax.ShapeDtypeStruct((1, 128), F32)],
        scratch_shapes=[pltpu.VMEM((DN_HEADS, 128, 128), F32)],
        input_output_aliases={0: 0},
        compiler_params=_cparams(("arbitrary",)),
    )(dproj, w, qe, ke, pm, eg, o, proj, dyc, dn_g)


def _gdn_chunk_grad(qd, kd, vd, gb, gbt, tmi, ssave, dsn, do, dvn, vn, name):
    T = qd.shape[0]
    G = min(GROUP, T // CHUNK)
    ng = T // (CHUNK * G)

    def body(q_ref, k_ref, v_ref, gb_ref, gt_ref, t_ref, ss_ref, dsn_ref, do_ref, dvn_ref, vn_ref,
             dq_ref, dk_ref, dv_ref, dgb_ref):
        lower, strict, _ = _tri_consts()
        lmat = lower.astype(F32)
        ones = jnp.ones((CHUNK, 128), F32)
        lane = lax.broadcasted_iota(jnp.int32, (CHUNK, 128), 1)
        decay = [_chunk_decay(gb_ref, gt_ref, lmat, g) for g in range(G)]
        chains = [(g, h) for g in range(G) for h in range(DN_HEADS)]
        gens = []
        for g, h in chains:
            rows, gcs, grs = decay[g]
            sl = slice(128 * h, 128 * h + 128)
            c = _gdn_local(q_ref[rows, sl], k_ref[rows, sl], v_ref[rows, sl], gcs[:, h:h + 1], grs[h:h + 1, :],
                           gb_ref[rows, DN_HEADS + h:DN_HEADS + h + 1], lower, strict)
            c.update(tm=t_ref[rows, 64 * h:64 * h + 64], s=ss_ref[g, h], vn=vn_ref[rows, sl])
            gens.append(_gdn_chunk_bwd(c, do_ref[rows, sl], dvn_ref[rows, sl], dsn_ref[g, h], lower, strict, ones))
        dgb = [jnp.zeros((CHUNK, 128), F32) for _ in range(G)]
        for (g, h), (dq, dk, dv, dg, dbeta) in zip(chains, _lockstep(gens)):
            rows, sl = decay[g][0], slice(128 * h, 128 * h + 128)
            dq_ref[rows, sl], dk_ref[rows, sl], dv_ref[rows, sl] = dq, dk, dv
            dgb[g] = dgb[g] + jnp.where(lane == h, dg, 0.0) + jnp.where(lane == DN_HEADS + h, dbeta, 0.0)
        for g in range(G):
            dgb_ref[decay[g][0], :] = dgb[g]

    blk = pl.BlockSpec((CHUNK * G, 512), lambda n: (n, 0))
    half = pl.BlockSpec((CHUNK * G, 256), lambda n: (n, 0))
    nar = pl.BlockSpec((CHUNK * G, 128), lambda n: (n, 0))
    state = pl.BlockSpec((G, DN_HEADS, 128, 128), lambda n: (n, 0, 0, 0))
    return pl.pallas_call(
        body, name=name, grid=(ng,),
        in_specs=[blk, blk, blk, nar, pl.BlockSpec((G, 8, CHUNK), lambda n: (n, 0, 0)), half, state, state,
                  blk, blk, blk],
        out_specs=[blk, blk, blk, nar],
        out_shape=[jax.ShapeDtypeStruct((T, 512), F32)] * 3 + [jax.ShapeDtypeStruct((T, 128), F32)],
        compiler_params=_cparams(("parallel",)),
    )(qd, kd, vd, gb, gbt, tmi, ssave, dsn, do, dvn, vn)


def _gdn_prep_bwd1(dproj, proj, dqd, dkd, dvd, dgb, dkv_a, sconv_w, alog_v, dtb_v, name):
    T = proj.shape[0]
    tm = min(512, T)
    cur, prev, ab = _gdn_specs(T, tm)

    def body(dp_any, x_ref, xp_ref, ab_ref, dq_ref, dk_ref, dv_ref, dgb_ref, dkv_ref, w_ref, al_ref, dt_ref,
             o_ref, dpre_ref, st_ref, ext_ref):
        i = pl.program_id(0)
        pre = _gdn_conv(i, tm, x_ref, xp_ref, w_ref, ext_ref)
        y, dsl = _silu(pre), _dsilu(pre)
        for h in range(DN_HEADS):
            for base, g_ref, scale in ((0, dq_ref, 128 ** -0.5), (512, dk_ref, 1.0)):
                sl = slice(base + 128 * h, base + 128 * h + 128)
                xh = y[:, sl]
                r = lax.rsqrt(jnp.sum(xh * xh, axis=-1, keepdims=True) + EPS)
                xn = xh * r
                gy = g_ref[:, 128 * h:128 * h + 128]
                dpre_ref[:, sl] = (scale * r) * (gy - xn * jnp.sum(gy * xn, axis=-1, keepdims=True)) * dsl[:, sl]
        dpre_ref[:, 1024:] = dv_ref[...] * dsl[:, 1024:]
        abv, dgb = ab_ref[...], dgb_ref[...]
        lane = lax.broadcasted_iota(jnp.int32, (tm, 128), 1)
        na = -jnp.exp(al_ref[...])
        xs = abv + dt_ref[...]
        da = dgb * na * _sig(xs)
        b = _sig(abv)
        o_ref[:, :256] = dkv_ref[...].astype(BF16)
        o_ref[:, 256:] = jnp.where(lane < DN_HEADS, da,
                                   jnp.where(lane < 2 * DN_HEADS, dgb * b * (1.0 - b), 0.0)).astype(BF16)
        head = lane < DN_HEADS
        upd = jnp.concatenate([jnp.sum(jnp.where(head, dgb * na * _softplus(xs), 0.0), axis=0, keepdims=True),
                               jnp.sum(jnp.where(head, da, 0.0), axis=0, keepdims=True), jnp.zeros((6, 128), F32)],
                              axis=0)

        @pl.when(i == 0)
        def _():
            st_ref[...] = upd

        @pl.when(i > 0)
        def _():
            st_ref[...] += upd

    full = lambda s: pl.BlockSpec(s, lambda i: (0, 0))
    blk = pl.BlockSpec((tm, 512), lambda i: (i, 0))
    return pl.pallas_call(
        body, name=name, grid=(T // tm,),
        in_specs=[pl.BlockSpec(memory_space=pl.ANY), cur, prev, ab, blk, blk, blk,
                  pl.BlockSpec((tm, 128), lambda i: (i, 0)), pl.BlockSpec((tm, 256), lambda i: (i, 0)),
                  full((DN_K, QKV_C)), full((1, 128)), full((1, 128))],
        out_specs=[pl.BlockSpec((tm, 384), lambda i: (i, C_KA // 384)),
                   pl.BlockSpec((tm, QKV_C), lambda i: (i, 0)), full((8, 128))],
        out_shape=[jax.ShapeDtypeStruct(dproj.shape, BF16), jax.ShapeDtypeStruct((T, QKV_C), F32),
                   jax.ShapeDtypeStruct((8, 128), F32)],
        scratch_shapes=[pltpu.VMEM((tm + HALO_C, QKV_C), F32)],
        input_output_aliases={0: 0},
        compiler_params=_cparams(("arbitrary",)),
    )(dproj, proj, proj, proj, dqd, dkd, dvd, dgb, dkv_a, sconv_w, alog_v, dtb_v)


def _gdn_prep_bwd2(dproj, proj, dpre, sconv_w, name):
    T = proj.shape[0]
    tm = min(512, T)
    nt = T // tm
    r = tm // HALO_C
    cur, prev, _ = _gdn_specs(T, tm)

    def body(dp_any, x_ref, xp_ref, d_ref, dn_ref, w_ref, dx_ref, gw_ref, extx_ref, extd_ref):
        i = pl.program_id(0)
        extx_ref[:HALO_C] = jnp.where(i > 0, xp_ref[...], 0.0)
        extx_ref[HALO_C:] = x_ref[...]
        d = d_ref[...]
        extd_ref[:tm] = d
        extd_ref[tm:] = jnp.where(i < nt - 1, dn_ref[...], 0.0)
        dx = jnp.zeros((tm, QKV_C), F32)
        rows = []
        for k in range(DN_K):
            dx = dx + w_ref[k:k + 1, :] * extd_ref[pl.ds(DN_K - 1 - k, tm), :]
            rows.append(jnp.sum(d * extx_ref[pl.ds(HALO_C - DN_K + 1 + k, tm), :], axis=0, keepdims=True))
        rows.append(jnp.zeros((8 - DN_K, QKV_C), F32))
        gw = jnp.concatenate(rows, axis=0)
        dx_ref[...] = dx.astype(BF16)

        @pl.when(i == 0)
        def _():
            gw_ref[...] = gw

        @pl.when(i > 0)
        def _():
            gw_ref[...] += gw

    full = lambda s: pl.BlockSpec(s, lambda i: (0, 0))
    return pl.pallas_call(
        body, name=name, grid=(nt,),
        in_specs=[pl.BlockSpec(memory_space=pl.ANY), cur, prev, pl.BlockSpec((tm, QKV_C), lambda i: (i, 0)),
                  pl.BlockSpec((HALO_C, QKV_C), lambda i: (jnp.minimum((i + 1) * r, T // HALO_C - 1), 0)),
                  full((DN_K, QKV_C))],
        out_specs=[cur, full((8, QKV_C))],
        out_shape=[jax.ShapeDtypeStruct(dproj.shape, BF16), jax.ShapeDtypeStruct((8, QKV_C), F32)],
        scratch_shapes=[pltpu.VMEM((tm + HALO_C, QKV_C), F32), pltpu.VMEM((tm + HALO_C, QKV_C), F32)],
        input_output_aliases={0: 0},
        compiler_params=_cparams(("arbitrary",)),
    )(dproj, proj, proj, dpre, dpre, sconv_w)


def _merge_fwd(x, proj, ya, yb, yc, wa, wb, wc, wo, gate, name):
    T = x.shape[0]
    tm = min(256, T)

    def body(x_ref, mg_ref, ya_ref, yb_ref, yc_ref, wa_ref, wb_ref, wc_ref, wo_ref, gate_ref, o_ref):
        merged = (_sig(mg_ref[:, :D]) * _bdot(ya_ref[...], wa_ref[...])
                  + _sig(mg_ref[:, D:2 * D]) * _bdot(yb_ref[...], wb_ref[...])
                  + _sig(mg_ref[:, 2 * D:]) * _bdot(yc_ref[...], wc_ref[...]))
        o_ref[...] = x_ref[...] + gate_ref[...] * _bdot(merged, wo_ref[...])

    full = lambda s: pl.BlockSpec(s, lambda i: (0, 0))
    yb_ = pl.BlockSpec((tm, 512), lambda i: (i, 0))
    return pl.pallas_call(
        body, name=name, grid=(T // tm,),
        in_specs=[pl.BlockSpec((tm, D), lambda i: (i, 0)), pl.BlockSpec((tm, 3 * D), lambda i: (i, 0)), yb_, yb_, yb_,
                  full((512, D)), full((512, D)), full((512, D)), full((D, D)), full((1, D))],
        out_specs=pl.BlockSpec((tm, D), lambda i: (i, 0)),
        out_shape=jax.ShapeDtypeStruct((T, D), F32),
        compiler_params=_cparams(("parallel",)),
    )(x, proj, ya, yb, yc, wa, wb, wc, wo, _row(gate))


def _merge_fwd_loss(x, proj, ya, yb, yc, wa, wb, wc, wo, gate, tgt, name):
    T = x.shape[0]
    tm = min(256, T)

    def body(x_ref, mg_ref, ya_ref, yb_ref, yc_ref, wa_ref, wb_ref, wc_ref, wo_ref, gate_ref, t_ref, dy_ref, l_ref):
        i = pl.program_id(0)
        merged = (_sig(mg_ref[:, :D]) * _bdot(ya_ref[...], wa_ref[...])
                  + _sig(mg_ref[:, D:2 * D]) * _bdot(yb_ref[...], wb_ref[...])
                  + _sig(mg_ref[:, 2 * D:]) * _bdot(yc_ref[...], wc_ref[...]))
        diff = (x_ref[...] + gate_ref[...] * _bdot(merged, wo_ref[...])) - t_ref[...]
        dy_ref[...] = diff * (1.0 / D)
        part = jnp.sum(diff * diff, axis=0, keepdims=True)

        @pl.when(i == 0)
        def _():
            l_ref[...] = part

        @pl.when(i > 0)
        def _():
            l_ref[...] += part

    full = lambda s: pl.BlockSpec(s, lambda i: (0, 0))
    yb_ = pl.BlockSpec((tm, 512), lambda i: (i, 0))
    row = pl.BlockSpec((tm, D), lambda i: (i, 0))
    return pl.pallas_call(
        body, name=name, grid=(T // tm,),
        in_specs=[row, pl.BlockSpec((tm, 3 * D), lambda i: (i, 0)), yb_, yb_, yb_,
                  full((512, D)), full((512, D)), full((512, D)), full((D, D)), full((1, D)), row],
        out_specs=[row, full((1, D))],
        out_shape=[jax.ShapeDtypeStruct((T, D), F32), jax.ShapeDtypeStruct((1, D), F32)],
        compiler_params=_cparams(("arbitrary",)),
    )(x, proj, ya, yb, yc, wa, wb, wc, wo, _row(gate), tgt)


def _merge_bwd(dout, proj, ya, yb, yc, wa, wb, wc, wo, gate, name):
    T = dout.shape[0]
    tm = min(256, T)
    nt = T // tm

    def body(do_ref, mg_ref, ya_ref, yb_ref, yc_ref, wa_ref, wb_ref, wc_ref, wo_ref, gate_ref,
             dmg_ref, dya_ref, dyb_ref, dyc_ref, gwa_hbm, gwb_hbm, gwc_hbm, gwo_hbm, gg_ref,
             gwa_ref, gwb_ref, gwc_ref, gwo_ref):
        i = pl.program_id(0)

        @pl.when(i == 0)
        def _():
            for r in (gwa_ref, gwb_ref, gwc_ref, gwo_ref, gg_ref):
                r[...] = jnp.zeros_like(r)

        ys = (ya_ref[...], yb_ref[...], yc_ref[...])
        ws = (wa_ref, wb_ref, wc_ref)
        gs = tuple(_sig(mg_ref[:, j * D:(j + 1) * D]) for j in range(3))
        ps = tuple(_bdot(ys[j], ws[j][...]) for j in range(3))
        merged = gs[0] * ps[0] + gs[1] * ps[1] + gs[2] * ps[2]
        do = do_ref[...]
        dmerged = _bdot(do * gate_ref[...], wo_ref[...], NT)
        gwo_ref[...] += _bdot(merged, do, TN)
        for j, (dy_ref, gw_ref) in enumerate(((dya_ref, gwa_ref), (dyb_ref, gwb_ref), (dyc_ref, gwc_ref))):
            dp = dmerged * gs[j]
            dmg_ref[:, j * D:(j + 1) * D] = (dmerged * ps[j] * gs[j] * (1.0 - gs[j])).astype(BF16)
            dy_ref[...] = _bdot(dp, ws[j][...], NT)
            gw_ref[...] += _bdot(ys[j], dp, TN)

        @pl.when(i == nt - 1)
        def _():
            m = gwo_ref[...]
            gg_ref[...] = jnp.sum(wo_ref[...].astype(F32) * m, axis=0, keepdims=True)
            gwo_ref[...] = m * gate_ref[...]
            for src, dst in ((gwa_ref, gwa_hbm), (gwb_ref, gwb_hbm), (gwc_ref, gwc_hbm), (gwo_ref, gwo_hbm)):
                pltpu.sync_copy(src, dst)

    full = lambda s: pl.BlockSpec(s, lambda i: (0, 0))
    yb_ = pl.BlockSpec((tm, 512), lambda i: (i, 0))
    anyspec = pl.BlockSpec(memory_space=pl.ANY)
    return pl.pallas_call(
        body, name=name, grid=(nt,),
        in_specs=[pl.BlockSpec((tm, D), lambda i: (i, 0)), pl.BlockSpec((tm, 3 * D), lambda i: (i, 0)), yb_, yb_, yb_,
                  full((512, D)), full((512, D)), full((512, D)), full((D, D)), full((1, D))],
        out_specs=[pl.BlockSpec((tm, 3 * D), lambda i: (i, 0)), yb_, yb_, yb_, anyspec, anyspec, anyspec, anyspec,
                   full((1, D))],
        out_shape=[jax.ShapeDtypeStruct((T, NP), BF16)] + [jax.ShapeDtypeStruct((T, 512), F32)] * 3
        + [jax.ShapeDtypeStruct((512, D), F32)] * 3 + [jax.ShapeDtypeStruct((D, D), F32), jax.ShapeDtypeStruct((1, D), F32)],
        scratch_shapes=[pltpu.VMEM((512, D), F32)] * 3 + [pltpu.VMEM((D, D), F32)],
        compiler_params=_cparams(("arbitrary",)),
    )(dout, proj, ya, yb, yc, wa, wb, wc, wo, _row(gate))


def _loss_head(y, tgt, name):
    T = y.shape[0]
    tm = min(512, T)

    def body(y_ref, t_ref, dy_ref, l_ref):
        i = pl.program_id(0)
        diff = y_ref[...] - t_ref[...]
        dy_ref[...] = diff * (1.0 / D)
        part = jnp.sum(diff * diff, axis=0, keepdims=True)

        @pl.when(i == 0)
        def _():
            l_ref[...] = part

        @pl.when(i > 0)
        def _():
            l_ref[...] += part

    blk = pl.BlockSpec((tm, D), lambda i: (i, 0))
    return pl.pallas_call(
        body, name=name, grid=(T // tm,), in_specs=[blk, blk],
        out_specs=[blk, pl.BlockSpec((1, D), lambda i: (0, 0))],
        out_shape=[jax.ShapeDtypeStruct((T, D), F32), jax.ShapeDtypeStruct((1, D), F32)],
        compiler_params=_cparams(("arbitrary",)),
    )(y, tgt)


def _ada_fwd(c_all, w_ada, b_my, name):
    def body(c_ref, w_ref, b_ref, o_ref):
        sc = _silu(c_ref[...])
        for l in range(DEPTH):
            o_ref[l] = _bdot(sc, w_ref[l]) + b_ref[l:l + 1, :]

    return pl.pallas_call(body, name=name, out_shape=jax.ShapeDtypeStruct((DEPTH, N_DEV, w_ada.shape[2]), F32),
                          compiler_params=_cparams())(c_all, w_ada, b_my)


def _ada_bwd(c_all, dmod_my, name):
    def body(c_ref, d_ref, o_ref):
        sc = _silu(c_ref[...])
        for l in range(DEPTH):
            o_ref[l] = _bdot(sc, d_ref[l], TN)

    return pl.pallas_call(body, name=name, out_shape=jax.ShapeDtypeStruct((DEPTH, D, dmod_my.shape[2]), F32),
                          compiler_params=_cparams())(c_all, dmod_my)


def _adam_math(w, g, m, v):
    m = ADAM_B1 * m + (1.0 - ADAM_B1) * g
    v = ADAM_B2 * v + (1.0 - ADAM_B2) * (g * g)
    m_hat = m / (1.0 - ADAM_B1 ** ADAM_STEP)
    v_hat = v / (1.0 - ADAM_B2 ** ADAM_STEP)
    return -ADAM_LR * (m_hat / (jnp.sqrt(v_hat) + ADAM_EPS) + ADAM_WD * w), m, v


def _row_tile(rows, cap):
    best = rows
    for t in range(8, min(rows, cap) + 1, 8):
        if rows % t == 0:
            best = t
    return best if best <= cap else rows


def _adamw(w, g, m, v, name):
    R, C = w.shape
    tr = _row_tile(R, 256)

    def body(w_ref, g_ref, m_ref, v_ref, d_ref, mo_ref, vo_ref):
        d_ref[...], mo_ref[...], vo_ref[...] = _adam_math(w_ref[...], g_ref[...], m_ref[...], v_ref[...])

    blk = pl.BlockSpec((tr, C), lambda i: (i, 0))
    return pl.pallas_call(body, name=name, grid=(R // tr,), in_specs=[blk] * 4, out_specs=[blk] * 3,
                          out_shape=[jax.ShapeDtypeStruct((R, C), F32)] * 3,
                          compiler_params=_cparams(("parallel",)))(w, g, m, v)


def _sum_adamw_many(parts, ws, ms, vs, name):
    n = len(ws)

    def body(*refs):
        ins, outs = refs[:4 * n], refs[4 * n:]
        for i in range(n):
            g = ins[i][0]
            for j in range(1, N_DEV):
                g = g + ins[i][j]
            d, m, v = _adam_math(ins[n + i][...], g, ins[2 * n + i][...], ins[3 * n + i][...])
            outs[i][...], outs[n + i][...], outs[2 * n + i][...], outs[3 * n + i][...] = g, d, m, v

    shapes = [jax.ShapeDtypeStruct(w.shape, F32) for w in ws]
    out = pl.pallas_call(body, name=name, out_shape=shapes * 4, compiler_params=_cparams())(*parts, *ws, *ms, *vs)
    return out[:n], out[n:2 * n], out[2 * n:3 * n], out[3 * n:]


def _sum_adamw(parts0, parts1, w, m, v, name):
    P, R, C = parts0.shape
    tr = _row_tile(R, 128)
    nt = R // tr

    def body(p0_ref, p1_ref, w_ref, m_ref, v_ref, g_ref, d_ref, mo_ref, vo_ref):
        def emit(p_ref):
            g = p_ref[0].astype(F32)
            for j in range(1, P):
                g = g + p_ref[j].astype(F32)
            g_ref[...] = g
            d_ref[...], mo_ref[...], vo_ref[...] = _adam_math(w_ref[...], g, m_ref[...], v_ref[...])

        @pl.when(pl.program_id(0) == 0)
        def _():
            emit(p0_ref)

        @pl.when(pl.program_id(0) == 1)
        def _():
            emit(p1_ref)

    blk = pl.BlockSpec((tr, C), lambda l, i: (l * nt + i, 0))
    return pl.pallas_call(
        body, name=name, grid=(DEPTH, nt),
        in_specs=[pl.BlockSpec((P, tr, C), lambda l, i: (0, i * (1 - l) + (nt - 1) * l, 0)),
                  pl.BlockSpec((P, tr, C), lambda l, i: (0, i * l, 0)), blk, blk, blk],
        out_specs=[blk] * 4, out_shape=[jax.ShapeDtypeStruct((DEPTH * R, C), F32)] * 4,
        compiler_params=_cparams(("arbitrary", "arbitrary")))(parts0, parts1, w, m, v)


def _pair_sum_many(core, bufs, recvs, name):
    n = len(bufs)

    def body(c_ref, *refs):
        for a_ref, b_ref, o_ref in zip(refs[:n], refs[n:2 * n], refs[2 * n:]):
            o_ref[...] = (a_ref[:, 0].astype(F32) + b_ref[...].astype(F32)).astype(BF16)

    mine = [pl.BlockSpec((4, 1) + b.shape[2:], lambda i, c: (0, c[0], 0, 0)) for b in bufs]
    whole = [pl.BlockSpec(r.shape, lambda i, c: (0, 0, 0)) for r in recvs]
    return pl.pallas_call(
        body, name=name,
        grid_spec=pltpu.PrefetchScalarGridSpec(num_scalar_prefetch=1, grid=(1,), in_specs=mine + whole,
                                               out_specs=whole),
        out_shape=[jax.ShapeDtypeStruct(r.shape, BF16) for r in recvs],
        compiler_params=_cparams(("arbitrary",)))(core, *bufs, *recvs)


def _pair_sum(core, buf, recv, name):
    _, _, R, C = buf.shape
    tr = _row_tile(R, 128)

    def body(c_ref, a_ref, b_ref, o_ref):
        o_ref[...] = (a_ref[:, 0].astype(F32) + b_ref[...].astype(F32)).astype(BF16)

    return pl.pallas_call(
        body, name=name,
        grid_spec=pltpu.PrefetchScalarGridSpec(
            num_scalar_prefetch=1, grid=(R // tr,),
            in_specs=[pl.BlockSpec((4, 1, tr, C), lambda i, c: (0, c[0], i, 0)),
                      pl.BlockSpec((4, tr, C), lambda i, c: (0, i, 0))],
            out_specs=pl.BlockSpec((4, tr, C), lambda i, c: (0, i, 0))),
        out_shape=jax.ShapeDtypeStruct((4, R, C), BF16),
        compiler_params=_cparams(("parallel",)))(core, buf, recv)


SHARD_IN = D_IN // N_DEV


def _w_in_pieces():
    out, p = [], 0
    for a, b in _PAD_FROM:
        for j in range(N_DEV):
            lo, hi = max(a, SHARD_IN * j), min(b, SHARD_IN * (j + 1))
            if lo < hi:
                out.append((j, lo - SHARD_IN * j, hi - SHARD_IN * j, p + lo - a))
        p += b - a
    return out


def _assemble_w_in(gw, name):
    tr = 256
    nt = D // tr

    def body(x_ref, o_ref):
        for j, s0, s1, d0 in _w_in_pieces():
            o_ref[:, d0:d0 + s1 - s0] = x_ref[j, :, s0:s1]
        o_ref[:, D_IN:] = jnp.zeros((tr, NP - D_IN), gw.dtype)

    return pl.pallas_call(
        body, name=name, grid=(nt,),
        in_specs=[pl.BlockSpec((N_DEV, tr, SHARD_IN), lambda i: (0, i, 0))],
        out_specs=pl.BlockSpec((tr, NP), lambda i: (i, 0)),
        out_shape=jax.ShapeDtypeStruct((D, NP), gw.dtype),
        compiler_params=_cparams(("parallel",)))(gw)


def _split_w_in_grad(g, name):
    tr = 256

    def body(g_ref, o_ref):
        for j, s0, s1, d0 in _w_in_pieces():
            o_ref[j, :, s0:s1] = g_ref[:, d0:d0 + s1 - s0].astype(BF16)

    return pl.pallas_call(
        body, name=name, grid=(D // tr,),
        in_specs=[pl.BlockSpec((tr, NP), lambda i: (i, 0))],
        out_specs=pl.BlockSpec((N_DEV, tr, SHARD_IN), lambda i: (0, i, 0)),
        out_shape=jax.ShapeDtypeStruct((N_DEV, D, SHARD_IN), BF16),
        compiler_params=_cparams(("parallel",)))(g)


def _mesh_pos():
    return lax.axis_index("x"), lax.axis_index("y"), lax.axis_index("c")


def _launch(copies, peers, bufs, out_structs, sems, name, collective_id):
    n = len(bufs)
    if collective_id is None:
        anyspec = pl.BlockSpec(memory_space=pl.ANY)
        return list(pl.pallas_call(
            lambda *refs: copies(refs[:n], refs[n:n + len(out_structs)], *refs[n + len(out_structs):]),
            name=name, in_specs=[anyspec] * n, out_specs=[anyspec] * len(out_structs), out_shape=list(out_structs),
            scratch_shapes=list(sems))(*bufs))
    ins = [jax.new_ref(b, memory_space=pltpu.MemorySpace.HBM) for b in bufs]
    outs = [jax.empty_ref(s, memory_space=pltpu.MemorySpace.HBM) for s in out_structs]

    @pl.kernel(mesh=plsc.ScalarSubcoreMesh(axis_name="sequencer", num_cores=1), name=name, scratch_types=tuple(sems),
               compiler_params=pltpu.CompilerParams(collective_id=collective_id))
    def on_sequencer(*sem_refs):
        barrier = pltpu.get_barrier_semaphore()
        targets = peers()
        for p in targets:
            pl.semaphore_signal(barrier, inc=1, device_id=p, device_id_type=pl.DeviceIdType.MESH)
        pl.semaphore_wait(barrier, len(targets))
        copies(ins, outs, *sem_refs)

    on_sequencer()
    return [r[...] for r in outs]


def _all_gather(blocks, name, collective_id=None):
    n = len(blocks)

    def peers():
        x, y, c = _mesh_pos()
        return [(x, y, 1 - c), (1 - x, y, c), (x, 1 - y, c), (1 - x, 1 - y, c)]

    def copies(ins, outs, send_sems, recv_sems, local_sems):
        x, y, c = _mesh_pos()
        me, sibling = (x, y, c), (x, y, 1 - c)
        chips = [(1 - x, y), (x, 1 - y), (1 - x, 1 - y)]
        idx = lambda p: 4 * p[0] + 2 * p[1] + p[2]

        def copy(a, k, block, to, src=None):
            dst = outs[a].at[idx(block)]
            return pltpu.make_async_remote_copy(
                src_ref=dst if src is None else src, dst_ref=dst, send_sem=send_sems.at[a, k],
                recv_sem=recv_sems.at[a, k], device_id=to, device_id_type=pl.DeviceIdType.MESH)

        mine = [pltpu.make_async_copy(ins[a], outs[a].at[idx(me)], local_sems.at[a]) for a in range(n)]
        for cp in mine:
            cp.start()
        first = []
        for a in range(n):
            first.append(copy(a, 0, me, sibling, src=ins[a]))
            first += [copy(a, 1 + j, me, (*chip, c), src=ins[a]) for j, chip in enumerate(chips)]
        for cp in first:
            cp.start()
        passed = []
        for j, chip in enumerate(chips):
            for a in range(n):
                copy(a, 1 + j, (*chip, c), me).wait_recv()
                cp = copy(a, 4 + j, (*chip, c), sibling)
                cp.start()
                passed.append(cp)
        for a in range(n):
            copy(a, 0, sibling, me).wait_recv()
            for j, chip in enumerate(chips):
                copy(a, 4 + j, (*chip, 1 - c), me).wait_recv()
        for cp in first + passed:
            cp.wait_send()
        for cp in mine:
            cp.wait()

    return _launch(copies, peers, blocks, [jax.ShapeDtypeStruct((N_DEV,) + b.shape, b.dtype) for b in blocks],
                   [pltpu.SemaphoreType.DMA((n, 7)), pltpu.SemaphoreType.DMA((n, 7)), pltpu.SemaphoreType.DMA((n,))],
                   name, collective_id)


def _exchange_core(bufs, name, collective_id=None):
    n = len(bufs)

    def peers():
        x, y, c = _mesh_pos()
        return [(x, y, 1 - c)]

    def copies(ins, outs, send_sems, recv_sems):
        x, y, c = _mesh_pos()
        started = []
        for a in range(n):
            for q in range(4):
                cp = pltpu.make_async_remote_copy(
                    src_ref=ins[a].at[q, 1 - c], dst_ref=outs[a].at[q], send_sem=send_sems.at[a, q],
                    recv_sem=recv_sems.at[a, q], device_id=(x, y, 1 - c), device_id_type=pl.DeviceIdType.MESH)
                cp.start()
                started.append(cp)
        for cp in started:
            cp.wait()

    return _launch(copies, peers, bufs, [jax.ShapeDtypeStruct((4,) + b.shape[2:], b.dtype) for b in bufs],
                   [pltpu.SemaphoreType.DMA((n, 4)), pltpu.SemaphoreType.DMA((n, 4))], name, collective_id)


def _exchange_chips(bufs, name, collective_id=None):
    n = len(bufs)

    def peers():
        x, y, c = _mesh_pos()
        return [(1 - x, y, c), (x, 1 - y, c), (1 - x, 1 - y, c)]

    def copies(ins, outs, send_sems, recv_sems, local_sems):
        x, y, c = _mesh_pos()
        chip = 2 * x + y
        local = [pltpu.make_async_copy(ins[a].at[chip], outs[a].at[chip], local_sems.at[a]) for a in range(n)]
        for cp in local:
            cp.start()
        started = []
        for k in range(1, 4):
            px = 1 - x if k & 2 else x
            py = 1 - y if k & 1 else y
            for a in range(n):
                cp = pltpu.make_async_remote_copy(
                    src_ref=ins[a].at[2 * px + py], dst_ref=outs[a].at[chip], send_sem=send_sems.at[a, k - 1],
                    recv_sem=recv_sems.at[a, k - 1], device_id=(px, py, c), device_id_type=pl.DeviceIdType.MESH)
                cp.start()
                started.append(cp)
        for cp in started:
            cp.wait()
        for cp in local:
            cp.wait()

    return _launch(copies, peers, bufs, [jax.ShapeDtypeStruct(b.shape, b.dtype) for b in bufs],
                   [pltpu.SemaphoreType.DMA((n, 3)), pltpu.SemaphoreType.DMA((n, 3)), pltpu.SemaphoreType.DMA((n,))],
                   name, collective_id)


_SMALL = ("b_ada", "norm_g", "q_norm_g", "k_norm_g", "sinks", "dw_b", "ln_g", "ln_b", "pw2_b", "a_log", "dt_bias",
          "dn_norm_g", "dw_w", "sconv_w")


def _lane4(v):
    return jnp.pad(v, (0, 124)).reshape(1, 128)


def kernel(x, c, w_ada, b_ada, norm_g, w_in, q_norm_g, k_norm_g, sinks, dw_w, dw_b, ln_g, ln_b, pw2_w, pw2_b, sconv_w, a_log, dt_bias, dn_norm_g, w_proj_a, w_proj_b, w_proj_c, w_out, loss_target, m_w_ada, m_b_ada, m_norm_g, m_w_in, m_q_norm_g, m_k_norm_g, m_sinks, m_dw_w, m_dw_b, m_ln_g, m_ln_b, m_pw2_w, m_pw2_b, m_sconv_w, m_a_log, m_dt_bias, m_dn_norm_g, m_w_proj_a, m_w_proj_b, m_w_proj_c, m_w_out, v_w_ada, v_b_ada, v_norm_g, v_w_in, v_q_norm_g, v_k_norm_g, v_sinks, v_dw_w, v_dw_b, v_ln_g, v_ln_b, v_pw2_w, v_pw2_b, v_sconv_w, v_a_log, v_dt_bias, v_dn_norm_g, v_w_proj_a, v_w_proj_b, v_w_proj_c, v_w_out):
    T = x.shape[1]
    nc = T // CHUNK
    xi, yi, ci = _mesh_pos()
    me = 4 * xi + 2 * yi + ci
    big_w = (w_in, pw2_w, w_proj_a, w_proj_b, w_proj_c, w_out)
    big_m = (m_w_in, m_pw2_w, m_w_proj_a, m_w_proj_b, m_w_proj_c, m_w_out)
    big_v = (v_w_in, v_pw2_w, v_w_proj_a, v_w_proj_b, v_w_proj_c, v_w_out)

    ada_cols = w_ada.shape[2]
    dw_cols, sc_cols = dw_w.shape[2], sconv_w.shape[2]
    flat2 = lambda a: a.reshape(-1, a.shape[-1])
    big16 = [[a[l].astype(BF16) for l in range(DEPTH)] for a in big_w]
    c_all, gdw, gsc = _all_gather([c, dw_w, sconv_w], "gather_small", collective_id=0)
    (gw_in0,) = _all_gather([big16[0][0]], "gather_w_in0", collective_id=7)
    gw_in0, _ = lax.optimization_barrier((gw_in0, (flat2(m_w_in), flat2(v_w_in))))
    c_all = c_all.reshape(N_DEV, D)
    dw_f = gdw.transpose(1, 2, 0, 3).reshape(DEPTH, CONV_K, 512)
    sc_f = gsc.transpose(1, 2, 0, 3).reshape(DEPTH, DN_K, QKV_C)

    b_my = lax.dynamic_slice(b_ada, (0, me * ada_cols), (DEPTH, ada_cols))
    mod_part = _ada_fwd(c_all, w_ada, b_my, "ada_fwd")
    (gmod,) = _all_gather([mod_part.reshape(-1, 128)], "gather_mod")

    rest0 = [a[0] for a in big16[1:]]
    all1 = [a[1] for a in big16]
    (rest0, all1), gmod = lax.optimization_barrier(((rest0, all1), gmod))
    got0 = [gw_in0] + _all_gather(rest0, "gather_rest0", collective_id=1)
    got1 = _all_gather(all1, "gather_weights1", collective_id=6)
    wp, pw2_f, wa_f, wb_f, wc_f, wo_f = [], [], [], [], [], []
    for l, (gw_in, gpw2, gpa, gpb, gpc, gwo) in enumerate((got0, got1)):
        wp.append(_assemble_w_in(gw_in, f"assemble_w_in{l}"))
        pw2_f.append(gpw2.reshape(512, 512))
        for dst, g in ((wa_f, gpa), (wb_f, gpb), (wc_f, gpc)):
            dst.append(g.transpose(1, 0, 2).reshape(512, D))
        wo_f.append(gwo.reshape(D, D))
    mod_all = gmod.reshape(N_DEV, DEPTH, N_DEV, ada_cols).transpose(1, 2, 0, 3).reshape(DEPTH, N_DEV, 3 * D)
    mod = lax.dynamic_index_in_dim(mod_all, me, axis=1, keepdims=False)
    shift, scale, gate = mod[:, :D], mod[:, D:2 * D], mod[:, 2 * D:]

    xs, saved = [x[0]], []
    for l in range(DEPTH):
        xl = xs[-1]
        h = _norm_fwd(xl, norm_g[l], scale[l], shift[l], f"norm_fwd{l}")
        proj = _mm(h, wp[l], tm=min(2048, T), tn=1152, tk=D, name=f"in_proj{l}")
        ya = _attn_fwd(proj, q_norm_g[l], k_norm_g[l], sinks[l], f"attn_fwd{l}")
        yb, cv = _conf_fwd(proj, dw_f[l], dw_b[l], ln_g[l], ln_b[l], pw2_f[l], pw2_b[l], f"conf_fwd{l}")
        alv, dtv, dng = _lane4(a_log[l]), _lane4(dt_bias[l]), _row(dn_norm_g[l])
        qd, kd, vd, gb = _gdn_prep_fwd(proj, sc_f[l], alv, dtv, f"gdn_prep_fwd{l}")
        gbt = gb[:, :8].reshape(nc, CHUNK, 8).transpose(0, 2, 1)
        u, w, qe, ke, pm, tmi, eg = _gdn_chunk_fwd(qd, kd, vd, gb, gbt, f"gdn_chunk_fwd{l}")
        yc, o, vn, ss = _gdn_scan_fwd(u, w, qe, ke, pm, eg, proj, dng, f"gdn_scan_fwd{l}")
        if l < DEPTH - 1:
            xs.append(_merge_fwd(xl, proj, ya, yb, yc, wa_f[l], wb_f[l], wc_f[l], wo_f[l], gate[l], f"merge_fwd{l}"))
        else:
            dout, lsum = _merge_fwd_loss(xl, proj, ya, yb, yc, wa_f[l], wb_f[l], wc_f[l], wo_f[l], gate[l],
                                         loss_target[0], "merge_fwd_loss")
        saved.append((h, proj, ya, yb, yc, qd, kd, vd, gb, gbt, ss, alv, dtv, dng, w, qe, ke, pm, tmi, eg, o, vn, cv))


    small = {name: [None] * DEPTH for name in _SMALL}
    big_parts = [None] * DEPTH
    core = jnp.reshape(ci, (1,)).astype(jnp.int32)
    for l in reversed(range(DEPTH)):
        h, proj, ya, yb, yc, qd, kd, vd, gb, gbt, ss, alv, dtv, dng, w, qe, ke, pm, tmi, eg, o, vn, cv = saved[l]
        dproj, dya, dyb, dyc, g_wa, g_wb, g_wc, g_wo, g_gate = _merge_bwd(
            dout, proj, ya, yb, yc, wa_f[l], wb_f[l], wc_f[l], wo_f[l], gate[l], f"merge_bwd{l}")
        dproj, dkv_a, g_q, g_k, g_s = _attn_bwd(dproj, proj, dya, q_norm_g[l], k_norm_g[l], sinks[l], f"attn_bwd{l}")
        dproj, du1, g_pw2, st_b = _conf_bwd1(dproj, proj, cv, dyb, ln_g[l], ln_b[l], pw2_f[l], pw2_b[l],
                                             f"conf_bwd_a{l}")
        dproj, g_dw = _conf_bwd2(dproj, proj, du1, dw_f[l], f"conf_bwd_b{l}")
        dproj, do, dvn, dsn, g_dn = _gdn_scan_bwd(dproj, w, qe, ke, pm, eg, o, proj, dyc, dng, f"gdn_scan_bwd{l}")
        dqd, dkd, dvd, dgb = _gdn_chunk_grad(qd, kd, vd, gb, gbt, tmi, ss, dsn, do, dvn, vn, f"gdn_chunk_bwd{l}")
        dproj, dpre, st_c = _gdn_prep_bwd1(dproj, proj, dqd, dkd, dvd, dgb, dkv_a, sc_f[l], alv, dtv,
                                           f"gdn_prep_bwd_a{l}")
        dproj, g_sc = _gdn_prep_bwd2(dproj, proj, dpre, sc_f[l], f"gdn_prep_bwd_b{l}")
        g_wp = _mm(h, dproj, ta=True, tm=D, tn=1152, tk=min(2048, T), name=f"d_w_in{l}")
        by_dest = [_split_w_in_grad(g_wp, f"split_w_in_grad{l}"), g_pw2.reshape(N_DEV, -1, 512).astype(BF16)]
        by_dest += [g.reshape(512, N_DEV, -1).transpose(1, 0, 2).astype(BF16) for g in (g_wa, g_wb, g_wc)]
        by_dest.append(g_wo.reshape(N_DEV, -1, D).astype(BF16))
        by_dest = [b.reshape(4, 2, -1, b.shape[-1]) for b in by_dest]
        if l < DEPTH - 1:
            by_dest, big_parts[l + 1] = lax.optimization_barrier((by_dest, big_parts[l + 1]))
        from_sibling = _exchange_core(by_dest, f"exchange_grads_core{l}", collective_id=2 + 2 * l)

        def input_grad(dproj, dout):
            dh = _mm(dproj, wp[l], tb=True, tm=min(1024, T), tn=D, tk=2688, name=f"d_h{l}")
            return _norm_bwd(dh, xs[l], dout, norm_g[l], scale[l], f"norm_bwd{l}")

        if l > 0:
            dout, st_n = input_grad(dproj, dout)
            from_sibling, dout = lax.optimization_barrier((from_sibling, dout))
        chip_sums = [_pair_sum(core, by_dest[0], from_sibling[0], f"pair_sum_w_in{l}")]
        chip_sums += _pair_sum_many(core, by_dest[1:], from_sibling[1:], f"pair_sum_rest{l}")
        big_parts[l] = _exchange_chips(chip_sums, f"exchange_grads_chips{l}", collective_id=3 + 2 * l)
        if l > 0:
            dout, chip_sums = lax.optimization_barrier((dout, chip_sums))
        else:
            dproj, chip_sums = lax.optimization_barrier((dproj, chip_sums))
            dout, st_n = input_grad(dproj, dout)
        for name, g in (("b_ada", jnp.concatenate([st_n[0], st_n[1], g_gate[0]])), ("norm_g", st_n[2]),
                        ("q_norm_g", g_q.reshape(ATT_HEADS, ATT_HD).sum(0)), ("k_norm_g", g_k.reshape(2, ATT_HD).sum(0)),
                        ("sinks", g_s[0]), ("dw_b", st_b[3]),
                        ("ln_g", st_b[1]), ("ln_b", st_b[2]), ("pw2_b", st_b[0]), ("a_log", st_c[0, :4]),
                        ("dt_bias", st_c[1, :4]), ("dn_norm_g", g_dn[0]), ("dw_w", g_dw[:CONV_K]),
                        ("sconv_w", g_sc[:DN_K])):
            small[name][l] = g
    grad_x = dout[None]

    big_parts, dout = lax.optimization_barrier((big_parts, dout))
    sum_big = lambda i: _sum_adamw(big_parts[0][i], big_parts[1][i], flat2(big_w[i]), flat2(big_m[i]),
                                   flat2(big_v[i]), f"sum_adamw{i}")
    res = [sum_big(0)]

    names = list(_SMALL)
    gathered = _all_gather([jnp.stack(small[n]) for n in names] + [lsum], "gather_small_grads")
    gparts = dict(zip(names, gathered))
    loss = 0.5 * jnp.sum(jnp.sum(gathered[-1], axis=(1, 2))) / D
    dmod_my = lax.dynamic_slice(gparts["b_ada"], (0, 0, me * ada_cols), (N_DEV, DEPTH, ada_cols)).transpose(1, 0, 2)
    g_w_ada = _ada_bwd(c_all, dmod_my, "ada_bwd")
    gparts["dw_w"] = lax.dynamic_slice(gparts["dw_w"], (0, 0, 0, me * dw_cols), (N_DEV, DEPTH, CONV_K, dw_cols))
    gparts["sconv_w"] = lax.dynamic_slice(gparts["sconv_w"], (0, 0, 0, me * sc_cols), (N_DEV, DEPTH, DN_K, sc_cols))
    env = dict(b_ada=(b_ada, m_b_ada, v_b_ada), norm_g=(norm_g, m_norm_g, v_norm_g),
               q_norm_g=(q_norm_g, m_q_norm_g, v_q_norm_g), k_norm_g=(k_norm_g, m_k_norm_g, v_k_norm_g),
               sinks=(sinks, m_sinks, v_sinks), dw_b=(dw_b, m_dw_b, v_dw_b), ln_g=(ln_g, m_ln_g, v_ln_g),
               ln_b=(ln_b, m_ln_b, v_ln_b), pw2_b=(pw2_b, m_pw2_b, v_pw2_b), a_log=(a_log, m_a_log, v_a_log),
               dt_bias=(dt_bias, m_dt_bias, v_dt_bias), dn_norm_g=(dn_norm_g, m_dn_norm_g, v_dn_norm_g),
               dw_w=(dw_w, m_dw_w, v_dw_w), sconv_w=(sconv_w, m_sconv_w, v_sconv_w))
    upd = _sum_adamw_many([gparts[n] for n in names], [env[n][0] for n in names], [env[n][1] for n in names],
                          [env[n][2] for n in names], "sum_adamw_small")

    d_ada, nm_ada, nv_ada = (u.reshape(w_ada.shape) for u in
                             _adamw(flat2(w_ada), flat2(g_w_ada), flat2(m_w_ada), flat2(v_w_ada), "adamw_w_ada"))

    g_small, d_small, m_small, v_small = (dict(zip(names, u)) for u in upd)
    res += [sum_big(i) for i in range(1, len(big_w))]
    g_big, d_big, m_big, v_big =([r[k].reshape(w.shape) for r, w in zip(res, big_w)] for k in range(4))

    order = ("w_ada", "b_ada", "norm_g", "w_in", "q_norm_g", "k_norm_g", "sinks", "dw_w", "dw_b", "ln_g", "ln_b",
             "pw2_w", "pw2_b", "sconv_w", "a_log", "dt_bias", "dn_norm_g", "w_proj_a", "w_proj_b", "w_proj_c", "w_out")
    big_names = ("w_in", "pw2_w", "w_proj_a", "w_proj_b", "w_proj_c", "w_out")

    def pick(kind):
        src_small = (g_small, d_small, m_small, v_small)[kind]
        src_big = (g_big, d_big, m_big, v_big)[kind]
        src_ada = (g_w_ada, d_ada, nm_ada, nv_ada)[kind]
        return [src_ada if n == "w_ada" else src_big[big_names.index(n)] if n in big_names else src_small[n]
                for n in order]

    return (loss, grad_x, *pick(0), *pick(1), *pick(2), *pick(3))
```
